```python
import math
import jax, jax.numpy as jnp
from jax import lax
import numpy as np

D_MODEL = 1024
BATCH = 8
SEQ = 4096
DEPTH = 2

GRID_W = 64
CTX_LEN = 256
BRANCH_W = 256
HEAD_DIM = 64
N_BRANCHES = 4
N_MOD = 6
RWKV_HEADS = BRANCH_W // HEAD_DIM
RWKV_DECAY_LORA = 64
RWKV_A_LORA = 64
RWKV_G_LORA = 128
RWKV_COLS = 3 * BRANCH_W + RWKV_DECAY_LORA + RWKV_A_LORA + RWKV_G_LORA
RWKV_SPLITS = (BRANCH_W, 2 * BRANCH_W, 3 * BRANCH_W,
               3 * BRANCH_W + RWKV_DECAY_LORA, 3 * BRANCH_W + RWKV_DECAY_LORA + RWKV_A_LORA)
RWKV_GN_EPS = 64e-5
HYENA_COLS = 3 * BRANCH_W
HYENA_EMB = 33
HYENA_FILTER_HIDDEN = 64
HYENA_FAST_DECAY = 0.3
HYENA_SLOW_DECAY = 1.5
HYENA_TARGET = 1e-2
SCONV_COLS = 3 * BRANCH_W
ATTN_Q_HEADS = BRANCH_W // HEAD_DIM
ATTN_KV_HEADS = 2
Q_W = ATTN_Q_HEADS * HEAD_DIM
KV_W = ATTN_KV_HEADS * HEAD_DIM
ATTN_COLS = Q_W + 2 * KV_W
Q_BLOCK = 128
ROPE_THETA = 10000.0
RMS_EPS = 1e-6
GATE_COLS = N_BRANCHES * D_MODEL
OFF_HYENA = RWKV_COLS
OFF_SCONV = OFF_HYENA + HYENA_COLS
OFF_ATTN = OFF_SCONV + SCONV_COLS
OFF_GATE = OFF_ATTN + ATTN_COLS
IN_COLS = OFF_GATE + GATE_COLS
N_EXPERTS = 16
N_GROUPS = 4
EXPERTS_PER_GROUP = N_EXPERTS // N_GROUPS
TOP_K = 2
D_EXPERT = 512
ALPHA = (2 * DEPTH) ** 0.25
BETA = (8 * DEPTH) ** -0.25
LN_EPS = 1e-6

kernel_name = "hybrid_rwkv_hyena_conv_gqa_moe_dit"

F32 = jnp.float32


def ln_plain(x):
    xf = x.astype(F32)
    mu = xf.mean(-1, keepdims=True)
    var = jnp.mean(jnp.square(xf - mu), -1, keepdims=True)
    return ((xf - mu) * lax.rsqrt(var + LN_EPS)).astype(x.dtype)


def layer_norm(x, g, b):
    return ln_plain(x) * g + b


def modulate(x, shift, scale):
    return ln_plain(x) * (1.0 + scale) + shift


def rms_norm(x, g):
    xf = x.astype(F32)
    return (xf * lax.rsqrt(jnp.mean(xf * xf, -1, keepdims=True) + RMS_EPS) * g).astype(x.dtype)


def conv3(x, w):
    xp = jnp.pad(x, ((0, 0), (1, 1), (0, 0)))
    return w[0] * xp[:, :-2] + w[1] * xp[:, 1:-1] + w[2] * xp[:, 2:]


def centred_shift_mix(p, mu):
    xp = jnp.pad(p, ((0, 0), (1, 1), (0, 0)))
    return p + mu * (0.5 * (xp[:, :-2] + xp[:, 2:]) - p)


def to_heads(t):
    return t.reshape(t.shape[:-1] + (RWKV_HEADS, HEAD_DIM))


def rwkv_inputs(p, mu, w0, w_up, a0, a_up, k_k, k_a):
    p = centred_shift_mix(p, mu).astype(F32)
    r, k, v, wd, ad, gd = jnp.split(p, RWKV_SPLITS, axis=-1)
    wlog = w0[:, None, None, :] + jnp.einsum('btr,drc->dbtc', jnp.tanh(wd), w_up)
    decay = jnp.exp(-jnp.exp(-jax.nn.softplus(-wlog) - 0.5))
    a = jax.nn.sigmoid(a0[:, None, None, :] + jnp.einsum('btr,drc->dbtc', ad, a_up))
    kk = to_heads(k * k_k)
    kk = kk * lax.rsqrt(jnp.maximum(jnp.sum(kk * kk, -1, keepdims=True), 1e-24))
    k_dir = k[None] * (1.0 + (a - 1.0) * k_a)
    return to_heads(r), to_heads(k_dir), to_heads(v), kk, to_heads(a), to_heads(decay), gd


def wkv_scan(S0, r, decay, k, v, kk, a, reverse, emit):
    def step(S, inp):
        r_t, w_t, k_t, v_t, kk_t, a_t = inp
        S = (S * w_t[:, :, None, :]
             - jnp.einsum('bhvk,bhk->bhv', S, kk_t)[..., None] * (kk_t * a_t)[:, :, None, :]
             + v_t[..., :, None] * k_t[..., None, :])
        return S, (jnp.einsum('bhvk,bhk->bhv', S, r_t) if emit else None)
    xs = tuple(jnp.moveaxis(t, 1, 0) for t in (r, decay, k, v, kk, a))
    S, ys = lax.scan(step, S0, xs, reverse=reverse)
    return S, (jnp.moveaxis(ys, 0, 1) if emit else None)


def direction_scan(S0, ins, d, reverse, emit):
    r, k_dir, v, kk, a, decay, _ = ins
    return wkv_scan(S0, r, decay[d], k_dir[d], v, kk, a[d], reverse, emit)


def rwkv_output(y, ins, g_up, r_k, lnx_g, lnx_b):
    r, k_dir, v, _, _, _, gd = ins
    B, T = y.shape[:2]
    mu = y.mean(-1, keepdims=True)
    var = jnp.mean(jnp.square(y - mu), -1, keepdims=True)
    yn = ((y - mu) * lax.rsqrt(var + RWKV_GN_EPS)).reshape(B, T, BRANCH_W) * lnx_g + lnx_b
    bonus = jnp.sum(r[None] * k_dir * to_heads(r_k), axis=-1, keepdims=True).sum(0) * v
    g = jax.nn.sigmoid(gd) @ g_up
    return (yn + bonus.reshape(B, T, BRANCH_W)) * g


def rwkv_mixer(p_ctx, p_lat, mu, w0, w_up, a0, a_up, g_up, k_k, k_a, r_k, lnx_g, lnx_b, ctx_out):
    ins_c = rwkv_inputs(p_ctx, mu, w0, w_up, a0, a_up, k_k, k_a)
    ins_l = rwkv_inputs(p_lat, mu, w0, w_up, a0, a_up, k_k, k_a)
    S0 = jnp.zeros((p_lat.shape[0], RWKV_HEADS, HEAD_DIM, HEAD_DIM), F32)
    y_lat, y_ctx = [], []
    for d, rev in enumerate((False, True)):
        S_ctx, yc = direction_scan(S0, ins_c, d, rev, ctx_out)
        _, yl = direction_scan(S_ctx, ins_l, d, rev, True)
        y_lat.append(yl)
        y_ctx.append(yc)
    out_l = rwkv_output(y_lat[0] + y_lat[1], ins_l, g_up, r_k, lnx_g, lnx_b).astype(p_lat.dtype)
    out_c = (rwkv_output(y_ctx[0] + y_ctx[1], ins_c, g_up, r_k, lnx_g, lnx_b).astype(p_ctx.dtype)
             if ctx_out else None)
    return out_c, out_l


def hyena_filter(L, w1, b1, f1, w2, b2, f2, w3):
    bands = (HYENA_EMB - 1) // 2
    t = jnp.linspace(0.0, 1.0, L, dtype=F32)[:, None]
    wpos = 2.0 * math.pi * jnp.arange(L, dtype=F32)[:, None] / L
    f = jnp.linspace(1e-4, bands - 1, bands, dtype=F32)[None, :]
    z = jnp.concatenate([t, jnp.cos(f * wpos), -jnp.sin(f * wpos)], axis=-1)
    hdn = jnp.sin(f1 * (z @ w1 + b1))
    hdn = jnp.sin(f2 * (hdn @ w2 + b2))
    filt = (hdn @ w3).astype(F32)
    C = filt.shape[-1] // 2
    max_decay = math.log(HYENA_TARGET) / HYENA_FAST_DECAY
    min_decay = math.log(HYENA_TARGET) / HYENA_SLOW_DECAY
    deltas = jnp.abs(jnp.linspace(min_decay, max_decay, C, dtype=F32))
    window = jnp.exp(-t * deltas[None, :])
    return filt[:, :C] * window, filt[:, C:] * window


def bidir_fft_conv(u, h_fwd, h_bwd):
    B, L, C = u.shape
    k = jnp.concatenate([h_fwd, jnp.zeros((1, C), F32), h_bwd[:0:-1]], axis=0)
    uf = jnp.fft.rfft(u.astype(F32), n=2 * L, axis=1)
    kf = jnp.fft.rfft(k, n=2 * L, axis=0)
    return jnp.fft.irfft(uf * kf[None], n=2 * L, axis=1)[:, :L]


def hyena_mixer(p, conv_w, skip, w1, b1, f1, w2, b2, f2, w3):
    p = conv3(p, conv_w)
    x0, x1, v = jnp.split(p, 3, axis=-1)
    u = x1 * v
    h_fwd, h_bwd = hyena_filter(p.shape[1], w1, b1, f1, w2, b2, f2, w3)
    y = bidir_fft_conv(u, h_fwd, h_bwd) + u.astype(F32) * skip
    return (x0 * y).astype(p.dtype)


def short_conv_mixer(p, w):
    b_gate, c_gate, xin = jnp.split(p, 3, axis=-1)
    return b_gate * conv3(c_gate * xin, w)


def rope_2d(x, rows, cols):
    half = x.shape[-1] // 2
    inv = ROPE_THETA ** (-jnp.arange(0, half, 2, dtype=F32) / half)
    ang = jnp.concatenate([rows[:, None].astype(F32) * inv, cols[:, None].astype(F32) * inv], -1)
    cos = jnp.cos(ang)[None, :, None, :]
    sin = jnp.sin(ang)[None, :, None, :]
    xp = x.astype(F32).reshape(x.shape[:-1] + (half, 2))
    x1, x2 = xp[..., 0], xp[..., 1]
    out = jnp.stack([x1 * cos - x2 * sin, x1 * sin + x2 * cos], -1).reshape(x.shape)
    return out.astype(x.dtype)


def attn_q(p_q, q_norm):
    B, T, _ = p_q.shape
    return rms_norm(p_q.reshape(B, T, ATTN_Q_HEADS, HEAD_DIM), q_norm)


def attn_kv(p_kv, k_norm):
    B, T, _ = p_kv.shape
    k = rms_norm(p_kv[..., :KV_W].reshape(B, T, ATTN_KV_HEADS, HEAD_DIM), k_norm)
    v = p_kv[..., KV_W:].reshape(B, T, ATTN_KV_HEADS, HEAD_DIM)
    return k, v


def blocked_attention(q, k, v):
    B, T, H, Dh = q.shape
    G = k.shape[2]
    R = H // G
    nb = T // Q_BLOCK
    qb = q.reshape(B, nb, Q_BLOCK, G, R, Dh).transpose(1, 0, 2, 3, 4, 5)
    scale = Dh ** -0.5

    def one_block(qblk):
        s = jnp.einsum('bqgrd,bkgd->bgrqk', qblk, k).astype(F32) * scale
        pr = jax.nn.softmax(s, axis=-1).astype(v.dtype)
        return jnp.einsum('bgrqk,bkgd->bqgrd', pr, v)

    o = lax.map(one_block, qb)
    return o.transpose(1, 0, 2, 3, 4, 5).reshape(B, T, H * Dh)


def attention_mixer(p_ctx_kv, p_ctx_q, p_lat, q_norm, k_norm, rows, cols, ctx_out):
    kc, vc = attn_kv(p_ctx_kv, k_norm)
    ql = rope_2d(attn_q(p_lat[..., :Q_W], q_norm), rows, cols)
    kl, vl = attn_kv(p_lat[..., Q_W:], k_norm)
    kl = rope_2d(kl, rows, cols)
    y_lat = blocked_attention(ql, jnp.concatenate([kc, kl], 1), jnp.concatenate([vc, vl], 1))
    y_ctx = blocked_attention(attn_q(p_ctx_q, q_norm), kc, vc) if ctx_out else None
    return y_ctx, y_lat


def merge_branches(ys, gate_proj, w_branch, w_out):
    B, T, _ = gate_proj.shape
    gates = jax.nn.sigmoid(gate_proj.astype(F32)).reshape(B, T, N_BRANCHES, D_MODEL).astype(gate_proj.dtype)
    merged = gates[:, :, 0] * (ys[0] @ w_branch[0])
    for n in range(1, N_BRANCHES):
        merged = merged + gates[:, :, n] * (ys[n] @ w_branch[n])
    return merged @ w_out


def moe(h, router_w, router_bias, w1, w3, w2):
    B, T, D = h.shape
    hf = h.reshape(-1, D)
    N = hf.shape[0]
    s = jax.nn.sigmoid((hf @ router_w).astype(F32))
    sel = s + router_bias.astype(F32)
    grp_score = lax.top_k(sel.reshape(N, N_GROUPS, EXPERTS_PER_GROUP), 2)[0].sum(-1)
    g_idx = jnp.argmax(grp_score, axis=-1)
    in_group = jnp.repeat(jax.nn.one_hot(g_idx, N_GROUPS, dtype=F32) > 0, EXPERTS_PER_GROUP, axis=-1)
    _, idx = lax.top_k(jnp.where(in_group, sel, -jnp.inf), TOP_K)
    wts = jnp.take_along_axis(s, idx, axis=-1)
    wts = wts / jnp.sum(wts, -1, keepdims=True)
    gates = jnp.einsum('nk,nke->ne', wts, jax.nn.one_hot(idx, N_EXPERTS, dtype=F32)).astype(h.dtype)
    out = jnp.zeros_like(hf)
    for e in range(N_EXPERTS):
        act = jax.nn.silu(hf @ w1[e]) * (hf @ w3[e])
        out = out + gates[:, e:e + 1] * (act @ w2[e])
    return out.reshape(B, T, D)


def setup_inputs(seed: int = 0) -> dict:
    key = jax.random.key(seed)
    ks = iter(jax.random.split(key, 64))

    def nrm(shape, s):
        return jax.random.normal(next(ks), shape, F32) * s

    def uni(shape, lo, hi):
        return jax.random.uniform(next(ks), shape, F32, lo, hi)

    C = BRANCH_W
    Hd = HYENA_FILTER_HIDDEN
    return {
        "x": nrm((BATCH, SEQ, D_MODEL), 1.0),
        "c": nrm((BATCH, D_MODEL), 1.0),
        "ctx": nrm((BATCH, CTX_LEN, D_MODEL), 1.0),
        "c_ctx": nrm((D_MODEL,), 1.0),
        "ada_w": nrm((DEPTH, D_MODEL, N_MOD * D_MODEL), 0.5 * D_MODEL ** -0.5),
        "ada_b": nrm((DEPTH, N_MOD * D_MODEL), 0.02),
        "w_in": nrm((DEPTH, D_MODEL, IN_COLS), D_MODEL ** -0.5),
        "rwkv_mu": uni((DEPTH, RWKV_COLS), 0.0, 1.0),
        "rwkv_w0": uni((DEPTH, 2, C), -6.0, -1.0),
        "rwkv_w_up": nrm((DEPTH, 2, RWKV_DECAY_LORA, C), RWKV_DECAY_LORA ** -0.5),
        "rwkv_a0": nrm((DEPTH, 2, C), 0.1),
        "rwkv_a_up": nrm((DEPTH, 2, RWKV_A_LORA, C), 0.5 * RWKV_A_LORA ** -0.5),
        "rwkv_g_up": nrm((DEPTH, RWKV_G_LORA, C), RWKV_G_LORA ** -0.5),
        "rwkv_k_k": 0.85 + nrm((DEPTH, C), 0.05),
        "rwkv_k_a": 1.0 + nrm((DEPTH, C), 0.05),
        "rwkv_r_k": nrm((DEPTH, C), 0.1),
        "rwkv_lnx_g": 1.0 + nrm((DEPTH, C), 0.05),
        "rwkv_lnx_b": nrm((DEPTH, C), 0.02),
        "hyena_conv": nrm((DEPTH, 3, HYENA_COLS), 3 ** -0.5),
        "hyena_w1": nrm((DEPTH, HYENA_EMB, Hd), HYENA_EMB ** -0.5),
        "hyena_b1": nrm((DEPTH, Hd), 0.1),
        "hyena_freq1": 1.0 + nrm((DEPTH, Hd), 0.05),
        "hyena_w2": nrm((DEPTH, Hd, Hd), Hd ** -0.5),
        "hyena_b2": nrm((DEPTH, Hd), 0.1),
        "hyena_freq2": 1.0 + nrm((DEPTH, Hd), 0.05),
        "hyena_w3": nrm((DEPTH, Hd, 2 * C), 0.1 * Hd ** -0.5),
        "hyena_skip": nrm((DEPTH, C), 0.5),
        "sconv_w": nrm((DEPTH, 3, C), 3 ** -0.5),
        "attn_q_norm": 1.0 + nrm((DEPTH, HEAD_DIM), 0.05),
        "attn_k_norm": 1.0 + nrm((DEPTH, HEAD_DIM), 0.05),
        "w_branch": nrm((DEPTH, N_BRANCHES, C, D_MODEL), C ** -0.5),
        "w_out": nrm((DEPTH, D_MODEL, D_MODEL), BETA * D_MODEL ** -0.5),
        "ln1_g": 1.0 + nrm((DEPTH, D_MODEL), 0.05),
        "ln1_b": nrm((DEPTH, D_MODEL), 0.02),
        "ln2_g": 1.0 + nrm((DEPTH, D_MODEL), 0.05),
        "ln2_b": nrm((DEPTH, D_MODEL), 0.02),
        "router_w": nrm((D_MODEL, N_EXPERTS), D_MODEL ** -0.5),
        "router_bias": nrm((N_EXPERTS,), 0.01),
        "exp_w1": nrm((DEPTH, N_EXPERTS, D_MODEL, D_EXPERT), D_MODEL ** -0.5),
        "exp_w3": nrm((DEPTH, N_EXPERTS, D_MODEL, D_EXPERT), D_MODEL ** -0.5),
        "exp_w2": nrm((DEPTH, N_EXPERTS, D_EXPERT, D_MODEL), BETA * D_EXPERT ** -0.5),
    }


def reference(x, c, ctx, c_ctx, ada_w, ada_b, w_in, rwkv_mu, rwkv_w0, rwkv_w_up, rwkv_a0, rwkv_a_up,
              rwkv_g_up, rwkv_k_k, rwkv_k_a, rwkv_r_k, rwkv_lnx_g, rwkv_lnx_b, hyena_conv, hyena_w1,
              hyena_b1, hyena_freq1, hyena_w2, hyena_b2, hyena_freq2, hyena_w3, hyena_skip, sconv_w,
              attn_q_norm, attn_k_norm, w_branch, w_out, ln1_g, ln1_b, ln2_g, ln2_b, router_w,
              router_bias, exp_w1, exp_w3, exp_w2):
    L = x.shape[1]
    Lc = ctx.shape[1]
    n_rows = L // GRID_W
    rows = jnp.repeat(jnp.arange(n_rows), GRID_W)
    cols = jnp.tile(jnp.arange(GRID_W), n_rows)
    splits = (OFF_HYENA, OFF_SCONV, OFF_ATTN, OFF_GATE)
    for l in range(DEPTH):
        ctx_out = l < DEPTH - 1
        mod = jax.nn.silu(c) @ ada_w[l] + ada_b[l]
        mod_c = jax.nn.silu(c_ctx) @ ada_w[l] + ada_b[l]
        sh_a, sc_a, g_a, sh_f, sc_f, g_f = jnp.split(mod[:, None, :], N_MOD, axis=-1)
        sh_ac, sc_ac, g_ac, sh_fc, sc_fc, g_fc = jnp.split(mod_c[None, None, :], N_MOD, axis=-1)
        h = modulate(x, sh_a, sc_a)
        hc = modulate(ctx, sh_ac, sc_ac)
        wl = w_in[l]
        pa, pb, pcv, pd, pg = jnp.split(h @ wl, splits, axis=-1)
        if ctx_out:
            pa_c, pb_c, pcv_c, pd_c, pg_c = jnp.split(hc @ wl, splits, axis=-1)
            pd_c_q, pd_c_kv = pd_c[..., :Q_W], pd_c[..., Q_W:]
        else:
            pa_c = hc @ wl[:, :OFF_HYENA]
            pd_c_kv = hc @ wl[:, OFF_ATTN + Q_W:OFF_GATE]
            pd_c_q = None
        ya_c, ya = rwkv_mixer(pa_c, pa, rwkv_mu[l], rwkv_w0[l], rwkv_w_up[l], rwkv_a0[l], rwkv_a_up[l],
                              rwkv_g_up[l], rwkv_k_k[l], rwkv_k_a[l], rwkv_r_k[l], rwkv_lnx_g[l],
                              rwkv_lnx_b[l], ctx_out)
        yd_c, yd = attention_mixer(pd_c_kv, pd_c_q, pd, attn_q_norm[l], attn_k_norm[l], rows, cols, ctx_out)
        filt = (hyena_w1[l], hyena_b1[l], hyena_freq1[l], hyena_w2[l], hyena_b2[l], hyena_freq2[l], hyena_w3[l])
        yb = hyena_mixer(pb, hyena_conv[l], hyena_skip[l], *filt)
        ycv = short_conv_mixer(pcv, sconv_w[l])
        y = merge_branches((ya, yb, ycv, yd), pg, w_branch[l], w_out[l])
        x = layer_norm(ALPHA * x + g_a * y, ln1_g[l], ln1_b[l])
        hf = modulate(x, sh_f, sc_f)
        if ctx_out:
            yb_c = hyena_mixer(pb_c, hyena_conv[l], hyena_skip[l], *filt)
            ycv_c = short_conv_mixer(pcv_c, sconv_w[l])
            y_c = merge_branches((ya_c, yb_c, ycv_c, yd_c), pg_c, w_branch[l], w_out[l])
            ctx = layer_norm(ALPHA * ctx + g_ac * y_c, ln1_g[l], ln1_b[l])
            hfc = modulate(ctx, sh_fc, sc_fc)
            f_all = moe(jnp.concatenate([hfc, hf], axis=1), router_w, router_bias, exp_w1[l], exp_w3[l], exp_w2[l])
            ctx = layer_norm(ALPHA * ctx + g_fc * f_all[:, :Lc], ln2_g[l], ln2_b[l])
            x = layer_norm(ALPHA * x + g_f * f_all[:, Lc:], ln2_g[l], ln2_b[l])
        else:
            f_lat = moe(hf, router_w, router_bias, exp_w1[l], exp_w3[l], exp_w2[l])
            x = layer_norm(ALPHA * x + g_f * f_lat, ln2_g[l], ln2_b[l])
    return x
```

```python
import functools
import math

import numpy as np
import jax
import jax.numpy as jnp
from jax import lax
from jax.experimental import pallas as pl
from jax.experimental.pallas import tpu as pltpu

F32 = jnp.float32
BF16 = jnp.bfloat16
HI = lax.Precision.HIGHEST

D_MODEL = 1024
GRID_W = 64
BRANCH_W = 256
HEAD_DIM = 64
N_BRANCHES = 4
N_MOD = 6
RWKV_HEADS = 4
RWKV_COLS = 1024
RWKV_GN_EPS = 64e-5
HYENA_COLS = 768
HYENA_EMB = 33
HYENA_FAST_DECAY = 0.3
HYENA_SLOW_DECAY = 1.5
HYENA_TARGET = 1e-2
SCONV_COLS = 768
Q_W = 256
KV_W = 128
ATTN_COLS = 512
ROPE_THETA = 10000.0
RMS_EPS = 1e-6
OFF_HYENA = RWKV_COLS
OFF_SCONV = OFF_HYENA + HYENA_COLS
OFF_ATTN = OFF_SCONV + SCONV_COLS
OFF_GATE = OFF_ATTN + ATTN_COLS
N_EXPERTS = 16
N_GROUPS = 4
EXPERTS_PER_GROUP = 4
D_EXPERT = 512
LN_EPS = 1e-6

SUBLANES = 8
LANES = 128
TT = 256
SCAN_BLK = 128
SCAN_GRP = 8
HY_BLK = 256
VMEM_LIMIT = 56 * 1024 * 1024


def _cparams(sem, vmem=None):
    return pltpu.CompilerParams(dimension_semantics=sem, vmem_limit_bytes=vmem)


def _ln(xf):
    mu = jnp.mean(xf, -1, keepdims=True)
    xc = xf - mu
    var = jnp.mean(xc * xc, -1, keepdims=True)
    return xc * lax.rsqrt(var + LN_EPS)


def _pick_tile(n, cands):
    for c in cands:
        if n % c == 0:
            return c
    raise ValueError(f"no tile for {n}")


def _ada_kernel(c_ref, w_ref, b_ref, o_ref):
    c = c_ref[...]
    a = c * jax.nn.sigmoid(c)
    o_ref[0] = jnp.dot(a, w_ref[0], precision=HI, preferred_element_type=F32) + b_ref[0]


def ada_mod(cc, ada_w, ada_b):
    depth, d, n = ada_w.shape
    rows = cc.shape[0]
    return pl.pallas_call(
        _ada_kernel,
        out_shape=jax.ShapeDtypeStruct((depth, rows, n), F32),
        grid=(depth, n // d),
        in_specs=[pl.BlockSpec((rows, d), lambda l, j: (0, 0)),
                  pl.BlockSpec((1, d, d), lambda l, j: (l, 0, j)),
                  pl.BlockSpec((1, 1, d), lambda l, j: (l, 0, j))],
        out_specs=pl.BlockSpec((1, rows, d), lambda l, j: (l, 0, j)),
        compiler_params=_cparams(("parallel", "parallel"), VMEM_LIMIT),
        name="ada_mod",
    )(cc, ada_w, ada_b.reshape(depth, 1, n))


def _lnmod_kernel(x_ref, sh_ref, sc_ref, o_ref):
    h = _ln(x_ref[0]) * (1.0 + sc_ref[0]) + sh_ref[0]
    o_ref[0] = h.astype(BF16)


def _mod_spec(col, nct, nb):
    return pl.BlockSpec((1, 1, D_MODEL), lambda b, t: (jnp.where(t < nct, nb, b), 0, col))


def lnmod(x_all, mod3, nct, col_shift, col_scale):
    nb, ta, d = x_all.shape
    return pl.pallas_call(
        _lnmod_kernel,
        out_shape=jax.ShapeDtypeStruct((nb, ta, d), BF16),
        grid=(nb, ta // TT),
        in_specs=[pl.BlockSpec((1, TT, d), lambda b, t: (b, t, 0)),
                  _mod_spec(col_shift, nct, nb), _mod_spec(col_scale, nct, nb)],
        out_specs=pl.BlockSpec((1, TT, d), lambda b, t: (b, t, 0)),
        compiler_params=_cparams(("parallel", "parallel")),
        name="lnmod",
    )(x_all, mod3, mod3)


def _mm_kernel(a_ref, b_ref, o_ref):
    o_ref[...] = jnp.dot(a_ref[...], b_ref[...], preferred_element_type=F32).astype(o_ref.dtype)


def matmul(a, b, out_dtype=F32):
    m, k = a.shape
    _, n = b.shape
    tm = _pick_tile(m, (1024, 512, 256))
    tn = _pick_tile(n, (1024, 512, 256))
    return pl.pallas_call(
        _mm_kernel,
        out_shape=jax.ShapeDtypeStruct((m, n), out_dtype),
        grid=(m // tm, n // tn),
        in_specs=[pl.BlockSpec((tm, k), lambda i, j: (i, 0)),
                  pl.BlockSpec((k, tn), lambda i, j: (0, j))],
        out_specs=pl.BlockSpec((tm, tn), lambda i, j: (i, j)),
        compiler_params=_cparams(("parallel", "parallel"), VMEM_LIMIT),
        name="matmul",
    )(a, b)


def _halo_specs(width, ta):
    nblk8 = ta // SUBLANES
    per = TT // SUBLANES
    cur = pl.BlockSpec((1, TT, width), lambda b, t: (b, t, 0))
    prev = pl.BlockSpec((1, SUBLANES, width), lambda b, t: (b, jnp.maximum(t * per - 1, 0), 0))
    nxt = pl.BlockSpec((1, SUBLANES, width), lambda b, t: (b, jnp.minimum((t + 1) * per, nblk8 - 1), 0))
    return [cur, prev, nxt]


def _neighbours(cur, prev_ref, next_ref, nct, nt):
    t = pl.program_id(1)
    seg_start = jnp.logical_or(t == 0, t == nct)
    seg_end = jnp.logical_or(t == nct - 1, t == nt - 1)
    prev_row = prev_ref[0][SUBLANES - 1:SUBLANES, :] * jnp.where(seg_start, 0.0, 1.0)
    next_row = next_ref[0][0:1, :] * jnp.where(seg_end, 0.0, 1.0)
    row = lax.broadcasted_iota(jnp.int32, (TT, 1), 0)
    xm1 = jnp.where(row == 0, prev_row, pltpu.roll(cur, 1, axis=0))
    xp1 = jnp.where(row == TT - 1, next_row, pltpu.roll(cur, TT - 1, axis=0))
    return xm1, xp1


def _rwkv_prep_kernel(nct, nt, cur_ref, prev_ref, next_ref, mu_ref, w0_ref, wup_ref, a0_ref, aup_ref,
                      gup_ref, kk_ref, ka_ref, rk_ref, bd_ref,
                      r_o, kk_o, w0_o, w1_o, k0_o, k1_o, b0_o, b1_o, v_o, g_o, bon_o):
    cur = cur_ref[0]
    xm1, xp1 = _neighbours(cur, prev_ref, next_ref, nct, nt)
    p = cur + mu_ref[...] * (0.5 * (xm1 + xp1) - cur)
    c = BRANCH_W
    r, k, v = p[:, 0:c], p[:, c:2 * c], p[:, 2 * c:3 * c]
    wd = p[:, 3 * c:3 * c + 64]
    ad = p[:, 3 * c + 64:3 * c + 128]
    gd = p[:, 3 * c + 128:3 * c + 256]
    bd = bd_ref[...]
    kk = k * kk_ref[...]
    ss = jnp.dot(kk * kk, bd, precision=HI, preferred_element_type=F32)
    kkn = kk * lax.rsqrt(jnp.maximum(ss, 1e-24))
    twd = jnp.tanh(wd)
    ka = ka_ref[...]
    kdirs = []
    w_outs, k_outs, b_outs = (w0_o, w1_o), (k0_o, k1_o), (b0_o, b1_o)
    for d in range(2):
        wlog = w0_ref[d:d + 1, :] + jnp.dot(twd, wup_ref[d], precision=HI, preferred_element_type=F32)
        decay = jnp.exp(-math.exp(-0.5) * jax.nn.sigmoid(wlog))
        a = jax.nn.sigmoid(a0_ref[d:d + 1, :] + jnp.dot(ad, aup_ref[d], precision=HI, preferred_element_type=F32))
        kdir = k * (1.0 + (a - 1.0) * ka)
        bdir = kkn * a
        kdirs.append(kdir)
        for h in range(RWKV_HEADS):
            sl = slice(h * HEAD_DIM, (h + 1) * HEAD_DIM)
            w_outs[d][0, h] = decay[:, sl]
            k_outs[d][0, h] = kdir[:, sl]
            b_outs[d][0, h] = bdir[:, sl]
    for h in range(RWKV_HEADS):
        sl = slice(h * HEAD_DIM, (h + 1) * HEAD_DIM)
        r_o[0, h] = r[:, sl]
        kk_o[0, h] = kkn[:, sl]
    v_o[0] = v
    g_o[0] = jnp.dot(jax.nn.sigmoid(gd), gup_ref[...], precision=HI, preferred_element_type=F32)
    rkk = r * rk_ref[...] * (kdirs[0] + kdirs[1])
    bon_o[0] = jnp.dot(rkk, bd, precision=HI, preferred_element_type=F32) * v


def rwkv_prep(p_rwkv, nct, mu, w0, w_up, a0, a_up, g_up, k_k, k_a, r_k, bd):
    nb, ta, _ = p_rwkv.shape
    nt = ta // TT
    c = BRANCH_W
    hm = jax.ShapeDtypeStruct((nb, RWKV_HEADS, ta, HEAD_DIM), F32)
    nat = jax.ShapeDtypeStruct((nb, ta, c), F32)
    hm_spec = pl.BlockSpec((1, RWKV_HEADS, TT, HEAD_DIM), lambda b, t: (b, 0, t, 0))
    nat_spec = pl.BlockSpec((1, TT, c), lambda b, t: (b, t, 0))

    def full(a):
        nd = a.ndim
        return pl.BlockSpec(a.shape, lambda b, t: (0,) * nd)

    consts = [mu.reshape(1, -1), w0, w_up, a0, a_up, g_up, k_k.reshape(1, -1), k_a.reshape(1, -1),
              r_k.reshape(1, -1), bd]
    return pl.pallas_call(
        functools.partial(_rwkv_prep_kernel, nct, nt),
        out_shape=[hm] * 8 + [nat] * 3,
        grid=(nb, nt),
        in_specs=_halo_specs(RWKV_COLS, ta) + [full(a) for a in consts],
        out_specs=[hm_spec] * 8 + [nat_spec] * 3,
        compiler_params=_cparams(("parallel", "parallel"), VMEM_LIMIT),
        name="rwkv_prep",
    )(p_rwkv, p_rwkv, p_rwkv, *consts)


def _scan_kernel(rf, kkf, wf, kf, bf, vf, rb, kkb, wb, kb, bb, vb, yf_o, yb_o, s_scr, vt_scr):
    step = pl.program_id(1)
    nh = RWKV_HEADS

    @pl.when(step == 0)
    def _():
        s_scr[...] = jnp.zeros_like(s_scr)

    for d, vref in ((0, vf), (1, vb)):
        vblk = vref[0]
        vt_scr[d, 0] = vblk[:, 0:LANES].T
        vt_scr[d, 1] = vblk[:, LANES:2 * LANES].T

    rows = ((rf, kkf, wf, kf, bf), (rb, kkb, wb, kb, bb))
    y_out = (yf_o, yb_o)
    ngrp = SCAN_BLK // SCAN_GRP

    def group(g, states):
        states = list(states)
        base = (pl.multiple_of(g * SCAN_GRP, SCAN_GRP),
                pl.multiple_of(SCAN_BLK - SCAN_GRP - g * SCAN_GRP, SCAN_GRP))
        vts, blk = [], []
        for d in range(2):
            shift = lax.rem(SCAN_BLK - base[d], SCAN_BLK)
            vts.append([pltpu.roll(vt_scr[d, p], shift, axis=1) for p in range(2)])
            blk.append([[ref[0, h, pl.ds(base[d], SCAN_GRP), :] for ref in rows[d]] for h in range(nh)])
        for j in range(SCAN_GRP):
            for d in range(2):
                jj = j if d == 0 else SCAN_GRP - 1 - j
                for h in range(nh):
                    r_r, kk_r, w_r, k_r, b_r = (a[jj:jj + 1, :] for a in blk[d][h])
                    hp = (h % 2) * HEAD_DIM
                    v_c = vts[d][h // 2][hp:hp + HEAD_DIM, jj:jj + 1]
                    s = states[d * nh + h]
                    skk = jnp.sum(s * kk_r, axis=1, keepdims=True)
                    s = s * w_r - skk * b_r + v_c * k_r
                    states[d * nh + h] = s
                    y = lax.dot_general(jnp.broadcast_to(r_r, (SUBLANES, HEAD_DIM)), s,
                                        (((1,), (1,)), ((), ())), preferred_element_type=F32)
                    y_out[d][0, h, pl.ds(base[d] + jj, 1), :] = y[0:1, :]
        return tuple(states)

    init = tuple(s_scr[i] for i in range(2 * nh))
    final = lax.fori_loop(0, ngrp, group, init)
    for i in range(2 * nh):
        s_scr[i] = final[i]


def rwkv_scan(r, kk, v, w0, w1, k0, k1, b0, b1, lc):
    nb, nh, ta, hd = r.shape
    nblk = ta // SCAN_BLK
    nctb = lc // SCAN_BLK

    def fwd(s):
        return s

    def bwd(s):
        return jnp.where(s < nctb, nctb - 1 - s, nblk - 1 - (s - nctb))

    def hm(idx):
        return pl.BlockSpec((1, nh, SCAN_BLK, hd), lambda b, s: (b, 0, idx(s), 0))

    def nat(idx):
        return pl.BlockSpec((1, SCAN_BLK, BRANCH_W), lambda b, s: (b, idx(s), 0))

    out = jax.ShapeDtypeStruct((nb, nh, ta, hd), F32)
    return pl.pallas_call(
        _scan_kernel,
        out_shape=[out, out],
        grid=(nb, nblk),
        in_specs=[hm(fwd)] * 5 + [nat(fwd)] + [hm(bwd)] * 5 + [nat(bwd)],
        out_specs=[hm(fwd), hm(bwd)],
        scratch_shapes=[pltpu.VMEM((2 * nh, hd, hd), F32), pltpu.VMEM((2, 2, LANES, SCAN_BLK), F32)],
        compiler_params=_cparams(("parallel", "arbitrary")),
        name="rwkv_scan",
    )(r, kk, w0, k0, b0, v, r, kk, w1, k1, b1, v)


def _rwkv_out_kernel(yf_ref, yb_ref, g_ref, bon_ref, lg_ref, lb_ref, o_ref):
    parts = []
    for h in range(RWKV_HEADS):
        y = yf_ref[0, h] + yb_ref[0, h]
        mu = jnp.mean(y, -1, keepdims=True)
        yc = y - mu
        var = jnp.mean(yc * yc, -1, keepdims=True)
        parts.append(yc * lax.rsqrt(var + RWKV_GN_EPS))
    yn = jnp.concatenate(parts, axis=1) * lg_ref[...] + lb_ref[...]
    o_ref[0] = ((yn + bon_ref[0]) * g_ref[0]).astype(BF16)


def rwkv_out(yf, yb, g, bon, lnx_g, lnx_b):
    nb, nh, ta, hd = yf.shape
    hm_spec = pl.BlockSpec((1, nh, TT, hd), lambda b, t: (b, 0, t, 0))
    nat_spec = pl.BlockSpec((1, TT, BRANCH_W), lambda b, t: (b, t, 0))
    row = pl.BlockSpec((1, BRANCH_W), lambda b, t: (0, 0))
    return pl.pallas_call(
        _rwkv_out_kernel,
        out_shape=jax.ShapeDtypeStruct((nb, ta, BRANCH_W), BF16),
        grid=(nb, ta // TT),
        in_specs=[hm_spec, hm_spec, nat_spec, nat_spec, row, row],
        out_specs=nat_spec,
        compiler_params=_cparams(("parallel", "parallel")),
        name="rwkv_out",
    )(yf, yb, g, bon, lnx_g.reshape(1, -1), lnx_b.reshape(1, -1))


def _pair_swap(x):
    lane = lax.broadcasted_iota(jnp.int32, x.shape, 1)
    n = x.shape[1]
    return jnp.where(lane % 2 == 0, pltpu.roll(x, n - 1, axis=1), pltpu.roll(x, 1, axis=1))


def _attn_prep_kernel(p_ref, cos_ref, sin_ref, qg_ref, kg_ref, bd_ref, q_o, k_o, v_o):
    p = p_ref[0]
    q, k, v = p[:, :Q_W], p[:, Q_W:Q_W + KV_W], p[:, Q_W + KV_W:]
    bd = bd_ref[...]
    cos, sin = cos_ref[...], sin_ref[...]
    qms = jnp.dot(q * q, bd, precision=HI, preferred_element_type=F32) * (1.0 / HEAD_DIM)
    qn = q * lax.rsqrt(qms + RMS_EPS) * qg_ref[...]
    qr = qn * cos + _pair_swap(qn) * sin
    q_o[0] = (qr * HEAD_DIM ** -0.5).astype(BF16)
    kms = jnp.dot(k * k, bd[:KV_W, :KV_W], precision=HI, preferred_element_type=F32) * (1.0 / HEAD_DIM)
    kn = k * lax.rsqrt(kms + RMS_EPS) * kg_ref[...]
    kr = kn * cos[:, :KV_W] + _pair_swap(kn) * sin[:, :KV_W]
    for g in range(KV_W // HEAD_DIM):
        sl = slice(g * HEAD_DIM, (g + 1) * HEAD_DIM)
        k_o[0, g] = kr[:, sl].astype(BF16)
        v_o[0, g] = v[:, sl].astype(BF16)


def attn_prep(p_attn, cos_t, sin_t, q_norm, k_norm, bd):
    nb, ta, _ = p_attn.shape
    ng = KV_W // HEAD_DIM
    qg = jnp.tile(q_norm, Q_W // HEAD_DIM).reshape(1, -1)
    kg = jnp.tile(k_norm, ng).reshape(1, -1)
    kv_shape = jax.ShapeDtypeStruct((nb, ng, ta, HEAD_DIM), BF16)
    kv_spec = pl.BlockSpec((1, ng, TT, HEAD_DIM), lambda b, t: (b, 0, t, 0))
    return pl.pallas_call(
        _attn_prep_kernel,
        out_shape=[jax.ShapeDtypeStruct((nb, ta, Q_W), BF16), kv_shape, kv_shape],
        grid=(nb, ta // TT),
        in_specs=[pl.BlockSpec((1, TT, ATTN_COLS), lambda b, t: (b, t, 0)),
                  pl.BlockSpec((TT, Q_W), lambda b, t: (t, 0)),
                  pl.BlockSpec((TT, Q_W), lambda b, t: (t, 0)),
                  pl.BlockSpec((1, Q_W), lambda b, t: (0, 0)),
                  pl.BlockSpec((1, KV_W), lambda b, t: (0, 0)),
                  pl.BlockSpec(bd.shape, lambda b, t: (0, 0))],
        out_specs=[pl.BlockSpec((1, TT, Q_W), lambda b, t: (b, t, 0)), kv_spec, kv_spec],
        compiler_params=_cparams(("parallel", "parallel")),
        name="attn_prep",
    )(p_attn, cos_t, sin_t, qg, kg, bd)


def _attn_kernel(nct, lc, q_ref, k_ref, v_ref, o_ref):
    t = pl.program_id(2)

    def run(kk, vv):
        outs = []
        for r in range(2):
            q = q_ref[0][:, r * HEAD_DIM:(r + 1) * HEAD_DIM]
            s = lax.dot_general(q, kk, (((1,), (1,)), ((), ())), preferred_element_type=F32)
            p = jnp.exp(s - jnp.max(s, -1, keepdims=True))
            l = jnp.sum(p, -1, keepdims=True)
            o = jnp.dot(p.astype(BF16), vv, preferred_element_type=F32)
            outs.append(o / l)
        o_ref[0] = jnp.concatenate(outs, axis=1).astype(BF16)

    @pl.when(t < nct)
    def _():
        run(k_ref[0, 0, :lc, :], v_ref[0, 0, :lc, :])

    @pl.when(t >= nct)
    def _():
        run(k_ref[0, 0], v_ref[0, 0])


def attention(q, k, v, lc):
    nb, ta, _ = q.shape
    ng = k.shape[1]
    nct = lc // TT
    qo_spec = pl.BlockSpec((1, TT, 2 * HEAD_DIM), lambda b, g, t: (b, t, g))
    kv_spec = pl.BlockSpec((1, 1, ta, HEAD_DIM), lambda b, g, t: (b, g, 0, 0))
    return pl.pallas_call(
        functools.partial(_attn_kernel, nct, lc),
        out_shape=jax.ShapeDtypeStruct((nb, ta, Q_W), BF16),
        grid=(nb, ng, ta // TT),
        in_specs=[qo_spec, kv_spec, kv_spec],
        out_specs=qo_spec,
        compiler_params=_cparams(("parallel", "parallel", "arbitrary"), VMEM_LIMIT),
        name="attention",
    )(q, k, v)


def _hs_pre_kernel(nct, nt, cur_ref, prev_ref, next_ref, hw_ref, sw_ref, x0_o, u_o, ycv_o):
    cur = cur_ref[0]
    xm1, xp1 = _neighbours(cur, prev_ref, next_ref, nct, nt)
    c = BRANCH_W
    hc = HYENA_COLS
    hw = hw_ref[...]
    ph = hw[0:1] * xm1[:, :hc] + hw[1:2] * cur[:, :hc] + hw[2:3] * xp1[:, :hc]
    x0_o[0] = ph[:, :c]
    u_o[0] = ph[:, c:2 * c] * ph[:, 2 * c:3 * c]
    sw = sw_ref[...]

    def cx(a):
        return a[:, hc + c:hc + 2 * c] * a[:, hc + 2 * c:hc + 3 * c]

    conv = sw[0:1] * cx(xm1) + sw[1:2] * cx(cur) + sw[2:3] * cx(xp1)
    ycv_o[0] = (cur[:, hc:hc + c] * conv).astype(BF16)


def hs_pre(p_hs, nct, hyena_conv, sconv_w):
    nb, ta, w = p_hs.shape
    nt = ta // TT
    nat = pl.BlockSpec((1, TT, BRANCH_W), lambda b, t: (b, t, 0))
    return pl.pallas_call(
        functools.partial(_hs_pre_kernel, nct, nt),
        out_shape=[jax.ShapeDtypeStruct((nb, ta, BRANCH_W), F32),
                   jax.ShapeDtypeStruct((nb, ta, BRANCH_W), F32),
                   jax.ShapeDtypeStruct((nb, ta, BRANCH_W), BF16)],
        grid=(nb, nt),
        in_specs=_halo_specs(w, ta) + [pl.BlockSpec(hyena_conv.shape, lambda b, t: (0, 0)),
                                       pl.BlockSpec(sconv_w.shape, lambda b, t: (0, 0))],
        out_specs=[nat, nat, nat],
        compiler_params=_cparams(("parallel", "parallel"), VMEM_LIMIT),
        name="hs_pre",
    )(p_hs, p_hs, p_hs, hyena_conv, sconv_w)


EMB_PAD = 40


def _filter_tables(lh):
    n = np.arange(2 * lh)
    pos = np.abs(n - (lh - 1)).astype(np.float64)
    bands = (HYENA_EMB - 1) // 2
    t = np.minimum(pos, lh - 1) / (lh - 1)
    wpos = 2.0 * math.pi * pos / lh
    f = np.linspace(1e-4, bands - 1, bands)[:, None]
    z = np.zeros((EMB_PAD, 2 * lh), np.float32)
    z[0] = t
    z[1:1 + bands] = np.cos(f * wpos[None, :])
    z[1 + bands:1 + 2 * bands] = -np.sin(f * wpos[None, :])
    max_decay = math.log(HYENA_TARGET) / HYENA_FAST_DECAY
    min_decay = math.log(HYENA_TARGET) / HYENA_SLOW_DECAY
    deltas = np.abs(np.linspace(min_decay, max_decay, BRANCH_W)).astype(np.float32)
    return z, deltas.reshape(-1, 1)


def _filter_kernel(lh, tn, z_ref, w1_ref, b1_ref, f1_ref, w2_ref, b2_ref, f2_ref, w3_ref, dl_ref, o_ref):
    z = z_ref[...]
    h1 = jnp.sin(f1_ref[...] * (jnp.dot(w1_ref[...], z, precision=HI, preferred_element_type=F32) + b1_ref[...]))
    h2 = jnp.sin(f2_ref[...] * (jnp.dot(w2_ref[...], h1, precision=HI, preferred_element_type=F32) + b2_ref[...]))
    f = jnp.dot(w3_ref[...], h2, precision=HI, preferred_element_type=F32)
    n = pl.program_id(0) * tn + lax.broadcasted_iota(jnp.int32, (1, tn), 1)
    filt = jnp.where(n >= lh - 1, f[:BRANCH_W], f[BRANCH_W:])
    win = jnp.exp(-z[0:1, :] * dl_ref[...])
    o_ref[...] = jnp.where(n == 2 * lh - 1, 0.0, filt * win)


def hyena_filter_table(lh, w1, b1, f1, w2, b2, f2, w3):
    z_np, dl_np = _filter_tables(lh)
    n2 = 2 * lh
    tn = _pick_tile(n2, (1024, 512))
    hd = w2.shape[0]
    w1t = jnp.zeros((hd, EMB_PAD), F32).at[:, :HYENA_EMB].set(w1.T)
    args = [jnp.asarray(z_np), w1t, b1.reshape(-1, 1), f1.reshape(-1, 1), w2.T, b2.reshape(-1, 1),
            f2.reshape(-1, 1), w3.T, jnp.asarray(dl_np)]

    def full(a):
        return pl.BlockSpec(a.shape, lambda j: (0, 0))

    return pl.pallas_call(
        functools.partial(_filter_kernel, lh, tn),
        out_shape=jax.ShapeDtypeStruct((BRANCH_W, n2), F32),
        grid=(n2 // tn,),
        in_specs=[pl.BlockSpec((EMB_PAD, tn), lambda j: (0, j))] + [full(a) for a in args[1:]],
        out_specs=pl.BlockSpec((BRANCH_W, tn), lambda j: (0, j)),
        compiler_params=_cparams(("parallel",)),
        name="hyena_filter",
    )(*args)


def _hyena_conv_kernel(nblk, bp, k_ref, u_ref, o_ref):
    nrow = nblk * bp

    def diag(di, acc):
        win = jnp.concatenate([k_ref[0, pl.ds(di, 1), :], k_ref[0, pl.ds(di + 1, 1), :]], axis=1)
        x = jnp.broadcast_to(win, (HY_BLK, 2 * HY_BLK))
        w = pltpu.roll(x, HY_BLK + 1, 1, stride=1, stride_axis=0)[:, :HY_BLK]
        start = pl.multiple_of((2 * (nblk - 1) - di) * bp, SUBLANES)
        lhs = u_ref[0, pl.ds(start, nrow), :]
        return acc + jnp.dot(lhs.astype(BF16), w.astype(BF16), preferred_element_type=F32)

    o_ref[0] = lax.fori_loop(0, 2 * nblk - 1, diag, jnp.zeros((nrow, HY_BLK), F32))


def hyena_conv(u, ktab):
    nb, l, c = u.shape
    nblk = l // HY_BLK
    bp = -(-nb // SUBLANES) * SUBLANES
    ut = jnp.transpose(u.reshape(nb, nblk, HY_BLK, c), (3, 1, 0, 2))
    ut = jnp.pad(ut, ((0, 0), (nblk - 1, nblk - 1), (0, bp - nb), (0, 0)))
    nrows_in = (3 * nblk - 2) * bp
    ut = ut.reshape(c, nrows_in, HY_BLK)
    k3 = ktab.reshape(c, 2 * nblk, HY_BLK)
    out = pl.pallas_call(
        functools.partial(_hyena_conv_kernel, nblk, bp),
        out_shape=jax.ShapeDtypeStruct((c, nblk * bp, HY_BLK), F32),
        grid=(c,),
        in_specs=[pl.BlockSpec((1, 2 * nblk, HY_BLK), lambda ch: (ch, 0, 0)),
                  pl.BlockSpec((1, nrows_in, HY_BLK), lambda ch: (ch, 0, 0))],
        out_specs=pl.BlockSpec((1, nblk * bp, HY_BLK), lambda ch: (ch, 0, 0)),
        compiler_params=_cparams(("parallel",)),
        name="hyena_conv",
    )(k3, ut)
    out = out.reshape(c, nblk, bp, HY_BLK)[:, :, :nb]
    return jnp.transpose(out, (2, 1, 3, 0)).reshape(nb, l, c)


def _route(logits, bias):
    s = jax.nn.sigmoid(logits)
    sel = s + bias
    srow = [s[e:e + 1] for e in range(N_EXPERTS)]
    row = [sel[e:e + 1] for e in range(N_EXPERTS)]
    best, gi = None, None
    for g in range(N_GROUPS):
        a, b, c, d = row[4 * g:4 * g + 4]
        hi1, lo1, hi2, lo2 = jnp.maximum(a, b), jnp.minimum(a, b), jnp.maximum(c, d), jnp.minimum(c, d)
        score = jnp.maximum(hi1, hi2) + jnp.maximum(jnp.minimum(hi1, hi2), jnp.maximum(lo1, lo2))
        if g == 0:
            best, gi = score, jnp.zeros(score.shape, jnp.int32)
        else:
            better = score > best
            gi = jnp.where(better, g, gi)
            best = jnp.where(better, score, best)
    neg = -jnp.inf
    msel = [jnp.where(gi == e // EXPERTS_PER_GROUP, row[e], neg) for e in range(N_EXPERTS)]

    def arg_first_max(vals):
        bv, bi = vals[0], jnp.zeros(vals[0].shape, jnp.int32)
        for e in range(1, N_EXPERTS):
            better = vals[e] > bv
            bi = jnp.where(better, e, bi)
            bv = jnp.where(better, vals[e], bv)
        return bi

    i1 = arg_first_max(msel)
    i2 = arg_first_max([jnp.where(i1 == e, neg, msel[e]) for e in range(N_EXPERTS)])
    w1 = sum(jnp.where(i1 == e, srow[e], 0.0) for e in range(N_EXPERTS))
    w2 = sum(jnp.where(i2 == e, srow[e], 0.0) for e in range(N_EXPERTS))
    den = w1 + w2
    g1, g2 = w1 / den, w2 / den
    return jnp.concatenate([jnp.where(i1 == e, g1, 0.0) + jnp.where(i2 == e, g2, 0.0)
                            for e in range(N_EXPERTS)], axis=0)


def _merge_kernel(ya_ref, x0_ref, u_ref, yc_ref, ycv_ref, yd_ref, pg_ref, x_ref, ga_ref, shf_ref, scf_ref,
                  skip_ref, wb_ref, wo_ref, g1_ref, b1_ref, rwt_ref, rb_ref, x1_o, hf_o, gates_o):
    yb = (x0_ref[0] * (yc_ref[0] + u_ref[0] * skip_ref[...])).astype(BF16)
    ys = (ya_ref[0], yb, ycv_ref[0], yd_ref[0])
    merged = None
    for n in range(N_BRANCHES):
        gate = jax.nn.sigmoid(pg_ref[0][:, n * D_MODEL:(n + 1) * D_MODEL])
        term = gate * jnp.dot(ys[n], wb_ref[n], preferred_element_type=F32)
        merged = term if merged is None else merged + term
    out = jnp.dot(merged.astype(BF16), wo_ref[...], preferred_element_type=F32)
    alpha = (2 * 2) ** 0.25
    x1 = _ln(alpha * x_ref[0] + ga_ref[0] * out) * g1_ref[...] + b1_ref[...]
    hf = _ln(x1) * (1.0 + scf_ref[0]) + shf_ref[0]
    x1_o[0] = x1
    hf_o[0] = hf.astype(BF16)
    logits = lax.dot_general(rwt_ref[...], hf, (((1,), (1,)), ((), ())), precision=HI,
                             preferred_element_type=F32)
    gates_o[...] = _route(logits, rb_ref[...])


def merge(ya, x0, u, yconv, ycv, yd, pg, x_all, mod3, nct, skip, wb, wo, ln_g, ln_b, rwt, rbias):
    nb, ta, d = x_all.shape
    nt = ta // TT
    nat = pl.BlockSpec((1, TT, BRANCH_W), lambda b, t: (b, t, 0))
    wide = pl.BlockSpec((1, TT, d), lambda b, t: (b, t, 0))

    def full(a):
        nd = a.ndim
        return pl.BlockSpec(a.shape, lambda b, t: (0,) * nd)

    consts = [skip.reshape(1, -1), wb, wo, ln_g.reshape(1, -1), ln_b.reshape(1, -1), rwt, rbias.reshape(-1, 1)]
    return pl.pallas_call(
        _merge_kernel,
        out_shape=[jax.ShapeDtypeStruct((nb, ta, d), F32), jax.ShapeDtypeStruct((nb, ta, d), BF16),
                   jax.ShapeDtypeStruct((N_EXPERTS, nb * ta), F32)],
        grid=(nb, nt),
        in_specs=[nat] * 6 + [pl.BlockSpec((1, TT, N_BRANCHES * d), lambda b, t: (b, t, 0)), wide,
                              _mod_spec(2, nct, nb), _mod_spec(3, nct, nb), _mod_spec(4, nct, nb)]
                 + [full(a) for a in consts],
        out_specs=[wide, wide, pl.BlockSpec((N_EXPERTS, TT), lambda b, t: (0, b * nt + t))],
        compiler_params=_cparams(("parallel", "parallel"), VMEM_LIMIT),
        name="merge",
    )(ya, x0, u, yconv, ycv, yd, pg, x_all, mod3, mod3, mod3, *consts)


def _moe_kernel(h_ref, g_ref, w1_ref, w3_ref, w2_ref, o_ref):
    e = pl.program_id(1)
    h = h_ref[...]
    a = jnp.dot(h, w1_ref[0], preferred_element_type=F32)
    b = jnp.dot(h, w3_ref[0], preferred_element_type=F32)
    act = (a * jax.nn.sigmoid(a)) * b
    gt = g_ref[...]
    lane = lax.broadcasted_iota(jnp.int32, gt.shape, 1)
    gcol = jnp.sum(jnp.where(lane == e, gt, 0.0), axis=1, keepdims=True)
    contrib = gcol * jnp.dot(act.astype(BF16), w2_ref[0], preferred_element_type=F32)

    @pl.when(e == 0)
    def _():
        o_ref[...] = contrib

    @pl.when(e > 0)
    def _():
        o_ref[...] += contrib


def moe_dense(hf, gates, w1, w3, w2):
    n, d = hf.shape
    ne, _, de = w1.shape
    tm = _pick_tile(n, (1024, 512, 256))
    return pl.pallas_call(
        _moe_kernel,
        out_shape=jax.ShapeDtypeStruct((n, d), F32),
        grid=(n // tm, ne),
        in_specs=[pl.BlockSpec((tm, d), lambda i, e: (i, 0)),
                  pl.BlockSpec((tm, LANES), lambda i, e: (i, 0)),
                  pl.BlockSpec((1, d, de), lambda i, e: (e, 0, 0)),
                  pl.BlockSpec((1, d, de), lambda i, e: (e, 0, 0)),
                  pl.BlockSpec((1, de, d), lambda i, e: (e, 0, 0))],
        out_specs=pl.BlockSpec((tm, d), lambda i, e: (i, 0)),
        compiler_params=_cparams(("parallel", "arbitrary"), VMEM_LIMIT),
        name="moe",
    )(hf, gates, w1, w3, w2)


def _ln2_kernel(x_ref, f_ref, gf_ref, g_ref, b_ref, o_ref):
    alpha = (2 * 2) ** 0.25
    o_ref[0] = _ln(alpha * x_ref[0] + gf_ref[0] * f_ref[0]) * g_ref[...] + b_ref[...]


def ln2(x1, f, mod3, nct, ln_g, ln_b):
    nb, ta, d = x1.shape
    wide = pl.BlockSpec((1, TT, d), lambda b, t: (b, t, 0))
    row = pl.BlockSpec((1, d), lambda b, t: (0, 0))
    return pl.pallas_call(
        _ln2_kernel,
        out_shape=jax.ShapeDtypeStruct((nb, ta, d), F32),
        grid=(nb, ta // TT),
        in_specs=[wide, wide, _mod_spec(5, nct, nb), row, row],
        out_specs=wide,
        compiler_params=_cparams(("parallel", "parallel")),
        name="ln2",
    )(x1, f, mod3, ln_g.reshape(1, -1), ln_b.reshape(1, -1))


def _rope_tables(l, lc):
    half = HEAD_DIM // 2
    inv = ROPE_THETA ** (-np.arange(0, half, 2, dtype=np.float64) / half)
    t = np.arange(l)
    rows, cols = t // GRID_W, t % GRID_W
    ang = np.concatenate([rows[:, None] * inv, cols[:, None] * inv], -1)
    ang = np.concatenate([np.zeros((lc, half)), ang], 0)
    cos = np.repeat(np.cos(ang), 2, axis=1)
    sin = np.repeat(np.sin(ang), 2, axis=1)
    sin[:, 0::2] *= -1.0
    reps = Q_W // HEAD_DIM
    return (jnp.asarray(np.tile(cos, (1, reps)), F32), jnp.asarray(np.tile(sin, (1, reps)), F32))


def _block_diag_ones():
    i = np.arange(BRANCH_W) // HEAD_DIM
    return jnp.asarray((i[:, None] == i[None, :]).astype(np.float32))


def kernel(x, c, ctx, c_ctx, ada_w, ada_b, w_in, rwkv_mu, rwkv_w0, rwkv_w_up, rwkv_a0, rwkv_a_up, rwkv_g_up, rwkv_k_k, rwkv_k_a, rwkv_r_k, rwkv_lnx_g, rwkv_lnx_b, hyena_conv, hyena_w1, hyena_b1, hyena_freq1, hyena_w2, hyena_b2, hyena_freq2, hyena_w3, hyena_skip, sconv_w, attn_q_norm, attn_k_norm, w_branch, w_out, ln1_g, ln1_b, ln2_g, ln2_b, router_w, router_bias, exp_w1, exp_w3, exp_w2):
    nb, l, d = x.shape
    lc = ctx.shape[1]
    depth = ada_w.shape[0]
    assert d == D_MODEL and lc % TT == 0 and l % TT == 0 and l % GRID_W == 0
    ta = lc + l
    nct = lc // TT
    x_all = jnp.concatenate([ctx, x], axis=1)

    mod_rows = -(-(nb + 1) // SUBLANES) * SUBLANES
    cc = jnp.zeros((mod_rows, d), F32).at[:nb].set(c).at[nb].set(c_ctx)
    mod = ada_mod(cc, ada_w, ada_b)

    cos_t, sin_t = _rope_tables(l, lc)
    bd = _block_diag_ones()
    rwt = router_w.T

    for li in range(depth):
        mod3 = mod[li].reshape(mod_rows, 1, N_MOD * d)
        wl = w_in[li].astype(BF16)
        h = lnmod(x_all, mod3, nct, 0, 1).reshape(nb * ta, d)
        p_rwkv = matmul(h, wl[:, :OFF_HYENA]).reshape(nb, ta, -1)
        p_hs = matmul(h, wl[:, OFF_HYENA:OFF_ATTN]).reshape(nb, ta, -1)
        p_attn = matmul(h, wl[:, OFF_ATTN:OFF_GATE]).reshape(nb, ta, -1)
        p_gate = matmul(h, wl[:, OFF_GATE:]).reshape(nb, ta, -1)

        r, kk, w0, w1, k0, k1, b0, b1, v, g, bon = rwkv_prep(
            p_rwkv, nct, rwkv_mu[li], rwkv_w0[li], rwkv_w_up[li], rwkv_a0[li], rwkv_a_up[li], rwkv_g_up[li],
            rwkv_k_k[li], rwkv_k_a[li], rwkv_r_k[li], bd)
        yf, yb = rwkv_scan(r, kk, v, w0, w1, k0, k1, b0, b1, lc)
        ya = rwkv_out(yf, yb, g, bon, rwkv_lnx_g[li], rwkv_lnx_b[li])

        q, kx, vx = attn_prep(p_attn, cos_t, sin_t, attn_q_norm[li], attn_k_norm[li], bd)
        yd = attention(q, kx, vx, lc)

        x0, u, ycv = hs_pre(p_hs, nct, hyena_conv[li], sconv_w[li])
        fargs = (hyena_w1[li], hyena_b1[li], hyena_freq1[li], hyena_w2[li], hyena_b2[li], hyena_freq2[li],
                 hyena_w3[li])
        yconv = jnp.concatenate([hyena_conv_seg(u[:, :lc], fargs), hyena_conv_seg(u[:, lc:], fargs)], axis=1)

        x1, hf, gates_t = merge(ya, x0, u, yconv, ycv, yd, p_gate, x_all, mod3, nct, hyena_skip[li],
                                w_branch[li].astype(BF16), w_out[li].astype(BF16), ln1_g[li], ln1_b[li],
                                rwt, router_bias)
        gates = jnp.pad(gates_t.T, ((0, 0), (0, LANES - N_EXPERTS)))
        f = moe_dense(hf.reshape(nb * ta, d), gates, exp_w1[li].astype(BF16), exp_w3[li].astype(BF16),
                      exp_w2[li].astype(BF16)).reshape(nb, ta, d)
        x_all = ln2(x1, f, mod3, nct, ln2_g[li], ln2_b[li])
    return x_all[:, lc:]


def hyena_conv_seg(u_seg, fargs):
    ktab = hyena_filter_table(u_seg.shape[1], *fargs)
    return hyena_conv(u_seg, ktab)
```

```python
import functools
import math

import numpy as np
import jax
import jax.numpy as jnp
from jax import lax
from jax.experimental import pallas as pl
from jax.experimental.pallas import tpu as pltpu

F32 = jnp.float32
BF16 = jnp.bfloat16
HI = lax.Precision.HIGHEST

D_MODEL = 1024
GRID_W = 64
BRANCH_W = 256
HEAD_DIM = 64
N_BRANCHES = 4
N_MOD = 6
RWKV_HEADS = 4
RWKV_COLS = 1024
RWKV_GN_EPS = 64e-5
HYENA_COLS = 768
HYENA_EMB = 33
HYENA_FAST_DECAY = 0.3
HYENA_SLOW_DECAY = 1.5
HYENA_TARGET = 1e-2
SCONV_COLS = 768
Q_W = 256
KV_W = 128
ATTN_COLS = 512
ROPE_THETA = 10000.0
RMS_EPS = 1e-6
OFF_HYENA = RWKV_COLS
OFF_SCONV = OFF_HYENA + HYENA_COLS
OFF_ATTN = OFF_SCONV + SCONV_COLS
OFF_GATE = OFF_ATTN + ATTN_COLS
N_EXPERTS = 16
N_GROUPS = 4
EXPERTS_PER_GROUP = 4
D_EXPERT = 512
LN_EPS = 1e-6

SUBLANES = 8
LANES = 128
TT = 256
SCAN_BLK = LANES // 2
SCAN_GRP = 4
HY_BLK = 256
VMEM_LIMIT = 56 * 1024 * 1024


def _cparams(sem, vmem=None):
    return pltpu.CompilerParams(dimension_semantics=sem, vmem_limit_bytes=vmem)


def _ln(xf):
    mu = jnp.mean(xf, -1, keepdims=True)
    xc = xf - mu
    var = jnp.mean(xc * xc, -1, keepdims=True)
    return xc * lax.rsqrt(var + LN_EPS)


def _pick_tile(n, cands):
    for c in cands:
        if n % c == 0:
            return c
    raise ValueError(f"no tile for {n}")


def _ada_kernel(c_ref, w_ref, b_ref, o_ref):
    c = c_ref[...]
    a = c * jax.nn.sigmoid(c)
    o_ref[0] = jnp.dot(a, w_ref[0], precision=HI, preferred_element_type=F32) + b_ref[0]


def ada_mod(cc, ada_w, ada_b):
    depth, d, n = ada_w.shape
    rows = cc.shape[0]
    return pl.pallas_call(
        _ada_kernel,
        out_shape=jax.ShapeDtypeStruct((depth, rows, n), F32),
        grid=(depth, n // d),
        in_specs=[pl.BlockSpec((rows, d), lambda l, j: (0, 0)),
                  pl.BlockSpec((1, d, d), lambda l, j: (l, 0, j)),
                  pl.BlockSpec((1, 1, d), lambda l, j: (l, 0, j))],
        out_specs=pl.BlockSpec((1, rows, d), lambda l, j: (l, 0, j)),
        compiler_params=_cparams(("parallel", "parallel"), VMEM_LIMIT),
        name="ada_mod",
    )(cc, ada_w, ada_b.reshape(depth, 1, n))


def _lnmod_kernel(x_ref, sh_ref, sc_ref, o_ref):
    h = _ln(x_ref[0]) * (1.0 + sc_ref[0]) + sh_ref[0]
    o_ref[0] = h.astype(BF16)


def _mod_spec(col, nct, nb):
    return pl.BlockSpec((1, 1, D_MODEL), lambda b, t: (jnp.where(t < nct, nb, b), 0, col))


def lnmod(x_all, mod3, nct, col_shift, col_scale):
    nb, ta, d = x_all.shape
    return pl.pallas_call(
        _lnmod_kernel,
        out_shape=jax.ShapeDtypeStruct((nb, ta, d), BF16),
        grid=(nb, ta // TT),
        in_specs=[pl.BlockSpec((1, TT, d), lambda b, t: (b, t, 0)),
                  _mod_spec(col_shift, nct, nb), _mod_spec(col_scale, nct, nb)],
        out_specs=pl.BlockSpec((1, TT, d), lambda b, t: (b, t, 0)),
        compiler_params=_cparams(("parallel", "parallel")),
        name="lnmod",
    )(x_all, mod3, mod3)


def _mm_kernel(a_ref, b_ref, o_ref):
    o_ref[...] = jnp.dot(a_ref[...], b_ref[...], preferred_element_type=F32).astype(o_ref.dtype)


def matmul(a, b, out_dtype=F32):
    m, k = a.shape
    _, n = b.shape
    tm = _pick_tile(m, (1024, 512, 256))
    tn = _pick_tile(n, (1024, 512, 256))
    return pl.pallas_call(
        _mm_kernel,
        out_shape=jax.ShapeDtypeStruct((m, n), out_dtype),
        grid=(m // tm, n // tn),
        in_specs=[pl.BlockSpec((tm, k), lambda i, j: (i, 0)),
                  pl.BlockSpec((k, tn), lambda i, j: (0, j))],
        out_specs=pl.BlockSpec((tm, tn), lambda i, j: (i, j)),
        compiler_params=_cparams(("parallel", "parallel"), VMEM_LIMIT),
        name="matmul",
    )(a, b)


def _halo_specs(width, ta):
    nblk8 = ta // SUBLANES
    per = TT // SUBLANES
    cur = pl.BlockSpec((1, TT, width), lambda b, t: (b, t, 0))
    prev = pl.BlockSpec((1, SUBLANES, width), lambda b, t: (b, jnp.maximum(t * per - 1, 0), 0))
    nxt = pl.BlockSpec((1, SUBLANES, width), lambda b, t: (b, jnp.minimum((t + 1) * per, nblk8 - 1), 0))
    return [cur, prev, nxt]


def _neighbours(cur, prev_ref, next_ref, nct, nt):
    t = pl.program_id(1)
    seg_start = jnp.logical_or(t == 0, t == nct)
    seg_end = jnp.logical_or(t == nct - 1, t == nt - 1)
    prev_row = prev_ref[0][SUBLANES - 1:SUBLANES, :] * jnp.where(seg_start, 0.0, 1.0)
    next_row = next_ref[0][0:1, :] * jnp.where(seg_end, 0.0, 1.0)
    row = lax.broadcasted_iota(jnp.int32, (TT, 1), 0)
    xm1 = jnp.where(row == 0, prev_row, pltpu.roll(cur, 1, axis=0))
    xp1 = jnp.where(row == TT - 1, next_row, pltpu.roll(cur, TT - 1, axis=0))
    return xm1, xp1


def _rwkv_prep_kernel(nct, nt, cur_ref, prev_ref, next_ref, mu_ref, w0_ref, wup_ref, a0_ref, aup_ref,
                      gup_ref, kk_ref, ka_ref, rk_ref, bd_ref,
                      r_o, kk_o, w0_o, w1_o, k0_o, k1_o, b0_o, b1_o, v_o, g_o, bon_o):
    cur = cur_ref[0]
    xm1, xp1 = _neighbours(cur, prev_ref, next_ref, nct, nt)
    p = cur + mu_ref[...] * (0.5 * (xm1 + xp1) - cur)
    c = BRANCH_W
    r, k, v = p[:, 0:c], p[:, c:2 * c], p[:, 2 * c:3 * c]
    wd = p[:, 3 * c:3 * c + 64]
    ad = p[:, 3 * c + 64:3 * c + 128]
    gd = p[:, 3 * c + 128:3 * c + 256]
    bd = bd_ref[...]
    kk = k * kk_ref[...]
    ss = jnp.dot(kk * kk, bd, precision=HI, preferred_element_type=F32)
    kkn = kk * lax.rsqrt(jnp.maximum(ss, 1e-24))
    twd = jnp.tanh(wd)
    ka = ka_ref[...]
    kdirs = []
    w_outs, k_outs, b_outs = (w0_o, w1_o), (k0_o, k1_o), (b0_o, b1_o)
    for d in range(2):
        wlog = w0_ref[d:d + 1, :] + jnp.dot(twd, wup_ref[d], precision=HI, preferred_element_type=F32)
        decay = jnp.exp(-math.exp(-0.5) * jax.nn.sigmoid(wlog))
        a = jax.nn.sigmoid(a0_ref[d:d + 1, :] + jnp.dot(ad, aup_ref[d], precision=HI, preferred_element_type=F32))
        kdir = k * (1.0 + (a - 1.0) * ka)
        bdir = kkn * a
        kdirs.append(kdir)
        w_outs[d][0] = decay
        k_outs[d][0] = kdir
        b_outs[d][0] = bdir
    r_o[0] = r
    kk_o[0] = kkn
    v_o[0] = v
    g_o[0] = jnp.dot(jax.nn.sigmoid(gd), gup_ref[...], precision=HI, preferred_element_type=F32)
    rkk = r * rk_ref[...] * (kdirs[0] + kdirs[1])
    bon_o[0] = jnp.dot(rkk, bd, precision=HI, preferred_element_type=F32) * v


def rwkv_prep(p_rwkv, nct, mu, w0, w_up, a0, a_up, g_up, k_k, k_a, r_k, bd):
    nb, ta, _ = p_rwkv.shape
    nt = ta // TT
    c = BRANCH_W
    nat = jax.ShapeDtypeStruct((nb, ta, c), F32)
    nat_spec = pl.BlockSpec((1, TT, c), lambda b, t: (b, t, 0))

    def full(a):
        nd = a.ndim
        return pl.BlockSpec(a.shape, lambda b, t: (0,) * nd)

    consts = [mu.reshape(1, -1), w0, w_up, a0, a_up, g_up, k_k.reshape(1, -1), k_a.reshape(1, -1),
              r_k.reshape(1, -1), bd]
    return pl.pallas_call(
        functools.partial(_rwkv_prep_kernel, nct, nt),
        out_shape=[nat] * 11,
        grid=(nb, nt),
        in_specs=_halo_specs(RWKV_COLS, ta) + [full(a) for a in consts],
        out_specs=[nat_spec] * 11,
        compiler_params=_cparams(("parallel", "parallel"), VMEM_LIMIT),
        name="rwkv_prep",
    )(p_rwkv, p_rwkv, p_rwkv, *consts)


def _scan_kernel(nb, rf, kkf, wf, kf, bf, vtf, rb, kkb, wb, kb, bb, vtb, bd_ref, yf_o, yb_o, s_scr):
    step = pl.program_id(0)

    @pl.when(step == 0)
    def _():
        s_scr[...] = jnp.zeros_like(s_scr)

    bd = bd_ref[...]
    lane_tok = lax.broadcasted_iota(jnp.int32, (HEAD_DIM, LANES), 1) % HEAD_DIM
    rows = ((rf, kkf, wf, kf, bf), (rb, kkb, wb, kb, bb))
    vts = (vtf, vtb)
    y_out = (yf_o, yb_o)

    def seg_sum(x):
        return jnp.dot(x, bd, preferred_element_type=F32)

    def group(g, carry):
        for j in range(SCAN_GRP):
            tf = g * SCAN_GRP + j
            chains, lhs = [], []
            for d in range(2):
                t = tf if d == 0 else SCAN_BLK - 1 - tf
                tok = lane_tok == t
                for b in range(nb):
                    row5 = [ref[b, pl.ds(t, 1), :] for ref in rows[d]]
                    for p in range(2):
                        sl = slice(p * LANES, (p + 1) * LANES)
                        r_row, kk_row, w_row, k_row, b_row = (a[:, sl] for a in row5)
                        s = s_scr[d, b, p]
                        lhs.append(s * kk_row)
                        lhs.append(jnp.where(tok, vts[d][b, p, 0], 0.0))
                        chains.append((d, b, p, tok, s, r_row, w_row, k_row, b_row))
            res = seg_sum(jnp.concatenate(lhs, axis=0))
            ylhs = []
            for i, (d, b, p, tok, s, r_row, w_row, k_row, b_row) in enumerate(chains):
                skk = res[(2 * i) * HEAD_DIM:(2 * i + 1) * HEAD_DIM]
                vcol = res[(2 * i + 1) * HEAD_DIM:(2 * i + 2) * HEAD_DIM]
                s = s * w_row - skk * b_row + vcol * k_row
                s_scr[d, b, p] = s
                ylhs.append(s * r_row)
            yres = seg_sum(jnp.concatenate(ylhs, axis=0))
            for i, (d, b, p, tok, *_) in enumerate(chains):
                pltpu.store(y_out[d].at[b, p, 0], yres[i * HEAD_DIM:(i + 1) * HEAD_DIM], mask=tok)
        return carry

    lax.fori_loop(0, SCAN_BLK // SCAN_GRP, group, 0)


def rwkv_scan(r, kk, v, w0, w1, k0, k1, b0, b1, lc, bd_pair):
    nb, ta, c = r.shape
    nblk = ta // SCAN_BLK
    nctb = lc // SCAN_BLK
    npair = c // LANES

    def to_cols(a):
        a = a.reshape(nb, nblk, SCAN_BLK, npair, 2, HEAD_DIM)
        return jnp.transpose(a, (0, 3, 1, 5, 4, 2)).reshape(nb, npair, nblk, HEAD_DIM, LANES)

    def from_cols(a):
        a = a.reshape(nb, npair, nblk, HEAD_DIM, 2, SCAN_BLK)
        return jnp.transpose(a, (0, 2, 5, 1, 4, 3)).reshape(nb, ta, c)

    def fwd(s):
        return s

    def bwd(s):
        return jnp.where(s < nctb, nctb - 1 - s, nblk - 1 - (s - nctb))

    def nat(idx):
        return pl.BlockSpec((nb, SCAN_BLK, c), lambda s: (0, idx(s), 0))

    def cols(idx):
        return pl.BlockSpec((nb, npair, 1, HEAD_DIM, LANES), lambda s: (0, 0, idx(s), 0, 0))

    vt = to_cols(v)
    out = jax.ShapeDtypeStruct((nb, npair, nblk, HEAD_DIM, LANES), F32)
    yf, yb = pl.pallas_call(
        functools.partial(_scan_kernel, nb),
        out_shape=[out, out],
        grid=(nblk,),
        in_specs=[nat(fwd)] * 5 + [cols(fwd)] + [nat(bwd)] * 5 + [cols(bwd)]
                 + [pl.BlockSpec(bd_pair.shape, lambda s: (0, 0))],
        out_specs=[cols(fwd), cols(bwd)],
        scratch_shapes=[pltpu.VMEM((2, nb, npair, HEAD_DIM, LANES), F32)],
        compiler_params=_cparams(("arbitrary",), VMEM_LIMIT),
        name="rwkv_scan",
    )(r, kk, w0, k0, b0, vt, r, kk, w1, k1, b1, vt, bd_pair)
    return from_cols(yf), from_cols(yb)


def _rwkv_out_kernel(yf_ref, yb_ref, g_ref, bon_ref, lg_ref, lb_ref, bd_ref, o_ref):
    y = yf_ref[0] + yb_ref[0]
    bd = bd_ref[...]
    mu = jnp.dot(y, bd, precision=HI, preferred_element_type=F32) * (1.0 / HEAD_DIM)
    yc = y - mu
    var = jnp.dot(yc * yc, bd, precision=HI, preferred_element_type=F32) * (1.0 / HEAD_DIM)
    yn = yc * lax.rsqrt(var + RWKV_GN_EPS) * lg_ref[...] + lb_ref[...]
    o_ref[0] = ((yn + bon_ref[0]) * g_ref[0]).astype(BF16)


def rwkv_out(yf, yb, g, bon, lnx_g, lnx_b, bd):
    nb, ta, _ = yf.shape
    nat_spec = pl.BlockSpec((1, TT, BRANCH_W), lambda b, t: (b, t, 0))
    row = pl.BlockSpec((1, BRANCH_W), lambda b, t: (0, 0))
    return pl.pallas_call(
        _rwkv_out_kernel,
        out_shape=jax.ShapeDtypeStruct((nb, ta, BRANCH_W), BF16),
        grid=(nb, ta // TT),
        in_specs=[nat_spec] * 4 + [row, row, pl.BlockSpec(bd.shape, lambda b, t: (0, 0))],
        out_specs=nat_spec,
        compiler_params=_cparams(("parallel", "parallel")),
        name="rwkv_out",
    )(yf, yb, g, bon, lnx_g.reshape(1, -1), lnx_b.reshape(1, -1), bd)


def _pair_swap(x):
    lane = lax.broadcasted_iota(jnp.int32, x.shape, 1)
    n = x.shape[1]
    return jnp.where(lane % 2 == 0, pltpu.roll(x, n - 1, axis=1), pltpu.roll(x, 1, axis=1))


def _attn_prep_kernel(p_ref, cos_ref, sin_ref, qg_ref, kg_ref, bd_ref, q_o, k_o, v_o):
    p = p_ref[0]
    q, k, v = p[:, :Q_W], p[:, Q_W:Q_W + KV_W], p[:, Q_W + KV_W:]
    bd = bd_ref[...]
    cos, sin = cos_ref[...], sin_ref[...]
    qms = jnp.dot(q * q, bd, precision=HI, preferred_element_type=F32) * (1.0 / HEAD_DIM)
    qn = q * lax.rsqrt(qms + RMS_EPS) * qg_ref[...]
    qr = qn * cos + _pair_swap(qn) * sin
    q_o[0] = (qr * HEAD_DIM ** -0.5).astype(BF16)
    kms = jnp.dot(k * k, bd[:KV_W, :KV_W], precision=HI, preferred_element_type=F32) * (1.0 / HEAD_DIM)
    kn = k * lax.rsqrt(kms + RMS_EPS) * kg_ref[...]
    kr = kn * cos[:, :KV_W] + _pair_swap(kn) * sin[:, :KV_W]
    for g in range(KV_W // HEAD_DIM):
        sl = slice(g * HEAD_DIM, (g + 1) * HEAD_DIM)
        k_o[0, g] = kr[:, sl].astype(BF16)
        v_o[0, g] = v[:, sl].astype(BF16)


def attn_prep(p_attn, cos_t, sin_t, q_norm, k_norm, bd):
    nb, ta, _ = p_attn.shape
    ng = KV_W // HEAD_DIM
    qg = jnp.tile(q_norm, Q_W // HEAD_DIM).reshape(1, -1)
    kg = jnp.tile(k_norm, ng).reshape(1, -1)
    kv_shape = jax.ShapeDtypeStruct((nb, ng, ta, HEAD_DIM), BF16)
    kv_spec = pl.BlockSpec((1, ng, TT, HEAD_DIM), lambda b, t: (b, 0, t, 0))
    return pl.pallas_call(
        _attn_prep_kernel,
        out_shape=[jax.ShapeDtypeStruct((nb, ta, Q_W), BF16), kv_shape, kv_shape],
        grid=(nb, ta // TT),
        in_specs=[pl.BlockSpec((1, TT, ATTN_COLS), lambda b, t: (b, t, 0)),
                  pl.BlockSpec((TT, Q_W), lambda b, t: (t, 0)),
                  pl.BlockSpec((TT, Q_W), lambda b, t: (t, 0)),
                  pl.BlockSpec((1, Q_W), lambda b, t: (0, 0)),
                  pl.BlockSpec((1, KV_W), lambda b, t: (0, 0)),
                  pl.BlockSpec(bd.shape, lambda b, t: (0, 0))],
        out_specs=[pl.BlockSpec((1, TT, Q_W), lambda b, t: (b, t, 0)), kv_spec, kv_spec],
        compiler_params=_cparams(("parallel", "parallel")),
        name="attn_prep",
    )(p_attn, cos_t, sin_t, qg, kg, bd)


def _attn_kernel(nct, lc, q_ref, k_ref, v_ref, o_ref):
    t = pl.program_id(2)

    def run(kk, vv):
        outs = []
        for r in range(2):
            q = q_ref[0][:, r * HEAD_DIM:(r + 1) * HEAD_DIM]
            s = lax.dot_general(q, kk, (((1,), (1,)), ((), ())), preferred_element_type=F32)
            p = jnp.exp(s - jnp.max(s, -1, keepdims=True))
            l = jnp.sum(p, -1, keepdims=True)
            o = jnp.dot(p.astype(BF16), vv, preferred_element_type=F32)
            outs.append(o / l)
        o_ref[0] = jnp.concatenate(outs, axis=1).astype(BF16)

    @pl.when(t < nct)
    def _():
        run(k_ref[0, 0, :lc, :], v_ref[0, 0, :lc, :])

    @pl.when(t >= nct)
    def _():
        run(k_ref[0, 0], v_ref[0, 0])


def attention(q, k, v, lc):
    nb, ta, _ = q.shape
    ng = k.shape[1]
    nct = lc // TT
    qo_spec = pl.BlockSpec((1, TT, 2 * HEAD_DIM), lambda b, g, t: (b, t, g))
    kv_spec = pl.BlockSpec((1, 1, ta, HEAD_DIM), lambda b, g, t: (b, g, 0, 0))
    return pl.pallas_call(
        functools.partial(_attn_kernel, nct, lc),
        out_shape=jax.ShapeDtypeStruct((nb, ta, Q_W), BF16),
        grid=(nb, ng, ta // TT),
        in_specs=[qo_spec, kv_spec, kv_spec],
        out_specs=qo_spec,
        compiler_params=_cparams(("parallel", "parallel", "arbitrary"), VMEM_LIMIT),
        name="attention",
    )(q, k, v)


def _hs_pre_kernel(nct, nt, cur_ref, prev_ref, next_ref, hw_ref, sw_ref, x0_o, u_o, ycv_o):
    cur = cur_ref[0]
    xm1, xp1 = _neighbours(cur, prev_ref, next_ref, nct, nt)
    c = BRANCH_W
    hc = HYENA_COLS
    hw = hw_ref[...]
    ph = hw[0:1] * xm1[:, :hc] + hw[1:2] * cur[:, :hc] + hw[2:3] * xp1[:, :hc]
    x0_o[0] = ph[:, :c]
    u_o[0] = ph[:, c:2 * c] * ph[:, 2 * c:3 * c]
    sw = sw_ref[...]

    def cx(a):
        return a[:, hc + c:hc + 2 * c] * a[:, hc + 2 * c:hc + 3 * c]

    conv = sw[0:1] * cx(xm1) + sw[1:2] * cx(cur) + sw[2:3] * cx(xp1)
    ycv_o[0] = (cur[:, hc:hc + c] * conv).astype(BF16)


def hs_pre(p_hs, nct, hyena_conv, sconv_w):
    nb, ta, w = p_hs.shape
    nt = ta // TT
    nat = pl.BlockSpec((1, TT, BRANCH_W), lambda b, t: (b, t, 0))
    return pl.pallas_call(
        functools.partial(_hs_pre_kernel, nct, nt),
        out_shape=[jax.ShapeDtypeStruct((nb, ta, BRANCH_W), F32),
                   jax.ShapeDtypeStruct((nb, ta, BRANCH_W), F32),
                   jax.ShapeDtypeStruct((nb, ta, BRANCH_W), BF16)],
        grid=(nb, nt),
        in_specs=_halo_specs(w, ta) + [pl.BlockSpec(hyena_conv.shape, lambda b, t: (0, 0)),
                                       pl.BlockSpec(sconv_w.shape, lambda b, t: (0, 0))],
        out_specs=[nat, nat, nat],
        compiler_params=_cparams(("parallel", "parallel"), VMEM_LIMIT),
        name="hs_pre",
    )(p_hs, p_hs, p_hs, hyena_conv, sconv_w)


EMB_PAD = 40


def _filter_tables(lh):
    n = np.arange(2 * lh)
    pos = np.abs(n - (lh - 1)).astype(np.float64)
    bands = (HYENA_EMB - 1) // 2
    t = np.minimum(pos, lh - 1) / (lh - 1)
    wpos = 2.0 * math.pi * pos / lh
    f = np.linspace(1e-4, bands - 1, bands)[:, None]
    z = np.zeros((EMB_PAD, 2 * lh), np.float32)
    z[0] = t
    z[1:1 + bands] = np.cos(f * wpos[None, :])
    z[1 + bands:1 + 2 * bands] = -np.sin(f * wpos[None, :])
    max_decay = math.log(HYENA_TARGET) / HYENA_FAST_DECAY
    min_decay = math.log(HYENA_TARGET) / HYENA_SLOW_DECAY
    deltas = np.abs(np.linspace(min_decay, max_decay, BRANCH_W)).astype(np.float32)
    return z, deltas.reshape(-1, 1)


def _filter_kernel(lh, tn, z_ref, w1_ref, b1_ref, f1_ref, w2_ref, b2_ref, f2_ref, w3_ref, dl_ref, o_ref):
    z = z_ref[...]
    h1 = jnp.sin(f1_ref[...] * (jnp.dot(w1_ref[...], z, precision=HI, preferred_element_type=F32) + b1_ref[...]))
    h2 = jnp.sin(f2_ref[...] * (jnp.dot(w2_ref[...], h1, precision=HI, preferred_element_type=F32) + b2_ref[...]))
    f = jnp.dot(w3_ref[...], h2, precision=HI, preferred_element_type=F32)
    n = pl.program_id(0) * tn + lax.broadcasted_iota(jnp.int32, (1, tn), 1)
    filt = jnp.where(n >= lh - 1, f[:BRANCH_W], f[BRANCH_W:])
    win = jnp.exp(-z[0:1, :] * dl_ref[...])
    o_ref[...] = jnp.where(n == 2 * lh - 1, 0.0, filt * win)


def hyena_filter_table(lh, w1, b1, f1, w2, b2, f2, w3):
    z_np, dl_np = _filter_tables(lh)
    n2 = 2 * lh
    tn = _pick_tile(n2, (1024, 512))
    hd = w2.shape[0]
    w1t = jnp.zeros((hd, EMB_PAD), F32).at[:, :HYENA_EMB].set(w1.T)
    args = [jnp.asarray(z_np), w1t, b1.reshape(-1, 1), f1.reshape(-1, 1), w2.T, b2.reshape(-1, 1),
            f2.reshape(-1, 1), w3.T, jnp.asarray(dl_np)]

    def full(a):
        return pl.BlockSpec(a.shape, lambda j: (0, 0))

    return pl.pallas_call(
        functools.partial(_filter_kernel, lh, tn),
        out_shape=jax.ShapeDtypeStruct((BRANCH_W, n2), F32),
        grid=(n2 // tn,),
        in_specs=[pl.BlockSpec((EMB_PAD, tn), lambda j: (0, j))] + [full(a) for a in args[1:]],
        out_specs=pl.BlockSpec((BRANCH_W, tn), lambda j: (0, j)),
        compiler_params=_cparams(("parallel",)),
        name="hyena_filter",
    )(*args)


def _hyena_conv_kernel(nblk, bp, k_ref, u_ref, o_ref):
    nrow = nblk * bp

    def diag(di, acc):
        win = jnp.concatenate([k_ref[0, pl.ds(di, 1), :], k_ref[0, pl.ds(di + 1, 1), :]], axis=1)
        x = jnp.broadcast_to(win, (HY_BLK, 2 * HY_BLK))
        w = pltpu.roll(x, HY_BLK + 1, 1, stride=1, stride_axis=0)[:, :HY_BLK]
        start = pl.multiple_of((2 * (nblk - 1) - di) * bp, SUBLANES)
        lhs = u_ref[0, pl.ds(start, nrow), :]
        return acc + jnp.dot(lhs.astype(BF16), w.astype(BF16), preferred_element_type=F32)

    o_ref[0] = lax.fori_loop(0, 2 * nblk - 1, diag, jnp.zeros((nrow, HY_BLK), F32))


def hyena_conv(u, ktab):
    nb, l, c = u.shape
    nblk = l // HY_BLK
    bp = -(-nb // SUBLANES) * SUBLANES
    ut = jnp.transpose(u.reshape(nb, nblk, HY_BLK, c), (3, 1, 0, 2))
    ut = jnp.pad(ut, ((0, 0), (nblk - 1, nblk - 1), (0, bp - nb), (0, 0)))
    nrows_in = (3 * nblk - 2) * bp
    ut = ut.reshape(c, nrows_in, HY_BLK)
    k3 = ktab.reshape(c, 2 * nblk, HY_BLK)
    out = pl.pallas_call(
        functools.partial(_hyena_conv_kernel, nblk, bp),
        out_shape=jax.ShapeDtypeStruct((c, nblk * bp, HY_BLK), F32),
        grid=(c,),
        in_specs=[pl.BlockSpec((1, 2 * nblk, HY_BLK), lambda ch: (ch, 0, 0)),
                  pl.BlockSpec((1, nrows_in, HY_BLK), lambda ch: (ch, 0, 0))],
        out_specs=pl.BlockSpec((1, nblk * bp, HY_BLK), lambda ch: (ch, 0, 0)),
        compiler_params=_cparams(("parallel",)),
        name="hyena_conv",
    )(k3, ut)
    out = out.reshape(c, nblk, bp, HY_BLK)[:, :, :nb]
    return jnp.transpose(out, (2, 1, 3, 0)).reshape(nb, l, c)


def _route(logits, bias):
    s = jax.nn.sigmoid(logits)
    sel = s + bias
    srow = [s[e:e + 1] for e in range(N_EXPERTS)]
    row = [sel[e:e + 1] for e in range(N_EXPERTS)]
    best, gi = None, None
    for g in range(N_GROUPS):
        a, b, c, d = row[4 * g:4 * g + 4]
        hi1, lo1, hi2, lo2 = jnp.maximum(a, b), jnp.minimum(a, b), jnp.maximum(c, d), jnp.minimum(c, d)
        score = jnp.maximum(hi1, hi2) + jnp.maximum(jnp.minimum(hi1, hi2), jnp.maximum(lo1, lo2))
        if g == 0:
            best, gi = score, jnp.zeros(score.shape, jnp.int32)
        else:
            better = score > best
            gi = jnp.where(better, g, gi)
            best = jnp.where(better, score, best)
    neg = -jnp.inf
    msel = [jnp.where(gi == e // EXPERTS_PER_GROUP, row[e], neg) for e in range(N_EXPERTS)]

    def arg_first_max(vals):
        bv, bi = vals[0], jnp.zeros(vals[0].shape, jnp.int32)
        for e in range(1, N_EXPERTS):
            better = vals[e] > bv
            bi = jnp.where(better, e, bi)
            bv = jnp.where(better, vals[e], bv)
        return bi

    i1 = arg_first_max(msel)
    i2 = arg_first_max([jnp.where(i1 == e, neg, msel[e]) for e in range(N_EXPERTS)])
    w1 = sum(jnp.where(i1 == e, srow[e], 0.0) for e in range(N_EXPERTS))
    w2 = sum(jnp.where(i2 == e, srow[e], 0.0) for e in range(N_EXPERTS))
    den = w1 + w2
    g1, g2 = w1 / den, w2 / den
    return jnp.concatenate([jnp.where(i1 == e, g1, 0.0) + jnp.where(i2 == e, g2, 0.0)
                            for e in range(N_EXPERTS)], axis=0)


def _merge_kernel(ya_ref, x0_ref, u_ref, yc_ref, ycv_ref, yd_ref, pg_ref, x_ref, ga_ref, shf_ref, scf_ref,
                  skip_ref, wb_ref, wo_ref, g1_ref, b1_ref, rwt_ref, rb_ref, x1_o, hf_o, gates_o):
    yb = (x0_ref[0] * (yc_ref[0] + u_ref[0] * skip_ref[...])).astype(BF16)
    ys = (ya_ref[0], yb, ycv_ref[0], yd_ref[0])
    merged = None
    for n in range(N_BRANCHES):
        gate = jax.nn.sigmoid(pg_ref[0][:, n * D_MODEL:(n + 1) * D_MODEL])
        term = gate * jnp.dot(ys[n], wb_ref[n], preferred_element_type=F32)
        merged = term if merged is None else merged + term
    out = jnp.dot(merged.astype(BF16), wo_ref[...], preferred_element_type=F32)
    alpha = (2 * 2) ** 0.25
    x1 = _ln(alpha * x_ref[0] + ga_ref[0] * out) * g1_ref[...] + b1_ref[...]
    hf = _ln(x1) * (1.0 + scf_ref[0]) + shf_ref[0]
    x1_o[0] = x1
    hf_o[0] = hf.astype(BF16)
    logits = lax.dot_general(rwt_ref[...], hf, (((1,), (1,)), ((), ())), precision=HI,
                             preferred_element_type=F32)
    gates_o[...] = _route(logits, rb_ref[...])


def merge(ya, x0, u, yconv, ycv, yd, pg, x_all, mod3, nct, skip, wb, wo, ln_g, ln_b, rwt, rbias):
    nb, ta, d = x_all.shape
    nt = ta // TT
    nat = pl.BlockSpec((1, TT, BRANCH_W), lambda b, t: (b, t, 0))
    wide = pl.BlockSpec((1, TT, d), lambda b, t: (b, t, 0))

    def full(a):
        nd = a.ndim
        return pl.BlockSpec(a.shape, lambda b, t: (0,) * nd)

    consts = [skip.reshape(1, -1), wb, wo, ln_g.reshape(1, -1), ln_b.reshape(1, -1), rwt, rbias.reshape(-1, 1)]
    return pl.pallas_call(
        _merge_kernel,
        out_shape=[jax.ShapeDtypeStruct((nb, ta, d), F32), jax.ShapeDtypeStruct((nb, ta, d), BF16),
                   jax.ShapeDtypeStruct((N_EXPERTS, nb * ta), F32)],
        grid=(nb, nt),
        in_specs=[nat] * 6 + [pl.BlockSpec((1, TT, N_BRANCHES * d), lambda b, t: (b, t, 0)), wide,
                              _mod_spec(2, nct, nb), _mod_spec(3, nct, nb), _mod_spec(4, nct, nb)]
                 + [full(a) for a in consts],
        out_specs=[wide, wide, pl.BlockSpec((N_EXPERTS, TT), lambda b, t: (0, b * nt + t))],
        compiler_params=_cparams(("parallel", "parallel"), VMEM_LIMIT),
        name="merge",
    )(ya, x0, u, yconv, ycv, yd, pg, x_all, mod3, mod3, mod3, *consts)


def _moe_kernel(h_ref, g_ref, w1_ref, w3_ref, w2_ref, o_ref):
    e = pl.program_id(1)
    h = h_ref[...]
    a = jnp.dot(h, w1_ref[0], preferred_element_type=F32)
    b = jnp.dot(h, w3_ref[0], preferred_element_type=F32)
    act = (a * jax.nn.sigmoid(a)) * b
    gt = g_ref[...]
    lane = lax.broadcasted_iota(jnp.int32, gt.shape, 1)
    gcol = jnp.sum(jnp.where(lane == e, gt, 0.0), axis=1, keepdims=True)
    contrib = gcol * jnp.dot(act.astype(BF16), w2_ref[0], preferred_element_type=F32)

    @pl.when(e == 0)
    def _():
        o_ref[...] = contrib

    @pl.when(e > 0)
    def _():
        o_ref[...] += contrib


def moe_dense(hf, gates, w1, w3, w2):
    n, d = hf.shape
    ne, _, de = w1.shape
    tm = _pick_tile(n, (1024, 512, 256))
    return pl.pallas_call(
        _moe_kernel,
        out_shape=jax.ShapeDtypeStruct((n, d), F32),
        grid=(n // tm, ne),
        in_specs=[pl.BlockSpec((tm, d), lambda i, e: (i, 0)),
                  pl.BlockSpec((tm, LANES), lambda i, e: (i, 0)),
                  pl.BlockSpec((1, d, de), lambda i, e: (e, 0, 0)),
                  pl.BlockSpec((1, d, de), lambda i, e: (e, 0, 0)),
                  pl.BlockSpec((1, de, d), lambda i, e: (e, 0, 0))],
        out_specs=pl.BlockSpec((tm, d), lambda i, e: (i, 0)),
        compiler_params=_cparams(("parallel", "arbitrary"), VMEM_LIMIT),
        name="moe",
    )(hf, gates, w1, w3, w2)


def _ln2_kernel(x_ref, f_ref, gf_ref, g_ref, b_ref, o_ref):
    alpha = (2 * 2) ** 0.25
    o_ref[0] = _ln(alpha * x_ref[0] + gf_ref[0] * f_ref[0]) * g_ref[...] + b_ref[...]


def ln2(x1, f, mod3, nct, ln_g, ln_b):
    nb, ta, d = x1.shape
    wide = pl.BlockSpec((1, TT, d), lambda b, t: (b, t, 0))
    row = pl.BlockSpec((1, d), lambda b, t: (0, 0))
    return pl.pallas_call(
        _ln2_kernel,
        out_shape=jax.ShapeDtypeStruct((nb, ta, d), F32),
        grid=(nb, ta // TT),
        in_specs=[wide, wide, _mod_spec(5, nct, nb), row, row],
        out_specs=wide,
        compiler_params=_cparams(("parallel", "parallel")),
        name="ln2",
    )(x1, f, mod3, ln_g.reshape(1, -1), ln_b.reshape(1, -1))


def _rope_tables(l, lc):
    half = HEAD_DIM // 2
    inv = ROPE_THETA ** (-np.arange(0, half, 2, dtype=np.float64) / half)
    t = np.arange(l)
    rows, cols = t // GRID_W, t % GRID_W
    ang = np.concatenate([rows[:, None] * inv, cols[:, None] * inv], -1)
    ang = np.concatenate([np.zeros((lc, half)), ang], 0)
    cos = np.repeat(np.cos(ang), 2, axis=1)
    sin = np.repeat(np.sin(ang), 2, axis=1)
    sin[:, 0::2] *= -1.0
    reps = Q_W // HEAD_DIM
    return (jnp.asarray(np.tile(cos, (1, reps)), F32), jnp.asarray(np.tile(sin, (1, reps)), F32))


def _block_diag_ones():
    i = np.arange(BRANCH_W) // HEAD_DIM
    return jnp.asarray((i[:, None] == i[None, :]).astype(np.float32))


def kernel(x, c, ctx, c_ctx, ada_w, ada_b, w_in, rwkv_mu, rwkv_w0, rwkv_w_up, rwkv_a0, rwkv_a_up, rwkv_g_up, rwkv_k_k, rwkv_k_a, rwkv_r_k, rwkv_lnx_g, rwkv_lnx_b, hyena_conv, hyena_w1, hyena_b1, hyena_freq1, hyena_w2, hyena_b2, hyena_freq2, hyena_w3, hyena_skip, sconv_w, attn_q_norm, attn_k_norm, w_branch, w_out, ln1_g, ln1_b, ln2_g, ln2_b, router_w, router_bias, exp_w1, exp_w3, exp_w2):
    nb, l, d = x.shape
    lc = ctx.shape[1]
    depth = ada_w.shape[0]
    assert d == D_MODEL and lc % TT == 0 and l % TT == 0 and l % GRID_W == 0
    ta = lc + l
    nct = lc // TT
    x_all = jnp.concatenate([ctx, x], axis=1)

    mod_rows = -(-(nb + 1) // SUBLANES) * SUBLANES
    cc = jnp.zeros((mod_rows, d), F32).at[:nb].set(c).at[nb].set(c_ctx)
    mod = ada_mod(cc, ada_w, ada_b)

    cos_t, sin_t = _rope_tables(l, lc)
    bd = _block_diag_ones()
    rwt = router_w.T

    for li in range(depth):
        mod3 = mod[li].reshape(mod_rows, 1, N_MOD * d)
        wl = w_in[li].astype(BF16)
        h = lnmod(x_all, mod3, nct, 0, 1).reshape(nb * ta, d)
        p_rwkv = matmul(h, wl[:, :OFF_HYENA]).reshape(nb, ta, -1)
        p_hs = matmul(h, wl[:, OFF_HYENA:OFF_ATTN]).reshape(nb, ta, -1)
        p_attn = matmul(h, wl[:, OFF_ATTN:OFF_GATE]).reshape(nb, ta, -1)
        p_gate = matmul(h, wl[:, OFF_GATE:]).reshape(nb, ta, -1)

        r, kk, w0, w1, k0, k1, b0, b1, v, g, bon = rwkv_prep(
            p_rwkv, nct, rwkv_mu[li], rwkv_w0[li], rwkv_w_up[li], rwkv_a0[li], rwkv_a_up[li], rwkv_g_up[li],
            rwkv_k_k[li], rwkv_k_a[li], rwkv_r_k[li], bd)
        yf, yb = rwkv_scan(r, kk, v, w0, w1, k0, k1, b0, b1, lc, bd[:LANES, :LANES])
        ya = rwkv_out(yf, yb, g, bon, rwkv_lnx_g[li], rwkv_lnx_b[li], bd)

        q, kx, vx = attn_prep(p_attn, cos_t, sin_t, attn_q_norm[li], attn_k_norm[li], bd)
        yd = attention(q, kx, vx, lc)

        x0, u, ycv = hs_pre(p_hs, nct, hyena_conv[li], sconv_w[li])
        fargs = (hyena_w1[li], hyena_b1[li], hyena_freq1[li], hyena_w2[li], hyena_b2[li], hyena_freq2[li],
                 hyena_w3[li])
        yconv = jnp.concatenate([hyena_conv_seg(u[:, :lc], fargs), hyena_conv_seg(u[:, lc:], fargs)], axis=1)

        x1, hf, gates_t = merge(ya, x0, u, yconv, ycv, yd, p_gate, x_all, mod3, nct, hyena_skip[li],
                                w_branch[li].astype(BF16), w_out[li].astype(BF16), ln1_g[li], ln1_b[li],
                                rwt, router_bias)
        gates = jnp.pad(gates_t.T, ((0, 0), (0, LANES - N_EXPERTS)))
        f = moe_dense(hf.reshape(nb * ta, d), gates, exp_w1[li].astype(BF16), exp_w3[li].astype(BF16),
                      exp_w2[li].astype(BF16)).reshape(nb, ta, d)
        x_all = ln2(x1, f, mod3, nct, ln2_g[li], ln2_b[li])
    return x_all[:, lc:]


def hyena_conv_seg(u_seg, fargs):
    ktab = hyena_filter_table(u_seg.shape[1], *fargs)
    return hyena_conv(u_seg, ktab)
```

```python
import functools
import math

import numpy as np
import jax
import jax.numpy as jnp
from jax import lax
from jax.experimental import pallas as pl
from jax.experimental.pallas import tpu as pltpu

F32 = jnp.float32
BF16 = jnp.bfloat16
HI = lax.Precision.HIGHEST

D_MODEL = 1024
GRID_W = 64
BRANCH_W = 256
HEAD_DIM = 64
N_BRANCHES = 4
N_MOD = 6
RWKV_HEADS = 4
RWKV_COLS = 1024
RWKV_GN_EPS = 64e-5
HYENA_COLS = 768
HYENA_EMB = 33
HYENA_FAST_DECAY = 0.3
HYENA_SLOW_DECAY = 1.5
HYENA_TARGET = 1e-2
SCONV_COLS = 768
Q_W = 256
KV_W = 128
ATTN_COLS = 512
ROPE_THETA = 10000.0
RMS_EPS = 1e-6
OFF_HYENA = RWKV_COLS
OFF_SCONV = OFF_HYENA + HYENA_COLS
OFF_ATTN = OFF_SCONV + SCONV_COLS
OFF_GATE = OFF_ATTN + ATTN_COLS
N_EXPERTS = 16
N_GROUPS = 4
EXPERTS_PER_GROUP = 4
D_EXPERT = 512
LN_EPS = 1e-6

SUBLANES = 8
LANES = 128
TT = 256
SCAN_BLK = LANES
SCAN_GRP = 4
HY_BLK = 256
VMEM_LIMIT = 56 * 1024 * 1024


def _cparams(sem, vmem=None):
    return pltpu.CompilerParams(dimension_semantics=sem, vmem_limit_bytes=vmem)


def _ln(xf):
    mu = jnp.mean(xf, -1, keepdims=True)
    xc = xf - mu
    var = jnp.mean(xc * xc, -1, keepdims=True)
    return xc * lax.rsqrt(var + LN_EPS)


def _pick_tile(n, cands):
    for c in cands:
        if n % c == 0:
            return c
    raise ValueError(f"no tile for {n}")


def _ada_kernel(c_ref, w_ref, b_ref, o_ref):
    c = c_ref[...]
    a = c * jax.nn.sigmoid(c)
    o_ref[0] = jnp.dot(a, w_ref[0], precision=HI, preferred_element_type=F32) + b_ref[0]


def ada_mod(cc, ada_w, ada_b):
    depth, d, n = ada_w.shape
    rows = cc.shape[0]
    return pl.pallas_call(
        _ada_kernel,
        out_shape=jax.ShapeDtypeStruct((depth, rows, n), F32),
        grid=(depth, n // d),
        in_specs=[pl.BlockSpec((rows, d), lambda l, j: (0, 0)),
                  pl.BlockSpec((1, d, d), lambda l, j: (l, 0, j)),
                  pl.BlockSpec((1, 1, d), lambda l, j: (l, 0, j))],
        out_specs=pl.BlockSpec((1, rows, d), lambda l, j: (l, 0, j)),
        compiler_params=_cparams(("parallel", "parallel"), VMEM_LIMIT),
        name="ada_mod",
    )(cc, ada_w, ada_b.reshape(depth, 1, n))


def _lnmod_kernel(x_ref, sh_ref, sc_ref, o_ref):
    h = _ln(x_ref[0]) * (1.0 + sc_ref[0]) + sh_ref[0]
    o_ref[0] = h.astype(BF16)


def _mod_spec(col, nct, nb):
    return pl.BlockSpec((1, 1, D_MODEL), lambda b, t: (jnp.where(t < nct, nb, b), 0, col))


def lnmod(x_all, mod3, nct, col_shift, col_scale):
    nb, ta, d = x_all.shape
    return pl.pallas_call(
        _lnmod_kernel,
        out_shape=jax.ShapeDtypeStruct((nb, ta, d), BF16),
        grid=(nb, ta // TT),
        in_specs=[pl.BlockSpec((1, TT, d), lambda b, t: (b, t, 0)),
                  _mod_spec(col_shift, nct, nb), _mod_spec(col_scale, nct, nb)],
        out_specs=pl.BlockSpec((1, TT, d), lambda b, t: (b, t, 0)),
        compiler_params=_cparams(("parallel", "parallel")),
        name="lnmod",
    )(x_all, mod3, mod3)


def _mm_kernel(a_ref, b_ref, o_ref):
    o_ref[...] = jnp.dot(a_ref[...], b_ref[...], preferred_element_type=F32).astype(o_ref.dtype)


def matmul(a, b, out_dtype=F32):
    m, k = a.shape
    _, n = b.shape
    tm = _pick_tile(m, (1024, 512, 256))
    tn = _pick_tile(n, (1024, 512, 256))
    return pl.pallas_call(
        _mm_kernel,
        out_shape=jax.ShapeDtypeStruct((m, n), out_dtype),
        grid=(m // tm, n // tn),
        in_specs=[pl.BlockSpec((tm, k), lambda i, j: (i, 0)),
                  pl.BlockSpec((k, tn), lambda i, j: (0, j))],
        out_specs=pl.BlockSpec((tm, tn), lambda i, j: (i, j)),
        compiler_params=_cparams(("parallel", "parallel"), VMEM_LIMIT),
        name="matmul",
    )(a, b)


def _halo_specs(width, ta):
    nblk8 = ta // SUBLANES
    per = TT // SUBLANES
    cur = pl.BlockSpec((1, TT, width), lambda b, t: (b, t, 0))
    prev = pl.BlockSpec((1, SUBLANES, width), lambda b, t: (b, jnp.maximum(t * per - 1, 0), 0))
    nxt = pl.BlockSpec((1, SUBLANES, width), lambda b, t: (b, jnp.minimum((t + 1) * per, nblk8 - 1), 0))
    return [cur, prev, nxt]


def _neighbours(cur, prev_ref, next_ref, nct, nt):
    t = pl.program_id(1)
    seg_start = jnp.logical_or(t == 0, t == nct)
    seg_end = jnp.logical_or(t == nct - 1, t == nt - 1)
    prev_row = prev_ref[0][SUBLANES - 1:SUBLANES, :] * jnp.where(seg_start, 0.0, 1.0)
    next_row = next_ref[0][0:1, :] * jnp.where(seg_end, 0.0, 1.0)
    row = lax.broadcasted_iota(jnp.int32, (TT, 1), 0)
    xm1 = jnp.where(row == 0, prev_row, pltpu.roll(cur, 1, axis=0))
    xp1 = jnp.where(row == TT - 1, next_row, pltpu.roll(cur, TT - 1, axis=0))
    return xm1, xp1


def _rwkv_prep_kernel(nct, nt, cur_ref, prev_ref, next_ref, mu_ref, w0_ref, wup_ref, a0_ref, aup_ref,
                      gup_ref, kk_ref, ka_ref, rk_ref, bd_ref,
                      r_o, kk_o, w0_o, w1_o, k0_o, k1_o, b0_o, b1_o, v_o, g_o, bon_o):
    cur = cur_ref[0]
    xm1, xp1 = _neighbours(cur, prev_ref, next_ref, nct, nt)
    p = cur + mu_ref[...] * (0.5 * (xm1 + xp1) - cur)
    c = BRANCH_W
    r, k, v = p[:, 0:c], p[:, c:2 * c], p[:, 2 * c:3 * c]
    wd = p[:, 3 * c:3 * c + 64]
    ad = p[:, 3 * c + 64:3 * c + 128]
    gd = p[:, 3 * c + 128:3 * c + 256]
    bd = bd_ref[...]
    kk = k * kk_ref[...]
    ss = jnp.dot(kk * kk, bd, precision=HI, preferred_element_type=F32)
    kkn = kk * lax.rsqrt(jnp.maximum(ss, 1e-24))
    twd = jnp.tanh(wd)
    ka = ka_ref[...]
    kdirs = []
    w_outs, k_outs, b_outs = (w0_o, w1_o), (k0_o, k1_o), (b0_o, b1_o)
    for d in range(2):
        wlog = w0_ref[d:d + 1, :] + jnp.dot(twd, wup_ref[d], precision=HI, preferred_element_type=F32)
        decay = jnp.exp(-math.exp(-0.5) * jax.nn.sigmoid(wlog))
        a = jax.nn.sigmoid(a0_ref[d:d + 1, :] + jnp.dot(ad, aup_ref[d], precision=HI, preferred_element_type=F32))
        kdir = k * (1.0 + (a - 1.0) * ka)
        bdir = kkn * a
        kdirs.append(kdir)
        w_outs[d][0] = decay
        k_outs[d][0] = kdir
        b_outs[d][0] = bdir
    r_o[0] = r
    kk_o[0] = kkn
    v_o[0] = v
    g_o[0] = jnp.dot(jax.nn.sigmoid(gd), gup_ref[...], precision=HI, preferred_element_type=F32)
    rkk = r * rk_ref[...] * (kdirs[0] + kdirs[1])
    bon_o[0] = jnp.dot(rkk, bd, precision=HI, preferred_element_type=F32) * v


def rwkv_prep(p_rwkv, nct, mu, w0, w_up, a0, a_up, g_up, k_k, k_a, r_k, bd):
    nb, ta, _ = p_rwkv.shape
    nt = ta // TT
    c = BRANCH_W
    nat = jax.ShapeDtypeStruct((nb, ta, c), F32)
    nat_spec = pl.BlockSpec((1, TT, c), lambda b, t: (b, t, 0))

    def full(a):
        nd = a.ndim
        return pl.BlockSpec(a.shape, lambda b, t: (0,) * nd)

    consts = [mu.reshape(1, -1), w0, w_up, a0, a_up, g_up, k_k.reshape(1, -1), k_a.reshape(1, -1),
              r_k.reshape(1, -1), bd]
    return pl.pallas_call(
        functools.partial(_rwkv_prep_kernel, nct, nt),
        out_shape=[nat] * 11,
        grid=(nb, nt),
        in_specs=_halo_specs(RWKV_COLS, ta) + [full(a) for a in consts],
        out_specs=[nat_spec] * 11,
        compiler_params=_cparams(("parallel", "parallel"), VMEM_LIMIT),
        name="rwkv_prep",
    )(p_rwkv, p_rwkv, p_rwkv, *consts)


def _scan_kernel(nb, rf, kkf, wf, kf, bf, vf, rb, kkb, wb, kb, bb, vb, bd_ref, yf_o, yb_o,
                 s_scr, vt_scr, yt_scr):
    step = pl.program_id(0)
    npair = BRANCH_W // LANES
    nsub = SCAN_BLK // HEAD_DIM

    @pl.when(step == 0)
    def _():
        s_scr[...] = jnp.zeros_like(s_scr)

    for d, vref in ((0, vf), (1, vb)):
        for b in range(nb):
            vblk = vref[b]
            for p in range(npair):
                vt = vblk[:, p * LANES:(p + 1) * LANES].T
                for sb in range(nsub):
                    tk = slice(sb * HEAD_DIM, (sb + 1) * HEAD_DIM)
                    vt_scr[d, b, p, sb] = jnp.concatenate([vt[:HEAD_DIM, tk], vt[HEAD_DIM:, tk]], axis=1)

    bd = bd_ref[...]
    bd_lo = bd.astype(BF16)
    lane_tok = lax.broadcasted_iota(jnp.int32, (HEAD_DIM, LANES), 1) % HEAD_DIM
    rows = ((rf, kkf, wf, kf, bf), (rb, kkb, wb, kb, bb))

    def seg_sum(parts):
        return jnp.dot(jnp.concatenate(parts, axis=0).astype(BF16), bd_lo, preferred_element_type=F32)

    def half(hb, carry):
        sub = (hb, nsub - 1 - hb)

        def group(g, carry):
            for j in range(SCAN_GRP):
                tf = g * SCAN_GRP + j
                chains, t1, xv = [], [], []
                for d in range(2):
                    tl = tf if d == 0 else HEAD_DIM - 1 - tf
                    t = sub[d] * HEAD_DIM + tl
                    tok = lane_tok == tl
                    for b in range(nb):
                        row5 = [ref[b, pl.ds(t, 1), :] for ref in rows[d]]
                        for p in range(npair):
                            sl = slice(p * LANES, (p + 1) * LANES)
                            r_row, kk_row, w_row, k_row, b_row = (a[:, sl] for a in row5)
                            s = s_scr[d, b, p]
                            t1.append(s * kk_row)
                            xv.append(jnp.where(tok, vt_scr[d, b, p, sub[d]], 0.0))
                            chains.append((d, b, p, tok, s, r_row, w_row, k_row, b_row))
                skk_all = seg_sum(t1)
                vcol_all = jnp.dot(jnp.concatenate(xv, axis=0), bd, preferred_element_type=F32)
                ylhs = []
                for i, (d, b, p, tok, s, r_row, w_row, k_row, b_row) in enumerate(chains):
                    rs = slice(i * HEAD_DIM, (i + 1) * HEAD_DIM)
                    s = s * w_row - skk_all[rs] * b_row + vcol_all[rs] * k_row
                    s_scr[d, b, p] = s
                    ylhs.append(s * r_row)
                yres = seg_sum(ylhs)
                for i, (d, b, p, tok, *_) in enumerate(chains):
                    pltpu.store(yt_scr.at[d, b, p, sub[d]], yres[i * HEAD_DIM:(i + 1) * HEAD_DIM], mask=tok)
            return carry

        return lax.fori_loop(0, HEAD_DIM // SCAN_GRP, group, carry)

    lax.fori_loop(0, nsub, half, 0)

    for d, yref in ((0, yf_o), (1, yb_o)):
        for b in range(nb):
            for p in range(npair):
                heads = [jnp.concatenate([yt_scr[d, b, p, sb][:, h * HEAD_DIM:(h + 1) * HEAD_DIM]
                                          for sb in range(nsub)], axis=1) for h in range(2)]
                yref[b, :, p * LANES:(p + 1) * LANES] = jnp.concatenate(heads, axis=0).T


def rwkv_scan(r, kk, v, w0, w1, k0, k1, b0, b1, lc, bd_pair):
    nb, ta, c = r.shape
    nblk = ta // SCAN_BLK
    nctb = lc // SCAN_BLK
    npair = c // LANES
    nsub = SCAN_BLK // HEAD_DIM

    def fwd(s):
        return s

    def bwd(s):
        return jnp.where(s < nctb, nctb - 1 - s, nblk - 1 - (s - nctb))

    def nat(idx):
        return pl.BlockSpec((nb, SCAN_BLK, c), lambda s: (0, idx(s), 0))

    out = jax.ShapeDtypeStruct((nb, ta, c), F32)
    tiles = pltpu.VMEM((2, nb, npair, nsub, HEAD_DIM, LANES), F32)
    return pl.pallas_call(
        functools.partial(_scan_kernel, nb),
        out_shape=[out, out],
        grid=(nblk,),
        in_specs=[nat(fwd)] * 6 + [nat(bwd)] * 6 + [pl.BlockSpec(bd_pair.shape, lambda s: (0, 0))],
        out_specs=[nat(fwd), nat(bwd)],
        scratch_shapes=[pltpu.VMEM((2, nb, npair, HEAD_DIM, LANES), F32), tiles, tiles],
        compiler_params=_cparams(("arbitrary",), VMEM_LIMIT),
        name="rwkv_scan",
    )(r, kk, w0, k0, b0, v, r, kk, w1, k1, b1, v, bd_pair)


def _rwkv_out_kernel(yf_ref, yb_ref, g_ref, bon_ref, lg_ref, lb_ref, bd_ref, o_ref):
    y = yf_ref[0] + yb_ref[0]
    bd = bd_ref[...]
    mu = jnp.dot(y, bd, precision=HI, preferred_element_type=F32) * (1.0 / HEAD_DIM)
    yc = y - mu
    var = jnp.dot(yc * yc, bd, precision=HI, preferred_element_type=F32) * (1.0 / HEAD_DIM)
    yn = yc * lax.rsqrt(var + RWKV_GN_EPS) * lg_ref[...] + lb_ref[...]
    o_ref[0] = ((yn + bon_ref[0]) * g_ref[0]).astype(BF16)


def rwkv_out(yf, yb, g, bon, lnx_g, lnx_b, bd):
    nb, ta, _ = yf.shape
    nat_spec = pl.BlockSpec((1, TT, BRANCH_W), lambda b, t: (b, t, 0))
    row = pl.BlockSpec((1, BRANCH_W), lambda b, t: (0, 0))
    return pl.pallas_call(
        _rwkv_out_kernel,
        out_shape=jax.ShapeDtypeStruct((nb, ta, BRANCH_W), BF16),
        grid=(nb, ta // TT),
        in_specs=[nat_spec] * 4 + [row, row, pl.BlockSpec(bd.shape, lambda b, t: (0, 0))],
        out_specs=nat_spec,
        compiler_params=_cparams(("parallel", "parallel")),
        name="rwkv_out",
    )(yf, yb, g, bon, lnx_g.reshape(1, -1), lnx_b.reshape(1, -1), bd)


def _pair_swap(x):
    lane = lax.broadcasted_iota(jnp.int32, x.shape, 1)
    n = x.shape[1]
    return jnp.where(lane % 2 == 0, pltpu.roll(x, n - 1, axis=1), pltpu.roll(x, 1, axis=1))


def _attn_prep_kernel(p_ref, cos_ref, sin_ref, qg_ref, kg_ref, bd_ref, q_o, k_o, v_o):
    p = p_ref[0]
    q, k, v = p[:, :Q_W], p[:, Q_W:Q_W + KV_W], p[:, Q_W + KV_W:]
    bd = bd_ref[...]
    cos, sin = cos_ref[...], sin_ref[...]
    qms = jnp.dot(q * q, bd, precision=HI, preferred_element_type=F32) * (1.0 / HEAD_DIM)
    qn = q * lax.rsqrt(qms + RMS_EPS) * qg_ref[...]
    qr = qn * cos + _pair_swap(qn) * sin
    q_o[0] = (qr * HEAD_DIM ** -0.5).astype(BF16)
    kms = jnp.dot(k * k, bd[:KV_W, :KV_W], precision=HI, preferred_element_type=F32) * (1.0 / HEAD_DIM)
    kn = k * lax.rsqrt(kms + RMS_EPS) * kg_ref[...]
    kr = kn * cos[:, :KV_W] + _pair_swap(kn) * sin[:, :KV_W]
    for g in range(KV_W // HEAD_DIM):
        sl = slice(g * HEAD_DIM, (g + 1) * HEAD_DIM)
        k_o[0, g] = kr[:, sl].astype(BF16)
        v_o[0, g] = v[:, sl].astype(BF16)


def attn_prep(p_attn, cos_t, sin_t, q_norm, k_norm, bd):
    nb, ta, _ = p_attn.shape
    ng = KV_W // HEAD_DIM
    qg = jnp.tile(q_norm, Q_W // HEAD_DIM).reshape(1, -1)
    kg = jnp.tile(k_norm, ng).reshape(1, -1)
    kv_shape = jax.ShapeDtypeStruct((nb, ng, ta, HEAD_DIM), BF16)
    kv_spec = pl.BlockSpec((1, ng, TT, HEAD_DIM), lambda b, t: (b, 0, t, 0))
    return pl.pallas_call(
        _attn_prep_kernel,
        out_shape=[jax.ShapeDtypeStruct((nb, ta, Q_W), BF16), kv_shape, kv_shape],
        grid=(nb, ta // TT),
        in_specs=[pl.BlockSpec((1, TT, ATTN_COLS), lambda b, t: (b, t, 0)),
                  pl.BlockSpec((TT, Q_W), lambda b, t: (t, 0)),
                  pl.BlockSpec((TT, Q_W), lambda b, t: (t, 0)),
                  pl.BlockSpec((1, Q_W), lambda b, t: (0, 0)),
                  pl.BlockSpec((1, KV_W), lambda b, t: (0, 0)),
                  pl.BlockSpec(bd.shape, lambda b, t: (0, 0))],
        out_specs=[pl.BlockSpec((1, TT, Q_W), lambda b, t: (b, t, 0)), kv_spec, kv_spec],
        compiler_params=_cparams(("parallel", "parallel")),
        name="attn_prep",
    )(p_attn, cos_t, sin_t, qg, kg, bd)


def _attn_kernel(nct, lc, q_ref, k_ref, v_ref, o_ref):
    t = pl.program_id(2)

    def run(kk, vv):
        outs = []
        for r in range(2):
            q = q_ref[0][:, r * HEAD_DIM:(r + 1) * HEAD_DIM]
            s = lax.dot_general(q, kk, (((1,), (1,)), ((), ())), preferred_element_type=F32)
            p = jnp.exp(s - jnp.max(s, -1, keepdims=True))
            l = jnp.sum(p, -1, keepdims=True)
            o = jnp.dot(p.astype(BF16), vv, preferred_element_type=F32)
            outs.append(o / l)
        o_ref[0] = jnp.concatenate(outs, axis=1).astype(BF16)

    @pl.when(t < nct)
    def _():
        run(k_ref[0, 0, :lc, :], v_ref[0, 0, :lc, :])

    @pl.when(t >= nct)
    def _():
        run(k_ref[0, 0], v_ref[0, 0])


def attention(q, k, v, lc):
    nb, ta, _ = q.shape
    ng = k.shape[1]
    nct = lc // TT
    qo_spec = pl.BlockSpec((1, TT, 2 * HEAD_DIM), lambda b, g, t: (b, t, g))
    kv_spec = pl.BlockSpec((1, 1, ta, HEAD_DIM), lambda b, g, t: (b, g, 0, 0))
    return pl.pallas_call(
        functools.partial(_attn_kernel, nct, lc),
        out_shape=jax.ShapeDtypeStruct((nb, ta, Q_W), BF16),
        grid=(nb, ng, ta // TT),
        in_specs=[qo_spec, kv_spec, kv_spec],
        out_specs=qo_spec,
        compiler_params=_cparams(("parallel", "parallel", "arbitrary"), VMEM_LIMIT),
        name="attention",
    )(q, k, v)


def _hs_pre_kernel(nct, nt, cur_ref, prev_ref, next_ref, hw_ref, sw_ref, x0_o, u_o, ycv_o):
    cur = cur_ref[0]
    xm1, xp1 = _neighbours(cur, prev_ref, next_ref, nct, nt)
    c = BRANCH_W
    hc = HYENA_COLS
    hw = hw_ref[...]
    ph = hw[0:1] * xm1[:, :hc] + hw[1:2] * cur[:, :hc] + hw[2:3] * xp1[:, :hc]
    x0_o[0] = ph[:, :c]
    u_o[0] = ph[:, c:2 * c] * ph[:, 2 * c:3 * c]
    sw = sw_ref[...]

    def cx(a):
        return a[:, hc + c:hc + 2 * c] * a[:, hc + 2 * c:hc + 3 * c]

    conv = sw[0:1] * cx(xm1) + sw[1:2] * cx(cur) + sw[2:3] * cx(xp1)
    ycv_o[0] = (cur[:, hc:hc + c] * conv).astype(BF16)


def hs_pre(p_hs, nct, hyena_conv, sconv_w):
    nb, ta, w = p_hs.shape
    nt = ta // TT
    nat = pl.BlockSpec((1, TT, BRANCH_W), lambda b, t: (b, t, 0))
    return pl.pallas_call(
        functools.partial(_hs_pre_kernel, nct, nt),
        out_shape=[jax.ShapeDtypeStruct((nb, ta, BRANCH_W), F32),
                   jax.ShapeDtypeStruct((nb, ta, BRANCH_W), F32),
                   jax.ShapeDtypeStruct((nb, ta, BRANCH_W), BF16)],
        grid=(nb, nt),
        in_specs=_halo_specs(w, ta) + [pl.BlockSpec(hyena_conv.shape, lambda b, t: (0, 0)),
                                       pl.BlockSpec(sconv_w.shape, lambda b, t: (0, 0))],
        out_specs=[nat, nat, nat],
        compiler_params=_cparams(("parallel", "parallel"), VMEM_LIMIT),
        name="hs_pre",
    )(p_hs, p_hs, p_hs, hyena_conv, sconv_w)


EMB_PAD = 40


def _filter_tables(lh):
    n = np.arange(2 * lh)
    pos = np.abs(n - (lh - 1)).astype(np.float64)
    bands = (HYENA_EMB - 1) // 2
    t = np.minimum(pos, lh - 1) / (lh - 1)
    wpos = 2.0 * math.pi * pos / lh
    f = np.linspace(1e-4, bands - 1, bands)[:, None]
    z = np.zeros((EMB_PAD, 2 * lh), np.float32)
    z[0] = t
    z[1:1 + bands] = np.cos(f * wpos[None, :])
    z[1 + bands:1 + 2 * bands] = -np.sin(f * wpos[None, :])
    max_decay = math.log(HYENA_TARGET) / HYENA_FAST_DECAY
    min_decay = math.log(HYENA_TARGET) / HYENA_SLOW_DECAY
    deltas = np.abs(np.linspace(min_decay, max_decay, BRANCH_W)).astype(np.float32)
    return z, deltas.reshape(-1, 1)


def _filter_kernel(lh, tn, z_ref, w1_ref, b1_ref, f1_ref, w2_ref, b2_ref, f2_ref, w3_ref, dl_ref, o_ref):
    z = z_ref[...]
    h1 = jnp.sin(f1_ref[...] * (jnp.dot(w1_ref[...], z, precision=HI, preferred_element_type=F32) + b1_ref[...]))
    h2 = jnp.sin(f2_ref[...] * (jnp.dot(w2_ref[...], h1, precision=HI, preferred_element_type=F32) + b2_ref[...]))
    f = jnp.dot(w3_ref[...], h2, precision=HI, preferred_element_type=F32)
    n = pl.program_id(0) * tn + lax.broadcasted_iota(jnp.int32, (1, tn), 1)
    filt = jnp.where(n >= lh - 1, f[:BRANCH_W], f[BRANCH_W:])
    win = jnp.exp(-z[0:1, :] * dl_ref[...])
    o_ref[...] = jnp.where(n == 2 * lh - 1, 0.0, filt * win)


def hyena_filter_table(lh, w1, b1, f1, w2, b2, f2, w3):
    z_np, dl_np = _filter_tables(lh)
    n2 = 2 * lh
    tn = _pick_tile(n2, (1024, 512))
    hd = w2.shape[0]
    w1t = jnp.zeros((hd, EMB_PAD), F32).at[:, :HYENA_EMB].set(w1.T)
    args = [jnp.asarray(z_np), w1t, b1.reshape(-1, 1), f1.reshape(-1, 1), w2.T, b2.reshape(-1, 1),
            f2.reshape(-1, 1), w3.T, jnp.asarray(dl_np)]

    def full(a):
        return pl.BlockSpec(a.shape, lambda j: (0, 0))

    return pl.pallas_call(
        functools.partial(_filter_kernel, lh, tn),
        out_shape=jax.ShapeDtypeStruct((BRANCH_W, n2), F32),
        grid=(n2 // tn,),
        in_specs=[pl.BlockSpec((EMB_PAD, tn), lambda j: (0, j))] + [full(a) for a in args[1:]],
        out_specs=pl.BlockSpec((BRANCH_W, tn), lambda j: (0, j)),
        compiler_params=_cparams(("parallel",)),
        name="hyena_filter",
    )(*args)


def _hyena_conv_kernel(nblk, bp, k_ref, u_ref, o_ref):
    nrow = nblk * bp

    def diag(di, acc):
        win = jnp.concatenate([k_ref[0, pl.ds(di, 1), :], k_ref[0, pl.ds(di + 1, 1), :]], axis=1)
        x = jnp.broadcast_to(win, (HY_BLK, 2 * HY_BLK))
        w = pltpu.roll(x, HY_BLK + 1, 1, stride=1, stride_axis=0)[:, :HY_BLK]
        start = pl.multiple_of((2 * (nblk - 1) - di) * bp, SUBLANES)
        lhs = u_ref[0, pl.ds(start, nrow), :]
        return acc + jnp.dot(lhs.astype(BF16), w.astype(BF16), preferred_element_type=F32)

    o_ref[0] = lax.fori_loop(0, 2 * nblk - 1, diag, jnp.zeros((nrow, HY_BLK), F32))


def hyena_conv(u, ktab):
    nb, l, c = u.shape
    nblk = l // HY_BLK
    bp = -(-nb // SUBLANES) * SUBLANES
    ut = jnp.transpose(u.reshape(nb, nblk, HY_BLK, c), (3, 1, 0, 2))
    ut = jnp.pad(ut, ((0, 0), (nblk - 1, nblk - 1), (0, bp - nb), (0, 0)))
    nrows_in = (3 * nblk - 2) * bp
    ut = ut.reshape(c, nrows_in, HY_BLK)
    k3 = ktab.reshape(c, 2 * nblk, HY_BLK)
    out = pl.pallas_call(
        functools.partial(_hyena_conv_kernel, nblk, bp),
        out_shape=jax.ShapeDtypeStruct((c, nblk * bp, HY_BLK), F32),
        grid=(c,),
        in_specs=[pl.BlockSpec((1, 2 * nblk, HY_BLK), lambda ch: (ch, 0, 0)),
                  pl.BlockSpec((1, nrows_in, HY_BLK), lambda ch: (ch, 0, 0))],
        out_specs=pl.BlockSpec((1, nblk * bp, HY_BLK), lambda ch: (ch, 0, 0)),
        compiler_params=_cparams(("parallel",)),
        name="hyena_conv",
    )(k3, ut)
    out = out.reshape(c, nblk, bp, HY_BLK)[:, :, :nb]
    return jnp.transpose(out, (2, 1, 3, 0)).reshape(nb, l, c)


def _route(logits, bias):
    s = jax.nn.sigmoid(logits)
    sel = s + bias
    srow = [s[e:e + 1] for e in range(N_EXPERTS)]
    row = [sel[e:e + 1] for e in range(N_EXPERTS)]
    best, gi = None, None
    for g in range(N_GROUPS):
        a, b, c, d = row[4 * g:4 * g + 4]
        hi1, lo1, hi2, lo2 = jnp.maximum(a, b), jnp.minimum(a, b), jnp.maximum(c, d), jnp.minimum(c, d)
        score = jnp.maximum(hi1, hi2) + jnp.maximum(jnp.minimum(hi1, hi2), jnp.maximum(lo1, lo2))
        if g == 0:
            best, gi = score, jnp.zeros(score.shape, jnp.int32)
        else:
            better = score > best
            gi = jnp.where(better, g, gi)
            best = jnp.where(better, score, best)
    neg = -jnp.inf
    msel = [jnp.where(gi == e // EXPERTS_PER_GROUP, row[e], neg) for e in range(N_EXPERTS)]

    def arg_first_max(vals):
        bv, bi = vals[0], jnp.zeros(vals[0].shape, jnp.int32)
        for e in range(1, N_EXPERTS):
            better = vals[e] > bv
            bi = jnp.where(better, e, bi)
            bv = jnp.where(better, vals[e], bv)
        return bi

    i1 = arg_first_max(msel)
    i2 = arg_first_max([jnp.where(i1 == e, neg, msel[e]) for e in range(N_EXPERTS)])
    w1 = sum(jnp.where(i1 == e, srow[e], 0.0) for e in range(N_EXPERTS))
    w2 = sum(jnp.where(i2 == e, srow[e], 0.0) for e in range(N_EXPERTS))
    den = w1 + w2
    g1, g2 = w1 / den, w2 / den
    return jnp.concatenate([jnp.where(i1 == e, g1, 0.0) + jnp.where(i2 == e, g2, 0.0)
                            for e in range(N_EXPERTS)], axis=0)


def _merge_kernel(ya_ref, x0_ref, u_ref, yc_ref, ycv_ref, yd_ref, pg_ref, x_ref, ga_ref, shf_ref, scf_ref,
                  skip_ref, wb_ref, wo_ref, g1_ref, b1_ref, rwt_ref, rb_ref, x1_o, hf_o, gates_o):
    yb = (x0_ref[0] * (yc_ref[0] + u_ref[0] * skip_ref[...])).astype(BF16)
    ys = (ya_ref[0], yb, ycv_ref[0], yd_ref[0])
    merged = None
    for n in range(N_BRANCHES):
        gate = jax.nn.sigmoid(pg_ref[0][:, n * D_MODEL:(n + 1) * D_MODEL])
        term = gate * jnp.dot(ys[n], wb_ref[n], preferred_element_type=F32)
        merged = term if merged is None else merged + term
    out = jnp.dot(merged.astype(BF16), wo_ref[...], preferred_element_type=F32)
    alpha = (2 * 2) ** 0.25
    x1 = _ln(alpha * x_ref[0] + ga_ref[0] * out) * g1_ref[...] + b1_ref[...]
    hf = _ln(x1) * (1.0 + scf_ref[0]) + shf_ref[0]
    x1_o[0] = x1
    hf_o[0] = hf.astype(BF16)
    logits = lax.dot_general(rwt_ref[...], hf, (((1,), (1,)), ((), ())), precision=HI,
                             preferred_element_type=F32)
    gates_o[...] = _route(logits, rb_ref[...])


def merge(ya, x0, u, yconv, ycv, yd, pg, x_all, mod3, nct, skip, wb, wo, ln_g, ln_b, rwt, rbias):
    nb, ta, d = x_all.shape
    nt = ta // TT
    nat = pl.BlockSpec((1, TT, BRANCH_W), lambda b, t: (b, t, 0))
    wide = pl.BlockSpec((1, TT, d), lambda b, t: (b, t, 0))

    def full(a):
        nd = a.ndim
        return pl.BlockSpec(a.shape, lambda b, t: (0,) * nd)

    consts = [skip.reshape(1, -1), wb, wo, ln_g.reshape(1, -1), ln_b.reshape(1, -1), rwt, rbias.reshape(-1, 1)]
    return pl.pallas_call(
        _merge_kernel,
        out_shape=[jax.ShapeDtypeStruct((nb, ta, d), F32), jax.ShapeDtypeStruct((nb, ta, d), BF16),
                   jax.ShapeDtypeStruct((N_EXPERTS, nb * ta), F32)],
        grid=(nb, nt),
        in_specs=[nat] * 6 + [pl.BlockSpec((1, TT, N_BRANCHES * d), lambda b, t: (b, t, 0)), wide,
                              _mod_spec(2, nct, nb), _mod_spec(3, nct, nb), _mod_spec(4, nct, nb)]
                 + [full(a) for a in consts],
        out_specs=[wide, wide, pl.BlockSpec((N_EXPERTS, TT), lambda b, t: (0, b * nt + t))],
        compiler_params=_cparams(("parallel", "parallel"), VMEM_LIMIT),
        name="merge",
    )(ya, x0, u, yconv, ycv, yd, pg, x_all, mod3, mod3, mod3, *consts)


def _moe_kernel(h_ref, g_ref, w1_ref, w3_ref, w2_ref, o_ref):
    e = pl.program_id(1)
    h = h_ref[...]
    a = jnp.dot(h, w1_ref[0], preferred_element_type=F32)
    b = jnp.dot(h, w3_ref[0], preferred_element_type=F32)
    act = (a * jax.nn.sigmoid(a)) * b
    gt = g_ref[...]
    lane = lax.broadcasted_iota(jnp.int32, gt.shape, 1)
    gcol = jnp.sum(jnp.where(lane == e, gt, 0.0), axis=1, keepdims=True)
    contrib = gcol * jnp.dot(act.astype(BF16), w2_ref[0], preferred_element_type=F32)

    @pl.when(e == 0)
    def _():
        o_ref[...] = contrib

    @pl.when(e > 0)
    def _():
        o_ref[...] += contrib


def moe_dense(hf, gates, w1, w3, w2):
    n, d = hf.shape
    ne, _, de = w1.shape
    tm = _pick_tile(n, (1024, 512, 256))
    return pl.pallas_call(
        _moe_kernel,
        out_shape=jax.ShapeDtypeStruct((n, d), F32),
        grid=(n // tm, ne),
        in_specs=[pl.BlockSpec((tm, d), lambda i, e: (i, 0)),
                  pl.BlockSpec((tm, LANES), lambda i, e: (i, 0)),
                  pl.BlockSpec((1, d, de), lambda i, e: (e, 0, 0)),
                  pl.BlockSpec((1, d, de), lambda i, e: (e, 0, 0)),
                  pl.BlockSpec((1, de, d), lambda i, e: (e, 0, 0))],
        out_specs=pl.BlockSpec((tm, d), lambda i, e: (i, 0)),
        compiler_params=_cparams(("parallel", "arbitrary"), VMEM_LIMIT),
        name="moe",
    )(hf, gates, w1, w3, w2)


def _ln2_kernel(x_ref, f_ref, gf_ref, g_ref, b_ref, o_ref):
    alpha = (2 * 2) ** 0.25
    o_ref[0] = _ln(alpha * x_ref[0] + gf_ref[0] * f_ref[0]) * g_ref[...] + b_ref[...]


def ln2(x1, f, mod3, nct, ln_g, ln_b):
    nb, ta, d = x1.shape
    wide = pl.BlockSpec((1, TT, d), lambda b, t: (b, t, 0))
    row = pl.BlockSpec((1, d), lambda b, t: (0, 0))
    return pl.pallas_call(
        _ln2_kernel,
        out_shape=jax.ShapeDtypeStruct((nb, ta, d), F32),
        grid=(nb, ta // TT),
        in_specs=[wide, wide, _mod_spec(5, nct, nb), row, row],
        out_specs=wide,
        compiler_params=_cparams(("parallel", "parallel")),
        name="ln2",
    )(x1, f, mod3, ln_g.reshape(1, -1), ln_b.reshape(1, -1))


def _rope_tables(l, lc):
    half = HEAD_DIM // 2
    inv = ROPE_THETA ** (-np.arange(0, half, 2, dtype=np.float64) / half)
    t = np.arange(l)
    rows, cols = t // GRID_W, t % GRID_W
    ang = np.concatenate([rows[:, None] * inv, cols[:, None] * inv], -1)
    ang = np.concatenate([np.zeros((lc, half)), ang], 0)
    cos = np.repeat(np.cos(ang), 2, axis=1)
    sin = np.repeat(np.sin(ang), 2, axis=1)
    sin[:, 0::2] *= -1.0
    reps = Q_W // HEAD_DIM
    return (jnp.asarray(np.tile(cos, (1, reps)), F32), jnp.asarray(np.tile(sin, (1, reps)), F32))


def _block_diag_ones():
    i = np.arange(BRANCH_W) // HEAD_DIM
    return jnp.asarray((i[:, None] == i[None, :]).astype(np.float32))


def kernel(x, c, ctx, c_ctx, ada_w, ada_b, w_in, rwkv_mu, rwkv_w0, rwkv_w_up, rwkv_a0, rwkv_a_up, rwkv_g_up, rwkv_k_k, rwkv_k_a, rwkv_r_k, rwkv_lnx_g, rwkv_lnx_b, hyena_conv, hyena_w1, hyena_b1, hyena_freq1, hyena_w2, hyena_b2, hyena_freq2, hyena_w3, hyena_skip, sconv_w, attn_q_norm, attn_k_norm, w_branch, w_out, ln1_g, ln1_b, ln2_g, ln2_b, router_w, router_bias, exp_w1, exp_w3, exp_w2):
    nb, l, d = x.shape
    lc = ctx.shape[1]
    depth = ada_w.shape[0]
    assert d == D_MODEL and lc % TT == 0 and l % TT == 0 and l % GRID_W == 0
    ta = lc + l
    nct = lc // TT
    x_all = jnp.concatenate([ctx, x], axis=1)

    mod_rows = -(-(nb + 1) // SUBLANES) * SUBLANES
    cc = jnp.zeros((mod_rows, d), F32).at[:nb].set(c).at[nb].set(c_ctx)
    mod = ada_mod(cc, ada_w, ada_b)

    cos_t, sin_t = _rope_tables(l, lc)
    bd = _block_diag_ones()
    rwt = router_w.T

    for li in range(depth):
        mod3 = mod[li].reshape(mod_rows, 1, N_MOD * d)
        wl = w_in[li].astype(BF16)
        h = lnmod(x_all, mod3, nct, 0, 1).reshape(nb * ta, d)
        p_rwkv = matmul(h, wl[:, :OFF_HYENA]).reshape(nb, ta, -1)
        p_hs = matmul(h, wl[:, OFF_HYENA:OFF_ATTN]).reshape(nb, ta, -1)
        p_attn = matmul(h, wl[:, OFF_ATTN:OFF_GATE]).reshape(nb, ta, -1)
        p_gate = matmul(h, wl[:, OFF_GATE:]).reshape(nb, ta, -1)

        r, kk, w0, w1, k0, k1, b0, b1, v, g, bon = rwkv_prep(
            p_rwkv, nct, rwkv_mu[li], rwkv_w0[li], rwkv_w_up[li], rwkv_a0[li], rwkv_a_up[li], rwkv_g_up[li],
            rwkv_k_k[li], rwkv_k_a[li], rwkv_r_k[li], bd)
        yf, yb = rwkv_scan(r, kk, v, w0, w1, k0, k1, b0, b1, lc, bd[:LANES, :LANES])
        ya = rwkv_out(yf, yb, g, bon, rwkv_lnx_g[li], rwkv_lnx_b[li], bd)

        q, kx, vx = attn_prep(p_attn, cos_t, sin_t, attn_q_norm[li], attn_k_norm[li], bd)
        yd = attention(q, kx, vx, lc)

        x0, u, ycv = hs_pre(p_hs, nct, hyena_conv[li], sconv_w[li])
        fargs = (hyena_w1[li], hyena_b1[li], hyena_freq1[li], hyena_w2[li], hyena_b2[li], hyena_freq2[li],
                 hyena_w3[li])
        yconv = jnp.concatenate([hyena_conv_seg(u[:, :lc], fargs), hyena_conv_seg(u[:, lc:], fargs)], axis=1)

        x1, hf, gates_t = merge(ya, x0, u, yconv, ycv, yd, p_gate, x_all, mod3, nct, hyena_skip[li],
                                w_branch[li].astype(BF16), w_out[li].astype(BF16), ln1_g[li], ln1_b[li],
                                rwt, router_bias)
        gates = jnp.pad(gates_t.T, ((0, 0), (0, LANES - N_EXPERTS)))
        f = moe_dense(hf.reshape(nb * ta, d), gates, exp_w1[li].astype(BF16), exp_w3[li].astype(BF16),
                      exp_w2[li].astype(BF16)).reshape(nb, ta, d)
        x_all = ln2(x1, f, mod3, nct, ln2_g[li], ln2_b[li])
    return x_all[:, lc:]


def hyena_conv_seg(u_seg, fargs):
    ktab = hyena_filter_table(u_seg.shape[1], *fargs)
    return hyena_conv(u_seg, ktab)
```

```python
import functools
import math

import numpy as np
import jax
import jax.numpy as jnp
from jax import lax
from jax.experimental import pallas as pl
from jax.experimental.pallas import tpu as pltpu

F32 = jnp.float32
BF16 = jnp.bfloat16
HI = lax.Precision.HIGHEST

D_MODEL = 1024
GRID_W = 64
BRANCH_W = 256
HEAD_DIM = 64
N_BRANCHES = 4
N_MOD = 6
RWKV_HEADS = 4
RWKV_COLS = 1024
RWKV_GN_EPS = 64e-5
HYENA_COLS = 768
HYENA_EMB = 33
HYENA_FAST_DECAY = 0.3
HYENA_SLOW_DECAY = 1.5
HYENA_TARGET = 1e-2
SCONV_COLS = 768
Q_W = 256
KV_W = 128
ATTN_COLS = 512
ROPE_THETA = 10000.0
RMS_EPS = 1e-6
OFF_HYENA = RWKV_COLS
OFF_SCONV = OFF_HYENA + HYENA_COLS
OFF_ATTN = OFF_SCONV + SCONV_COLS
OFF_GATE = OFF_ATTN + ATTN_COLS
N_EXPERTS = 16
N_GROUPS = 4
EXPERTS_PER_GROUP = 4
D_EXPERT = 512
LN_EPS = 1e-6

SUBLANES = 8
LANES = 128
TT = 256
SCAN_BLK = LANES
SCAN_GRP = 4
HY_BLK = 256
VMEM_LIMIT = 56 * 1024 * 1024


def _cparams(sem, vmem=None):
    return pltpu.CompilerParams(dimension_semantics=sem, vmem_limit_bytes=vmem)


def _ln(xf):
    mu = jnp.mean(xf, -1, keepdims=True)
    xc = xf - mu
    var = jnp.mean(xc * xc, -1, keepdims=True)
    return xc * lax.rsqrt(var + LN_EPS)


def _pick_tile(n, cands):
    for c in cands:
        if n % c == 0:
            return c
    raise ValueError(f"no tile for {n}")


def _ada_kernel(c_ref, w_ref, b_ref, o_ref):
    c = c_ref[...]
    a = c * jax.nn.sigmoid(c)
    o_ref[0] = jnp.dot(a, w_ref[0], precision=HI, preferred_element_type=F32) + b_ref[0]


def ada_mod(cc, ada_w, ada_b):
    depth, d, n = ada_w.shape
    rows = cc.shape[0]
    return pl.pallas_call(
        _ada_kernel,
        out_shape=jax.ShapeDtypeStruct((depth, rows, n), F32),
        grid=(depth, n // d),
        in_specs=[pl.BlockSpec((rows, d), lambda l, j: (0, 0)),
                  pl.BlockSpec((1, d, d), lambda l, j: (l, 0, j)),
                  pl.BlockSpec((1, 1, d), lambda l, j: (l, 0, j))],
        out_specs=pl.BlockSpec((1, rows, d), lambda l, j: (l, 0, j)),
        compiler_params=_cparams(("parallel", "parallel"), VMEM_LIMIT),
        name="ada_mod",
    )(cc, ada_w, ada_b.reshape(depth, 1, n))


def _lnmod_kernel(x_ref, sh_ref, sc_ref, o_ref):
    h = _ln(x_ref[0]) * (1.0 + sc_ref[0]) + sh_ref[0]
    o_ref[0] = h.astype(BF16)


def _mod_spec(col, nct, nb):
    return pl.BlockSpec((1, 1, D_MODEL), lambda b, t: (jnp.where(t < nct, nb, b), 0, col))


def lnmod(x_all, mod3, nct, col_shift, col_scale):
    nb, ta, d = x_all.shape
    return pl.pallas_call(
        _lnmod_kernel,
        out_shape=jax.ShapeDtypeStruct((nb, ta, d), BF16),
        grid=(nb, ta // TT),
        in_specs=[pl.BlockSpec((1, TT, d), lambda b, t: (b, t, 0)),
                  _mod_spec(col_shift, nct, nb), _mod_spec(col_scale, nct, nb)],
        out_specs=pl.BlockSpec((1, TT, d), lambda b, t: (b, t, 0)),
        compiler_params=_cparams(("parallel", "parallel")),
        name="lnmod",
    )(x_all, mod3, mod3)


def _mm_kernel(a_ref, b_ref, o_ref):
    o_ref[...] = jnp.dot(a_ref[...], b_ref[...], preferred_element_type=F32).astype(o_ref.dtype)


def matmul(a, b, out_dtype=F32):
    m, k = a.shape
    _, n = b.shape
    tm = _pick_tile(m, (1024, 512, 256))
    tn = _pick_tile(n, (1024, 512, 256))
    return pl.pallas_call(
        _mm_kernel,
        out_shape=jax.ShapeDtypeStruct((m, n), out_dtype),
        grid=(m // tm, n // tn),
        in_specs=[pl.BlockSpec((tm, k), lambda i, j: (i, 0)),
                  pl.BlockSpec((k, tn), lambda i, j: (0, j))],
        out_specs=pl.BlockSpec((tm, tn), lambda i, j: (i, j)),
        compiler_params=_cparams(("parallel", "parallel"), VMEM_LIMIT),
        name="matmul",
    )(a, b)


def _halo_specs(width, ta):
    nblk8 = ta // SUBLANES
    per = TT // SUBLANES
    cur = pl.BlockSpec((1, TT, width), lambda b, t: (b, t, 0))
    prev = pl.BlockSpec((1, SUBLANES, width), lambda b, t: (b, jnp.maximum(t * per - 1, 0), 0))
    nxt = pl.BlockSpec((1, SUBLANES, width), lambda b, t: (b, jnp.minimum((t + 1) * per, nblk8 - 1), 0))
    return [cur, prev, nxt]


def _neighbours(cur, prev_ref, next_ref, nct, nt):
    t = pl.program_id(1)
    seg_start = jnp.logical_or(t == 0, t == nct)
    seg_end = jnp.logical_or(t == nct - 1, t == nt - 1)
    prev_row = prev_ref[0][SUBLANES - 1:SUBLANES, :] * jnp.where(seg_start, 0.0, 1.0)
    next_row = next_ref[0][0:1, :] * jnp.where(seg_end, 0.0, 1.0)
    row = lax.broadcasted_iota(jnp.int32, (TT, 1), 0)
    xm1 = jnp.where(row == 0, prev_row, pltpu.roll(cur, 1, axis=0))
    xp1 = jnp.where(row == TT - 1, next_row, pltpu.roll(cur, TT - 1, axis=0))
    return xm1, xp1


def _rwkv_prep_kernel(nct, nt, cur_ref, prev_ref, next_ref, mu_ref, w0_ref, wup_ref, a0_ref, aup_ref,
                      gup_ref, kk_ref, ka_ref, rk_ref, bd_ref,
                      r_o, kk_o, w0_o, w1_o, k0_o, k1_o, b0_o, b1_o, v_o, g_o, bon_o):
    cur = cur_ref[0]
    xm1, xp1 = _neighbours(cur, prev_ref, next_ref, nct, nt)
    p = cur + mu_ref[...] * (0.5 * (xm1 + xp1) - cur)
    c = BRANCH_W
    r, k, v = p[:, 0:c], p[:, c:2 * c], p[:, 2 * c:3 * c]
    wd = p[:, 3 * c:3 * c + 64]
    ad = p[:, 3 * c + 64:3 * c + 128]
    gd = p[:, 3 * c + 128:3 * c + 256]
    bd = bd_ref[...]
    kk = k * kk_ref[...]
    ss = jnp.dot(kk * kk, bd, precision=HI, preferred_element_type=F32)
    kkn = kk * lax.rsqrt(jnp.maximum(ss, 1e-24))
    twd = jnp.tanh(wd)
    ka = ka_ref[...]
    kdirs = []
    w_outs, k_outs, b_outs = (w0_o, w1_o), (k0_o, k1_o), (b0_o, b1_o)
    for d in range(2):
        wlog = w0_ref[d:d + 1, :] + jnp.dot(twd, wup_ref[d], precision=HI, preferred_element_type=F32)
        decay = jnp.exp(-math.exp(-0.5) * jax.nn.sigmoid(wlog))
        a = jax.nn.sigmoid(a0_ref[d:d + 1, :] + jnp.dot(ad, aup_ref[d], precision=HI, preferred_element_type=F32))
        kdir = k * (1.0 + (a - 1.0) * ka)
        bdir = kkn * a
        kdirs.append(kdir)
        w_outs[d][0] = decay
        k_outs[d][0] = kdir
        b_outs[d][0] = bdir
    r_o[0] = r
    kk_o[0] = kkn
    v_o[0] = v
    g_o[0] = jnp.dot(jax.nn.sigmoid(gd), gup_ref[...], precision=HI, preferred_element_type=F32)
    rkk = r * rk_ref[...] * (kdirs[0] + kdirs[1])
    bon_o[0] = jnp.dot(rkk, bd, precision=HI, preferred_element_type=F32) * v


def rwkv_prep(p_rwkv, nct, mu, w0, w_up, a0, a_up, g_up, k_k, k_a, r_k, bd):
    nb, ta, _ = p_rwkv.shape
    nt = ta // TT
    c = BRANCH_W
    nat = jax.ShapeDtypeStruct((nb, ta, c), F32)
    nat_spec = pl.BlockSpec((1, TT, c), lambda b, t: (b, t, 0))

    def full(a):
        nd = a.ndim
        return pl.BlockSpec(a.shape, lambda b, t: (0,) * nd)

    consts = [mu.reshape(1, -1), w0, w_up, a0, a_up, g_up, k_k.reshape(1, -1), k_a.reshape(1, -1),
              r_k.reshape(1, -1), bd]
    return pl.pallas_call(
        functools.partial(_rwkv_prep_kernel, nct, nt),
        out_shape=[nat] * 11,
        grid=(nb, nt),
        in_specs=_halo_specs(RWKV_COLS, ta) + [full(a) for a in consts],
        out_specs=[nat_spec] * 11,
        compiler_params=_cparams(("parallel", "parallel"), VMEM_LIMIT),
        name="rwkv_prep",
    )(p_rwkv, p_rwkv, p_rwkv, *consts)


def _scan_kernel(nb, rf, kkf, wf, kf, bf, vf, rb, kkb, wb, kb, bb, vb, bd_ref, yf_o, yb_o,
                 s_scr, vt_scr, yt_scr):
    step = pl.program_id(0)
    npair = BRANCH_W // LANES
    nsub = SCAN_BLK // HEAD_DIM

    @pl.when(step == 0)
    def _():
        s_scr[...] = jnp.zeros_like(s_scr)

    for d, vref in ((0, vf), (1, vb)):
        for b in range(nb):
            vblk = vref[b]
            for p in range(npair):
                vt = vblk[:, p * LANES:(p + 1) * LANES].T
                for sb in range(nsub):
                    tk = slice(sb * HEAD_DIM, (sb + 1) * HEAD_DIM)
                    vt_scr[d, b, p, sb] = jnp.concatenate([vt[:HEAD_DIM, tk], vt[HEAD_DIM:, tk]], axis=1)

    bd = bd_ref[...]
    bd_lo = bd.astype(BF16)
    lane_tok = lax.broadcasted_iota(jnp.int32, (HEAD_DIM, LANES), 1) % HEAD_DIM
    rows = ((rf, kkf, wf, kf, bf), (rb, kkb, wb, kb, bb))

    def seg_sum(parts):
        return jnp.dot(jnp.concatenate(parts, axis=0).astype(BF16), bd_lo, preferred_element_type=F32)

    def half(hb, carry):
        sub = (hb, nsub - 1 - hb)

        def group(g, carry):
            for j in range(SCAN_GRP):
                tf = g * SCAN_GRP + j
                chains, t1, xv = [], [], []
                for d in range(2):
                    tl = tf if d == 0 else HEAD_DIM - 1 - tf
                    t = sub[d] * HEAD_DIM + tl
                    tok = lane_tok == tl
                    for b in range(nb):
                        row5 = [ref[b, pl.ds(t, 1), :] for ref in rows[d]]
                        for p in range(npair):
                            sl = slice(p * LANES, (p + 1) * LANES)
                            r_row, kk_row, w_row, k_row, b_row = (a[:, sl] for a in row5)
                            s = s_scr[d, b, p]
                            t1.append(s * kk_row)
                            xv.append(jnp.where(tok, vt_scr[d, b, p, sub[d]], 0.0))
                            chains.append((d, b, p, tok, s, r_row, w_row, k_row, b_row))
                skk_all = seg_sum(t1)
                vcol_all = jnp.dot(jnp.concatenate(xv, axis=0), bd, preferred_element_type=F32)
                ylhs = []
                for i, (d, b, p, tok, s, r_row, w_row, k_row, b_row) in enumerate(chains):
                    rs = slice(i * HEAD_DIM, (i + 1) * HEAD_DIM)
                    s = s * w_row - skk_all[rs] * b_row + vcol_all[rs] * k_row
                    s_scr[d, b, p] = s
                    ylhs.append(s * r_row)
                yres = seg_sum(ylhs)
                for i, (d, b, p, tok, *_) in enumerate(chains):
                    pltpu.store(yt_scr.at[d, b, p, sub[d]], yres[i * HEAD_DIM:(i + 1) * HEAD_DIM], mask=tok)
            return carry

        return lax.fori_loop(0, HEAD_DIM // SCAN_GRP, group, carry)

    lax.fori_loop(0, nsub, half, 0)

    for d, yref in ((0, yf_o), (1, yb_o)):
        for b in range(nb):
            for p in range(npair):
                heads = [jnp.concatenate([yt_scr[d, b, p, sb][:, h * HEAD_DIM:(h + 1) * HEAD_DIM]
                                          for sb in range(nsub)], axis=1) for h in range(2)]
                yref[b, :, p * LANES:(p + 1) * LANES] = jnp.concatenate(heads, axis=0).T


def rwkv_scan(r, kk, v, w0, w1, k0, k1, b0, b1, lc, bd_pair):
    nb, ta, c = r.shape
    nblk = ta // SCAN_BLK
    nctb = lc // SCAN_BLK
    npair = c // LANES
    nsub = SCAN_BLK // HEAD_DIM

    def fwd(s):
        return s

    def bwd(s):
        return jnp.where(s < nctb, nctb - 1 - s, nblk - 1 - (s - nctb))

    def nat(idx):
        return pl.BlockSpec((nb, SCAN_BLK, c), lambda s: (0, idx(s), 0))

    out = jax.ShapeDtypeStruct((nb, ta, c), F32)
    tiles = pltpu.VMEM((2, nb, npair, nsub, HEAD_DIM, LANES), F32)
    return pl.pallas_call(
        functools.partial(_scan_kernel, nb),
        out_shape=[out, out],
        grid=(nblk,),
        in_specs=[nat(fwd)] * 6 + [nat(bwd)] * 6 + [pl.BlockSpec(bd_pair.shape, lambda s: (0, 0))],
        out_specs=[nat(fwd), nat(bwd)],
        scratch_shapes=[pltpu.VMEM((2, nb, npair, HEAD_DIM, LANES), F32), tiles, tiles],
        compiler_params=_cparams(("arbitrary",), VMEM_LIMIT),
        name="rwkv_scan",
    )(r, kk, w0, k0, b0, v, r, kk, w1, k1, b1, v, bd_pair)


def _rwkv_out_kernel(yf_ref, yb_ref, g_ref, bon_ref, lg_ref, lb_ref, bd_ref, o_ref):
    y = yf_ref[0] + yb_ref[0]
    bd = bd_ref[...]
    mu = jnp.dot(y, bd, precision=HI, preferred_element_type=F32) * (1.0 / HEAD_DIM)
    yc = y - mu
    var = jnp.dot(yc * yc, bd, precision=HI, preferred_element_type=F32) * (1.0 / HEAD_DIM)
    yn = yc * lax.rsqrt(var + RWKV_GN_EPS) * lg_ref[...] + lb_ref[...]
    o_ref[0] = ((yn + bon_ref[0]) * g_ref[0]).astype(BF16)


def rwkv_out(yf, yb, g, bon, lnx_g, lnx_b, bd):
    nb, ta, _ = yf.shape
    nat_spec = pl.BlockSpec((1, TT, BRANCH_W), lambda b, t: (b, t, 0))
    row = pl.BlockSpec((1, BRANCH_W), lambda b, t: (0, 0))
    return pl.pallas_call(
        _rwkv_out_kernel,
        out_shape=jax.ShapeDtypeStruct((nb, ta, BRANCH_W), BF16),
        grid=(nb, ta // TT),
        in_specs=[nat_spec] * 4 + [row, row, pl.BlockSpec(bd.shape, lambda b, t: (0, 0))],
        out_specs=nat_spec,
        compiler_params=_cparams(("parallel", "parallel")),
        name="rwkv_out",
    )(yf, yb, g, bon, lnx_g.reshape(1, -1), lnx_b.reshape(1, -1), bd)


def _pair_swap(x):
    lane = lax.broadcasted_iota(jnp.int32, x.shape, 1)
    n = x.shape[1]
    return jnp.where(lane % 2 == 0, pltpu.roll(x, n - 1, axis=1), pltpu.roll(x, 1, axis=1))


def _attn_prep_kernel(p_ref, cos_ref, sin_ref, qg_ref, kg_ref, bd_ref, q_o, k_o, v_o):
    p = p_ref[0]
    q, k, v = p[:, :Q_W], p[:, Q_W:Q_W + KV_W], p[:, Q_W + KV_W:]
    bd = bd_ref[...]
    cos, sin = cos_ref[...], sin_ref[...]
    qms = jnp.dot(q * q, bd, precision=HI, preferred_element_type=F32) * (1.0 / HEAD_DIM)
    qn = q * lax.rsqrt(qms + RMS_EPS) * qg_ref[...]
    qr = qn * cos + _pair_swap(qn) * sin
    q_o[0] = (qr * HEAD_DIM ** -0.5).astype(BF16)
    kms = jnp.dot(k * k, bd[:KV_W, :KV_W], precision=HI, preferred_element_type=F32) * (1.0 / HEAD_DIM)
    kn = k * lax.rsqrt(kms + RMS_EPS) * kg_ref[...]
    kr = kn * cos[:, :KV_W] + _pair_swap(kn) * sin[:, :KV_W]
    for g in range(KV_W // HEAD_DIM):
        sl = slice(g * HEAD_DIM, (g + 1) * HEAD_DIM)
        k_o[0, g] = kr[:, sl].astype(BF16)
        v_o[0, g] = v[:, sl].astype(BF16)


def attn_prep(p_attn, cos_t, sin_t, q_norm, k_norm, bd):
    nb, ta, _ = p_attn.shape
    ng = KV_W // HEAD_DIM
    qg = jnp.tile(q_norm, Q_W // HEAD_DIM).reshape(1, -1)
    kg = jnp.tile(k_norm, ng).reshape(1, -1)
    kv_shape = jax.ShapeDtypeStruct((nb, ng, ta, HEAD_DIM), BF16)
    kv_spec = pl.BlockSpec((1, ng, TT, HEAD_DIM), lambda b, t: (b, 0, t, 0))
    return pl.pallas_call(
        _attn_prep_kernel,
        out_shape=[jax.ShapeDtypeStruct((nb, ta, Q_W), BF16), kv_shape, kv_shape],
        grid=(nb, ta // TT),
        in_specs=[pl.BlockSpec((1, TT, ATTN_COLS), lambda b, t: (b, t, 0)),
                  pl.BlockSpec((TT, Q_W), lambda b, t: (t, 0)),
                  pl.BlockSpec((TT, Q_W), lambda b, t: (t, 0)),
                  pl.BlockSpec((1, Q_W), lambda b, t: (0, 0)),
                  pl.BlockSpec((1, KV_W), lambda b, t: (0, 0)),
                  pl.BlockSpec(bd.shape, lambda b, t: (0, 0))],
        out_specs=[pl.BlockSpec((1, TT, Q_W), lambda b, t: (b, t, 0)), kv_spec, kv_spec],
        compiler_params=_cparams(("parallel", "parallel")),
        name="attn_prep",
    )(p_attn, cos_t, sin_t, qg, kg, bd)


def _attn_kernel(nct, lc, q_ref, k_ref, v_ref, o_ref):
    t = pl.program_id(2)

    def run(kk, vv):
        outs = []
        for r in range(2):
            q = q_ref[0][:, r * HEAD_DIM:(r + 1) * HEAD_DIM]
            s = lax.dot_general(q, kk, (((1,), (1,)), ((), ())), preferred_element_type=F32)
            p = jnp.exp(s - jnp.max(s, -1, keepdims=True))
            l = jnp.sum(p, -1, keepdims=True)
            o = jnp.dot(p.astype(BF16), vv, preferred_element_type=F32)
            outs.append(o / l)
        o_ref[0] = jnp.concatenate(outs, axis=1).astype(BF16)

    @pl.when(t < nct)
    def _():
        run(k_ref[0, 0, :lc, :], v_ref[0, 0, :lc, :])

    @pl.when(t >= nct)
    def _():
        run(k_ref[0, 0], v_ref[0, 0])


def attention(q, k, v, lc):
    nb, ta, _ = q.shape
    ng = k.shape[1]
    nct = lc // TT
    qo_spec = pl.BlockSpec((1, TT, 2 * HEAD_DIM), lambda b, g, t: (b, t, g))
    kv_spec = pl.BlockSpec((1, 1, ta, HEAD_DIM), lambda b, g, t: (b, g, 0, 0))
    return pl.pallas_call(
        functools.partial(_attn_kernel, nct, lc),
        out_shape=jax.ShapeDtypeStruct((nb, ta, Q_W), BF16),
        grid=(nb, ng, ta // TT),
        in_specs=[qo_spec, kv_spec, kv_spec],
        out_specs=qo_spec,
        compiler_params=_cparams(("parallel", "parallel", "arbitrary"), VMEM_LIMIT),
        name="attention",
    )(q, k, v)


def _hs_pre_kernel(nct, nt, cur_ref, prev_ref, next_ref, hw_ref, sw_ref, x0_o, u_o, ycv_o):
    cur = cur_ref[0]
    xm1, xp1 = _neighbours(cur, prev_ref, next_ref, nct, nt)
    c = BRANCH_W
    hc = HYENA_COLS
    hw = hw_ref[...]
    ph = hw[0:1] * xm1[:, :hc] + hw[1:2] * cur[:, :hc] + hw[2:3] * xp1[:, :hc]
    x0_o[0] = ph[:, :c]
    u_o[0] = ph[:, c:2 * c] * ph[:, 2 * c:3 * c]
    sw = sw_ref[...]

    def cx(a):
        return a[:, hc + c:hc + 2 * c] * a[:, hc + 2 * c:hc + 3 * c]

    conv = sw[0:1] * cx(xm1) + sw[1:2] * cx(cur) + sw[2:3] * cx(xp1)
    ycv_o[0] = (cur[:, hc:hc + c] * conv).astype(BF16)


def hs_pre(p_hs, nct, hyena_conv, sconv_w):
    nb, ta, w = p_hs.shape
    nt = ta // TT
    nat = pl.BlockSpec((1, TT, BRANCH_W), lambda b, t: (b, t, 0))
    return pl.pallas_call(
        functools.partial(_hs_pre_kernel, nct, nt),
        out_shape=[jax.ShapeDtypeStruct((nb, ta, BRANCH_W), F32),
                   jax.ShapeDtypeStruct((nb, ta, BRANCH_W), F32),
                   jax.ShapeDtypeStruct((nb, ta, BRANCH_W), BF16)],
        grid=(nb, nt),
        in_specs=_halo_specs(w, ta) + [pl.BlockSpec(hyena_conv.shape, lambda b, t: (0, 0)),
                                       pl.BlockSpec(sconv_w.shape, lambda b, t: (0, 0))],
        out_specs=[nat, nat, nat],
        compiler_params=_cparams(("parallel", "parallel"), VMEM_LIMIT),
        name="hs_pre",
    )(p_hs, p_hs, p_hs, hyena_conv, sconv_w)


EMB_PAD = 40


def _filter_tables(lh):
    n = np.arange(2 * lh)
    pos = np.abs(n - (lh - 1)).astype(np.float64)
    bands = (HYENA_EMB - 1) // 2
    t = np.minimum(pos, lh - 1) / (lh - 1)
    wpos = 2.0 * math.pi * pos / lh
    f = np.linspace(1e-4, bands - 1, bands)[:, None]
    z = np.zeros((EMB_PAD, 2 * lh), np.float32)
    z[0] = t
    z[1:1 + bands] = np.cos(f * wpos[None, :])
    z[1 + bands:1 + 2 * bands] = -np.sin(f * wpos[None, :])
    max_decay = math.log(HYENA_TARGET) / HYENA_FAST_DECAY
    min_decay = math.log(HYENA_TARGET) / HYENA_SLOW_DECAY
    deltas = np.abs(np.linspace(min_decay, max_decay, BRANCH_W)).astype(np.float32)
    return z, deltas.reshape(-1, 1)


def _filter_kernel(lh, tn, z_ref, w1_ref, b1_ref, f1_ref, w2_ref, b2_ref, f2_ref, w3_ref, dl_ref, o_ref):
    z = z_ref[...]
    h1 = jnp.sin(f1_ref[...] * (jnp.dot(w1_ref[...], z, precision=HI, preferred_element_type=F32) + b1_ref[...]))
    h2 = jnp.sin(f2_ref[...] * (jnp.dot(w2_ref[...], h1, precision=HI, preferred_element_type=F32) + b2_ref[...]))
    f = jnp.dot(w3_ref[...], h2, precision=HI, preferred_element_type=F32)
    n = pl.program_id(0) * tn + lax.broadcasted_iota(jnp.int32, (1, tn), 1)
    filt = jnp.where(n >= lh - 1, f[:BRANCH_W], f[BRANCH_W:])
    win = jnp.exp(-z[0:1, :] * dl_ref[...])
    o_ref[...] = jnp.where(n == 2 * lh - 1, 0.0, filt * win)


def hyena_filter_table(lh, w1, b1, f1, w2, b2, f2, w3):
    z_np, dl_np = _filter_tables(lh)
    n2 = 2 * lh
    tn = _pick_tile(n2, (1024, 512))
    hd = w2.shape[0]
    w1t = jnp.zeros((hd, EMB_PAD), F32).at[:, :HYENA_EMB].set(w1.T)
    args = [jnp.asarray(z_np), w1t, b1.reshape(-1, 1), f1.reshape(-1, 1), w2.T, b2.reshape(-1, 1),
            f2.reshape(-1, 1), w3.T, jnp.asarray(dl_np)]

    def full(a):
        return pl.BlockSpec(a.shape, lambda j: (0, 0))

    return pl.pallas_call(
        functools.partial(_filter_kernel, lh, tn),
        out_shape=jax.ShapeDtypeStruct((BRANCH_W, n2), F32),
        grid=(n2 // tn,),
        in_specs=[pl.BlockSpec((EMB_PAD, tn), lambda j: (0, j))] + [full(a) for a in args[1:]],
        out_specs=pl.BlockSpec((BRANCH_W, tn), lambda j: (0, j)),
        compiler_params=_cparams(("parallel",)),
        name="hyena_filter",
    )(*args)


def _hyena_conv_kernel(nblk, bp, k_ref, u_ref, o_ref, t_scr):
    ntile = 4 * nblk - 1
    mc = 2 * nblk - 1
    for m in range(ntile):
        win = jnp.concatenate([k_ref[0, m:m + 1, :], k_ref[0, m + 1:m + 2, :]], axis=1)
        x = jnp.broadcast_to(win, (LANES, 2 * LANES))
        t_scr[m] = pltpu.roll(x, LANES + 1, 1, stride=1, stride_axis=0)[:, :LANES].astype(BF16)

    for d in [0] + [s * a for a in range(1, nblk) for s in (1, -1)]:
        m0 = 2 * d + mc
        w = jnp.concatenate([jnp.concatenate([t_scr[m0], t_scr[m0 + 1]], axis=1),
                             jnp.concatenate([t_scr[m0 - 1], t_scr[m0]], axis=1)], axis=0)
        i0, i1 = max(0, d), min(nblk, nblk + d)
        lhs = u_ref[0, (i0 - d) * bp:(i1 - d) * bp, :].astype(BF16)
        res = jnp.dot(lhs, w, preferred_element_type=F32)
        if d == 0:
            o_ref[0] = res
        else:
            o_ref[0, i0 * bp:i1 * bp, :] += res


def hyena_conv(u, ktab):
    nb, l, c = u.shape
    nblk = l // HY_BLK
    bp = -(-nb // SUBLANES) * SUBLANES
    ut = jnp.transpose(u.reshape(nb, nblk, HY_BLK, c), (3, 1, 0, 2))
    if bp != nb:
        ut = jnp.pad(ut, ((0, 0), (0, 0), (0, bp - nb), (0, 0)))
    ut = ut.reshape(c, nblk * bp, HY_BLK)
    k3 = ktab.reshape(c, 4 * nblk, LANES)
    out = pl.pallas_call(
        functools.partial(_hyena_conv_kernel, nblk, bp),
        out_shape=jax.ShapeDtypeStruct((c, nblk * bp, HY_BLK), F32),
        grid=(c,),
        in_specs=[pl.BlockSpec((1, 4 * nblk, LANES), lambda ch: (ch, 0, 0)),
                  pl.BlockSpec((1, nblk * bp, HY_BLK), lambda ch: (ch, 0, 0))],
        out_specs=pl.BlockSpec((1, nblk * bp, HY_BLK), lambda ch: (ch, 0, 0)),
        scratch_shapes=[pltpu.VMEM((4 * nblk - 1, LANES, LANES), BF16)],
        compiler_params=_cparams(("parallel",)),
        name="hyena_conv",
    )(k3, ut)
    out = out.reshape(c, nblk, bp, HY_BLK)[:, :, :nb]
    return jnp.transpose(out, (2, 1, 3, 0)).reshape(nb, l, c)


def _route(logits, bias):
    s = jax.nn.sigmoid(logits)
    sel = s + bias
    srow = [s[e:e + 1] for e in range(N_EXPERTS)]
    row = [sel[e:e + 1] for e in range(N_EXPERTS)]
    best, gi = None, None
    for g in range(N_GROUPS):
        a, b, c, d = row[4 * g:4 * g + 4]
        hi1, lo1, hi2, lo2 = jnp.maximum(a, b), jnp.minimum(a, b), jnp.maximum(c, d), jnp.minimum(c, d)
        score = jnp.maximum(hi1, hi2) + jnp.maximum(jnp.minimum(hi1, hi2), jnp.maximum(lo1, lo2))
        if g == 0:
            best, gi = score, jnp.zeros(score.shape, jnp.int32)
        else:
            better = score > best
            gi = jnp.where(better, g, gi)
            best = jnp.where(better, score, best)
    neg = -jnp.inf
    msel = [jnp.where(gi == e // EXPERTS_PER_GROUP, row[e], neg) for e in range(N_EXPERTS)]

    def arg_first_max(vals):
        bv, bi = vals[0], jnp.zeros(vals[0].shape, jnp.int32)
        for e in range(1, N_EXPERTS):
            better = vals[e] > bv
            bi = jnp.where(better, e, bi)
            bv = jnp.where(better, vals[e], bv)
        return bi

    i1 = arg_first_max(msel)
    i2 = arg_first_max([jnp.where(i1 == e, neg, msel[e]) for e in range(N_EXPERTS)])
    w1 = sum(jnp.where(i1 == e, srow[e], 0.0) for e in range(N_EXPERTS))
    w2 = sum(jnp.where(i2 == e, srow[e], 0.0) for e in range(N_EXPERTS))
    den = w1 + w2
    g1, g2 = w1 / den, w2 / den
    return jnp.concatenate([jnp.where(i1 == e, g1, 0.0) + jnp.where(i2 == e, g2, 0.0)
                            for e in range(N_EXPERTS)], axis=0)


def _merge_kernel(ya_ref, x0_ref, u_ref, yc_ref, ycv_ref, yd_ref, pg_ref, x_ref, ga_ref, shf_ref, scf_ref,
                  skip_ref, wb_ref, wo_ref, g1_ref, b1_ref, rwt_ref, rb_ref, x1_o, hf_o, gates_o):
    yb = (x0_ref[0] * (yc_ref[0] + u_ref[0] * skip_ref[...])).astype(BF16)
    ys = (ya_ref[0], yb, ycv_ref[0], yd_ref[0])
    merged = None
    for n in range(N_BRANCHES):
        gate = jax.nn.sigmoid(pg_ref[0][:, n * D_MODEL:(n + 1) * D_MODEL])
        term = gate * jnp.dot(ys[n], wb_ref[n], preferred_element_type=F32)
        merged = term if merged is None else merged + term
    out = jnp.dot(merged.astype(BF16), wo_ref[...], preferred_element_type=F32)
    alpha = (2 * 2) ** 0.25
    x1 = _ln(alpha * x_ref[0] + ga_ref[0] * out) * g1_ref[...] + b1_ref[...]
    hf = _ln(x1) * (1.0 + scf_ref[0]) + shf_ref[0]
    x1_o[0] = x1
    hf_o[0] = hf.astype(BF16)
    logits = lax.dot_general(rwt_ref[...], hf, (((1,), (1,)), ((), ())), precision=HI,
                             preferred_element_type=F32)
    gates_o[...] = _route(logits, rb_ref[...])


def merge(ya, x0, u, yconv, ycv, yd, pg, x_all, mod3, nct, skip, wb, wo, ln_g, ln_b, rwt, rbias):
    nb, ta, d = x_all.shape
    nt = ta // TT
    nat = pl.BlockSpec((1, TT, BRANCH_W), lambda b, t: (b, t, 0))
    wide = pl.BlockSpec((1, TT, d), lambda b, t: (b, t, 0))

    def full(a):
        nd = a.ndim
        return pl.BlockSpec(a.shape, lambda b, t: (0,) * nd)

    consts = [skip.reshape(1, -1), wb, wo, ln_g.reshape(1, -1), ln_b.reshape(1, -1), rwt, rbias.reshape(-1, 1)]
    return pl.pallas_call(
        _merge_kernel,
        out_shape=[jax.ShapeDtypeStruct((nb, ta, d), F32), jax.ShapeDtypeStruct((nb, ta, d), BF16),
                   jax.ShapeDtypeStruct((N_EXPERTS, nb * ta), F32)],
        grid=(nb, nt),
        in_specs=[nat] * 6 + [pl.BlockSpec((1, TT, N_BRANCHES * d), lambda b, t: (b, t, 0)), wide,
                              _mod_spec(2, nct, nb), _mod_spec(3, nct, nb), _mod_spec(4, nct, nb)]
                 + [full(a) for a in consts],
        out_specs=[wide, wide, pl.BlockSpec((N_EXPERTS, TT), lambda b, t: (0, b * nt + t))],
        compiler_params=_cparams(("parallel", "parallel"), VMEM_LIMIT),
        name="merge",
    )(ya, x0, u, yconv, ycv, yd, pg, x_all, mod3, mod3, mod3, *consts)


def _moe_kernel(h_ref, g_ref, w1_ref, w3_ref, w2_ref, o_ref):
    e = pl.program_id(1)
    h = h_ref[...]
    a = jnp.dot(h, w1_ref[0], preferred_element_type=F32)
    b = jnp.dot(h, w3_ref[0], preferred_element_type=F32)
    act = (a * jax.nn.sigmoid(a)) * b
    gt = g_ref[...]
    lane = lax.broadcasted_iota(jnp.int32, gt.shape, 1)
    gcol = jnp.sum(jnp.where(lane == e, gt, 0.0), axis=1, keepdims=True)
    contrib = gcol * jnp.dot(act.astype(BF16), w2_ref[0], preferred_element_type=F32)

    @pl.when(e == 0)
    def _():
        o_ref[...] = contrib

    @pl.when(e > 0)
    def _():
        o_ref[...] += contrib


def moe_dense(hf, gates, w1, w3, w2):
    n, d = hf.shape
    ne, _, de = w1.shape
    tm = _pick_tile(n, (1024, 512, 256))
    return pl.pallas_call(
        _moe_kernel,
        out_shape=jax.ShapeDtypeStruct((n, d), F32),
        grid=(n // tm, ne),
        in_specs=[pl.BlockSpec((tm, d), lambda i, e: (i, 0)),
                  pl.BlockSpec((tm, LANES), lambda i, e: (i, 0)),
                  pl.BlockSpec((1, d, de), lambda i, e: (e, 0, 0)),
                  pl.BlockSpec((1, d, de), lambda i, e: (e, 0, 0)),
                  pl.BlockSpec((1, de, d), lambda i, e: (e, 0, 0))],
        out_specs=pl.BlockSpec((tm, d), lambda i, e: (i, 0)),
        compiler_params=_cparams(("parallel", "arbitrary"), VMEM_LIMIT),
        name="moe",
    )(hf, gates, w1, w3, w2)


def _ln2_kernel(x_ref, f_ref, gf_ref, g_ref, b_ref, o_ref):
    alpha = (2 * 2) ** 0.25
    o_ref[0] = _ln(alpha * x_ref[0] + gf_ref[0] * f_ref[0]) * g_ref[...] + b_ref[...]


def ln2(x1, f, mod3, nct, ln_g, ln_b):
    nb, ta, d = x1.shape
    wide = pl.BlockSpec((1, TT, d), lambda b, t: (b, t, 0))
    row = pl.BlockSpec((1, d), lambda b, t: (0, 0))
    return pl.pallas_call(
        _ln2_kernel,
        out_shape=jax.ShapeDtypeStruct((nb, ta, d), F32),
        grid=(nb, ta // TT),
        in_specs=[wide, wide, _mod_spec(5, nct, nb), row, row],
        out_specs=wide,
        compiler_params=_cparams(("parallel", "parallel")),
        name="ln2",
    )(x1, f, mod3, ln_g.reshape(1, -1), ln_b.reshape(1, -1))


def _rope_tables(l, lc):
    half = HEAD_DIM // 2
    inv = ROPE_THETA ** (-np.arange(0, half, 2, dtype=np.float64) / half)
    t = np.arange(l)
    rows, cols = t // GRID_W, t % GRID_W
    ang = np.concatenate([rows[:, None] * inv, cols[:, None] * inv], -1)
    ang = np.concatenate([np.zeros((lc, half)), ang], 0)
    cos = np.repeat(np.cos(ang), 2, axis=1)
    sin = np.repeat(np.sin(ang), 2, axis=1)
    sin[:, 0::2] *= -1.0
    reps = Q_W // HEAD_DIM
    return (jnp.asarray(np.tile(cos, (1, reps)), F32), jnp.asarray(np.tile(sin, (1, reps)), F32))


def _block_diag_ones():
    i = np.arange(BRANCH_W) // HEAD_DIM
    return jnp.asarray((i[:, None] == i[None, :]).astype(np.float32))


def kernel(x, c, ctx, c_ctx, ada_w, ada_b, w_in, rwkv_mu, rwkv_w0, rwkv_w_up, rwkv_a0, rwkv_a_up, rwkv_g_up, rwkv_k_k, rwkv_k_a, rwkv_r_k, rwkv_lnx_g, rwkv_lnx_b, hyena_conv, hyena_w1, hyena_b1, hyena_freq1, hyena_w2, hyena_b2, hyena_freq2, hyena_w3, hyena_skip, sconv_w, attn_q_norm, attn_k_norm, w_branch, w_out, ln1_g, ln1_b, ln2_g, ln2_b, router_w, router_bias, exp_w1, exp_w3, exp_w2):
    nb, l, d = x.shape
    lc = ctx.shape[1]
    depth = ada_w.shape[0]
    assert d == D_MODEL and lc % TT == 0 and l % TT == 0 and l % GRID_W == 0
    ta = lc + l
    nct = lc // TT
    x_all = jnp.concatenate([ctx, x], axis=1)

    mod_rows = -(-(nb + 1) // SUBLANES) * SUBLANES
    cc = jnp.zeros((mod_rows, d), F32).at[:nb].set(c).at[nb].set(c_ctx)
    mod = ada_mod(cc, ada_w, ada_b)

    cos_t, sin_t = _rope_tables(l, lc)
    bd = _block_diag_ones()
    rwt = router_w.T

    for li in range(depth):
        mod3 = mod[li].reshape(mod_rows, 1, N_MOD * d)
        wl = w_in[li].astype(BF16)
        h = lnmod(x_all, mod3, nct, 0, 1).reshape(nb * ta, d)
        p_rwkv = matmul(h, wl[:, :OFF_HYENA]).reshape(nb, ta, -1)
        p_hs = matmul(h, wl[:, OFF_HYENA:OFF_ATTN]).reshape(nb, ta, -1)
        p_attn = matmul(h, wl[:, OFF_ATTN:OFF_GATE]).reshape(nb, ta, -1)
        p_gate = matmul(h, wl[:, OFF_GATE:]).reshape(nb, ta, -1)

        r, kk, w0, w1, k0, k1, b0, b1, v, g, bon = rwkv_prep(
            p_rwkv, nct, rwkv_mu[li], rwkv_w0[li], rwkv_w_up[li], rwkv_a0[li], rwkv_a_up[li], rwkv_g_up[li],
            rwkv_k_k[li], rwkv_k_a[li], rwkv_r_k[li], bd)
        yf, yb = rwkv_scan(r, kk, v, w0, w1, k0, k1, b0, b1, lc, bd[:LANES, :LANES])
        ya = rwkv_out(yf, yb, g, bon, rwkv_lnx_g[li], rwkv_lnx_b[li], bd)

        q, kx, vx = attn_prep(p_attn, cos_t, sin_t, attn_q_norm[li], attn_k_norm[li], bd)
        yd = attention(q, kx, vx, lc)

        x0, u, ycv = hs_pre(p_hs, nct, hyena_conv[li], sconv_w[li])
        fargs = (hyena_w1[li], hyena_b1[li], hyena_freq1[li], hyena_w2[li], hyena_b2[li], hyena_freq2[li],
                 hyena_w3[li])
        yconv = jnp.concatenate([hyena_conv_seg(u[:, :lc], fargs), hyena_conv_seg(u[:, lc:], fargs)], axis=1)

        x1, hf, gates_t = merge(ya, x0, u, yconv, ycv, yd, p_gate, x_all, mod3, nct, hyena_skip[li],
                                w_branch[li].astype(BF16), w_out[li].astype(BF16), ln1_g[li], ln1_b[li],
                                rwt, router_bias)
        gates = jnp.pad(gates_t.T, ((0, 0), (0, LANES - N_EXPERTS)))
        f = moe_dense(hf.reshape(nb * ta, d), gates, exp_w1[li].astype(BF16), exp_w3[li].astype(BF16),
                      exp_w2[li].astype(BF16)).reshape(nb, ta, d)
        x_all = ln2(x1, f, mod3, nct, ln2_g[li], ln2_b[li])
    return x_all[:, lc:]


def hyena_conv_seg(u_seg, fargs):
    ktab = hyena_filter_table(u_seg.shape[1], *fargs)
    return hyena_conv(u_seg, ktab)
```

```python
import functools
import math

import numpy as np
import jax
import jax.numpy as jnp
from jax import lax
from jax.experimental import pallas as pl
from jax.experimental.pallas import tpu as pltpu

F32 = jnp.float32
BF16 = jnp.bfloat16
HI = lax.Precision.HIGHEST

D_MODEL = 1024
GRID_W = 64
BRANCH_W = 256
HEAD_DIM = 64
N_BRANCHES = 4
N_MOD = 6
RWKV_HEADS = 4
RWKV_COLS = 1024
RWKV_GN_EPS = 64e-5
HYENA_COLS = 768
HYENA_EMB = 33
HYENA_FAST_DECAY = 0.3
HYENA_SLOW_DECAY = 1.5
HYENA_TARGET = 1e-2
SCONV_COLS = 768
Q_W = 256
KV_W = 128
ATTN_COLS = 512
ROPE_THETA = 10000.0
RMS_EPS = 1e-6
OFF_HYENA = RWKV_COLS
OFF_SCONV = OFF_HYENA + HYENA_COLS
OFF_ATTN = OFF_SCONV + SCONV_COLS
OFF_GATE = OFF_ATTN + ATTN_COLS
N_EXPERTS = 16
N_GROUPS = 4
EXPERTS_PER_GROUP = 4
D_EXPERT = 512
LN_EPS = 1e-6

SUBLANES = 8
LANES = 128
TT = 256
SCAN_BLK = LANES
HY_BLK = 256
VMEM_LIMIT = 56 * 1024 * 1024


def _cparams(sem, vmem=None):
    return pltpu.CompilerParams(dimension_semantics=sem, vmem_limit_bytes=vmem)


def _ln(xf):
    mu = jnp.mean(xf, -1, keepdims=True)
    xc = xf - mu
    var = jnp.mean(xc * xc, -1, keepdims=True)
    return xc * lax.rsqrt(var + LN_EPS)


def _pick_tile(n, cands):
    for c in cands:
        if n % c == 0:
            return c
    raise ValueError(f"no tile for {n}")


def _ada_kernel(c_ref, w_ref, b_ref, o_ref):
    c = c_ref[...]
    a = c * jax.nn.sigmoid(c)
    o_ref[0] = jnp.dot(a, w_ref[0], precision=HI, preferred_element_type=F32) + b_ref[0]


def ada_mod(cc, ada_w, ada_b):
    depth, d, n = ada_w.shape
    rows = cc.shape[0]
    return pl.pallas_call(
        _ada_kernel,
        out_shape=jax.ShapeDtypeStruct((depth, rows, n), F32),
        grid=(depth, n // d),
        in_specs=[pl.BlockSpec((rows, d), lambda l, j: (0, 0)),
                  pl.BlockSpec((1, d, d), lambda l, j: (l, 0, j)),
                  pl.BlockSpec((1, 1, d), lambda l, j: (l, 0, j))],
        out_specs=pl.BlockSpec((1, rows, d), lambda l, j: (l, 0, j)),
        compiler_params=_cparams(("parallel", "parallel"), VMEM_LIMIT),
        name="ada_mod",
    )(cc, ada_w, ada_b.reshape(depth, 1, n))


def _lnmod_kernel(x_ref, sh_ref, sc_ref, o_ref):
    h = _ln(x_ref[0]) * (1.0 + sc_ref[0]) + sh_ref[0]
    o_ref[0] = h.astype(BF16)


def _mod_spec(col, nct, nb):
    return pl.BlockSpec((1, 1, D_MODEL), lambda b, t: (jnp.where(t < nct, nb, b), 0, col))


def lnmod(x_all, mod3, nct, col_shift, col_scale):
    nb, ta, d = x_all.shape
    return pl.pallas_call(
        _lnmod_kernel,
        out_shape=jax.ShapeDtypeStruct((nb, ta, d), BF16),
        grid=(nb, ta // TT),
        in_specs=[pl.BlockSpec((1, TT, d), lambda b, t: (b, t, 0)),
                  _mod_spec(col_shift, nct, nb), _mod_spec(col_scale, nct, nb)],
        out_specs=pl.BlockSpec((1, TT, d), lambda b, t: (b, t, 0)),
        compiler_params=_cparams(("parallel", "parallel")),
        name="lnmod",
    )(x_all, mod3, mod3)


def _mm_kernel(a_ref, b_ref, o_ref):
    o_ref[...] = jnp.dot(a_ref[...], b_ref[...], preferred_element_type=F32).astype(o_ref.dtype)


def matmul(a, b, out_dtype=F32):
    m, k = a.shape
    _, n = b.shape
    tm = _pick_tile(m, (1024, 512, 256))
    tn = _pick_tile(n, (1024, 512, 256))
    return pl.pallas_call(
        _mm_kernel,
        out_shape=jax.ShapeDtypeStruct((m, n), out_dtype),
        grid=(m // tm, n // tn),
        in_specs=[pl.BlockSpec((tm, k), lambda i, j: (i, 0)),
                  pl.BlockSpec((k, tn), lambda i, j: (0, j))],
        out_specs=pl.BlockSpec((tm, tn), lambda i, j: (i, j)),
        compiler_params=_cparams(("parallel", "parallel"), VMEM_LIMIT),
        name="matmul",
    )(a, b)


def _halo_specs(width, ta):
    nblk8 = ta // SUBLANES
    per = TT // SUBLANES
    cur = pl.BlockSpec((1, TT, width), lambda b, t: (b, t, 0))
    prev = pl.BlockSpec((1, SUBLANES, width), lambda b, t: (b, jnp.maximum(t * per - 1, 0), 0))
    nxt = pl.BlockSpec((1, SUBLANES, width), lambda b, t: (b, jnp.minimum((t + 1) * per, nblk8 - 1), 0))
    return [cur, prev, nxt]


def _neighbours(cur, prev_ref, next_ref, nct, nt):
    t = pl.program_id(1)
    seg_start = jnp.logical_or(t == 0, t == nct)
    seg_end = jnp.logical_or(t == nct - 1, t == nt - 1)
    prev_row = prev_ref[0][SUBLANES - 1:SUBLANES, :] * jnp.where(seg_start, 0.0, 1.0)
    next_row = next_ref[0][0:1, :] * jnp.where(seg_end, 0.0, 1.0)
    row = lax.broadcasted_iota(jnp.int32, (TT, 1), 0)
    xm1 = jnp.where(row == 0, prev_row, pltpu.roll(cur, 1, axis=0))
    xp1 = jnp.where(row == TT - 1, next_row, pltpu.roll(cur, TT - 1, axis=0))
    return xm1, xp1


def _rwkv_prep_kernel(nct, nt, cur_ref, prev_ref, next_ref, mu_ref, w0_ref, wup_ref, a0_ref, aup_ref,
                      gup_ref, kk_ref, ka_ref, rk_ref, bd_ref,
                      r_o, kk_o, w0_o, w1_o, k0_o, k1_o, b0_o, b1_o, v_o, g_o, bon_o):
    cur = cur_ref[0]
    xm1, xp1 = _neighbours(cur, prev_ref, next_ref, nct, nt)
    p = cur + mu_ref[...] * (0.5 * (xm1 + xp1) - cur)
    c = BRANCH_W
    r, k, v = p[:, 0:c], p[:, c:2 * c], p[:, 2 * c:3 * c]
    wd = p[:, 3 * c:3 * c + 64]
    ad = p[:, 3 * c + 64:3 * c + 128]
    gd = p[:, 3 * c + 128:3 * c + 256]
    bd = bd_ref[...]
    kk = k * kk_ref[...]
    ss = jnp.dot(kk * kk, bd, precision=HI, preferred_element_type=F32)
    kkn = kk * lax.rsqrt(jnp.maximum(ss, 1e-24))
    twd = jnp.tanh(wd)
    ka = ka_ref[...]
    kdirs = []
    w_outs, k_outs, b_outs = (w0_o, w1_o), (k0_o, k1_o), (b0_o, b1_o)
    for d in range(2):
        wlog = w0_ref[d:d + 1, :] + jnp.dot(twd, wup_ref[d], precision=HI, preferred_element_type=F32)
        decay = -math.exp(-0.5) * jax.nn.sigmoid(wlog)
        a = jax.nn.sigmoid(a0_ref[d:d + 1, :] + jnp.dot(ad, aup_ref[d], precision=HI, preferred_element_type=F32))
        kdir = k * (1.0 + (a - 1.0) * ka)
        bdir = kkn * a
        kdirs.append(kdir)
        w_outs[d][0] = decay
        k_outs[d][0] = kdir
        b_outs[d][0] = bdir
    r_o[0] = r
    kk_o[0] = kkn
    v_o[0] = v
    g_o[0] = jnp.dot(jax.nn.sigmoid(gd), gup_ref[...], precision=HI, preferred_element_type=F32)
    rkk = r * rk_ref[...] * (kdirs[0] + kdirs[1])
    bon_o[0] = jnp.dot(rkk, bd, precision=HI, preferred_element_type=F32) * v


def rwkv_prep(p_rwkv, nct, mu, w0, w_up, a0, a_up, g_up, k_k, k_a, r_k, bd):
    nb, ta, _ = p_rwkv.shape
    nt = ta // TT
    c = BRANCH_W
    nat = jax.ShapeDtypeStruct((nb, ta, c), F32)
    nat_spec = pl.BlockSpec((1, TT, c), lambda b, t: (b, t, 0))

    def full(a):
        nd = a.ndim
        return pl.BlockSpec(a.shape, lambda b, t: (0,) * nd)

    consts = [mu.reshape(1, -1), w0, w_up, a0, a_up, g_up, k_k.reshape(1, -1), k_a.reshape(1, -1),
              r_k.reshape(1, -1), bd]
    return pl.pallas_call(
        functools.partial(_rwkv_prep_kernel, nct, nt),
        out_shape=[nat] * 11,
        grid=(nb, nt),
        in_specs=_halo_specs(RWKV_COLS, ta) + [full(a) for a in consts],
        out_specs=[nat_spec] * 11,
        compiler_params=_cparams(("parallel", "parallel"), VMEM_LIMIT),
        name="rwkv_prep",
    )(p_rwkv, p_rwkv, p_rwkv, *consts)


CHUNK = 16


def _chunk_scan_rows(x, reverse):
    pos = lax.broadcasted_iota(jnp.int32, x.shape, 0) % CHUNK
    step = 1
    while step < CHUNK:
        if reverse:
            x = x + jnp.where(pos < CHUNK - step, pltpu.roll(x, x.shape[0] - step, axis=0), 0.0)
        else:
            x = x + jnp.where(pos >= step, pltpu.roll(x, step, axis=0), 0.0)
        step *= 2
    return x


def _chunk_prep_kernel(r_ref, kk_ref, v_ref, lw0, k0, b0, lw1, k1, b1,
                       a0_o, bm0_o, rp0_o, y00_o, a1_o, bm1_o, rp1_o, y01_o):
    blk = SCAN_BLK
    npair = BRANCH_W // LANES
    nchunk = blk // CHUNK
    r, kk, v = r_ref[0], kk_ref[0], v_ref[0]
    ti = lax.broadcasted_iota(jnp.int32, (blk, blk), 0)
    si = lax.broadcasted_iota(jnp.int32, (blk, blk), 1)
    same = (ti // CHUNK) == (si // CHUNK)
    eye = ti == si
    bd64 = (ti // HEAD_DIM) == (si // HEAD_DIM)
    head0 = si < HEAD_DIM
    lane_half = lax.broadcasted_iota(jnp.int32, (HEAD_DIM, LANES), 1)
    ti2 = lax.broadcasted_iota(jnp.int32, (2 * blk, 2 * blk), 0)
    si2 = lax.broadcasted_iota(jnp.int32, (2 * blk, 2 * blk), 1)
    eye2 = jnp.where(ti2 == si2, 1.0, 0.0)
    zero = jnp.zeros((blk, blk), F32)
    dot = lambda x, y: jnp.dot(x, y, preferred_element_type=F32)
    lo = lambda x: x.astype(BF16)
    split = lambda x: jnp.concatenate([jnp.where(head0, x, 0.0), jnp.where(head0, 0.0, x)], axis=0)
    cat = lambda ms: lo(jnp.concatenate(ms, axis=1))

    probs = []
    for d, (lw_ref, k_ref, b_ref) in enumerate(((lw0, k0, b0), (lw1, k1, b1))):
        reverse = d == 1
        lw, k, b = lw_ref[0], k_ref[0], b_ref[0]
        lg = _chunk_scan_rows(lw, reverse)
        lg_end = lg + _chunk_scan_rows(lw, not reverse) - lw
        g, gi, g_end = jnp.exp(lg), jnp.exp(-lg), jnp.exp(lg_end)
        to_end = jnp.exp(lg_end - lg)
        arrs = (kk * jnp.exp(lg - lw), b * gi, k * gi, r * g, v, k * to_end, b * to_end, g_end)
        incl = jnp.logical_and(same, si >= ti if reverse else si <= ti)
        strict = jnp.logical_and(same, si > ti if reverse else si < ti)
        for p in range(npair):
            lanes = slice(p * LANES, (p + 1) * LANES)
            probs.append(dict(d=d, p=p, incl=incl, strict=strict, arrs=tuple(a[:, lanes] for a in arrs)))

    for q in probs:
        pp_, q_, kt_, rt_ = q["arrs"][:4]
        rhs_g = lo(jnp.concatenate([q_, kt_], axis=0))
        lm, mm, n2, nn = [], [], [], []
        for hh in range(2):
            hm = head0 if hh == 0 else jnp.logical_not(head0)
            lhs_g = lo(jnp.concatenate([jnp.where(hm, pp_, 0.0), jnp.where(hm, rt_, 0.0)], axis=0))
            gm = lax.dot_general(lhs_g, rhs_g, (((1,), (1,)), ((), ())), preferred_element_type=F32)
            lm.append(jnp.where(q["strict"], gm[:blk, :blk], 0.0))
            mm.append(jnp.where(q["strict"], gm[:blk, blk:], 0.0))
            n2.append(jnp.where(q["incl"], gm[blk:, :blk], 0.0))
            nn.append(jnp.where(q["incl"], gm[blk:, blk:], 0.0))
        q["pw"] = jnp.concatenate([jnp.concatenate([lm[0], zero], axis=1), jnp.concatenate([zero, lm[1]], axis=1)], axis=0)
        q["tm"] = eye2 - q["pw"]
        q["m_cat"], q["n2_cat"], q["nn_cat"] = cat(mm), cat(n2), cat(nn)
    for _ in range(CHUNK.bit_length() - 2):
        for q in probs:
            pwl = lo(q["pw"])
            q["pw"] = dot(pwl, pwl)
        for q in probs:
            q["tm"] = dot(lo(q["tm"]), lo(eye2 + q["pw"]))
    for q in probs:
        q["t_cat"] = cat([q["tm"][:blk, :blk], q["tm"][blk:, blk:]])
        q["v_st"] = lo(split(q["arrs"][4]))
        q["pp"] = dot(q["t_cat"], lo(split(q["arrs"][0])))
        q["mv"] = dot(q["m_cat"], q["v_st"])
    for q in probs:
        q["w2"] = dot(q["t_cat"], lo(split(q["mv"])))
        q["rp"] = q["arrs"][3] - dot(q["n2_cat"], lo(split(q["pp"])))
    for q in probs:
        q["y0"] = dot(q["nn_cat"], q["v_st"]) - dot(q["n2_cat"], lo(split(q["w2"])))
    in_chunk = [si // CHUNK == c for c in range(nchunk)]
    for q in probs:
        ppt, vt, w2t = q["pp"].T, q["arrs"][4].T, q["w2"].T
        kg_, qg_ = q["arrs"][5], q["arrs"][6]
        lhs_a = jnp.concatenate([jnp.where(cm, ppt, 0.0) for cm in in_chunk], axis=0)
        q["pq"] = dot(lo(lhs_a), lo(qg_))
        lhs_b = jnp.concatenate([jnp.concatenate([jnp.where(cm, vt, 0.0), jnp.where(cm, -w2t, 0.0)], axis=1)
                                 for cm in in_chunk], axis=0)
        q["bf"] = dot(lo(lhs_b), lo(jnp.concatenate([kg_, qg_], axis=0)))
    outs = ((a0_o, bm0_o, rp0_o, y00_o), (a1_o, bm1_o, rp1_o, y01_o))
    for q in probs:
        a_o, bm_o = outs[q["d"]][:2]
        g_end_p = q["arrs"][7]
        for c in range(nchunk):
            pq_c = q["pq"][c * blk:(c + 1) * blk]
            a_o[0, 0, c, q["p"]] = (jnp.where(eye, g_end_p[c * CHUNK:c * CHUNK + 1], 0.0)
                                    - jnp.where(bd64, pq_c, 0.0)).astype(BF16)
            bm_o[0, 0, c, q["p"]] = jnp.where(lane_half < HEAD_DIM, q["bf"][c * blk:c * blk + HEAD_DIM],
                                              q["bf"][c * blk + HEAD_DIM:(c + 1) * blk])
    for d in range(2):
        rp_o, y0_o = outs[d][2:]
        rp_o[0] = jnp.concatenate([q["rp"] for q in probs if q["d"] == d], axis=1)
        y0_o[0] = jnp.concatenate([q["y0"] for q in probs if q["d"] == d], axis=1)


def chunk_prep(r, kk, v, lw0, lw1, k0, k1, b0, b1):
    nb, ta, c = r.shape
    nblk = ta // SCAN_BLK
    npair = c // LANES
    nchunk = SCAN_BLK // CHUNK
    nat = pl.BlockSpec((1, SCAN_BLK, c), lambda b, s: (b, s, 0))
    a_shape = jax.ShapeDtypeStruct((nb, nblk, nchunk, npair, LANES, LANES), BF16)
    bm_shape = jax.ShapeDtypeStruct((nb, nblk, nchunk, npair, HEAD_DIM, LANES), F32)
    nat_shape = jax.ShapeDtypeStruct((nb, ta, c), F32)
    a_spec = pl.BlockSpec((1, 1, nchunk, npair, LANES, LANES), lambda b, s: (b, s, 0, 0, 0, 0))
    bm_spec = pl.BlockSpec((1, 1, nchunk, npair, HEAD_DIM, LANES), lambda b, s: (b, s, 0, 0, 0, 0))
    return pl.pallas_call(
        _chunk_prep_kernel,
        out_shape=[a_shape, bm_shape, nat_shape, nat_shape] * 2,
        grid=(nb, nblk),
        in_specs=[nat] * 9,
        out_specs=[a_spec, bm_spec, nat, nat] * 2,
        compiler_params=_cparams(("parallel", "parallel"), VMEM_LIMIT),
        name="chunk_prep",
    )(r, kk, v, lw0, k0, b0, lw1, k1, b1)


def _chunk_scan_kernel(nb, a0, bm0, rp0, y00, a1, bm1, rp1, y01, yf_o, yb_o, s_scr):
    step = pl.program_id(0)
    npair = BRANCH_W // LANES
    nchunk = SCAN_BLK // CHUNK

    @pl.when(step == 0)
    def _():
        s_scr[...] = jnp.zeros_like(s_scr)

    lane = lax.broadcasted_iota(jnp.int32, (CHUNK, LANES), 1)
    refs = ((a0, bm0, rp0, y00, yf_o), (a1, bm1, rp1, y01, yb_o))
    for ci in range(nchunk):
        for d in range(2):
            a_ref, bm_ref, rp_ref, y0_ref, y_ref = refs[d]
            c = ci if d == 0 else nchunk - 1 - ci
            rows = slice(c * CHUNK, (c + 1) * CHUNK)
            for b in range(nb):
                for p in range(npair):
                    lanes = slice(p * LANES, (p + 1) * LANES)
                    s = s_scr[d, b, p]
                    rpc = rp_ref[b, rows, lanes]
                    lhs = jnp.concatenate([jnp.where(lane < HEAD_DIM, rpc, 0.0), jnp.where(lane >= HEAD_DIM, rpc, 0.0)],
                                          axis=0)
                    yh = lax.dot_general(lhs, s, (((1,), (1,)), ((), ())), preferred_element_type=F32)
                    y_ref[b, rows, lanes] = jnp.concatenate([yh[:CHUNK], yh[CHUNK:]], axis=1) + y0_ref[b, rows, lanes]
                    s_scr[d, b, p] = (jnp.dot(s.astype(BF16), a_ref[b, 0, c, p], preferred_element_type=F32)
                                      + bm_ref[b, 0, c, p])


def rwkv_scan(r, kk, v, lw0, lw1, k0, k1, b0, b1, lc):
    nb, ta, c = r.shape
    nblk = ta // SCAN_BLK
    nctb = lc // SCAN_BLK
    npair = c // LANES
    nchunk = SCAN_BLK // CHUNK
    a0, bm0, rp0, y00, a1, bm1, rp1, y01 = chunk_prep(r, kk, v, lw0, lw1, k0, k1, b0, b1)

    def fwd(s):
        return s

    def bwd(s):
        return jnp.where(s < nctb, nctb - 1 - s, nblk - 1 - (s - nctb))

    def specs(idx):
        return [pl.BlockSpec((nb, 1, nchunk, npair, LANES, LANES), lambda s: (0, idx(s), 0, 0, 0, 0)),
                pl.BlockSpec((nb, 1, nchunk, npair, HEAD_DIM, LANES), lambda s: (0, idx(s), 0, 0, 0, 0)),
                pl.BlockSpec((nb, SCAN_BLK, c), lambda s: (0, idx(s), 0)),
                pl.BlockSpec((nb, SCAN_BLK, c), lambda s: (0, idx(s), 0))]

    out = jax.ShapeDtypeStruct((nb, ta, c), F32)
    return pl.pallas_call(
        functools.partial(_chunk_scan_kernel, nb),
        out_shape=[out, out],
        grid=(nblk,),
        in_specs=specs(fwd) + specs(bwd),
        out_specs=[pl.BlockSpec((nb, SCAN_BLK, c), lambda s: (0, fwd(s), 0)),
                   pl.BlockSpec((nb, SCAN_BLK, c), lambda s: (0, bwd(s), 0))],
        scratch_shapes=[pltpu.VMEM((2, nb, npair, HEAD_DIM, LANES), F32)],
        compiler_params=_cparams(("arbitrary",), VMEM_LIMIT),
        name="chunk_scan",
    )(a0, bm0, rp0, y00, a1, bm1, rp1, y01)


def _rwkv_out_kernel(yf_ref, yb_ref, g_ref, bon_ref, lg_ref, lb_ref, bd_ref, o_ref):
    y = yf_ref[0] + yb_ref[0]
    bd = bd_ref[...]
    mu = jnp.dot(y, bd, precision=HI, preferred_element_type=F32) * (1.0 / HEAD_DIM)
    yc = y - mu
    var = jnp.dot(yc * yc, bd, precision=HI, preferred_element_type=F32) * (1.0 / HEAD_DIM)
    yn = yc * lax.rsqrt(var + RWKV_GN_EPS) * lg_ref[...] + lb_ref[...]
    o_ref[0] = ((yn + bon_ref[0]) * g_ref[0]).astype(BF16)


def rwkv_out(yf, yb, g, bon, lnx_g, lnx_b, bd):
    nb, ta, _ = yf.shape
    nat_spec = pl.BlockSpec((1, TT, BRANCH_W), lambda b, t: (b, t, 0))
    row = pl.BlockSpec((1, BRANCH_W), lambda b, t: (0, 0))
    return pl.pallas_call(
        _rwkv_out_kernel,
        out_shape=jax.ShapeDtypeStruct((nb, ta, BRANCH_W), BF16),
        grid=(nb, ta // TT),
        in_specs=[nat_spec] * 4 + [row, row, pl.BlockSpec(bd.shape, lambda b, t: (0, 0))],
        out_specs=nat_spec,
        compiler_params=_cparams(("parallel", "parallel")),
        name="rwkv_out",
    )(yf, yb, g, bon, lnx_g.reshape(1, -1), lnx_b.reshape(1, -1), bd)


def _pair_swap(x):
    lane = lax.broadcasted_iota(jnp.int32, x.shape, 1)
    n = x.shape[1]
    return jnp.where(lane % 2 == 0, pltpu.roll(x, n - 1, axis=1), pltpu.roll(x, 1, axis=1))


def _attn_prep_kernel(p_ref, cos_ref, sin_ref, qg_ref, kg_ref, bd_ref, q_o, k_o, v_o):
    p = p_ref[0]
    q, k, v = p[:, :Q_W], p[:, Q_W:Q_W + KV_W], p[:, Q_W + KV_W:]
    bd = bd_ref[...]
    cos, sin = cos_ref[...], sin_ref[...]
    qms = jnp.dot(q * q, bd, precision=HI, preferred_element_type=F32) * (1.0 / HEAD_DIM)
    qn = q * lax.rsqrt(qms + RMS_EPS) * qg_ref[...]
    qr = qn * cos + _pair_swap(qn) * sin
    q_o[0] = (qr * HEAD_DIM ** -0.5).astype(BF16)
    kms = jnp.dot(k * k, bd[:KV_W, :KV_W], precision=HI, preferred_element_type=F32) * (1.0 / HEAD_DIM)
    kn = k * lax.rsqrt(kms + RMS_EPS) * kg_ref[...]
    kr = kn * cos[:, :KV_W] + _pair_swap(kn) * sin[:, :KV_W]
    for g in range(KV_W // HEAD_DIM):
        sl = slice(g * HEAD_DIM, (g + 1) * HEAD_DIM)
        k_o[0, g] = kr[:, sl].astype(BF16)
        v_o[0, g] = v[:, sl].astype(BF16)


def attn_prep(p_attn, cos_t, sin_t, q_norm, k_norm, bd):
    nb, ta, _ = p_attn.shape
    ng = KV_W // HEAD_DIM
    qg = jnp.tile(q_norm, Q_W // HEAD_DIM).reshape(1, -1)
    kg = jnp.tile(k_norm, ng).reshape(1, -1)
    kv_shape = jax.ShapeDtypeStruct((nb, ng, ta, HEAD_DIM), BF16)
    kv_spec = pl.BlockSpec((1, ng, TT, HEAD_DIM), lambda b, t: (b, 0, t, 0))
    return pl.pallas_call(
        _attn_prep_kernel,
        out_shape=[jax.ShapeDtypeStruct((nb, ta, Q_W), BF16), kv_shape, kv_shape],
        grid=(nb, ta // TT),
        in_specs=[pl.BlockSpec((1, TT, ATTN_COLS), lambda b, t: (b, t, 0)),
                  pl.BlockSpec((TT, Q_W), lambda b, t: (t, 0)),
                  pl.BlockSpec((TT, Q_W), lambda b, t: (t, 0)),
                  pl.BlockSpec((1, Q_W), lambda b, t: (0, 0)),
                  pl.BlockSpec((1, KV_W), lambda b, t: (0, 0)),
                  pl.BlockSpec(bd.shape, lambda b, t: (0, 0))],
        out_specs=[pl.BlockSpec((1, TT, Q_W), lambda b, t: (b, t, 0)), kv_spec, kv_spec],
        compiler_params=_cparams(("parallel", "parallel")),
        name="attn_prep",
    )(p_attn, cos_t, sin_t, qg, kg, bd)


def _attn_kernel(nct, lc, q_ref, k_ref, v_ref, o_ref):
    t = pl.program_id(2)

    def run(kk, vv):
        outs = []
        for r in range(2):
            q = q_ref[0][:, r * HEAD_DIM:(r + 1) * HEAD_DIM]
            s = lax.dot_general(q, kk, (((1,), (1,)), ((), ())), preferred_element_type=F32)
            p = jnp.exp(s - jnp.max(s, -1, keepdims=True))
            l = jnp.sum(p, -1, keepdims=True)
            o = jnp.dot(p.astype(BF16), vv, preferred_element_type=F32)
            outs.append(o / l)
        o_ref[0] = jnp.concatenate(outs, axis=1).astype(BF16)

    @pl.when(t < nct)
    def _():
        run(k_ref[0, 0, :lc, :], v_ref[0, 0, :lc, :])

    @pl.when(t >= nct)
    def _():
        run(k_ref[0, 0], v_ref[0, 0])


def attention(q, k, v, lc):
    nb, ta, _ = q.shape
    ng = k.shape[1]
    nct = lc // TT
    qo_spec = pl.BlockSpec((1, TT, 2 * HEAD_DIM), lambda b, g, t: (b, t, g))
    kv_spec = pl.BlockSpec((1, 1, ta, HEAD_DIM), lambda b, g, t: (b, g, 0, 0))
    return pl.pallas_call(
        functools.partial(_attn_kernel, nct, lc),
        out_shape=jax.ShapeDtypeStruct((nb, ta, Q_W), BF16),
        grid=(nb, ng, ta // TT),
        in_specs=[qo_spec, kv_spec, kv_spec],
        out_specs=qo_spec,
        compiler_params=_cparams(("parallel", "parallel", "arbitrary"), VMEM_LIMIT),
        name="attention",
    )(q, k, v)


def _hs_pre_kernel(nct, nt, cur_ref, prev_ref, next_ref, hw_ref, sw_ref, x0_o, u_o, ycv_o):
    cur = cur_ref[0]
    xm1, xp1 = _neighbours(cur, prev_ref, next_ref, nct, nt)
    c = BRANCH_W
    hc = HYENA_COLS
    hw = hw_ref[...]
    ph = hw[0:1] * xm1[:, :hc] + hw[1:2] * cur[:, :hc] + hw[2:3] * xp1[:, :hc]
    x0_o[0] = ph[:, :c]
    u_o[0] = ph[:, c:2 * c] * ph[:, 2 * c:3 * c]
    sw = sw_ref[...]

    def cx(a):
        return a[:, hc + c:hc + 2 * c] * a[:, hc + 2 * c:hc + 3 * c]

    conv = sw[0:1] * cx(xm1) + sw[1:2] * cx(cur) + sw[2:3] * cx(xp1)
    ycv_o[0] = (cur[:, hc:hc + c] * conv).astype(BF16)


def hs_pre(p_hs, nct, hyena_conv, sconv_w):
    nb, ta, w = p_hs.shape
    nt = ta // TT
    nat = pl.BlockSpec((1, TT, BRANCH_W), lambda b, t: (b, t, 0))
    return pl.pallas_call(
        functools.partial(_hs_pre_kernel, nct, nt),
        out_shape=[jax.ShapeDtypeStruct((nb, ta, BRANCH_W), F32),
                   jax.ShapeDtypeStruct((nb, ta, BRANCH_W), F32),
                   jax.ShapeDtypeStruct((nb, ta, BRANCH_W), BF16)],
        grid=(nb, nt),
        in_specs=_halo_specs(w, ta) + [pl.BlockSpec(hyena_conv.shape, lambda b, t: (0, 0)),
                                       pl.BlockSpec(sconv_w.shape, lambda b, t: (0, 0))],
        out_specs=[nat, nat, nat],
        compiler_params=_cparams(("parallel", "parallel"), VMEM_LIMIT),
        name="hs_pre",
    )(p_hs, p_hs, p_hs, hyena_conv, sconv_w)


EMB_PAD = 40


def _filter_tables(lh):
    n = np.arange(2 * lh)
    pos = np.abs(n - (lh - 1)).astype(np.float64)
    bands = (HYENA_EMB - 1) // 2
    t = np.minimum(pos, lh - 1) / (lh - 1)
    wpos = 2.0 * math.pi * pos / lh
    f = np.linspace(1e-4, bands - 1, bands)[:, None]
    z = np.zeros((EMB_PAD, 2 * lh), np.float32)
    z[0] = t
    z[1:1 + bands] = np.cos(f * wpos[None, :])
    z[1 + bands:1 + 2 * bands] = -np.sin(f * wpos[None, :])
    max_decay = math.log(HYENA_TARGET) / HYENA_FAST_DECAY
    min_decay = math.log(HYENA_TARGET) / HYENA_SLOW_DECAY
    deltas = np.abs(np.linspace(min_decay, max_decay, BRANCH_W)).astype(np.float32)
    return z, deltas.reshape(-1, 1)


def _filter_kernel(lh, tn, z_ref, w1_ref, b1_ref, f1_ref, w2_ref, b2_ref, f2_ref, w3_ref, dl_ref, o_ref):
    z = z_ref[...]
    h1 = jnp.sin(f1_ref[...] * (jnp.dot(w1_ref[...], z, precision=HI, preferred_element_type=F32) + b1_ref[...]))
    h2 = jnp.sin(f2_ref[...] * (jnp.dot(w2_ref[...], h1, precision=HI, preferred_element_type=F32) + b2_ref[...]))
    f = jnp.dot(w3_ref[...], h2, precision=HI, preferred_element_type=F32)
    n = pl.program_id(0) * tn + lax.broadcasted_iota(jnp.int32, (1, tn), 1)
    filt = jnp.where(n >= lh - 1, f[:BRANCH_W], f[BRANCH_W:])
    win = jnp.exp(-z[0:1, :] * dl_ref[...])
    o_ref[...] = jnp.where(n == 2 * lh - 1, 0.0, filt * win)


def hyena_filter_table(lh, w1, b1, f1, w2, b2, f2, w3):
    z_np, dl_np = _filter_tables(lh)
    n2 = 2 * lh
    tn = _pick_tile(n2, (1024, 512))
    hd = w2.shape[0]
    w1t = jnp.zeros((hd, EMB_PAD), F32).at[:, :HYENA_EMB].set(w1.T)
    args = [jnp.asarray(z_np), w1t, b1.reshape(-1, 1), f1.reshape(-1, 1), w2.T, b2.reshape(-1, 1),
            f2.reshape(-1, 1), w3.T, jnp.asarray(dl_np)]

    def full(a):
        return pl.BlockSpec(a.shape, lambda j: (0, 0))

    return pl.pallas_call(
        functools.partial(_filter_kernel, lh, tn),
        out_shape=jax.ShapeDtypeStruct((BRANCH_W, n2), F32),
        grid=(n2 // tn,),
        in_specs=[pl.BlockSpec((EMB_PAD, tn), lambda j: (0, j))] + [full(a) for a in args[1:]],
        out_specs=pl.BlockSpec((BRANCH_W, tn), lambda j: (0, j)),
        compiler_params=_cparams(("parallel",)),
        name="hyena_filter",
    )(*args)


def _hyena_conv_kernel(nblk, bp, k_ref, u_ref, o_ref, t_scr):
    ntile = 4 * nblk - 1
    mc = 2 * nblk - 1
    for m in range(ntile):
        win = jnp.concatenate([k_ref[0, m:m + 1, :], k_ref[0, m + 1:m + 2, :]], axis=1)
        x = jnp.broadcast_to(win, (LANES, 2 * LANES))
        t_scr[m] = pltpu.roll(x, LANES + 1, 1, stride=1, stride_axis=0)[:, :LANES].astype(BF16)

    for d in [0] + [s * a for a in range(1, nblk) for s in (1, -1)]:
        m0 = 2 * d + mc
        w = jnp.concatenate([jnp.concatenate([t_scr[m0], t_scr[m0 + 1]], axis=1),
                             jnp.concatenate([t_scr[m0 - 1], t_scr[m0]], axis=1)], axis=0)
        i0, i1 = max(0, d), min(nblk, nblk + d)
        lhs = u_ref[0, (i0 - d) * bp:(i1 - d) * bp, :].astype(BF16)
        res = jnp.dot(lhs, w, preferred_element_type=F32)
        if d == 0:
            o_ref[0] = res
        else:
            o_ref[0, i0 * bp:i1 * bp, :] += res


def hyena_conv(u, ktab):
    nb, l, c = u.shape
    nblk = l // HY_BLK
    bp = -(-nb // SUBLANES) * SUBLANES
    ut = jnp.transpose(u.reshape(nb, nblk, HY_BLK, c), (3, 1, 0, 2))
    if bp != nb:
        ut = jnp.pad(ut, ((0, 0), (0, 0), (0, bp - nb), (0, 0)))
    ut = ut.reshape(c, nblk * bp, HY_BLK)
    k3 = ktab.reshape(c, 4 * nblk, LANES)
    out = pl.pallas_call(
        functools.partial(_hyena_conv_kernel, nblk, bp),
        out_shape=jax.ShapeDtypeStruct((c, nblk * bp, HY_BLK), F32),
        grid=(c,),
        in_specs=[pl.BlockSpec((1, 4 * nblk, LANES), lambda ch: (ch, 0, 0)),
                  pl.BlockSpec((1, nblk * bp, HY_BLK), lambda ch: (ch, 0, 0))],
        out_specs=pl.BlockSpec((1, nblk * bp, HY_BLK), lambda ch: (ch, 0, 0)),
        scratch_shapes=[pltpu.VMEM((4 * nblk - 1, LANES, LANES), BF16)],
        compiler_params=_cparams(("parallel",)),
        name="hyena_conv",
    )(k3, ut)
    out = out.reshape(c, nblk, bp, HY_BLK)[:, :, :nb]
    return jnp.transpose(out, (2, 1, 3, 0)).reshape(nb, l, c)


def _route(logits, bias):
    s = jax.nn.sigmoid(logits)
    sel = s + bias
    srow = [s[e:e + 1] for e in range(N_EXPERTS)]
    row = [sel[e:e + 1] for e in range(N_EXPERTS)]
    best, gi = None, None
    for g in range(N_GROUPS):
        a, b, c, d = row[4 * g:4 * g + 4]
        hi1, lo1, hi2, lo2 = jnp.maximum(a, b), jnp.minimum(a, b), jnp.maximum(c, d), jnp.minimum(c, d)
        score = jnp.maximum(hi1, hi2) + jnp.maximum(jnp.minimum(hi1, hi2), jnp.maximum(lo1, lo2))
        if g == 0:
            best, gi = score, jnp.zeros(score.shape, jnp.int32)
        else:
            better = score > best
            gi = jnp.where(better, g, gi)
            best = jnp.where(better, score, best)
    neg = -jnp.inf
    msel = [jnp.where(gi == e // EXPERTS_PER_GROUP, row[e], neg) for e in range(N_EXPERTS)]

    def arg_first_max(vals):
        bv, bi = vals[0], jnp.zeros(vals[0].shape, jnp.int32)
        for e in range(1, N_EXPERTS):
            better = vals[e] > bv
            bi = jnp.where(better, e, bi)
            bv = jnp.where(better, vals[e], bv)
        return bi

    i1 = arg_first_max(msel)
    i2 = arg_first_max([jnp.where(i1 == e, neg, msel[e]) for e in range(N_EXPERTS)])
    w1 = sum(jnp.where(i1 == e, srow[e], 0.0) for e in range(N_EXPERTS))
    w2 = sum(jnp.where(i2 == e, srow[e], 0.0) for e in range(N_EXPERTS))
    den = w1 + w2
    g1, g2 = w1 / den, w2 / den
    return jnp.concatenate([jnp.where(i1 == e, g1, 0.0) + jnp.where(i2 == e, g2, 0.0)
                            for e in range(N_EXPERTS)], axis=0)


def _merge_kernel(ya_ref, x0_ref, u_ref, yc_ref, ycv_ref, yd_ref, pg_ref, x_ref, ga_ref, shf_ref, scf_ref,
                  skip_ref, wb_ref, wo_ref, g1_ref, b1_ref, rwt_ref, rb_ref, x1_o, hf_o, gates_o):
    yb = (x0_ref[0] * (yc_ref[0] + u_ref[0] * skip_ref[...])).astype(BF16)
    ys = (ya_ref[0], yb, ycv_ref[0], yd_ref[0])
    merged = None
    for n in range(N_BRANCHES):
        gate = jax.nn.sigmoid(pg_ref[0][:, n * D_MODEL:(n + 1) * D_MODEL])
        term = gate * jnp.dot(ys[n], wb_ref[n], preferred_element_type=F32)
        merged = term if merged is None else merged + term
    out = jnp.dot(merged.astype(BF16), wo_ref[...], preferred_element_type=F32)
    alpha = (2 * 2) ** 0.25
    x1 = _ln(alpha * x_ref[0] + ga_ref[0] * out) * g1_ref[...] + b1_ref[...]
    hf = _ln(x1) * (1.0 + scf_ref[0]) + shf_ref[0]
    x1_o[0] = x1
    hf_o[0] = hf.astype(BF16)
    logits = lax.dot_general(rwt_ref[...], hf, (((1,), (1,)), ((), ())), precision=HI,
                             preferred_element_type=F32)
    gates_o[...] = _route(logits, rb_ref[...])


def merge(ya, x0, u, yconv, ycv, yd, pg, x_all, mod3, nct, skip, wb, wo, ln_g, ln_b, rwt, rbias):
    nb, ta, d = x_all.shape
    nt = ta // TT
    nat = pl.BlockSpec((1, TT, BRANCH_W), lambda b, t: (b, t, 0))
    wide = pl.BlockSpec((1, TT, d), lambda b, t: (b, t, 0))

    def full(a):
        nd = a.ndim
        return pl.BlockSpec(a.shape, lambda b, t: (0,) * nd)

    consts = [skip.reshape(1, -1), wb, wo, ln_g.reshape(1, -1), ln_b.reshape(1, -1), rwt, rbias.reshape(-1, 1)]
    return pl.pallas_call(
        _merge_kernel,
        out_shape=[jax.ShapeDtypeStruct((nb, ta, d), F32), jax.ShapeDtypeStruct((nb, ta, d), BF16),
                   jax.ShapeDtypeStruct((N_EXPERTS, nb * ta), F32)],
        grid=(nb, nt),
        in_specs=[nat] * 6 + [pl.BlockSpec((1, TT, N_BRANCHES * d), lambda b, t: (b, t, 0)), wide,
                              _mod_spec(2, nct, nb), _mod_spec(3, nct, nb), _mod_spec(4, nct, nb)]
                 + [full(a) for a in consts],
        out_specs=[wide, wide, pl.BlockSpec((N_EXPERTS, TT), lambda b, t: (0, b * nt + t))],
        compiler_params=_cparams(("parallel", "parallel"), VMEM_LIMIT),
        name="merge",
    )(ya, x0, u, yconv, ycv, yd, pg, x_all, mod3, mod3, mod3, *consts)


def _moe_kernel(h_ref, g_ref, w1_ref, w3_ref, w2_ref, o_ref):
    e = pl.program_id(1)
    h = h_ref[...]
    a = jnp.dot(h, w1_ref[0], preferred_element_type=F32)
    b = jnp.dot(h, w3_ref[0], preferred_element_type=F32)
    act = (a * jax.nn.sigmoid(a)) * b
    gt = g_ref[...]
    lane = lax.broadcasted_iota(jnp.int32, gt.shape, 1)
    gcol = jnp.sum(jnp.where(lane == e, gt, 0.0), axis=1, keepdims=True)
    contrib = gcol * jnp.dot(act.astype(BF16), w2_ref[0], preferred_element_type=F32)

    @pl.when(e == 0)
    def _():
        o_ref[...] = contrib

    @pl.when(e > 0)
    def _():
        o_ref[...] += contrib


def moe_dense(hf, gates, w1, w3, w2):
    n, d = hf.shape
    ne, _, de = w1.shape
    tm = _pick_tile(n, (1024, 512, 256))
    return pl.pallas_call(
        _moe_kernel,
        out_shape=jax.ShapeDtypeStruct((n, d), F32),
        grid=(n // tm, ne),
        in_specs=[pl.BlockSpec((tm, d), lambda i, e: (i, 0)),
                  pl.BlockSpec((tm, LANES), lambda i, e: (i, 0)),
                  pl.BlockSpec((1, d, de), lambda i, e: (e, 0, 0)),
                  pl.BlockSpec((1, d, de), lambda i, e: (e, 0, 0)),
                  pl.BlockSpec((1, de, d), lambda i, e: (e, 0, 0))],
        out_specs=pl.BlockSpec((tm, d), lambda i, e: (i, 0)),
        compiler_params=_cparams(("parallel", "arbitrary"), VMEM_LIMIT),
        name="moe",
    )(hf, gates, w1, w3, w2)


def _ln2_kernel(x_ref, f_ref, gf_ref, g_ref, b_ref, o_ref):
    alpha = (2 * 2) ** 0.25
    o_ref[0] = _ln(alpha * x_ref[0] + gf_ref[0] * f_ref[0]) * g_ref[...] + b_ref[...]


def ln2(x1, f, mod3, nct, ln_g, ln_b):
    nb, ta, d = x1.shape
    wide = pl.BlockSpec((1, TT, d), lambda b, t: (b, t, 0))
    row = pl.BlockSpec((1, d), lambda b, t: (0, 0))
    return pl.pallas_call(
        _ln2_kernel,
        out_shape=jax.ShapeDtypeStruct((nb, ta, d), F32),
        grid=(nb, ta // TT),
        in_specs=[wide, wide, _mod_spec(5, nct, nb), row, row],
        out_specs=wide,
        compiler_params=_cparams(("parallel", "parallel")),
        name="ln2",
    )(x1, f, mod3, ln_g.reshape(1, -1), ln_b.reshape(1, -1))


def _rope_tables(l, lc):
    half = HEAD_DIM // 2
    inv = ROPE_THETA ** (-np.arange(0, half, 2, dtype=np.float64) / half)
    t = np.arange(l)
    rows, cols = t // GRID_W, t % GRID_W
    ang = np.concatenate([rows[:, None] * inv, cols[:, None] * inv], -1)
    ang = np.concatenate([np.zeros((lc, half)), ang], 0)
    cos = np.repeat(np.cos(ang), 2, axis=1)
    sin = np.repeat(np.sin(ang), 2, axis=1)
    sin[:, 0::2] *= -1.0
    reps = Q_W // HEAD_DIM
    return (jnp.asarray(np.tile(cos, (1, reps)), F32), jnp.asarray(np.tile(sin, (1, reps)), F32))


def _block_diag_ones():
    i = np.arange(BRANCH_W) // HEAD_DIM
    return jnp.asarray((i[:, None] == i[None, :]).astype(np.float32))


def kernel(x, c, ctx, c_ctx, ada_w, ada_b, w_in, rwkv_mu, rwkv_w0, rwkv_w_up, rwkv_a0, rwkv_a_up, rwkv_g_up, rwkv_k_k, rwkv_k_a, rwkv_r_k, rwkv_lnx_g, rwkv_lnx_b, hyena_conv, hyena_w1, hyena_b1, hyena_freq1, hyena_w2, hyena_b2, hyena_freq2, hyena_w3, hyena_skip, sconv_w, attn_q_norm, attn_k_norm, w_branch, w_out, ln1_g, ln1_b, ln2_g, ln2_b, router_w, router_bias, exp_w1, exp_w3, exp_w2):
    nb, l, d = x.shape
    lc = ctx.shape[1]
    depth = ada_w.shape[0]
    assert d == D_MODEL and lc % TT == 0 and l % TT == 0 and l % GRID_W == 0
    ta = lc + l
    nct = lc // TT
    x_all = jnp.concatenate([ctx, x], axis=1)

    mod_rows = -(-(nb + 1) // SUBLANES) * SUBLANES
    cc = jnp.zeros((mod_rows, d), F32).at[:nb].set(c).at[nb].set(c_ctx)
    mod = ada_mod(cc, ada_w, ada_b)

    cos_t, sin_t = _rope_tables(l, lc)
    bd = _block_diag_ones()
    rwt = router_w.T

    for li in range(depth):
        mod3 = mod[li].reshape(mod_rows, 1, N_MOD * d)
        wl = w_in[li].astype(BF16)
        h = lnmod(x_all, mod3, nct, 0, 1).reshape(nb * ta, d)
        p_rwkv = matmul(h, wl[:, :OFF_HYENA]).reshape(nb, ta, -1)
        p_hs = matmul(h, wl[:, OFF_HYENA:OFF_ATTN]).reshape(nb, ta, -1)
        p_attn = matmul(h, wl[:, OFF_ATTN:OFF_GATE]).reshape(nb, ta, -1)
        p_gate = matmul(h, wl[:, OFF_GATE:]).reshape(nb, ta, -1)

        r, kk, w0, w1, k0, k1, b0, b1, v, g, bon = rwkv_prep(
            p_rwkv, nct, rwkv_mu[li], rwkv_w0[li], rwkv_w_up[li], rwkv_a0[li], rwkv_a_up[li], rwkv_g_up[li],
            rwkv_k_k[li], rwkv_k_a[li], rwkv_r_k[li], bd)
        yf, yb = rwkv_scan(r, kk, v, w0, w1, k0, k1, b0, b1, lc)
        ya = rwkv_out(yf, yb, g, bon, rwkv_lnx_g[li], rwkv_lnx_b[li], bd)

        q, kx, vx = attn_prep(p_attn, cos_t, sin_t, attn_q_norm[li], attn_k_norm[li], bd)
        yd = attention(q, kx, vx, lc)

        x0, u, ycv = hs_pre(p_hs, nct, hyena_conv[li], sconv_w[li])
        fargs = (hyena_w1[li], hyena_b1[li], hyena_freq1[li], hyena_w2[li], hyena_b2[li], hyena_freq2[li],
                 hyena_w3[li])
        yconv = jnp.concatenate([hyena_conv_seg(u[:, :lc], fargs), hyena_conv_seg(u[:, lc:], fargs)], axis=1)

        x1, hf, gates_t = merge(ya, x0, u, yconv, ycv, yd, p_gate, x_all, mod3, nct, hyena_skip[li],
                                w_branch[li].astype(BF16), w_out[li].astype(BF16), ln1_g[li], ln1_b[li],
                                rwt, router_bias)
        gates = jnp.pad(gates_t.T, ((0, 0), (0, LANES - N_EXPERTS)))
        f = moe_dense(hf.reshape(nb * ta, d), gates, exp_w1[li].astype(BF16), exp_w3[li].astype(BF16),
                      exp_w2[li].astype(BF16)).reshape(nb, ta, d)
        x_all = ln2(x1, f, mod3, nct, ln2_g[li], ln2_b[li])
    return x_all[:, lc:]


def hyena_conv_seg(u_seg, fargs):
    ktab = hyena_filter_table(u_seg.shape[1], *fargs)
    return hyena_conv(u_seg, ktab)
```

```python
import functools
import math

import numpy as np
import jax
import jax.numpy as jnp
from jax import lax
from jax.experimental import pallas as pl
from jax.experimental.pallas import tpu as pltpu

F32 = jnp.float32
BF16 = jnp.bfloat16
HI = lax.Precision.HIGHEST

D_MODEL = 1024
GRID_W = 64
BRANCH_W = 256
HEAD_DIM = 64
N_BRANCHES = 4
N_MOD = 6
RWKV_HEADS = 4
RWKV_COLS = 1024
RWKV_GN_EPS = 64e-5
HYENA_COLS = 768
HYENA_EMB = 33
HYENA_FAST_DECAY = 0.3
HYENA_SLOW_DECAY = 1.5
HYENA_TARGET = 1e-2
SCONV_COLS = 768
Q_W = 256
KV_W = 128
ATTN_COLS = 512
ROPE_THETA = 10000.0
RMS_EPS = 1e-6
OFF_HYENA = RWKV_COLS
OFF_SCONV = OFF_HYENA + HYENA_COLS
OFF_ATTN = OFF_SCONV + SCONV_COLS
OFF_GATE = OFF_ATTN + ATTN_COLS
N_EXPERTS = 16
N_GROUPS = 4
EXPERTS_PER_GROUP = 4
D_EXPERT = 512
LN_EPS = 1e-6

SUBLANES = 8
LANES = 128
TT = 256
SCAN_BLK = LANES
HY_BLK = 256
VMEM_LIMIT = 56 * 1024 * 1024


def _cparams(sem, vmem=None):
    return pltpu.CompilerParams(dimension_semantics=sem, vmem_limit_bytes=vmem)


def _ln(xf):
    mu = jnp.mean(xf, -1, keepdims=True)
    xc = xf - mu
    var = jnp.mean(xc * xc, -1, keepdims=True)
    return xc * lax.rsqrt(var + LN_EPS)


def _pick_tile(n, cands):
    for c in cands:
        if n % c == 0:
            return c
    raise ValueError(f"no tile for {n}")


def _ada_kernel(c_ref, w_ref, b_ref, o_ref):
    c = c_ref[...]
    a = c * jax.nn.sigmoid(c)
    o_ref[0] = jnp.dot(a, w_ref[0], precision=HI, preferred_element_type=F32) + b_ref[0]


def ada_mod(cc, ada_w, ada_b):
    depth, d, n = ada_w.shape
    rows = cc.shape[0]
    return pl.pallas_call(
        _ada_kernel,
        out_shape=jax.ShapeDtypeStruct((depth, rows, n), F32),
        grid=(depth, n // d),
        in_specs=[pl.BlockSpec((rows, d), lambda l, j: (0, 0)),
                  pl.BlockSpec((1, d, d), lambda l, j: (l, 0, j)),
                  pl.BlockSpec((1, 1, d), lambda l, j: (l, 0, j))],
        out_specs=pl.BlockSpec((1, rows, d), lambda l, j: (l, 0, j)),
        compiler_params=_cparams(("parallel", "parallel"), VMEM_LIMIT),
        name="ada_mod",
    )(cc, ada_w, ada_b.reshape(depth, 1, n))


def _lnmod_kernel(x_ref, sh_ref, sc_ref, o_ref):
    h = _ln(x_ref[0]) * (1.0 + sc_ref[0]) + sh_ref[0]
    o_ref[0] = h.astype(BF16)


def _mod_spec(col, nct, nb):
    return pl.BlockSpec((1, 1, D_MODEL), lambda b, t: (jnp.where(t < nct, nb, b), 0, col))


def lnmod(x_all, mod3, nct, col_shift, col_scale):
    nb, ta, d = x_all.shape
    return pl.pallas_call(
        _lnmod_kernel,
        out_shape=jax.ShapeDtypeStruct((nb, ta, d), BF16),
        grid=(nb, ta // TT),
        in_specs=[pl.BlockSpec((1, TT, d), lambda b, t: (b, t, 0)),
                  _mod_spec(col_shift, nct, nb), _mod_spec(col_scale, nct, nb)],
        out_specs=pl.BlockSpec((1, TT, d), lambda b, t: (b, t, 0)),
        compiler_params=_cparams(("parallel", "parallel")),
        name="lnmod",
    )(x_all, mod3, mod3)


def _mm_kernel(a_ref, b_ref, o_ref):
    o_ref[...] = jnp.dot(a_ref[...], b_ref[...], preferred_element_type=F32).astype(o_ref.dtype)


def matmul(a, b, out_dtype=F32):
    m, k = a.shape
    _, n = b.shape
    tm = _pick_tile(m, (1024, 512, 256))
    tn = _pick_tile(n, (1024, 512, 256))
    return pl.pallas_call(
        _mm_kernel,
        out_shape=jax.ShapeDtypeStruct((m, n), out_dtype),
        grid=(m // tm, n // tn),
        in_specs=[pl.BlockSpec((tm, k), lambda i, j: (i, 0)),
                  pl.BlockSpec((k, tn), lambda i, j: (0, j))],
        out_specs=pl.BlockSpec((tm, tn), lambda i, j: (i, j)),
        compiler_params=_cparams(("parallel", "parallel"), VMEM_LIMIT),
        name="matmul",
    )(a, b)


def _halo_specs(width, ta):
    nblk8 = ta // SUBLANES
    per = TT // SUBLANES
    cur = pl.BlockSpec((1, TT, width), lambda b, t: (b, t, 0))
    prev = pl.BlockSpec((1, SUBLANES, width), lambda b, t: (b, jnp.maximum(t * per - 1, 0), 0))
    nxt = pl.BlockSpec((1, SUBLANES, width), lambda b, t: (b, jnp.minimum((t + 1) * per, nblk8 - 1), 0))
    return [cur, prev, nxt]


def _neighbours(cur, prev_ref, next_ref, nct, nt):
    t = pl.program_id(1)
    seg_start = jnp.logical_or(t == 0, t == nct)
    seg_end = jnp.logical_or(t == nct - 1, t == nt - 1)
    prev_row = prev_ref[0][SUBLANES - 1:SUBLANES, :] * jnp.where(seg_start, 0.0, 1.0)
    next_row = next_ref[0][0:1, :] * jnp.where(seg_end, 0.0, 1.0)
    row = lax.broadcasted_iota(jnp.int32, (TT, 1), 0)
    xm1 = jnp.where(row == 0, prev_row, pltpu.roll(cur, 1, axis=0))
    xp1 = jnp.where(row == TT - 1, next_row, pltpu.roll(cur, TT - 1, axis=0))
    return xm1, xp1


def _rwkv_prep_kernel(nct, nt, cur_ref, prev_ref, next_ref, mu_ref, w0_ref, wup_ref, a0_ref, aup_ref,
                      gup_ref, kk_ref, ka_ref, rk_ref, bd_ref,
                      r_o, kk_o, w0_o, w1_o, k0_o, k1_o, b0_o, b1_o, v_o, g_o, bon_o):
    cur = cur_ref[0]
    xm1, xp1 = _neighbours(cur, prev_ref, next_ref, nct, nt)
    p = cur + mu_ref[...] * (0.5 * (xm1 + xp1) - cur)
    c = BRANCH_W
    r, k, v = p[:, 0:c], p[:, c:2 * c], p[:, 2 * c:3 * c]
    wd = p[:, 3 * c:3 * c + 64]
    ad = p[:, 3 * c + 64:3 * c + 128]
    gd = p[:, 3 * c + 128:3 * c + 256]
    bd = bd_ref[...]
    kk = k * kk_ref[...]
    ss = jnp.dot(kk * kk, bd, precision=HI, preferred_element_type=F32)
    kkn = kk * lax.rsqrt(jnp.maximum(ss, 1e-24))
    twd = jnp.tanh(wd)
    ka = ka_ref[...]
    kdirs = []
    w_outs, k_outs, b_outs = (w0_o, w1_o), (k0_o, k1_o), (b0_o, b1_o)
    for d in range(2):
        wlog = w0_ref[d:d + 1, :] + jnp.dot(twd, wup_ref[d], precision=HI, preferred_element_type=F32)
        decay = -math.exp(-0.5) * jax.nn.sigmoid(wlog)
        a = jax.nn.sigmoid(a0_ref[d:d + 1, :] + jnp.dot(ad, aup_ref[d], precision=HI, preferred_element_type=F32))
        kdir = k * (1.0 + (a - 1.0) * ka)
        bdir = kkn * a
        kdirs.append(kdir)
        w_outs[d][0] = decay
        k_outs[d][0] = kdir
        b_outs[d][0] = bdir
    r_o[0] = r
    kk_o[0] = kkn
    v_o[0] = v
    g_o[0] = jnp.dot(jax.nn.sigmoid(gd), gup_ref[...], precision=HI, preferred_element_type=F32)
    rkk = r * rk_ref[...] * (kdirs[0] + kdirs[1])
    bon_o[0] = jnp.dot(rkk, bd, precision=HI, preferred_element_type=F32) * v


def rwkv_prep(p_rwkv, nct, mu, w0, w_up, a0, a_up, g_up, k_k, k_a, r_k, bd):
    nb, ta, _ = p_rwkv.shape
    nt = ta // TT
    c = BRANCH_W
    nat = jax.ShapeDtypeStruct((nb, ta, c), F32)
    nat_spec = pl.BlockSpec((1, TT, c), lambda b, t: (b, t, 0))

    def full(a):
        nd = a.ndim
        return pl.BlockSpec(a.shape, lambda b, t: (0,) * nd)

    consts = [mu.reshape(1, -1), w0, w_up, a0, a_up, g_up, k_k.reshape(1, -1), k_a.reshape(1, -1),
              r_k.reshape(1, -1), bd]
    return pl.pallas_call(
        functools.partial(_rwkv_prep_kernel, nct, nt),
        out_shape=[nat] * 11,
        grid=(nb, nt),
        in_specs=_halo_specs(RWKV_COLS, ta) + [full(a) for a in consts],
        out_specs=[nat_spec] * 11,
        compiler_params=_cparams(("parallel", "parallel"), VMEM_LIMIT),
        name="rwkv_prep",
    )(p_rwkv, p_rwkv, p_rwkv, *consts)


CHUNK = 16


def _chunk_scan_rows(x, reverse):
    pos = lax.broadcasted_iota(jnp.int32, x.shape, 0) % CHUNK
    step = 1
    while step < CHUNK:
        if reverse:
            x = x + jnp.where(pos < CHUNK - step, pltpu.roll(x, x.shape[0] - step, axis=0), 0.0)
        else:
            x = x + jnp.where(pos >= step, pltpu.roll(x, step, axis=0), 0.0)
        step *= 2
    return x


def _chunk_prep_kernel(r_ref, kk_ref, v_ref, lw0, k0, b0, lw1, k1, b1,
                       a0_o, bm0_o, rp0_o, y00_o, a1_o, bm1_o, rp1_o, y01_o):
    blk = SCAN_BLK
    npair = BRANCH_W // LANES
    nchunk = blk // CHUNK
    r, kk, v = r_ref[0], kk_ref[0], v_ref[0]
    ti = lax.broadcasted_iota(jnp.int32, (blk, blk), 0)
    si = lax.broadcasted_iota(jnp.int32, (blk, blk), 1)
    same = (ti // CHUNK) == (si // CHUNK)
    eye = ti == si
    bd64 = (ti // HEAD_DIM) == (si // HEAD_DIM)
    head0 = si < HEAD_DIM
    lane_half = lax.broadcasted_iota(jnp.int32, (HEAD_DIM, LANES), 1)
    ti2 = lax.broadcasted_iota(jnp.int32, (2 * blk, 2 * blk), 0)
    si2 = lax.broadcasted_iota(jnp.int32, (2 * blk, 2 * blk), 1)
    eye2 = jnp.where(ti2 == si2, 1.0, 0.0)
    zero = jnp.zeros((blk, blk), F32)
    dot = lambda x, y: jnp.dot(x, y, preferred_element_type=F32)
    lo = lambda x: x.astype(BF16)
    split = lambda x: jnp.concatenate([jnp.where(head0, x, 0.0), jnp.where(head0, 0.0, x)], axis=0)
    cat = lambda ms: lo(jnp.concatenate(ms, axis=1))

    probs = []
    for d, (lw_ref, k_ref, b_ref) in enumerate(((lw0, k0, b0), (lw1, k1, b1))):
        reverse = d == 1
        lw, k, b = lw_ref[0], k_ref[0], b_ref[0]
        lg = _chunk_scan_rows(lw, reverse)
        lg_end = lg + _chunk_scan_rows(lw, not reverse) - lw
        g, gi, g_end = jnp.exp(lg), jnp.exp(-lg), jnp.exp(lg_end)
        to_end = jnp.exp(lg_end - lg)
        arrs = (kk * jnp.exp(lg - lw), b * gi, k * gi, r * g, v, k * to_end, b * to_end, g_end)
        incl = jnp.logical_and(same, si >= ti if reverse else si <= ti)
        strict = jnp.logical_and(same, si > ti if reverse else si < ti)
        for p in range(npair):
            lanes = slice(p * LANES, (p + 1) * LANES)
            probs.append(dict(d=d, p=p, incl=incl, strict=strict, arrs=tuple(a[:, lanes] for a in arrs)))

    for q in probs:
        pp_, q_, kt_, rt_ = q["arrs"][:4]
        rhs_g = lo(jnp.concatenate([q_, kt_], axis=0))
        lm, mm, n2, nn = [], [], [], []
        for hh in range(2):
            hm = head0 if hh == 0 else jnp.logical_not(head0)
            lhs_g = lo(jnp.concatenate([jnp.where(hm, pp_, 0.0), jnp.where(hm, rt_, 0.0)], axis=0))
            gm = lax.dot_general(lhs_g, rhs_g, (((1,), (1,)), ((), ())), preferred_element_type=F32)
            lm.append(jnp.where(q["strict"], gm[:blk, :blk], 0.0))
            mm.append(jnp.where(q["strict"], gm[:blk, blk:], 0.0))
            n2.append(jnp.where(q["incl"], gm[blk:, :blk], 0.0))
            nn.append(jnp.where(q["incl"], gm[blk:, blk:], 0.0))
        q["pw"] = jnp.concatenate([jnp.concatenate([lm[0], zero], axis=1), jnp.concatenate([zero, lm[1]], axis=1)], axis=0)
        q["tm"] = eye2 - q["pw"]
        q["m_cat"], q["n2_cat"], q["nn_cat"] = cat(mm), cat(n2), cat(nn)
    for _ in range(CHUNK.bit_length() - 2):
        for q in probs:
            pwl = lo(q["pw"])
            q["pw"] = dot(pwl, pwl)
        for q in probs:
            q["tm"] = dot(lo(q["tm"]), lo(eye2 + q["pw"]))
    for q in probs:
        q["t_cat"] = cat([q["tm"][:blk, :blk], q["tm"][blk:, blk:]])
        q["v_st"] = lo(split(q["arrs"][4]))
        q["pp"] = dot(q["t_cat"], lo(split(q["arrs"][0])))
        q["mv"] = dot(q["m_cat"], q["v_st"])
    for q in probs:
        q["w2"] = dot(q["t_cat"], lo(split(q["mv"])))
        q["rp"] = q["arrs"][3] - dot(q["n2_cat"], lo(split(q["pp"])))
    for q in probs:
        q["y0"] = dot(q["nn_cat"], q["v_st"]) - dot(q["n2_cat"], lo(split(q["w2"])))
    in_chunk = [si // CHUNK == c for c in range(nchunk)]
    for q in probs:
        ppt, vt, w2t = q["pp"].T, q["arrs"][4].T, q["w2"].T
        kg_, qg_ = q["arrs"][5], q["arrs"][6]
        lhs_a = jnp.concatenate([jnp.where(cm, ppt, 0.0) for cm in in_chunk], axis=0)
        q["pq"] = dot(lo(lhs_a), lo(qg_))
        lhs_b = jnp.concatenate([jnp.concatenate([jnp.where(cm, vt, 0.0), jnp.where(cm, -w2t, 0.0)], axis=1)
                                 for cm in in_chunk], axis=0)
        q["bf"] = dot(lo(lhs_b), lo(jnp.concatenate([kg_, qg_], axis=0)))
    outs = ((a0_o, bm0_o, rp0_o, y00_o), (a1_o, bm1_o, rp1_o, y01_o))
    for q in probs:
        a_o, bm_o = outs[q["d"]][:2]
        g_end_p = q["arrs"][7]
        for c in range(nchunk):
            pq_c = q["pq"][c * blk:(c + 1) * blk]
            a_o[0, 0, c, q["p"]] = (jnp.where(eye, g_end_p[c * CHUNK:c * CHUNK + 1], 0.0)
                                    - jnp.where(bd64, pq_c, 0.0)).astype(BF16)
            bm_o[0, 0, c, q["p"]] = jnp.where(lane_half < HEAD_DIM, q["bf"][c * blk:c * blk + HEAD_DIM],
                                              q["bf"][c * blk + HEAD_DIM:(c + 1) * blk])
    for d in range(2):
        rp_o, y0_o = outs[d][2:]
        rp_o[0] = jnp.concatenate([q["rp"] for q in probs if q["d"] == d], axis=1)
        y0_o[0] = jnp.concatenate([q["y0"] for q in probs if q["d"] == d], axis=1)


def chunk_prep(r, kk, v, lw0, lw1, k0, k1, b0, b1):
    nb, ta, c = r.shape
    nblk = ta // SCAN_BLK
    npair = c // LANES
    nchunk = SCAN_BLK // CHUNK
    nat = pl.BlockSpec((1, SCAN_BLK, c), lambda b, s: (b, s, 0))
    a_shape = jax.ShapeDtypeStruct((nb, nblk, nchunk, npair, LANES, LANES), BF16)
    bm_shape = jax.ShapeDtypeStruct((nb, nblk, nchunk, npair, HEAD_DIM, LANES), F32)
    nat_shape = jax.ShapeDtypeStruct((nb, ta, c), F32)
    a_spec = pl.BlockSpec((1, 1, nchunk, npair, LANES, LANES), lambda b, s: (b, s, 0, 0, 0, 0))
    bm_spec = pl.BlockSpec((1, 1, nchunk, npair, HEAD_DIM, LANES), lambda b, s: (b, s, 0, 0, 0, 0))
    return pl.pallas_call(
        _chunk_prep_kernel,
        out_shape=[a_shape, bm_shape, nat_shape, nat_shape] * 2,
        grid=(nb, nblk),
        in_specs=[nat] * 9,
        out_specs=[a_spec, bm_spec, nat, nat] * 2,
        compiler_params=_cparams(("parallel", "parallel"), VMEM_LIMIT),
        name="chunk_prep",
    )(r, kk, v, lw0, k0, b0, lw1, k1, b1)


def _chunk_scan_kernel(nb, a0, bm0, rp0, y00, a1, bm1, rp1, y01, yf_o, yb_o, s_scr):
    step = pl.program_id(0)
    npair = BRANCH_W // LANES
    nchunk = SCAN_BLK // CHUNK

    @pl.when(step == 0)
    def _():
        s_scr[...] = jnp.zeros_like(s_scr)

    lane = lax.broadcasted_iota(jnp.int32, (CHUNK, LANES), 1)
    refs = ((a0, bm0, rp0, y00, yf_o), (a1, bm1, rp1, y01, yb_o))
    for ci in range(nchunk):
        for d in range(2):
            a_ref, bm_ref, rp_ref, y0_ref, y_ref = refs[d]
            c = ci if d == 0 else nchunk - 1 - ci
            rows = slice(c * CHUNK, (c + 1) * CHUNK)
            for b in range(nb):
                for p in range(npair):
                    lanes = slice(p * LANES, (p + 1) * LANES)
                    s = s_scr[d, b, p]
                    rpc = rp_ref[b, rows, lanes]
                    lhs = jnp.concatenate([jnp.where(lane < HEAD_DIM, rpc, 0.0), jnp.where(lane >= HEAD_DIM, rpc, 0.0)],
                                          axis=0)
                    yh = lax.dot_general(lhs, s, (((1,), (1,)), ((), ())), preferred_element_type=F32)
                    y_ref[b, rows, lanes] = jnp.concatenate([yh[:CHUNK], yh[CHUNK:]], axis=1) + y0_ref[b, rows, lanes]
                    s_scr[d, b, p] = (jnp.dot(s.astype(BF16), a_ref[b, 0, c, p], preferred_element_type=F32)
                                      + bm_ref[b, 0, c, p])


def rwkv_scan(r, kk, v, lw0, lw1, k0, k1, b0, b1, lc):
    nb, ta, c = r.shape
    nblk = ta // SCAN_BLK
    nctb = lc // SCAN_BLK
    npair = c // LANES
    nchunk = SCAN_BLK // CHUNK
    a0, bm0, rp0, y00, a1, bm1, rp1, y01 = chunk_prep(r, kk, v, lw0, lw1, k0, k1, b0, b1)

    def fwd(s):
        return s

    def bwd(s):
        return jnp.where(s < nctb, nctb - 1 - s, nblk - 1 - (s - nctb))

    def specs(idx):
        return [pl.BlockSpec((nb, 1, nchunk, npair, LANES, LANES), lambda s: (0, idx(s), 0, 0, 0, 0)),
                pl.BlockSpec((nb, 1, nchunk, npair, HEAD_DIM, LANES), lambda s: (0, idx(s), 0, 0, 0, 0)),
                pl.BlockSpec((nb, SCAN_BLK, c), lambda s: (0, idx(s), 0)),
                pl.BlockSpec((nb, SCAN_BLK, c), lambda s: (0, idx(s), 0))]

    out = jax.ShapeDtypeStruct((nb, ta, c), F32)
    return pl.pallas_call(
        functools.partial(_chunk_scan_kernel, nb),
        out_shape=[out, out],
        grid=(nblk,),
        in_specs=specs(fwd) + specs(bwd),
        out_specs=[pl.BlockSpec((nb, SCAN_BLK, c), lambda s: (0, fwd(s), 0)),
                   pl.BlockSpec((nb, SCAN_BLK, c), lambda s: (0, bwd(s), 0))],
        scratch_shapes=[pltpu.VMEM((2, nb, npair, HEAD_DIM, LANES), F32)],
        compiler_params=_cparams(("arbitrary",), VMEM_LIMIT),
        name="chunk_scan",
    )(a0, bm0, rp0, y00, a1, bm1, rp1, y01)


def _rwkv_out_kernel(yf_ref, yb_ref, g_ref, bon_ref, lg_ref, lb_ref, bd_ref, o_ref):
    y = yf_ref[0] + yb_ref[0]
    bd = bd_ref[...]
    mu = jnp.dot(y, bd, precision=HI, preferred_element_type=F32) * (1.0 / HEAD_DIM)
    yc = y - mu
    var = jnp.dot(yc * yc, bd, precision=HI, preferred_element_type=F32) * (1.0 / HEAD_DIM)
    yn = yc * lax.rsqrt(var + RWKV_GN_EPS) * lg_ref[...] + lb_ref[...]
    o_ref[0] = ((yn + bon_ref[0]) * g_ref[0]).astype(BF16)


def rwkv_out(yf, yb, g, bon, lnx_g, lnx_b, bd):
    nb, ta, _ = yf.shape
    nat_spec = pl.BlockSpec((1, TT, BRANCH_W), lambda b, t: (b, t, 0))
    row = pl.BlockSpec((1, BRANCH_W), lambda b, t: (0, 0))
    return pl.pallas_call(
        _rwkv_out_kernel,
        out_shape=jax.ShapeDtypeStruct((nb, ta, BRANCH_W), BF16),
        grid=(nb, ta // TT),
        in_specs=[nat_spec] * 4 + [row, row, pl.BlockSpec(bd.shape, lambda b, t: (0, 0))],
        out_specs=nat_spec,
        compiler_params=_cparams(("parallel", "parallel")),
        name="rwkv_out",
    )(yf, yb, g, bon, lnx_g.reshape(1, -1), lnx_b.reshape(1, -1), bd)


def _pair_swap(x):
    lane = lax.broadcasted_iota(jnp.int32, x.shape, 1)
    n = x.shape[1]
    return jnp.where(lane % 2 == 0, pltpu.roll(x, n - 1, axis=1), pltpu.roll(x, 1, axis=1))


def _attn_prep_kernel(p_ref, cos_ref, sin_ref, qg_ref, kg_ref, bd_ref, q_o, k_o, v_o):
    p = p_ref[0]
    q, k, v = p[:, :Q_W], p[:, Q_W:Q_W + KV_W], p[:, Q_W + KV_W:]
    bd = bd_ref[...]
    cos, sin = cos_ref[...], sin_ref[...]
    qms = jnp.dot(q * q, bd, precision=HI, preferred_element_type=F32) * (1.0 / HEAD_DIM)
    qn = q * lax.rsqrt(qms + RMS_EPS) * qg_ref[...]
    qr = qn * cos + _pair_swap(qn) * sin
    q_o[0] = (qr * HEAD_DIM ** -0.5).astype(BF16)
    kms = jnp.dot(k * k, bd[:KV_W, :KV_W], precision=HI, preferred_element_type=F32) * (1.0 / HEAD_DIM)
    kn = k * lax.rsqrt(kms + RMS_EPS) * kg_ref[...]
    kr = kn * cos[:, :KV_W] + _pair_swap(kn) * sin[:, :KV_W]
    for g in range(KV_W // HEAD_DIM):
        sl = slice(g * HEAD_DIM, (g + 1) * HEAD_DIM)
        k_o[0, g] = kr[:, sl].astype(BF16)
        v_o[0, g] = v[:, sl].astype(BF16)


def attn_prep(p_attn, cos_t, sin_t, q_norm, k_norm, bd):
    nb, ta, _ = p_attn.shape
    ng = KV_W // HEAD_DIM
    qg = jnp.tile(q_norm, Q_W // HEAD_DIM).reshape(1, -1)
    kg = jnp.tile(k_norm, ng).reshape(1, -1)
    kv_shape = jax.ShapeDtypeStruct((nb, ng, ta, HEAD_DIM), BF16)
    kv_spec = pl.BlockSpec((1, ng, TT, HEAD_DIM), lambda b, t: (b, 0, t, 0))
    return pl.pallas_call(
        _attn_prep_kernel,
        out_shape=[jax.ShapeDtypeStruct((nb, ta, Q_W), BF16), kv_shape, kv_shape],
        grid=(nb, ta // TT),
        in_specs=[pl.BlockSpec((1, TT, ATTN_COLS), lambda b, t: (b, t, 0)),
                  pl.BlockSpec((TT, Q_W), lambda b, t: (t, 0)),
                  pl.BlockSpec((TT, Q_W), lambda b, t: (t, 0)),
                  pl.BlockSpec((1, Q_W), lambda b, t: (0, 0)),
                  pl.BlockSpec((1, KV_W), lambda b, t: (0, 0)),
                  pl.BlockSpec(bd.shape, lambda b, t: (0, 0))],
        out_specs=[pl.BlockSpec((1, TT, Q_W), lambda b, t: (b, t, 0)), kv_spec, kv_spec],
        compiler_params=_cparams(("parallel", "parallel")),
        name="attn_prep",
    )(p_attn, cos_t, sin_t, qg, kg, bd)


def _attn_kernel(nct, lc, q_ref, k_ref, v_ref, o_ref):
    t = pl.program_id(2)

    def run(kk, vv):
        outs = []
        for r in range(2):
            q = q_ref[0][:, r * HEAD_DIM:(r + 1) * HEAD_DIM]
            s = lax.dot_general(q, kk, (((1,), (1,)), ((), ())), preferred_element_type=F32)
            p = jnp.exp(s - jnp.max(s, -1, keepdims=True))
            l = jnp.sum(p, -1, keepdims=True)
            o = jnp.dot(p.astype(BF16), vv, preferred_element_type=F32)
            outs.append(o / l)
        o_ref[0] = jnp.concatenate(outs, axis=1).astype(BF16)

    @pl.when(t < nct)
    def _():
        run(k_ref[0, 0, :lc, :], v_ref[0, 0, :lc, :])

    @pl.when(t >= nct)
    def _():
        run(k_ref[0, 0], v_ref[0, 0])


def attention(q, k, v, lc):
    nb, ta, _ = q.shape
    ng = k.shape[1]
    nct = lc // TT
    qo_spec = pl.BlockSpec((1, TT, 2 * HEAD_DIM), lambda b, g, t: (b, t, g))
    kv_spec = pl.BlockSpec((1, 1, ta, HEAD_DIM), lambda b, g, t: (b, g, 0, 0))
    return pl.pallas_call(
        functools.partial(_attn_kernel, nct, lc),
        out_shape=jax.ShapeDtypeStruct((nb, ta, Q_W), BF16),
        grid=(nb, ng, ta // TT),
        in_specs=[qo_spec, kv_spec, kv_spec],
        out_specs=qo_spec,
        compiler_params=_cparams(("parallel", "parallel", "arbitrary"), VMEM_LIMIT),
        name="attention",
    )(q, k, v)


def _hs_pre_kernel(nct, nt, cur_ref, prev_ref, next_ref, hw_ref, sw_ref, x0_o, u_o, ycv_o):
    cur = cur_ref[0]
    xm1, xp1 = _neighbours(cur, prev_ref, next_ref, nct, nt)
    c = BRANCH_W
    hc = HYENA_COLS
    hw = hw_ref[...]
    ph = hw[0:1] * xm1[:, :hc] + hw[1:2] * cur[:, :hc] + hw[2:3] * xp1[:, :hc]
    x0_o[0] = ph[:, :c]
    u_o[0] = ph[:, c:2 * c] * ph[:, 2 * c:3 * c]
    sw = sw_ref[...]

    def cx(a):
        return a[:, hc + c:hc + 2 * c] * a[:, hc + 2 * c:hc + 3 * c]

    conv = sw[0:1] * cx(xm1) + sw[1:2] * cx(cur) + sw[2:3] * cx(xp1)
    ycv_o[0] = (cur[:, hc:hc + c] * conv).astype(BF16)


def hs_pre(p_hs, nct, hyena_conv, sconv_w):
    nb, ta, w = p_hs.shape
    nt = ta // TT
    nat = pl.BlockSpec((1, TT, BRANCH_W), lambda b, t: (b, t, 0))
    return pl.pallas_call(
        functools.partial(_hs_pre_kernel, nct, nt),
        out_shape=[jax.ShapeDtypeStruct((nb, ta, BRANCH_W), F32),
                   jax.ShapeDtypeStruct((nb, ta, BRANCH_W), F32),
                   jax.ShapeDtypeStruct((nb, ta, BRANCH_W), BF16)],
        grid=(nb, nt),
        in_specs=_halo_specs(w, ta) + [pl.BlockSpec(hyena_conv.shape, lambda b, t: (0, 0)),
                                       pl.BlockSpec(sconv_w.shape, lambda b, t: (0, 0))],
        out_specs=[nat, nat, nat],
        compiler_params=_cparams(("parallel", "parallel"), VMEM_LIMIT),
        name="hs_pre",
    )(p_hs, p_hs, p_hs, hyena_conv, sconv_w)


EMB_PAD = 40


def _filter_tables(lh):
    n = np.arange(2 * lh)
    pos = np.abs(n - (lh - 1)).astype(np.float64)
    bands = (HYENA_EMB - 1) // 2
    t = np.minimum(pos, lh - 1) / (lh - 1)
    wpos = 2.0 * math.pi * pos / lh
    f = np.linspace(1e-4, bands - 1, bands)[:, None]
    z = np.zeros((EMB_PAD, 2 * lh), np.float32)
    z[0] = t
    z[1:1 + bands] = np.cos(f * wpos[None, :])
    z[1 + bands:1 + 2 * bands] = -np.sin(f * wpos[None, :])
    max_decay = math.log(HYENA_TARGET) / HYENA_FAST_DECAY
    min_decay = math.log(HYENA_TARGET) / HYENA_SLOW_DECAY
    deltas = np.abs(np.linspace(min_decay, max_decay, BRANCH_W)).astype(np.float32)
    return z, deltas.reshape(-1, 1)


def _filter_kernel(lh, tn, z_ref, w1_ref, b1_ref, f1_ref, w2_ref, b2_ref, f2_ref, w3_ref, dl_ref, o_ref):
    z = z_ref[...]
    h1 = jnp.sin(f1_ref[...] * (jnp.dot(w1_ref[...], z, precision=HI, preferred_element_type=F32) + b1_ref[...]))
    h2 = jnp.sin(f2_ref[...] * (jnp.dot(w2_ref[...], h1, precision=HI, preferred_element_type=F32) + b2_ref[...]))
    f = jnp.dot(w3_ref[...], h2, precision=HI, preferred_element_type=F32)
    n = pl.program_id(0) * tn + lax.broadcasted_iota(jnp.int32, (1, tn), 1)
    filt = jnp.where(n >= lh - 1, f[:BRANCH_W], f[BRANCH_W:])
    win = jnp.exp(-z[0:1, :] * dl_ref[...])
    o_ref[...] = jnp.where(n == 2 * lh - 1, 0.0, filt * win)


def hyena_filter_table(lh, w1, b1, f1, w2, b2, f2, w3):
    z_np, dl_np = _filter_tables(lh)
    n2 = 2 * lh
    tn = _pick_tile(n2, (1024, 512))
    hd = w2.shape[0]
    w1t = jnp.zeros((hd, EMB_PAD), F32).at[:, :HYENA_EMB].set(w1.T)
    args = [jnp.asarray(z_np), w1t, b1.reshape(-1, 1), f1.reshape(-1, 1), w2.T, b2.reshape(-1, 1),
            f2.reshape(-1, 1), w3.T, jnp.asarray(dl_np)]

    def full(a):
        return pl.BlockSpec(a.shape, lambda j: (0, 0))

    return pl.pallas_call(
        functools.partial(_filter_kernel, lh, tn),
        out_shape=jax.ShapeDtypeStruct((BRANCH_W, n2), F32),
        grid=(n2 // tn,),
        in_specs=[pl.BlockSpec((EMB_PAD, tn), lambda j: (0, j))] + [full(a) for a in args[1:]],
        out_specs=pl.BlockSpec((BRANCH_W, tn), lambda j: (0, j)),
        compiler_params=_cparams(("parallel",)),
        name="hyena_filter",
    )(*args)


def _hyena_conv_kernel(nblk, bp, k_ref, u_ref, o_ref, t_scr):
    ntile = 4 * nblk - 1
    mc = 2 * nblk - 1
    for m in range(ntile):
        win = jnp.concatenate([k_ref[0, m:m + 1, :], k_ref[0, m + 1:m + 2, :]], axis=1)
        x = jnp.broadcast_to(win, (LANES, 2 * LANES))
        t_scr[m] = pltpu.roll(x, LANES + 1, 1, stride=1, stride_axis=0)[:, :LANES].astype(BF16)

    for d in [0] + [s * a for a in range(1, nblk) for s in (1, -1)]:
        m0 = 2 * d + mc
        w = jnp.concatenate([jnp.concatenate([t_scr[m0], t_scr[m0 + 1]], axis=1),
                             jnp.concatenate([t_scr[m0 - 1], t_scr[m0]], axis=1)], axis=0)
        i0, i1 = max(0, d), min(nblk, nblk + d)
        lhs = u_ref[0, (i0 - d) * bp:(i1 - d) * bp, :].astype(BF16)
        res = jnp.dot(lhs, w, preferred_element_type=F32)
        if d == 0:
            o_ref[0] = res
        else:
            o_ref[0, i0 * bp:i1 * bp, :] += res


def hyena_conv(u, ktab):
    nb, l, c = u.shape
    nblk = l // HY_BLK
    bp = -(-nb // SUBLANES) * SUBLANES
    ut = jnp.transpose(u.reshape(nb, nblk, HY_BLK, c), (3, 1, 0, 2))
    if bp != nb:
        ut = jnp.pad(ut, ((0, 0), (0, 0), (0, bp - nb), (0, 0)))
    ut = ut.reshape(c, nblk * bp, HY_BLK)
    k3 = ktab.reshape(c, 4 * nblk, LANES)
    out = pl.pallas_call(
        functools.partial(_hyena_conv_kernel, nblk, bp),
        out_shape=jax.ShapeDtypeStruct((c, nblk * bp, HY_BLK), F32),
        grid=(c,),
        in_specs=[pl.BlockSpec((1, 4 * nblk, LANES), lambda ch: (ch, 0, 0)),
                  pl.BlockSpec((1, nblk * bp, HY_BLK), lambda ch: (ch, 0, 0))],
        out_specs=pl.BlockSpec((1, nblk * bp, HY_BLK), lambda ch: (ch, 0, 0)),
        scratch_shapes=[pltpu.VMEM((4 * nblk - 1, LANES, LANES), BF16)],
        compiler_params=_cparams(("parallel",)),
        name="hyena_conv",
    )(k3, ut)
    out = out.reshape(c, nblk, bp, HY_BLK)[:, :, :nb]
    return jnp.transpose(out, (2, 1, 3, 0)).reshape(nb, l, c)


def _route(logits, bias):
    s = jax.nn.sigmoid(logits)
    sel = s + bias
    srow = [s[e:e + 1] for e in range(N_EXPERTS)]
    row = [sel[e:e + 1] for e in range(N_EXPERTS)]
    best, gi = None, None
    for g in range(N_GROUPS):
        a, b, c, d = row[4 * g:4 * g + 4]
        hi1, lo1, hi2, lo2 = jnp.maximum(a, b), jnp.minimum(a, b), jnp.maximum(c, d), jnp.minimum(c, d)
        score = jnp.maximum(hi1, hi2) + jnp.maximum(jnp.minimum(hi1, hi2), jnp.maximum(lo1, lo2))
        if g == 0:
            best, gi = score, jnp.zeros(score.shape, jnp.int32)
        else:
            better = score > best
            gi = jnp.where(better, g, gi)
            best = jnp.where(better, score, best)
    neg = -jnp.inf
    msel = [jnp.where(gi == e // EXPERTS_PER_GROUP, row[e], neg) for e in range(N_EXPERTS)]

    def arg_first_max(vals):
        bv, bi = vals[0], jnp.zeros(vals[0].shape, jnp.int32)
        for e in range(1, N_EXPERTS):
            better = vals[e] > bv
            bi = jnp.where(better, e, bi)
            bv = jnp.where(better, vals[e], bv)
        return bi

    i1 = arg_first_max(msel)
    i2 = arg_first_max([jnp.where(i1 == e, neg, msel[e]) for e in range(N_EXPERTS)])
    w1 = sum(jnp.where(i1 == e, srow[e], 0.0) for e in range(N_EXPERTS))
    w2 = sum(jnp.where(i2 == e, srow[e], 0.0) for e in range(N_EXPERTS))
    den = w1 + w2
    g1, g2 = w1 / den, w2 / den
    rows = [jnp.where(i1 == e, g1, 0.0) + jnp.where(i2 == e, g2, 0.0) for e in range(N_EXPERTS)]
    rows.append(gi.astype(F32))
    rows.extend([jnp.zeros_like(g1)] * (ROUTE_ROWS - len(rows)))
    return jnp.concatenate(rows, axis=0)


def _merge_kernel(ya_ref, x0_ref, u_ref, yc_ref, ycv_ref, yd_ref, pg_ref, x_ref, ga_ref, shf_ref, scf_ref,
                  skip_ref, wb_ref, wo_ref, g1_ref, b1_ref, rwt_ref, rb_ref, x1_o, hf_o, gates_o):
    yb = (x0_ref[0] * (yc_ref[0] + u_ref[0] * skip_ref[...])).astype(BF16)
    ys = (ya_ref[0], yb, ycv_ref[0], yd_ref[0])
    merged = None
    for n in range(N_BRANCHES):
        gate = jax.nn.sigmoid(pg_ref[0][:, n * D_MODEL:(n + 1) * D_MODEL])
        term = gate * jnp.dot(ys[n], wb_ref[n], preferred_element_type=F32)
        merged = term if merged is None else merged + term
    out = jnp.dot(merged.astype(BF16), wo_ref[...], preferred_element_type=F32)
    alpha = (2 * 2) ** 0.25
    x1 = _ln(alpha * x_ref[0] + ga_ref[0] * out) * g1_ref[...] + b1_ref[...]
    hf = _ln(x1) * (1.0 + scf_ref[0]) + shf_ref[0]
    x1_o[0] = x1
    hf_o[0] = hf.astype(BF16)
    logits = lax.dot_general(rwt_ref[...], hf, (((1,), (1,)), ((), ())), precision=HI,
                             preferred_element_type=F32)
    gates_o[...] = _route(logits, rb_ref[...])


def merge(ya, x0, u, yconv, ycv, yd, pg, x_all, mod3, nct, skip, wb, wo, ln_g, ln_b, rwt, rbias):
    nb, ta, d = x_all.shape
    nt = ta // TT
    nat = pl.BlockSpec((1, TT, BRANCH_W), lambda b, t: (b, t, 0))
    wide = pl.BlockSpec((1, TT, d), lambda b, t: (b, t, 0))

    def full(a):
        nd = a.ndim
        return pl.BlockSpec(a.shape, lambda b, t: (0,) * nd)

    consts = [skip.reshape(1, -1), wb, wo, ln_g.reshape(1, -1), ln_b.reshape(1, -1), rwt, rbias.reshape(-1, 1)]
    return pl.pallas_call(
        _merge_kernel,
        out_shape=[jax.ShapeDtypeStruct((nb, ta, d), F32), jax.ShapeDtypeStruct((nb, ta, d), BF16),
                   jax.ShapeDtypeStruct((ROUTE_ROWS, nb * ta), F32)],
        grid=(nb, nt),
        in_specs=[nat] * 6 + [pl.BlockSpec((1, TT, N_BRANCHES * d), lambda b, t: (b, t, 0)), wide,
                              _mod_spec(2, nct, nb), _mod_spec(3, nct, nb), _mod_spec(4, nct, nb)]
                 + [full(a) for a in consts],
        out_specs=[wide, wide, pl.BlockSpec((ROUTE_ROWS, TT), lambda b, t: (0, b * nt + t))],
        compiler_params=_cparams(("parallel", "parallel"), VMEM_LIMIT),
        name="merge",
    )(ya, x0, u, yconv, ycv, yd, pg, x_all, mod3, mod3, mod3, *consts)


MOE_TILE = 1024
MOE_CHUNK = 3 * LANES
MOE_ALIGN = 2 * SUBLANES
MOE_SORTED = MOE_TILE + LANES
MOE_ROWS = MOE_SORTED + MOE_CHUNK
GID_ROW = N_EXPERTS
ROUTE_ROWS = 3 * SUBLANES
META_LANES = 2 * N_GROUPS
assert N_GROUPS * (MOE_ALIGN - 1) <= MOE_SORTED - MOE_TILE


def _moe_sort_kernel(gt_ref, g_ref, h_ref, up_ref, hs_o, gs_o, pt_o, meta_o):
    gid = gt_ref[GID_ROW:GID_ROW + 1, :]
    onehot = [jnp.where(gid == float(g), 1.0, 0.0) for g in range(N_GROUPS)]
    g4 = jnp.concatenate(onehot + [jnp.zeros((SUBLANES - N_GROUPS, MOE_TILE), F32)], axis=0)
    before = jnp.dot(g4.astype(BF16), up_ref[...], preferred_element_type=F32)
    lane = lax.broadcasted_iota(jnp.int32, (SUBLANES, LANES), 1)
    meta = jnp.zeros((SUBLANES, LANES), F32)
    off = jnp.zeros((1, 1), F32)
    pos = jnp.zeros((1, MOE_TILE), F32)
    for g in range(N_GROUPS):
        cnt = jnp.sum(onehot[g], axis=1, keepdims=True)
        pos = pos + onehot[g] * (before[g:g + 1] + off)
        meta = jnp.where(lane == g, off, meta)
        meta = jnp.where(lane == N_GROUPS + g, cnt, meta)
        off = off + jnp.ceil(cnt * (1.0 / MOE_ALIGN)) * MOE_ALIGN
    meta_o[0] = meta.astype(jnp.int32)
    row = lax.broadcasted_iota(jnp.int32, (MOE_ROWS, MOE_TILE), 0)
    place = jnp.where(row == pos.astype(jnp.int32), 1.0, 0.0)
    p16 = place.astype(BF16)
    hs_o[0] = jnp.dot(p16, h_ref[...], preferred_element_type=F32).astype(BF16)
    gts = g_ref[...]
    hi = gts.astype(BF16)
    r1 = gts - hi.astype(F32)
    mid = r1.astype(BF16)
    low = (r1 - mid.astype(F32)).astype(BF16)
    gs_o[0] = (jnp.dot(p16, hi, preferred_element_type=F32) + jnp.dot(p16, mid, preferred_element_type=F32)
               + jnp.dot(p16, low, preferred_element_type=F32))
    pt_o[...] = place.T.astype(BF16)


def moe_sort(hf, gates_t, gates):
    n, d = hf.shape
    ntile = n // MOE_TILE
    upper = jnp.asarray(np.triu(np.ones((MOE_TILE, MOE_TILE), np.float32), 1), BF16)
    return pl.pallas_call(
        _moe_sort_kernel,
        out_shape=[jax.ShapeDtypeStruct((ntile, MOE_ROWS, d), BF16),
                   jax.ShapeDtypeStruct((ntile, MOE_ROWS, LANES), F32),
                   jax.ShapeDtypeStruct((n, MOE_ROWS), BF16),
                   jax.ShapeDtypeStruct((ntile, SUBLANES, LANES), jnp.int32)],
        grid=(ntile,),
        in_specs=[pl.BlockSpec((ROUTE_ROWS, MOE_TILE), lambda i: (0, i)),
                  pl.BlockSpec((MOE_TILE, LANES), lambda i: (i, 0)),
                  pl.BlockSpec((MOE_TILE, d), lambda i: (i, 0)),
                  pl.BlockSpec((MOE_TILE, MOE_TILE), lambda i: (0, 0))],
        out_specs=[pl.BlockSpec((1, MOE_ROWS, d), lambda i: (i, 0, 0)),
                   pl.BlockSpec((1, MOE_ROWS, LANES), lambda i: (i, 0, 0)),
                   pl.BlockSpec((MOE_TILE, MOE_ROWS), lambda i: (i, 0)),
                   pl.BlockSpec((1, SUBLANES, LANES), lambda i: (i, 0, 0))],
        compiler_params=_cparams(("parallel",), VMEM_LIMIT),
        name="moe_sort",
    )(gates_t, gates, hf, upper)


def _moe_group_kernel(meta_ref, hs_ref, gs_ref, w1_ref, w3_ref, w2_ref, ys_o, ysb_o):
    i, g = pl.program_id(0), pl.program_id(1)

    @pl.when(g == 0)
    def _():
        ys_o[...] = jnp.zeros_like(ys_o)

    off = meta_ref[i * META_LANES + g]
    cnt = meta_ref[i * META_LANES + N_GROUPS + g]
    lane = lax.broadcasted_iota(jnp.int32, (MOE_CHUNK, LANES), 1)

    def chunk(j, carry):
        rows = pl.ds(pl.multiple_of(off + j * MOE_CHUNK, MOE_ALIGN), MOE_CHUNK)
        hs = hs_ref[0, rows, :]
        gs = gs_ref[0, rows, :]
        acc = None
        for e in range(EXPERTS_PER_GROUP):
            a = jnp.dot(hs, w1_ref[e], preferred_element_type=F32)
            b = jnp.dot(hs, w3_ref[e], preferred_element_type=F32)
            act = (a * jax.nn.sigmoid(a)) * b
            gcol = jnp.sum(jnp.where(lane == g * EXPERTS_PER_GROUP + e, gs, 0.0), axis=1, keepdims=True)
            term = gcol * jnp.dot(act.astype(BF16), w2_ref[e], preferred_element_type=F32)
            acc = term if acc is None else acc + term
        ys_o[0, rows, :] += acc
        return carry

    lax.fori_loop(0, lax.div(cnt + (MOE_CHUNK - 1), MOE_CHUNK), chunk, 0)

    @pl.when(g == N_GROUPS - 1)
    def _():
        ysb_o[0] = ys_o[0].astype(BF16)


def moe_group(hs, gs, meta, w1, w3, w2):
    ntile, _, d = hs.shape
    de = w1.shape[2]
    epg = EXPERTS_PER_GROUP
    grid_spec = pltpu.PrefetchScalarGridSpec(
        num_scalar_prefetch=1,
        grid=(ntile, N_GROUPS),
        in_specs=[pl.BlockSpec((1, MOE_ROWS, d), lambda i, g, m: (i, 0, 0)),
                  pl.BlockSpec((1, MOE_ROWS, LANES), lambda i, g, m: (i, 0, 0)),
                  pl.BlockSpec((epg, d, de), lambda i, g, m: (g, 0, 0)),
                  pl.BlockSpec((epg, d, de), lambda i, g, m: (g, 0, 0)),
                  pl.BlockSpec((epg, de, d), lambda i, g, m: (g, 0, 0))],
        out_specs=[pl.BlockSpec((1, MOE_ROWS, d), lambda i, g, m: (i, 0, 0)),
                   pl.BlockSpec((1, MOE_ROWS, d), lambda i, g, m: (i, 0, 0))],
    )
    return pl.pallas_call(
        _moe_group_kernel,
        out_shape=[jax.ShapeDtypeStruct((ntile, MOE_ROWS, d), F32), jax.ShapeDtypeStruct((ntile, MOE_ROWS, d), BF16)],
        grid_spec=grid_spec,
        compiler_params=_cparams(("parallel", "arbitrary"), VMEM_LIMIT),
        name="moe_group",
    )(meta, hs, gs, w1, w3, w2)[1]


def _ln2_kernel(x_ref, pt_ref, ys_ref, gf_ref, g_ref, b_ref, o_ref):
    alpha = (2 * 2) ** 0.25
    f = jnp.dot(pt_ref[...], ys_ref[0], preferred_element_type=F32)
    o_ref[0] = _ln(alpha * x_ref[0] + gf_ref[0] * f) * g_ref[...] + b_ref[...]


def ln2(x1, pt, ysb, mod3, nct, ln_g, ln_b):
    nb, ta, d = x1.shape
    nt = ta // TT
    per = MOE_TILE // TT
    wide = pl.BlockSpec((1, TT, d), lambda b, t: (b, t, 0))
    row = pl.BlockSpec((1, d), lambda b, t: (0, 0))
    return pl.pallas_call(
        _ln2_kernel,
        out_shape=jax.ShapeDtypeStruct((nb, ta, d), F32),
        grid=(nb, nt),
        in_specs=[wide, pl.BlockSpec((TT, MOE_ROWS), lambda b, t: (b * nt + t, 0)),
                  pl.BlockSpec((1, MOE_ROWS, d), lambda b, t: ((b * nt + t) // per, 0, 0)),
                  _mod_spec(5, nct, nb), row, row],
        out_specs=wide,
        compiler_params=_cparams(("parallel", "parallel"), VMEM_LIMIT),
        name="ln2",
    )(x1, pt, ysb, mod3, ln_g.reshape(1, -1), ln_b.reshape(1, -1))


def _rope_tables(l, lc):
    half = HEAD_DIM // 2
    inv = ROPE_THETA ** (-np.arange(0, half, 2, dtype=np.float64) / half)
    t = np.arange(l)
    rows, cols = t // GRID_W, t % GRID_W
    ang = np.concatenate([rows[:, None] * inv, cols[:, None] * inv], -1)
    ang = np.concatenate([np.zeros((lc, half)), ang], 0)
    cos = np.repeat(np.cos(ang), 2, axis=1)
    sin = np.repeat(np.sin(ang), 2, axis=1)
    sin[:, 0::2] *= -1.0
    reps = Q_W // HEAD_DIM
    return (jnp.asarray(np.tile(cos, (1, reps)), F32), jnp.asarray(np.tile(sin, (1, reps)), F32))


def _block_diag_ones():
    i = np.arange(BRANCH_W) // HEAD_DIM
    return jnp.asarray((i[:, None] == i[None, :]).astype(np.float32))


def kernel(x, c, ctx, c_ctx, ada_w, ada_b, w_in, rwkv_mu, rwkv_w0, rwkv_w_up, rwkv_a0, rwkv_a_up, rwkv_g_up, rwkv_k_k, rwkv_k_a, rwkv_r_k, rwkv_lnx_g, rwkv_lnx_b, hyena_conv, hyena_w1, hyena_b1, hyena_freq1, hyena_w2, hyena_b2, hyena_freq2, hyena_w3, hyena_skip, sconv_w, attn_q_norm, attn_k_norm, w_branch, w_out, ln1_g, ln1_b, ln2_g, ln2_b, router_w, router_bias, exp_w1, exp_w3, exp_w2):
    nb, l, d = x.shape
    lc = ctx.shape[1]
    depth = ada_w.shape[0]
    assert d == D_MODEL and lc % TT == 0 and l % TT == 0 and l % GRID_W == 0 and (nb * (lc + l)) % MOE_TILE == 0
    ta = lc + l
    nct = lc // TT
    x_all = jnp.concatenate([ctx, x], axis=1)

    mod_rows = -(-(nb + 1) // SUBLANES) * SUBLANES
    cc = jnp.zeros((mod_rows, d), F32).at[:nb].set(c).at[nb].set(c_ctx)
    mod = ada_mod(cc, ada_w, ada_b)

    cos_t, sin_t = _rope_tables(l, lc)
    bd = _block_diag_ones()
    rwt = router_w.T

    for li in range(depth):
        mod3 = mod[li].reshape(mod_rows, 1, N_MOD * d)
        wl = w_in[li].astype(BF16)
        h = lnmod(x_all, mod3, nct, 0, 1).reshape(nb * ta, d)
        p_rwkv = matmul(h, wl[:, :OFF_HYENA]).reshape(nb, ta, -1)
        p_hs = matmul(h, wl[:, OFF_HYENA:OFF_ATTN]).reshape(nb, ta, -1)
        p_attn = matmul(h, wl[:, OFF_ATTN:OFF_GATE]).reshape(nb, ta, -1)
        p_gate = matmul(h, wl[:, OFF_GATE:]).reshape(nb, ta, -1)

        r, kk, w0, w1, k0, k1, b0, b1, v, g, bon = rwkv_prep(
            p_rwkv, nct, rwkv_mu[li], rwkv_w0[li], rwkv_w_up[li], rwkv_a0[li], rwkv_a_up[li], rwkv_g_up[li],
            rwkv_k_k[li], rwkv_k_a[li], rwkv_r_k[li], bd)
        yf, yb = rwkv_scan(r, kk, v, w0, w1, k0, k1, b0, b1, lc)
        ya = rwkv_out(yf, yb, g, bon, rwkv_lnx_g[li], rwkv_lnx_b[li], bd)

        q, kx, vx = attn_prep(p_attn, cos_t, sin_t, attn_q_norm[li], attn_k_norm[li], bd)
        yd = attention(q, kx, vx, lc)

        x0, u, ycv = hs_pre(p_hs, nct, hyena_conv[li], sconv_w[li])
        fargs = (hyena_w1[li], hyena_b1[li], hyena_freq1[li], hyena_w2[li], hyena_b2[li], hyena_freq2[li],
                 hyena_w3[li])
        yconv = jnp.concatenate([hyena_conv_seg(u[:, :lc], fargs), hyena_conv_seg(u[:, lc:], fargs)], axis=1)

        x1, hf, gates_t = merge(ya, x0, u, yconv, ycv, yd, p_gate, x_all, mod3, nct, hyena_skip[li],
                                w_branch[li].astype(BF16), w_out[li].astype(BF16), ln1_g[li], ln1_b[li],
                                rwt, router_bias)
        gates = jnp.pad(gates_t.T, ((0, 0), (0, LANES - ROUTE_ROWS)))
        hs, gs, pt, meta = moe_sort(hf.reshape(nb * ta, d), gates_t, gates)
        ysb = moe_group(hs, gs, meta[:, 0, :META_LANES].reshape(-1), exp_w1[li].astype(BF16),
                        exp_w3[li].astype(BF16), exp_w2[li].astype(BF16))
        x_all = ln2(x1, pt, ysb, mod3, nct, ln2_g[li], ln2_b[li])
    return x_all[:, lc:]


def hyena_conv_seg(u_seg, fargs):
    ktab = hyena_filter_table(u_seg.shape[1], *fargs)
    return hyena_conv(u_seg, ktab)
```

```python
import functools
import math

import numpy as np
import jax
import jax.numpy as jnp
from jax import lax
from jax.experimental import pallas as pl
from jax.experimental.pallas import tpu as pltpu

F32 = jnp.float32
BF16 = jnp.bfloat16
HI = lax.Precision.HIGHEST

D_MODEL = 1024
GRID_W = 64
BRANCH_W = 256
HEAD_DIM = 64
N_BRANCHES = 4
N_MOD = 6
RWKV_HEADS = 4
RWKV_COLS = 1024
RWKV_GN_EPS = 64e-5
HYENA_COLS = 768
HYENA_EMB = 33
HYENA_FAST_DECAY = 0.3
HYENA_SLOW_DECAY = 1.5
HYENA_TARGET = 1e-2
SCONV_COLS = 768
Q_W = 256
KV_W = 128
ATTN_COLS = 512
ROPE_THETA = 10000.0
RMS_EPS = 1e-6
OFF_HYENA = RWKV_COLS
OFF_SCONV = OFF_HYENA + HYENA_COLS
OFF_ATTN = OFF_SCONV + SCONV_COLS
OFF_GATE = OFF_ATTN + ATTN_COLS
N_EXPERTS = 16
N_GROUPS = 4
EXPERTS_PER_GROUP = 4
D_EXPERT = 512
LN_EPS = 1e-6

SUBLANES = 8
LANES = 128
TT = 256
SCAN_BLK = LANES
HY_BLK = 256
VMEM_LIMIT = 56 * 1024 * 1024


def _cparams(sem, vmem=None):
    return pltpu.CompilerParams(dimension_semantics=sem, vmem_limit_bytes=vmem)


def _ln(xf):
    mu = jnp.mean(xf, -1, keepdims=True)
    xc = xf - mu
    var = jnp.mean(xc * xc, -1, keepdims=True)
    return xc * lax.rsqrt(var + LN_EPS)


def _head_sums(x, ones_bd):
    ones16 = ones_bd.astype(BF16)
    hi = x.astype(BF16)
    lo = (x - hi.astype(F32)).astype(BF16)
    return jnp.dot(hi, ones16, preferred_element_type=F32) + jnp.dot(lo, ones16, preferred_element_type=F32)


def _pick_tile(n, cands):
    for c in cands:
        if n % c == 0:
            return c
    raise ValueError(f"no tile for {n}")


def _ada_kernel(c_ref, w_ref, b_ref, o_ref):
    c = c_ref[...]
    a = c * jax.nn.sigmoid(c)
    o_ref[0] = jnp.dot(a, w_ref[0], precision=HI, preferred_element_type=F32) + b_ref[0]


def ada_mod(cc, ada_w, ada_b):
    depth, d, n = ada_w.shape
    rows = cc.shape[0]
    return pl.pallas_call(
        _ada_kernel,
        out_shape=jax.ShapeDtypeStruct((depth, rows, n), F32),
        grid=(depth, n // d),
        in_specs=[pl.BlockSpec((rows, d), lambda l, j: (0, 0)),
                  pl.BlockSpec((1, d, d), lambda l, j: (l, 0, j)),
                  pl.BlockSpec((1, 1, d), lambda l, j: (l, 0, j))],
        out_specs=pl.BlockSpec((1, rows, d), lambda l, j: (l, 0, j)),
        compiler_params=_cparams(("parallel", "parallel"), VMEM_LIMIT),
        name="ada_mod",
    )(cc, ada_w, ada_b.reshape(depth, 1, n))


def _lnmod_kernel(x_ref, sh_ref, sc_ref, o_ref):
    h = _ln(x_ref[0]) * (1.0 + sc_ref[0]) + sh_ref[0]
    o_ref[0] = h.astype(BF16)


def _mod_spec(col, nct, nb):
    return pl.BlockSpec((1, 1, D_MODEL), lambda b, t: (jnp.where(t < nct, nb, b), 0, col))


def lnmod(x_all, mod3, nct, col_shift, col_scale):
    nb, ta, d = x_all.shape
    return pl.pallas_call(
        _lnmod_kernel,
        out_shape=jax.ShapeDtypeStruct((nb, ta, d), BF16),
        grid=(nb, ta // TT),
        in_specs=[pl.BlockSpec((1, TT, d), lambda b, t: (b, t, 0)),
                  _mod_spec(col_shift, nct, nb), _mod_spec(col_scale, nct, nb)],
        out_specs=pl.BlockSpec((1, TT, d), lambda b, t: (b, t, 0)),
        compiler_params=_cparams(("parallel", "parallel")),
        name="lnmod",
    )(x_all, mod3, mod3)


def _mm_kernel(a_ref, b_ref, o_ref):
    o_ref[...] = jnp.dot(a_ref[...], b_ref[...], preferred_element_type=F32).astype(o_ref.dtype)


def matmul(a, b, out_dtype=F32):
    m, k = a.shape
    _, n = b.shape
    tm = _pick_tile(m, (1024, 512, 256))
    tn = _pick_tile(n, (1024, 512, 256))
    return pl.pallas_call(
        _mm_kernel,
        out_shape=jax.ShapeDtypeStruct((m, n), out_dtype),
        grid=(m // tm, n // tn),
        in_specs=[pl.BlockSpec((tm, k), lambda i, j: (i, 0)),
                  pl.BlockSpec((k, tn), lambda i, j: (0, j))],
        out_specs=pl.BlockSpec((tm, tn), lambda i, j: (i, j)),
        compiler_params=_cparams(("parallel", "parallel"), VMEM_LIMIT),
        name="matmul",
    )(a, b)


def _halo_specs(width, ta):
    nblk8 = ta // SUBLANES
    per = TT // SUBLANES
    cur = pl.BlockSpec((1, TT, width), lambda b, t: (b, t, 0))
    prev = pl.BlockSpec((1, SUBLANES, width), lambda b, t: (b, jnp.maximum(t * per - 1, 0), 0))
    nxt = pl.BlockSpec((1, SUBLANES, width), lambda b, t: (b, jnp.minimum((t + 1) * per, nblk8 - 1), 0))
    return [cur, prev, nxt]


def _neighbours(cur, prev_ref, next_ref, nct, nt):
    t = pl.program_id(1)
    seg_start = jnp.logical_or(t == 0, t == nct)
    seg_end = jnp.logical_or(t == nct - 1, t == nt - 1)
    prev_row = prev_ref[0][SUBLANES - 1:SUBLANES, :] * jnp.where(seg_start, 0.0, 1.0)
    next_row = next_ref[0][0:1, :] * jnp.where(seg_end, 0.0, 1.0)
    row = lax.broadcasted_iota(jnp.int32, (TT, 1), 0)
    xm1 = jnp.where(row == 0, prev_row, pltpu.roll(cur, 1, axis=0))
    xp1 = jnp.where(row == TT - 1, next_row, pltpu.roll(cur, TT - 1, axis=0))
    return xm1, xp1


def _rwkv_prep_kernel(nct, nt, cur_ref, prev_ref, next_ref, mu_ref, w0_ref, wup_ref, a0_ref, aup_ref,
                      gup_ref, kk_ref, ka_ref, rk_ref, bd_ref,
                      r_o, kk_o, w0_o, w1_o, k0_o, k1_o, b0_o, b1_o, v_o, g_o, bon_o):
    cur = cur_ref[0]
    xm1, xp1 = _neighbours(cur, prev_ref, next_ref, nct, nt)
    p = cur + mu_ref[...] * (0.5 * (xm1 + xp1) - cur)
    c = BRANCH_W
    r, k, v = p[:, 0:c], p[:, c:2 * c], p[:, 2 * c:3 * c]
    wd = p[:, 3 * c:3 * c + 64]
    ad = p[:, 3 * c + 64:3 * c + 128]
    gd = p[:, 3 * c + 128:3 * c + 256]
    bd = bd_ref[...]
    kk = k * kk_ref[...]
    ss = _head_sums(kk * kk, bd)
    kkn = kk * lax.rsqrt(jnp.maximum(ss, 1e-24))
    twd = jnp.tanh(wd)
    ka = ka_ref[...]
    kdirs = []
    w_outs, k_outs, b_outs = (w0_o, w1_o), (k0_o, k1_o), (b0_o, b1_o)
    for d in range(2):
        wlog = w0_ref[d:d + 1, :] + jnp.dot(twd, wup_ref[d], precision=HI, preferred_element_type=F32)
        decay = -math.exp(-0.5) * jax.nn.sigmoid(wlog)
        a = jax.nn.sigmoid(a0_ref[d:d + 1, :] + jnp.dot(ad, aup_ref[d], precision=HI, preferred_element_type=F32))
        kdir = k * (1.0 + (a - 1.0) * ka)
        bdir = kkn * a
        kdirs.append(kdir)
        w_outs[d][0] = decay
        k_outs[d][0] = kdir
        b_outs[d][0] = bdir
    r_o[0] = r
    kk_o[0] = kkn
    v_o[0] = v
    g_o[0] = jnp.dot(jax.nn.sigmoid(gd), gup_ref[...], precision=HI, preferred_element_type=F32)
    rkk = r * rk_ref[...] * (kdirs[0] + kdirs[1])
    bon_o[0] = _head_sums(rkk, bd) * v


def rwkv_prep(p_rwkv, nct, mu, w0, w_up, a0, a_up, g_up, k_k, k_a, r_k, bd):
    nb, ta, _ = p_rwkv.shape
    nt = ta // TT
    c = BRANCH_W
    nat = jax.ShapeDtypeStruct((nb, ta, c), F32)
    nat_spec = pl.BlockSpec((1, TT, c), lambda b, t: (b, t, 0))

    def full(a):
        nd = a.ndim
        return pl.BlockSpec(a.shape, lambda b, t: (0,) * nd)

    consts = [mu.reshape(1, -1), w0, w_up, a0, a_up, g_up, k_k.reshape(1, -1), k_a.reshape(1, -1),
              r_k.reshape(1, -1), bd]
    return pl.pallas_call(
        functools.partial(_rwkv_prep_kernel, nct, nt),
        out_shape=[nat] * 11,
        grid=(nb, nt),
        in_specs=_halo_specs(RWKV_COLS, ta) + [full(a) for a in consts],
        out_specs=[nat_spec] * 11,
        compiler_params=_cparams(("parallel", "parallel"), VMEM_LIMIT),
        name="rwkv_prep",
    )(p_rwkv, p_rwkv, p_rwkv, *consts)


CHUNK = 32


def _chunk_scan_rows(x, reverse):
    pos = lax.broadcasted_iota(jnp.int32, x.shape, 0) % CHUNK
    step = 1
    while step < CHUNK:
        if reverse:
            x = x + jnp.where(pos < CHUNK - step, pltpu.roll(x, x.shape[0] - step, axis=0), 0.0)
        else:
            x = x + jnp.where(pos >= step, pltpu.roll(x, step, axis=0), 0.0)
        step *= 2
    return x


def _chunk_prep_kernel(r_ref, kk_ref, v_ref, lw0, k0, b0, lw1, k1, b1,
                       a0_o, bm0_o, rp0_o, y00_o, a1_o, bm1_o, rp1_o, y01_o):
    blk = SCAN_BLK
    npair = BRANCH_W // LANES
    nchunk = blk // CHUNK
    r, kk, v = r_ref[0], kk_ref[0], v_ref[0]
    ti = lax.broadcasted_iota(jnp.int32, (blk, blk), 0)
    si = lax.broadcasted_iota(jnp.int32, (blk, blk), 1)
    same = (ti // CHUNK) == (si // CHUNK)
    eye = ti == si
    bd64 = (ti // HEAD_DIM) == (si // HEAD_DIM)
    head0 = si < HEAD_DIM
    lane_half = lax.broadcasted_iota(jnp.int32, (HEAD_DIM, LANES), 1)
    ti2 = lax.broadcasted_iota(jnp.int32, (2 * blk, 2 * blk), 0)
    si2 = lax.broadcasted_iota(jnp.int32, (2 * blk, 2 * blk), 1)
    eye2 = jnp.where(ti2 == si2, 1.0, 0.0)
    zero = jnp.zeros((blk, blk), F32)
    dot = lambda x, y: jnp.dot(x, y, preferred_element_type=F32)
    lo = lambda x: x.astype(BF16)
    split = lambda x: jnp.concatenate([jnp.where(head0, x, 0.0), jnp.where(head0, 0.0, x)], axis=0)
    cat = lambda ms: lo(jnp.concatenate(ms, axis=1))

    probs = []
    for d, (lw_ref, k_ref, b_ref) in enumerate(((lw0, k0, b0), (lw1, k1, b1))):
        reverse = d == 1
        lw, k, b = lw_ref[0], k_ref[0], b_ref[0]
        lg = _chunk_scan_rows(lw, reverse)
        lg_end = lg + _chunk_scan_rows(lw, not reverse) - lw
        g, gi, g_end = jnp.exp(lg), jnp.exp(-lg), jnp.exp(lg_end)
        to_end = jnp.exp(lg_end - lg)
        arrs = (kk * jnp.exp(lg - lw), b * gi, k * gi, r * g, v, k * to_end, b * to_end, g_end)
        incl = jnp.logical_and(same, si >= ti if reverse else si <= ti)
        strict = jnp.logical_and(same, si > ti if reverse else si < ti)
        for p in range(npair):
            lanes = slice(p * LANES, (p + 1) * LANES)
            probs.append(dict(d=d, p=p, incl=incl, strict=strict, arrs=tuple(a[:, lanes] for a in arrs)))

    for q in probs:
        pp_, q_, kt_, rt_ = q["arrs"][:4]
        rhs_g = lo(jnp.concatenate([q_, kt_], axis=0))
        lm, mm, n2, nn = [], [], [], []
        for hh in range(2):
            hm = head0 if hh == 0 else jnp.logical_not(head0)
            lhs_g = lo(jnp.concatenate([jnp.where(hm, pp_, 0.0), jnp.where(hm, rt_, 0.0)], axis=0))
            gm = lax.dot_general(lhs_g, rhs_g, (((1,), (1,)), ((), ())), preferred_element_type=F32)
            lm.append(jnp.where(q["strict"], gm[:blk, :blk], 0.0))
            mm.append(jnp.where(q["strict"], gm[:blk, blk:], 0.0))
            n2.append(jnp.where(q["incl"], gm[blk:, :blk], 0.0))
            nn.append(jnp.where(q["incl"], gm[blk:, blk:], 0.0))
        q["pw"] = jnp.concatenate([jnp.concatenate([lm[0], zero], axis=1), jnp.concatenate([zero, lm[1]], axis=1)], axis=0)
        q["tm"] = eye2 - q["pw"]
        q["m_cat"], q["n2_cat"], q["nn_cat"] = cat(mm), cat(n2), cat(nn)
    for _ in range(CHUNK.bit_length() - 2):
        for q in probs:
            pwl = lo(q["pw"])
            q["pw"] = dot(pwl, pwl)
        for q in probs:
            q["tm"] = dot(lo(q["tm"]), lo(eye2 + q["pw"]))
    for q in probs:
        q["t_cat"] = cat([q["tm"][:blk, :blk], q["tm"][blk:, blk:]])
        q["v_st"] = lo(split(q["arrs"][4]))
        q["pp"] = dot(q["t_cat"], lo(split(q["arrs"][0])))
        q["mv"] = dot(q["m_cat"], q["v_st"])
    for q in probs:
        q["w2"] = dot(q["t_cat"], lo(split(q["mv"])))
        q["rp"] = q["arrs"][3] - dot(q["n2_cat"], lo(split(q["pp"])))
    for q in probs:
        q["y0"] = dot(q["nn_cat"], q["v_st"]) - dot(q["n2_cat"], lo(split(q["w2"])))
    in_chunk = [si // CHUNK == c for c in range(nchunk)]
    for q in probs:
        ppt, vt, w2t = q["pp"].T, q["arrs"][4].T, q["w2"].T
        kg_, qg_ = q["arrs"][5], q["arrs"][6]
        lhs_a = jnp.concatenate([jnp.where(cm, ppt, 0.0) for cm in in_chunk], axis=0)
        q["pq"] = dot(lo(lhs_a), lo(qg_))
        lhs_b = jnp.concatenate([jnp.concatenate([jnp.where(cm, vt, 0.0), jnp.where(cm, -w2t, 0.0)], axis=1)
                                 for cm in in_chunk], axis=0)
        q["bf"] = dot(lo(lhs_b), lo(jnp.concatenate([kg_, qg_], axis=0)))
    outs = ((a0_o, bm0_o, rp0_o, y00_o), (a1_o, bm1_o, rp1_o, y01_o))
    for q in probs:
        a_o, bm_o = outs[q["d"]][:2]
        g_end_p = q["arrs"][7]
        for c in range(nchunk):
            pq_c = q["pq"][c * blk:(c + 1) * blk]
            a_o[0, 0, c, q["p"]] = (jnp.where(eye, g_end_p[c * CHUNK:c * CHUNK + 1], 0.0)
                                    - jnp.where(bd64, pq_c, 0.0)).astype(BF16)
            bm_o[0, 0, c, q["p"]] = jnp.where(lane_half < HEAD_DIM, q["bf"][c * blk:c * blk + HEAD_DIM],
                                              q["bf"][c * blk + HEAD_DIM:(c + 1) * blk])
    for d in range(2):
        rp_o, y0_o = outs[d][2:]
        rp_o[0] = jnp.concatenate([q["rp"] for q in probs if q["d"] == d], axis=1)
        y0_o[0] = jnp.concatenate([q["y0"] for q in probs if q["d"] == d], axis=1)


def chunk_prep(r, kk, v, lw0, lw1, k0, k1, b0, b1):
    nb, ta, c = r.shape
    nblk = ta // SCAN_BLK
    npair = c // LANES
    nchunk = SCAN_BLK // CHUNK
    nat = pl.BlockSpec((1, SCAN_BLK, c), lambda b, s: (b, s, 0))
    a_shape = jax.ShapeDtypeStruct((nb, nblk, nchunk, npair, LANES, LANES), BF16)
    bm_shape = jax.ShapeDtypeStruct((nb, nblk, nchunk, npair, HEAD_DIM, LANES), F32)
    nat_shape = jax.ShapeDtypeStruct((nb, ta, c), F32)
    a_spec = pl.BlockSpec((1, 1, nchunk, npair, LANES, LANES), lambda b, s: (b, s, 0, 0, 0, 0))
    bm_spec = pl.BlockSpec((1, 1, nchunk, npair, HEAD_DIM, LANES), lambda b, s: (b, s, 0, 0, 0, 0))
    return pl.pallas_call(
        _chunk_prep_kernel,
        out_shape=[a_shape, bm_shape, nat_shape, nat_shape] * 2,
        grid=(nb, nblk),
        in_specs=[nat] * 9,
        out_specs=[a_spec, bm_spec, nat, nat] * 2,
        compiler_params=_cparams(("parallel", "parallel"), VMEM_LIMIT),
        name="chunk_prep",
    )(r, kk, v, lw0, k0, b0, lw1, k1, b1)


def _chunk_scan_kernel(nb, a0, bm0, rp0, y00, a1, bm1, rp1, y01, yf_o, yb_o, s_scr):
    step = pl.program_id(0)
    npair = BRANCH_W // LANES
    nchunk = SCAN_BLK // CHUNK

    @pl.when(step == 0)
    def _():
        s_scr[...] = jnp.zeros_like(s_scr)

    lane = lax.broadcasted_iota(jnp.int32, (CHUNK, LANES), 1)
    refs = ((a0, bm0, rp0, y00, yf_o), (a1, bm1, rp1, y01, yb_o))
    for ci in range(nchunk):
        for d in range(2):
            a_ref, bm_ref, rp_ref, y0_ref, y_ref = refs[d]
            c = ci if d == 0 else nchunk - 1 - ci
            rows = slice(c * CHUNK, (c + 1) * CHUNK)
            for b in range(nb):
                for p in range(npair):
                    lanes = slice(p * LANES, (p + 1) * LANES)
                    s = s_scr[d, b, p]
                    rpc = rp_ref[b, rows, lanes]
                    lhs = jnp.concatenate([jnp.where(lane < HEAD_DIM, rpc, 0.0), jnp.where(lane >= HEAD_DIM, rpc, 0.0)],
                                          axis=0)
                    yh = lax.dot_general(lhs, s, (((1,), (1,)), ((), ())), preferred_element_type=F32)
                    y_ref[b, rows, lanes] = jnp.concatenate([yh[:CHUNK], yh[CHUNK:]], axis=1) + y0_ref[b, rows, lanes]
                    s_scr[d, b, p] = (jnp.dot(s.astype(BF16), a_ref[b, 0, c, p], preferred_element_type=F32)
                                      + bm_ref[b, 0, c, p])


def rwkv_scan(r, kk, v, lw0, lw1, k0, k1, b0, b1, lc):
    nb, ta, c = r.shape
    nblk = ta // SCAN_BLK
    nctb = lc // SCAN_BLK
    npair = c // LANES
    nchunk = SCAN_BLK // CHUNK
    a0, bm0, rp0, y00, a1, bm1, rp1, y01 = chunk_prep(r, kk, v, lw0, lw1, k0, k1, b0, b1)

    def fwd(s):
        return s

    def bwd(s):
        return jnp.where(s < nctb, nctb - 1 - s, nblk - 1 - (s - nctb))

    def specs(idx):
        return [pl.BlockSpec((nb, 1, nchunk, npair, LANES, LANES), lambda s: (0, idx(s), 0, 0, 0, 0)),
                pl.BlockSpec((nb, 1, nchunk, npair, HEAD_DIM, LANES), lambda s: (0, idx(s), 0, 0, 0, 0)),
                pl.BlockSpec((nb, SCAN_BLK, c), lambda s: (0, idx(s), 0)),
                pl.BlockSpec((nb, SCAN_BLK, c), lambda s: (0, idx(s), 0))]

    out = jax.ShapeDtypeStruct((nb, ta, c), F32)
    return pl.pallas_call(
        functools.partial(_chunk_scan_kernel, nb),
        out_shape=[out, out],
        grid=(nblk,),
        in_specs=specs(fwd) + specs(bwd),
        out_specs=[pl.BlockSpec((nb, SCAN_BLK, c), lambda s: (0, fwd(s), 0)),
                   pl.BlockSpec((nb, SCAN_BLK, c), lambda s: (0, bwd(s), 0))],
        scratch_shapes=[pltpu.VMEM((2, nb, npair, HEAD_DIM, LANES), F32)],
        compiler_params=_cparams(("arbitrary",), VMEM_LIMIT),
        name="chunk_scan",
    )(a0, bm0, rp0, y00, a1, bm1, rp1, y01)


def _rwkv_out_kernel(yf_ref, yb_ref, g_ref, bon_ref, lg_ref, lb_ref, bd_ref, o_ref):
    y = yf_ref[0] + yb_ref[0]
    bd = bd_ref[...]
    mu = _head_sums(y, bd) * (1.0 / HEAD_DIM)
    yc = y - mu
    var = _head_sums(yc * yc, bd) * (1.0 / HEAD_DIM)
    yn = yc * lax.rsqrt(var + RWKV_GN_EPS) * lg_ref[...] + lb_ref[...]
    o_ref[0] = ((yn + bon_ref[0]) * g_ref[0]).astype(BF16)


def rwkv_out(yf, yb, g, bon, lnx_g, lnx_b, bd):
    nb, ta, _ = yf.shape
    nat_spec = pl.BlockSpec((1, TT, BRANCH_W), lambda b, t: (b, t, 0))
    row = pl.BlockSpec((1, BRANCH_W), lambda b, t: (0, 0))
    return pl.pallas_call(
        _rwkv_out_kernel,
        out_shape=jax.ShapeDtypeStruct((nb, ta, BRANCH_W), BF16),
        grid=(nb, ta // TT),
        in_specs=[nat_spec] * 4 + [row, row, pl.BlockSpec(bd.shape, lambda b, t: (0, 0))],
        out_specs=nat_spec,
        compiler_params=_cparams(("parallel", "parallel")),
        name="rwkv_out",
    )(yf, yb, g, bon, lnx_g.reshape(1, -1), lnx_b.reshape(1, -1), bd)


def _pair_swap(x):
    lane = lax.broadcasted_iota(jnp.int32, x.shape, 1)
    n = x.shape[1]
    return jnp.where(lane % 2 == 0, pltpu.roll(x, n - 1, axis=1), pltpu.roll(x, 1, axis=1))


def _attn_prep_kernel(p_ref, cos_ref, sin_ref, qg_ref, kg_ref, bd_ref, q_o, k_o, v_o):
    p = p_ref[0]
    q, k, v = p[:, :Q_W], p[:, Q_W:Q_W + KV_W], p[:, Q_W + KV_W:]
    bd = bd_ref[...]
    cos, sin = cos_ref[...], sin_ref[...]
    qms = _head_sums(q * q, bd) * (1.0 / HEAD_DIM)
    qn = q * lax.rsqrt(qms + RMS_EPS) * qg_ref[...]
    qr = qn * cos + _pair_swap(qn) * sin
    q_o[0] = (qr * HEAD_DIM ** -0.5).astype(BF16)
    kms = _head_sums(k * k, bd[:KV_W, :KV_W]) * (1.0 / HEAD_DIM)
    kn = k * lax.rsqrt(kms + RMS_EPS) * kg_ref[...]
    kr = kn * cos[:, :KV_W] + _pair_swap(kn) * sin[:, :KV_W]
    for g in range(KV_W // HEAD_DIM):
        sl = slice(g * HEAD_DIM, (g + 1) * HEAD_DIM)
        k_o[0, g] = kr[:, sl].astype(BF16)
        v_o[0, g] = v[:, sl].astype(BF16)


def attn_prep(p_attn, cos_t, sin_t, q_norm, k_norm, bd):
    nb, ta, _ = p_attn.shape
    ng = KV_W // HEAD_DIM
    qg = jnp.tile(q_norm, Q_W // HEAD_DIM).reshape(1, -1)
    kg = jnp.tile(k_norm, ng).reshape(1, -1)
    kv_shape = jax.ShapeDtypeStruct((nb, ng, ta, HEAD_DIM), BF16)
    kv_spec = pl.BlockSpec((1, ng, TT, HEAD_DIM), lambda b, t: (b, 0, t, 0))
    return pl.pallas_call(
        _attn_prep_kernel,
        out_shape=[jax.ShapeDtypeStruct((nb, ta, Q_W), BF16), kv_shape, kv_shape],
        grid=(nb, ta // TT),
        in_specs=[pl.BlockSpec((1, TT, ATTN_COLS), lambda b, t: (b, t, 0)),
                  pl.BlockSpec((TT, Q_W), lambda b, t: (t, 0)),
                  pl.BlockSpec((TT, Q_W), lambda b, t: (t, 0)),
                  pl.BlockSpec((1, Q_W), lambda b, t: (0, 0)),
                  pl.BlockSpec((1, KV_W), lambda b, t: (0, 0)),
                  pl.BlockSpec(bd.shape, lambda b, t: (0, 0))],
        out_specs=[pl.BlockSpec((1, TT, Q_W), lambda b, t: (b, t, 0)), kv_spec, kv_spec],
        compiler_params=_cparams(("parallel", "parallel")),
        name="attn_prep",
    )(p_attn, cos_t, sin_t, qg, kg, bd)


def _attn_kernel(nct, lc, q_ref, k_ref, v_ref, o_ref):
    t = pl.program_id(2)

    def run(kk, vv):
        outs = []
        for r in range(2):
            q = q_ref[0][:, r * HEAD_DIM:(r + 1) * HEAD_DIM]
            s = lax.dot_general(q, kk, (((1,), (1,)), ((), ())), preferred_element_type=F32)
            p = jnp.exp(s - jnp.max(s, -1, keepdims=True))
            l = jnp.sum(p, -1, keepdims=True)
            o = jnp.dot(p.astype(BF16), vv, preferred_element_type=F32)
            outs.append(o / l)
        o_ref[0] = jnp.concatenate(outs, axis=1).astype(BF16)

    @pl.when(t < nct)
    def _():
        run(k_ref[0, 0, :lc, :], v_ref[0, 0, :lc, :])

    @pl.when(t >= nct)
    def _():
        run(k_ref[0, 0], v_ref[0, 0])


def attention(q, k, v, lc):
    nb, ta, _ = q.shape
    ng = k.shape[1]
    nct = lc // TT
    qo_spec = pl.BlockSpec((1, TT, 2 * HEAD_DIM), lambda b, g, t: (b, t, g))
    kv_spec = pl.BlockSpec((1, 1, ta, HEAD_DIM), lambda b, g, t: (b, g, 0, 0))
    return pl.pallas_call(
        functools.partial(_attn_kernel, nct, lc),
        out_shape=jax.ShapeDtypeStruct((nb, ta, Q_W), BF16),
        grid=(nb, ng, ta // TT),
        in_specs=[qo_spec, kv_spec, kv_spec],
        out_specs=qo_spec,
        compiler_params=_cparams(("parallel", "parallel", "arbitrary"), VMEM_LIMIT),
        name="attention",
    )(q, k, v)


def _hs_pre_kernel(nct, nt, cur_ref, prev_ref, next_ref, hw_ref, sw_ref, x0_o, u_o, ycv_o):
    cur = cur_ref[0]
    xm1, xp1 = _neighbours(cur, prev_ref, next_ref, nct, nt)
    c = BRANCH_W
    hc = HYENA_COLS
    hw = hw_ref[...]
    ph = hw[0:1] * xm1[:, :hc] + hw[1:2] * cur[:, :hc] + hw[2:3] * xp1[:, :hc]
    x0_o[0] = ph[:, :c]
    u_o[0] = ph[:, c:2 * c] * ph[:, 2 * c:3 * c]
    sw = sw_ref[...]

    def cx(a):
        return a[:, hc + c:hc + 2 * c] * a[:, hc + 2 * c:hc + 3 * c]

    conv = sw[0:1] * cx(xm1) + sw[1:2] * cx(cur) + sw[2:3] * cx(xp1)
    ycv_o[0] = (cur[:, hc:hc + c] * conv).astype(BF16)


def hs_pre(p_hs, nct, hyena_conv, sconv_w):
    nb, ta, w = p_hs.shape
    nt = ta // TT
    nat = pl.BlockSpec((1, TT, BRANCH_W), lambda b, t: (b, t, 0))
    return pl.pallas_call(
        functools.partial(_hs_pre_kernel, nct, nt),
        out_shape=[jax.ShapeDtypeStruct((nb, ta, BRANCH_W), F32),
                   jax.ShapeDtypeStruct((nb, ta, BRANCH_W), F32),
                   jax.ShapeDtypeStruct((nb, ta, BRANCH_W), BF16)],
        grid=(nb, nt),
        in_specs=_halo_specs(w, ta) + [pl.BlockSpec(hyena_conv.shape, lambda b, t: (0, 0)),
                                       pl.BlockSpec(sconv_w.shape, lambda b, t: (0, 0))],
        out_specs=[nat, nat, nat],
        compiler_params=_cparams(("parallel", "parallel"), VMEM_LIMIT),
        name="hs_pre",
    )(p_hs, p_hs, p_hs, hyena_conv, sconv_w)


EMB_PAD = 40


def _filter_tables(lh):
    n = np.arange(2 * lh)
    pos = np.abs(n - (lh - 1)).astype(np.float64)
    bands = (HYENA_EMB - 1) // 2
    t = np.minimum(pos, lh - 1) / (lh - 1)
    wpos = 2.0 * math.pi * pos / lh
    f = np.linspace(1e-4, bands - 1, bands)[:, None]
    z = np.zeros((EMB_PAD, 2 * lh), np.float32)
    z[0] = t
    z[1:1 + bands] = np.cos(f * wpos[None, :])
    z[1 + bands:1 + 2 * bands] = -np.sin(f * wpos[None, :])
    max_decay = math.log(HYENA_TARGET) / HYENA_FAST_DECAY
    min_decay = math.log(HYENA_TARGET) / HYENA_SLOW_DECAY
    deltas = np.abs(np.linspace(min_decay, max_decay, BRANCH_W)).astype(np.float32)
    return z, deltas.reshape(-1, 1)


def _filter_kernel(lh, tn, z_ref, w1_ref, b1_ref, f1_ref, w2_ref, b2_ref, f2_ref, w3_ref, dl_ref, o_ref):
    z = z_ref[...]
    h1 = jnp.sin(f1_ref[...] * (jnp.dot(w1_ref[...], z, precision=HI, preferred_element_type=F32) + b1_ref[...]))
    h2 = jnp.sin(f2_ref[...] * (jnp.dot(w2_ref[...], h1, precision=HI, preferred_element_type=F32) + b2_ref[...]))
    f = jnp.dot(w3_ref[...], h2, precision=HI, preferred_element_type=F32)
    n = pl.program_id(0) * tn + lax.broadcasted_iota(jnp.int32, (1, tn), 1)
    filt = jnp.where(n >= lh - 1, f[:BRANCH_W], f[BRANCH_W:])
    win = jnp.exp(-z[0:1, :] * dl_ref[...])
    o_ref[...] = jnp.where(n == 2 * lh - 1, 0.0, filt * win)


def hyena_filter_table(lh, w1, b1, f1, w2, b2, f2, w3):
    z_np, dl_np = _filter_tables(lh)
    n2 = 2 * lh
    tn = _pick_tile(n2, (1024, 512))
    hd = w2.shape[0]
    w1t = jnp.zeros((hd, EMB_PAD), F32).at[:, :HYENA_EMB].set(w1.T)
    args = [jnp.asarray(z_np), w1t, b1.reshape(-1, 1), f1.reshape(-1, 1), w2.T, b2.reshape(-1, 1),
            f2.reshape(-1, 1), w3.T, jnp.asarray(dl_np)]

    def full(a):
        return pl.BlockSpec(a.shape, lambda j: (0, 0))

    return pl.pallas_call(
        functools.partial(_filter_kernel, lh, tn),
        out_shape=jax.ShapeDtypeStruct((BRANCH_W, n2), F32),
        grid=(n2 // tn,),
        in_specs=[pl.BlockSpec((EMB_PAD, tn), lambda j: (0, j))] + [full(a) for a in args[1:]],
        out_specs=pl.BlockSpec((BRANCH_W, tn), lambda j: (0, j)),
        compiler_params=_cparams(("parallel",)),
        name="hyena_filter",
    )(*args)


def _hyena_conv_kernel(nblk, bp, nch, k_ref, u_ref, o_ref, t_scr):
    ntile = 4 * nblk - 1
    mc = 2 * nblk - 1
    for ch in range(nch):
        for m in range(ntile):
            win = jnp.concatenate([k_ref[ch, m:m + 1, :], k_ref[ch, m + 1:m + 2, :]], axis=1)
            x = jnp.broadcast_to(win, (LANES, 2 * LANES))
            t_scr[ch, m] = pltpu.roll(x, LANES + 1, 1, stride=1, stride_axis=0)[:, :LANES].astype(BF16)

    for ch in range(nch):
        for d in [0] + [s * a for a in range(1, nblk) for s in (1, -1)]:
            m0 = 2 * d + mc
            w = jnp.concatenate([jnp.concatenate([t_scr[ch, m0], t_scr[ch, m0 + 1]], axis=1),
                                 jnp.concatenate([t_scr[ch, m0 - 1], t_scr[ch, m0]], axis=1)], axis=0)
            i0, i1 = max(0, d), min(nblk, nblk + d)
            lhs = u_ref[ch, (i0 - d) * bp:(i1 - d) * bp, :].astype(BF16)
            res = jnp.dot(lhs, w, preferred_element_type=F32)
            if d == 0:
                o_ref[ch] = res
            else:
                o_ref[ch, i0 * bp:i1 * bp, :] += res


def hyena_conv(u, ktab):
    nb, l, c = u.shape
    nblk = l // HY_BLK
    bp = -(-nb // SUBLANES) * SUBLANES
    nch = SUBLANES if nblk == 1 else 1
    ut = jnp.transpose(u.reshape(nb, nblk, HY_BLK, c), (3, 1, 0, 2))
    if bp != nb:
        ut = jnp.pad(ut, ((0, 0), (0, 0), (0, bp - nb), (0, 0)))
    ut = ut.reshape(c, nblk * bp, HY_BLK)
    k3 = ktab.reshape(c, 4 * nblk, LANES)
    out = pl.pallas_call(
        functools.partial(_hyena_conv_kernel, nblk, bp, nch),
        out_shape=jax.ShapeDtypeStruct((c, nblk * bp, HY_BLK), F32),
        grid=(c // nch,),
        in_specs=[pl.BlockSpec((nch, 4 * nblk, LANES), lambda ch: (ch, 0, 0)),
                  pl.BlockSpec((nch, nblk * bp, HY_BLK), lambda ch: (ch, 0, 0))],
        out_specs=pl.BlockSpec((nch, nblk * bp, HY_BLK), lambda ch: (ch, 0, 0)),
        scratch_shapes=[pltpu.VMEM((nch, 4 * nblk - 1, LANES, LANES), BF16)],
        compiler_params=_cparams(("parallel",)),
        name="hyena_conv",
    )(k3, ut)
    out = out.reshape(c, nblk, bp, HY_BLK)[:, :, :nb]
    return jnp.transpose(out, (2, 1, 3, 0)).reshape(nb, l, c)


def _route(logits, bias):
    s = jax.nn.sigmoid(logits)
    sel = s + bias
    srow = [s[e:e + 1] for e in range(N_EXPERTS)]
    row = [sel[e:e + 1] for e in range(N_EXPERTS)]
    best, gi = None, None
    for g in range(N_GROUPS):
        a, b, c, d = row[4 * g:4 * g + 4]
        hi1, lo1, hi2, lo2 = jnp.maximum(a, b), jnp.minimum(a, b), jnp.maximum(c, d), jnp.minimum(c, d)
        score = jnp.maximum(hi1, hi2) + jnp.maximum(jnp.minimum(hi1, hi2), jnp.maximum(lo1, lo2))
        if g == 0:
            best, gi = score, jnp.zeros(score.shape, jnp.int32)
        else:
            better = score > best
            gi = jnp.where(better, g, gi)
            best = jnp.where(better, score, best)
    neg = -jnp.inf
    msel = [jnp.where(gi == e // EXPERTS_PER_GROUP, row[e], neg) for e in range(N_EXPERTS)]

    def arg_first_max(vals):
        bv, bi = vals[0], jnp.zeros(vals[0].shape, jnp.int32)
        for e in range(1, N_EXPERTS):
            better = vals[e] > bv
            bi = jnp.where(better, e, bi)
            bv = jnp.where(better, vals[e], bv)
        return bi

    i1 = arg_first_max(msel)
    i2 = arg_first_max([jnp.where(i1 == e, neg, msel[e]) for e in range(N_EXPERTS)])
    w1 = sum(jnp.where(i1 == e, srow[e], 0.0) for e in range(N_EXPERTS))
    w2 = sum(jnp.where(i2 == e, srow[e], 0.0) for e in range(N_EXPERTS))
    den = w1 + w2
    g1, g2 = w1 / den, w2 / den
    rows = [jnp.where(i1 == e, g1, 0.0) + jnp.where(i2 == e, g2, 0.0) for e in range(N_EXPERTS)]
    rows.append(gi.astype(F32))
    rows.extend([jnp.zeros_like(g1)] * (ROUTE_ROWS - len(rows)))
    return jnp.concatenate(rows, axis=0)


def _merge_kernel(ya_ref, x0_ref, u_ref, yc_ref, ycv_ref, yd_ref, pg_ref, x_ref, ga_ref, shf_ref, scf_ref,
                  skip_ref, wb_ref, wo_ref, g1_ref, b1_ref, rwt_ref, rb_ref, x1_o, hf_o, gates_o):
    yb = (x0_ref[0] * (yc_ref[0] + u_ref[0] * skip_ref[...])).astype(BF16)
    ys = (ya_ref[0], yb, ycv_ref[0], yd_ref[0])
    merged = None
    for n in range(N_BRANCHES):
        gate = jax.nn.sigmoid(pg_ref[0][:, n * D_MODEL:(n + 1) * D_MODEL])
        term = gate * jnp.dot(ys[n], wb_ref[n], preferred_element_type=F32)
        merged = term if merged is None else merged + term
    out = jnp.dot(merged.astype(BF16), wo_ref[...], preferred_element_type=F32)
    alpha = (2 * 2) ** 0.25
    x1 = _ln(alpha * x_ref[0] + ga_ref[0] * out) * g1_ref[...] + b1_ref[...]
    hf = _ln(x1) * (1.0 + scf_ref[0]) + shf_ref[0]
    x1_o[0] = x1
    hf_o[0] = hf.astype(BF16)
    logits = lax.dot_general(rwt_ref[...], hf, (((1,), (1,)), ((), ())), precision=HI,
                             preferred_element_type=F32)
    gates_o[...] = _route(logits, rb_ref[...])


def merge(ya, x0, u, yconv, ycv, yd, pg, x_all, mod3, nct, skip, wb, wo, ln_g, ln_b, rwt, rbias):
    nb, ta, d = x_all.shape
    nt = ta // TT
    nat = pl.BlockSpec((1, TT, BRANCH_W), lambda b, t: (b, t, 0))
    wide = pl.BlockSpec((1, TT, d), lambda b, t: (b, t, 0))

    def full(a):
        nd = a.ndim
        return pl.BlockSpec(a.shape, lambda b, t: (0,) * nd)

    consts = [skip.reshape(1, -1), wb, wo, ln_g.reshape(1, -1), ln_b.reshape(1, -1), rwt, rbias.reshape(-1, 1)]
    return pl.pallas_call(
        _merge_kernel,
        out_shape=[jax.ShapeDtypeStruct((nb, ta, d), F32), jax.ShapeDtypeStruct((nb, ta, d), BF16),
                   jax.ShapeDtypeStruct((ROUTE_ROWS, nb * ta), F32)],
        grid=(nb, nt),
        in_specs=[nat] * 6 + [pl.BlockSpec((1, TT, N_BRANCHES * d), lambda b, t: (b, t, 0)), wide,
                              _mod_spec(2, nct, nb), _mod_spec(3, nct, nb), _mod_spec(4, nct, nb)]
                 + [full(a) for a in consts],
        out_specs=[wide, wide, pl.BlockSpec((ROUTE_ROWS, TT), lambda b, t: (0, b * nt + t))],
        compiler_params=_cparams(("parallel", "parallel"), VMEM_LIMIT),
        name="merge",
    )(ya, x0, u, yconv, ycv, yd, pg, x_all, mod3, mod3, mod3, *consts)


MOE_TILE = 1024
MOE_CHUNK = 3 * LANES
MOE_ALIGN = 2 * SUBLANES
MOE_SORTED = MOE_TILE + LANES
MOE_ROWS = MOE_SORTED + MOE_CHUNK
GID_ROW = N_EXPERTS
ROUTE_ROWS = 3 * SUBLANES
META_LANES = 2 * N_GROUPS
assert N_GROUPS * (MOE_ALIGN - 1) <= MOE_SORTED - MOE_TILE


def _moe_sort_kernel(gt_ref, g_ref, h_ref, up_ref, hs_o, gs_o, pt_o, meta_o):
    gid = gt_ref[GID_ROW:GID_ROW + 1, :]
    onehot = [jnp.where(gid == float(g), 1.0, 0.0) for g in range(N_GROUPS)]
    g4 = jnp.concatenate(onehot + [jnp.zeros((SUBLANES - N_GROUPS, MOE_TILE), F32)], axis=0)
    before = jnp.dot(g4.astype(BF16), up_ref[...], preferred_element_type=F32)
    lane = lax.broadcasted_iota(jnp.int32, (SUBLANES, LANES), 1)
    meta = jnp.zeros((SUBLANES, LANES), F32)
    off = jnp.zeros((1, 1), F32)
    pos = jnp.zeros((1, MOE_TILE), F32)
    for g in range(N_GROUPS):
        cnt = jnp.sum(onehot[g], axis=1, keepdims=True)
        pos = pos + onehot[g] * (before[g:g + 1] + off)
        meta = jnp.where(lane == g, off, meta)
        meta = jnp.where(lane == N_GROUPS + g, cnt, meta)
        off = off + jnp.ceil(cnt * (1.0 / MOE_ALIGN)) * MOE_ALIGN
    meta_o[0] = meta.astype(jnp.int32)
    row = lax.broadcasted_iota(jnp.int32, (MOE_ROWS, MOE_TILE), 0)
    place = jnp.where(row == pos.astype(jnp.int32), 1.0, 0.0)
    p16 = place.astype(BF16)
    hs_o[0] = jnp.dot(p16, h_ref[...], preferred_element_type=F32).astype(BF16)
    gts = g_ref[...]
    hi = gts.astype(BF16)
    r1 = gts - hi.astype(F32)
    mid = r1.astype(BF16)
    low = (r1 - mid.astype(F32)).astype(BF16)
    gs_o[0] = (jnp.dot(p16, hi, preferred_element_type=F32) + jnp.dot(p16, mid, preferred_element_type=F32)
               + jnp.dot(p16, low, preferred_element_type=F32))
    pt_o[...] = place.T.astype(BF16)


def moe_sort(hf, gates_t, gates):
    n, d = hf.shape
    ntile = n // MOE_TILE
    upper = jnp.asarray(np.triu(np.ones((MOE_TILE, MOE_TILE), np.float32), 1), BF16)
    return pl.pallas_call(
        _moe_sort_kernel,
        out_shape=[jax.ShapeDtypeStruct((ntile, MOE_ROWS, d), BF16),
                   jax.ShapeDtypeStruct((ntile, MOE_ROWS, LANES), F32),
                   jax.ShapeDtypeStruct((n, MOE_ROWS), BF16),
                   jax.ShapeDtypeStruct((ntile, SUBLANES, LANES), jnp.int32)],
        grid=(ntile,),
        in_specs=[pl.BlockSpec((ROUTE_ROWS, MOE_TILE), lambda i: (0, i)),
                  pl.BlockSpec((MOE_TILE, LANES), lambda i: (i, 0)),
                  pl.BlockSpec((MOE_TILE, d), lambda i: (i, 0)),
                  pl.BlockSpec((MOE_TILE, MOE_TILE), lambda i: (0, 0))],
        out_specs=[pl.BlockSpec((1, MOE_ROWS, d), lambda i: (i, 0, 0)),
                   pl.BlockSpec((1, MOE_ROWS, LANES), lambda i: (i, 0, 0)),
                   pl.BlockSpec((MOE_TILE, MOE_ROWS), lambda i: (i, 0)),
                   pl.BlockSpec((1, SUBLANES, LANES), lambda i: (i, 0, 0))],
        compiler_params=_cparams(("parallel",), VMEM_LIMIT),
        name="moe_sort",
    )(gates_t, gates, hf, upper)


def _moe_group_kernel(meta_ref, hs_ref, gs_ref, w1_ref, w3_ref, w2_ref, ys_o, ysb_o):
    i, g = pl.program_id(0), pl.program_id(1)

    @pl.when(g == 0)
    def _():
        ys_o[...] = jnp.zeros_like(ys_o)

    off = meta_ref[i * META_LANES + g]
    cnt = meta_ref[i * META_LANES + N_GROUPS + g]
    lane = lax.broadcasted_iota(jnp.int32, (MOE_CHUNK, LANES), 1)

    def chunk(j, carry):
        rows = pl.ds(pl.multiple_of(off + j * MOE_CHUNK, MOE_ALIGN), MOE_CHUNK)
        hs = hs_ref[0, rows, :]
        gs = gs_ref[0, rows, :]
        acc = None
        for e in range(EXPERTS_PER_GROUP):
            a = jnp.dot(hs, w1_ref[e], preferred_element_type=F32)
            b = jnp.dot(hs, w3_ref[e], preferred_element_type=F32)
            act = (a * jax.nn.sigmoid(a)) * b
            gcol = jnp.sum(jnp.where(lane == g * EXPERTS_PER_GROUP + e, gs, 0.0), axis=1, keepdims=True)
            term = gcol * jnp.dot(act.astype(BF16), w2_ref[e], preferred_element_type=F32)
            acc = term if acc is None else acc + term
        ys_o[0, rows, :] += acc
        return carry

    lax.fori_loop(0, lax.div(cnt + (MOE_CHUNK - 1), MOE_CHUNK), chunk, 0)

    @pl.when(g == N_GROUPS - 1)
    def _():
        ysb_o[0] = ys_o[0].astype(BF16)


def moe_group(hs, gs, meta, w1, w3, w2):
    ntile, _, d = hs.shape
    de = w1.shape[2]
    epg = EXPERTS_PER_GROUP
    grid_spec = pltpu.PrefetchScalarGridSpec(
        num_scalar_prefetch=1,
        grid=(ntile, N_GROUPS),
        in_specs=[pl.BlockSpec((1, MOE_ROWS, d), lambda i, g, m: (i, 0, 0)),
                  pl.BlockSpec((1, MOE_ROWS, LANES), lambda i, g, m: (i, 0, 0)),
                  pl.BlockSpec((epg, d, de), lambda i, g, m: (g, 0, 0)),
                  pl.BlockSpec((epg, d, de), lambda i, g, m: (g, 0, 0)),
                  pl.BlockSpec((epg, de, d), lambda i, g, m: (g, 0, 0))],
        out_specs=[pl.BlockSpec((1, MOE_ROWS, d), lambda i, g, m: (i, 0, 0)),
                   pl.BlockSpec((1, MOE_ROWS, d), lambda i, g, m: (i, 0, 0))],
    )
    return pl.pallas_call(
        _moe_group_kernel,
        out_shape=[jax.ShapeDtypeStruct((ntile, MOE_ROWS, d), F32), jax.ShapeDtypeStruct((ntile, MOE_ROWS, d), BF16)],
        grid_spec=grid_spec,
        compiler_params=_cparams(("parallel", "arbitrary"), VMEM_LIMIT),
        name="moe_group",
    )(meta, hs, gs, w1, w3, w2)[1]


def _ln2_kernel(x_ref, pt_ref, ys_ref, gf_ref, g_ref, b_ref, o_ref):
    alpha = (2 * 2) ** 0.25
    f = jnp.dot(pt_ref[...], ys_ref[0], preferred_element_type=F32)
    o_ref[0] = _ln(alpha * x_ref[0] + gf_ref[0] * f) * g_ref[...] + b_ref[...]


def ln2(x1, pt, ysb, mod3, nct, ln_g, ln_b, latent_only):
    nb, ta, d = x1.shape
    nt = ta // TT
    per = MOE_TILE // TT
    wide = pl.BlockSpec((1, TT, d), lambda b, t: (b, t, 0))
    row = pl.BlockSpec((1, d), lambda b, t: (0, 0))
    out_rows, out_spec = ta, wide
    if latent_only:
        out_rows, out_spec = ta - nct * TT, pl.BlockSpec((1, TT, d), lambda b, t: (b, jnp.maximum(t - nct, 0), 0))
    return pl.pallas_call(
        _ln2_kernel,
        out_shape=jax.ShapeDtypeStruct((nb, out_rows, d), F32),
        grid=(nb, nt),
        in_specs=[wide, pl.BlockSpec((TT, MOE_ROWS), lambda b, t: (b * nt + t, 0)),
                  pl.BlockSpec((1, MOE_ROWS, d), lambda b, t: ((b * nt + t) // per, 0, 0)),
                  _mod_spec(5, nct, nb), row, row],
        out_specs=out_spec,
        compiler_params=_cparams(("parallel", "arbitrary"), VMEM_LIMIT),
        name="ln2",
    )(x1, pt, ysb, mod3, ln_g.reshape(1, -1), ln_b.reshape(1, -1))


def _rope_tables(l, lc):
    half = HEAD_DIM // 2
    inv = ROPE_THETA ** (-np.arange(0, half, 2, dtype=np.float64) / half)
    t = np.arange(l)
    rows, cols = t // GRID_W, t % GRID_W
    ang = np.concatenate([rows[:, None] * inv, cols[:, None] * inv], -1)
    ang = np.concatenate([np.zeros((lc, half)), ang], 0)
    cos = np.repeat(np.cos(ang), 2, axis=1)
    sin = np.repeat(np.sin(ang), 2, axis=1)
    sin[:, 0::2] *= -1.0
    reps = Q_W // HEAD_DIM
    return (jnp.asarray(np.tile(cos, (1, reps)), F32), jnp.asarray(np.tile(sin, (1, reps)), F32))


def _block_diag_ones():
    i = np.arange(BRANCH_W) // HEAD_DIM
    return jnp.asarray((i[:, None] == i[None, :]).astype(np.float32))


def kernel(x, c, ctx, c_ctx, ada_w, ada_b, w_in, rwkv_mu, rwkv_w0, rwkv_w_up, rwkv_a0, rwkv_a_up, rwkv_g_up, rwkv_k_k, rwkv_k_a, rwkv_r_k, rwkv_lnx_g, rwkv_lnx_b, hyena_conv, hyena_w1, hyena_b1, hyena_freq1, hyena_w2, hyena_b2, hyena_freq2, hyena_w3, hyena_skip, sconv_w, attn_q_norm, attn_k_norm, w_branch, w_out, ln1_g, ln1_b, ln2_g, ln2_b, router_w, router_bias, exp_w1, exp_w3, exp_w2):
    nb, l, d = x.shape
    lc = ctx.shape[1]
    depth = ada_w.shape[0]
    assert d == D_MODEL and lc % TT == 0 and l % TT == 0 and l % GRID_W == 0 and (nb * (lc + l)) % MOE_TILE == 0
    ta = lc + l
    nct = lc // TT
    x_all = jnp.concatenate([ctx, x], axis=1)

    mod_rows = -(-(nb + 1) // SUBLANES) * SUBLANES
    cc = jnp.zeros((mod_rows, d), F32).at[:nb].set(c).at[nb].set(c_ctx)
    mod = ada_mod(cc, ada_w, ada_b)

    cos_t, sin_t = _rope_tables(l, lc)
    bd = _block_diag_ones()
    rwt = router_w.T

    for li in range(depth):
        mod3 = mod[li].reshape(mod_rows, 1, N_MOD * d)
        wl = w_in[li].astype(BF16)
        h = lnmod(x_all, mod3, nct, 0, 1).reshape(nb * ta, d)
        p_rwkv = matmul(h, wl[:, :OFF_HYENA]).reshape(nb, ta, -1)
        p_hs = matmul(h, wl[:, OFF_HYENA:OFF_ATTN]).reshape(nb, ta, -1)
        p_attn = matmul(h, wl[:, OFF_ATTN:OFF_GATE]).reshape(nb, ta, -1)
        p_gate = matmul(h, wl[:, OFF_GATE:]).reshape(nb, ta, -1)

        r, kk, w0, w1, k0, k1, b0, b1, v, g, bon = rwkv_prep(
            p_rwkv, nct, rwkv_mu[li], rwkv_w0[li], rwkv_w_up[li], rwkv_a0[li], rwkv_a_up[li], rwkv_g_up[li],
            rwkv_k_k[li], rwkv_k_a[li], rwkv_r_k[li], bd)
        yf, yb = rwkv_scan(r, kk, v, w0, w1, k0, k1, b0, b1, lc)
        ya = rwkv_out(yf, yb, g, bon, rwkv_lnx_g[li], rwkv_lnx_b[li], bd)

        q, kx, vx = attn_prep(p_attn, cos_t, sin_t, attn_q_norm[li], attn_k_norm[li], bd)
        yd = attention(q, kx, vx, lc)

        x0, u, ycv = hs_pre(p_hs, nct, hyena_conv[li], sconv_w[li])
        fargs = (hyena_w1[li], hyena_b1[li], hyena_freq1[li], hyena_w2[li], hyena_b2[li], hyena_freq2[li],
                 hyena_w3[li])
        yconv_ctx = hyena_conv_seg(u[:, :lc], fargs) if li < depth - 1 else jnp.zeros((nb, lc, BRANCH_W), F32)
        yconv = jnp.concatenate([yconv_ctx, hyena_conv_seg(u[:, lc:], fargs)], axis=1)

        x1, hf, gates_t = merge(ya, x0, u, yconv, ycv, yd, p_gate, x_all, mod3, nct, hyena_skip[li],
                                w_branch[li].astype(BF16), w_out[li].astype(BF16), ln1_g[li], ln1_b[li],
                                rwt, router_bias)
        gates = jnp.pad(gates_t.T, ((0, 0), (0, LANES - ROUTE_ROWS)))
        hs, gs, pt, meta = moe_sort(hf.reshape(nb * ta, d), gates_t, gates)
        ysb = moe_group(hs, gs, meta[:, 0, :META_LANES].reshape(-1), exp_w1[li].astype(BF16),
                        exp_w3[li].astype(BF16), exp_w2[li].astype(BF16))
        x_all = ln2(x1, pt, ysb, mod3, nct, ln2_g[li], ln2_b[li], latent_only=li == depth - 1)
    return x_all


def hyena_conv_seg(u_seg, fargs):
    ktab = hyena_filter_table(u_seg.shape[1], *fargs)
    return hyena_conv(u_seg, ktab)
```

```python
import functools
import math

import numpy as np
import jax
import jax.numpy as jnp
from jax import lax
from jax.experimental import pallas as pl
from jax.experimental.pallas import tpu as pltpu

F32 = jnp.float32
BF16 = jnp.bfloat16
HI = lax.Precision.HIGHEST

D_MODEL = 1024
GRID_W = 64
BRANCH_W = 256
HEAD_DIM = 64
N_BRANCHES = 4
N_MOD = 6
RWKV_HEADS = 4
RWKV_COLS = 1024
RWKV_GN_EPS = 64e-5
HYENA_COLS = 768
HYENA_EMB = 33
HYENA_FAST_DECAY = 0.3
HYENA_SLOW_DECAY = 1.5
HYENA_TARGET = 1e-2
SCONV_COLS = 768
Q_W = 256
KV_W = 128
ATTN_COLS = 512
ROPE_THETA = 10000.0
RMS_EPS = 1e-6
OFF_HYENA = RWKV_COLS
OFF_SCONV = OFF_HYENA + HYENA_COLS
OFF_ATTN = OFF_SCONV + SCONV_COLS
OFF_GATE = OFF_ATTN + ATTN_COLS
N_EXPERTS = 16
N_GROUPS = 4
EXPERTS_PER_GROUP = 4
D_EXPERT = 512
LN_EPS = 1e-6

SUBLANES = 8
LANES = 128
TT = 256
SCAN_BLK = LANES
HY_BLK = 256
VMEM_LIMIT = 56 * 1024 * 1024


def _cparams(sem, vmem=None):
    return pltpu.CompilerParams(dimension_semantics=sem, vmem_limit_bytes=vmem)


def _ln(xf):
    mu = jnp.mean(xf, -1, keepdims=True)
    xc = xf - mu
    var = jnp.mean(xc * xc, -1, keepdims=True)
    return xc * lax.rsqrt(var + LN_EPS)


def _head_sums(x, ones_bd):
    ones16 = ones_bd.astype(BF16)
    hi = x.astype(BF16)
    lo = (x - hi.astype(F32)).astype(BF16)
    return jnp.dot(hi, ones16, preferred_element_type=F32) + jnp.dot(lo, ones16, preferred_element_type=F32)


def _pick_tile(n, cands):
    for c in cands:
        if n % c == 0:
            return c
    raise ValueError(f"no tile for {n}")


def _ada_kernel(c_ref, w_ref, b_ref, o_ref):
    c = c_ref[...]
    a = c * jax.nn.sigmoid(c)
    o_ref[0] = jnp.dot(a, w_ref[0], precision=HI, preferred_element_type=F32) + b_ref[0]


def ada_mod(cc, ada_w, ada_b):
    depth, d, n = ada_w.shape
    rows = cc.shape[0]
    return pl.pallas_call(
        _ada_kernel,
        out_shape=jax.ShapeDtypeStruct((depth, rows, n), F32),
        grid=(depth, n // d),
        in_specs=[pl.BlockSpec((rows, d), lambda l, j: (0, 0)),
                  pl.BlockSpec((1, d, d), lambda l, j: (l, 0, j)),
                  pl.BlockSpec((1, 1, d), lambda l, j: (l, 0, j))],
        out_specs=pl.BlockSpec((1, rows, d), lambda l, j: (l, 0, j)),
        compiler_params=_cparams(("parallel", "parallel"), VMEM_LIMIT),
        name="ada_mod",
    )(cc, ada_w, ada_b.reshape(depth, 1, n))


def _lnmod_kernel(x_ref, sh_ref, sc_ref, o_ref):
    h = _ln(x_ref[0]) * (1.0 + sc_ref[0]) + sh_ref[0]
    o_ref[0] = h.astype(BF16)


def _mod_spec(col, nct, nb):
    return pl.BlockSpec((1, 1, D_MODEL), lambda b, t: (jnp.where(t < nct, nb, b), 0, col))


def lnmod(x_all, mod3, nct, col_shift, col_scale):
    nb, ta, d = x_all.shape
    return pl.pallas_call(
        _lnmod_kernel,
        out_shape=jax.ShapeDtypeStruct((nb, ta, d), BF16),
        grid=(nb, ta // TT),
        in_specs=[pl.BlockSpec((1, TT, d), lambda b, t: (b, t, 0)),
                  _mod_spec(col_shift, nct, nb), _mod_spec(col_scale, nct, nb)],
        out_specs=pl.BlockSpec((1, TT, d), lambda b, t: (b, t, 0)),
        compiler_params=_cparams(("parallel", "parallel")),
        name="lnmod",
    )(x_all, mod3, mod3)


def _mm_kernel(a_ref, b_ref, o_ref):
    o_ref[...] = jnp.dot(a_ref[...], b_ref[...], preferred_element_type=F32).astype(o_ref.dtype)


def matmul(a, b, out_dtype=F32):
    m, k = a.shape
    _, n = b.shape
    tm = _pick_tile(m, (1024, 512, 256))
    tn = _pick_tile(n, (1024, 512, 256))
    return pl.pallas_call(
        _mm_kernel,
        out_shape=jax.ShapeDtypeStruct((m, n), out_dtype),
        grid=(m // tm, n // tn),
        in_specs=[pl.BlockSpec((tm, k), lambda i, j: (i, 0)),
                  pl.BlockSpec((k, tn), lambda i, j: (0, j))],
        out_specs=pl.BlockSpec((tm, tn), lambda i, j: (i, j)),
        compiler_params=_cparams(("parallel", "parallel"), VMEM_LIMIT),
        name="matmul",
    )(a, b)


def _halo_specs(width, ta):
    nblk8 = ta // SUBLANES
    per = TT // SUBLANES
    cur = pl.BlockSpec((1, TT, width), lambda b, t: (b, t, 0))
    prev = pl.BlockSpec((1, SUBLANES, width), lambda b, t: (b, jnp.maximum(t * per - 1, 0), 0))
    nxt = pl.BlockSpec((1, SUBLANES, width), lambda b, t: (b, jnp.minimum((t + 1) * per, nblk8 - 1), 0))
    return [cur, prev, nxt]


def _neighbours(cur, prev_ref, next_ref, nct, nt):
    t = pl.program_id(1)
    seg_start = jnp.logical_or(t == 0, t == nct)
    seg_end = jnp.logical_or(t == nct - 1, t == nt - 1)
    prev_row = prev_ref[0][SUBLANES - 1:SUBLANES, :] * jnp.where(seg_start, 0.0, 1.0)
    next_row = next_ref[0][0:1, :] * jnp.where(seg_end, 0.0, 1.0)
    row = lax.broadcasted_iota(jnp.int32, (TT, 1), 0)
    xm1 = jnp.where(row == 0, prev_row, pltpu.roll(cur, 1, axis=0))
    xp1 = jnp.where(row == TT - 1, next_row, pltpu.roll(cur, TT - 1, axis=0))
    return xm1, xp1


def _rwkv_prep_kernel(nct, nt, cur_ref, prev_ref, next_ref, mu_ref, w0_ref, wup_ref, a0_ref, aup_ref,
                      gup_ref, kk_ref, ka_ref, rk_ref, bd_ref,
                      r_o, kk_o, w0_o, w1_o, k0_o, k1_o, b0_o, b1_o, v_o, g_o, bon_o):
    cur = cur_ref[0]
    xm1, xp1 = _neighbours(cur, prev_ref, next_ref, nct, nt)
    p = cur + mu_ref[...] * (0.5 * (xm1 + xp1) - cur)
    c = BRANCH_W
    r, k, v = p[:, 0:c], p[:, c:2 * c], p[:, 2 * c:3 * c]
    wd = p[:, 3 * c:3 * c + 64]
    ad = p[:, 3 * c + 64:3 * c + 128]
    gd = p[:, 3 * c + 128:3 * c + 256]
    bd = bd_ref[...]
    kk = k * kk_ref[...]
    ss = _head_sums(kk * kk, bd)
    kkn = kk * lax.rsqrt(jnp.maximum(ss, 1e-24))
    twd = jnp.tanh(wd)
    ka = ka_ref[...]
    kdirs = []
    w_outs, k_outs, b_outs = (w0_o, w1_o), (k0_o, k1_o), (b0_o, b1_o)
    for d in range(2):
        wlog = w0_ref[d:d + 1, :] + jnp.dot(twd, wup_ref[d], precision=HI, preferred_element_type=F32)
        decay = -math.exp(-0.5) * jax.nn.sigmoid(wlog)
        a = jax.nn.sigmoid(a0_ref[d:d + 1, :] + jnp.dot(ad, aup_ref[d], precision=HI, preferred_element_type=F32))
        kdir = k * (1.0 + (a - 1.0) * ka)
        bdir = kkn * a
        kdirs.append(kdir)
        w_outs[d][0] = decay
        k_outs[d][0] = kdir
        b_outs[d][0] = bdir
    r_o[0] = r
    kk_o[0] = kkn
    v_o[0] = v
    g_o[0] = jnp.dot(jax.nn.sigmoid(gd), gup_ref[...], precision=HI, preferred_element_type=F32)
    rkk = r * rk_ref[...] * (kdirs[0] + kdirs[1])
    bon_o[0] = _head_sums(rkk, bd) * v


def rwkv_prep(p_rwkv, nct, mu, w0, w_up, a0, a_up, g_up, k_k, k_a, r_k, bd):
    nb, ta, _ = p_rwkv.shape
    nt = ta // TT
    c = BRANCH_W
    nat = jax.ShapeDtypeStruct((nb, ta, c), F32)
    nat_spec = pl.BlockSpec((1, TT, c), lambda b, t: (b, t, 0))

    def full(a):
        nd = a.ndim
        return pl.BlockSpec(a.shape, lambda b, t: (0,) * nd)

    consts = [mu.reshape(1, -1), w0, w_up, a0, a_up, g_up, k_k.reshape(1, -1), k_a.reshape(1, -1),
              r_k.reshape(1, -1), bd]
    return pl.pallas_call(
        functools.partial(_rwkv_prep_kernel, nct, nt),
        out_shape=[nat] * 11,
        grid=(nb, nt),
        in_specs=_halo_specs(RWKV_COLS, ta) + [full(a) for a in consts],
        out_specs=[nat_spec] * 11,
        compiler_params=_cparams(("parallel", "parallel"), VMEM_LIMIT),
        name="rwkv_prep",
    )(p_rwkv, p_rwkv, p_rwkv, *consts)


CHUNK = 32


def _chunk_scan_rows(x, reverse):
    pos = lax.broadcasted_iota(jnp.int32, x.shape, 0) % CHUNK
    step = 1
    while step < CHUNK:
        if reverse:
            x = x + jnp.where(pos < CHUNK - step, pltpu.roll(x, x.shape[0] - step, axis=0), 0.0)
        else:
            x = x + jnp.where(pos >= step, pltpu.roll(x, step, axis=0), 0.0)
        step *= 2
    return x


def _chunk_prep_kernel(r_ref, kk_ref, v_ref, lw0, k0, b0, lw1, k1, b1,
                       a0_o, bm0_o, rp0_o, y00_o, a1_o, bm1_o, rp1_o, y01_o):
    blk = SCAN_BLK
    npair = BRANCH_W // LANES
    nchunk = blk // CHUNK
    r, kk, v = r_ref[0], kk_ref[0], v_ref[0]
    ti = lax.broadcasted_iota(jnp.int32, (blk, blk), 0)
    si = lax.broadcasted_iota(jnp.int32, (blk, blk), 1)
    same = (ti // CHUNK) == (si // CHUNK)
    eye = ti == si
    bd64 = (ti // HEAD_DIM) == (si // HEAD_DIM)
    head0 = si < HEAD_DIM
    lane_half = lax.broadcasted_iota(jnp.int32, (HEAD_DIM, LANES), 1)
    eyef = jnp.where(eye, 1.0, 0.0)
    dot = lambda x, y: jnp.dot(x, y, preferred_element_type=F32)
    lo = lambda x: x.astype(BF16)
    split = lambda x: jnp.concatenate([jnp.where(head0, x, 0.0), jnp.where(head0, 0.0, x)], axis=0)
    cat = lambda ms: lo(jnp.concatenate(ms, axis=1))

    probs = []
    for d, (lw_ref, k_ref, b_ref) in enumerate(((lw0, k0, b0), (lw1, k1, b1))):
        reverse = d == 1
        lw, k, b = lw_ref[0], k_ref[0], b_ref[0]
        lg = _chunk_scan_rows(lw, reverse)
        lg_end = lg + _chunk_scan_rows(lw, not reverse) - lw
        g, gi, g_end = jnp.exp(lg), jnp.exp(-lg), jnp.exp(lg_end)
        to_end = jnp.exp(lg_end - lg)
        arrs = (kk * jnp.exp(lg - lw), b * gi, k * gi, r * g, v, k * to_end, b * to_end, g_end)
        incl = jnp.logical_and(same, si >= ti if reverse else si <= ti)
        strict = jnp.logical_and(same, si > ti if reverse else si < ti)
        for p in range(npair):
            lanes = slice(p * LANES, (p + 1) * LANES)
            probs.append(dict(d=d, p=p, incl=incl, strict=strict, arrs=tuple(a[:, lanes] for a in arrs)))

    for q in probs:
        pp_, q_, kt_, rt_ = q["arrs"][:4]
        rhs_g = lo(jnp.concatenate([q_, kt_], axis=0))
        lm, mm, n2, nn = [], [], [], []
        for hh in range(2):
            hm = head0 if hh == 0 else jnp.logical_not(head0)
            lhs_g = lo(jnp.concatenate([jnp.where(hm, pp_, 0.0), jnp.where(hm, rt_, 0.0)], axis=0))
            gm = lax.dot_general(lhs_g, rhs_g, (((1,), (1,)), ((), ())), preferred_element_type=F32)
            lm.append(jnp.where(q["strict"], gm[:blk, :blk], 0.0))
            mm.append(jnp.where(q["strict"], gm[:blk, blk:], 0.0))
            n2.append(jnp.where(q["incl"], gm[blk:, :blk], 0.0))
            nn.append(jnp.where(q["incl"], gm[blk:, blk:], 0.0))
        q["pw"], q["tm"] = lm, [eyef - lm[0], eyef - lm[1]]
        q["m_cat"], q["n2_cat"], q["nn_cat"] = cat(mm), cat(n2), cat(nn)
    for _ in range(CHUNK.bit_length() - 2):
        for q in probs:
            pwl = [lo(x) for x in q["pw"]]
            q["pw"] = [dot(x, x) for x in pwl]
        for q in probs:
            q["tm"] = [dot(lo(t), lo(eyef + x)) for t, x in zip(q["tm"], q["pw"])]
    for q in probs:
        q["t_cat"] = cat(q["tm"])
        q["v_st"] = lo(split(q["arrs"][4]))
        q["pp"] = dot(q["t_cat"], lo(split(q["arrs"][0])))
        q["mv"] = dot(q["m_cat"], q["v_st"])
    for q in probs:
        q["w2"] = dot(q["t_cat"], lo(split(q["mv"])))
        q["rp"] = q["arrs"][3] - dot(q["n2_cat"], lo(split(q["pp"])))
    for q in probs:
        q["y0"] = dot(q["nn_cat"], q["v_st"]) - dot(q["n2_cat"], lo(split(q["w2"])))
    in_chunk = [si // CHUNK == c for c in range(nchunk)]
    for q in probs:
        ppt, vt, w2t = q["pp"].T, q["arrs"][4].T, q["w2"].T
        kg_, qg_ = q["arrs"][5], q["arrs"][6]
        lhs_a = jnp.concatenate([jnp.where(cm, ppt, 0.0) for cm in in_chunk], axis=0)
        q["pq"] = dot(lo(lhs_a), lo(qg_))
        lhs_b = jnp.concatenate([jnp.concatenate([jnp.where(cm, vt, 0.0), jnp.where(cm, -w2t, 0.0)], axis=1)
                                 for cm in in_chunk], axis=0)
        q["bf"] = dot(lo(lhs_b), lo(jnp.concatenate([kg_, qg_], axis=0)))
    outs = ((a0_o, bm0_o, rp0_o, y00_o), (a1_o, bm1_o, rp1_o, y01_o))
    for q in probs:
        a_o, bm_o = outs[q["d"]][:2]
        g_end_p = q["arrs"][7]
        for c in range(nchunk):
            pq_c = q["pq"][c * blk:(c + 1) * blk]
            a_o[0, 0, c, q["p"]] = (jnp.where(eye, g_end_p[c * CHUNK:c * CHUNK + 1], 0.0)
                                    - jnp.where(bd64, pq_c, 0.0)).astype(BF16)
            bm_o[0, 0, c, q["p"]] = jnp.where(lane_half < HEAD_DIM, q["bf"][c * blk:c * blk + HEAD_DIM],
                                              q["bf"][c * blk + HEAD_DIM:(c + 1) * blk])
    for d in range(2):
        rp_o, y0_o = outs[d][2:]
        rp_o[0] = jnp.concatenate([q["rp"] for q in probs if q["d"] == d], axis=1)
        y0_o[0] = jnp.concatenate([q["y0"] for q in probs if q["d"] == d], axis=1)


def chunk_prep(r, kk, v, lw0, lw1, k0, k1, b0, b1):
    nb, ta, c = r.shape
    nblk = ta // SCAN_BLK
    npair = c // LANES
    nchunk = SCAN_BLK // CHUNK
    nat = pl.BlockSpec((1, SCAN_BLK, c), lambda b, s: (b, s, 0))
    a_shape = jax.ShapeDtypeStruct((nb, nblk, nchunk, npair, LANES, LANES), BF16)
    bm_shape = jax.ShapeDtypeStruct((nb, nblk, nchunk, npair, HEAD_DIM, LANES), F32)
    nat_shape = jax.ShapeDtypeStruct((nb, ta, c), F32)
    a_spec = pl.BlockSpec((1, 1, nchunk, npair, LANES, LANES), lambda b, s: (b, s, 0, 0, 0, 0))
    bm_spec = pl.BlockSpec((1, 1, nchunk, npair, HEAD_DIM, LANES), lambda b, s: (b, s, 0, 0, 0, 0))
    return pl.pallas_call(
        _chunk_prep_kernel,
        out_shape=[a_shape, bm_shape, nat_shape, nat_shape] * 2,
        grid=(nb, nblk),
        in_specs=[nat] * 9,
        out_specs=[a_spec, bm_spec, nat, nat] * 2,
        compiler_params=_cparams(("parallel", "parallel"), VMEM_LIMIT),
        name="chunk_prep",
    )(r, kk, v, lw0, k0, b0, lw1, k1, b1)


def _chunk_scan_kernel(nb, a0, bm0, rp0, y00, a1, bm1, rp1, y01, yf_o, yb_o, s_scr):
    step = pl.program_id(0)
    npair = BRANCH_W // LANES
    nchunk = SCAN_BLK // CHUNK

    @pl.when(step == 0)
    def _():
        s_scr[...] = jnp.zeros_like(s_scr)

    lane = lax.broadcasted_iota(jnp.int32, (CHUNK, LANES), 1)
    refs = ((a0, bm0, rp0, y00, yf_o), (a1, bm1, rp1, y01, yb_o))
    for ci in range(nchunk):
        for d in range(2):
            a_ref, bm_ref, rp_ref, y0_ref, y_ref = refs[d]
            c = ci if d == 0 else nchunk - 1 - ci
            rows = slice(c * CHUNK, (c + 1) * CHUNK)
            for b in range(nb):
                for p in range(npair):
                    lanes = slice(p * LANES, (p + 1) * LANES)
                    s = s_scr[d, b, p]
                    rpc = rp_ref[b, rows, lanes]
                    lhs = jnp.concatenate([jnp.where(lane < HEAD_DIM, rpc, 0.0), jnp.where(lane >= HEAD_DIM, rpc, 0.0)],
                                          axis=0)
                    yh = lax.dot_general(lhs, s, (((1,), (1,)), ((), ())), preferred_element_type=F32)
                    y_ref[b, rows, lanes] = jnp.concatenate([yh[:CHUNK], yh[CHUNK:]], axis=1) + y0_ref[b, rows, lanes]
                    s_scr[d, b, p] = (jnp.dot(s.astype(BF16), a_ref[b, 0, c, p], preferred_element_type=F32)
                                      + bm_ref[b, 0, c, p])


def rwkv_scan(r, kk, v, lw0, lw1, k0, k1, b0, b1, lc):
    nb, ta, c = r.shape
    nblk = ta // SCAN_BLK
    nctb = lc // SCAN_BLK
    npair = c // LANES
    nchunk = SCAN_BLK // CHUNK
    a0, bm0, rp0, y00, a1, bm1, rp1, y01 = chunk_prep(r, kk, v, lw0, lw1, k0, k1, b0, b1)

    def fwd(s):
        return s

    def bwd(s):
        return jnp.where(s < nctb, nctb - 1 - s, nblk - 1 - (s - nctb))

    def specs(idx):
        return [pl.BlockSpec((nb, 1, nchunk, npair, LANES, LANES), lambda s: (0, idx(s), 0, 0, 0, 0)),
                pl.BlockSpec((nb, 1, nchunk, npair, HEAD_DIM, LANES), lambda s: (0, idx(s), 0, 0, 0, 0)),
                pl.BlockSpec((nb, SCAN_BLK, c), lambda s: (0, idx(s), 0)),
                pl.BlockSpec((nb, SCAN_BLK, c), lambda s: (0, idx(s), 0))]

    out = jax.ShapeDtypeStruct((nb, ta, c), F32)
    return pl.pallas_call(
        functools.partial(_chunk_scan_kernel, nb),
        out_shape=[out, out],
        grid=(nblk,),
        in_specs=specs(fwd) + specs(bwd),
        out_specs=[pl.BlockSpec((nb, SCAN_BLK, c), lambda s: (0, fwd(s), 0)),
                   pl.BlockSpec((nb, SCAN_BLK, c), lambda s: (0, bwd(s), 0))],
        scratch_shapes=[pltpu.VMEM((2, nb, npair, HEAD_DIM, LANES), F32)],
        compiler_params=_cparams(("arbitrary",), VMEM_LIMIT),
        name="chunk_scan",
    )(a0, bm0, rp0, y00, a1, bm1, rp1, y01)


def _rwkv_out_kernel(yf_ref, yb_ref, g_ref, bon_ref, lg_ref, lb_ref, bd_ref, o_ref):
    y = yf_ref[0] + yb_ref[0]
    bd = bd_ref[...]
    mu = _head_sums(y, bd) * (1.0 / HEAD_DIM)
    yc = y - mu
    var = _head_sums(yc * yc, bd) * (1.0 / HEAD_DIM)
    yn = yc * lax.rsqrt(var + RWKV_GN_EPS) * lg_ref[...] + lb_ref[...]
    o_ref[0] = ((yn + bon_ref[0]) * g_ref[0]).astype(BF16)


def rwkv_out(yf, yb, g, bon, lnx_g, lnx_b, bd):
    nb, ta, _ = yf.shape
    nat_spec = pl.BlockSpec((1, TT, BRANCH_W), lambda b, t: (b, t, 0))
    row = pl.BlockSpec((1, BRANCH_W), lambda b, t: (0, 0))
    return pl.pallas_call(
        _rwkv_out_kernel,
        out_shape=jax.ShapeDtypeStruct((nb, ta, BRANCH_W), BF16),
        grid=(nb, ta // TT),
        in_specs=[nat_spec] * 4 + [row, row, pl.BlockSpec(bd.shape, lambda b, t: (0, 0))],
        out_specs=nat_spec,
        compiler_params=_cparams(("parallel", "parallel")),
        name="rwkv_out",
    )(yf, yb, g, bon, lnx_g.reshape(1, -1), lnx_b.reshape(1, -1), bd)


def _pair_swap(x):
    lane = lax.broadcasted_iota(jnp.int32, x.shape, 1)
    n = x.shape[1]
    return jnp.where(lane % 2 == 0, pltpu.roll(x, n - 1, axis=1), pltpu.roll(x, 1, axis=1))


def _attn_prep_kernel(p_ref, cos_ref, sin_ref, qg_ref, kg_ref, bd_ref, q_o, k_o, v_o):
    p = p_ref[0]
    q, k, v = p[:, :Q_W], p[:, Q_W:Q_W + KV_W], p[:, Q_W + KV_W:]
    bd = bd_ref[...]
    cos, sin = cos_ref[...], sin_ref[...]
    qms = _head_sums(q * q, bd) * (1.0 / HEAD_DIM)
    qn = q * lax.rsqrt(qms + RMS_EPS) * qg_ref[...]
    qr = qn * cos + _pair_swap(qn) * sin
    q_o[0] = (qr * HEAD_DIM ** -0.5).astype(BF16)
    kms = _head_sums(k * k, bd[:KV_W, :KV_W]) * (1.0 / HEAD_DIM)
    kn = k * lax.rsqrt(kms + RMS_EPS) * kg_ref[...]
    kr = kn * cos[:, :KV_W] + _pair_swap(kn) * sin[:, :KV_W]
    for g in range(KV_W // HEAD_DIM):
        sl = slice(g * HEAD_DIM, (g + 1) * HEAD_DIM)
        k_o[0, g] = kr[:, sl].astype(BF16)
        v_o[0, g] = v[:, sl].astype(BF16)


def attn_prep(p_attn, cos_t, sin_t, q_norm, k_norm, bd):
    nb, ta, _ = p_attn.shape
    ng = KV_W // HEAD_DIM
    qg = jnp.tile(q_norm, Q_W // HEAD_DIM).reshape(1, -1)
    kg = jnp.tile(k_norm, ng).reshape(1, -1)
    kv_shape = jax.ShapeDtypeStruct((nb, ng, ta, HEAD_DIM), BF16)
    kv_spec = pl.BlockSpec((1, ng, TT, HEAD_DIM), lambda b, t: (b, 0, t, 0))
    return pl.pallas_call(
        _attn_prep_kernel,
        out_shape=[jax.ShapeDtypeStruct((nb, ta, Q_W), BF16), kv_shape, kv_shape],
        grid=(nb, ta // TT),
        in_specs=[pl.BlockSpec((1, TT, ATTN_COLS), lambda b, t: (b, t, 0)),
                  pl.BlockSpec((TT, Q_W), lambda b, t: (t, 0)),
                  pl.BlockSpec((TT, Q_W), lambda b, t: (t, 0)),
                  pl.BlockSpec((1, Q_W), lambda b, t: (0, 0)),
                  pl.BlockSpec((1, KV_W), lambda b, t: (0, 0)),
                  pl.BlockSpec(bd.shape, lambda b, t: (0, 0))],
        out_specs=[pl.BlockSpec((1, TT, Q_W), lambda b, t: (b, t, 0)), kv_spec, kv_spec],
        compiler_params=_cparams(("parallel", "parallel")),
        name="attn_prep",
    )(p_attn, cos_t, sin_t, qg, kg, bd)


def _attn_kernel(nct, lc, q_ref, k_ref, v_ref, o_ref):
    t = pl.program_id(2)

    def run(kk, vv):
        outs = []
        for r in range(2):
            q = q_ref[0][:, r * HEAD_DIM:(r + 1) * HEAD_DIM]
            s = lax.dot_general(q, kk, (((1,), (1,)), ((), ())), preferred_element_type=F32)
            p = jnp.exp(s - jnp.max(s, -1, keepdims=True))
            l = jnp.sum(p, -1, keepdims=True)
            o = jnp.dot(p.astype(BF16), vv, preferred_element_type=F32)
            outs.append(o / l)
        o_ref[0] = jnp.concatenate(outs, axis=1).astype(BF16)

    @pl.when(t < nct)
    def _():
        run(k_ref[0, 0, :lc, :], v_ref[0, 0, :lc, :])

    @pl.when(t >= nct)
    def _():
        run(k_ref[0, 0], v_ref[0, 0])


def attention(q, k, v, lc):
    nb, ta, _ = q.shape
    ng = k.shape[1]
    nct = lc // TT
    qo_spec = pl.BlockSpec((1, TT, 2 * HEAD_DIM), lambda b, g, t: (b, t, g))
    kv_spec = pl.BlockSpec((1, 1, ta, HEAD_DIM), lambda b, g, t: (b, g, 0, 0))
    return pl.pallas_call(
        functools.partial(_attn_kernel, nct, lc),
        out_shape=jax.ShapeDtypeStruct((nb, ta, Q_W), BF16),
        grid=(nb, ng, ta // TT),
        in_specs=[qo_spec, kv_spec, kv_spec],
        out_specs=qo_spec,
        compiler_params=_cparams(("parallel", "parallel", "arbitrary"), VMEM_LIMIT),
        name="attention",
    )(q, k, v)


def _hs_pre_kernel(nct, nt, cur_ref, prev_ref, next_ref, hw_ref, sw_ref, x0_o, u_o, ycv_o):
    cur = cur_ref[0]
    xm1, xp1 = _neighbours(cur, prev_ref, next_ref, nct, nt)
    c = BRANCH_W
    hc = HYENA_COLS
    hw = hw_ref[...]
    ph = hw[0:1] * xm1[:, :hc] + hw[1:2] * cur[:, :hc] + hw[2:3] * xp1[:, :hc]
    x0_o[0] = ph[:, :c]
    u_o[0] = ph[:, c:2 * c] * ph[:, 2 * c:3 * c]
    sw = sw_ref[...]

    def cx(a):
        return a[:, hc + c:hc + 2 * c] * a[:, hc + 2 * c:hc + 3 * c]

    conv = sw[0:1] * cx(xm1) + sw[1:2] * cx(cur) + sw[2:3] * cx(xp1)
    ycv_o[0] = (cur[:, hc:hc + c] * conv).astype(BF16)


def hs_pre(p_hs, nct, hyena_conv, sconv_w):
    nb, ta, w = p_hs.shape
    nt = ta // TT
    nat = pl.BlockSpec((1, TT, BRANCH_W), lambda b, t: (b, t, 0))
    return pl.pallas_call(
        functools.partial(_hs_pre_kernel, nct, nt),
        out_shape=[jax.ShapeDtypeStruct((nb, ta, BRANCH_W), F32),
                   jax.ShapeDtypeStruct((nb, ta, BRANCH_W), F32),
                   jax.ShapeDtypeStruct((nb, ta, BRANCH_W), BF16)],
        grid=(nb, nt),
        in_specs=_halo_specs(w, ta) + [pl.BlockSpec(hyena_conv.shape, lambda b, t: (0, 0)),
                                       pl.BlockSpec(sconv_w.shape, lambda b, t: (0, 0))],
        out_specs=[nat, nat, nat],
        compiler_params=_cparams(("parallel", "parallel"), VMEM_LIMIT),
        name="hs_pre",
    )(p_hs, p_hs, p_hs, hyena_conv, sconv_w)


EMB_PAD = 40


def _filter_tables(lh):
    n = np.arange(2 * lh)
    pos = np.abs(n - (lh - 1)).astype(np.float64)
    bands = (HYENA_EMB - 1) // 2
    t = np.minimum(pos, lh - 1) / (lh - 1)
    wpos = 2.0 * math.pi * pos / lh
    f = np.linspace(1e-4, bands - 1, bands)[:, None]
    z = np.zeros((EMB_PAD, 2 * lh), np.float32)
    z[0] = t
    z[1:1 + bands] = np.cos(f * wpos[None, :])
    z[1 + bands:1 + 2 * bands] = -np.sin(f * wpos[None, :])
    max_decay = math.log(HYENA_TARGET) / HYENA_FAST_DECAY
    min_decay = math.log(HYENA_TARGET) / HYENA_SLOW_DECAY
    deltas = np.abs(np.linspace(min_decay, max_decay, BRANCH_W)).astype(np.float32)
    return z, deltas.reshape(-1, 1)


def _filter_kernel(lh, tn, z_ref, w1_ref, b1_ref, f1_ref, w2_ref, b2_ref, f2_ref, w3_ref, dl_ref, o_ref):
    z = z_ref[...]
    h1 = jnp.sin(f1_ref[...] * (jnp.dot(w1_ref[...], z, precision=HI, preferred_element_type=F32) + b1_ref[...]))
    h2 = jnp.sin(f2_ref[...] * (jnp.dot(w2_ref[...], h1, precision=HI, preferred_element_type=F32) + b2_ref[...]))
    f = jnp.dot(w3_ref[...], h2, precision=HI, preferred_element_type=F32)
    n = pl.program_id(0) * tn + lax.broadcasted_iota(jnp.int32, (1, tn), 1)
    filt = jnp.where(n >= lh - 1, f[:BRANCH_W], f[BRANCH_W:])
    win = jnp.exp(-z[0:1, :] * dl_ref[...])
    o_ref[...] = jnp.where(n == 2 * lh - 1, 0.0, filt * win)


def hyena_filter_table(lh, w1, b1, f1, w2, b2, f2, w3):
    z_np, dl_np = _filter_tables(lh)
    n2 = 2 * lh
    tn = _pick_tile(n2, (1024, 512))
    hd = w2.shape[0]
    w1t = jnp.zeros((hd, EMB_PAD), F32).at[:, :HYENA_EMB].set(w1.T)
    args = [jnp.asarray(z_np), w1t, b1.reshape(-1, 1), f1.reshape(-1, 1), w2.T, b2.reshape(-1, 1),
            f2.reshape(-1, 1), w3.T, jnp.asarray(dl_np)]

    def full(a):
        return pl.BlockSpec(a.shape, lambda j: (0, 0))

    return pl.pallas_call(
        functools.partial(_filter_kernel, lh, tn),
        out_shape=jax.ShapeDtypeStruct((BRANCH_W, n2), F32),
        grid=(n2 // tn,),
        in_specs=[pl.BlockSpec((EMB_PAD, tn), lambda j: (0, j))] + [full(a) for a in args[1:]],
        out_specs=pl.BlockSpec((BRANCH_W, tn), lambda j: (0, j)),
        compiler_params=_cparams(("parallel",)),
        name="hyena_filter",
    )(*args)


def _hyena_conv_kernel(nblk, bp, nch, k_ref, u_ref, o_ref, t_scr):
    ntile = 4 * nblk - 1
    mc = 2 * nblk - 1
    for ch in range(nch):
        for m in range(ntile):
            win = jnp.concatenate([k_ref[ch, m:m + 1, :], k_ref[ch, m + 1:m + 2, :]], axis=1)
            x = jnp.broadcast_to(win, (LANES, 2 * LANES))
            t_scr[ch, m] = pltpu.roll(x, LANES + 1, 1, stride=1, stride_axis=0)[:, :LANES].astype(BF16)

    for ch in range(nch):
        for d in [0] + [s * a for a in range(1, nblk) for s in (1, -1)]:
            m0 = 2 * d + mc
            w = jnp.concatenate([jnp.concatenate([t_scr[ch, m0], t_scr[ch, m0 + 1]], axis=1),
                                 jnp.concatenate([t_scr[ch, m0 - 1], t_scr[ch, m0]], axis=1)], axis=0)
            i0, i1 = max(0, d), min(nblk, nblk + d)
            lhs = u_ref[ch, (i0 - d) * bp:(i1 - d) * bp, :].astype(BF16)
            res = jnp.dot(lhs, w, preferred_element_type=F32)
            if d == 0:
                o_ref[ch] = res
            else:
                o_ref[ch, i0 * bp:i1 * bp, :] += res


def hyena_conv(u, ktab):
    nb, l, c = u.shape
    nblk = l // HY_BLK
    bp = -(-nb // SUBLANES) * SUBLANES
    nch = SUBLANES if nblk == 1 else 1
    ut = jnp.transpose(u.reshape(nb, nblk, HY_BLK, c), (3, 1, 0, 2))
    if bp != nb:
        ut = jnp.pad(ut, ((0, 0), (0, 0), (0, bp - nb), (0, 0)))
    ut = ut.reshape(c, nblk * bp, HY_BLK)
    k3 = ktab.reshape(c, 4 * nblk, LANES)
    out = pl.pallas_call(
        functools.partial(_hyena_conv_kernel, nblk, bp, nch),
        out_shape=jax.ShapeDtypeStruct((c, nblk * bp, HY_BLK), F32),
        grid=(c // nch,),
        in_specs=[pl.BlockSpec((nch, 4 * nblk, LANES), lambda ch: (ch, 0, 0)),
                  pl.BlockSpec((nch, nblk * bp, HY_BLK), lambda ch: (ch, 0, 0))],
        out_specs=pl.BlockSpec((nch, nblk * bp, HY_BLK), lambda ch: (ch, 0, 0)),
        scratch_shapes=[pltpu.VMEM((nch, 4 * nblk - 1, LANES, LANES), BF16)],
        compiler_params=_cparams(("parallel",)),
        name="hyena_conv",
    )(k3, ut)
    out = out.reshape(c, nblk, bp, HY_BLK)[:, :, :nb]
    return jnp.transpose(out, (2, 1, 3, 0)).reshape(nb, l, c)


def _route(logits, bias):
    s = jax.nn.sigmoid(logits)
    sel = s + bias
    srow = [s[e:e + 1] for e in range(N_EXPERTS)]
    row = [sel[e:e + 1] for e in range(N_EXPERTS)]
    best, gi = None, None
    for g in range(N_GROUPS):
        a, b, c, d = row[4 * g:4 * g + 4]
        hi1, lo1, hi2, lo2 = jnp.maximum(a, b), jnp.minimum(a, b), jnp.maximum(c, d), jnp.minimum(c, d)
        score = jnp.maximum(hi1, hi2) + jnp.maximum(jnp.minimum(hi1, hi2), jnp.maximum(lo1, lo2))
        if g == 0:
            best, gi = score, jnp.zeros(score.shape, jnp.int32)
        else:
            better = score > best
            gi = jnp.where(better, g, gi)
            best = jnp.where(better, score, best)
    neg = -jnp.inf
    msel = [jnp.where(gi == e // EXPERTS_PER_GROUP, row[e], neg) for e in range(N_EXPERTS)]

    def arg_first_max(vals):
        bv, bi = vals[0], jnp.zeros(vals[0].shape, jnp.int32)
        for e in range(1, N_EXPERTS):
            better = vals[e] > bv
            bi = jnp.where(better, e, bi)
            bv = jnp.where(better, vals[e], bv)
        return bi

    i1 = arg_first_max(msel)
    i2 = arg_first_max([jnp.where(i1 == e, neg, msel[e]) for e in range(N_EXPERTS)])
    w1 = sum(jnp.where(i1 == e, srow[e], 0.0) for e in range(N_EXPERTS))
    w2 = sum(jnp.where(i2 == e, srow[e], 0.0) for e in range(N_EXPERTS))
    den = w1 + w2
    g1, g2 = w1 / den, w2 / den
    rows = [jnp.where(i1 == e, g1, 0.0) + jnp.where(i2 == e, g2, 0.0) for e in range(N_EXPERTS)]
    rows.append(gi.astype(F32))
    rows.extend([jnp.zeros_like(g1)] * (ROUTE_ROWS - len(rows)))
    return jnp.concatenate(rows, axis=0)


def _merge_kernel(ya_ref, x0_ref, u_ref, yc_ref, ycv_ref, yd_ref, h_ref, x_ref, ga_ref, shf_ref, scf_ref,
                  skip_ref, wg_ref, wb_ref, wo_ref, g1_ref, b1_ref, rwt_ref, rb_ref, x1_o, hf_o, gates_o):
    yb = (x0_ref[0] * (yc_ref[0] + u_ref[0] * skip_ref[...])).astype(BF16)
    ys = (ya_ref[0], yb, ycv_ref[0], yd_ref[0])
    merged = None
    for n in range(N_BRANCHES):
        gate = jax.nn.sigmoid(jnp.dot(h_ref[0], wg_ref[:, n * D_MODEL:(n + 1) * D_MODEL], preferred_element_type=F32))
        term = gate * jnp.dot(ys[n], wb_ref[n], preferred_element_type=F32)
        merged = term if merged is None else merged + term
    out = jnp.dot(merged.astype(BF16), wo_ref[...], preferred_element_type=F32)
    alpha = (2 * 2) ** 0.25
    x1 = _ln(alpha * x_ref[0] + ga_ref[0] * out) * g1_ref[...] + b1_ref[...]
    hf = _ln(x1) * (1.0 + scf_ref[0]) + shf_ref[0]
    x1_o[0] = x1
    hf_o[0] = hf.astype(BF16)
    logits = lax.dot_general(rwt_ref[...], hf, (((1,), (1,)), ((), ())), precision=HI,
                             preferred_element_type=F32)
    gates_o[...] = _route(logits, rb_ref[...])


def merge(ya, x0, u, yconv, ycv, yd, h, x_all, mod3, nct, skip, wg, wb, wo, ln_g, ln_b, rwt, rbias):
    nb, ta, d = x_all.shape
    nt = ta // TT
    nat = pl.BlockSpec((1, TT, BRANCH_W), lambda b, t: (b, t, 0))
    wide = pl.BlockSpec((1, TT, d), lambda b, t: (b, t, 0))

    def full(a):
        nd = a.ndim
        return pl.BlockSpec(a.shape, lambda b, t: (0,) * nd)

    consts = [skip.reshape(1, -1), wg, wb, wo, ln_g.reshape(1, -1), ln_b.reshape(1, -1), rwt, rbias.reshape(-1, 1)]
    return pl.pallas_call(
        _merge_kernel,
        out_shape=[jax.ShapeDtypeStruct((nb, ta, d), F32), jax.ShapeDtypeStruct((nb, ta, d), BF16),
                   jax.ShapeDtypeStruct((ROUTE_ROWS, nb * ta), F32)],
        grid=(nb, nt),
        in_specs=[nat] * 6 + [wide, wide,
                              _mod_spec(2, nct, nb), _mod_spec(3, nct, nb), _mod_spec(4, nct, nb)]
                 + [full(a) for a in consts],
        out_specs=[wide, wide, pl.BlockSpec((ROUTE_ROWS, TT), lambda b, t: (0, b * nt + t))],
        compiler_params=_cparams(("parallel", "parallel"), VMEM_LIMIT),
        name="merge",
    )(ya, x0, u, yconv, ycv, yd, h, x_all, mod3, mod3, mod3, *consts)


MOE_TILE = 1024
MOE_ALIGN = 2 * SUBLANES
MOE_CHUNK = 18 * MOE_ALIGN
MOE_SORTED = MOE_TILE + LANES
MOE_ROWS = MOE_SORTED + 3 * LANES
GID_ROW = N_EXPERTS
ROUTE_ROWS = 3 * SUBLANES
META_LANES = 2 * N_GROUPS
assert N_GROUPS * (MOE_ALIGN - 1) <= MOE_SORTED - MOE_TILE and MOE_SORTED + MOE_CHUNK <= MOE_ROWS


def _moe_sort_kernel(gt_ref, g_ref, h_ref, up_ref, hs_o, gs_o, pt_o, meta_o):
    gid = gt_ref[GID_ROW:GID_ROW + 1, :]
    onehot = [jnp.where(gid == float(g), 1.0, 0.0) for g in range(N_GROUPS)]
    g4 = jnp.concatenate(onehot + [jnp.zeros((SUBLANES - N_GROUPS, MOE_TILE), F32)], axis=0)
    before = jnp.dot(g4.astype(BF16), up_ref[...], preferred_element_type=F32)
    lane = lax.broadcasted_iota(jnp.int32, (SUBLANES, LANES), 1)
    meta = jnp.zeros((SUBLANES, LANES), F32)
    off = jnp.zeros((1, 1), F32)
    pos = jnp.zeros((1, MOE_TILE), F32)
    for g in range(N_GROUPS):
        cnt = jnp.sum(onehot[g], axis=1, keepdims=True)
        pos = pos + onehot[g] * (before[g:g + 1] + off)
        meta = jnp.where(lane == g, off, meta)
        meta = jnp.where(lane == N_GROUPS + g, cnt, meta)
        off = off + jnp.ceil(cnt * (1.0 / MOE_ALIGN)) * MOE_ALIGN
    meta_o[0] = meta.astype(jnp.int32)
    row = lax.broadcasted_iota(jnp.int32, (MOE_ROWS, MOE_TILE), 0)
    place = jnp.where(row == pos.astype(jnp.int32), 1.0, 0.0)
    p16 = place.astype(BF16)
    hs_o[0] = jnp.dot(p16, h_ref[...], preferred_element_type=F32).astype(BF16)
    gts = g_ref[...]
    hi = gts.astype(BF16)
    r1 = gts - hi.astype(F32)
    mid = r1.astype(BF16)
    low = (r1 - mid.astype(F32)).astype(BF16)
    gs_o[0] = (jnp.dot(p16, hi, preferred_element_type=F32) + jnp.dot(p16, mid, preferred_element_type=F32)
               + jnp.dot(p16, low, preferred_element_type=F32))
    pt_o[...] = place.T.astype(BF16)


def moe_sort(hf, gates_t, gates):
    n, d = hf.shape
    ntile = n // MOE_TILE
    upper = jnp.asarray(np.triu(np.ones((MOE_TILE, MOE_TILE), np.float32), 1), BF16)
    return pl.pallas_call(
        _moe_sort_kernel,
        out_shape=[jax.ShapeDtypeStruct((ntile, MOE_ROWS, d), BF16),
                   jax.ShapeDtypeStruct((ntile, MOE_ROWS, LANES), F32),
                   jax.ShapeDtypeStruct((n, MOE_ROWS), BF16),
                   jax.ShapeDtypeStruct((ntile, SUBLANES, LANES), jnp.int32)],
        grid=(ntile,),
        in_specs=[pl.BlockSpec((ROUTE_ROWS, MOE_TILE), lambda i: (0, i)),
                  pl.BlockSpec((MOE_TILE, LANES), lambda i: (i, 0)),
                  pl.BlockSpec((MOE_TILE, d), lambda i: (i, 0)),
                  pl.BlockSpec((MOE_TILE, MOE_TILE), lambda i: (0, 0))],
        out_specs=[pl.BlockSpec((1, MOE_ROWS, d), lambda i: (i, 0, 0)),
                   pl.BlockSpec((1, MOE_ROWS, LANES), lambda i: (i, 0, 0)),
                   pl.BlockSpec((MOE_TILE, MOE_ROWS), lambda i: (i, 0)),
                   pl.BlockSpec((1, SUBLANES, LANES), lambda i: (i, 0, 0))],
        compiler_params=_cparams(("parallel",), VMEM_LIMIT),
        name="moe_sort",
    )(gates_t, gates, hf, upper)


def _moe_group_kernel(meta_ref, hs_ref, gs_ref, w1_ref, w3_ref, w2_ref, ys_o, ysb_o):
    i, g = pl.program_id(0), pl.program_id(1)

    @pl.when(g == 0)
    def _():
        ys_o[...] = jnp.zeros_like(ys_o)

    off = meta_ref[i * META_LANES + g]
    cnt = meta_ref[i * META_LANES + N_GROUPS + g]
    lane = lax.broadcasted_iota(jnp.int32, (MOE_CHUNK, LANES), 1)

    def chunk(j, carry):
        rows = pl.ds(pl.multiple_of(off + j * MOE_CHUNK, MOE_ALIGN), MOE_CHUNK)
        hs = hs_ref[0, rows, :]
        gs = gs_ref[0, rows, :]
        acc = None
        for e in range(EXPERTS_PER_GROUP):
            a = jnp.dot(hs, w1_ref[e], preferred_element_type=F32)
            b = jnp.dot(hs, w3_ref[e], preferred_element_type=F32)
            act = (a * jax.nn.sigmoid(a)) * b
            gcol = jnp.sum(jnp.where(lane == g * EXPERTS_PER_GROUP + e, gs, 0.0), axis=1, keepdims=True)
            term = gcol * jnp.dot(act.astype(BF16), w2_ref[e], preferred_element_type=F32)
            acc = term if acc is None else acc + term
        ys_o[0, rows, :] += acc
        return carry

    lax.fori_loop(0, lax.div(cnt + (MOE_CHUNK - 1), MOE_CHUNK), chunk, 0)

    @pl.when(g == N_GROUPS - 1)
    def _():
        ysb_o[0] = ys_o[0].astype(BF16)


def moe_group(hs, gs, meta, w1, w3, w2):
    ntile, _, d = hs.shape
    de = w1.shape[2]
    epg = EXPERTS_PER_GROUP
    grid_spec = pltpu.PrefetchScalarGridSpec(
        num_scalar_prefetch=1,
        grid=(ntile, N_GROUPS),
        in_specs=[pl.BlockSpec((1, MOE_ROWS, d), lambda i, g, m: (i, 0, 0)),
                  pl.BlockSpec((1, MOE_ROWS, LANES), lambda i, g, m: (i, 0, 0)),
                  pl.BlockSpec((epg, d, de), lambda i, g, m: (g, 0, 0)),
                  pl.BlockSpec((epg, d, de), lambda i, g, m: (g, 0, 0)),
                  pl.BlockSpec((epg, de, d), lambda i, g, m: (g, 0, 0))],
        out_specs=[pl.BlockSpec((1, MOE_ROWS, d), lambda i, g, m: (i, 0, 0)),
                   pl.BlockSpec((1, MOE_ROWS, d), lambda i, g, m: (i, 0, 0))],
    )
    return pl.pallas_call(
        _moe_group_kernel,
        out_shape=[jax.ShapeDtypeStruct((ntile, MOE_ROWS, d), F32), jax.ShapeDtypeStruct((ntile, MOE_ROWS, d), BF16)],
        grid_spec=grid_spec,
        compiler_params=_cparams(("parallel", "arbitrary"), VMEM_LIMIT),
        name="moe_group",
    )(meta, hs, gs, w1, w3, w2)[1]


def _ln2_kernel(x_ref, pt_ref, ys_ref, gf_ref, g_ref, b_ref, o_ref):
    alpha = (2 * 2) ** 0.25
    f = jnp.dot(pt_ref[...], ys_ref[0], preferred_element_type=F32)
    o_ref[0] = _ln(alpha * x_ref[0] + gf_ref[0] * f) * g_ref[...] + b_ref[...]


def ln2(x1, pt, ysb, mod3, nct, ln_g, ln_b, latent_only):
    nb, ta, d = x1.shape
    nt = ta // TT
    per = MOE_TILE // TT
    wide = pl.BlockSpec((1, TT, d), lambda b, t: (b, t, 0))
    row = pl.BlockSpec((1, d), lambda b, t: (0, 0))
    out_rows, out_spec = ta, wide
    if latent_only:
        out_rows, out_spec = ta - nct * TT, pl.BlockSpec((1, TT, d), lambda b, t: (b, jnp.maximum(t - nct, 0), 0))
    return pl.pallas_call(
        _ln2_kernel,
        out_shape=jax.ShapeDtypeStruct((nb, out_rows, d), F32),
        grid=(nb, nt),
        in_specs=[wide, pl.BlockSpec((TT, MOE_ROWS), lambda b, t: (b * nt + t, 0)),
                  pl.BlockSpec((1, MOE_ROWS, d), lambda b, t: ((b * nt + t) // per, 0, 0)),
                  _mod_spec(5, nct, nb), row, row],
        out_specs=out_spec,
        compiler_params=_cparams(("parallel", "arbitrary"), VMEM_LIMIT),
        name="ln2",
    )(x1, pt, ysb, mod3, ln_g.reshape(1, -1), ln_b.reshape(1, -1))


def _rope_tables(l, lc):
    half = HEAD_DIM // 2
    inv = ROPE_THETA ** (-np.arange(0, half, 2, dtype=np.float64) / half)
    t = np.arange(l)
    rows, cols = t // GRID_W, t % GRID_W
    ang = np.concatenate([rows[:, None] * inv, cols[:, None] * inv], -1)
    ang = np.concatenate([np.zeros((lc, half)), ang], 0)
    cos = np.repeat(np.cos(ang), 2, axis=1)
    sin = np.repeat(np.sin(ang), 2, axis=1)
    sin[:, 0::2] *= -1.0
    reps = Q_W // HEAD_DIM
    return (jnp.asarray(np.tile(cos, (1, reps)), F32), jnp.asarray(np.tile(sin, (1, reps)), F32))


def _block_diag_ones():
    i = np.arange(BRANCH_W) // HEAD_DIM
    return jnp.asarray((i[:, None] == i[None, :]).astype(np.float32))


def kernel(x, c, ctx, c_ctx, ada_w, ada_b, w_in, rwkv_mu, rwkv_w0, rwkv_w_up, rwkv_a0, rwkv_a_up, rwkv_g_up, rwkv_k_k, rwkv_k_a, rwkv_r_k, rwkv_lnx_g, rwkv_lnx_b, hyena_conv, hyena_w1, hyena_b1, hyena_freq1, hyena_w2, hyena_b2, hyena_freq2, hyena_w3, hyena_skip, sconv_w, attn_q_norm, attn_k_norm, w_branch, w_out, ln1_g, ln1_b, ln2_g, ln2_b, router_w, router_bias, exp_w1, exp_w3, exp_w2):
    nb, l, d = x.shape
    lc = ctx.shape[1]
    depth = ada_w.shape[0]
    assert d == D_MODEL and lc % TT == 0 and l % TT == 0 and l % GRID_W == 0 and (nb * (lc + l)) % MOE_TILE == 0
    ta = lc + l
    nct = lc // TT
    x_all = jnp.concatenate([ctx, x], axis=1)

    mod_rows = -(-(nb + 1) // SUBLANES) * SUBLANES
    cc = jnp.zeros((mod_rows, d), F32).at[:nb].set(c).at[nb].set(c_ctx)
    mod = ada_mod(cc, ada_w, ada_b)

    cos_t, sin_t = _rope_tables(l, lc)
    bd = _block_diag_ones()
    rwt = router_w.T

    for li in range(depth):
        mod3 = mod[li].reshape(mod_rows, 1, N_MOD * d)
        wl = w_in[li].astype(BF16)
        h3 = lnmod(x_all, mod3, nct, 0, 1)
        h = h3.reshape(nb * ta, d)
        p_rwkv = matmul(h, wl[:, :OFF_HYENA]).reshape(nb, ta, -1)
        p_hs = matmul(h, wl[:, OFF_HYENA:OFF_ATTN]).reshape(nb, ta, -1)
        p_attn = matmul(h, wl[:, OFF_ATTN:OFF_GATE]).reshape(nb, ta, -1)

        r, kk, w0, w1, k0, k1, b0, b1, v, g, bon = rwkv_prep(
            p_rwkv, nct, rwkv_mu[li], rwkv_w0[li], rwkv_w_up[li], rwkv_a0[li], rwkv_a_up[li], rwkv_g_up[li],
            rwkv_k_k[li], rwkv_k_a[li], rwkv_r_k[li], bd)
        yf, yb = rwkv_scan(r, kk, v, w0, w1, k0, k1, b0, b1, lc)
        ya = rwkv_out(yf, yb, g, bon, rwkv_lnx_g[li], rwkv_lnx_b[li], bd)

        q, kx, vx = attn_prep(p_attn, cos_t, sin_t, attn_q_norm[li], attn_k_norm[li], bd)
        yd = attention(q, kx, vx, lc)

        x0, u, ycv = hs_pre(p_hs, nct, hyena_conv[li], sconv_w[li])
        fargs = (hyena_w1[li], hyena_b1[li], hyena_freq1[li], hyena_w2[li], hyena_b2[li], hyena_freq2[li],
                 hyena_w3[li])
        yconv_ctx = hyena_conv_seg(u[:, :lc], fargs) if li < depth - 1 else jnp.zeros((nb, lc, BRANCH_W), F32)
        yconv = jnp.concatenate([yconv_ctx, hyena_conv_seg(u[:, lc:], fargs)], axis=1)

        x1, hf, gates_t = merge(ya, x0, u, yconv, ycv, yd, h3, x_all, mod3, nct, hyena_skip[li], wl[:, OFF_GATE:],
                                w_branch[li].astype(BF16), w_out[li].astype(BF16), ln1_g[li], ln1_b[li],
                                rwt, router_bias)
        gates = jnp.pad(gates_t.T, ((0, 0), (0, LANES - ROUTE_ROWS)))
        hs, gs, pt, meta = moe_sort(hf.reshape(nb * ta, d), gates_t, gates)
        ysb = moe_group(hs, gs, meta[:, 0, :META_LANES].reshape(-1), exp_w1[li].astype(BF16),
                        exp_w3[li].astype(BF16), exp_w2[li].astype(BF16))
        x_all = ln2(x1, pt, ysb, mod3, nct, ln2_g[li], ln2_b[li], latent_only=li == depth - 1)
    return x_all


def hyena_conv_seg(u_seg, fargs):
    ktab = hyena_filter_table(u_seg.shape[1], *fargs)
    return hyena_conv(u_seg, ktab)
```

```python
import functools
import math

import numpy as np
import jax
import jax.numpy as jnp
from jax import lax
from jax.experimental import pallas as pl
from jax.experimental.pallas import tpu as pltpu

F32 = jnp.float32
BF16 = jnp.bfloat16
HI = lax.Precision.HIGHEST

D_MODEL = 1024
GRID_W = 64
BRANCH_W = 256
HEAD_DIM = 64
N_BRANCHES = 4
N_MOD = 6
RWKV_HEADS = 4
RWKV_COLS = 1024
RWKV_GN_EPS = 64e-5
HYENA_COLS = 768
HYENA_EMB = 33
HYENA_FAST_DECAY = 0.3
HYENA_SLOW_DECAY = 1.5
HYENA_TARGET = 1e-2
SCONV_COLS = 768
Q_W = 256
KV_W = 128
ATTN_COLS = 512
ROPE_THETA = 10000.0
RMS_EPS = 1e-6
OFF_HYENA = RWKV_COLS
OFF_SCONV = OFF_HYENA + HYENA_COLS
OFF_ATTN = OFF_SCONV + SCONV_COLS
OFF_GATE = OFF_ATTN + ATTN_COLS
N_EXPERTS = 16
N_GROUPS = 4
EXPERTS_PER_GROUP = 4
D_EXPERT = 512
LN_EPS = 1e-6

SUBLANES = 8
LANES = 128
TT = 256
SCAN_BLK = LANES
HY_BLK = 256
VMEM_LIMIT = 56 * 1024 * 1024


def _cparams(sem, vmem=None):
    return pltpu.CompilerParams(dimension_semantics=sem, vmem_limit_bytes=vmem)


def _ln(xf):
    mu = jnp.mean(xf, -1, keepdims=True)
    xc = xf - mu
    var = jnp.mean(xc * xc, -1, keepdims=True)
    return xc * lax.rsqrt(var + LN_EPS)


def _head_sums(x, ones_bd):
    ones16 = ones_bd.astype(BF16)
    hi = x.astype(BF16)
    lo = (x - hi.astype(F32)).astype(BF16)
    return jnp.dot(hi, ones16, preferred_element_type=F32) + jnp.dot(lo, ones16, preferred_element_type=F32)


def _pick_tile(n, cands):
    for c in cands:
        if n % c == 0:
            return c
    raise ValueError(f"no tile for {n}")


def _ada_kernel(c_ref, w_ref, b_ref, o_ref):
    c = c_ref[...]
    a = c * jax.nn.sigmoid(c)
    o_ref[0] = jnp.dot(a, w_ref[0], precision=HI, preferred_element_type=F32) + b_ref[0]


def ada_mod(cc, ada_w, ada_b):
    depth, d, n = ada_w.shape
    rows = cc.shape[0]
    return pl.pallas_call(
        _ada_kernel,
        out_shape=jax.ShapeDtypeStruct((depth, rows, n), F32),
        grid=(depth, n // d),
        in_specs=[pl.BlockSpec((rows, d), lambda l, j: (0, 0)),
                  pl.BlockSpec((1, d, d), lambda l, j: (l, 0, j)),
                  pl.BlockSpec((1, 1, d), lambda l, j: (l, 0, j))],
        out_specs=pl.BlockSpec((1, rows, d), lambda l, j: (l, 0, j)),
        compiler_params=_cparams(("parallel", "parallel"), VMEM_LIMIT),
        name="ada_mod",
    )(cc, ada_w, ada_b.reshape(depth, 1, n))


def _lnmod_kernel(x_ref, sh_ref, sc_ref, o_ref):
    h = _ln(x_ref[0]) * (1.0 + sc_ref[0]) + sh_ref[0]
    o_ref[0] = h.astype(BF16)


def _mod_spec(col, nct, nb):
    return pl.BlockSpec((1, 1, D_MODEL), lambda b, t: (jnp.where(t < nct, nb, b), 0, col))


def lnmod(x_all, mod3, nct, col_shift, col_scale):
    nb, ta, d = x_all.shape
    return pl.pallas_call(
        _lnmod_kernel,
        out_shape=jax.ShapeDtypeStruct((nb, ta, d), BF16),
        grid=(nb, ta // TT),
        in_specs=[pl.BlockSpec((1, TT, d), lambda b, t: (b, t, 0)),
                  _mod_spec(col_shift, nct, nb), _mod_spec(col_scale, nct, nb)],
        out_specs=pl.BlockSpec((1, TT, d), lambda b, t: (b, t, 0)),
        compiler_params=_cparams(("parallel", "parallel")),
        name="lnmod",
    )(x_all, mod3, mod3)


def _mm_kernel(a_ref, b_ref, o_ref):
    o_ref[...] = jnp.dot(a_ref[...], b_ref[...], preferred_element_type=F32).astype(o_ref.dtype)


def matmul(a, b, out_dtype=F32):
    m, k = a.shape
    _, n = b.shape
    tm = _pick_tile(m, (1024, 512, 256))
    tn = _pick_tile(n, (1024, 512, 256))
    return pl.pallas_call(
        _mm_kernel,
        out_shape=jax.ShapeDtypeStruct((m, n), out_dtype),
        grid=(m // tm, n // tn),
        in_specs=[pl.BlockSpec((tm, k), lambda i, j: (i, 0)),
                  pl.BlockSpec((k, tn), lambda i, j: (0, j))],
        out_specs=pl.BlockSpec((tm, tn), lambda i, j: (i, j)),
        compiler_params=_cparams(("parallel", "parallel"), VMEM_LIMIT),
        name="matmul",
    )(a, b)


def _halo_specs(width, ta):
    nblk8 = ta // SUBLANES
    per = TT // SUBLANES
    cur = pl.BlockSpec((1, TT, width), lambda b, t: (b, t, 0))
    prev = pl.BlockSpec((1, SUBLANES, width), lambda b, t: (b, jnp.maximum(t * per - 1, 0), 0))
    nxt = pl.BlockSpec((1, SUBLANES, width), lambda b, t: (b, jnp.minimum((t + 1) * per, nblk8 - 1), 0))
    return [cur, prev, nxt]


def _neighbours(cur, prev_ref, next_ref, nct, nt):
    t = pl.program_id(1)
    seg_start = jnp.logical_or(t == 0, t == nct)
    seg_end = jnp.logical_or(t == nct - 1, t == nt - 1)
    prev_row = prev_ref[0][SUBLANES - 1:SUBLANES, :] * jnp.where(seg_start, 0.0, 1.0)
    next_row = next_ref[0][0:1, :] * jnp.where(seg_end, 0.0, 1.0)
    row = lax.broadcasted_iota(jnp.int32, (TT, 1), 0)
    xm1 = jnp.where(row == 0, prev_row, pltpu.roll(cur, 1, axis=0))
    xp1 = jnp.where(row == TT - 1, next_row, pltpu.roll(cur, TT - 1, axis=0))
    return xm1, xp1


def _rwkv_prep_kernel(nct, nt, cur_ref, prev_ref, next_ref, mu_ref, w0_ref, wup_ref, a0_ref, aup_ref,
                      gup_ref, kk_ref, ka_ref, rk_ref, bd_ref,
                      r_o, kk_o, w0_o, w1_o, k0_o, k1_o, b0_o, b1_o, v_o, g_o, bon_o):
    cur = cur_ref[0]
    xm1, xp1 = _neighbours(cur, prev_ref, next_ref, nct, nt)
    p = cur + mu_ref[...] * (0.5 * (xm1 + xp1) - cur)
    c = BRANCH_W
    r, k, v = p[:, 0:c], p[:, c:2 * c], p[:, 2 * c:3 * c]
    wd = p[:, 3 * c:3 * c + 64]
    ad = p[:, 3 * c + 64:3 * c + 128]
    gd = p[:, 3 * c + 128:3 * c + 256]
    bd = bd_ref[...]
    kk = k * kk_ref[...]
    ss = _head_sums(kk * kk, bd)
    kkn = kk * lax.rsqrt(jnp.maximum(ss, 1e-24))
    twd = jnp.tanh(wd)
    ka = ka_ref[...]
    kdirs = []
    w_outs, k_outs, b_outs = (w0_o, w1_o), (k0_o, k1_o), (b0_o, b1_o)
    for d in range(2):
        wlog = w0_ref[d:d + 1, :] + jnp.dot(twd, wup_ref[d], precision=HI, preferred_element_type=F32)
        decay = -math.exp(-0.5) * jax.nn.sigmoid(wlog)
        a = jax.nn.sigmoid(a0_ref[d:d + 1, :] + jnp.dot(ad, aup_ref[d], precision=HI, preferred_element_type=F32))
        kdir = k * (1.0 + (a - 1.0) * ka)
        bdir = kkn * a
        kdirs.append(kdir)
        w_outs[d][0] = decay
        k_outs[d][0] = kdir
        b_outs[d][0] = bdir
    r_o[0] = r
    kk_o[0] = kkn
    v_o[0] = v
    g_o[0] = jnp.dot(jax.nn.sigmoid(gd), gup_ref[...], precision=HI, preferred_element_type=F32)
    rkk = r * rk_ref[...] * (kdirs[0] + kdirs[1])
    bon_o[0] = _head_sums(rkk, bd) * v


def rwkv_prep(p_rwkv, nct, mu, w0, w_up, a0, a_up, g_up, k_k, k_a, r_k, bd):
    nb, ta, _ = p_rwkv.shape
    nt = ta // TT
    c = BRANCH_W
    nat = jax.ShapeDtypeStruct((nb, ta, c), F32)
    nat_spec = pl.BlockSpec((1, TT, c), lambda b, t: (b, t, 0))

    def full(a):
        nd = a.ndim
        return pl.BlockSpec(a.shape, lambda b, t: (0,) * nd)

    consts = [mu.reshape(1, -1), w0, w_up, a0, a_up, g_up, k_k.reshape(1, -1), k_a.reshape(1, -1),
              r_k.reshape(1, -1), bd]
    return pl.pallas_call(
        functools.partial(_rwkv_prep_kernel, nct, nt),
        out_shape=[nat] * 11,
        grid=(nb, nt),
        in_specs=_halo_specs(RWKV_COLS, ta) + [full(a) for a in consts],
        out_specs=[nat_spec] * 11,
        compiler_params=_cparams(("parallel", "parallel"), VMEM_LIMIT),
        name="rwkv_prep",
    )(p_rwkv, p_rwkv, p_rwkv, *consts)


CHUNK = 32


def _chunk_scan_rows(x, reverse):
    pos = lax.broadcasted_iota(jnp.int32, x.shape, 0) % CHUNK
    step = 1
    while step < CHUNK:
        if reverse:
            x = x + jnp.where(pos < CHUNK - step, pltpu.roll(x, x.shape[0] - step, axis=0), 0.0)
        else:
            x = x + jnp.where(pos >= step, pltpu.roll(x, step, axis=0), 0.0)
        step *= 2
    return x


def _chunk_prep_kernel(r_ref, kk_ref, v_ref, lw0, k0, b0, lw1, k1, b1,
                       a0_o, bm0_o, rp0_o, y00_o, a1_o, bm1_o, rp1_o, y01_o):
    blk = SCAN_BLK
    npair = BRANCH_W // LANES
    nchunk = blk // CHUNK
    r, kk, v = r_ref[0], kk_ref[0], v_ref[0]
    ti = lax.broadcasted_iota(jnp.int32, (blk, blk), 0)
    si = lax.broadcasted_iota(jnp.int32, (blk, blk), 1)
    same = (ti // CHUNK) == (si // CHUNK)
    eye = ti == si
    bd64 = (ti // HEAD_DIM) == (si // HEAD_DIM)
    head0 = si < HEAD_DIM
    lane_half = lax.broadcasted_iota(jnp.int32, (HEAD_DIM, LANES), 1)
    eyef = jnp.where(eye, 1.0, 0.0)
    dot = lambda x, y: jnp.dot(x, y, preferred_element_type=F32)
    lo = lambda x: x.astype(BF16)
    split = lambda x: jnp.concatenate([jnp.where(head0, x, 0.0), jnp.where(head0, 0.0, x)], axis=0)
    cat = lambda ms: lo(jnp.concatenate(ms, axis=1))

    probs = []
    for d, (lw_ref, k_ref, b_ref) in enumerate(((lw0, k0, b0), (lw1, k1, b1))):
        reverse = d == 1
        lw, k, b = lw_ref[0], k_ref[0], b_ref[0]
        lg = _chunk_scan_rows(lw, reverse)
        lg_end = lg + _chunk_scan_rows(lw, not reverse) - lw
        g, gi, g_end = jnp.exp(lg), jnp.exp(-lg), jnp.exp(lg_end)
        to_end = jnp.exp(lg_end - lg)
        arrs = (kk * jnp.exp(lg - lw), b * gi, k * gi, r * g, v, k * to_end, b * to_end, g_end)
        incl = jnp.logical_and(same, si >= ti if reverse else si <= ti)
        strict = jnp.logical_and(same, si > ti if reverse else si < ti)
        for p in range(npair):
            lanes = slice(p * LANES, (p + 1) * LANES)
            probs.append(dict(d=d, p=p, incl=incl, strict=strict, arrs=tuple(a[:, lanes] for a in arrs)))

    for q in probs:
        pp_, q_, kt_, rt_ = q["arrs"][:4]
        rhs_g = lo(jnp.concatenate([q_, kt_], axis=0))
        lm, mm, n2, nn = [], [], [], []
        for hh in range(2):
            hm = head0 if hh == 0 else jnp.logical_not(head0)
            lhs_g = lo(jnp.concatenate([jnp.where(hm, pp_, 0.0), jnp.where(hm, rt_, 0.0)], axis=0))
            gm = lax.dot_general(lhs_g, rhs_g, (((1,), (1,)), ((), ())), preferred_element_type=F32)
            lm.append(jnp.where(q["strict"], gm[:blk, :blk], 0.0))
            mm.append(jnp.where(q["strict"], gm[:blk, blk:], 0.0))
            n2.append(jnp.where(q["incl"], gm[blk:, :blk], 0.0))
            nn.append(jnp.where(q["incl"], gm[blk:, blk:], 0.0))
        q["pw"], q["tm"] = lm, [eyef - lm[0], eyef - lm[1]]
        q["m_cat"], q["n2_cat"], q["nn_cat"] = cat(mm), cat(n2), cat(nn)
    for _ in range(CHUNK.bit_length() - 2):
        for q in probs:
            pwl = [lo(x) for x in q["pw"]]
            q["pw"] = [dot(x, x) for x in pwl]
        for q in probs:
            q["tm"] = [dot(lo(t), lo(eyef + x)) for t, x in zip(q["tm"], q["pw"])]
    for q in probs:
        q["t_cat"] = cat(q["tm"])
        q["v_st"] = lo(split(q["arrs"][4]))
        q["pp"] = dot(q["t_cat"], lo(split(q["arrs"][0])))
        q["mv"] = dot(q["m_cat"], q["v_st"])
    for q in probs:
        q["w2"] = dot(q["t_cat"], lo(split(q["mv"])))
        q["rp"] = q["arrs"][3] - dot(q["n2_cat"], lo(split(q["pp"])))
    for q in probs:
        q["y0"] = dot(q["nn_cat"], q["v_st"]) - dot(q["n2_cat"], lo(split(q["w2"])))
    in_chunk = [si // CHUNK == c for c in range(nchunk)]
    for q in probs:
        ppt, vt, w2t = q["pp"].T, q["arrs"][4].T, q["w2"].T
        kg_, qg_ = q["arrs"][5], q["arrs"][6]
        lhs_a = jnp.concatenate([jnp.where(cm, ppt, 0.0) for cm in in_chunk], axis=0)
        q["pq"] = dot(lo(lhs_a), lo(qg_))
        lhs_b = jnp.concatenate([jnp.concatenate([jnp.where(cm, vt, 0.0), jnp.where(cm, -w2t, 0.0)], axis=1)
                                 for cm in in_chunk], axis=0)
        q["bf"] = dot(lo(lhs_b), lo(jnp.concatenate([kg_, qg_], axis=0)))
    outs = ((a0_o, bm0_o, rp0_o, y00_o), (a1_o, bm1_o, rp1_o, y01_o))
    for q in probs:
        a_o, bm_o = outs[q["d"]][:2]
        g_end_p = q["arrs"][7]
        for c in range(nchunk):
            pq_c = q["pq"][c * blk:(c + 1) * blk]
            a_o[0, 0, c, q["p"]] = (jnp.where(eye, g_end_p[c * CHUNK:c * CHUNK + 1], 0.0)
                                    - jnp.where(bd64, pq_c, 0.0)).astype(BF16)
            bm_o[0, 0, c, q["p"]] = jnp.where(lane_half < HEAD_DIM, q["bf"][c * blk:c * blk + HEAD_DIM],
                                              q["bf"][c * blk + HEAD_DIM:(c + 1) * blk])
    for d in range(2):
        rp_o, y0_o = outs[d][2:]
        rp_o[0] = jnp.concatenate([q["rp"] for q in probs if q["d"] == d], axis=1)
        y0_o[0] = jnp.concatenate([q["y0"] for q in probs if q["d"] == d], axis=1)


def chunk_prep(r, kk, v, lw0, lw1, k0, k1, b0, b1):
    nb, ta, c = r.shape
    nblk = ta // SCAN_BLK
    npair = c // LANES
    nchunk = SCAN_BLK // CHUNK
    nat = pl.BlockSpec((1, SCAN_BLK, c), lambda b, s: (b, s, 0))
    a_shape = jax.ShapeDtypeStruct((nb, nblk, nchunk, npair, LANES, LANES), BF16)
    bm_shape = jax.ShapeDtypeStruct((nb, nblk, nchunk, npair, HEAD_DIM, LANES), F32)
    nat_shape = jax.ShapeDtypeStruct((nb, ta, c), F32)
    a_spec = pl.BlockSpec((1, 1, nchunk, npair, LANES, LANES), lambda b, s: (b, s, 0, 0, 0, 0))
    bm_spec = pl.BlockSpec((1, 1, nchunk, npair, HEAD_DIM, LANES), lambda b, s: (b, s, 0, 0, 0, 0))
    return pl.pallas_call(
        _chunk_prep_kernel,
        out_shape=[a_shape, bm_shape, nat_shape, nat_shape] * 2,
        grid=(nb, nblk),
        in_specs=[nat] * 9,
        out_specs=[a_spec, bm_spec, nat, nat] * 2,
        compiler_params=_cparams(("parallel", "parallel"), VMEM_LIMIT),
        name="chunk_prep",
    )(r, kk, v, lw0, k0, b0, lw1, k1, b1)


def _chunk_scan_kernel(nb, a0, bm0, rp0, y00, a1, bm1, rp1, y01, yf_o, yb_o, s_scr):
    step = pl.program_id(0)
    npair = BRANCH_W // LANES
    nchunk = SCAN_BLK // CHUNK

    @pl.when(step == 0)
    def _():
        s_scr[...] = jnp.zeros_like(s_scr)

    lane = lax.broadcasted_iota(jnp.int32, (CHUNK, LANES), 1)
    refs = ((a0, bm0, rp0, y00, yf_o), (a1, bm1, rp1, y01, yb_o))
    for ci in range(nchunk):
        for d in range(2):
            a_ref, bm_ref, rp_ref, y0_ref, y_ref = refs[d]
            c = ci if d == 0 else nchunk - 1 - ci
            rows = slice(c * CHUNK, (c + 1) * CHUNK)
            for b in range(nb):
                for p in range(npair):
                    lanes = slice(p * LANES, (p + 1) * LANES)
                    s = s_scr[d, b, p]
                    rpc = rp_ref[b, rows, lanes]
                    lhs = jnp.concatenate([jnp.where(lane < HEAD_DIM, rpc, 0.0), jnp.where(lane >= HEAD_DIM, rpc, 0.0)],
                                          axis=0)
                    yh = lax.dot_general(lhs, s, (((1,), (1,)), ((), ())), preferred_element_type=F32)
                    y_ref[b, rows, lanes] = jnp.concatenate([yh[:CHUNK], yh[CHUNK:]], axis=1) + y0_ref[b, rows, lanes]
                    s_scr[d, b, p] = (jnp.dot(s.astype(BF16), a_ref[b, 0, c, p], preferred_element_type=F32)
                                      + bm_ref[b, 0, c, p])


def rwkv_scan(r, kk, v, lw0, lw1, k0, k1, b0, b1, lc):
    nb, ta, c = r.shape
    nblk = ta // SCAN_BLK
    nctb = lc // SCAN_BLK
    npair = c // LANES
    nchunk = SCAN_BLK // CHUNK
    a0, bm0, rp0, y00, a1, bm1, rp1, y01 = chunk_prep(r, kk, v, lw0, lw1, k0, k1, b0, b1)

    def fwd(s):
        return s

    def bwd(s):
        return jnp.where(s < nctb, nctb - 1 - s, nblk - 1 - (s - nctb))

    def specs(idx):
        return [pl.BlockSpec((nb, 1, nchunk, npair, LANES, LANES), lambda s: (0, idx(s), 0, 0, 0, 0)),
                pl.BlockSpec((nb, 1, nchunk, npair, HEAD_DIM, LANES), lambda s: (0, idx(s), 0, 0, 0, 0)),
                pl.BlockSpec((nb, SCAN_BLK, c), lambda s: (0, idx(s), 0)),
                pl.BlockSpec((nb, SCAN_BLK, c), lambda s: (0, idx(s), 0))]

    out = jax.ShapeDtypeStruct((nb, ta, c), F32)
    return pl.pallas_call(
        functools.partial(_chunk_scan_kernel, nb),
        out_shape=[out, out],
        grid=(nblk,),
        in_specs=specs(fwd) + specs(bwd),
        out_specs=[pl.BlockSpec((nb, SCAN_BLK, c), lambda s: (0, fwd(s), 0)),
                   pl.BlockSpec((nb, SCAN_BLK, c), lambda s: (0, bwd(s), 0))],
        scratch_shapes=[pltpu.VMEM((2, nb, npair, HEAD_DIM, LANES), F32)],
        compiler_params=_cparams(("arbitrary",), VMEM_LIMIT),
        name="chunk_scan",
    )(a0, bm0, rp0, y00, a1, bm1, rp1, y01)


def _rwkv_out_kernel(yf_ref, yb_ref, g_ref, bon_ref, lg_ref, lb_ref, bd_ref, o_ref):
    y = yf_ref[0] + yb_ref[0]
    bd = bd_ref[...]
    mu = _head_sums(y, bd) * (1.0 / HEAD_DIM)
    yc = y - mu
    var = _head_sums(yc * yc, bd) * (1.0 / HEAD_DIM)
    yn = yc * lax.rsqrt(var + RWKV_GN_EPS) * lg_ref[...] + lb_ref[...]
    o_ref[0] = ((yn + bon_ref[0]) * g_ref[0]).astype(BF16)


def rwkv_out(yf, yb, g, bon, lnx_g, lnx_b, bd):
    nb, ta, _ = yf.shape
    nat_spec = pl.BlockSpec((1, TT, BRANCH_W), lambda b, t: (b, t, 0))
    row = pl.BlockSpec((1, BRANCH_W), lambda b, t: (0, 0))
    return pl.pallas_call(
        _rwkv_out_kernel,
        out_shape=jax.ShapeDtypeStruct((nb, ta, BRANCH_W), BF16),
        grid=(nb, ta // TT),
        in_specs=[nat_spec] * 4 + [row, row, pl.BlockSpec(bd.shape, lambda b, t: (0, 0))],
        out_specs=nat_spec,
        compiler_params=_cparams(("parallel", "parallel")),
        name="rwkv_out",
    )(yf, yb, g, bon, lnx_g.reshape(1, -1), lnx_b.reshape(1, -1), bd)


def _pair_swap(x):
    lane = lax.broadcasted_iota(jnp.int32, x.shape, 1)
    n = x.shape[1]
    return jnp.where(lane % 2 == 0, pltpu.roll(x, n - 1, axis=1), pltpu.roll(x, 1, axis=1))


def _attn_prep_kernel(p_ref, cos_ref, sin_ref, qg_ref, kg_ref, bd_ref, q_o, k_o, v_o):
    p = p_ref[0]
    q, k, v = p[:, :Q_W], p[:, Q_W:Q_W + KV_W], p[:, Q_W + KV_W:]
    bd = bd_ref[...]
    cos, sin = cos_ref[...], sin_ref[...]
    qms = _head_sums(q * q, bd) * (1.0 / HEAD_DIM)
    qn = q * lax.rsqrt(qms + RMS_EPS) * qg_ref[...]
    qr = qn * cos + _pair_swap(qn) * sin
    q_o[0] = (qr * HEAD_DIM ** -0.5).astype(BF16)
    kms = _head_sums(k * k, bd[:KV_W, :KV_W]) * (1.0 / HEAD_DIM)
    kn = k * lax.rsqrt(kms + RMS_EPS) * kg_ref[...]
    kr = kn * cos[:, :KV_W] + _pair_swap(kn) * sin[:, :KV_W]
    for g in range(KV_W // HEAD_DIM):
        sl = slice(g * HEAD_DIM, (g + 1) * HEAD_DIM)
        k_o[0, g] = kr[:, sl].astype(BF16)
        v_o[0, g] = v[:, sl].astype(BF16)


def attn_prep(p_attn, cos_t, sin_t, q_norm, k_norm, bd):
    nb, ta, _ = p_attn.shape
    ng = KV_W // HEAD_DIM
    qg = jnp.tile(q_norm, Q_W // HEAD_DIM).reshape(1, -1)
    kg = jnp.tile(k_norm, ng).reshape(1, -1)
    kv_shape = jax.ShapeDtypeStruct((nb, ng, ta, HEAD_DIM), BF16)
    kv_spec = pl.BlockSpec((1, ng, TT, HEAD_DIM), lambda b, t: (b, 0, t, 0))
    return pl.pallas_call(
        _attn_prep_kernel,
        out_shape=[jax.ShapeDtypeStruct((nb, ta, Q_W), BF16), kv_shape, kv_shape],
        grid=(nb, ta // TT),
        in_specs=[pl.BlockSpec((1, TT, ATTN_COLS), lambda b, t: (b, t, 0)),
                  pl.BlockSpec((TT, Q_W), lambda b, t: (t, 0)),
                  pl.BlockSpec((TT, Q_W), lambda b, t: (t, 0)),
                  pl.BlockSpec((1, Q_W), lambda b, t: (0, 0)),
                  pl.BlockSpec((1, KV_W), lambda b, t: (0, 0)),
                  pl.BlockSpec(bd.shape, lambda b, t: (0, 0))],
        out_specs=[pl.BlockSpec((1, TT, Q_W), lambda b, t: (b, t, 0)), kv_spec, kv_spec],
        compiler_params=_cparams(("parallel", "parallel")),
        name="attn_prep",
    )(p_attn, cos_t, sin_t, qg, kg, bd)


def _attn_kernel(nct, lc, q_ref, k_ref, v_ref, o_ref):
    t = pl.program_id(1)
    ng = k_ref.shape[1]
    rep = Q_W // HEAD_DIM // ng

    def run(nk):
        outs = []
        for g in range(ng):
            kk, vv = k_ref[0, g, :nk, :], v_ref[0, g, :nk, :]
            for r in range(rep):
                h = g * rep + r
                q = q_ref[0][:, h * HEAD_DIM:(h + 1) * HEAD_DIM]
                s = lax.dot_general(q, kk, (((1,), (1,)), ((), ())), preferred_element_type=F32)
                p = jnp.exp(s - jnp.max(s, -1, keepdims=True))
                l = jnp.sum(p, -1, keepdims=True)
                o = jnp.dot(p.astype(BF16), vv, preferred_element_type=F32)
                outs.append(o / l)
        o_ref[0] = jnp.concatenate(outs, axis=1).astype(BF16)

    @pl.when(t < nct)
    def _():
        run(lc)

    @pl.when(t >= nct)
    def _():
        run(k_ref.shape[2])


def attention(q, k, v, lc):
    nb, ta, _ = q.shape
    ng = k.shape[1]
    nct = lc // TT
    qo_spec = pl.BlockSpec((1, TT, Q_W), lambda b, t: (b, t, 0))
    kv_spec = pl.BlockSpec((1, ng, ta, HEAD_DIM), lambda b, t: (b, 0, 0, 0))
    return pl.pallas_call(
        functools.partial(_attn_kernel, nct, lc),
        out_shape=jax.ShapeDtypeStruct((nb, ta, Q_W), BF16),
        grid=(nb, ta // TT),
        in_specs=[qo_spec, kv_spec, kv_spec],
        out_specs=qo_spec,
        compiler_params=_cparams(("parallel", "arbitrary"), VMEM_LIMIT),
        name="attention",
    )(q, k, v)


def _hs_pre_kernel(nct, nt, cur_ref, prev_ref, next_ref, hw_ref, sw_ref, x0_o, u_o, ycv_o):
    cur = cur_ref[0]
    xm1, xp1 = _neighbours(cur, prev_ref, next_ref, nct, nt)
    c = BRANCH_W
    hc = HYENA_COLS
    hw = hw_ref[...]
    ph = hw[0:1] * xm1[:, :hc] + hw[1:2] * cur[:, :hc] + hw[2:3] * xp1[:, :hc]
    x0_o[0] = ph[:, :c]
    u_o[0] = ph[:, c:2 * c] * ph[:, 2 * c:3 * c]
    sw = sw_ref[...]

    def cx(a):
        return a[:, hc + c:hc + 2 * c] * a[:, hc + 2 * c:hc + 3 * c]

    conv = sw[0:1] * cx(xm1) + sw[1:2] * cx(cur) + sw[2:3] * cx(xp1)
    ycv_o[0] = (cur[:, hc:hc + c] * conv).astype(BF16)


def hs_pre(p_hs, nct, hyena_conv, sconv_w):
    nb, ta, w = p_hs.shape
    nt = ta // TT
    nat = pl.BlockSpec((1, TT, BRANCH_W), lambda b, t: (b, t, 0))
    return pl.pallas_call(
        functools.partial(_hs_pre_kernel, nct, nt),
        out_shape=[jax.ShapeDtypeStruct((nb, ta, BRANCH_W), F32),
                   jax.ShapeDtypeStruct((nb, ta, BRANCH_W), F32),
                   jax.ShapeDtypeStruct((nb, ta, BRANCH_W), BF16)],
        grid=(nb, nt),
        in_specs=_halo_specs(w, ta) + [pl.BlockSpec(hyena_conv.shape, lambda b, t: (0, 0)),
                                       pl.BlockSpec(sconv_w.shape, lambda b, t: (0, 0))],
        out_specs=[nat, nat, nat],
        compiler_params=_cparams(("parallel", "parallel"), VMEM_LIMIT),
        name="hs_pre",
    )(p_hs, p_hs, p_hs, hyena_conv, sconv_w)


EMB_PAD = 40


def _filter_tables(lh):
    n = np.arange(2 * lh)
    pos = np.abs(n - (lh - 1)).astype(np.float64)
    bands = (HYENA_EMB - 1) // 2
    t = np.minimum(pos, lh - 1) / (lh - 1)
    wpos = 2.0 * math.pi * pos / lh
    f = np.linspace(1e-4, bands - 1, bands)[:, None]
    z = np.zeros((EMB_PAD, 2 * lh), np.float32)
    z[0] = t
    z[1:1 + bands] = np.cos(f * wpos[None, :])
    z[1 + bands:1 + 2 * bands] = -np.sin(f * wpos[None, :])
    max_decay = math.log(HYENA_TARGET) / HYENA_FAST_DECAY
    min_decay = math.log(HYENA_TARGET) / HYENA_SLOW_DECAY
    deltas = np.abs(np.linspace(min_decay, max_decay, BRANCH_W)).astype(np.float32)
    return z, deltas.reshape(-1, 1)


def _filter_kernel(lh, tn, z_ref, w1_ref, b1_ref, f1_ref, w2_ref, b2_ref, f2_ref, w3_ref, dl_ref, o_ref):
    z = z_ref[...]
    h1 = jnp.sin(f1_ref[...] * (jnp.dot(w1_ref[...], z, precision=HI, preferred_element_type=F32) + b1_ref[...]))
    h2 = jnp.sin(f2_ref[...] * (jnp.dot(w2_ref[...], h1, precision=HI, preferred_element_type=F32) + b2_ref[...]))
    f = jnp.dot(w3_ref[...], h2, precision=HI, preferred_element_type=F32)
    n = pl.program_id(0) * tn + lax.broadcasted_iota(jnp.int32, (1, tn), 1)
    filt = jnp.where(n >= lh - 1, f[:BRANCH_W], f[BRANCH_W:])
    win = jnp.exp(-z[0:1, :] * dl_ref[...])
    o_ref[...] = jnp.where(n == 2 * lh - 1, 0.0, filt * win)


def hyena_filter_table(lh, w1, b1, f1, w2, b2, f2, w3):
    z_np, dl_np = _filter_tables(lh)
    n2 = 2 * lh
    tn = _pick_tile(n2, (1024, 512))
    hd = w2.shape[0]
    w1t = jnp.zeros((hd, EMB_PAD), F32).at[:, :HYENA_EMB].set(w1.T)
    args = [jnp.asarray(z_np), w1t, b1.reshape(-1, 1), f1.reshape(-1, 1), w2.T, b2.reshape(-1, 1),
            f2.reshape(-1, 1), w3.T, jnp.asarray(dl_np)]

    def full(a):
        return pl.BlockSpec(a.shape, lambda j: (0, 0))

    return pl.pallas_call(
        functools.partial(_filter_kernel, lh, tn),
        out_shape=jax.ShapeDtypeStruct((BRANCH_W, n2), F32),
        grid=(n2 // tn,),
        in_specs=[pl.BlockSpec((EMB_PAD, tn), lambda j: (0, j))] + [full(a) for a in args[1:]],
        out_specs=pl.BlockSpec((BRANCH_W, tn), lambda j: (0, j)),
        compiler_params=_cparams(("parallel",)),
        name="hyena_filter",
    )(*args)


def _hyena_conv_kernel(nblk, bp, nch, k_ref, u_ref, o_ref, t_scr):
    ntile = 4 * nblk - 1
    mc = 2 * nblk - 1
    for ch in range(nch):
        for m in range(ntile):
            win = jnp.concatenate([k_ref[ch, m:m + 1, :], k_ref[ch, m + 1:m + 2, :]], axis=1)
            x = jnp.broadcast_to(win, (LANES, 2 * LANES))
            t_scr[ch, m] = pltpu.roll(x, LANES + 1, 1, stride=1, stride_axis=0)[:, :LANES].astype(BF16)

    for ch in range(nch):
        for d in [0] + [s * a for a in range(1, nblk) for s in (1, -1)]:
            m0 = 2 * d + mc
            w = jnp.concatenate([jnp.concatenate([t_scr[ch, m0], t_scr[ch, m0 + 1]], axis=1),
                                 jnp.concatenate([t_scr[ch, m0 - 1], t_scr[ch, m0]], axis=1)], axis=0)
            i0, i1 = max(0, d), min(nblk, nblk + d)
            lhs = u_ref[ch, (i0 - d) * bp:(i1 - d) * bp, :].astype(BF16)
            res = jnp.dot(lhs, w, preferred_element_type=F32)
            if d == 0:
                o_ref[ch] = res
            else:
                o_ref[ch, i0 * bp:i1 * bp, :] += res


def hyena_conv(u, ktab):
    nb, l, c = u.shape
    nblk = l // HY_BLK
    bp = -(-nb // SUBLANES) * SUBLANES
    nch = SUBLANES if nblk == 1 else 1
    ut = jnp.transpose(u.reshape(nb, nblk, HY_BLK, c), (3, 1, 0, 2))
    if bp != nb:
        ut = jnp.pad(ut, ((0, 0), (0, 0), (0, bp - nb), (0, 0)))
    ut = ut.reshape(c, nblk * bp, HY_BLK)
    k3 = ktab.reshape(c, 4 * nblk, LANES)
    out = pl.pallas_call(
        functools.partial(_hyena_conv_kernel, nblk, bp, nch),
        out_shape=jax.ShapeDtypeStruct((c, nblk * bp, HY_BLK), F32),
        grid=(c // nch,),
        in_specs=[pl.BlockSpec((nch, 4 * nblk, LANES), lambda ch: (ch, 0, 0)),
                  pl.BlockSpec((nch, nblk * bp, HY_BLK), lambda ch: (ch, 0, 0))],
        out_specs=pl.BlockSpec((nch, nblk * bp, HY_BLK), lambda ch: (ch, 0, 0)),
        scratch_shapes=[pltpu.VMEM((nch, 4 * nblk - 1, LANES, LANES), BF16)],
        compiler_params=_cparams(("parallel",)),
        name="hyena_conv",
    )(k3, ut)
    out = out.reshape(c, nblk, bp, HY_BLK)[:, :, :nb]
    return jnp.transpose(out, (2, 1, 3, 0)).reshape(nb, l, c)


def _route(logits, bias):
    s = jax.nn.sigmoid(logits)
    sel = s + bias
    srow = [s[e:e + 1] for e in range(N_EXPERTS)]
    row = [sel[e:e + 1] for e in range(N_EXPERTS)]
    best, gi = None, None
    for g in range(N_GROUPS):
        a, b, c, d = row[4 * g:4 * g + 4]
        hi1, lo1, hi2, lo2 = jnp.maximum(a, b), jnp.minimum(a, b), jnp.maximum(c, d), jnp.minimum(c, d)
        score = jnp.maximum(hi1, hi2) + jnp.maximum(jnp.minimum(hi1, hi2), jnp.maximum(lo1, lo2))
        if g == 0:
            best, gi = score, jnp.zeros(score.shape, jnp.int32)
        else:
            better = score > best
            gi = jnp.where(better, g, gi)
            best = jnp.where(better, score, best)
    neg = -jnp.inf
    msel = [jnp.where(gi == e // EXPERTS_PER_GROUP, row[e], neg) for e in range(N_EXPERTS)]

    def arg_first_max(vals):
        bv, bi = vals[0], jnp.zeros(vals[0].shape, jnp.int32)
        for e in range(1, N_EXPERTS):
            better = vals[e] > bv
            bi = jnp.where(better, e, bi)
            bv = jnp.where(better, vals[e], bv)
        return bi

    i1 = arg_first_max(msel)
    i2 = arg_first_max([jnp.where(i1 == e, neg, msel[e]) for e in range(N_EXPERTS)])
    w1 = sum(jnp.where(i1 == e, srow[e], 0.0) for e in range(N_EXPERTS))
    w2 = sum(jnp.where(i2 == e, srow[e], 0.0) for e in range(N_EXPERTS))
    den = w1 + w2
    g1, g2 = w1 / den, w2 / den
    rows = [jnp.where(i1 == e, g1, 0.0) + jnp.where(i2 == e, g2, 0.0) for e in range(N_EXPERTS)]
    rows.append(gi.astype(F32))
    rows.extend([jnp.zeros_like(g1)] * (ROUTE_ROWS - len(rows)))
    return jnp.concatenate(rows, axis=0)


def _merge_kernel(ya_ref, x0_ref, u_ref, yc_ref, ycv_ref, yd_ref, h_ref, x_ref, ga_ref, shf_ref, scf_ref,
                  skip_ref, wg_ref, wb_ref, wo_ref, g1_ref, b1_ref, rwt_ref, rb_ref, x1_o, hf_o, gates_o):
    yb = (x0_ref[0] * (yc_ref[0] + u_ref[0] * skip_ref[...])).astype(BF16)
    ys = (ya_ref[0], yb, ycv_ref[0], yd_ref[0])
    merged = None
    for n in range(N_BRANCHES):
        gate = jax.nn.sigmoid(jnp.dot(h_ref[0], wg_ref[:, n * D_MODEL:(n + 1) * D_MODEL], preferred_element_type=F32))
        term = gate * jnp.dot(ys[n], wb_ref[n], preferred_element_type=F32)
        merged = term if merged is None else merged + term
    out = jnp.dot(merged.astype(BF16), wo_ref[...], preferred_element_type=F32)
    alpha = (2 * 2) ** 0.25
    x1 = _ln(alpha * x_ref[0] + ga_ref[0] * out) * g1_ref[...] + b1_ref[...]
    hf = _ln(x1) * (1.0 + scf_ref[0]) + shf_ref[0]
    x1_o[0] = x1
    hf_o[0] = hf.astype(BF16)
    logits = lax.dot_general(rwt_ref[...], hf, (((1,), (1,)), ((), ())), precision=HI,
                             preferred_element_type=F32)
    gates_o[...] = _route(logits, rb_ref[...])


def merge(ya, x0, u, yconv, ycv, yd, h, x_all, mod3, nct, skip, wg, wb, wo, ln_g, ln_b, rwt, rbias):
    nb, ta, d = x_all.shape
    nt = ta // TT
    nat = pl.BlockSpec((1, TT, BRANCH_W), lambda b, t: (b, t, 0))
    wide = pl.BlockSpec((1, TT, d), lambda b, t: (b, t, 0))

    def full(a):
        nd = a.ndim
        return pl.BlockSpec(a.shape, lambda b, t: (0,) * nd)

    consts = [skip.reshape(1, -1), wg, wb, wo, ln_g.reshape(1, -1), ln_b.reshape(1, -1), rwt, rbias.reshape(-1, 1)]
    return pl.pallas_call(
        _merge_kernel,
        out_shape=[jax.ShapeDtypeStruct((nb, ta, d), F32), jax.ShapeDtypeStruct((nb, ta, d), BF16),
                   jax.ShapeDtypeStruct((ROUTE_ROWS, nb * ta), F32)],
        grid=(nb, nt),
        in_specs=[nat] * 6 + [wide, wide,
                              _mod_spec(2, nct, nb), _mod_spec(3, nct, nb), _mod_spec(4, nct, nb)]
                 + [full(a) for a in consts],
        out_specs=[wide, wide, pl.BlockSpec((ROUTE_ROWS, TT), lambda b, t: (0, b * nt + t))],
        compiler_params=_cparams(("parallel", "parallel"), VMEM_LIMIT),
        name="merge",
    )(ya, x0, u, yconv, ycv, yd, h, x_all, mod3, mod3, mod3, *consts)


MOE_TILE = 1024
MOE_ALIGN = 2 * SUBLANES
MOE_CHUNK = 18 * MOE_ALIGN
MOE_SORTED = MOE_TILE + LANES
MOE_ROWS = MOE_SORTED + 3 * LANES
GID_ROW = N_EXPERTS
ROUTE_ROWS = 3 * SUBLANES
META_LANES = 2 * N_GROUPS
assert N_GROUPS * (MOE_ALIGN - 1) <= MOE_SORTED - MOE_TILE and MOE_SORTED + MOE_CHUNK <= MOE_ROWS


def _moe_sort_kernel(gt_ref, g_ref, h_ref, up_ref, hs_o, gs_o, pt_o, meta_o):
    gid = gt_ref[GID_ROW:GID_ROW + 1, :]
    onehot = [jnp.where(gid == float(g), 1.0, 0.0) for g in range(N_GROUPS)]
    g4 = jnp.concatenate(onehot + [jnp.zeros((SUBLANES - N_GROUPS, MOE_TILE), F32)], axis=0)
    before = jnp.dot(g4.astype(BF16), up_ref[...], preferred_element_type=F32)
    lane = lax.broadcasted_iota(jnp.int32, (SUBLANES, LANES), 1)
    meta = jnp.zeros((SUBLANES, LANES), F32)
    off = jnp.zeros((1, 1), F32)
    pos = jnp.zeros((1, MOE_TILE), F32)
    for g in range(N_GROUPS):
        cnt = jnp.sum(onehot[g], axis=1, keepdims=True)
        pos = pos + onehot[g] * (before[g:g + 1] + off)
        meta = jnp.where(lane == g, off, meta)
        meta = jnp.where(lane == N_GROUPS + g, cnt, meta)
        off = off + jnp.ceil(cnt * (1.0 / MOE_ALIGN)) * MOE_ALIGN
    meta_o[0] = meta.astype(jnp.int32)
    row = lax.broadcasted_iota(jnp.int32, (MOE_ROWS, MOE_TILE), 0)
    place = jnp.where(row == pos.astype(jnp.int32), 1.0, 0.0)
    p16 = place.astype(BF16)
    hs_o[0] = jnp.dot(p16, h_ref[...], preferred_element_type=F32).astype(BF16)
    gts = g_ref[...]
    hi = gts.astype(BF16)
    r1 = gts - hi.astype(F32)
    mid = r1.astype(BF16)
    low = (r1 - mid.astype(F32)).astype(BF16)
    gs_o[0] = (jnp.dot(p16, hi, preferred_element_type=F32) + jnp.dot(p16, mid, preferred_element_type=F32)
               + jnp.dot(p16, low, preferred_element_type=F32))
    pt_o[...] = place.T.astype(BF16)


def moe_sort(hf, gates_t, gates):
    n, d = hf.shape
    ntile = n // MOE_TILE
    upper = jnp.asarray(np.triu(np.ones((MOE_TILE, MOE_TILE), np.float32), 1), BF16)
    return pl.pallas_call(
        _moe_sort_kernel,
        out_shape=[jax.ShapeDtypeStruct((ntile, MOE_ROWS, d), BF16),
                   jax.ShapeDtypeStruct((ntile, MOE_ROWS, LANES), F32),
                   jax.ShapeDtypeStruct((n, MOE_ROWS), BF16),
                   jax.ShapeDtypeStruct((ntile, SUBLANES, LANES), jnp.int32)],
        grid=(ntile,),
        in_specs=[pl.BlockSpec((ROUTE_ROWS, MOE_TILE), lambda i: (0, i)),
                  pl.BlockSpec((MOE_TILE, LANES), lambda i: (i, 0)),
                  pl.BlockSpec((MOE_TILE, d), lambda i: (i, 0)),
                  pl.BlockSpec((MOE_TILE, MOE_TILE), lambda i: (0, 0))],
        out_specs=[pl.BlockSpec((1, MOE_ROWS, d), lambda i: (i, 0, 0)),
                   pl.BlockSpec((1, MOE_ROWS, LANES), lambda i: (i, 0, 0)),
                   pl.BlockSpec((MOE_TILE, MOE_ROWS), lambda i: (i, 0)),
                   pl.BlockSpec((1, SUBLANES, LANES), lambda i: (i, 0, 0))],
        compiler_params=_cparams(("parallel",), VMEM_LIMIT),
        name="moe_sort",
    )(gates_t, gates, hf, upper)


def _moe_group_kernel(meta_ref, hs_ref, gs_ref, w1_ref, w3_ref, w2_ref, ysb_o, ys_acc):
    i, g = pl.program_id(0), pl.program_id(1)

    @pl.when(g == 0)
    def _():
        ys_acc[...] = jnp.zeros_like(ys_acc)

    off = meta_ref[i * META_LANES + g]
    cnt = meta_ref[i * META_LANES + N_GROUPS + g]
    lane = lax.broadcasted_iota(jnp.int32, (MOE_CHUNK, LANES), 1)

    def chunk(j, carry):
        rows = pl.ds(pl.multiple_of(off + j * MOE_CHUNK, MOE_ALIGN), MOE_CHUNK)
        hs = hs_ref[0, rows, :]
        gs = gs_ref[0, rows, :]
        acc = None
        for e in range(EXPERTS_PER_GROUP):
            a = jnp.dot(hs, w1_ref[e], preferred_element_type=F32)
            b = jnp.dot(hs, w3_ref[e], preferred_element_type=F32)
            act = (a * jax.nn.sigmoid(a)) * b
            gcol = jnp.sum(jnp.where(lane == g * EXPERTS_PER_GROUP + e, gs, 0.0), axis=1, keepdims=True)
            term = gcol * jnp.dot(act.astype(BF16), w2_ref[e], preferred_element_type=F32)
            acc = term if acc is None else acc + term
        ys_acc[rows, :] += acc
        return carry

    lax.fori_loop(0, lax.div(cnt + (MOE_CHUNK - 1), MOE_CHUNK), chunk, 0)

    @pl.when(g == N_GROUPS - 1)
    def _():
        ysb_o[0] = ys_acc[...].astype(BF16)


def moe_group(hs, gs, meta, w1, w3, w2):
    ntile, _, d = hs.shape
    de = w1.shape[2]
    epg = EXPERTS_PER_GROUP
    grid_spec = pltpu.PrefetchScalarGridSpec(
        num_scalar_prefetch=1,
        grid=(ntile, N_GROUPS),
        in_specs=[pl.BlockSpec((1, MOE_ROWS, d), lambda i, g, m: (i, 0, 0)),
                  pl.BlockSpec((1, MOE_ROWS, LANES), lambda i, g, m: (i, 0, 0)),
                  pl.BlockSpec((epg, d, de), lambda i, g, m: (g, 0, 0)),
                  pl.BlockSpec((epg, d, de), lambda i, g, m: (g, 0, 0)),
                  pl.BlockSpec((epg, de, d), lambda i, g, m: (g, 0, 0))],
        out_specs=pl.BlockSpec((1, MOE_ROWS, d), lambda i, g, m: (i, 0, 0)),
        scratch_shapes=[pltpu.VMEM((MOE_ROWS, d), F32)],
    )
    return pl.pallas_call(
        _moe_group_kernel,
        out_shape=jax.ShapeDtypeStruct((ntile, MOE_ROWS, d), BF16),
        grid_spec=grid_spec,
        compiler_params=_cparams(("parallel", "arbitrary"), VMEM_LIMIT),
        name="moe_group",
    )(meta, hs, gs, w1, w3, w2)


def _ln2_kernel(emit_next, x_ref, pt_ref, ys_ref, gf_ref, g_ref, b_ref, *rest):
    alpha = (2 * 2) ** 0.25
    f = jnp.dot(pt_ref[...], ys_ref[0], preferred_element_type=F32)
    x2 = _ln(alpha * x_ref[0] + gf_ref[0] * f) * g_ref[...] + b_ref[...]
    if emit_next:
        sh_ref, sc_ref, o_ref, h_o = rest
        h_o[0] = (_ln(x2) * (1.0 + sc_ref[0]) + sh_ref[0]).astype(BF16)
    else:
        o_ref, = rest
    o_ref[0] = x2


def ln2(x1, pt, ysb, mod3, nct, ln_g, ln_b, mod3_next):
    nb, ta, d = x1.shape
    nt = ta // TT
    per = MOE_TILE // TT
    wide = pl.BlockSpec((1, TT, d), lambda b, t: (b, t, 0))
    row = pl.BlockSpec((1, d), lambda b, t: (0, 0))
    in_specs = [wide, pl.BlockSpec((TT, MOE_SORTED), lambda b, t: (b * nt + t, 0)),
                pl.BlockSpec((1, MOE_SORTED, d), lambda b, t: ((b * nt + t) // per, 0, 0)),
                _mod_spec(5, nct, nb), row, row]
    args = [x1, pt, ysb, mod3, ln_g.reshape(1, -1), ln_b.reshape(1, -1)]
    if mod3_next is None:
        out_shape = jax.ShapeDtypeStruct((nb, ta - nct * TT, d), F32)
        out_specs = pl.BlockSpec((1, TT, d), lambda b, t: (b, jnp.maximum(t - nct, 0), 0))
    else:
        in_specs += [_mod_spec(0, nct, nb), _mod_spec(1, nct, nb)]
        args += [mod3_next, mod3_next]
        out_shape = [jax.ShapeDtypeStruct((nb, ta, d), F32), jax.ShapeDtypeStruct((nb, ta, d), BF16)]
        out_specs = [wide, wide]
    return pl.pallas_call(
        functools.partial(_ln2_kernel, mod3_next is not None),
        out_shape=out_shape,
        grid=(nb, nt),
        in_specs=in_specs,
        out_specs=out_specs,
        compiler_params=_cparams(("parallel", "arbitrary"), VMEM_LIMIT),
        name="ln2",
    )(*args)


def _rope_tables(l, lc):
    half = HEAD_DIM // 2
    inv = ROPE_THETA ** (-np.arange(0, half, 2, dtype=np.float64) / half)
    t = np.arange(l)
    rows, cols = t // GRID_W, t % GRID_W
    ang = np.concatenate([rows[:, None] * inv, cols[:, None] * inv], -1)
    ang = np.concatenate([np.zeros((lc, half)), ang], 0)
    cos = np.repeat(np.cos(ang), 2, axis=1)
    sin = np.repeat(np.sin(ang), 2, axis=1)
    sin[:, 0::2] *= -1.0
    reps = Q_W // HEAD_DIM
    return (jnp.asarray(np.tile(cos, (1, reps)), F32), jnp.asarray(np.tile(sin, (1, reps)), F32))


def _block_diag_ones():
    i = np.arange(BRANCH_W) // HEAD_DIM
    return jnp.asarray((i[:, None] == i[None, :]).astype(np.float32))


def kernel(x, c, ctx, c_ctx, ada_w, ada_b, w_in, rwkv_mu, rwkv_w0, rwkv_w_up, rwkv_a0, rwkv_a_up, rwkv_g_up, rwkv_k_k, rwkv_k_a, rwkv_r_k, rwkv_lnx_g, rwkv_lnx_b, hyena_conv, hyena_w1, hyena_b1, hyena_freq1, hyena_w2, hyena_b2, hyena_freq2, hyena_w3, hyena_skip, sconv_w, attn_q_norm, attn_k_norm, w_branch, w_out, ln1_g, ln1_b, ln2_g, ln2_b, router_w, router_bias, exp_w1, exp_w3, exp_w2):
    nb, l, d = x.shape
    lc = ctx.shape[1]
    depth = ada_w.shape[0]
    assert d == D_MODEL and lc % TT == 0 and l % TT == 0 and l % GRID_W == 0 and (nb * (lc + l)) % MOE_TILE == 0
    ta = lc + l
    nct = lc // TT
    x_all = jnp.concatenate([ctx, x], axis=1)

    mod_rows = -(-(nb + 1) // SUBLANES) * SUBLANES
    cc = jnp.zeros((mod_rows, d), F32).at[:nb].set(c).at[nb].set(c_ctx)
    mod = ada_mod(cc, ada_w, ada_b)

    cos_t, sin_t = _rope_tables(l, lc)
    bd = _block_diag_ones()
    rwt = router_w.T

    for li in range(depth):
        mod3 = mod[li].reshape(mod_rows, 1, N_MOD * d)
        wl = w_in[li].astype(BF16)
        if li == 0:
            h3 = lnmod(x_all, mod3, nct, 0, 1)
        h = h3.reshape(nb * ta, d)
        p_rwkv = matmul(h, wl[:, :OFF_HYENA]).reshape(nb, ta, -1)
        p_hs = matmul(h, wl[:, OFF_HYENA:OFF_ATTN]).reshape(nb, ta, -1)
        p_attn = matmul(h, wl[:, OFF_ATTN:OFF_GATE]).reshape(nb, ta, -1)

        r, kk, w0, w1, k0, k1, b0, b1, v, g, bon = rwkv_prep(
            p_rwkv, nct, rwkv_mu[li], rwkv_w0[li], rwkv_w_up[li], rwkv_a0[li], rwkv_a_up[li], rwkv_g_up[li],
            rwkv_k_k[li], rwkv_k_a[li], rwkv_r_k[li], bd)
        yf, yb = rwkv_scan(r, kk, v, w0, w1, k0, k1, b0, b1, lc)
        ya = rwkv_out(yf, yb, g, bon, rwkv_lnx_g[li], rwkv_lnx_b[li], bd)

        q, kx, vx = attn_prep(p_attn, cos_t, sin_t, attn_q_norm[li], attn_k_norm[li], bd)
        yd = attention(q, kx, vx, lc)

        x0, u, ycv = hs_pre(p_hs, nct, hyena_conv[li], sconv_w[li])
        fargs = (hyena_w1[li], hyena_b1[li], hyena_freq1[li], hyena_w2[li], hyena_b2[li], hyena_freq2[li],
                 hyena_w3[li])
        yconv_ctx = hyena_conv_seg(u[:, :lc], fargs) if li < depth - 1 else jnp.zeros((nb, lc, BRANCH_W), F32)
        yconv = jnp.concatenate([yconv_ctx, hyena_conv_seg(u[:, lc:], fargs)], axis=1)

        x1, hf, gates_t = merge(ya, x0, u, yconv, ycv, yd, h3, x_all, mod3, nct, hyena_skip[li], wl[:, OFF_GATE:],
                                w_branch[li].astype(BF16), w_out[li].astype(BF16), ln1_g[li], ln1_b[li],
                                rwt, router_bias)
        gates = jnp.pad(gates_t.T, ((0, 0), (0, LANES - ROUTE_ROWS)))
        hs, gs, pt, meta = moe_sort(hf.reshape(nb * ta, d), gates_t, gates)
        ysb = moe_group(hs, gs, meta[:, 0, :META_LANES].reshape(-1), exp_w1[li].astype(BF16),
                        exp_w3[li].astype(BF16), exp_w2[li].astype(BF16))
        if li == depth - 1:
            return ln2(x1, pt, ysb, mod3, nct, ln2_g[li], ln2_b[li], None)
        x_all, h3 = ln2(x1, pt, ysb, mod3, nct, ln2_g[li], ln2_b[li], mod[li + 1].reshape(mod_rows, 1, N_MOD * d))


def hyena_conv_seg(u_seg, fargs):
    ktab = hyena_filter_table(u_seg.shape[1], *fargs)
    return hyena_conv(u_seg, ktab)
```

```python
import functools
import math

import numpy as np
import jax
import jax.numpy as jnp
from jax import lax
from jax.experimental import pallas as pl
from jax.experimental.pallas import tpu as pltpu

F32 = jnp.float32
BF16 = jnp.bfloat16
HI = lax.Precision.HIGHEST

D_MODEL = 1024
GRID_W = 64
BRANCH_W = 256
HEAD_DIM = 64
N_BRANCHES = 4
N_MOD = 6
RWKV_HEADS = 4
RWKV_COLS = 1024
RWKV_GN_EPS = 64e-5
HYENA_COLS = 768
HYENA_EMB = 33
HYENA_FAST_DECAY = 0.3
HYENA_SLOW_DECAY = 1.5
HYENA_TARGET = 1e-2
SCONV_COLS = 768
Q_W = 256
KV_W = 128
ATTN_COLS = 512
ROPE_THETA = 10000.0
RMS_EPS = 1e-6
OFF_HYENA = RWKV_COLS
OFF_SCONV = OFF_HYENA + HYENA_COLS
OFF_ATTN = OFF_SCONV + SCONV_COLS
OFF_GATE = OFF_ATTN + ATTN_COLS
N_EXPERTS = 16
N_GROUPS = 4
EXPERTS_PER_GROUP = 4
D_EXPERT = 512
LN_EPS = 1e-6

SUBLANES = 8
LANES = 128
TT = 256
SCAN_BLK = LANES
HY_BLK = 256
VMEM_LIMIT = 56 * 1024 * 1024


def _cparams(sem, vmem=None):
    return pltpu.CompilerParams(dimension_semantics=sem, vmem_limit_bytes=vmem)


def _ln(xf):
    mu = jnp.mean(xf, -1, keepdims=True)
    xc = xf - mu
    var = jnp.mean(xc * xc, -1, keepdims=True)
    return xc * lax.rsqrt(var + LN_EPS)


def _head_sums(x, ones_bd):
    ones16 = ones_bd.astype(BF16)
    hi = x.astype(BF16)
    lo = (x - hi.astype(F32)).astype(BF16)
    return jnp.dot(hi, ones16, preferred_element_type=F32) + jnp.dot(lo, ones16, preferred_element_type=F32)


def _pick_tile(n, cands):
    for c in cands:
        if n % c == 0:
            return c
    raise ValueError(f"no tile for {n}")


def _ada_kernel(c_ref, w_ref, b_ref, o_ref):
    c = c_ref[...]
    a = c * jax.nn.sigmoid(c)
    o_ref[0] = jnp.dot(a, w_ref[0], precision=HI, preferred_element_type=F32) + b_ref[0]


def ada_mod(cc, ada_w, ada_b):
    depth, d, n = ada_w.shape
    rows = cc.shape[0]
    return pl.pallas_call(
        _ada_kernel,
        out_shape=jax.ShapeDtypeStruct((depth, rows, n), F32),
        grid=(depth, n // d),
        in_specs=[pl.BlockSpec((rows, d), lambda l, j: (0, 0)),
                  pl.BlockSpec((1, d, d), lambda l, j: (l, 0, j)),
                  pl.BlockSpec((1, 1, d), lambda l, j: (l, 0, j))],
        out_specs=pl.BlockSpec((1, rows, d), lambda l, j: (l, 0, j)),
        compiler_params=_cparams(("parallel", "parallel"), VMEM_LIMIT),
        name="ada_mod",
    )(cc, ada_w, ada_b.reshape(depth, 1, n))


def _lnmod_kernel(x_ref, sh_ref, sc_ref, o_ref):
    h = _ln(x_ref[0]) * (1.0 + sc_ref[0]) + sh_ref[0]
    o_ref[0] = h.astype(BF16)


def _mod_spec(col, nct, nb):
    return pl.BlockSpec((1, 1, D_MODEL), lambda b, t: (jnp.where(t < nct, nb, b), 0, col))


def lnmod(x_all, mod3, nct, col_shift, col_scale):
    nb, ta, d = x_all.shape
    return pl.pallas_call(
        _lnmod_kernel,
        out_shape=jax.ShapeDtypeStruct((nb, ta, d), BF16),
        grid=(nb, ta // TT),
        in_specs=[pl.BlockSpec((1, TT, d), lambda b, t: (b, t, 0)),
                  _mod_spec(col_shift, nct, nb), _mod_spec(col_scale, nct, nb)],
        out_specs=pl.BlockSpec((1, TT, d), lambda b, t: (b, t, 0)),
        compiler_params=_cparams(("parallel", "parallel")),
        name="lnmod",
    )(x_all, mod3, mod3)


def _mm_kernel(a_ref, b_ref, o_ref):
    o_ref[...] = jnp.dot(a_ref[...], b_ref[...], preferred_element_type=F32).astype(o_ref.dtype)


def matmul(a, b, out_dtype=F32):
    m, k = a.shape
    _, n = b.shape
    tm = _pick_tile(m, (1024, 512, 256))
    tn = _pick_tile(n, (1024, 512, 256))
    return pl.pallas_call(
        _mm_kernel,
        out_shape=jax.ShapeDtypeStruct((m, n), out_dtype),
        grid=(m // tm, n // tn),
        in_specs=[pl.BlockSpec((tm, k), lambda i, j: (i, 0)),
                  pl.BlockSpec((k, tn), lambda i, j: (0, j))],
        out_specs=pl.BlockSpec((tm, tn), lambda i, j: (i, j)),
        compiler_params=_cparams(("parallel", "parallel"), VMEM_LIMIT),
        name="matmul",
    )(a, b)


def _halo_specs(width, ta):
    nblk8 = ta // SUBLANES
    per = TT // SUBLANES
    cur = pl.BlockSpec((1, TT, width), lambda b, t: (b, t, 0))
    prev = pl.BlockSpec((1, SUBLANES, width), lambda b, t: (b, jnp.maximum(t * per - 1, 0), 0))
    nxt = pl.BlockSpec((1, SUBLANES, width), lambda b, t: (b, jnp.minimum((t + 1) * per, nblk8 - 1), 0))
    return [cur, prev, nxt]


def _neighbours(cur, prev_ref, next_ref, nct, nt):
    t = pl.program_id(1)
    seg_start = jnp.logical_or(t == 0, t == nct)
    seg_end = jnp.logical_or(t == nct - 1, t == nt - 1)
    prev_row = prev_ref[0][SUBLANES - 1:SUBLANES, :] * jnp.where(seg_start, 0.0, 1.0)
    next_row = next_ref[0][0:1, :] * jnp.where(seg_end, 0.0, 1.0)
    row = lax.broadcasted_iota(jnp.int32, (TT, 1), 0)
    xm1 = jnp.where(row == 0, prev_row, pltpu.roll(cur, 1, axis=0))
    xp1 = jnp.where(row == TT - 1, next_row, pltpu.roll(cur, TT - 1, axis=0))
    return xm1, xp1


def _rwkv_prep_kernel(nct, nt, cur_ref, prev_ref, next_ref, mu_ref, w0_ref, wup_ref, a0_ref, aup_ref,
                      gup_ref, kk_ref, ka_ref, rk_ref, bd_ref,
                      r_o, kk_o, w0_o, w1_o, k0_o, k1_o, b0_o, b1_o, v_o, g_o, bon_o):
    cur = cur_ref[0]
    xm1, xp1 = _neighbours(cur, prev_ref, next_ref, nct, nt)
    p = cur + mu_ref[...] * (0.5 * (xm1 + xp1) - cur)
    c = BRANCH_W
    r, k, v = p[:, 0:c], p[:, c:2 * c], p[:, 2 * c:3 * c]
    wd = p[:, 3 * c:3 * c + 64]
    ad = p[:, 3 * c + 64:3 * c + 128]
    gd = p[:, 3 * c + 128:3 * c + 256]
    bd = bd_ref[...]
    kk = k * kk_ref[...]
    ss = _head_sums(kk * kk, bd)
    kkn = kk * lax.rsqrt(jnp.maximum(ss, 1e-24))
    twd = jnp.tanh(wd)
    ka = ka_ref[...]
    kdirs = []
    w_outs, k_outs, b_outs = (w0_o, w1_o), (k0_o, k1_o), (b0_o, b1_o)
    for d in range(2):
        wlog = w0_ref[d:d + 1, :] + jnp.dot(twd, wup_ref[d], precision=HI, preferred_element_type=F32)
        decay = -math.exp(-0.5) * jax.nn.sigmoid(wlog)
        a = jax.nn.sigmoid(a0_ref[d:d + 1, :] + jnp.dot(ad, aup_ref[d], precision=HI, preferred_element_type=F32))
        kdir = k * (1.0 + (a - 1.0) * ka)
        bdir = kkn * a
        kdirs.append(kdir)
        w_outs[d][0] = decay
        k_outs[d][0] = kdir
        b_outs[d][0] = bdir
    r_o[0] = r
    kk_o[0] = kkn
    v_o[0] = v
    g_o[0] = jnp.dot(jax.nn.sigmoid(gd), gup_ref[...], precision=HI, preferred_element_type=F32)
    rkk = r * rk_ref[...] * (kdirs[0] + kdirs[1])
    bon_o[0] = _head_sums(rkk, bd) * v


def rwkv_prep(p_rwkv, nct, mu, w0, w_up, a0, a_up, g_up, k_k, k_a, r_k, bd):
    nb, ta, _ = p_rwkv.shape
    nt = ta // TT
    c = BRANCH_W
    nat = jax.ShapeDtypeStruct((nb, ta, c), F32)
    nat_spec = pl.BlockSpec((1, TT, c), lambda b, t: (b, t, 0))

    def full(a):
        nd = a.ndim
        return pl.BlockSpec(a.shape, lambda b, t: (0,) * nd)

    consts = [mu.reshape(1, -1), w0, w_up, a0, a_up, g_up, k_k.reshape(1, -1), k_a.reshape(1, -1),
              r_k.reshape(1, -1), bd]
    return pl.pallas_call(
        functools.partial(_rwkv_prep_kernel, nct, nt),
        out_shape=[nat] * 11,
        grid=(nb, nt),
        in_specs=_halo_specs(RWKV_COLS, ta) + [full(a) for a in consts],
        out_specs=[nat_spec] * 11,
        compiler_params=_cparams(("parallel", "parallel"), VMEM_LIMIT),
        name="rwkv_prep",
    )(p_rwkv, p_rwkv, p_rwkv, *consts)


CHUNK = 32


def _chunk_scan_rows(x, reverse):
    pos = lax.broadcasted_iota(jnp.int32, x.shape, 0) % CHUNK
    step = 1
    while step < CHUNK:
        if reverse:
            x = x + jnp.where(pos < CHUNK - step, pltpu.roll(x, x.shape[0] - step, axis=0), 0.0)
        else:
            x = x + jnp.where(pos >= step, pltpu.roll(x, step, axis=0), 0.0)
        step *= 2
    return x


def _chunk_prep_kernel(r_ref, kk_ref, v_ref, lw0, k0, b0, lw1, k1, b1,
                       a0_o, bm0_o, rp0_o, y00_o, a1_o, bm1_o, rp1_o, y01_o):
    blk = SCAN_BLK
    npair = BRANCH_W // LANES
    nchunk = blk // CHUNK
    r, kk, v = r_ref[0], kk_ref[0], v_ref[0]
    ti = lax.broadcasted_iota(jnp.int32, (blk, blk), 0)
    si = lax.broadcasted_iota(jnp.int32, (blk, blk), 1)
    same = (ti // CHUNK) == (si // CHUNK)
    eye = ti == si
    bd64 = (ti // HEAD_DIM) == (si // HEAD_DIM)
    head0 = si < HEAD_DIM
    lane_half = lax.broadcasted_iota(jnp.int32, (HEAD_DIM, LANES), 1)
    eyef = jnp.where(eye, 1.0, 0.0)
    dot = lambda x, y: jnp.dot(x, y, preferred_element_type=F32)
    lo = lambda x: x.astype(BF16)
    split = lambda x: jnp.concatenate([jnp.where(head0, x, 0.0), jnp.where(head0, 0.0, x)], axis=0)
    cat = lambda ms: lo(jnp.concatenate(ms, axis=1))

    probs = []
    for d, (lw_ref, k_ref, b_ref) in enumerate(((lw0, k0, b0), (lw1, k1, b1))):
        reverse = d == 1
        lw, k, b = lw_ref[0], k_ref[0], b_ref[0]
        lg = _chunk_scan_rows(lw, reverse)
        lg_end = lg + _chunk_scan_rows(lw, not reverse) - lw
        g, gi, g_end = jnp.exp(lg), jnp.exp(-lg), jnp.exp(lg_end)
        to_end = jnp.exp(lg_end - lg)
        arrs = (kk * jnp.exp(lg - lw), b * gi, k * gi, r * g, v, k * to_end, b * to_end, g_end)
        incl = jnp.logical_and(same, si >= ti if reverse else si <= ti)
        strict = jnp.logical_and(same, si > ti if reverse else si < ti)
        for p in range(npair):
            lanes = slice(p * LANES, (p + 1) * LANES)
            probs.append(dict(d=d, p=p, incl=incl, strict=strict, arrs=tuple(a[:, lanes] for a in arrs)))

    for q in probs:
        pp_, q_, kt_, rt_ = q["arrs"][:4]
        rhs_g = lo(jnp.concatenate([q_, kt_], axis=0))
        lm, mm, n2, nn = [], [], [], []
        for hh in range(2):
            hm = head0 if hh == 0 else jnp.logical_not(head0)
            lhs_g = lo(jnp.concatenate([jnp.where(hm, pp_, 0.0), jnp.where(hm, rt_, 0.0)], axis=0))
            gm = lax.dot_general(lhs_g, rhs_g, (((1,), (1,)), ((), ())), preferred_element_type=F32)
            lm.append(jnp.where(q["strict"], gm[:blk, :blk], 0.0))
            mm.append(jnp.where(q["strict"], gm[:blk, blk:], 0.0))
            n2.append(jnp.where(q["incl"], gm[blk:, :blk], 0.0))
            nn.append(jnp.where(q["incl"], gm[blk:, blk:], 0.0))
        q["pw"], q["tm"] = lm, [eyef - lm[0], eyef - lm[1]]
        q["m_cat"], q["n2_cat"], q["nn_cat"] = cat(mm), cat(n2), cat(nn)
    for _ in range(CHUNK.bit_length() - 2):
        for q in probs:
            pwl = [lo(x) for x in q["pw"]]
            q["pw"] = [dot(x, x) for x in pwl]
        for q in probs:
            q["tm"] = [dot(lo(t), lo(eyef + x)) for t, x in zip(q["tm"], q["pw"])]
    for q in probs:
        q["t_cat"] = cat(q["tm"])
        q["v_st"] = lo(split(q["arrs"][4]))
        q["pp"] = dot(q["t_cat"], lo(split(q["arrs"][0])))
        q["mv"] = dot(q["m_cat"], q["v_st"])
    for q in probs:
        q["w2"] = dot(q["t_cat"], lo(split(q["mv"])))
        q["rp"] = q["arrs"][3] - dot(q["n2_cat"], lo(split(q["pp"])))
    for q in probs:
        q["y0"] = dot(q["nn_cat"], q["v_st"]) - dot(q["n2_cat"], lo(split(q["w2"])))
    in_chunk = [si // CHUNK == c for c in range(nchunk)]
    for q in probs:
        ppt, vt, w2t = q["pp"].T, q["arrs"][4].T, q["w2"].T
        kg_, qg_ = q["arrs"][5], q["arrs"][6]
        lhs_a = jnp.concatenate([jnp.where(cm, ppt, 0.0) for cm in in_chunk], axis=0)
        q["pq"] = dot(lo(lhs_a), lo(qg_))
        lhs_b = jnp.concatenate([jnp.concatenate([jnp.where(cm, vt, 0.0), jnp.where(cm, -w2t, 0.0)], axis=1)
                                 for cm in in_chunk], axis=0)
        q["bf"] = dot(lo(lhs_b), lo(jnp.concatenate([kg_, qg_], axis=0)))
    outs = ((a0_o, bm0_o, rp0_o, y00_o), (a1_o, bm1_o, rp1_o, y01_o))
    for q in probs:
        a_o, bm_o = outs[q["d"]][:2]
        g_end_p = q["arrs"][7]
        for c in range(nchunk):
            pq_c = q["pq"][c * blk:(c + 1) * blk]
            a_o[0, 0, c, q["p"]] = (jnp.where(eye, g_end_p[c * CHUNK:c * CHUNK + 1], 0.0)
                                    - jnp.where(bd64, pq_c, 0.0)).astype(BF16)
            bm_o[0, 0, c, q["p"]] = jnp.where(lane_half < HEAD_DIM, q["bf"][c * blk:c * blk + HEAD_DIM],
                                              q["bf"][c * blk + HEAD_DIM:(c + 1) * blk])
    for d in range(2):
        rp_o, y0_o = outs[d][2:]
        rp_o[0] = jnp.concatenate([q["rp"] for q in probs if q["d"] == d], axis=1)
        y0_o[0] = jnp.concatenate([q["y0"] for q in probs if q["d"] == d], axis=1)


def chunk_prep(r, kk, v, lw0, lw1, k0, k1, b0, b1):
    nb, ta, c = r.shape
    nblk = ta // SCAN_BLK
    npair = c // LANES
    nchunk = SCAN_BLK // CHUNK
    nat = pl.BlockSpec((1, SCAN_BLK, c), lambda b, s: (b, s, 0))
    a_shape = jax.ShapeDtypeStruct((nb, nblk, nchunk, npair, LANES, LANES), BF16)
    bm_shape = jax.ShapeDtypeStruct((nb, nblk, nchunk, npair, HEAD_DIM, LANES), F32)
    nat_shape = jax.ShapeDtypeStruct((nb, ta, c), F32)
    a_spec = pl.BlockSpec((1, 1, nchunk, npair, LANES, LANES), lambda b, s: (b, s, 0, 0, 0, 0))
    bm_spec = pl.BlockSpec((1, 1, nchunk, npair, HEAD_DIM, LANES), lambda b, s: (b, s, 0, 0, 0, 0))
    return pl.pallas_call(
        _chunk_prep_kernel,
        out_shape=[a_shape, bm_shape, nat_shape, nat_shape] * 2,
        grid=(nb, nblk),
        in_specs=[nat] * 9,
        out_specs=[a_spec, bm_spec, nat, nat] * 2,
        compiler_params=_cparams(("parallel", "parallel"), VMEM_LIMIT),
        name="chunk_prep",
    )(r, kk, v, lw0, k0, b0, lw1, k1, b1)


def _chunk_scan_kernel(nb, a0, bm0, rp0, y00, a1, bm1, rp1, y01, yf_o, yb_o, s_scr):
    step = pl.program_id(0)
    npair = BRANCH_W // LANES
    nchunk = SCAN_BLK // CHUNK

    @pl.when(step == 0)
    def _():
        s_scr[...] = jnp.zeros_like(s_scr)

    lane = lax.broadcasted_iota(jnp.int32, (CHUNK, LANES), 1)
    refs = ((a0, bm0, rp0, y00, yf_o), (a1, bm1, rp1, y01, yb_o))
    for ci in range(nchunk):
        for d in range(2):
            a_ref, bm_ref, rp_ref, y0_ref, y_ref = refs[d]
            c = ci if d == 0 else nchunk - 1 - ci
            rows = slice(c * CHUNK, (c + 1) * CHUNK)
            for b in range(nb):
                for p in range(npair):
                    lanes = slice(p * LANES, (p + 1) * LANES)
                    s = s_scr[d, b, p]
                    rpc = rp_ref[b, rows, lanes]
                    lhs = jnp.concatenate([jnp.where(lane < HEAD_DIM, rpc, 0.0), jnp.where(lane >= HEAD_DIM, rpc, 0.0)],
                                          axis=0)
                    yh = lax.dot_general(lhs, s, (((1,), (1,)), ((), ())), preferred_element_type=F32)
                    y_ref[b, rows, lanes] = jnp.concatenate([yh[:CHUNK], yh[CHUNK:]], axis=1) + y0_ref[b, rows, lanes]
                    s_scr[d, b, p] = (jnp.dot(s.astype(BF16), a_ref[b, 0, c, p], preferred_element_type=F32)
                                      + bm_ref[b, 0, c, p])


def rwkv_scan(r, kk, v, lw0, lw1, k0, k1, b0, b1, lc):
    nb, ta, c = r.shape
    nblk = ta // SCAN_BLK
    nctb = lc // SCAN_BLK
    npair = c // LANES
    nchunk = SCAN_BLK // CHUNK
    a0, bm0, rp0, y00, a1, bm1, rp1, y01 = chunk_prep(r, kk, v, lw0, lw1, k0, k1, b0, b1)

    def fwd(s):
        return s

    def bwd(s):
        return jnp.where(s < nctb, nctb - 1 - s, nblk - 1 - (s - nctb))

    def specs(idx):
        return [pl.BlockSpec((nb, 1, nchunk, npair, LANES, LANES), lambda s: (0, idx(s), 0, 0, 0, 0)),
                pl.BlockSpec((nb, 1, nchunk, npair, HEAD_DIM, LANES), lambda s: (0, idx(s), 0, 0, 0, 0)),
                pl.BlockSpec((nb, SCAN_BLK, c), lambda s: (0, idx(s), 0)),
                pl.BlockSpec((nb, SCAN_BLK, c), lambda s: (0, idx(s), 0))]

    out = jax.ShapeDtypeStruct((nb, ta, c), F32)
    return pl.pallas_call(
        functools.partial(_chunk_scan_kernel, nb),
        out_shape=[out, out],
        grid=(nblk,),
        in_specs=specs(fwd) + specs(bwd),
        out_specs=[pl.BlockSpec((nb, SCAN_BLK, c), lambda s: (0, fwd(s), 0)),
                   pl.BlockSpec((nb, SCAN_BLK, c), lambda s: (0, bwd(s), 0))],
        scratch_shapes=[pltpu.VMEM((2, nb, npair, HEAD_DIM, LANES), F32)],
        compiler_params=_cparams(("arbitrary",), VMEM_LIMIT),
        name="chunk_scan",
    )(a0, bm0, rp0, y00, a1, bm1, rp1, y01)


def _rwkv_out_kernel(yf_ref, yb_ref, g_ref, bon_ref, lg_ref, lb_ref, bd_ref, o_ref):
    y = yf_ref[0] + yb_ref[0]
    bd = bd_ref[...]
    mu = _head_sums(y, bd) * (1.0 / HEAD_DIM)
    yc = y - mu
    var = _head_sums(yc * yc, bd) * (1.0 / HEAD_DIM)
    yn = yc * lax.rsqrt(var + RWKV_GN_EPS) * lg_ref[...] + lb_ref[...]
    o_ref[0] = ((yn + bon_ref[0]) * g_ref[0]).astype(BF16)


def rwkv_out(yf, yb, g, bon, lnx_g, lnx_b, bd):
    nb, ta, _ = yf.shape
    nat_spec = pl.BlockSpec((1, TT, BRANCH_W), lambda b, t: (b, t, 0))
    row = pl.BlockSpec((1, BRANCH_W), lambda b, t: (0, 0))
    return pl.pallas_call(
        _rwkv_out_kernel,
        out_shape=jax.ShapeDtypeStruct((nb, ta, BRANCH_W), BF16),
        grid=(nb, ta // TT),
        in_specs=[nat_spec] * 4 + [row, row, pl.BlockSpec(bd.shape, lambda b, t: (0, 0))],
        out_specs=nat_spec,
        compiler_params=_cparams(("parallel", "parallel")),
        name="rwkv_out",
    )(yf, yb, g, bon, lnx_g.reshape(1, -1), lnx_b.reshape(1, -1), bd)


def _pair_swap(x):
    lane = lax.broadcasted_iota(jnp.int32, x.shape, 1)
    n = x.shape[1]
    return jnp.where(lane % 2 == 0, pltpu.roll(x, n - 1, axis=1), pltpu.roll(x, 1, axis=1))


def _attn_prep_kernel(p_ref, cos_ref, sin_ref, qg_ref, kg_ref, bd_ref, q_o, k_o, v_o):
    p = p_ref[0]
    q, k, v = p[:, :Q_W], p[:, Q_W:Q_W + KV_W], p[:, Q_W + KV_W:]
    bd = bd_ref[...]
    cos, sin = cos_ref[...], sin_ref[...]
    qms = _head_sums(q * q, bd) * (1.0 / HEAD_DIM)
    qn = q * lax.rsqrt(qms + RMS_EPS) * qg_ref[...]
    qr = qn * cos + _pair_swap(qn) * sin
    q_o[0] = (qr * HEAD_DIM ** -0.5).astype(BF16)
    kms = _head_sums(k * k, bd[:KV_W, :KV_W]) * (1.0 / HEAD_DIM)
    kn = k * lax.rsqrt(kms + RMS_EPS) * kg_ref[...]
    kr = kn * cos[:, :KV_W] + _pair_swap(kn) * sin[:, :KV_W]
    for g in range(KV_W // HEAD_DIM):
        sl = slice(g * HEAD_DIM, (g + 1) * HEAD_DIM)
        k_o[0, g] = kr[:, sl].astype(BF16)
        v_o[0, g] = v[:, sl].astype(BF16)


def attn_prep(p_attn, cos_t, sin_t, q_norm, k_norm, bd):
    nb, ta, _ = p_attn.shape
    ng = KV_W // HEAD_DIM
    qg = jnp.tile(q_norm, Q_W // HEAD_DIM).reshape(1, -1)
    kg = jnp.tile(k_norm, ng).reshape(1, -1)
    kv_shape = jax.ShapeDtypeStruct((nb, ng, ta, HEAD_DIM), BF16)
    kv_spec = pl.BlockSpec((1, ng, TT, HEAD_DIM), lambda b, t: (b, 0, t, 0))
    return pl.pallas_call(
        _attn_prep_kernel,
        out_shape=[jax.ShapeDtypeStruct((nb, ta, Q_W), BF16), kv_shape, kv_shape],
        grid=(nb, ta // TT),
        in_specs=[pl.BlockSpec((1, TT, ATTN_COLS), lambda b, t: (b, t, 0)),
                  pl.BlockSpec((TT, Q_W), lambda b, t: (t, 0)),
                  pl.BlockSpec((TT, Q_W), lambda b, t: (t, 0)),
                  pl.BlockSpec((1, Q_W), lambda b, t: (0, 0)),
                  pl.BlockSpec((1, KV_W), lambda b, t: (0, 0)),
                  pl.BlockSpec(bd.shape, lambda b, t: (0, 0))],
        out_specs=[pl.BlockSpec((1, TT, Q_W), lambda b, t: (b, t, 0)), kv_spec, kv_spec],
        compiler_params=_cparams(("parallel", "parallel")),
        name="attn_prep",
    )(p_attn, cos_t, sin_t, qg, kg, bd)


def _attn_kernel(nct, lc, q_ref, k_ref, v_ref, o_ref):
    t = pl.program_id(1)
    ng = k_ref.shape[1]
    rep = Q_W // HEAD_DIM // ng

    def run(nk):
        outs = []
        for g in range(ng):
            kk, vv = k_ref[0, g, :nk, :], v_ref[0, g, :nk, :]
            for r in range(rep):
                h = g * rep + r
                q = q_ref[0][:, h * HEAD_DIM:(h + 1) * HEAD_DIM]
                s = lax.dot_general(q, kk, (((1,), (1,)), ((), ())), preferred_element_type=F32)
                p = jnp.exp(s - jnp.max(s, -1, keepdims=True))
                l = jnp.sum(p, -1, keepdims=True)
                o = jnp.dot(p.astype(BF16), vv, preferred_element_type=F32)
                outs.append(o / l)
        o_ref[0] = jnp.concatenate(outs, axis=1).astype(BF16)

    @pl.when(t < nct)
    def _():
        run(lc)

    @pl.when(t >= nct)
    def _():
        run(k_ref.shape[2])


def attention(q, k, v, lc):
    nb, ta, _ = q.shape
    ng = k.shape[1]
    nct = lc // TT
    qo_spec = pl.BlockSpec((1, TT, Q_W), lambda b, t: (b, t, 0))
    kv_spec = pl.BlockSpec((1, ng, ta, HEAD_DIM), lambda b, t: (b, 0, 0, 0))
    return pl.pallas_call(
        functools.partial(_attn_kernel, nct, lc),
        out_shape=jax.ShapeDtypeStruct((nb, ta, Q_W), BF16),
        grid=(nb, ta // TT),
        in_specs=[qo_spec, kv_spec, kv_spec],
        out_specs=qo_spec,
        compiler_params=_cparams(("parallel", "arbitrary"), VMEM_LIMIT),
        name="attention",
    )(q, k, v)


def _hs_pre_kernel(nct, nt, cur_ref, prev_ref, next_ref, hw_ref, sw_ref, x0_o, u_o, ycv_o):
    cur = cur_ref[0]
    xm1, xp1 = _neighbours(cur, prev_ref, next_ref, nct, nt)
    c = BRANCH_W
    hc = HYENA_COLS
    hw = hw_ref[...]
    ph = hw[0:1] * xm1[:, :hc] + hw[1:2] * cur[:, :hc] + hw[2:3] * xp1[:, :hc]
    x0_o[0] = ph[:, :c]
    u_o[0] = ph[:, c:2 * c] * ph[:, 2 * c:3 * c]
    sw = sw_ref[...]

    def cx(a):
        return a[:, hc + c:hc + 2 * c] * a[:, hc + 2 * c:hc + 3 * c]

    conv = sw[0:1] * cx(xm1) + sw[1:2] * cx(cur) + sw[2:3] * cx(xp1)
    ycv_o[0] = (cur[:, hc:hc + c] * conv).astype(BF16)


def hs_pre(p_hs, nct, hyena_conv, sconv_w):
    nb, ta, w = p_hs.shape
    nt = ta // TT
    nat = pl.BlockSpec((1, TT, BRANCH_W), lambda b, t: (b, t, 0))
    return pl.pallas_call(
        functools.partial(_hs_pre_kernel, nct, nt),
        out_shape=[jax.ShapeDtypeStruct((nb, ta, BRANCH_W), F32),
                   jax.ShapeDtypeStruct((nb, ta, BRANCH_W), F32),
                   jax.ShapeDtypeStruct((nb, ta, BRANCH_W), BF16)],
        grid=(nb, nt),
        in_specs=_halo_specs(w, ta) + [pl.BlockSpec(hyena_conv.shape, lambda b, t: (0, 0)),
                                       pl.BlockSpec(sconv_w.shape, lambda b, t: (0, 0))],
        out_specs=[nat, nat, nat],
        compiler_params=_cparams(("parallel", "parallel"), VMEM_LIMIT),
        name="hs_pre",
    )(p_hs, p_hs, p_hs, hyena_conv, sconv_w)


EMB_PAD = 40


def _filter_tables(lh):
    n = np.arange(2 * lh)
    pos = np.abs(n - (lh - 1)).astype(np.float64)
    bands = (HYENA_EMB - 1) // 2
    t = np.minimum(pos, lh - 1) / (lh - 1)
    wpos = 2.0 * math.pi * pos / lh
    f = np.linspace(1e-4, bands - 1, bands)[:, None]
    z = np.zeros((EMB_PAD, 2 * lh), np.float32)
    z[0] = t
    z[1:1 + bands] = np.cos(f * wpos[None, :])
    z[1 + bands:1 + 2 * bands] = -np.sin(f * wpos[None, :])
    max_decay = math.log(HYENA_TARGET) / HYENA_FAST_DECAY
    min_decay = math.log(HYENA_TARGET) / HYENA_SLOW_DECAY
    deltas = np.abs(np.linspace(min_decay, max_decay, BRANCH_W)).astype(np.float32)
    return z, deltas.reshape(-1, 1)


def _filter_kernel(lh, tn, z_ref, w1_ref, b1_ref, f1_ref, w2_ref, b2_ref, f2_ref, w3_ref, dl_ref, o_ref):
    z = z_ref[...]
    h1 = jnp.sin(f1_ref[...] * (jnp.dot(w1_ref[...], z, precision=HI, preferred_element_type=F32) + b1_ref[...]))
    h2 = jnp.sin(f2_ref[...] * (jnp.dot(w2_ref[...], h1, precision=HI, preferred_element_type=F32) + b2_ref[...]))
    f = jnp.dot(w3_ref[...], h2, precision=HI, preferred_element_type=F32)
    n = pl.program_id(0) * tn + lax.broadcasted_iota(jnp.int32, (1, tn), 1)
    filt = jnp.where(n >= lh - 1, f[:BRANCH_W], f[BRANCH_W:])
    win = jnp.exp(-z[0:1, :] * dl_ref[...])
    o_ref[...] = jnp.where(n == 2 * lh - 1, 0.0, filt * win)


def hyena_filter_table(lh, w1, b1, f1, w2, b2, f2, w3):
    z_np, dl_np = _filter_tables(lh)
    n2 = 2 * lh
    tn = _pick_tile(n2, (1024, 512))
    hd = w2.shape[0]
    w1t = jnp.zeros((hd, EMB_PAD), F32).at[:, :HYENA_EMB].set(w1.T)
    args = [jnp.asarray(z_np), w1t, b1.reshape(-1, 1), f1.reshape(-1, 1), w2.T, b2.reshape(-1, 1),
            f2.reshape(-1, 1), w3.T, jnp.asarray(dl_np)]

    def full(a):
        return pl.BlockSpec(a.shape, lambda j: (0, 0))

    return pl.pallas_call(
        functools.partial(_filter_kernel, lh, tn),
        out_shape=jax.ShapeDtypeStruct((BRANCH_W, n2), F32),
        grid=(n2 // tn,),
        in_specs=[pl.BlockSpec((EMB_PAD, tn), lambda j: (0, j))] + [full(a) for a in args[1:]],
        out_specs=pl.BlockSpec((BRANCH_W, tn), lambda j: (0, j)),
        compiler_params=_cparams(("parallel",)),
        name="hyena_filter",
    )(*args)


def _hyena_conv_kernel(nblk, bp, nch, k_ref, u_ref, o_ref, t_scr):
    ntile = 4 * nblk - 1
    mc = 2 * nblk - 1
    width = (ntile + 1) * LANES
    for ch in range(nch):
        big = pltpu.roll(jnp.broadcast_to(k_ref[ch], (LANES, width)), width - (LANES - 1), 1, stride=1, stride_axis=0)
        for m in range(ntile):
            t_scr[ch, m] = big[:, m * LANES:(m + 1) * LANES].astype(BF16)

    for ch in range(nch):
        for d in [0] + [s * a for a in range(1, nblk) for s in (1, -1)]:
            m0 = 2 * d + mc
            w = jnp.concatenate([jnp.concatenate([t_scr[ch, m0], t_scr[ch, m0 + 1]], axis=1),
                                 jnp.concatenate([t_scr[ch, m0 - 1], t_scr[ch, m0]], axis=1)], axis=0)
            i0, i1 = max(0, d), min(nblk, nblk + d)
            lhs = u_ref[ch, (i0 - d) * bp:(i1 - d) * bp, :].astype(BF16)
            res = jnp.dot(lhs, w, preferred_element_type=F32)
            if d == 0:
                o_ref[ch] = res
            else:
                o_ref[ch, i0 * bp:i1 * bp, :] += res


def hyena_conv(u, ktab):
    nb, l, c = u.shape
    nblk = l // HY_BLK
    bp = -(-nb // SUBLANES) * SUBLANES
    nch = SUBLANES if nblk == 1 else 1
    ut = jnp.transpose(u.reshape(nb, nblk, HY_BLK, c), (3, 1, 0, 2))
    if bp != nb:
        ut = jnp.pad(ut, ((0, 0), (0, 0), (0, bp - nb), (0, 0)))
    ut = ut.reshape(c, nblk * bp, HY_BLK)
    k3 = ktab.reshape(c, 1, 4 * nblk * LANES)
    out = pl.pallas_call(
        functools.partial(_hyena_conv_kernel, nblk, bp, nch),
        out_shape=jax.ShapeDtypeStruct((c, nblk * bp, HY_BLK), F32),
        grid=(c // nch,),
        in_specs=[pl.BlockSpec((nch, 1, 4 * nblk * LANES), lambda ch: (ch, 0, 0)),
                  pl.BlockSpec((nch, nblk * bp, HY_BLK), lambda ch: (ch, 0, 0))],
        out_specs=pl.BlockSpec((nch, nblk * bp, HY_BLK), lambda ch: (ch, 0, 0)),
        scratch_shapes=[pltpu.VMEM((nch, 4 * nblk - 1, LANES, LANES), BF16)],
        compiler_params=_cparams(("parallel",)),
        name="hyena_conv",
    )(k3, ut)
    out = out.reshape(c, nblk, bp, HY_BLK)[:, :, :nb]
    return jnp.transpose(out, (2, 1, 3, 0)).reshape(nb, l, c)


def _route(logits, bias):
    s = jax.nn.sigmoid(logits)
    sel = s + bias
    srow = [s[e:e + 1] for e in range(N_EXPERTS)]
    row = [sel[e:e + 1] for e in range(N_EXPERTS)]
    best, gi = None, None
    for g in range(N_GROUPS):
        a, b, c, d = row[4 * g:4 * g + 4]
        hi1, lo1, hi2, lo2 = jnp.maximum(a, b), jnp.minimum(a, b), jnp.maximum(c, d), jnp.minimum(c, d)
        score = jnp.maximum(hi1, hi2) + jnp.maximum(jnp.minimum(hi1, hi2), jnp.maximum(lo1, lo2))
        if g == 0:
            best, gi = score, jnp.zeros(score.shape, jnp.int32)
        else:
            better = score > best
            gi = jnp.where(better, g, gi)
            best = jnp.where(better, score, best)
    neg = -jnp.inf
    msel = [jnp.where(gi == e // EXPERTS_PER_GROUP, row[e], neg) for e in range(N_EXPERTS)]

    def arg_first_max(vals):
        bv, bi = vals[0], jnp.zeros(vals[0].shape, jnp.int32)
        for e in range(1, N_EXPERTS):
            better = vals[e] > bv
            bi = jnp.where(better, e, bi)
            bv = jnp.where(better, vals[e], bv)
        return bi

    i1 = arg_first_max(msel)
    i2 = arg_first_max([jnp.where(i1 == e, neg, msel[e]) for e in range(N_EXPERTS)])
    w1 = sum(jnp.where(i1 == e, srow[e], 0.0) for e in range(N_EXPERTS))
    w2 = sum(jnp.where(i2 == e, srow[e], 0.0) for e in range(N_EXPERTS))
    den = w1 + w2
    g1, g2 = w1 / den, w2 / den
    rows = [jnp.where(i1 == e, g1, 0.0) + jnp.where(i2 == e, g2, 0.0) for e in range(N_EXPERTS)]
    rows.append(gi.astype(F32))
    rows.extend([jnp.zeros_like(g1)] * (ROUTE_ROWS - len(rows)))
    return jnp.concatenate(rows, axis=0)


def _merge_kernel(ya_ref, x0_ref, u_ref, yc_ref, ycv_ref, yd_ref, h_ref, x_ref, ga_ref, shf_ref, scf_ref,
                  skip_ref, wg_ref, wb_ref, wo_ref, g1_ref, b1_ref, rwt_ref, rb_ref, x1_o, hf_o, gates_o):
    yb = (x0_ref[0] * (yc_ref[0] + u_ref[0] * skip_ref[...])).astype(BF16)
    ys = (ya_ref[0], yb, ycv_ref[0], yd_ref[0])
    merged = None
    for n in range(N_BRANCHES):
        gate = jax.nn.sigmoid(jnp.dot(h_ref[0], wg_ref[:, n * D_MODEL:(n + 1) * D_MODEL], preferred_element_type=F32))
        term = gate * jnp.dot(ys[n], wb_ref[n], preferred_element_type=F32)
        merged = term if merged is None else merged + term
    out = jnp.dot(merged.astype(BF16), wo_ref[...], preferred_element_type=F32)
    alpha = (2 * 2) ** 0.25
    x1 = _ln(alpha * x_ref[0] + ga_ref[0] * out) * g1_ref[...] + b1_ref[...]
    hf = _ln(x1) * (1.0 + scf_ref[0]) + shf_ref[0]
    x1_o[0] = x1
    hf_o[0] = hf.astype(BF16)
    logits = lax.dot_general(rwt_ref[...], hf, (((1,), (1,)), ((), ())), precision=HI,
                             preferred_element_type=F32)
    gates_o[...] = _route(logits, rb_ref[...])


def merge(ya, x0, u, yconv, ycv, yd, h, x_all, mod3, nct, skip, wg, wb, wo, ln_g, ln_b, rwt, rbias):
    nb, ta, d = x_all.shape
    nt = ta // TT
    nat = pl.BlockSpec((1, TT, BRANCH_W), lambda b, t: (b, t, 0))
    wide = pl.BlockSpec((1, TT, d), lambda b, t: (b, t, 0))

    def full(a):
        nd = a.ndim
        return pl.BlockSpec(a.shape, lambda b, t: (0,) * nd)

    consts = [skip.reshape(1, -1), wg, wb, wo, ln_g.reshape(1, -1), ln_b.reshape(1, -1), rwt, rbias.reshape(-1, 1)]
    return pl.pallas_call(
        _merge_kernel,
        out_shape=[jax.ShapeDtypeStruct((nb, ta, d), F32), jax.ShapeDtypeStruct((nb, ta, d), BF16),
                   jax.ShapeDtypeStruct((ROUTE_ROWS, nb * ta), F32)],
        grid=(nb, nt),
        in_specs=[nat] * 6 + [wide, wide,
                              _mod_spec(2, nct, nb), _mod_spec(3, nct, nb), _mod_spec(4, nct, nb)]
                 + [full(a) for a in consts],
        out_specs=[wide, wide, pl.BlockSpec((ROUTE_ROWS, TT), lambda b, t: (0, b * nt + t))],
        compiler_params=_cparams(("parallel", "parallel"), VMEM_LIMIT),
        name="merge",
    )(ya, x0, u, yconv, ycv, yd, h, x_all, mod3, mod3, mod3, *consts)


MOE_TILE = 1024
MOE_ALIGN = 2 * SUBLANES
MOE_CHUNK = 18 * MOE_ALIGN
MOE_SORTED = MOE_TILE + LANES
MOE_ROWS = MOE_SORTED + 3 * LANES
GID_ROW = N_EXPERTS
ROUTE_ROWS = 3 * SUBLANES
META_LANES = 2 * N_GROUPS
assert N_GROUPS * (MOE_ALIGN - 1) <= MOE_SORTED - MOE_TILE and MOE_SORTED + MOE_CHUNK <= MOE_ROWS


def _moe_sort_kernel(gt_ref, g_ref, h_ref, up_ref, hs_o, gs_o, pt_o, meta_o):
    gid = gt_ref[GID_ROW:GID_ROW + 1, :]
    onehot = [jnp.where(gid == float(g), 1.0, 0.0) for g in range(N_GROUPS)]
    g4 = jnp.concatenate(onehot + [jnp.zeros((SUBLANES - N_GROUPS, MOE_TILE), F32)], axis=0)
    before = jnp.dot(g4.astype(BF16), up_ref[...], preferred_element_type=F32)
    lane = lax.broadcasted_iota(jnp.int32, (SUBLANES, LANES), 1)
    meta = jnp.zeros((SUBLANES, LANES), F32)
    off = jnp.zeros((1, 1), F32)
    pos = jnp.zeros((1, MOE_TILE), F32)
    for g in range(N_GROUPS):
        cnt = jnp.sum(onehot[g], axis=1, keepdims=True)
        pos = pos + onehot[g] * (before[g:g + 1] + off)
        meta = jnp.where(lane == g, off, meta)
        meta = jnp.where(lane == N_GROUPS + g, cnt, meta)
        off = off + jnp.ceil(cnt * (1.0 / MOE_ALIGN)) * MOE_ALIGN
    meta_o[0] = meta.astype(jnp.int32)
    row = lax.broadcasted_iota(jnp.int32, (MOE_ROWS, MOE_TILE), 0)
    place = jnp.where(row == pos.astype(jnp.int32), 1.0, 0.0)
    p16 = place.astype(BF16)
    hs_o[0] = jnp.dot(p16, h_ref[...], preferred_element_type=F32).astype(BF16)
    gts = g_ref[...]
    hi = gts.astype(BF16)
    r1 = gts - hi.astype(F32)
    mid = r1.astype(BF16)
    low = (r1 - mid.astype(F32)).astype(BF16)
    gs_o[0] = (jnp.dot(p16, hi, preferred_element_type=F32) + jnp.dot(p16, mid, preferred_element_type=F32)
               + jnp.dot(p16, low, preferred_element_type=F32))
    pt_o[...] = place.T.astype(BF16)


def moe_sort(hf, gates_t, gates):
    n, d = hf.shape
    ntile = n // MOE_TILE
    upper = jnp.asarray(np.triu(np.ones((MOE_TILE, MOE_TILE), np.float32), 1), BF16)
    return pl.pallas_call(
        _moe_sort_kernel,
        out_shape=[jax.ShapeDtypeStruct((ntile, MOE_ROWS, d), BF16),
                   jax.ShapeDtypeStruct((ntile, MOE_ROWS, LANES), F32),
                   jax.ShapeDtypeStruct((n, MOE_ROWS), BF16),
                   jax.ShapeDtypeStruct((ntile, SUBLANES, LANES), jnp.int32)],
        grid=(ntile,),
        in_specs=[pl.BlockSpec((ROUTE_ROWS, MOE_TILE), lambda i: (0, i)),
                  pl.BlockSpec((MOE_TILE, LANES), lambda i: (i, 0)),
                  pl.BlockSpec((MOE_TILE, d), lambda i: (i, 0)),
                  pl.BlockSpec((MOE_TILE, MOE_TILE), lambda i: (0, 0))],
        out_specs=[pl.BlockSpec((1, MOE_ROWS, d), lambda i: (i, 0, 0)),
                   pl.BlockSpec((1, MOE_ROWS, LANES), lambda i: (i, 0, 0)),
                   pl.BlockSpec((MOE_TILE, MOE_ROWS), lambda i: (i, 0)),
                   pl.BlockSpec((1, SUBLANES, LANES), lambda i: (i, 0, 0))],
        compiler_params=_cparams(("parallel",), VMEM_LIMIT),
        name="moe_sort",
    )(gates_t, gates, hf, upper)


def _moe_group_kernel(meta_ref, hs_ref, gs_ref, w1_ref, w3_ref, w2_ref, prev_ref, ys_o):
    g, i = pl.program_id(0), pl.program_id(1)

    @pl.when(g == 0)
    def _():
        ys_o[...] = jnp.zeros_like(ys_o)

    @pl.when(g > 0)
    def _():
        ys_o[...] = prev_ref[...]

    off = meta_ref[i * META_LANES + g]
    cnt = meta_ref[i * META_LANES + N_GROUPS + g]
    lane = lax.broadcasted_iota(jnp.int32, (MOE_CHUNK, LANES), 1)

    def chunk(j, carry):
        rows = pl.ds(pl.multiple_of(off + j * MOE_CHUNK, MOE_ALIGN), MOE_CHUNK)
        hs = hs_ref[0, rows, :]
        gs = gs_ref[0, rows, :]
        acc = ys_o[0, rows, :].astype(F32)
        for e in range(EXPERTS_PER_GROUP):
            a = jnp.dot(hs, w1_ref[e], preferred_element_type=F32)
            b = jnp.dot(hs, w3_ref[e], preferred_element_type=F32)
            act = (a * jax.nn.sigmoid(a)) * b
            gcol = jnp.sum(jnp.where(lane == g * EXPERTS_PER_GROUP + e, gs, 0.0), axis=1, keepdims=True)
            acc = acc + gcol * jnp.dot(act.astype(BF16), w2_ref[e], preferred_element_type=F32)
        ys_o[0, rows, :] = acc.astype(BF16)
        return carry

    lax.fori_loop(0, lax.div(cnt + (MOE_CHUNK - 1), MOE_CHUNK), chunk, 0)


def moe_group(hs, gs, meta, w1, w3, w2):
    ntile, _, d = hs.shape
    de = w1.shape[2]
    epg = EXPERTS_PER_GROUP
    tile_spec = pl.BlockSpec((1, MOE_ROWS, d), lambda g, i, m: (i, 0, 0))
    grid_spec = pltpu.PrefetchScalarGridSpec(
        num_scalar_prefetch=1,
        grid=(N_GROUPS, ntile),
        in_specs=[tile_spec,
                  pl.BlockSpec((1, MOE_ROWS, LANES), lambda g, i, m: (i, 0, 0)),
                  pl.BlockSpec((epg, d, de), lambda g, i, m: (g, 0, 0)),
                  pl.BlockSpec((epg, d, de), lambda g, i, m: (g, 0, 0)),
                  pl.BlockSpec((epg, de, d), lambda g, i, m: (g, 0, 0)),
                  tile_spec],
        out_specs=tile_spec,
    )
    carried = jnp.zeros((ntile, MOE_ROWS, d), BF16)
    return pl.pallas_call(
        _moe_group_kernel,
        out_shape=jax.ShapeDtypeStruct((ntile, MOE_ROWS, d), BF16),
        grid_spec=grid_spec,
        input_output_aliases={6: 0},
        compiler_params=_cparams(("arbitrary", "arbitrary"), VMEM_LIMIT),
        name="moe_group",
    )(meta, hs, gs, w1, w3, w2, carried)


def _ln2_kernel(emit_next, x_ref, pt_ref, ys_ref, gf_ref, g_ref, b_ref, *rest):
    alpha = (2 * 2) ** 0.25
    f = jnp.dot(pt_ref[...], ys_ref[0], preferred_element_type=F32)
    x2 = _ln(alpha * x_ref[0] + gf_ref[0] * f) * g_ref[...] + b_ref[...]
    if emit_next:
        sh_ref, sc_ref, o_ref, h_o = rest
        h_o[0] = (_ln(x2) * (1.0 + sc_ref[0]) + sh_ref[0]).astype(BF16)
    else:
        o_ref, = rest
    o_ref[0] = x2


def ln2(x1, pt, ysb, mod3, nct, ln_g, ln_b, mod3_next):
    nb, ta, d = x1.shape
    nt = ta // TT
    per = MOE_TILE // TT
    wide = pl.BlockSpec((1, TT, d), lambda b, t: (b, t, 0))
    row = pl.BlockSpec((1, d), lambda b, t: (0, 0))
    in_specs = [wide, pl.BlockSpec((TT, MOE_SORTED), lambda b, t: (b * nt + t, 0)),
                pl.BlockSpec((1, MOE_SORTED, d), lambda b, t: ((b * nt + t) // per, 0, 0)),
                _mod_spec(5, nct, nb), row, row]
    args = [x1, pt, ysb, mod3, ln_g.reshape(1, -1), ln_b.reshape(1, -1)]
    if mod3_next is None:
        out_shape = jax.ShapeDtypeStruct((nb, ta - nct * TT, d), F32)
        out_specs = pl.BlockSpec((1, TT, d), lambda b, t: (b, jnp.maximum(t - nct, 0), 0))
    else:
        in_specs += [_mod_spec(0, nct, nb), _mod_spec(1, nct, nb)]
        args += [mod3_next, mod3_next]
        out_shape = [jax.ShapeDtypeStruct((nb, ta, d), F32), jax.ShapeDtypeStruct((nb, ta, d), BF16)]
        out_specs = [wide, wide]
    return pl.pallas_call(
        functools.partial(_ln2_kernel, mod3_next is not None),
        out_shape=out_shape,
        grid=(nb, nt),
        in_specs=in_specs,
        out_specs=out_specs,
        compiler_params=_cparams(("parallel", "arbitrary"), VMEM_LIMIT),
        name="ln2",
    )(*args)


def _rope_tables(l, lc):
    half = HEAD_DIM // 2
    inv = ROPE_THETA ** (-np.arange(0, half, 2, dtype=np.float64) / half)
    t = np.arange(l)
    rows, cols = t // GRID_W, t % GRID_W
    ang = np.concatenate([rows[:, None] * inv, cols[:, None] * inv], -1)
    ang = np.concatenate([np.zeros((lc, half)), ang], 0)
    cos = np.repeat(np.cos(ang), 2, axis=1)
    sin = np.repeat(np.sin(ang), 2, axis=1)
    sin[:, 0::2] *= -1.0
    reps = Q_W // HEAD_DIM
    return (jnp.asarray(np.tile(cos, (1, reps)), F32), jnp.asarray(np.tile(sin, (1, reps)), F32))


def _block_diag_ones():
    i = np.arange(BRANCH_W) // HEAD_DIM
    return jnp.asarray((i[:, None] == i[None, :]).astype(np.float32))


def kernel(x, c, ctx, c_ctx, ada_w, ada_b, w_in, rwkv_mu, rwkv_w0, rwkv_w_up, rwkv_a0, rwkv_a_up, rwkv_g_up, rwkv_k_k, rwkv_k_a, rwkv_r_k, rwkv_lnx_g, rwkv_lnx_b, hyena_conv, hyena_w1, hyena_b1, hyena_freq1, hyena_w2, hyena_b2, hyena_freq2, hyena_w3, hyena_skip, sconv_w, attn_q_norm, attn_k_norm, w_branch, w_out, ln1_g, ln1_b, ln2_g, ln2_b, router_w, router_bias, exp_w1, exp_w3, exp_w2):
    nb, l, d = x.shape
    lc = ctx.shape[1]
    depth = ada_w.shape[0]
    assert d == D_MODEL and lc % TT == 0 and l % TT == 0 and l % GRID_W == 0 and (nb * (lc + l)) % MOE_TILE == 0
    ta = lc + l
    nct = lc // TT
    x_all = jnp.concatenate([ctx, x], axis=1)

    mod_rows = -(-(nb + 1) // SUBLANES) * SUBLANES
    cc = jnp.zeros((mod_rows, d), F32).at[:nb].set(c).at[nb].set(c_ctx)
    mod = ada_mod(cc, ada_w, ada_b)

    cos_t, sin_t = _rope_tables(l, lc)
    bd = _block_diag_ones()
    rwt = router_w.T

    for li in range(depth):
        mod3 = mod[li].reshape(mod_rows, 1, N_MOD * d)
        wl = w_in[li].astype(BF16)
        if li == 0:
            h3 = lnmod(x_all, mod3, nct, 0, 1)
        h = h3.reshape(nb * ta, d)
        p_rwkv = matmul(h, wl[:, :OFF_HYENA]).reshape(nb, ta, -1)
        p_hs = matmul(h, wl[:, OFF_HYENA:OFF_ATTN]).reshape(nb, ta, -1)
        p_attn = matmul(h, wl[:, OFF_ATTN:OFF_GATE]).reshape(nb, ta, -1)

        r, kk, w0, w1, k0, k1, b0, b1, v, g, bon = rwkv_prep(
            p_rwkv, nct, rwkv_mu[li], rwkv_w0[li], rwkv_w_up[li], rwkv_a0[li], rwkv_a_up[li], rwkv_g_up[li],
            rwkv_k_k[li], rwkv_k_a[li], rwkv_r_k[li], bd)
        yf, yb = rwkv_scan(r, kk, v, w0, w1, k0, k1, b0, b1, lc)
        ya = rwkv_out(yf, yb, g, bon, rwkv_lnx_g[li], rwkv_lnx_b[li], bd)

        q, kx, vx = attn_prep(p_attn, cos_t, sin_t, attn_q_norm[li], attn_k_norm[li], bd)
        yd = attention(q, kx, vx, lc)

        x0, u, ycv = hs_pre(p_hs, nct, hyena_conv[li], sconv_w[li])
        fargs = (hyena_w1[li], hyena_b1[li], hyena_freq1[li], hyena_w2[li], hyena_b2[li], hyena_freq2[li],
                 hyena_w3[li])
        yconv_ctx = hyena_conv_seg(u[:, :lc], fargs) if li < depth - 1 else jnp.zeros((nb, lc, BRANCH_W), F32)
        yconv = jnp.concatenate([yconv_ctx, hyena_conv_seg(u[:, lc:], fargs)], axis=1)

        x1, hf, gates_t = merge(ya, x0, u, yconv, ycv, yd, h3, x_all, mod3, nct, hyena_skip[li], wl[:, OFF_GATE:],
                                w_branch[li].astype(BF16), w_out[li].astype(BF16), ln1_g[li], ln1_b[li],
                                rwt, router_bias)
        gates = jnp.pad(gates_t.T, ((0, 0), (0, LANES - ROUTE_ROWS)))
        hs, gs, pt, meta = moe_sort(hf.reshape(nb * ta, d), gates_t, gates)
        ysb = moe_group(hs, gs, meta[:, 0, :META_LANES].reshape(-1), exp_w1[li].astype(BF16),
                        exp_w3[li].astype(BF16), exp_w2[li].astype(BF16))
        if li == depth - 1:
            return ln2(x1, pt, ysb, mod3, nct, ln2_g[li], ln2_b[li], None)
        x_all, h3 = ln2(x1, pt, ysb, mod3, nct, ln2_g[li], ln2_b[li], mod[li + 1].reshape(mod_rows, 1, N_MOD * d))


def hyena_conv_seg(u_seg, fargs):
    ktab = hyena_filter_table(u_seg.shape[1], *fargs)
    return hyena_conv(u_seg, ktab)
```

```python
import functools
import math

import numpy as np
import jax
import jax.numpy as jnp
from jax import lax
from jax.experimental import pallas as pl
from jax.experimental.pallas import tpu as pltpu

F32 = jnp.float32
BF16 = jnp.bfloat16
HI = lax.Precision.HIGHEST

D_MODEL = 1024
GRID_W = 64
BRANCH_W = 256
HEAD_DIM = 64
N_BRANCHES = 4
N_MOD = 6
RWKV_HEADS = 4
RWKV_COLS = 1024
RWKV_GN_EPS = 64e-5
HYENA_COLS = 768
HYENA_EMB = 33
HYENA_FAST_DECAY = 0.3
HYENA_SLOW_DECAY = 1.5
HYENA_TARGET = 1e-2
SCONV_COLS = 768
Q_W = 256
KV_W = 128
ATTN_COLS = 512
ROPE_THETA = 10000.0
RMS_EPS = 1e-6
OFF_HYENA = RWKV_COLS
OFF_SCONV = OFF_HYENA + HYENA_COLS
OFF_ATTN = OFF_SCONV + SCONV_COLS
OFF_GATE = OFF_ATTN + ATTN_COLS
N_EXPERTS = 16
N_GROUPS = 4
EXPERTS_PER_GROUP = 4
D_EXPERT = 512
LN_EPS = 1e-6

SUBLANES = 8
LANES = 128
TT = 256
SCAN_BLK = LANES
HY_BLK = 256
VMEM_LIMIT = 56 * 1024 * 1024


def _cparams(sem, vmem=None):
    return pltpu.CompilerParams(dimension_semantics=sem, vmem_limit_bytes=vmem)


def _ln(xf):
    mu = jnp.mean(xf, -1, keepdims=True)
    xc = xf - mu
    var = jnp.mean(xc * xc, -1, keepdims=True)
    return xc * lax.rsqrt(var + LN_EPS)


def _head_sums(x, ones_bd):
    ones16 = ones_bd.astype(BF16)
    hi = x.astype(BF16)
    lo = (x - hi.astype(F32)).astype(BF16)
    return jnp.dot(hi, ones16, preferred_element_type=F32) + jnp.dot(lo, ones16, preferred_element_type=F32)


def _pick_tile(n, cands):
    for c in cands:
        if n % c == 0:
            return c
    raise ValueError(f"no tile for {n}")


def _ada_kernel(c_ref, w_ref, b_ref, o_ref):
    c = c_ref[...]
    a = c * jax.nn.sigmoid(c)
    o_ref[0] = jnp.dot(a, w_ref[0], precision=HI, preferred_element_type=F32) + b_ref[0]


def ada_mod(cc, ada_w, ada_b):
    depth, d, n = ada_w.shape
    rows = cc.shape[0]
    return pl.pallas_call(
        _ada_kernel,
        out_shape=jax.ShapeDtypeStruct((depth, rows, n), F32),
        grid=(depth, n // d),
        in_specs=[pl.BlockSpec((rows, d), lambda l, j: (0, 0)),
                  pl.BlockSpec((1, d, d), lambda l, j: (l, 0, j)),
                  pl.BlockSpec((1, 1, d), lambda l, j: (l, 0, j))],
        out_specs=pl.BlockSpec((1, rows, d), lambda l, j: (l, 0, j)),
        compiler_params=_cparams(("parallel", "parallel"), VMEM_LIMIT),
        name="ada_mod",
    )(cc, ada_w, ada_b.reshape(depth, 1, n))


def _lnmod_kernel(x_ref, sh_ref, sc_ref, o_ref):
    h = _ln(x_ref[0]) * (1.0 + sc_ref[0]) + sh_ref[0]
    o_ref[0] = h.astype(BF16)


def _mod_spec(col, nct, nb):
    return pl.BlockSpec((1, 1, D_MODEL), lambda b, t: (jnp.where(t < nct, nb, b), 0, col))


def lnmod(x_all, mod3, nct, col_shift, col_scale):
    nb, ta, d = x_all.shape
    return pl.pallas_call(
        _lnmod_kernel,
        out_shape=jax.ShapeDtypeStruct((nb, ta, d), BF16),
        grid=(nb, ta // TT),
        in_specs=[pl.BlockSpec((1, TT, d), lambda b, t: (b, t, 0)),
                  _mod_spec(col_shift, nct, nb), _mod_spec(col_scale, nct, nb)],
        out_specs=pl.BlockSpec((1, TT, d), lambda b, t: (b, t, 0)),
        compiler_params=_cparams(("parallel", "parallel")),
        name="lnmod",
    )(x_all, mod3, mod3)


def _mm_kernel(a_ref, b_ref, o_ref):
    o_ref[...] = jnp.dot(a_ref[...], b_ref[...], preferred_element_type=F32).astype(o_ref.dtype)


def matmul(a, b, out_dtype=F32):
    m, k = a.shape
    _, n = b.shape
    tm = _pick_tile(m, (1024, 512, 256))
    tn = _pick_tile(n, (1024, 512, 256))
    return pl.pallas_call(
        _mm_kernel,
        out_shape=jax.ShapeDtypeStruct((m, n), out_dtype),
        grid=(m // tm, n // tn),
        in_specs=[pl.BlockSpec((tm, k), lambda i, j: (i, 0)),
                  pl.BlockSpec((k, tn), lambda i, j: (0, j))],
        out_specs=pl.BlockSpec((tm, tn), lambda i, j: (i, j)),
        compiler_params=_cparams(("parallel", "parallel"), VMEM_LIMIT),
        name="matmul",
    )(a, b)


def _halo_specs(width, ta):
    nblk8 = ta // SUBLANES
    per = TT // SUBLANES
    cur = pl.BlockSpec((1, TT, width), lambda b, t: (b, t, 0))
    prev = pl.BlockSpec((1, SUBLANES, width), lambda b, t: (b, jnp.maximum(t * per - 1, 0), 0))
    nxt = pl.BlockSpec((1, SUBLANES, width), lambda b, t: (b, jnp.minimum((t + 1) * per, nblk8 - 1), 0))
    return [cur, prev, nxt]


def _neighbours(cur, prev_ref, next_ref, nct, nt):
    t = pl.program_id(1)
    seg_start = jnp.logical_or(t == 0, t == nct)
    seg_end = jnp.logical_or(t == nct - 1, t == nt - 1)
    prev_row = prev_ref[0][SUBLANES - 1:SUBLANES, :] * jnp.where(seg_start, 0.0, 1.0)
    next_row = next_ref[0][0:1, :] * jnp.where(seg_end, 0.0, 1.0)
    row = lax.broadcasted_iota(jnp.int32, (TT, 1), 0)
    xm1 = jnp.where(row == 0, prev_row, pltpu.roll(cur, 1, axis=0))
    xp1 = jnp.where(row == TT - 1, next_row, pltpu.roll(cur, TT - 1, axis=0))
    return xm1, xp1


def _rwkv_prep_kernel(nct, nt, cur_ref, prev_ref, next_ref, mu_ref, w0_ref, wup_ref, a0_ref, aup_ref,
                      gup_ref, kk_ref, ka_ref, rk_ref, bd_ref,
                      r_o, kk_o, w0_o, w1_o, k0_o, k1_o, b0_o, b1_o, v_o, g_o, bon_o):
    cur = cur_ref[0]
    xm1, xp1 = _neighbours(cur, prev_ref, next_ref, nct, nt)
    p = cur + mu_ref[...] * (0.5 * (xm1 + xp1) - cur)
    c = BRANCH_W
    r, k, v = p[:, 0:c], p[:, c:2 * c], p[:, 2 * c:3 * c]
    wd = p[:, 3 * c:3 * c + 64]
    ad = p[:, 3 * c + 64:3 * c + 128]
    gd = p[:, 3 * c + 128:3 * c + 256]
    bd = bd_ref[...]
    kk = k * kk_ref[...]
    ss = _head_sums(kk * kk, bd)
    kkn = kk * lax.rsqrt(jnp.maximum(ss, 1e-24))
    twd = jnp.tanh(wd)
    ka = ka_ref[...]
    kdirs = []
    w_outs, k_outs, b_outs = (w0_o, w1_o), (k0_o, k1_o), (b0_o, b1_o)
    for d in range(2):
        wlog = w0_ref[d:d + 1, :] + jnp.dot(twd, wup_ref[d], precision=HI, preferred_element_type=F32)
        decay = -math.exp(-0.5) * jax.nn.sigmoid(wlog)
        a = jax.nn.sigmoid(a0_ref[d:d + 1, :] + jnp.dot(ad, aup_ref[d], precision=HI, preferred_element_type=F32))
        kdir = k * (1.0 + (a - 1.0) * ka)
        bdir = kkn * a
        kdirs.append(kdir)
        w_outs[d][0] = decay
        k_outs[d][0] = kdir
        b_outs[d][0] = bdir
    r_o[0] = r
    kk_o[0] = kkn
    v_o[0] = v
    g_o[0] = jnp.dot(jax.nn.sigmoid(gd), gup_ref[...], precision=HI, preferred_element_type=F32)
    rkk = r * rk_ref[...] * (kdirs[0] + kdirs[1])
    bon_o[0] = _head_sums(rkk, bd) * v


def rwkv_prep(p_rwkv, nct, mu, w0, w_up, a0, a_up, g_up, k_k, k_a, r_k, bd):
    nb, ta, _ = p_rwkv.shape
    nt = ta // TT
    c = BRANCH_W
    nat = jax.ShapeDtypeStruct((nb, ta, c), F32)
    nat_spec = pl.BlockSpec((1, TT, c), lambda b, t: (b, t, 0))

    def full(a):
        nd = a.ndim
        return pl.BlockSpec(a.shape, lambda b, t: (0,) * nd)

    consts = [mu.reshape(1, -1), w0, w_up, a0, a_up, g_up, k_k.reshape(1, -1), k_a.reshape(1, -1),
              r_k.reshape(1, -1), bd]
    return pl.pallas_call(
        functools.partial(_rwkv_prep_kernel, nct, nt),
        out_shape=[nat] * 11,
        grid=(nb, nt),
        in_specs=_halo_specs(RWKV_COLS, ta) + [full(a) for a in consts],
        out_specs=[nat_spec] * 11,
        compiler_params=_cparams(("parallel", "parallel"), VMEM_LIMIT),
        name="rwkv_prep",
    )(p_rwkv, p_rwkv, p_rwkv, *consts)


CHUNK = 32
PREP_NB = 2


def _chunk_scan_rows(x, reverse):
    pos = lax.broadcasted_iota(jnp.int32, x.shape, 0) % CHUNK
    step = 1
    while step < CHUNK:
        if reverse:
            x = x + jnp.where(pos < CHUNK - step, pltpu.roll(x, x.shape[0] - step, axis=0), 0.0)
        else:
            x = x + jnp.where(pos >= step, pltpu.roll(x, step, axis=0), 0.0)
        step *= 2
    return x


def _chunk_prep_kernel(r_ref, kk_ref, v_ref, lw0, k0, b0, lw1, k1, b1,
                       a0_o, bm0_o, rp0_o, y00_o, a1_o, bm1_o, rp1_o, y01_o):
    blk = SCAN_BLK
    npair = BRANCH_W // LANES
    nchunk = blk // CHUNK
    ti = lax.broadcasted_iota(jnp.int32, (blk, blk), 0)
    si = lax.broadcasted_iota(jnp.int32, (blk, blk), 1)
    same = (ti // CHUNK) == (si // CHUNK)
    eye = ti == si
    bd64 = (ti // HEAD_DIM) == (si // HEAD_DIM)
    head0 = si < HEAD_DIM
    lane_half = lax.broadcasted_iota(jnp.int32, (HEAD_DIM, LANES), 1)
    eyef = jnp.where(eye, 1.0, 0.0)
    dot = lambda x, y: jnp.dot(x, y, preferred_element_type=F32)
    lo = lambda x: x.astype(BF16)
    split = lambda x: jnp.concatenate([jnp.where(head0, x, 0.0), jnp.where(head0, 0.0, x)], axis=0)
    cat = lambda ms: lo(jnp.concatenate(ms, axis=1))

    probs = []
    for bi, d in [(bi, d) for bi in range(PREP_NB) for d in range(2)]:
        lw_ref, k_ref, b_ref = ((lw0, k0, b0), (lw1, k1, b1))[d]
        reverse = d == 1
        r, kk, v = r_ref[bi], kk_ref[bi], v_ref[bi]
        lw, k, b = lw_ref[bi], k_ref[bi], b_ref[bi]
        lg = _chunk_scan_rows(lw, reverse)
        lg_end = lg + _chunk_scan_rows(lw, not reverse) - lw
        g, gi, g_end = jnp.exp(lg), jnp.exp(-lg), jnp.exp(lg_end)
        to_end = jnp.exp(lg_end - lg)
        arrs = (kk * jnp.exp(lg - lw), b * gi, k * gi, r * g, v, k * to_end, b * to_end, g_end)
        incl = jnp.logical_and(same, si >= ti if reverse else si <= ti)
        strict = jnp.logical_and(same, si > ti if reverse else si < ti)
        for p in range(npair):
            lanes = slice(p * LANES, (p + 1) * LANES)
            probs.append(dict(bi=bi, d=d, p=p, incl=incl, strict=strict, arrs=tuple(a[:, lanes] for a in arrs)))

    for q in probs:
        pp_, q_, kt_, rt_ = q["arrs"][:4]
        rhs_g = lo(jnp.concatenate([q_, kt_], axis=0))
        lm, mm, n2, nn = [], [], [], []
        for hh in range(2):
            hm = head0 if hh == 0 else jnp.logical_not(head0)
            lhs_g = lo(jnp.concatenate([jnp.where(hm, pp_, 0.0), jnp.where(hm, rt_, 0.0)], axis=0))
            gm = lax.dot_general(lhs_g, rhs_g, (((1,), (1,)), ((), ())), preferred_element_type=F32)
            lm.append(jnp.where(q["strict"], gm[:blk, :blk], 0.0))
            mm.append(jnp.where(q["strict"], gm[:blk, blk:], 0.0))
            n2.append(jnp.where(q["incl"], gm[blk:, :blk], 0.0))
            nn.append(jnp.where(q["incl"], gm[blk:, blk:], 0.0))
        q["pw"], q["tm"] = lm, [eyef - lm[0], eyef - lm[1]]
        q["m_cat"], q["n2_cat"], q["nn_cat"] = cat(mm), cat(n2), cat(nn)
    for _ in range(CHUNK.bit_length() - 2):
        for q in probs:
            pwl = [lo(x) for x in q["pw"]]
            q["pw"] = [dot(x, x) for x in pwl]
        for q in probs:
            q["tm"] = [dot(lo(t), lo(eyef + x)) for t, x in zip(q["tm"], q["pw"])]
    for q in probs:
        q["t_cat"] = cat(q["tm"])
        q["v_st"] = lo(split(q["arrs"][4]))
        q["pp"] = dot(q["t_cat"], lo(split(q["arrs"][0])))
        q["mv"] = dot(q["m_cat"], q["v_st"])
    for q in probs:
        q["w2"] = dot(q["t_cat"], lo(split(q["mv"])))
        q["rp"] = q["arrs"][3] - dot(q["n2_cat"], lo(split(q["pp"])))
    for q in probs:
        q["y0"] = dot(q["nn_cat"], q["v_st"]) - dot(q["n2_cat"], lo(split(q["w2"])))
    in_chunk = [si // CHUNK == c for c in range(nchunk)]
    for q in probs:
        ppt, vt, w2t = q["pp"].T, q["arrs"][4].T, q["w2"].T
        kg_, qg_ = q["arrs"][5], q["arrs"][6]
        lhs_a = jnp.concatenate([jnp.where(cm, ppt, 0.0) for cm in in_chunk], axis=0)
        q["pq"] = dot(lo(lhs_a), lo(qg_))
        lhs_b = jnp.concatenate([jnp.concatenate([jnp.where(cm, vt, 0.0), jnp.where(cm, -w2t, 0.0)], axis=1)
                                 for cm in in_chunk], axis=0)
        q["bf"] = dot(lo(lhs_b), lo(jnp.concatenate([kg_, qg_], axis=0)))
    outs = ((a0_o, bm0_o, rp0_o, y00_o), (a1_o, bm1_o, rp1_o, y01_o))
    for q in probs:
        a_o, bm_o = outs[q["d"]][:2]
        g_end_p = q["arrs"][7]
        for c in range(nchunk):
            pq_c = q["pq"][c * blk:(c + 1) * blk]
            a_o[q["bi"], 0, c, q["p"]] = (jnp.where(eye, g_end_p[c * CHUNK:c * CHUNK + 1], 0.0)
                                          - jnp.where(bd64, pq_c, 0.0)).astype(BF16)
            bm_o[q["bi"], 0, c, q["p"]] = jnp.where(lane_half < HEAD_DIM, q["bf"][c * blk:c * blk + HEAD_DIM],
                                                    q["bf"][c * blk + HEAD_DIM:(c + 1) * blk])
    for bi in range(PREP_NB):
        for d in range(2):
            rp_o, y0_o = outs[d][2:]
            rp_o[bi] = jnp.concatenate([q["rp"] for q in probs if q["d"] == d and q["bi"] == bi], axis=1)
            y0_o[bi] = jnp.concatenate([q["y0"] for q in probs if q["d"] == d and q["bi"] == bi], axis=1)


def chunk_prep(r, kk, v, lw0, lw1, k0, k1, b0, b1):
    nb, ta, c = r.shape
    nblk = ta // SCAN_BLK
    npair = c // LANES
    nchunk = SCAN_BLK // CHUNK
    pb = PREP_NB
    assert nb % pb == 0
    nat = pl.BlockSpec((pb, SCAN_BLK, c), lambda b, s: (b, s, 0))
    a_shape = jax.ShapeDtypeStruct((nb, nblk, nchunk, npair, LANES, LANES), BF16)
    bm_shape = jax.ShapeDtypeStruct((nb, nblk, nchunk, npair, HEAD_DIM, LANES), F32)
    nat_shape = jax.ShapeDtypeStruct((nb, ta, c), F32)
    a_spec = pl.BlockSpec((pb, 1, nchunk, npair, LANES, LANES), lambda b, s: (b, s, 0, 0, 0, 0))
    bm_spec = pl.BlockSpec((pb, 1, nchunk, npair, HEAD_DIM, LANES), lambda b, s: (b, s, 0, 0, 0, 0))
    return pl.pallas_call(
        _chunk_prep_kernel,
        out_shape=[a_shape, bm_shape, nat_shape, nat_shape] * 2,
        grid=(nb // pb, nblk),
        in_specs=[nat] * 9,
        out_specs=[a_spec, bm_spec, nat, nat] * 2,
        compiler_params=_cparams(("parallel", "parallel"), VMEM_LIMIT),
        name="chunk_prep",
    )(r, kk, v, lw0, k0, b0, lw1, k1, b1)


def _chunk_scan_kernel(nb, a0, bm0, rp0, y00, a1, bm1, rp1, y01, yf_o, yb_o, s_scr):
    step = pl.program_id(0)
    npair = BRANCH_W // LANES
    nchunk = SCAN_BLK // CHUNK

    @pl.when(step == 0)
    def _():
        s_scr[...] = jnp.zeros_like(s_scr)

    lane = lax.broadcasted_iota(jnp.int32, (CHUNK, LANES), 1)
    refs = ((a0, bm0, rp0, y00, yf_o), (a1, bm1, rp1, y01, yb_o))
    for ci in range(nchunk):
        for d in range(2):
            a_ref, bm_ref, rp_ref, y0_ref, y_ref = refs[d]
            c = ci if d == 0 else nchunk - 1 - ci
            rows = slice(c * CHUNK, (c + 1) * CHUNK)
            for b in range(nb):
                for p in range(npair):
                    lanes = slice(p * LANES, (p + 1) * LANES)
                    s = s_scr[d, b, p]
                    rpc = rp_ref[b, rows, lanes]
                    lhs = jnp.concatenate([jnp.where(lane < HEAD_DIM, rpc, 0.0), jnp.where(lane >= HEAD_DIM, rpc, 0.0)],
                                          axis=0)
                    yh = lax.dot_general(lhs, s, (((1,), (1,)), ((), ())), preferred_element_type=F32)
                    y_ref[b, rows, lanes] = jnp.concatenate([yh[:CHUNK], yh[CHUNK:]], axis=1) + y0_ref[b, rows, lanes]
                    s_scr[d, b, p] = (jnp.dot(s.astype(BF16), a_ref[b, 0, c, p], preferred_element_type=F32)
                                      + bm_ref[b, 0, c, p])


def rwkv_scan(r, kk, v, lw0, lw1, k0, k1, b0, b1, lc):
    nb, ta, c = r.shape
    nblk = ta // SCAN_BLK
    nctb = lc // SCAN_BLK
    npair = c // LANES
    nchunk = SCAN_BLK // CHUNK
    a0, bm0, rp0, y00, a1, bm1, rp1, y01 = chunk_prep(r, kk, v, lw0, lw1, k0, k1, b0, b1)

    def fwd(s):
        return s

    def bwd(s):
        return jnp.where(s < nctb, nctb - 1 - s, nblk - 1 - (s - nctb))

    def specs(idx):
        return [pl.BlockSpec((nb, 1, nchunk, npair, LANES, LANES), lambda s: (0, idx(s), 0, 0, 0, 0)),
                pl.BlockSpec((nb, 1, nchunk, npair, HEAD_DIM, LANES), lambda s: (0, idx(s), 0, 0, 0, 0)),
                pl.BlockSpec((nb, SCAN_BLK, c), lambda s: (0, idx(s), 0)),
                pl.BlockSpec((nb, SCAN_BLK, c), lambda s: (0, idx(s), 0))]

    out = jax.ShapeDtypeStruct((nb, ta, c), F32)
    return pl.pallas_call(
        functools.partial(_chunk_scan_kernel, nb),
        out_shape=[out, out],
        grid=(nblk,),
        in_specs=specs(fwd) + specs(bwd),
        out_specs=[pl.BlockSpec((nb, SCAN_BLK, c), lambda s: (0, fwd(s), 0)),
                   pl.BlockSpec((nb, SCAN_BLK, c), lambda s: (0, bwd(s), 0))],
        scratch_shapes=[pltpu.VMEM((2, nb, npair, HEAD_DIM, LANES), F32)],
        compiler_params=_cparams(("arbitrary",), VMEM_LIMIT),
        name="chunk_scan",
    )(a0, bm0, rp0, y00, a1, bm1, rp1, y01)


def _rwkv_out_kernel(yf_ref, yb_ref, g_ref, bon_ref, lg_ref, lb_ref, bd_ref, o_ref):
    y = yf_ref[0] + yb_ref[0]
    bd = bd_ref[...]
    mu = _head_sums(y, bd) * (1.0 / HEAD_DIM)
    yc = y - mu
    var = _head_sums(yc * yc, bd) * (1.0 / HEAD_DIM)
    yn = yc * lax.rsqrt(var + RWKV_GN_EPS) * lg_ref[...] + lb_ref[...]
    o_ref[0] = ((yn + bon_ref[0]) * g_ref[0]).astype(BF16)


def rwkv_out(yf, yb, g, bon, lnx_g, lnx_b, bd):
    nb, ta, _ = yf.shape
    nat_spec = pl.BlockSpec((1, TT, BRANCH_W), lambda b, t: (b, t, 0))
    row = pl.BlockSpec((1, BRANCH_W), lambda b, t: (0, 0))
    return pl.pallas_call(
        _rwkv_out_kernel,
        out_shape=jax.ShapeDtypeStruct((nb, ta, BRANCH_W), BF16),
        grid=(nb, ta // TT),
        in_specs=[nat_spec] * 4 + [row, row, pl.BlockSpec(bd.shape, lambda b, t: (0, 0))],
        out_specs=nat_spec,
        compiler_params=_cparams(("parallel", "parallel")),
        name="rwkv_out",
    )(yf, yb, g, bon, lnx_g.reshape(1, -1), lnx_b.reshape(1, -1), bd)


def _pair_swap(x):
    lane = lax.broadcasted_iota(jnp.int32, x.shape, 1)
    n = x.shape[1]
    return jnp.where(lane % 2 == 0, pltpu.roll(x, n - 1, axis=1), pltpu.roll(x, 1, axis=1))


def _attn_prep_kernel(p_ref, cos_ref, sin_ref, qg_ref, kg_ref, bd_ref, q_o, k_o, v_o):
    p = p_ref[0]
    q, k, v = p[:, :Q_W], p[:, Q_W:Q_W + KV_W], p[:, Q_W + KV_W:]
    bd = bd_ref[...]
    cos, sin = cos_ref[...], sin_ref[...]
    qms = _head_sums(q * q, bd) * (1.0 / HEAD_DIM)
    qn = q * lax.rsqrt(qms + RMS_EPS) * qg_ref[...]
    qr = qn * cos + _pair_swap(qn) * sin
    q_o[0] = (qr * HEAD_DIM ** -0.5).astype(BF16)
    kms = _head_sums(k * k, bd[:KV_W, :KV_W]) * (1.0 / HEAD_DIM)
    kn = k * lax.rsqrt(kms + RMS_EPS) * kg_ref[...]
    kr = kn * cos[:, :KV_W] + _pair_swap(kn) * sin[:, :KV_W]
    for g in range(KV_W // HEAD_DIM):
        sl = slice(g * HEAD_DIM, (g + 1) * HEAD_DIM)
        k_o[0, g] = kr[:, sl].astype(BF16)
        v_o[0, g] = v[:, sl].astype(BF16)


def attn_prep(p_attn, cos_t, sin_t, q_norm, k_norm, bd):
    nb, ta, _ = p_attn.shape
    ng = KV_W // HEAD_DIM
    qg = jnp.tile(q_norm, Q_W // HEAD_DIM).reshape(1, -1)
    kg = jnp.tile(k_norm, ng).reshape(1, -1)
    kv_shape = jax.ShapeDtypeStruct((nb, ng, ta, HEAD_DIM), BF16)
    kv_spec = pl.BlockSpec((1, ng, TT, HEAD_DIM), lambda b, t: (b, 0, t, 0))
    return pl.pallas_call(
        _attn_prep_kernel,
        out_shape=[jax.ShapeDtypeStruct((nb, ta, Q_W), BF16), kv_shape, kv_shape],
        grid=(nb, ta // TT),
        in_specs=[pl.BlockSpec((1, TT, ATTN_COLS), lambda b, t: (b, t, 0)),
                  pl.BlockSpec((TT, Q_W), lambda b, t: (t, 0)),
                  pl.BlockSpec((TT, Q_W), lambda b, t: (t, 0)),
                  pl.BlockSpec((1, Q_W), lambda b, t: (0, 0)),
                  pl.BlockSpec((1, KV_W), lambda b, t: (0, 0)),
                  pl.BlockSpec(bd.shape, lambda b, t: (0, 0))],
        out_specs=[pl.BlockSpec((1, TT, Q_W), lambda b, t: (b, t, 0)), kv_spec, kv_spec],
        compiler_params=_cparams(("parallel", "parallel")),
        name="attn_prep",
    )(p_attn, cos_t, sin_t, qg, kg, bd)


def _attn_kernel(nct, lc, q_ref, k_ref, v_ref, o_ref):
    t = pl.program_id(1)
    ng = k_ref.shape[1]
    rep = Q_W // HEAD_DIM // ng

    def run(nk):
        outs = []
        for g in range(ng):
            kk, vv = k_ref[0, g, :nk, :], v_ref[0, g, :nk, :]
            for r in range(rep):
                h = g * rep + r
                q = q_ref[0][:, h * HEAD_DIM:(h + 1) * HEAD_DIM]
                s = lax.dot_general(q, kk, (((1,), (1,)), ((), ())), preferred_element_type=F32)
                p = jnp.exp(s - jnp.max(s, -1, keepdims=True))
                l = jnp.sum(p, -1, keepdims=True)
                o = jnp.dot(p.astype(BF16), vv, preferred_element_type=F32)
                outs.append(o / l)
        o_ref[0] = jnp.concatenate(outs, axis=1).astype(BF16)

    @pl.when(t < nct)
    def _():
        run(lc)

    @pl.when(t >= nct)
    def _():
        run(k_ref.shape[2])


def attention(q, k, v, lc):
    nb, ta, _ = q.shape
    ng = k.shape[1]
    nct = lc // TT
    qo_spec = pl.BlockSpec((1, TT, Q_W), lambda b, t: (b, t, 0))
    kv_spec = pl.BlockSpec((1, ng, ta, HEAD_DIM), lambda b, t: (b, 0, 0, 0))
    return pl.pallas_call(
        functools.partial(_attn_kernel, nct, lc),
        out_shape=jax.ShapeDtypeStruct((nb, ta, Q_W), BF16),
        grid=(nb, ta // TT),
        in_specs=[qo_spec, kv_spec, kv_spec],
        out_specs=qo_spec,
        compiler_params=_cparams(("parallel", "arbitrary"), VMEM_LIMIT),
        name="attention",
    )(q, k, v)


def _hs_pre_kernel(nct, nt, cur_ref, prev_ref, next_ref, hw_ref, sw_ref, x0_o, u_o, ycv_o):
    cur = cur_ref[0]
    xm1, xp1 = _neighbours(cur, prev_ref, next_ref, nct, nt)
    c = BRANCH_W
    hc = HYENA_COLS
    hw = hw_ref[...]
    ph = hw[0:1] * xm1[:, :hc] + hw[1:2] * cur[:, :hc] + hw[2:3] * xp1[:, :hc]
    x0_o[0] = ph[:, :c]
    u_o[0] = ph[:, c:2 * c] * ph[:, 2 * c:3 * c]
    sw = sw_ref[...]

    def cx(a):
        return a[:, hc + c:hc + 2 * c] * a[:, hc + 2 * c:hc + 3 * c]

    conv = sw[0:1] * cx(xm1) + sw[1:2] * cx(cur) + sw[2:3] * cx(xp1)
    ycv_o[0] = (cur[:, hc:hc + c] * conv).astype(BF16)


def hs_pre(p_hs, nct, hyena_conv, sconv_w):
    nb, ta, w = p_hs.shape
    nt = ta // TT
    nat = pl.BlockSpec((1, TT, BRANCH_W), lambda b, t: (b, t, 0))
    return pl.pallas_call(
        functools.partial(_hs_pre_kernel, nct, nt),
        out_shape=[jax.ShapeDtypeStruct((nb, ta, BRANCH_W), F32),
                   jax.ShapeDtypeStruct((nb, ta, BRANCH_W), F32),
                   jax.ShapeDtypeStruct((nb, ta, BRANCH_W), BF16)],
        grid=(nb, nt),
        in_specs=_halo_specs(w, ta) + [pl.BlockSpec(hyena_conv.shape, lambda b, t: (0, 0)),
                                       pl.BlockSpec(sconv_w.shape, lambda b, t: (0, 0))],
        out_specs=[nat, nat, nat],
        compiler_params=_cparams(("parallel", "parallel"), VMEM_LIMIT),
        name="hs_pre",
    )(p_hs, p_hs, p_hs, hyena_conv, sconv_w)


EMB_PAD = 40


def _filter_tables(lh):
    n = np.arange(2 * lh)
    pos = np.abs(n - (lh - 1)).astype(np.float64)
    bands = (HYENA_EMB - 1) // 2
    t = np.minimum(pos, lh - 1) / (lh - 1)
    wpos = 2.0 * math.pi * pos / lh
    f = np.linspace(1e-4, bands - 1, bands)[:, None]
    z = np.zeros((EMB_PAD, 2 * lh), np.float32)
    z[0] = t
    z[1:1 + bands] = np.cos(f * wpos[None, :])
    z[1 + bands:1 + 2 * bands] = -np.sin(f * wpos[None, :])
    max_decay = math.log(HYENA_TARGET) / HYENA_FAST_DECAY
    min_decay = math.log(HYENA_TARGET) / HYENA_SLOW_DECAY
    deltas = np.abs(np.linspace(min_decay, max_decay, BRANCH_W)).astype(np.float32)
    return z, deltas.reshape(-1, 1)


def _filter_kernel(lh, tn, z_ref, w1_ref, b1_ref, f1_ref, w2_ref, b2_ref, f2_ref, w3_ref, dl_ref, o_ref):
    z = z_ref[...]
    h1 = jnp.sin(f1_ref[...] * (jnp.dot(w1_ref[...], z, precision=HI, preferred_element_type=F32) + b1_ref[...]))
    h2 = jnp.sin(f2_ref[...] * (jnp.dot(w2_ref[...], h1, precision=HI, preferred_element_type=F32) + b2_ref[...]))
    f = jnp.dot(w3_ref[...], h2, precision=HI, preferred_element_type=F32)
    n = pl.program_id(0) * tn + lax.broadcasted_iota(jnp.int32, (1, tn), 1)
    filt = jnp.where(n >= lh - 1, f[:BRANCH_W], f[BRANCH_W:])
    win = jnp.exp(-z[0:1, :] * dl_ref[...])
    o_ref[...] = jnp.where(n == 2 * lh - 1, 0.0, filt * win)


def hyena_filter_table(lh, w1, b1, f1, w2, b2, f2, w3):
    z_np, dl_np = _filter_tables(lh)
    n2 = 2 * lh
    tn = _pick_tile(n2, (1024, 512))
    hd = w2.shape[0]
    w1t = jnp.zeros((hd, EMB_PAD), F32).at[:, :HYENA_EMB].set(w1.T)
    args = [jnp.asarray(z_np), w1t, b1.reshape(-1, 1), f1.reshape(-1, 1), w2.T, b2.reshape(-1, 1),
            f2.reshape(-1, 1), w3.T, jnp.asarray(dl_np)]

    def full(a):
        return pl.BlockSpec(a.shape, lambda j: (0, 0))

    return pl.pallas_call(
        functools.partial(_filter_kernel, lh, tn),
        out_shape=jax.ShapeDtypeStruct((BRANCH_W, n2), F32),
        grid=(n2 // tn,),
        in_specs=[pl.BlockSpec((EMB_PAD, tn), lambda j: (0, j))] + [full(a) for a in args[1:]],
        out_specs=pl.BlockSpec((BRANCH_W, tn), lambda j: (0, j)),
        compiler_params=_cparams(("parallel",)),
        name="hyena_filter",
    )(*args)


def _hyena_conv_kernel(nblk, bp, nch, k_ref, u_ref, o_ref, t_scr):
    ntile = 4 * nblk - 1
    mc = 2 * nblk - 1
    width = (ntile + 1) * LANES
    for ch in range(nch):
        big = pltpu.roll(jnp.broadcast_to(k_ref[ch], (LANES, width)), width - (LANES - 1), 1, stride=1, stride_axis=0)
        for m in range(ntile):
            t_scr[ch, m] = big[:, m * LANES:(m + 1) * LANES].astype(BF16)

    for ch in range(nch):
        for d in [0] + [s * a for a in range(1, nblk) for s in (1, -1)]:
            m0 = 2 * d + mc
            w = jnp.concatenate([jnp.concatenate([t_scr[ch, m0], t_scr[ch, m0 + 1]], axis=1),
                                 jnp.concatenate([t_scr[ch, m0 - 1], t_scr[ch, m0]], axis=1)], axis=0)
            i0, i1 = max(0, d), min(nblk, nblk + d)
            lhs = u_ref[ch, (i0 - d) * bp:(i1 - d) * bp, :].astype(BF16)
            res = jnp.dot(lhs, w, preferred_element_type=F32)
            if d == 0:
                o_ref[ch] = res
            else:
                o_ref[ch, i0 * bp:i1 * bp, :] += res


def hyena_conv(u, ktab):
    nb, l, c = u.shape
    nblk = l // HY_BLK
    bp = -(-nb // SUBLANES) * SUBLANES
    nch = SUBLANES if nblk == 1 else 1
    ut = jnp.transpose(u.reshape(nb, nblk, HY_BLK, c), (3, 1, 0, 2))
    if bp != nb:
        ut = jnp.pad(ut, ((0, 0), (0, 0), (0, bp - nb), (0, 0)))
    ut = ut.reshape(c, nblk * bp, HY_BLK)
    k3 = ktab.reshape(c, 1, 4 * nblk * LANES)
    out = pl.pallas_call(
        functools.partial(_hyena_conv_kernel, nblk, bp, nch),
        out_shape=jax.ShapeDtypeStruct((c, nblk * bp, HY_BLK), F32),
        grid=(c // nch,),
        in_specs=[pl.BlockSpec((nch, 1, 4 * nblk * LANES), lambda ch: (ch, 0, 0)),
                  pl.BlockSpec((nch, nblk * bp, HY_BLK), lambda ch: (ch, 0, 0))],
        out_specs=pl.BlockSpec((nch, nblk * bp, HY_BLK), lambda ch: (ch, 0, 0)),
        scratch_shapes=[pltpu.VMEM((nch, 4 * nblk - 1, LANES, LANES), BF16)],
        compiler_params=_cparams(("parallel",)),
        name="hyena_conv",
    )(k3, ut)
    out = out.reshape(c, nblk, bp, HY_BLK)[:, :, :nb]
    return jnp.transpose(out, (2, 1, 3, 0)).reshape(nb, l, c)


def _route(logits, bias):
    s = jax.nn.sigmoid(logits)
    sel = s + bias
    srow = [s[e:e + 1] for e in range(N_EXPERTS)]
    row = [sel[e:e + 1] for e in range(N_EXPERTS)]
    best, gi = None, None
    for g in range(N_GROUPS):
        a, b, c, d = row[4 * g:4 * g + 4]
        hi1, lo1, hi2, lo2 = jnp.maximum(a, b), jnp.minimum(a, b), jnp.maximum(c, d), jnp.minimum(c, d)
        score = jnp.maximum(hi1, hi2) + jnp.maximum(jnp.minimum(hi1, hi2), jnp.maximum(lo1, lo2))
        if g == 0:
            best, gi = score, jnp.zeros(score.shape, jnp.int32)
        else:
            better = score > best
            gi = jnp.where(better, g, gi)
            best = jnp.where(better, score, best)
    neg = -jnp.inf
    msel = [jnp.where(gi == e // EXPERTS_PER_GROUP, row[e], neg) for e in range(N_EXPERTS)]

    def arg_first_max(vals):
        bv, bi = vals[0], jnp.zeros(vals[0].shape, jnp.int32)
        for e in range(1, N_EXPERTS):
            better = vals[e] > bv
            bi = jnp.where(better, e, bi)
            bv = jnp.where(better, vals[e], bv)
        return bi

    i1 = arg_first_max(msel)
    i2 = arg_first_max([jnp.where(i1 == e, neg, msel[e]) for e in range(N_EXPERTS)])
    w1 = sum(jnp.where(i1 == e, srow[e], 0.0) for e in range(N_EXPERTS))
    w2 = sum(jnp.where(i2 == e, srow[e], 0.0) for e in range(N_EXPERTS))
    den = w1 + w2
    g1, g2 = w1 / den, w2 / den
    rows = [jnp.where(i1 == e, g1, 0.0) + jnp.where(i2 == e, g2, 0.0) for e in range(N_EXPERTS)]
    rows.append(gi.astype(F32))
    rows.extend([jnp.zeros_like(g1)] * (ROUTE_ROWS - len(rows)))
    return jnp.concatenate(rows, axis=0)


MERGE_NB = 2


def _merge_kernel(ya_ref, x0_ref, u_ref, yc_ref, ycv_ref, yd_ref, h_ref, x_ref, *rest):
    mods, rest = rest[:3 * MERGE_NB], rest[3 * MERGE_NB:]
    skip_ref, wg_ref, wb_ref, wo_ref, g1_ref, b1_ref, rwt_ref, rb_ref = rest[:8]
    x1_o, hf_o = rest[8:10]
    gates_o = rest[10:]
    rows = MERGE_NB * TT
    flat = lambda ref: ref[...].reshape(rows, ref.shape[-1])
    yb = (flat(x0_ref) * (flat(yc_ref) + flat(u_ref) * skip_ref[...])).astype(BF16)
    ys = (flat(ya_ref), yb, flat(ycv_ref), flat(yd_ref))
    h = flat(h_ref)
    merged = None
    for n in range(N_BRANCHES):
        gate = jax.nn.sigmoid(jnp.dot(h, wg_ref[:, n * D_MODEL:(n + 1) * D_MODEL], preferred_element_type=F32))
        term = gate * jnp.dot(ys[n], wb_ref[n], preferred_element_type=F32)
        merged = term if merged is None else merged + term
    out = jnp.dot(merged.astype(BF16), wo_ref[...], preferred_element_type=F32)
    alpha = (2 * 2) ** 0.25
    for k in range(MERGE_NB):
        ga_ref, shf_ref, scf_ref = mods[3 * k:3 * k + 3]
        x1 = _ln(alpha * x_ref[k] + ga_ref[0] * out[k * TT:(k + 1) * TT]) * g1_ref[...] + b1_ref[...]
        hf = _ln(x1) * (1.0 + scf_ref[0]) + shf_ref[0]
        x1_o[k] = x1
        hf_o[k] = hf.astype(BF16)
        logits = lax.dot_general(rwt_ref[...], hf, (((1,), (1,)), ((), ())), precision=HI,
                                 preferred_element_type=F32)
        gates_o[k][...] = _route(logits, rb_ref[...])


def merge(ya, x0, u, yconv, ycv, yd, h, x_all, mod3, nct, skip, wg, wb, wo, ln_g, ln_b, rwt, rbias):
    nb, ta, d = x_all.shape
    nt = ta // TT
    mb = MERGE_NB
    assert nb % mb == 0
    nat = pl.BlockSpec((mb, TT, BRANCH_W), lambda b, t: (b, t, 0))
    wide = pl.BlockSpec((mb, TT, d), lambda b, t: (b, t, 0))

    def full(a):
        nd = a.ndim
        return pl.BlockSpec(a.shape, lambda b, t: (0,) * nd)

    def mod_spec(col, k):
        return pl.BlockSpec((1, 1, D_MODEL), lambda b, t: (jnp.where(t < nct, nb, b * mb + k), 0, col))

    mod_specs = [mod_spec(col, k) for k in range(mb) for col in (2, 3, 4)]
    consts = [skip.reshape(1, -1), wg, wb, wo, ln_g.reshape(1, -1), ln_b.reshape(1, -1), rwt, rbias.reshape(-1, 1)]
    gate_shape = jax.ShapeDtypeStruct((ROUTE_ROWS, nb // mb * ta), F32)
    gate_spec = pl.BlockSpec((ROUTE_ROWS, TT), lambda b, t: (0, b * nt + t))
    outs = pl.pallas_call(
        _merge_kernel,
        out_shape=[jax.ShapeDtypeStruct((nb, ta, d), F32), jax.ShapeDtypeStruct((nb, ta, d), BF16)] + [gate_shape] * mb,
        grid=(nb // mb, nt),
        in_specs=[nat] * 6 + [wide, wide] + mod_specs + [full(a) for a in consts],
        out_specs=[wide, wide] + [gate_spec] * mb,
        compiler_params=_cparams(("parallel", "parallel"), VMEM_LIMIT),
        name="merge",
    )(ya, x0, u, yconv, ycv, yd, h, x_all, *([mod3] * (3 * mb)), *consts)
    gates_t = jnp.stack([g.reshape(ROUTE_ROWS, nb // mb, ta) for g in outs[2:]], axis=2).reshape(ROUTE_ROWS, nb * ta)
    return outs[0], outs[1], gates_t


MOE_TILE = 1024
MOE_ALIGN = 2 * SUBLANES
MOE_CHUNK = 18 * MOE_ALIGN
MOE_SORTED = MOE_TILE + LANES
MOE_ROWS = MOE_SORTED + 3 * LANES
GID_ROW = N_EXPERTS
ROUTE_ROWS = 3 * SUBLANES
META_LANES = 2 * N_GROUPS
assert N_GROUPS * (MOE_ALIGN - 1) <= MOE_SORTED - MOE_TILE and MOE_SORTED + MOE_CHUNK <= MOE_ROWS


def _moe_sort_kernel(gt_ref, g_ref, h_ref, up_ref, hs_o, gs_o, pt_o, meta_o):
    gid = gt_ref[GID_ROW:GID_ROW + 1, :]
    onehot = [jnp.where(gid == float(g), 1.0, 0.0) for g in range(N_GROUPS)]
    g4 = jnp.concatenate(onehot + [jnp.zeros((SUBLANES - N_GROUPS, MOE_TILE), F32)], axis=0)
    before = jnp.dot(g4.astype(BF16), up_ref[...], preferred_element_type=F32)
    lane = lax.broadcasted_iota(jnp.int32, (SUBLANES, LANES), 1)
    meta = jnp.zeros((SUBLANES, LANES), F32)
    off = jnp.zeros((1, 1), F32)
    pos = jnp.zeros((1, MOE_TILE), F32)
    for g in range(N_GROUPS):
        cnt = jnp.sum(onehot[g], axis=1, keepdims=True)
        pos = pos + onehot[g] * (before[g:g + 1] + off)
        meta = jnp.where(lane == g, off, meta)
        meta = jnp.where(lane == N_GROUPS + g, cnt, meta)
        off = off + jnp.ceil(cnt * (1.0 / MOE_ALIGN)) * MOE_ALIGN
    meta_o[0] = meta.astype(jnp.int32)
    row = lax.broadcasted_iota(jnp.int32, (MOE_ROWS, MOE_TILE), 0)
    place = jnp.where(row == pos.astype(jnp.int32), 1.0, 0.0)
    p16 = place.astype(BF16)
    hs_o[0] = jnp.dot(p16, h_ref[...], preferred_element_type=F32).astype(BF16)
    gts = g_ref[...]
    hi = gts.astype(BF16)
    r1 = gts - hi.astype(F32)
    mid = r1.astype(BF16)
    low = (r1 - mid.astype(F32)).astype(BF16)
    gs_o[0] = (jnp.dot(p16, hi, preferred_element_type=F32) + jnp.dot(p16, mid, preferred_element_type=F32)
               + jnp.dot(p16, low, preferred_element_type=F32))
    pt_o[...] = place.T.astype(BF16)


def moe_sort(hf, gates_t, gates):
    n, d = hf.shape
    ntile = n // MOE_TILE
    upper = jnp.asarray(np.triu(np.ones((MOE_TILE, MOE_TILE), np.float32), 1), BF16)
    return pl.pallas_call(
        _moe_sort_kernel,
        out_shape=[jax.ShapeDtypeStruct((ntile, MOE_ROWS, d), BF16),
                   jax.ShapeDtypeStruct((ntile, MOE_ROWS, LANES), F32),
                   jax.ShapeDtypeStruct((n, MOE_ROWS), BF16),
                   jax.ShapeDtypeStruct((ntile, SUBLANES, LANES), jnp.int32)],
        grid=(ntile,),
        in_specs=[pl.BlockSpec((ROUTE_ROWS, MOE_TILE), lambda i: (0, i)),
                  pl.BlockSpec((MOE_TILE, LANES), lambda i: (i, 0)),
                  pl.BlockSpec((MOE_TILE, d), lambda i: (i, 0)),
                  pl.BlockSpec((MOE_TILE, MOE_TILE), lambda i: (0, 0))],
        out_specs=[pl.BlockSpec((1, MOE_ROWS, d), lambda i: (i, 0, 0)),
                   pl.BlockSpec((1, MOE_ROWS, LANES), lambda i: (i, 0, 0)),
                   pl.BlockSpec((MOE_TILE, MOE_ROWS), lambda i: (i, 0)),
                   pl.BlockSpec((1, SUBLANES, LANES), lambda i: (i, 0, 0))],
        compiler_params=_cparams(("parallel",), VMEM_LIMIT),
        name="moe_sort",
    )(gates_t, gates, hf, upper)


def _moe_group_kernel(meta_ref, hs_ref, gs_ref, w1_ref, w3_ref, w2_ref, prev_ref, ys_o):
    g, i = pl.program_id(0), pl.program_id(1)

    @pl.when(g == 0)
    def _():
        ys_o[...] = jnp.zeros_like(ys_o)

    @pl.when(g > 0)
    def _():
        ys_o[...] = prev_ref[...]

    off = meta_ref[i * META_LANES + g]
    cnt = meta_ref[i * META_LANES + N_GROUPS + g]
    lane = lax.broadcasted_iota(jnp.int32, (MOE_CHUNK, LANES), 1)

    def chunk(j, carry):
        rows = pl.ds(pl.multiple_of(off + j * MOE_CHUNK, MOE_ALIGN), MOE_CHUNK)
        hs = hs_ref[0, rows, :]
        gs = gs_ref[0, rows, :]
        acc = ys_o[0, rows, :].astype(F32)
        for e in range(EXPERTS_PER_GROUP):
            a = jnp.dot(hs, w1_ref[e], preferred_element_type=F32)
            b = jnp.dot(hs, w3_ref[e], preferred_element_type=F32)
            act = (a * jax.nn.sigmoid(a)) * b
            gcol = jnp.sum(jnp.where(lane == g * EXPERTS_PER_GROUP + e, gs, 0.0), axis=1, keepdims=True)
            acc = acc + gcol * jnp.dot(act.astype(BF16), w2_ref[e], preferred_element_type=F32)
        ys_o[0, rows, :] = acc.astype(BF16)
        return carry

    lax.fori_loop(0, lax.div(cnt + (MOE_CHUNK - 1), MOE_CHUNK), chunk, 0)


def moe_group(hs, gs, meta, w1, w3, w2):
    ntile, _, d = hs.shape
    de = w1.shape[2]
    epg = EXPERTS_PER_GROUP
    tile_spec = pl.BlockSpec((1, MOE_ROWS, d), lambda g, i, m: (i, 0, 0))
    grid_spec = pltpu.PrefetchScalarGridSpec(
        num_scalar_prefetch=1,
        grid=(N_GROUPS, ntile),
        in_specs=[tile_spec,
                  pl.BlockSpec((1, MOE_ROWS, LANES), lambda g, i, m: (i, 0, 0)),
                  pl.BlockSpec((epg, d, de), lambda g, i, m: (g, 0, 0)),
                  pl.BlockSpec((epg, d, de), lambda g, i, m: (g, 0, 0)),
                  pl.BlockSpec((epg, de, d), lambda g, i, m: (g, 0, 0)),
                  tile_spec],
        out_specs=tile_spec,
    )
    carried = jnp.zeros((ntile, MOE_ROWS, d), BF16)
    return pl.pallas_call(
        _moe_group_kernel,
        out_shape=jax.ShapeDtypeStruct((ntile, MOE_ROWS, d), BF16),
        grid_spec=grid_spec,
        input_output_aliases={6: 0},
        compiler_params=_cparams(("arbitrary", "arbitrary"), VMEM_LIMIT),
        name="moe_group",
    )(meta, hs, gs, w1, w3, w2, carried)


def _ln2_kernel(emit_next, x_ref, pt_ref, ys_ref, gf_ref, g_ref, b_ref, *rest):
    alpha = (2 * 2) ** 0.25
    f = jnp.dot(pt_ref[...], ys_ref[0], preferred_element_type=F32)
    x2 = _ln(alpha * x_ref[0] + gf_ref[0] * f) * g_ref[...] + b_ref[...]
    if emit_next:
        sh_ref, sc_ref, o_ref, h_o = rest
        h_o[0] = (_ln(x2) * (1.0 + sc_ref[0]) + sh_ref[0]).astype(BF16)
    else:
        o_ref, = rest
    o_ref[0] = x2


def ln2(x1, pt, ysb, mod3, nct, ln_g, ln_b, mod3_next):
    nb, ta, d = x1.shape
    nt = ta // TT
    per = MOE_TILE // TT
    wide = pl.BlockSpec((1, TT, d), lambda b, t: (b, t, 0))
    row = pl.BlockSpec((1, d), lambda b, t: (0, 0))
    in_specs = [wide, pl.BlockSpec((TT, MOE_SORTED), lambda b, t: (b * nt + t, 0)),
                pl.BlockSpec((1, MOE_SORTED, d), lambda b, t: ((b * nt + t) // per, 0, 0)),
                _mod_spec(5, nct, nb), row, row]
    args = [x1, pt, ysb, mod3, ln_g.reshape(1, -1), ln_b.reshape(1, -1)]
    if mod3_next is None:
        out_shape = jax.ShapeDtypeStruct((nb, ta - nct * TT, d), F32)
        out_specs = pl.BlockSpec((1, TT, d), lambda b, t: (b, jnp.maximum(t - nct, 0), 0))
    else:
        in_specs += [_mod_spec(0, nct, nb), _mod_spec(1, nct, nb)]
        args += [mod3_next, mod3_next]
        out_shape = [jax.ShapeDtypeStruct((nb, ta, d), F32), jax.ShapeDtypeStruct((nb, ta, d), BF16)]
        out_specs = [wide, wide]
    return pl.pallas_call(
        functools.partial(_ln2_kernel, mod3_next is not None),
        out_shape=out_shape,
        grid=(nb, nt),
        in_specs=in_specs,
        out_specs=out_specs,
        compiler_params=_cparams(("parallel", "arbitrary"), VMEM_LIMIT),
        name="ln2",
    )(*args)


def _rope_tables(l, lc):
    half = HEAD_DIM // 2
    inv = ROPE_THETA ** (-np.arange(0, half, 2, dtype=np.float64) / half)
    t = np.arange(l)
    rows, cols = t // GRID_W, t % GRID_W
    ang = np.concatenate([rows[:, None] * inv, cols[:, None] * inv], -1)
    ang = np.concatenate([np.zeros((lc, half)), ang], 0)
    cos = np.repeat(np.cos(ang), 2, axis=1)
    sin = np.repeat(np.sin(ang), 2, axis=1)
    sin[:, 0::2] *= -1.0
    reps = Q_W // HEAD_DIM
    return (jnp.asarray(np.tile(cos, (1, reps)), F32), jnp.asarray(np.tile(sin, (1, reps)), F32))


def _block_diag_ones():
    i = np.arange(BRANCH_W) // HEAD_DIM
    return jnp.asarray((i[:, None] == i[None, :]).astype(np.float32))


def kernel(x, c, ctx, c_ctx, ada_w, ada_b, w_in, rwkv_mu, rwkv_w0, rwkv_w_up, rwkv_a0, rwkv_a_up, rwkv_g_up, rwkv_k_k, rwkv_k_a, rwkv_r_k, rwkv_lnx_g, rwkv_lnx_b, hyena_conv, hyena_w1, hyena_b1, hyena_freq1, hyena_w2, hyena_b2, hyena_freq2, hyena_w3, hyena_skip, sconv_w, attn_q_norm, attn_k_norm, w_branch, w_out, ln1_g, ln1_b, ln2_g, ln2_b, router_w, router_bias, exp_w1, exp_w3, exp_w2):
    nb, l, d = x.shape
    lc = ctx.shape[1]
    depth = ada_w.shape[0]
    assert d == D_MODEL and lc % TT == 0 and l % TT == 0 and l % GRID_W == 0 and (nb * (lc + l)) % MOE_TILE == 0
    ta = lc + l
    nct = lc // TT
    x_all = jnp.concatenate([ctx, x], axis=1)

    mod_rows = -(-(nb + 1) // SUBLANES) * SUBLANES
    cc = jnp.zeros((mod_rows, d), F32).at[:nb].set(c).at[nb].set(c_ctx)
    mod = ada_mod(cc, ada_w, ada_b)

    cos_t, sin_t = _rope_tables(l, lc)
    bd = _block_diag_ones()
    rwt = router_w.T

    for li in range(depth):
        mod3 = mod[li].reshape(mod_rows, 1, N_MOD * d)
        wl = w_in[li].astype(BF16)
        if li == 0:
            h3 = lnmod(x_all, mod3, nct, 0, 1)
        h = h3.reshape(nb * ta, d)
        p_rwkv = matmul(h, wl[:, :OFF_HYENA]).reshape(nb, ta, -1)
        p_hs = matmul(h, wl[:, OFF_HYENA:OFF_ATTN]).reshape(nb, ta, -1)
        p_attn = matmul(h, wl[:, OFF_ATTN:OFF_GATE]).reshape(nb, ta, -1)

        r, kk, w0, w1, k0, k1, b0, b1, v, g, bon = rwkv_prep(
            p_rwkv, nct, rwkv_mu[li], rwkv_w0[li], rwkv_w_up[li], rwkv_a0[li], rwkv_a_up[li], rwkv_g_up[li],
            rwkv_k_k[li], rwkv_k_a[li], rwkv_r_k[li], bd)
        yf, yb = rwkv_scan(r, kk, v, w0, w1, k0, k1, b0, b1, lc)
        ya = rwkv_out(yf, yb, g, bon, rwkv_lnx_g[li], rwkv_lnx_b[li], bd)

        q, kx, vx = attn_prep(p_attn, cos_t, sin_t, attn_q_norm[li], attn_k_norm[li], bd)
        yd = attention(q, kx, vx, lc)

        x0, u, ycv = hs_pre(p_hs, nct, hyena_conv[li], sconv_w[li])
        fargs = (hyena_w1[li], hyena_b1[li], hyena_freq1[li], hyena_w2[li], hyena_b2[li], hyena_freq2[li],
                 hyena_w3[li])
        yconv_ctx = hyena_conv_seg(u[:, :lc], fargs) if li < depth - 1 else jnp.zeros((nb, lc, BRANCH_W), F32)
        yconv = jnp.concatenate([yconv_ctx, hyena_conv_seg(u[:, lc:], fargs)], axis=1)

        x1, hf, gates_t = merge(ya, x0, u, yconv, ycv, yd, h3, x_all, mod3, nct, hyena_skip[li], wl[:, OFF_GATE:],
                                w_branch[li].astype(BF16), w_out[li].astype(BF16), ln1_g[li], ln1_b[li],
                                rwt, router_bias)
        gates = jnp.pad(gates_t.T, ((0, 0), (0, LANES - ROUTE_ROWS)))
        hs, gs, pt, meta = moe_sort(hf.reshape(nb * ta, d), gates_t, gates)
        ysb = moe_group(hs, gs, meta[:, 0, :META_LANES].reshape(-1), exp_w1[li].astype(BF16),
                        exp_w3[li].astype(BF16), exp_w2[li].astype(BF16))
        if li == depth - 1:
            return ln2(x1, pt, ysb, mod3, nct, ln2_g[li], ln2_b[li], None)
        x_all, h3 = ln2(x1, pt, ysb, mod3, nct, ln2_g[li], ln2_b[li], mod[li + 1].reshape(mod_rows, 1, N_MOD * d))


def hyena_conv_seg(u_seg, fargs):
    ktab = hyena_filter_table(u_seg.shape[1], *fargs)
    return hyena_conv(u_seg, ktab)
```

```python
import functools
import math

import numpy as np
import jax
import jax.numpy as jnp
from jax import lax
from jax.experimental import pallas as pl
from jax.experimental.pallas import tpu as pltpu

F32 = jnp.float32
BF16 = jnp.bfloat16
HI = lax.Precision.HIGHEST

D_MODEL = 1024
GRID_W = 64
BRANCH_W = 256
HEAD_DIM = 64
N_BRANCHES = 4
N_MOD = 6
RWKV_HEADS = 4
RWKV_COLS = 1024
RWKV_GN_EPS = 64e-5
HYENA_COLS = 768
HYENA_EMB = 33
HYENA_FAST_DECAY = 0.3
HYENA_SLOW_DECAY = 1.5
HYENA_TARGET = 1e-2
SCONV_COLS = 768
Q_W = 256
KV_W = 128
ATTN_COLS = 512
ROPE_THETA = 10000.0
RMS_EPS = 1e-6
OFF_HYENA = RWKV_COLS
OFF_SCONV = OFF_HYENA + HYENA_COLS
OFF_ATTN = OFF_SCONV + SCONV_COLS
OFF_GATE = OFF_ATTN + ATTN_COLS
N_EXPERTS = 16
N_GROUPS = 4
EXPERTS_PER_GROUP = 4
D_EXPERT = 512
LN_EPS = 1e-6

SUBLANES = 8
LANES = 128
TT = 256
SCAN_BLK = LANES
HY_BLK = 256
VMEM_LIMIT = 56 * 1024 * 1024


def _cparams(sem, vmem=None):
    return pltpu.CompilerParams(dimension_semantics=sem, vmem_limit_bytes=vmem)


def _ln(xf):
    mu = jnp.mean(xf, -1, keepdims=True)
    xc = xf - mu
    var = jnp.mean(xc * xc, -1, keepdims=True)
    return xc * lax.rsqrt(var + LN_EPS)


def _head_sums(x, ones_bd):
    ones16 = ones_bd.astype(BF16)
    hi = x.astype(BF16)
    lo = (x - hi.astype(F32)).astype(BF16)
    return jnp.dot(hi, ones16, preferred_element_type=F32) + jnp.dot(lo, ones16, preferred_element_type=F32)


def _pick_tile(n, cands):
    for c in cands:
        if n % c == 0:
            return c
    raise ValueError(f"no tile for {n}")


def _ada_kernel(c_ref, w_ref, b_ref, o_ref):
    c = c_ref[...]
    a = c * jax.nn.sigmoid(c)
    o_ref[0] = jnp.dot(a, w_ref[0], precision=HI, preferred_element_type=F32) + b_ref[0]


def ada_mod(cc, ada_w, ada_b):
    depth, d, n = ada_w.shape
    rows = cc.shape[0]
    return pl.pallas_call(
        _ada_kernel,
        out_shape=jax.ShapeDtypeStruct((depth, rows, n), F32),
        grid=(depth, n // d),
        in_specs=[pl.BlockSpec((rows, d), lambda l, j: (0, 0)),
                  pl.BlockSpec((1, d, d), lambda l, j: (l, 0, j)),
                  pl.BlockSpec((1, 1, d), lambda l, j: (l, 0, j))],
        out_specs=pl.BlockSpec((1, rows, d), lambda l, j: (l, 0, j)),
        compiler_params=_cparams(("parallel", "parallel"), VMEM_LIMIT),
        name="ada_mod",
    )(cc, ada_w, ada_b.reshape(depth, 1, n))


def _lnmod_kernel(x_ref, sh_ref, sc_ref, o_ref):
    h = _ln(x_ref[0]) * (1.0 + sc_ref[0]) + sh_ref[0]
    o_ref[0] = h.astype(BF16)


def _mod_spec(col, nct, nb):
    return pl.BlockSpec((1, 1, D_MODEL), lambda b, t: (jnp.where(t < nct, nb, b), 0, col))


def lnmod(x_all, mod3, nct, col_shift, col_scale):
    nb, ta, d = x_all.shape
    return pl.pallas_call(
        _lnmod_kernel,
        out_shape=jax.ShapeDtypeStruct((nb, ta, d), BF16),
        grid=(nb, ta // TT),
        in_specs=[pl.BlockSpec((1, TT, d), lambda b, t: (b, t, 0)),
                  _mod_spec(col_shift, nct, nb), _mod_spec(col_scale, nct, nb)],
        out_specs=pl.BlockSpec((1, TT, d), lambda b, t: (b, t, 0)),
        compiler_params=_cparams(("parallel", "parallel")),
        name="lnmod",
    )(x_all, mod3, mod3)


def _mm_kernel(a_ref, b_ref, o_ref):
    o_ref[...] = jnp.dot(a_ref[...], b_ref[...], preferred_element_type=F32).astype(o_ref.dtype)


def matmul(a, b, out_dtype=F32):
    m, k = a.shape
    _, n = b.shape
    tm = _pick_tile(m, (1024, 512, 256))
    tn = _pick_tile(n, (1024, 512, 256))
    return pl.pallas_call(
        _mm_kernel,
        out_shape=jax.ShapeDtypeStruct((m, n), out_dtype),
        grid=(m // tm, n // tn),
        in_specs=[pl.BlockSpec((tm, k), lambda i, j: (i, 0)),
                  pl.BlockSpec((k, tn), lambda i, j: (0, j))],
        out_specs=pl.BlockSpec((tm, tn), lambda i, j: (i, j)),
        compiler_params=_cparams(("parallel", "parallel"), VMEM_LIMIT),
        name="matmul",
    )(a, b)


def _halo_specs(width, ta):
    nblk8 = ta // SUBLANES
    per = TT // SUBLANES
    cur = pl.BlockSpec((1, TT, width), lambda b, t: (b, t, 0))
    prev = pl.BlockSpec((1, SUBLANES, width), lambda b, t: (b, jnp.maximum(t * per - 1, 0), 0))
    nxt = pl.BlockSpec((1, SUBLANES, width), lambda b, t: (b, jnp.minimum((t + 1) * per, nblk8 - 1), 0))
    return [cur, prev, nxt]


def _neighbours(cur, prev_ref, next_ref, nct, nt):
    t = pl.program_id(1)
    seg_start = jnp.logical_or(t == 0, t == nct)
    seg_end = jnp.logical_or(t == nct - 1, t == nt - 1)
    prev_row = prev_ref[0][SUBLANES - 1:SUBLANES, :] * jnp.where(seg_start, 0.0, 1.0)
    next_row = next_ref[0][0:1, :] * jnp.where(seg_end, 0.0, 1.0)
    row = lax.broadcasted_iota(jnp.int32, (TT, 1), 0)
    xm1 = jnp.where(row == 0, prev_row, pltpu.roll(cur, 1, axis=0))
    xp1 = jnp.where(row == TT - 1, next_row, pltpu.roll(cur, TT - 1, axis=0))
    return xm1, xp1


def _rwkv_prep_kernel(nct, nt, cur_ref, prev_ref, next_ref, mu_ref, w0_ref, wup_ref, a0_ref, aup_ref,
                      gup_ref, kk_ref, ka_ref, rk_ref, bd_ref,
                      r_o, kk_o, w0_o, w1_o, k0_o, k1_o, b0_o, b1_o, v_o, g_o, bon_o):
    cur = cur_ref[0]
    xm1, xp1 = _neighbours(cur, prev_ref, next_ref, nct, nt)
    p = cur + mu_ref[...] * (0.5 * (xm1 + xp1) - cur)
    c = BRANCH_W
    r, k, v = p[:, 0:c], p[:, c:2 * c], p[:, 2 * c:3 * c]
    wd = p[:, 3 * c:3 * c + 64]
    ad = p[:, 3 * c + 64:3 * c + 128]
    gd = p[:, 3 * c + 128:3 * c + 256]
    bd = bd_ref[...]
    kk = k * kk_ref[...]
    ss = _head_sums(kk * kk, bd)
    kkn = kk * lax.rsqrt(jnp.maximum(ss, 1e-24))
    twd = jnp.tanh(wd)
    ka = ka_ref[...]
    kdirs = []
    w_outs, k_outs, b_outs = (w0_o, w1_o), (k0_o, k1_o), (b0_o, b1_o)
    for d in range(2):
        wlog = w0_ref[d:d + 1, :] + jnp.dot(twd, wup_ref[d], precision=HI, preferred_element_type=F32)
        decay = -math.exp(-0.5) * jax.nn.sigmoid(wlog)
        a = jax.nn.sigmoid(a0_ref[d:d + 1, :] + jnp.dot(ad, aup_ref[d], precision=HI, preferred_element_type=F32))
        kdir = k * (1.0 + (a - 1.0) * ka)
        bdir = kkn * a
        kdirs.append(kdir)
        w_outs[d][0] = decay
        k_outs[d][0] = kdir
        b_outs[d][0] = bdir
    r_o[0] = r
    kk_o[0] = kkn
    v_o[0] = v
    g_o[0] = jnp.dot(jax.nn.sigmoid(gd), gup_ref[...], precision=HI, preferred_element_type=F32)
    rkk = r * rk_ref[...] * (kdirs[0] + kdirs[1])
    bon_o[0] = _head_sums(rkk, bd) * v


def rwkv_prep(p_rwkv, nct, mu, w0, w_up, a0, a_up, g_up, k_k, k_a, r_k, bd):
    nb, ta, _ = p_rwkv.shape
    nt = ta // TT
    c = BRANCH_W
    nat = jax.ShapeDtypeStruct((nb, ta, c), F32)
    nat_spec = pl.BlockSpec((1, TT, c), lambda b, t: (b, t, 0))

    def full(a):
        nd = a.ndim
        return pl.BlockSpec(a.shape, lambda b, t: (0,) * nd)

    consts = [mu.reshape(1, -1), w0, w_up, a0, a_up, g_up, k_k.reshape(1, -1), k_a.reshape(1, -1),
              r_k.reshape(1, -1), bd]
    return pl.pallas_call(
        functools.partial(_rwkv_prep_kernel, nct, nt),
        out_shape=[nat] * 11,
        grid=(nb, nt),
        in_specs=_halo_specs(RWKV_COLS, ta) + [full(a) for a in consts],
        out_specs=[nat_spec] * 11,
        compiler_params=_cparams(("parallel", "parallel"), VMEM_LIMIT),
        name="rwkv_prep",
    )(p_rwkv, p_rwkv, p_rwkv, *consts)


CHUNK = 32
PREP_NB = 2


def _chunk_scan_rows(x, reverse):
    pos = lax.broadcasted_iota(jnp.int32, x.shape, 0) % CHUNK
    step = 1
    while step < CHUNK:
        if reverse:
            x = x + jnp.where(pos < CHUNK - step, pltpu.roll(x, x.shape[0] - step, axis=0), 0.0)
        else:
            x = x + jnp.where(pos >= step, pltpu.roll(x, step, axis=0), 0.0)
        step *= 2
    return x


def _chunk_prep_kernel(r_ref, kk_ref, v_ref, lw0, k0, b0, lw1, k1, b1,
                       a0_o, bm0_o, rp0_o, y00_o, a1_o, bm1_o, rp1_o, y01_o):
    blk = SCAN_BLK
    npair = BRANCH_W // LANES
    nchunk = blk // CHUNK
    ti = lax.broadcasted_iota(jnp.int32, (blk, blk), 0)
    si = lax.broadcasted_iota(jnp.int32, (blk, blk), 1)
    same = (ti // CHUNK) == (si // CHUNK)
    eye = ti == si
    bd64 = (ti // HEAD_DIM) == (si // HEAD_DIM)
    head0 = si < HEAD_DIM
    lane_half = lax.broadcasted_iota(jnp.int32, (HEAD_DIM, LANES), 1)
    eyef = jnp.where(eye, 1.0, 0.0)
    dot = lambda x, y: jnp.dot(x, y, preferred_element_type=F32)
    lo = lambda x: x.astype(BF16)
    split = lambda x: jnp.concatenate([jnp.where(head0, x, 0.0), jnp.where(head0, 0.0, x)], axis=0)
    cat = lambda ms: lo(jnp.concatenate(ms, axis=1))

    probs = []
    for bi, d in [(bi, d) for bi in range(PREP_NB) for d in range(2)]:
        lw_ref, k_ref, b_ref = ((lw0, k0, b0), (lw1, k1, b1))[d]
        reverse = d == 1
        r, kk, v = r_ref[bi], kk_ref[bi], v_ref[bi]
        lw, k, b = lw_ref[bi], k_ref[bi], b_ref[bi]
        lg = _chunk_scan_rows(lw, reverse)
        lg_end = lg + _chunk_scan_rows(lw, not reverse) - lw
        g, gi, g_end = jnp.exp(lg), jnp.exp(-lg), jnp.exp(lg_end)
        to_end = jnp.exp(lg_end - lg)
        arrs = (kk * jnp.exp(lg - lw), b * gi, k * gi, r * g, v, k * to_end, b * to_end, g_end)
        incl = jnp.logical_and(same, si >= ti if reverse else si <= ti)
        strict = jnp.logical_and(same, si > ti if reverse else si < ti)
        for p in range(npair):
            lanes = slice(p * LANES, (p + 1) * LANES)
            probs.append(dict(bi=bi, d=d, p=p, incl=incl, strict=strict, arrs=tuple(a[:, lanes] for a in arrs)))

    for q in probs:
        pp_, q_, kt_, rt_ = q["arrs"][:4]
        rhs_g = lo(jnp.concatenate([q_, kt_], axis=0))
        lm, mm, n2, nn = [], [], [], []
        for hh in range(2):
            hm = head0 if hh == 0 else jnp.logical_not(head0)
            lhs_g = lo(jnp.concatenate([jnp.where(hm, pp_, 0.0), jnp.where(hm, rt_, 0.0)], axis=0))
            gm = lax.dot_general(lhs_g, rhs_g, (((1,), (1,)), ((), ())), preferred_element_type=F32)
            lm.append(jnp.where(q["strict"], gm[:blk, :blk], 0.0))
            mm.append(jnp.where(q["strict"], gm[:blk, blk:], 0.0))
            n2.append(jnp.where(q["incl"], gm[blk:, :blk], 0.0))
            nn.append(jnp.where(q["incl"], gm[blk:, blk:], 0.0))
        q["pw"], q["tm"] = lm, [eyef - lm[0], eyef - lm[1]]
        q["m_cat"], q["n2_cat"], q["nn_cat"] = cat(mm), cat(n2), cat(nn)
    for _ in range(CHUNK.bit_length() - 2):
        for q in probs:
            pwl = [lo(x) for x in q["pw"]]
            q["pw"] = [dot(x, x) for x in pwl]
        for q in probs:
            q["tm"] = [dot(lo(t), lo(eyef + x)) for t, x in zip(q["tm"], q["pw"])]
    for q in probs:
        q["t_cat"] = cat(q["tm"])
        q["v_st"] = lo(split(q["arrs"][4]))
        q["pp"] = dot(q["t_cat"], lo(split(q["arrs"][0])))
        q["mv"] = dot(q["m_cat"], q["v_st"])
    for q in probs:
        q["w2"] = dot(q["t_cat"], lo(split(q["mv"])))
        q["rp"] = q["arrs"][3] - dot(q["n2_cat"], lo(split(q["pp"])))
    for q in probs:
        q["y0"] = dot(q["nn_cat"], q["v_st"]) - dot(q["n2_cat"], lo(split(q["w2"])))
    in_chunk = [si // CHUNK == c for c in range(nchunk)]
    for q in probs:
        ppt, vt, w2t = q["pp"].T, q["arrs"][4].T, q["w2"].T
        kg_, qg_ = q["arrs"][5], q["arrs"][6]
        lhs_a = jnp.concatenate([jnp.where(cm, ppt, 0.0) for cm in in_chunk], axis=0)
        q["pq"] = dot(lo(lhs_a), lo(qg_))
        lhs_b = jnp.concatenate([jnp.concatenate([jnp.where(cm, vt, 0.0), jnp.where(cm, -w2t, 0.0)], axis=1)
                                 for cm in in_chunk], axis=0)
        q["bf"] = dot(lo(lhs_b), lo(jnp.concatenate([kg_, qg_], axis=0)))
    outs = ((a0_o, bm0_o, rp0_o, y00_o), (a1_o, bm1_o, rp1_o, y01_o))
    for q in probs:
        a_o, bm_o = outs[q["d"]][:2]
        g_end_p = q["arrs"][7]
        for c in range(nchunk):
            pq_c = q["pq"][c * blk:(c + 1) * blk]
            a_o[q["bi"], 0, c, q["p"]] = (jnp.where(eye, g_end_p[c * CHUNK:c * CHUNK + 1], 0.0)
                                          - jnp.where(bd64, pq_c, 0.0)).astype(BF16)
            bm_o[q["bi"], 0, c, q["p"]] = jnp.where(lane_half < HEAD_DIM, q["bf"][c * blk:c * blk + HEAD_DIM],
                                                    q["bf"][c * blk + HEAD_DIM:(c + 1) * blk])
    for bi in range(PREP_NB):
        for d in range(2):
            rp_o, y0_o = outs[d][2:]
            rp_o[bi] = jnp.concatenate([q["rp"] for q in probs if q["d"] == d and q["bi"] == bi], axis=1)
            y0_o[bi] = jnp.concatenate([q["y0"] for q in probs if q["d"] == d and q["bi"] == bi], axis=1)


def chunk_prep(r, kk, v, lw0, lw1, k0, k1, b0, b1):
    nb, ta, c = r.shape
    nblk = ta // SCAN_BLK
    npair = c // LANES
    nchunk = SCAN_BLK // CHUNK
    pb = PREP_NB
    assert nb % pb == 0
    nat = pl.BlockSpec((pb, SCAN_BLK, c), lambda b, s: (b, s, 0))
    a_shape = jax.ShapeDtypeStruct((nb, nblk, nchunk, npair, LANES, LANES), BF16)
    bm_shape = jax.ShapeDtypeStruct((nb, nblk, nchunk, npair, HEAD_DIM, LANES), F32)
    nat_shape = jax.ShapeDtypeStruct((nb, ta, c), F32)
    a_spec = pl.BlockSpec((pb, 1, nchunk, npair, LANES, LANES), lambda b, s: (b, s, 0, 0, 0, 0))
    bm_spec = pl.BlockSpec((pb, 1, nchunk, npair, HEAD_DIM, LANES), lambda b, s: (b, s, 0, 0, 0, 0))
    return pl.pallas_call(
        _chunk_prep_kernel,
        out_shape=[a_shape, bm_shape, nat_shape, nat_shape] * 2,
        grid=(nb // pb, nblk),
        in_specs=[nat] * 9,
        out_specs=[a_spec, bm_spec, nat, nat] * 2,
        compiler_params=_cparams(("parallel", "parallel"), VMEM_LIMIT),
        name="chunk_prep",
    )(r, kk, v, lw0, k0, b0, lw1, k1, b1)


def _chunk_scan_kernel(nb, a0, bm0, rp0, y00, a1, bm1, rp1, y01, yf_o, yb_o, s_scr):
    step = pl.program_id(0)
    npair = BRANCH_W // LANES
    nchunk = SCAN_BLK // CHUNK

    @pl.when(step == 0)
    def _():
        s_scr[...] = jnp.zeros_like(s_scr)

    lane = lax.broadcasted_iota(jnp.int32, (CHUNK, LANES), 1)
    refs = ((a0, bm0, rp0, y00, yf_o), (a1, bm1, rp1, y01, yb_o))
    for ci in range(nchunk):
        for d in range(2):
            a_ref, bm_ref, rp_ref, y0_ref, y_ref = refs[d]
            c = ci if d == 0 else nchunk - 1 - ci
            rows = slice(c * CHUNK, (c + 1) * CHUNK)
            for b in range(nb):
                for p in range(npair):
                    lanes = slice(p * LANES, (p + 1) * LANES)
                    s = s_scr[d, b, p]
                    rpc = rp_ref[b, rows, lanes]
                    lhs = jnp.concatenate([jnp.where(lane < HEAD_DIM, rpc, 0.0), jnp.where(lane >= HEAD_DIM, rpc, 0.0)],
                                          axis=0)
                    yh = lax.dot_general(lhs, s, (((1,), (1,)), ((), ())), preferred_element_type=F32)
                    y_ref[b, rows, lanes] = jnp.concatenate([yh[:CHUNK], yh[CHUNK:]], axis=1) + y0_ref[b, rows, lanes]
                    s_scr[d, b, p] = (jnp.dot(s.astype(BF16), a_ref[b, 0, c, p], preferred_element_type=F32)
                                      + bm_ref[b, 0, c, p])


def rwkv_scan(r, kk, v, lw0, lw1, k0, k1, b0, b1, lc):
    nb, ta, c = r.shape
    nblk = ta // SCAN_BLK
    nctb = lc // SCAN_BLK
    npair = c // LANES
    nchunk = SCAN_BLK // CHUNK
    a0, bm0, rp0, y00, a1, bm1, rp1, y01 = chunk_prep(r, kk, v, lw0, lw1, k0, k1, b0, b1)

    def fwd(s):
        return s

    def bwd(s):
        return jnp.where(s < nctb, nctb - 1 - s, nblk - 1 - (s - nctb))

    def specs(idx):
        return [pl.BlockSpec((nb, 1, nchunk, npair, LANES, LANES), lambda s: (0, idx(s), 0, 0, 0, 0)),
                pl.BlockSpec((nb, 1, nchunk, npair, HEAD_DIM, LANES), lambda s: (0, idx(s), 0, 0, 0, 0)),
                pl.BlockSpec((nb, SCAN_BLK, c), lambda s: (0, idx(s), 0)),
                pl.BlockSpec((nb, SCAN_BLK, c), lambda s: (0, idx(s), 0))]

    out = jax.ShapeDtypeStruct((nb, ta, c), F32)
    return pl.pallas_call(
        functools.partial(_chunk_scan_kernel, nb),
        out_shape=[out, out],
        grid=(nblk,),
        in_specs=specs(fwd) + specs(bwd),
        out_specs=[pl.BlockSpec((nb, SCAN_BLK, c), lambda s: (0, fwd(s), 0)),
                   pl.BlockSpec((nb, SCAN_BLK, c), lambda s: (0, bwd(s), 0))],
        scratch_shapes=[pltpu.VMEM((2, nb, npair, HEAD_DIM, LANES), F32)],
        compiler_params=_cparams(("arbitrary",), VMEM_LIMIT),
        name="chunk_scan",
    )(a0, bm0, rp0, y00, a1, bm1, rp1, y01)


def _rwkv_out_kernel(yf_ref, yb_ref, g_ref, bon_ref, lg_ref, lb_ref, bd_ref, o_ref):
    y = yf_ref[0] + yb_ref[0]
    bd = bd_ref[...]
    mu = _head_sums(y, bd) * (1.0 / HEAD_DIM)
    yc = y - mu
    var = _head_sums(yc * yc, bd) * (1.0 / HEAD_DIM)
    yn = yc * lax.rsqrt(var + RWKV_GN_EPS) * lg_ref[...] + lb_ref[...]
    o_ref[0] = ((yn + bon_ref[0]) * g_ref[0]).astype(BF16)


def rwkv_out(yf, yb, g, bon, lnx_g, lnx_b, bd):
    nb, ta, _ = yf.shape
    nat_spec = pl.BlockSpec((1, TT, BRANCH_W), lambda b, t: (b, t, 0))
    row = pl.BlockSpec((1, BRANCH_W), lambda b, t: (0, 0))
    return pl.pallas_call(
        _rwkv_out_kernel,
        out_shape=jax.ShapeDtypeStruct((nb, ta, BRANCH_W), BF16),
        grid=(nb, ta // TT),
        in_specs=[nat_spec] * 4 + [row, row, pl.BlockSpec(bd.shape, lambda b, t: (0, 0))],
        out_specs=nat_spec,
        compiler_params=_cparams(("parallel", "parallel")),
        name="rwkv_out",
    )(yf, yb, g, bon, lnx_g.reshape(1, -1), lnx_b.reshape(1, -1), bd)


def _pair_swap(x):
    lane = lax.broadcasted_iota(jnp.int32, x.shape, 1)
    n = x.shape[1]
    return jnp.where(lane % 2 == 0, pltpu.roll(x, n - 1, axis=1), pltpu.roll(x, 1, axis=1))


def _attn_prep_kernel(p_ref, cos_ref, sin_ref, qg_ref, kg_ref, bd_ref, q_o, k_o, v_o):
    p = p_ref[0]
    q, k, v = p[:, :Q_W], p[:, Q_W:Q_W + KV_W], p[:, Q_W + KV_W:]
    bd = bd_ref[...]
    cos, sin = cos_ref[...], sin_ref[...]
    qms = _head_sums(q * q, bd) * (1.0 / HEAD_DIM)
    qn = q * lax.rsqrt(qms + RMS_EPS) * qg_ref[...]
    qr = qn * cos + _pair_swap(qn) * sin
    q_o[0] = (qr * HEAD_DIM ** -0.5).astype(BF16)
    kms = _head_sums(k * k, bd[:KV_W, :KV_W]) * (1.0 / HEAD_DIM)
    kn = k * lax.rsqrt(kms + RMS_EPS) * kg_ref[...]
    kr = kn * cos[:, :KV_W] + _pair_swap(kn) * sin[:, :KV_W]
    for g in range(KV_W // HEAD_DIM):
        sl = slice(g * HEAD_DIM, (g + 1) * HEAD_DIM)
        k_o[0, g] = kr[:, sl].astype(BF16)
        v_o[0, g] = v[:, sl].astype(BF16)


def attn_prep(p_attn, cos_t, sin_t, q_norm, k_norm, bd):
    nb, ta, _ = p_attn.shape
    ng = KV_W // HEAD_DIM
    qg = jnp.tile(q_norm, Q_W // HEAD_DIM).reshape(1, -1)
    kg = jnp.tile(k_norm, ng).reshape(1, -1)
    kv_shape = jax.ShapeDtypeStruct((nb, ng, ta, HEAD_DIM), BF16)
    kv_spec = pl.BlockSpec((1, ng, TT, HEAD_DIM), lambda b, t: (b, 0, t, 0))
    return pl.pallas_call(
        _attn_prep_kernel,
        out_shape=[jax.ShapeDtypeStruct((nb, ta, Q_W), BF16), kv_shape, kv_shape],
        grid=(nb, ta // TT),
        in_specs=[pl.BlockSpec((1, TT, ATTN_COLS), lambda b, t: (b, t, 0)),
                  pl.BlockSpec((TT, Q_W), lambda b, t: (t, 0)),
                  pl.BlockSpec((TT, Q_W), lambda b, t: (t, 0)),
                  pl.BlockSpec((1, Q_W), lambda b, t: (0, 0)),
                  pl.BlockSpec((1, KV_W), lambda b, t: (0, 0)),
                  pl.BlockSpec(bd.shape, lambda b, t: (0, 0))],
        out_specs=[pl.BlockSpec((1, TT, Q_W), lambda b, t: (b, t, 0)), kv_spec, kv_spec],
        compiler_params=_cparams(("parallel", "parallel")),
        name="attn_prep",
    )(p_attn, cos_t, sin_t, qg, kg, bd)


def _attn_kernel(nct, lc, q_ref, k_ref, v_ref, o_ref):
    t = pl.program_id(1)
    ng = k_ref.shape[1]
    rep = Q_W // HEAD_DIM // ng

    def run(nk):
        outs = []
        for g in range(ng):
            kk, vv = k_ref[0, g, :nk, :], v_ref[0, g, :nk, :]
            for r in range(rep):
                h = g * rep + r
                q = q_ref[0][:, h * HEAD_DIM:(h + 1) * HEAD_DIM]
                s = lax.dot_general(q, kk, (((1,), (1,)), ((), ())), preferred_element_type=F32)
                p = jnp.exp(s - jnp.max(s, -1, keepdims=True))
                l = jnp.sum(p, -1, keepdims=True)
                o = jnp.dot(p.astype(BF16), vv, preferred_element_type=F32)
                outs.append(o / l)
        o_ref[0] = jnp.concatenate(outs, axis=1).astype(BF16)

    @pl.when(t < nct)
    def _():
        run(lc)

    @pl.when(t >= nct)
    def _():
        run(k_ref.shape[2])


def attention(q, k, v, lc):
    nb, ta, _ = q.shape
    ng = k.shape[1]
    nct = lc // TT
    qo_spec = pl.BlockSpec((1, TT, Q_W), lambda b, t: (b, t, 0))
    kv_spec = pl.BlockSpec((1, ng, ta, HEAD_DIM), lambda b, t: (b, 0, 0, 0))
    return pl.pallas_call(
        functools.partial(_attn_kernel, nct, lc),
        out_shape=jax.ShapeDtypeStruct((nb, ta, Q_W), BF16),
        grid=(nb, ta // TT),
        in_specs=[qo_spec, kv_spec, kv_spec],
        out_specs=qo_spec,
        compiler_params=_cparams(("parallel", "arbitrary"), VMEM_LIMIT),
        name="attention",
    )(q, k, v)


def _hs_pre_kernel(nct, nt, cur_ref, prev_ref, next_ref, hw_ref, sw_ref, x0_o, u_o, ycv_o):
    cur = cur_ref[0]
    xm1, xp1 = _neighbours(cur, prev_ref, next_ref, nct, nt)
    c = BRANCH_W
    hc = HYENA_COLS
    hw = hw_ref[...]
    ph = hw[0:1] * xm1[:, :hc] + hw[1:2] * cur[:, :hc] + hw[2:3] * xp1[:, :hc]
    x0_o[0] = ph[:, :c]
    u_o[0] = ph[:, c:2 * c] * ph[:, 2 * c:3 * c]
    sw = sw_ref[...]

    def cx(a):
        return a[:, hc + c:hc + 2 * c] * a[:, hc + 2 * c:hc + 3 * c]

    conv = sw[0:1] * cx(xm1) + sw[1:2] * cx(cur) + sw[2:3] * cx(xp1)
    ycv_o[0] = (cur[:, hc:hc + c] * conv).astype(BF16)


def hs_pre(p_hs, nct, hyena_conv, sconv_w):
    nb, ta, w = p_hs.shape
    nt = ta // TT
    nat = pl.BlockSpec((1, TT, BRANCH_W), lambda b, t: (b, t, 0))
    return pl.pallas_call(
        functools.partial(_hs_pre_kernel, nct, nt),
        out_shape=[jax.ShapeDtypeStruct((nb, ta, BRANCH_W), F32),
                   jax.ShapeDtypeStruct((nb, ta, BRANCH_W), F32),
                   jax.ShapeDtypeStruct((nb, ta, BRANCH_W), BF16)],
        grid=(nb, nt),
        in_specs=_halo_specs(w, ta) + [pl.BlockSpec(hyena_conv.shape, lambda b, t: (0, 0)),
                                       pl.BlockSpec(sconv_w.shape, lambda b, t: (0, 0))],
        out_specs=[nat, nat, nat],
        compiler_params=_cparams(("parallel", "parallel"), VMEM_LIMIT),
        name="hs_pre",
    )(p_hs, p_hs, p_hs, hyena_conv, sconv_w)


EMB_PAD = 40


def _filter_tables(lh):
    n = np.arange(2 * lh)
    pos = np.abs(n - (lh - 1)).astype(np.float64)
    bands = (HYENA_EMB - 1) // 2
    t = np.minimum(pos, lh - 1) / (lh - 1)
    wpos = 2.0 * math.pi * pos / lh
    f = np.linspace(1e-4, bands - 1, bands)[:, None]
    z = np.zeros((EMB_PAD, 2 * lh), np.float32)
    z[0] = t
    z[1:1 + bands] = np.cos(f * wpos[None, :])
    z[1 + bands:1 + 2 * bands] = -np.sin(f * wpos[None, :])
    max_decay = math.log(HYENA_TARGET) / HYENA_FAST_DECAY
    min_decay = math.log(HYENA_TARGET) / HYENA_SLOW_DECAY
    deltas = np.abs(np.linspace(min_decay, max_decay, BRANCH_W)).astype(np.float32)
    return z, deltas.reshape(-1, 1)


def _filter_kernel(lh, tn, z_ref, w1_ref, b1_ref, f1_ref, w2_ref, b2_ref, f2_ref, w3_ref, dl_ref, o_ref):
    z = z_ref[...]
    h1 = jnp.sin(f1_ref[...] * (jnp.dot(w1_ref[...], z, precision=HI, preferred_element_type=F32) + b1_ref[...]))
    h2 = jnp.sin(f2_ref[...] * (jnp.dot(w2_ref[...], h1, precision=HI, preferred_element_type=F32) + b2_ref[...]))
    f = jnp.dot(w3_ref[...], h2, precision=HI, preferred_element_type=F32)
    n = pl.program_id(0) * tn + lax.broadcasted_iota(jnp.int32, (1, tn), 1)
    filt = jnp.where(n >= lh - 1, f[:BRANCH_W], f[BRANCH_W:])
    win = jnp.exp(-z[0:1, :] * dl_ref[...])
    o_ref[...] = jnp.where(n == 2 * lh - 1, 0.0, filt * win)


def hyena_filter_table(lh, w1, b1, f1, w2, b2, f2, w3):
    z_np, dl_np = _filter_tables(lh)
    n2 = 2 * lh
    tn = _pick_tile(n2, (1024, 512))
    hd = w2.shape[0]
    w1t = jnp.zeros((hd, EMB_PAD), F32).at[:, :HYENA_EMB].set(w1.T)
    args = [jnp.asarray(z_np), w1t, b1.reshape(-1, 1), f1.reshape(-1, 1), w2.T, b2.reshape(-1, 1),
            f2.reshape(-1, 1), w3.T, jnp.asarray(dl_np)]

    def full(a):
        return pl.BlockSpec(a.shape, lambda j: (0, 0))

    return pl.pallas_call(
        functools.partial(_filter_kernel, lh, tn),
        out_shape=jax.ShapeDtypeStruct((BRANCH_W, n2), F32),
        grid=(n2 // tn,),
        in_specs=[pl.BlockSpec((EMB_PAD, tn), lambda j: (0, j))] + [full(a) for a in args[1:]],
        out_specs=pl.BlockSpec((BRANCH_W, tn), lambda j: (0, j)),
        compiler_params=_cparams(("parallel",)),
        name="hyena_filter",
    )(*args)


def _hyena_conv_kernel(nblk, bp, nch, k_ref, u_ref, o_ref, t_scr):
    ntile = 4 * nblk - 1
    mc = 2 * nblk - 1
    width = (ntile + 1) * LANES
    for ch in range(nch):
        big = pltpu.roll(jnp.broadcast_to(k_ref[ch], (LANES, width)), width - (LANES - 1), 1, stride=1, stride_axis=0)
        for m in range(ntile):
            t_scr[ch, m] = big[:, m * LANES:(m + 1) * LANES].astype(BF16)

    for ch in range(nch):
        for d in [0] + [s * a for a in range(1, nblk) for s in (1, -1)]:
            m0 = 2 * d + mc
            w = jnp.concatenate([jnp.concatenate([t_scr[ch, m0], t_scr[ch, m0 + 1]], axis=1),
                                 jnp.concatenate([t_scr[ch, m0 - 1], t_scr[ch, m0]], axis=1)], axis=0)
            i0, i1 = max(0, d), min(nblk, nblk + d)
            lhs = u_ref[ch, (i0 - d) * bp:(i1 - d) * bp, :].astype(BF16)
            res = jnp.dot(lhs, w, preferred_element_type=F32)
            if d == 0:
                o_ref[ch] = res
            else:
                o_ref[ch, i0 * bp:i1 * bp, :] += res


def hyena_conv(u, ktab):
    nb, l, c = u.shape
    nblk = l // HY_BLK
    bp = -(-nb // SUBLANES) * SUBLANES
    nch = SUBLANES if nblk == 1 else 1
    ut = jnp.transpose(u.reshape(nb, nblk, HY_BLK, c), (3, 1, 0, 2))
    if bp != nb:
        ut = jnp.pad(ut, ((0, 0), (0, 0), (0, bp - nb), (0, 0)))
    ut = ut.reshape(c, nblk * bp, HY_BLK)
    k3 = ktab.reshape(c, 1, 4 * nblk * LANES)
    out = pl.pallas_call(
        functools.partial(_hyena_conv_kernel, nblk, bp, nch),
        out_shape=jax.ShapeDtypeStruct((c, nblk * bp, HY_BLK), F32),
        grid=(c // nch,),
        in_specs=[pl.BlockSpec((nch, 1, 4 * nblk * LANES), lambda ch: (ch, 0, 0)),
                  pl.BlockSpec((nch, nblk * bp, HY_BLK), lambda ch: (ch, 0, 0))],
        out_specs=pl.BlockSpec((nch, nblk * bp, HY_BLK), lambda ch: (ch, 0, 0)),
        scratch_shapes=[pltpu.VMEM((nch, 4 * nblk - 1, LANES, LANES), BF16)],
        compiler_params=_cparams(("parallel",)),
        name="hyena_conv",
    )(k3, ut)
    out = out.reshape(c, nblk, bp, HY_BLK)[:, :, :nb]
    return jnp.transpose(out, (2, 1, 3, 0)).reshape(nb, l, c)


def _route(logits, bias):
    s = jax.nn.sigmoid(logits)
    sel = s + bias
    srow = [s[e:e + 1] for e in range(N_EXPERTS)]
    row = [sel[e:e + 1] for e in range(N_EXPERTS)]
    best, gi = None, None
    for g in range(N_GROUPS):
        a, b, c, d = row[4 * g:4 * g + 4]
        hi1, lo1, hi2, lo2 = jnp.maximum(a, b), jnp.minimum(a, b), jnp.maximum(c, d), jnp.minimum(c, d)
        score = jnp.maximum(hi1, hi2) + jnp.maximum(jnp.minimum(hi1, hi2), jnp.maximum(lo1, lo2))
        if g == 0:
            best, gi = score, jnp.zeros(score.shape, jnp.int32)
        else:
            better = score > best
            gi = jnp.where(better, g, gi)
            best = jnp.where(better, score, best)
    neg = -jnp.inf
    msel = [jnp.where(gi == e // EXPERTS_PER_GROUP, row[e], neg) for e in range(N_EXPERTS)]

    def arg_first_max(vals):
        bv, bi = vals[0], jnp.zeros(vals[0].shape, jnp.int32)
        for e in range(1, N_EXPERTS):
            better = vals[e] > bv
            bi = jnp.where(better, e, bi)
            bv = jnp.where(better, vals[e], bv)
        return bi

    i1 = arg_first_max(msel)
    i2 = arg_first_max([jnp.where(i1 == e, neg, msel[e]) for e in range(N_EXPERTS)])
    w1 = sum(jnp.where(i1 == e, srow[e], 0.0) for e in range(N_EXPERTS))
    w2 = sum(jnp.where(i2 == e, srow[e], 0.0) for e in range(N_EXPERTS))
    den = w1 + w2
    g1, g2 = w1 / den, w2 / den
    rows = [jnp.where(i1 == e, g1, 0.0) + jnp.where(i2 == e, g2, 0.0) for e in range(N_EXPERTS)]
    rows.append(gi.astype(F32))
    rows.extend([jnp.zeros_like(g1)] * (ROUTE_ROWS - len(rows)))
    return jnp.concatenate(rows, axis=0)


MERGE_NB = 2


def _merge_kernel(ya_ref, x0_ref, u_ref, yc_ref, ycv_ref, yd_ref, h_ref, x_ref, *rest):
    mods, rest = rest[:3 * MERGE_NB], rest[3 * MERGE_NB:]
    skip_ref, wg_ref, wb_ref, wo_ref, g1_ref, b1_ref, rwt_ref, rb_ref = rest[:8]
    x1_o, hf_o = rest[8:10]
    gates_o = rest[10:]
    rows = MERGE_NB * TT
    flat = lambda ref: ref[...].reshape(rows, ref.shape[-1])
    yb = (flat(x0_ref) * (flat(yc_ref) + flat(u_ref) * skip_ref[...])).astype(BF16)
    ys = (flat(ya_ref), yb, flat(ycv_ref), flat(yd_ref))
    h = flat(h_ref)
    merged = None
    for n in range(N_BRANCHES):
        gate = jax.nn.sigmoid(jnp.dot(h, wg_ref[:, n * D_MODEL:(n + 1) * D_MODEL], preferred_element_type=F32))
        term = gate * jnp.dot(ys[n], wb_ref[n], preferred_element_type=F32)
        merged = term if merged is None else merged + term
    out = jnp.dot(merged.astype(BF16), wo_ref[...], preferred_element_type=F32)
    alpha = (2 * 2) ** 0.25
    for k in range(MERGE_NB):
        ga_ref, shf_ref, scf_ref = mods[3 * k:3 * k + 3]
        x1 = _ln(alpha * x_ref[k] + ga_ref[0] * out[k * TT:(k + 1) * TT]) * g1_ref[...] + b1_ref[...]
        hf = _ln(x1) * (1.0 + scf_ref[0]) + shf_ref[0]
        x1_o[k] = x1
        hf_o[k] = hf.astype(BF16)
        logits = lax.dot_general(rwt_ref[...], hf, (((1,), (1,)), ((), ())), precision=HI,
                                 preferred_element_type=F32)
        gates_o[k][...] = _route(logits, rb_ref[...])


def merge(ya, x0, u, yconv, ycv, yd, h, x_all, mod3, nct, skip, wg, wb, wo, ln_g, ln_b, rwt, rbias):
    nb, ta, d = x_all.shape
    nt = ta // TT
    mb = MERGE_NB
    assert nb % mb == 0
    nat = pl.BlockSpec((mb, TT, BRANCH_W), lambda b, t: (b, t, 0))
    wide = pl.BlockSpec((mb, TT, d), lambda b, t: (b, t, 0))

    def full(a):
        nd = a.ndim
        return pl.BlockSpec(a.shape, lambda b, t: (0,) * nd)

    def mod_spec(col, k):
        return pl.BlockSpec((1, 1, D_MODEL), lambda b, t: (jnp.where(t < nct, nb, b * mb + k), 0, col))

    mod_specs = [mod_spec(col, k) for k in range(mb) for col in (2, 3, 4)]
    consts = [skip.reshape(1, -1), wg, wb, wo, ln_g.reshape(1, -1), ln_b.reshape(1, -1), rwt, rbias.reshape(-1, 1)]
    gate_shape = jax.ShapeDtypeStruct((ROUTE_ROWS, nb // mb * ta), F32)
    gate_spec = pl.BlockSpec((ROUTE_ROWS, TT), lambda b, t: (0, b * nt + t))
    outs = pl.pallas_call(
        _merge_kernel,
        out_shape=[jax.ShapeDtypeStruct((nb, ta, d), F32), jax.ShapeDtypeStruct((nb, ta, d), BF16)] + [gate_shape] * mb,
        grid=(nb // mb, nt),
        in_specs=[nat] * 6 + [wide, wide] + mod_specs + [full(a) for a in consts],
        out_specs=[wide, wide] + [gate_spec] * mb,
        compiler_params=_cparams(("parallel", "parallel"), VMEM_LIMIT),
        name="merge",
    )(ya, x0, u, yconv, ycv, yd, h, x_all, *([mod3] * (3 * mb)), *consts)
    gates_t = jnp.stack([g.reshape(ROUTE_ROWS, nb // mb, ta) for g in outs[2:]], axis=2).reshape(ROUTE_ROWS, nb * ta)
    return outs[0], outs[1], gates_t


MOE_TILE = 1024
MOE_ALIGN = 2 * SUBLANES
MOE_CHUNK = 18 * MOE_ALIGN
MOE_SORTED = MOE_TILE + LANES
MOE_WIN = 2 * MOE_CHUNK
MOE_NWIN = -(-MOE_TILE // MOE_WIN)
MOE_ROWS = MOE_SORTED + MOE_WIN
GID_ROW = N_EXPERTS
ROUTE_ROWS = 3 * SUBLANES
META_LANES = 2 * N_GROUPS
assert N_GROUPS * (MOE_ALIGN - 1) <= MOE_SORTED - MOE_TILE and MOE_ROWS % MOE_ALIGN == 0


def _moe_sort_kernel(gt_ref, g_ref, h_ref, up_ref, hs_o, gs_o, pt_o, meta_o):
    gid = gt_ref[GID_ROW:GID_ROW + 1, :]
    onehot = [jnp.where(gid == float(g), 1.0, 0.0) for g in range(N_GROUPS)]
    g4 = jnp.concatenate(onehot + [jnp.zeros((SUBLANES - N_GROUPS, MOE_TILE), F32)], axis=0)
    before = jnp.dot(g4.astype(BF16), up_ref[...], preferred_element_type=F32)
    lane = lax.broadcasted_iota(jnp.int32, (SUBLANES, LANES), 1)
    meta = jnp.zeros((SUBLANES, LANES), F32)
    off = jnp.zeros((1, 1), F32)
    pos = jnp.zeros((1, MOE_TILE), F32)
    for g in range(N_GROUPS):
        cnt = jnp.sum(onehot[g], axis=1, keepdims=True)
        pos = pos + onehot[g] * (before[g:g + 1] + off)
        meta = jnp.where(lane == g, off, meta)
        meta = jnp.where(lane == N_GROUPS + g, cnt, meta)
        off = off + jnp.ceil(cnt * (1.0 / MOE_ALIGN)) * MOE_ALIGN
    meta_o[0] = meta.astype(jnp.int32)
    row = lax.broadcasted_iota(jnp.int32, (MOE_SORTED, MOE_TILE), 0)
    place = jnp.where(row == pos.astype(jnp.int32), 1.0, 0.0)
    p16 = place.astype(BF16)
    hs_o[0, :MOE_SORTED, :] = jnp.dot(p16, h_ref[...], preferred_element_type=F32).astype(BF16)
    hs_o[0, MOE_SORTED:, :] = jnp.zeros((MOE_ROWS - MOE_SORTED, h_ref.shape[1]), BF16)
    gts = g_ref[...]
    hi = gts.astype(BF16)
    r1 = gts - hi.astype(F32)
    mid = r1.astype(BF16)
    low = (r1 - mid.astype(F32)).astype(BF16)
    gs_o[0, :MOE_SORTED, :] = (jnp.dot(p16, hi, preferred_element_type=F32)
                               + jnp.dot(p16, mid, preferred_element_type=F32)
                               + jnp.dot(p16, low, preferred_element_type=F32))
    gs_o[0, MOE_SORTED:, :] = jnp.zeros((MOE_ROWS - MOE_SORTED, LANES), F32)
    pt_o[...] = place.T.astype(BF16)


def moe_sort(hf, gates_t, gates):
    n, d = hf.shape
    ntile = n // MOE_TILE
    upper = jnp.asarray(np.triu(np.ones((MOE_TILE, MOE_TILE), np.float32), 1), BF16)
    return pl.pallas_call(
        _moe_sort_kernel,
        out_shape=[jax.ShapeDtypeStruct((ntile, MOE_ROWS, d), BF16),
                   jax.ShapeDtypeStruct((ntile, MOE_ROWS, LANES), F32),
                   jax.ShapeDtypeStruct((n, MOE_SORTED), BF16),
                   jax.ShapeDtypeStruct((ntile, SUBLANES, LANES), jnp.int32)],
        grid=(ntile,),
        in_specs=[pl.BlockSpec((ROUTE_ROWS, MOE_TILE), lambda i: (0, i)),
                  pl.BlockSpec((MOE_TILE, LANES), lambda i: (i, 0)),
                  pl.BlockSpec((MOE_TILE, d), lambda i: (i, 0)),
                  pl.BlockSpec((MOE_TILE, MOE_TILE), lambda i: (0, 0))],
        out_specs=[pl.BlockSpec((1, MOE_ROWS, d), lambda i: (i, 0, 0)),
                   pl.BlockSpec((1, MOE_ROWS, LANES), lambda i: (i, 0, 0)),
                   pl.BlockSpec((MOE_TILE, MOE_SORTED), lambda i: (i, 0)),
                   pl.BlockSpec((1, SUBLANES, LANES), lambda i: (i, 0, 0))],
        compiler_params=_cparams(("parallel",), VMEM_LIMIT),
        name="moe_sort",
    )(gates_t, gates, hf, upper)


def _moe_group_kernel(meta_ref, hs_ref, gs_ref, w1_ref, w3_ref, w2_ref, prev_ref, ys_o):
    g, i, w = pl.program_id(0), pl.program_id(1), pl.program_id(2)
    cnt = meta_ref[i * META_LANES + N_GROUPS + g]
    per_win = MOE_WIN // MOE_CHUNK
    todo = lax.div(cnt + (MOE_CHUNK - 1), MOE_CHUNK) - w * per_win
    lane = lax.broadcasted_iota(jnp.int32, (MOE_CHUNK, LANES), 1)

    @pl.when(jnp.logical_or(w == 0, todo > 0))
    def _():
        ys_o[...] = prev_ref[...]

        def chunk(j, carry):
            rows = pl.ds(pl.multiple_of(j * MOE_CHUNK, MOE_ALIGN), MOE_CHUNK)
            hs = hs_ref[0, rows, :]
            gs = gs_ref[0, rows, :]
            acc = ys_o[0, rows, :].astype(F32)
            for e in range(EXPERTS_PER_GROUP):
                a = jnp.dot(hs, w1_ref[e], preferred_element_type=F32)
                b = jnp.dot(hs, w3_ref[e], preferred_element_type=F32)
                act = (a * jax.nn.sigmoid(a)) * b
                gcol = jnp.sum(jnp.where(lane == g * EXPERTS_PER_GROUP + e, gs, 0.0), axis=1, keepdims=True)
                acc = acc + gcol * jnp.dot(act.astype(BF16), w2_ref[e], preferred_element_type=F32)
            ys_o[0, rows, :] = acc.astype(BF16)
            return carry

        lax.fori_loop(0, jnp.clip(todo, 0, per_win), chunk, 0)


def moe_group(hs, gs, meta, w1, w3, w2):
    ntile, _, d = hs.shape
    de = w1.shape[2]
    epg = EXPERTS_PER_GROUP

    def window(width):
        def index(g, i, w, m):
            off = m[i * META_LANES + g]
            nwin = lax.div(m[i * META_LANES + N_GROUPS + g] + (MOE_WIN - 1), MOE_WIN)
            wc = jnp.minimum(w, jnp.maximum(nwin - 1, 0))
            return i, pl.multiple_of(off + wc * MOE_WIN, MOE_ALIGN), 0
        return pl.BlockSpec((pl.Element(1), pl.Element(MOE_WIN), pl.Element(width)), index)

    grid_spec = pltpu.PrefetchScalarGridSpec(
        num_scalar_prefetch=1,
        grid=(N_GROUPS, ntile, MOE_NWIN),
        in_specs=[window(d), window(LANES),
                  pl.BlockSpec((epg, d, de), lambda g, i, w, m: (g, 0, 0)),
                  pl.BlockSpec((epg, d, de), lambda g, i, w, m: (g, 0, 0)),
                  pl.BlockSpec((epg, de, d), lambda g, i, w, m: (g, 0, 0)),
                  window(d)],
        out_specs=window(d),
    )
    carried = jnp.zeros((ntile, MOE_ROWS, d), BF16)
    return pl.pallas_call(
        _moe_group_kernel,
        out_shape=jax.ShapeDtypeStruct((ntile, MOE_ROWS, d), BF16),
        grid_spec=grid_spec,
        input_output_aliases={6: 0},
        compiler_params=_cparams(("arbitrary", "arbitrary", "arbitrary"), VMEM_LIMIT),
        name="moe_group",
    )(meta, hs, gs, w1, w3, w2, carried)


def _ln2_kernel(emit_next, x_ref, pt_ref, ys_ref, gf_ref, g_ref, b_ref, *rest):
    alpha = (2 * 2) ** 0.25
    f = jnp.dot(pt_ref[...], ys_ref[0], preferred_element_type=F32)
    x2 = _ln(alpha * x_ref[0] + gf_ref[0] * f) * g_ref[...] + b_ref[...]
    if emit_next:
        sh_ref, sc_ref, o_ref, h_o = rest
        h_o[0] = (_ln(x2) * (1.0 + sc_ref[0]) + sh_ref[0]).astype(BF16)
    else:
        o_ref, = rest
    o_ref[0] = x2


def ln2(x1, pt, ysb, mod3, nct, ln_g, ln_b, mod3_next):
    nb, ta, d = x1.shape
    nt = ta // TT
    per = MOE_TILE // TT
    wide = pl.BlockSpec((1, TT, d), lambda b, t: (b, t, 0))
    row = pl.BlockSpec((1, d), lambda b, t: (0, 0))
    in_specs = [wide, pl.BlockSpec((TT, MOE_SORTED), lambda b, t: (b * nt + t, 0)),
                pl.BlockSpec((1, MOE_SORTED, d), lambda b, t: ((b * nt + t) // per, 0, 0)),
                _mod_spec(5, nct, nb), row, row]
    args = [x1, pt, ysb, mod3, ln_g.reshape(1, -1), ln_b.reshape(1, -1)]
    if mod3_next is None:
        out_shape = jax.ShapeDtypeStruct((nb, ta - nct * TT, d), F32)
        out_specs = pl.BlockSpec((1, TT, d), lambda b, t: (b, jnp.maximum(t - nct, 0), 0))
    else:
        in_specs += [_mod_spec(0, nct, nb), _mod_spec(1, nct, nb)]
        args += [mod3_next, mod3_next]
        out_shape = [jax.ShapeDtypeStruct((nb, ta, d), F32), jax.ShapeDtypeStruct((nb, ta, d), BF16)]
        out_specs = [wide, wide]
    return pl.pallas_call(
        functools.partial(_ln2_kernel, mod3_next is not None),
        out_shape=out_shape,
        grid=(nb, nt),
        in_specs=in_specs,
        out_specs=out_specs,
        compiler_params=_cparams(("parallel", "arbitrary"), VMEM_LIMIT),
        name="ln2",
    )(*args)


def _rope_tables(l, lc):
    half = HEAD_DIM // 2
    inv = ROPE_THETA ** (-np.arange(0, half, 2, dtype=np.float64) / half)
    t = np.arange(l)
    rows, cols = t // GRID_W, t % GRID_W
    ang = np.concatenate([rows[:, None] * inv, cols[:, None] * inv], -1)
    ang = np.concatenate([np.zeros((lc, half)), ang], 0)
    cos = np.repeat(np.cos(ang), 2, axis=1)
    sin = np.repeat(np.sin(ang), 2, axis=1)
    sin[:, 0::2] *= -1.0
    reps = Q_W // HEAD_DIM
    return (jnp.asarray(np.tile(cos, (1, reps)), F32), jnp.asarray(np.tile(sin, (1, reps)), F32))


def _block_diag_ones():
    i = np.arange(BRANCH_W) // HEAD_DIM
    return jnp.asarray((i[:, None] == i[None, :]).astype(np.float32))


def kernel(x, c, ctx, c_ctx, ada_w, ada_b, w_in, rwkv_mu, rwkv_w0, rwkv_w_up, rwkv_a0, rwkv_a_up, rwkv_g_up, rwkv_k_k, rwkv_k_a, rwkv_r_k, rwkv_lnx_g, rwkv_lnx_b, hyena_conv, hyena_w1, hyena_b1, hyena_freq1, hyena_w2, hyena_b2, hyena_freq2, hyena_w3, hyena_skip, sconv_w, attn_q_norm, attn_k_norm, w_branch, w_out, ln1_g, ln1_b, ln2_g, ln2_b, router_w, router_bias, exp_w1, exp_w3, exp_w2):
    nb, l, d = x.shape
    lc = ctx.shape[1]
    depth = ada_w.shape[0]
    assert d == D_MODEL and lc % TT == 0 and l % TT == 0 and l % GRID_W == 0 and (nb * (lc + l)) % MOE_TILE == 0
    ta = lc + l
    nct = lc // TT
    x_all = jnp.concatenate([ctx, x], axis=1)

    mod_rows = -(-(nb + 1) // SUBLANES) * SUBLANES
    cc = jnp.zeros((mod_rows, d), F32).at[:nb].set(c).at[nb].set(c_ctx)
    mod = ada_mod(cc, ada_w, ada_b)

    cos_t, sin_t = _rope_tables(l, lc)
    bd = _block_diag_ones()
    rwt = router_w.T

    for li in range(depth):
        mod3 = mod[li].reshape(mod_rows, 1, N_MOD * d)
        wl = w_in[li].astype(BF16)
        if li == 0:
            h3 = lnmod(x_all, mod3, nct, 0, 1)
        h = h3.reshape(nb * ta, d)
        p_rwkv = matmul(h, wl[:, :OFF_HYENA]).reshape(nb, ta, -1)
        p_hs = matmul(h, wl[:, OFF_HYENA:OFF_ATTN]).reshape(nb, ta, -1)
        p_attn = matmul(h, wl[:, OFF_ATTN:OFF_GATE]).reshape(nb, ta, -1)

        r, kk, w0, w1, k0, k1, b0, b1, v, g, bon = rwkv_prep(
            p_rwkv, nct, rwkv_mu[li], rwkv_w0[li], rwkv_w_up[li], rwkv_a0[li], rwkv_a_up[li], rwkv_g_up[li],
            rwkv_k_k[li], rwkv_k_a[li], rwkv_r_k[li], bd)
        yf, yb = rwkv_scan(r, kk, v, w0, w1, k0, k1, b0, b1, lc)
        ya = rwkv_out(yf, yb, g, bon, rwkv_lnx_g[li], rwkv_lnx_b[li], bd)

        q, kx, vx = attn_prep(p_attn, cos_t, sin_t, attn_q_norm[li], attn_k_norm[li], bd)
        yd = attention(q, kx, vx, lc)

        x0, u, ycv = hs_pre(p_hs, nct, hyena_conv[li], sconv_w[li])
        fargs = (hyena_w1[li], hyena_b1[li], hyena_freq1[li], hyena_w2[li], hyena_b2[li], hyena_freq2[li],
                 hyena_w3[li])
        yconv_ctx = hyena_conv_seg(u[:, :lc], fargs) if li < depth - 1 else jnp.zeros((nb, lc, BRANCH_W), F32)
        yconv = jnp.concatenate([yconv_ctx, hyena_conv_seg(u[:, lc:], fargs)], axis=1)

        x1, hf, gates_t = merge(ya, x0, u, yconv, ycv, yd, h3, x_all, mod3, nct, hyena_skip[li], wl[:, OFF_GATE:],
                                w_branch[li].astype(BF16), w_out[li].astype(BF16), ln1_g[li], ln1_b[li],
                                rwt, router_bias)
        gates = jnp.pad(gates_t.T, ((0, 0), (0, LANES - ROUTE_ROWS)))
        hs, gs, pt, meta = moe_sort(hf.reshape(nb * ta, d), gates_t, gates)
        ysb = moe_group(hs, gs, meta[:, 0, :META_LANES].reshape(-1), exp_w1[li].astype(BF16),
                        exp_w3[li].astype(BF16), exp_w2[li].astype(BF16))
        if li == depth - 1:
            return ln2(x1, pt, ysb, mod3, nct, ln2_g[li], ln2_b[li], None)
        x_all, h3 = ln2(x1, pt, ysb, mod3, nct, ln2_g[li], ln2_b[li], mod[li + 1].reshape(mod_rows, 1, N_MOD * d))


def hyena_conv_seg(u_seg, fargs):
    ktab = hyena_filter_table(u_seg.shape[1], *fargs)
    return hyena_conv(u_seg, ktab)
```

```python
import functools
import math

import numpy as np
import jax
import jax.numpy as jnp
from jax import lax
from jax.experimental import pallas as pl
from jax.experimental.pallas import tpu as pltpu

F32 = jnp.float32
BF16 = jnp.bfloat16
HI = lax.Precision.HIGHEST

D_MODEL = 1024
GRID_W = 64
BRANCH_W = 256
HEAD_DIM = 64
N_BRANCHES = 4
N_MOD = 6
RWKV_HEADS = 4
RWKV_COLS = 1024
RWKV_GN_EPS = 64e-5
HYENA_COLS = 768
HYENA_EMB = 33
HYENA_FAST_DECAY = 0.3
HYENA_SLOW_DECAY = 1.5
HYENA_TARGET = 1e-2
SCONV_COLS = 768
Q_W = 256
KV_W = 128
ATTN_COLS = 512
ROPE_THETA = 10000.0
RMS_EPS = 1e-6
OFF_HYENA = RWKV_COLS
OFF_SCONV = OFF_HYENA + HYENA_COLS
OFF_ATTN = OFF_SCONV + SCONV_COLS
OFF_GATE = OFF_ATTN + ATTN_COLS
N_EXPERTS = 16
N_GROUPS = 4
EXPERTS_PER_GROUP = 4
D_EXPERT = 512
LN_EPS = 1e-6

SUBLANES = 8
LANES = 128
TT = 256
SCAN_BLK = LANES
HY_BLK = 256
VMEM_LIMIT = 56 * 1024 * 1024


def _cparams(sem, vmem=None):
    return pltpu.CompilerParams(dimension_semantics=sem, vmem_limit_bytes=vmem)


def _ln(xf):
    mu = jnp.mean(xf, -1, keepdims=True)
    xc = xf - mu
    var = jnp.mean(xc * xc, -1, keepdims=True)
    return xc * lax.rsqrt(var + LN_EPS)


def _head_sums(x, ones_bd):
    ones16 = ones_bd.astype(BF16)
    hi = x.astype(BF16)
    lo = (x - hi.astype(F32)).astype(BF16)
    return jnp.dot(hi, ones16, preferred_element_type=F32) + jnp.dot(lo, ones16, preferred_element_type=F32)


def _pick_tile(n, cands):
    for c in cands:
        if n % c == 0:
            return c
    raise ValueError(f"no tile for {n}")


def _ada_kernel(c_ref, w_ref, b_ref, o_ref):
    c = c_ref[...]
    a = c * jax.nn.sigmoid(c)
    o_ref[0] = jnp.dot(a, w_ref[0], precision=HI, preferred_element_type=F32) + b_ref[0]


def ada_mod(cc, ada_w, ada_b):
    depth, d, n = ada_w.shape
    rows = cc.shape[0]
    return pl.pallas_call(
        _ada_kernel,
        out_shape=jax.ShapeDtypeStruct((depth, rows, n), F32),
        grid=(depth, n // d),
        in_specs=[pl.BlockSpec((rows, d), lambda l, j: (0, 0)),
                  pl.BlockSpec((1, d, d), lambda l, j: (l, 0, j)),
                  pl.BlockSpec((1, 1, d), lambda l, j: (l, 0, j))],
        out_specs=pl.BlockSpec((1, rows, d), lambda l, j: (l, 0, j)),
        compiler_params=_cparams(("parallel", "parallel"), VMEM_LIMIT),
        name="ada_mod",
    )(cc, ada_w, ada_b.reshape(depth, 1, n))


def _lnmod_kernel(x_ref, sh_ref, sc_ref, o_ref):
    h = _ln(x_ref[0]) * (1.0 + sc_ref[0]) + sh_ref[0]
    o_ref[0] = h.astype(BF16)


def _mod_spec(col, nct, nb):
    return pl.BlockSpec((1, 1, D_MODEL), lambda b, t: (jnp.where(t < nct, nb, b), 0, col))


def lnmod(x_all, mod3, nct, col_shift, col_scale):
    nb, ta, d = x_all.shape
    return pl.pallas_call(
        _lnmod_kernel,
        out_shape=jax.ShapeDtypeStruct((nb, ta, d), BF16),
        grid=(nb, ta // TT),
        in_specs=[pl.BlockSpec((1, TT, d), lambda b, t: (b, t, 0)),
                  _mod_spec(col_shift, nct, nb), _mod_spec(col_scale, nct, nb)],
        out_specs=pl.BlockSpec((1, TT, d), lambda b, t: (b, t, 0)),
        compiler_params=_cparams(("parallel", "parallel")),
        name="lnmod",
    )(x_all, mod3, mod3)


def _mm_kernel(a_ref, b_ref, o_ref):
    o_ref[...] = jnp.dot(a_ref[...], b_ref[...], preferred_element_type=F32).astype(o_ref.dtype)


def matmul(a, b, out_dtype=F32):
    m, k = a.shape
    _, n = b.shape
    tm = _pick_tile(m, (1024, 512, 256))
    tn = _pick_tile(n, (1024, 512, 256))
    return pl.pallas_call(
        _mm_kernel,
        out_shape=jax.ShapeDtypeStruct((m, n), out_dtype),
        grid=(m // tm, n // tn),
        in_specs=[pl.BlockSpec((tm, k), lambda i, j: (i, 0)),
                  pl.BlockSpec((k, tn), lambda i, j: (0, j))],
        out_specs=pl.BlockSpec((tm, tn), lambda i, j: (i, j)),
        compiler_params=_cparams(("parallel", "parallel"), VMEM_LIMIT),
        name="matmul",
    )(a, b)


def _halo_specs(width, ta):
    nblk8 = ta // SUBLANES
    per = TT // SUBLANES
    cur = pl.BlockSpec((1, TT, width), lambda b, t: (b, t, 0))
    prev = pl.BlockSpec((1, SUBLANES, width), lambda b, t: (b, jnp.maximum(t * per - 1, 0), 0))
    nxt = pl.BlockSpec((1, SUBLANES, width), lambda b, t: (b, jnp.minimum((t + 1) * per, nblk8 - 1), 0))
    return [cur, prev, nxt]


def _neighbours(cur, prev_ref, next_ref, nct, nt):
    t = pl.program_id(1)
    seg_start = jnp.logical_or(t == 0, t == nct)
    seg_end = jnp.logical_or(t == nct - 1, t == nt - 1)
    prev_row = prev_ref[0][SUBLANES - 1:SUBLANES, :] * jnp.where(seg_start, 0.0, 1.0)
    next_row = next_ref[0][0:1, :] * jnp.where(seg_end, 0.0, 1.0)
    row = lax.broadcasted_iota(jnp.int32, (TT, 1), 0)
    xm1 = jnp.where(row == 0, prev_row, pltpu.roll(cur, 1, axis=0))
    xp1 = jnp.where(row == TT - 1, next_row, pltpu.roll(cur, TT - 1, axis=0))
    return xm1, xp1


def _rwkv_prep_kernel(nct, nt, cur_ref, prev_ref, next_ref, mu_ref, w0_ref, wup_ref, a0_ref, aup_ref,
                      gup_ref, kk_ref, ka_ref, rk_ref, bd_ref,
                      r_o, kk_o, w0_o, w1_o, k0_o, k1_o, b0_o, b1_o, v_o, g_o, bon_o):
    cur = cur_ref[0]
    xm1, xp1 = _neighbours(cur, prev_ref, next_ref, nct, nt)
    p = cur + mu_ref[...] * (0.5 * (xm1 + xp1) - cur)
    c = BRANCH_W
    r, k, v = p[:, 0:c], p[:, c:2 * c], p[:, 2 * c:3 * c]
    wd = p[:, 3 * c:3 * c + 64]
    ad = p[:, 3 * c + 64:3 * c + 128]
    gd = p[:, 3 * c + 128:3 * c + 256]
    bd = bd_ref[...]
    kk = k * kk_ref[...]
    ss = _head_sums(kk * kk, bd)
    kkn = kk * lax.rsqrt(jnp.maximum(ss, 1e-24))
    twd = jnp.tanh(wd)
    ka = ka_ref[...]
    kdirs = []
    w_outs, k_outs, b_outs = (w0_o, w1_o), (k0_o, k1_o), (b0_o, b1_o)
    for d in range(2):
        wlog = w0_ref[d:d + 1, :] + jnp.dot(twd, wup_ref[d], precision=HI, preferred_element_type=F32)
        decay = -math.exp(-0.5) * jax.nn.sigmoid(wlog)
        a = jax.nn.sigmoid(a0_ref[d:d + 1, :] + jnp.dot(ad, aup_ref[d], precision=HI, preferred_element_type=F32))
        kdir = k * (1.0 + (a - 1.0) * ka)
        bdir = kkn * a
        kdirs.append(kdir)
        w_outs[d][0] = decay
        k_outs[d][0] = kdir
        b_outs[d][0] = bdir
    r_o[0] = r
    kk_o[0] = kkn
    v_o[0] = v
    g_o[0] = jnp.dot(jax.nn.sigmoid(gd), gup_ref[...], precision=HI, preferred_element_type=F32)
    rkk = r * rk_ref[...] * (kdirs[0] + kdirs[1])
    bon_o[0] = _head_sums(rkk, bd) * v


def rwkv_prep(p_rwkv, nct, mu, w0, w_up, a0, a_up, g_up, k_k, k_a, r_k, bd):
    nb, ta, _ = p_rwkv.shape
    nt = ta // TT
    c = BRANCH_W
    nat = jax.ShapeDtypeStruct((nb, ta, c), F32)
    nat_spec = pl.BlockSpec((1, TT, c), lambda b, t: (b, t, 0))

    def full(a):
        nd = a.ndim
        return pl.BlockSpec(a.shape, lambda b, t: (0,) * nd)

    consts = [mu.reshape(1, -1), w0, w_up, a0, a_up, g_up, k_k.reshape(1, -1), k_a.reshape(1, -1),
              r_k.reshape(1, -1), bd]
    return pl.pallas_call(
        functools.partial(_rwkv_prep_kernel, nct, nt),
        out_shape=[nat] * 11,
        grid=(nb, nt),
        in_specs=_halo_specs(RWKV_COLS, ta) + [full(a) for a in consts],
        out_specs=[nat_spec] * 11,
        compiler_params=_cparams(("parallel", "parallel"), VMEM_LIMIT),
        name="rwkv_prep",
    )(p_rwkv, p_rwkv, p_rwkv, *consts)


CHUNK = 32
PREP_NB = 2


def _chunk_scan_rows(x, reverse):
    pos = lax.broadcasted_iota(jnp.int32, x.shape, 0) % CHUNK
    step = 1
    while step < CHUNK:
        if reverse:
            x = x + jnp.where(pos < CHUNK - step, pltpu.roll(x, x.shape[0] - step, axis=0), 0.0)
        else:
            x = x + jnp.where(pos >= step, pltpu.roll(x, step, axis=0), 0.0)
        step *= 2
    return x


def _chunk_prep_kernel(r_ref, kk_ref, v_ref, lw0, k0, b0, lw1, k1, b1,
                       a0_o, bm0_o, rp0_o, y00_o, a1_o, bm1_o, rp1_o, y01_o):
    blk = SCAN_BLK
    npair = BRANCH_W // LANES
    nchunk = blk // CHUNK
    ti = lax.broadcasted_iota(jnp.int32, (blk, blk), 0)
    si = lax.broadcasted_iota(jnp.int32, (blk, blk), 1)
    same = (ti // CHUNK) == (si // CHUNK)
    eye = ti == si
    bd64 = (ti // HEAD_DIM) == (si // HEAD_DIM)
    head0 = si < HEAD_DIM
    lane_half = lax.broadcasted_iota(jnp.int32, (HEAD_DIM, LANES), 1)
    eyef = jnp.where(eye, 1.0, 0.0)
    dot = lambda x, y: jnp.dot(x, y, preferred_element_type=F32)
    lo = lambda x: x.astype(BF16)
    split = lambda x: jnp.concatenate([jnp.where(head0, x, 0.0), jnp.where(head0, 0.0, x)], axis=0)
    cat = lambda ms: lo(jnp.concatenate(ms, axis=1))

    probs = []
    for bi, d in [(bi, d) for bi in range(PREP_NB) for d in range(2)]:
        lw_ref, k_ref, b_ref = ((lw0, k0, b0), (lw1, k1, b1))[d]
        reverse = d == 1
        r, kk, v = r_ref[bi], kk_ref[bi], v_ref[bi]
        lw, k, b = lw_ref[bi], k_ref[bi], b_ref[bi]
        lg = _chunk_scan_rows(lw, reverse)
        lg_end = lg + _chunk_scan_rows(lw, not reverse) - lw
        g, gi, g_end = jnp.exp(lg), jnp.exp(-lg), jnp.exp(lg_end)
        to_end = jnp.exp(lg_end - lg)
        arrs = (kk * jnp.exp(lg - lw), b * gi, k * gi, r * g, v, k * to_end, b * to_end, g_end)
        incl = jnp.logical_and(same, si >= ti if reverse else si <= ti)
        strict = jnp.logical_and(same, si > ti if reverse else si < ti)
        for p in range(npair):
            lanes = slice(p * LANES, (p + 1) * LANES)
            probs.append(dict(bi=bi, d=d, p=p, incl=incl, strict=strict, arrs=tuple(a[:, lanes] for a in arrs)))

    for q in probs:
        pp_, q_, kt_, rt_ = q["arrs"][:4]
        rhs_g = lo(jnp.concatenate([q_, kt_], axis=0))
        lm, mm, n2, nn = [], [], [], []
        for hh in range(2):
            hm = head0 if hh == 0 else jnp.logical_not(head0)
            lhs_g = lo(jnp.concatenate([jnp.where(hm, pp_, 0.0), jnp.where(hm, rt_, 0.0)], axis=0))
            gm = lax.dot_general(lhs_g, rhs_g, (((1,), (1,)), ((), ())), preferred_element_type=F32)
            lm.append(jnp.where(q["strict"], gm[:blk, :blk], 0.0))
            mm.append(jnp.where(q["strict"], gm[:blk, blk:], 0.0))
            n2.append(jnp.where(q["incl"], gm[blk:, :blk], 0.0))
            nn.append(jnp.where(q["incl"], gm[blk:, blk:], 0.0))
        q["pw"], q["tm"] = lm, [eyef - lm[0], eyef - lm[1]]
        q["m_cat"], q["n2_cat"], q["nn_cat"] = cat(mm), cat(n2), cat(nn)
    for _ in range(CHUNK.bit_length() - 2):
        for q in probs:
            pwl = [lo(x) for x in q["pw"]]
            q["pw"] = [dot(x, x) for x in pwl]
        for q in probs:
            q["tm"] = [dot(lo(t), lo(eyef + x)) for t, x in zip(q["tm"], q["pw"])]
    for q in probs:
        q["t_cat"] = cat(q["tm"])
        q["v_st"] = lo(split(q["arrs"][4]))
        q["pp"] = dot(q["t_cat"], lo(split(q["arrs"][0])))
        q["mv"] = dot(q["m_cat"], q["v_st"])
    for q in probs:
        q["w2"] = dot(q["t_cat"], lo(split(q["mv"])))
        q["rp"] = q["arrs"][3] - dot(q["n2_cat"], lo(split(q["pp"])))
    for q in probs:
        q["y0"] = dot(q["nn_cat"], q["v_st"]) - dot(q["n2_cat"], lo(split(q["w2"])))
    in_chunk = [si // CHUNK == c for c in range(nchunk)]
    for q in probs:
        ppt, vt, w2t = q["pp"].T, q["arrs"][4].T, q["w2"].T
        kg_, qg_ = q["arrs"][5], q["arrs"][6]
        lhs_a = jnp.concatenate([jnp.where(cm, ppt, 0.0) for cm in in_chunk], axis=0)
        q["pq"] = dot(lo(lhs_a), lo(qg_))
        lhs_b = jnp.concatenate([jnp.concatenate([jnp.where(cm, vt, 0.0), jnp.where(cm, -w2t, 0.0)], axis=1)
                                 for cm in in_chunk], axis=0)
        q["bf"] = dot(lo(lhs_b), lo(jnp.concatenate([kg_, qg_], axis=0)))
    outs = ((a0_o, bm0_o, rp0_o, y00_o), (a1_o, bm1_o, rp1_o, y01_o))
    for q in probs:
        a_o, bm_o = outs[q["d"]][:2]
        g_end_p = q["arrs"][7]
        for c in range(nchunk):
            pq_c = q["pq"][c * blk:(c + 1) * blk]
            a_o[q["bi"], 0, c, q["p"]] = (jnp.where(eye, g_end_p[c * CHUNK:c * CHUNK + 1], 0.0)
                                          - jnp.where(bd64, pq_c, 0.0)).astype(BF16)
            bm_o[q["bi"], 0, c, q["p"]] = jnp.where(lane_half < HEAD_DIM, q["bf"][c * blk:c * blk + HEAD_DIM],
                                                    q["bf"][c * blk + HEAD_DIM:(c + 1) * blk])
    for bi in range(PREP_NB):
        for d in range(2):
            rp_o, y0_o = outs[d][2:]
            rp_o[bi] = jnp.concatenate([q["rp"] for q in probs if q["d"] == d and q["bi"] == bi], axis=1)
            y0_o[bi] = jnp.concatenate([q["y0"] for q in probs if q["d"] == d and q["bi"] == bi], axis=1)


def chunk_prep(r, kk, v, lw0, lw1, k0, k1, b0, b1):
    nb, ta, c = r.shape
    nblk = ta // SCAN_BLK
    npair = c // LANES
    nchunk = SCAN_BLK // CHUNK
    pb = PREP_NB
    assert nb % pb == 0
    nat = pl.BlockSpec((pb, SCAN_BLK, c), lambda b, s: (b, s, 0))
    a_shape = jax.ShapeDtypeStruct((nb, nblk, nchunk, npair, LANES, LANES), BF16)
    bm_shape = jax.ShapeDtypeStruct((nb, nblk, nchunk, npair, HEAD_DIM, LANES), F32)
    nat_shape = jax.ShapeDtypeStruct((nb, ta, c), F32)
    a_spec = pl.BlockSpec((pb, 1, nchunk, npair, LANES, LANES), lambda b, s: (b, s, 0, 0, 0, 0))
    bm_spec = pl.BlockSpec((pb, 1, nchunk, npair, HEAD_DIM, LANES), lambda b, s: (b, s, 0, 0, 0, 0))
    return pl.pallas_call(
        _chunk_prep_kernel,
        out_shape=[a_shape, bm_shape, nat_shape, nat_shape] * 2,
        grid=(nb // pb, nblk),
        in_specs=[nat] * 9,
        out_specs=[a_spec, bm_spec, nat, nat] * 2,
        compiler_params=_cparams(("parallel", "parallel"), VMEM_LIMIT),
        name="chunk_prep",
    )(r, kk, v, lw0, k0, b0, lw1, k1, b1)


def _chunk_scan_kernel(nb, a0, bm0, rp0, y00, a1, bm1, rp1, y01, yf_o, yb_o, s_scr):
    step = pl.program_id(0)
    npair = BRANCH_W // LANES
    nchunk = SCAN_BLK // CHUNK

    @pl.when(step == 0)
    def _():
        s_scr[...] = jnp.zeros_like(s_scr)

    lane = lax.broadcasted_iota(jnp.int32, (CHUNK, LANES), 1)
    refs = ((a0, bm0, rp0, y00, yf_o), (a1, bm1, rp1, y01, yb_o))
    for ci in range(nchunk):
        for d in range(2):
            a_ref, bm_ref, rp_ref, y0_ref, y_ref = refs[d]
            c = ci if d == 0 else nchunk - 1 - ci
            rows = slice(c * CHUNK, (c + 1) * CHUNK)
            for b in range(nb):
                for p in range(npair):
                    lanes = slice(p * LANES, (p + 1) * LANES)
                    s = s_scr[d, b, p]
                    rpc = rp_ref[b, rows, lanes]
                    lhs = jnp.concatenate([jnp.where(lane < HEAD_DIM, rpc, 0.0), jnp.where(lane >= HEAD_DIM, rpc, 0.0)],
                                          axis=0)
                    yh = lax.dot_general(lhs, s, (((1,), (1,)), ((), ())), preferred_element_type=F32)
                    y_ref[b, rows, lanes] = jnp.concatenate([yh[:CHUNK], yh[CHUNK:]], axis=1) + y0_ref[b, rows, lanes]
                    s_scr[d, b, p] = (jnp.dot(s.astype(BF16), a_ref[b, 0, c, p], preferred_element_type=F32)
                                      + bm_ref[b, 0, c, p])


def rwkv_scan(r, kk, v, lw0, lw1, k0, k1, b0, b1, lc):
    nb, ta, c = r.shape
    nblk = ta // SCAN_BLK
    nctb = lc // SCAN_BLK
    npair = c // LANES
    nchunk = SCAN_BLK // CHUNK
    a0, bm0, rp0, y00, a1, bm1, rp1, y01 = chunk_prep(r, kk, v, lw0, lw1, k0, k1, b0, b1)

    def fwd(s):
        return s

    def bwd(s):
        return jnp.where(s < nctb, nctb - 1 - s, nblk - 1 - (s - nctb))

    def specs(idx):
        return [pl.BlockSpec((nb, 1, nchunk, npair, LANES, LANES), lambda s: (0, idx(s), 0, 0, 0, 0)),
                pl.BlockSpec((nb, 1, nchunk, npair, HEAD_DIM, LANES), lambda s: (0, idx(s), 0, 0, 0, 0)),
                pl.BlockSpec((nb, SCAN_BLK, c), lambda s: (0, idx(s), 0)),
                pl.BlockSpec((nb, SCAN_BLK, c), lambda s: (0, idx(s), 0))]

    out = jax.ShapeDtypeStruct((nb, ta, c), F32)
    return pl.pallas_call(
        functools.partial(_chunk_scan_kernel, nb),
        out_shape=[out, out],
        grid=(nblk,),
        in_specs=specs(fwd) + specs(bwd),
        out_specs=[pl.BlockSpec((nb, SCAN_BLK, c), lambda s: (0, fwd(s), 0)),
                   pl.BlockSpec((nb, SCAN_BLK, c), lambda s: (0, bwd(s), 0))],
        scratch_shapes=[pltpu.VMEM((2, nb, npair, HEAD_DIM, LANES), F32)],
        compiler_params=_cparams(("arbitrary",), VMEM_LIMIT),
        name="chunk_scan",
    )(a0, bm0, rp0, y00, a1, bm1, rp1, y01)


def _rwkv_out_kernel(yf_ref, yb_ref, g_ref, bon_ref, lg_ref, lb_ref, bd_ref, o_ref):
    y = yf_ref[0] + yb_ref[0]
    bd = bd_ref[...]
    mu = _head_sums(y, bd) * (1.0 / HEAD_DIM)
    yc = y - mu
    var = _head_sums(yc * yc, bd) * (1.0 / HEAD_DIM)
    yn = yc * lax.rsqrt(var + RWKV_GN_EPS) * lg_ref[...] + lb_ref[...]
    o_ref[0] = ((yn + bon_ref[0]) * g_ref[0]).astype(BF16)


def rwkv_out(yf, yb, g, bon, lnx_g, lnx_b, bd):
    nb, ta, _ = yf.shape
    nat_spec = pl.BlockSpec((1, TT, BRANCH_W), lambda b, t: (b, t, 0))
    row = pl.BlockSpec((1, BRANCH_W), lambda b, t: (0, 0))
    return pl.pallas_call(
        _rwkv_out_kernel,
        out_shape=jax.ShapeDtypeStruct((nb, ta, BRANCH_W), BF16),
        grid=(nb, ta // TT),
        in_specs=[nat_spec] * 4 + [row, row, pl.BlockSpec(bd.shape, lambda b, t: (0, 0))],
        out_specs=nat_spec,
        compiler_params=_cparams(("parallel", "parallel")),
        name="rwkv_out",
    )(yf, yb, g, bon, lnx_g.reshape(1, -1), lnx_b.reshape(1, -1), bd)


def _pair_swap(x):
    lane = lax.broadcasted_iota(jnp.int32, x.shape, 1)
    n = x.shape[1]
    return jnp.where(lane % 2 == 0, pltpu.roll(x, n - 1, axis=1), pltpu.roll(x, 1, axis=1))


def _attn_prep_kernel(p_ref, cos_ref, sin_ref, qg_ref, kg_ref, bd_ref, q_o, k_o, v_o):
    p = p_ref[0]
    q, k, v = p[:, :Q_W], p[:, Q_W:Q_W + KV_W], p[:, Q_W + KV_W:]
    bd = bd_ref[...]
    cos, sin = cos_ref[...], sin_ref[...]
    qms = _head_sums(q * q, bd) * (1.0 / HEAD_DIM)
    qn = q * lax.rsqrt(qms + RMS_EPS) * qg_ref[...]
    qr = qn * cos + _pair_swap(qn) * sin
    q_o[0] = (qr * HEAD_DIM ** -0.5).astype(BF16)
    kms = _head_sums(k * k, bd[:KV_W, :KV_W]) * (1.0 / HEAD_DIM)
    kn = k * lax.rsqrt(kms + RMS_EPS) * kg_ref[...]
    kr = kn * cos[:, :KV_W] + _pair_swap(kn) * sin[:, :KV_W]
    for g in range(KV_W // HEAD_DIM):
        sl = slice(g * HEAD_DIM, (g + 1) * HEAD_DIM)
        k_o[0, g] = kr[:, sl].astype(BF16)
        v_o[0, g] = v[:, sl].astype(BF16)


def attn_prep(p_attn, cos_t, sin_t, q_norm, k_norm, bd):
    nb, ta, _ = p_attn.shape
    ng = KV_W // HEAD_DIM
    qg = jnp.tile(q_norm, Q_W // HEAD_DIM).reshape(1, -1)
    kg = jnp.tile(k_norm, ng).reshape(1, -1)
    kv_shape = jax.ShapeDtypeStruct((nb, ng, ta, HEAD_DIM), BF16)
    kv_spec = pl.BlockSpec((1, ng, TT, HEAD_DIM), lambda b, t: (b, 0, t, 0))
    return pl.pallas_call(
        _attn_prep_kernel,
        out_shape=[jax.ShapeDtypeStruct((nb, ta, Q_W), BF16), kv_shape, kv_shape],
        grid=(nb, ta // TT),
        in_specs=[pl.BlockSpec((1, TT, ATTN_COLS), lambda b, t: (b, t, 0)),
                  pl.BlockSpec((TT, Q_W), lambda b, t: (t, 0)),
                  pl.BlockSpec((TT, Q_W), lambda b, t: (t, 0)),
                  pl.BlockSpec((1, Q_W), lambda b, t: (0, 0)),
                  pl.BlockSpec((1, KV_W), lambda b, t: (0, 0)),
                  pl.BlockSpec(bd.shape, lambda b, t: (0, 0))],
        out_specs=[pl.BlockSpec((1, TT, Q_W), lambda b, t: (b, t, 0)), kv_spec, kv_spec],
        compiler_params=_cparams(("parallel", "parallel")),
        name="attn_prep",
    )(p_attn, cos_t, sin_t, qg, kg, bd)


def _attn_kernel(nct, lc, q_ref, k_ref, v_ref, o_ref):
    t = pl.program_id(1)
    ng = k_ref.shape[1]
    rep = Q_W // HEAD_DIM // ng

    def run(nk):
        outs = []
        for g in range(ng):
            kk, vv = k_ref[0, g, :nk, :], v_ref[0, g, :nk, :]
            for r in range(rep):
                h = g * rep + r
                q = q_ref[0][:, h * HEAD_DIM:(h + 1) * HEAD_DIM]
                s = lax.dot_general(q, kk, (((1,), (1,)), ((), ())), preferred_element_type=F32)
                p = jnp.exp(s - jnp.max(s, -1, keepdims=True))
                l = jnp.sum(p, -1, keepdims=True)
                o = jnp.dot(p.astype(BF16), vv, preferred_element_type=F32)
                outs.append(o / l)
        o_ref[0] = jnp.concatenate(outs, axis=1).astype(BF16)

    @pl.when(t < nct)
    def _():
        run(lc)

    @pl.when(t >= nct)
    def _():
        run(k_ref.shape[2])


def attention(q, k, v, lc):
    nb, ta, _ = q.shape
    ng = k.shape[1]
    nct = lc // TT
    qo_spec = pl.BlockSpec((1, TT, Q_W), lambda b, t: (b, t, 0))
    kv_spec = pl.BlockSpec((1, ng, ta, HEAD_DIM), lambda b, t: (b, 0, 0, 0))
    return pl.pallas_call(
        functools.partial(_attn_kernel, nct, lc),
        out_shape=jax.ShapeDtypeStruct((nb, ta, Q_W), BF16),
        grid=(nb, ta // TT),
        in_specs=[qo_spec, kv_spec, kv_spec],
        out_specs=qo_spec,
        compiler_params=_cparams(("parallel", "arbitrary"), VMEM_LIMIT),
        name="attention",
    )(q, k, v)


def _hs_pre_kernel(nct, nt, cur_ref, prev_ref, next_ref, hw_ref, sw_ref, x0_o, u_o, ycv_o):
    cur = cur_ref[0]
    xm1, xp1 = _neighbours(cur, prev_ref, next_ref, nct, nt)
    c = BRANCH_W
    hc = HYENA_COLS
    hw = hw_ref[...]
    ph = hw[0:1] * xm1[:, :hc] + hw[1:2] * cur[:, :hc] + hw[2:3] * xp1[:, :hc]
    x0_o[0] = ph[:, :c]
    u_o[0] = ph[:, c:2 * c] * ph[:, 2 * c:3 * c]
    sw = sw_ref[...]

    def cx(a):
        return a[:, hc + c:hc + 2 * c] * a[:, hc + 2 * c:hc + 3 * c]

    conv = sw[0:1] * cx(xm1) + sw[1:2] * cx(cur) + sw[2:3] * cx(xp1)
    ycv_o[0] = (cur[:, hc:hc + c] * conv).astype(BF16)


def hs_pre(p_hs, nct, hyena_conv, sconv_w):
    nb, ta, w = p_hs.shape
    nt = ta // TT
    nat = pl.BlockSpec((1, TT, BRANCH_W), lambda b, t: (b, t, 0))
    return pl.pallas_call(
        functools.partial(_hs_pre_kernel, nct, nt),
        out_shape=[jax.ShapeDtypeStruct((nb, ta, BRANCH_W), F32),
                   jax.ShapeDtypeStruct((nb, ta, BRANCH_W), F32),
                   jax.ShapeDtypeStruct((nb, ta, BRANCH_W), BF16)],
        grid=(nb, nt),
        in_specs=_halo_specs(w, ta) + [pl.BlockSpec(hyena_conv.shape, lambda b, t: (0, 0)),
                                       pl.BlockSpec(sconv_w.shape, lambda b, t: (0, 0))],
        out_specs=[nat, nat, nat],
        compiler_params=_cparams(("parallel", "parallel"), VMEM_LIMIT),
        name="hs_pre",
    )(p_hs, p_hs, p_hs, hyena_conv, sconv_w)


EMB_PAD = 40


def _filter_tables(lh):
    n = np.arange(2 * lh)
    pos = np.abs(n - (lh - 1)).astype(np.float64)
    bands = (HYENA_EMB - 1) // 2
    t = np.minimum(pos, lh - 1) / (lh - 1)
    wpos = 2.0 * math.pi * pos / lh
    f = np.linspace(1e-4, bands - 1, bands)[:, None]
    z = np.zeros((EMB_PAD, 2 * lh), np.float32)
    z[0] = t
    z[1:1 + bands] = np.cos(f * wpos[None, :])
    z[1 + bands:1 + 2 * bands] = -np.sin(f * wpos[None, :])
    max_decay = math.log(HYENA_TARGET) / HYENA_FAST_DECAY
    min_decay = math.log(HYENA_TARGET) / HYENA_SLOW_DECAY
    deltas = np.abs(np.linspace(min_decay, max_decay, BRANCH_W)).astype(np.float32)
    return z, deltas.reshape(-1, 1)


def _filter_kernel(lh, tn, z_ref, w1_ref, b1_ref, f1_ref, w2_ref, b2_ref, f2_ref, w3_ref, dl_ref, o_ref):
    z = z_ref[...]
    h1 = jnp.sin(f1_ref[...] * (jnp.dot(w1_ref[...], z, precision=HI, preferred_element_type=F32) + b1_ref[...]))
    h2 = jnp.sin(f2_ref[...] * (jnp.dot(w2_ref[...], h1, precision=HI, preferred_element_type=F32) + b2_ref[...]))
    f = jnp.dot(w3_ref[...], h2, precision=HI, preferred_element_type=F32)
    n = pl.program_id(0) * tn + lax.broadcasted_iota(jnp.int32, (1, tn), 1)
    filt = jnp.where(n >= lh - 1, f[:BRANCH_W], f[BRANCH_W:])
    win = jnp.exp(-z[0:1, :] * dl_ref[...])
    o_ref[...] = jnp.where(n == 2 * lh - 1, 0.0, filt * win)


def hyena_filter_table(lh, w1, b1, f1, w2, b2, f2, w3):
    z_np, dl_np = _filter_tables(lh)
    n2 = 2 * lh
    tn = _pick_tile(n2, (1024, 512))
    hd = w2.shape[0]
    w1t = jnp.zeros((hd, EMB_PAD), F32).at[:, :HYENA_EMB].set(w1.T)
    args = [jnp.asarray(z_np), w1t, b1.reshape(-1, 1), f1.reshape(-1, 1), w2.T, b2.reshape(-1, 1),
            f2.reshape(-1, 1), w3.T, jnp.asarray(dl_np)]

    def full(a):
        return pl.BlockSpec(a.shape, lambda j: (0, 0))

    return pl.pallas_call(
        functools.partial(_filter_kernel, lh, tn),
        out_shape=jax.ShapeDtypeStruct((BRANCH_W, n2), F32),
        grid=(n2 // tn,),
        in_specs=[pl.BlockSpec((EMB_PAD, tn), lambda j: (0, j))] + [full(a) for a in args[1:]],
        out_specs=pl.BlockSpec((BRANCH_W, tn), lambda j: (0, j)),
        compiler_params=_cparams(("parallel",)),
        name="hyena_filter",
    )(*args)


def _hyena_conv_kernel(nblk, bp, nch, k_ref, u_ref, o_ref, t_scr):
    ntile = 4 * nblk - 1
    mc = 2 * nblk - 1
    width = (ntile + 1) * LANES
    for ch in range(nch):
        big = pltpu.roll(jnp.broadcast_to(k_ref[ch], (LANES, width)), width - (LANES - 1), 1, stride=1, stride_axis=0)
        for m in range(ntile):
            t_scr[ch, m] = big[:, m * LANES:(m + 1) * LANES].astype(BF16)

    for ch in range(nch):
        for d in [0] + [s * a for a in range(1, nblk) for s in (1, -1)]:
            m0 = 2 * d + mc
            w = jnp.concatenate([jnp.concatenate([t_scr[ch, m0], t_scr[ch, m0 + 1]], axis=1),
                                 jnp.concatenate([t_scr[ch, m0 - 1], t_scr[ch, m0]], axis=1)], axis=0)
            i0, i1 = max(0, d), min(nblk, nblk + d)
            lhs = u_ref[ch, (i0 - d) * bp:(i1 - d) * bp, :].astype(BF16)
            res = jnp.dot(lhs, w, preferred_element_type=F32)
            if d == 0:
                o_ref[ch] = res
            else:
                o_ref[ch, i0 * bp:i1 * bp, :] += res


def hyena_conv(u, ktab):
    nb, l, c = u.shape
    nblk = l // HY_BLK
    bp = -(-nb // SUBLANES) * SUBLANES
    nch = SUBLANES if nblk == 1 else 1
    ut = jnp.transpose(u.reshape(nb, nblk, HY_BLK, c), (3, 1, 0, 2))
    if bp != nb:
        ut = jnp.pad(ut, ((0, 0), (0, 0), (0, bp - nb), (0, 0)))
    ut = ut.reshape(c, nblk * bp, HY_BLK)
    k3 = ktab.reshape(c, 1, 4 * nblk * LANES)
    out = pl.pallas_call(
        functools.partial(_hyena_conv_kernel, nblk, bp, nch),
        out_shape=jax.ShapeDtypeStruct((c, nblk * bp, HY_BLK), F32),
        grid=(c // nch,),
        in_specs=[pl.BlockSpec((nch, 1, 4 * nblk * LANES), lambda ch: (ch, 0, 0)),
                  pl.BlockSpec((nch, nblk * bp, HY_BLK), lambda ch: (ch, 0, 0))],
        out_specs=pl.BlockSpec((nch, nblk * bp, HY_BLK), lambda ch: (ch, 0, 0)),
        scratch_shapes=[pltpu.VMEM((nch, 4 * nblk - 1, LANES, LANES), BF16)],
        compiler_params=_cparams(("parallel",)),
        name="hyena_conv",
    )(k3, ut)
    out = out.reshape(c, nblk, bp, HY_BLK)[:, :, :nb]
    return jnp.transpose(out, (2, 1, 3, 0)).reshape(nb, l, c)


def _route(logits, bias):
    s = jax.nn.sigmoid(logits)
    sel = s + bias
    srow = [s[e:e + 1] for e in range(N_EXPERTS)]
    row = [sel[e:e + 1] for e in range(N_EXPERTS)]
    best, gi = None, None
    for g in range(N_GROUPS):
        a, b, c, d = row[4 * g:4 * g + 4]
        hi1, lo1, hi2, lo2 = jnp.maximum(a, b), jnp.minimum(a, b), jnp.maximum(c, d), jnp.minimum(c, d)
        score = jnp.maximum(hi1, hi2) + jnp.maximum(jnp.minimum(hi1, hi2), jnp.maximum(lo1, lo2))
        if g == 0:
            best, gi = score, jnp.zeros(score.shape, jnp.int32)
        else:
            better = score > best
            gi = jnp.where(better, g, gi)
            best = jnp.where(better, score, best)
    neg = -jnp.inf
    msel = [jnp.where(gi == e // EXPERTS_PER_GROUP, row[e], neg) for e in range(N_EXPERTS)]

    def arg_first_max(vals):
        bv, bi = vals[0], jnp.zeros(vals[0].shape, jnp.int32)
        for e in range(1, N_EXPERTS):
            better = vals[e] > bv
            bi = jnp.where(better, e, bi)
            bv = jnp.where(better, vals[e], bv)
        return bi

    i1 = arg_first_max(msel)
    i2 = arg_first_max([jnp.where(i1 == e, neg, msel[e]) for e in range(N_EXPERTS)])
    w1 = sum(jnp.where(i1 == e, srow[e], 0.0) for e in range(N_EXPERTS))
    w2 = sum(jnp.where(i2 == e, srow[e], 0.0) for e in range(N_EXPERTS))
    den = w1 + w2
    g1, g2 = w1 / den, w2 / den
    rows = [jnp.where(i1 == e, g1, 0.0) + jnp.where(i2 == e, g2, 0.0) for e in range(N_EXPERTS)]
    rows.append(gi.astype(F32))
    rows.extend([jnp.zeros_like(g1)] * (ROUTE_ROWS - len(rows)))
    return jnp.concatenate(rows, axis=0)


MERGE_NB = 2


def _merge_kernel(ya_ref, x0_ref, u_ref, yc_ref, ycv_ref, yd_ref, h_ref, x_ref, *rest):
    mods, rest = rest[:3 * MERGE_NB], rest[3 * MERGE_NB:]
    skip_ref, wg_ref, wb_ref, wo_ref, g1_ref, b1_ref, rwt_ref, rb_ref = rest[:8]
    x1_o, hf_o = rest[8:10]
    gates_o = rest[10:]
    rows = MERGE_NB * TT
    flat = lambda ref: ref[...].reshape(rows, ref.shape[-1])
    yb = (flat(x0_ref) * (flat(yc_ref) + flat(u_ref) * skip_ref[...])).astype(BF16)
    ys = (flat(ya_ref), yb, flat(ycv_ref), flat(yd_ref))
    h = flat(h_ref)
    merged = None
    for n in range(N_BRANCHES):
        gate = jax.nn.sigmoid(jnp.dot(h, wg_ref[:, n * D_MODEL:(n + 1) * D_MODEL], preferred_element_type=F32))
        term = gate * jnp.dot(ys[n], wb_ref[n], preferred_element_type=F32)
        merged = term if merged is None else merged + term
    out = jnp.dot(merged.astype(BF16), wo_ref[...], preferred_element_type=F32)
    alpha = (2 * 2) ** 0.25
    for k in range(MERGE_NB):
        ga_ref, shf_ref, scf_ref = mods[3 * k:3 * k + 3]
        x1 = _ln(alpha * x_ref[k] + ga_ref[0] * out[k * TT:(k + 1) * TT]) * g1_ref[...] + b1_ref[...]
        hf = _ln(x1) * (1.0 + scf_ref[0]) + shf_ref[0]
        x1_o[k] = x1
        hf_o[k] = hf.astype(BF16)
        logits = lax.dot_general(rwt_ref[...], hf, (((1,), (1,)), ((), ())), precision=HI,
                                 preferred_element_type=F32)
        gates_o[k][...] = _route(logits, rb_ref[...])


def merge(ya, x0, u, yconv, ycv, yd, h, x_all, mod3, nct, skip, wg, wb, wo, ln_g, ln_b, rwt, rbias):
    nb, ta, d = x_all.shape
    nt = ta // TT
    mb = MERGE_NB
    assert nb % mb == 0
    nat = pl.BlockSpec((mb, TT, BRANCH_W), lambda b, t: (b, t, 0))
    wide = pl.BlockSpec((mb, TT, d), lambda b, t: (b, t, 0))

    def full(a):
        nd = a.ndim
        return pl.BlockSpec(a.shape, lambda b, t: (0,) * nd)

    def mod_spec(col, k):
        return pl.BlockSpec((1, 1, D_MODEL), lambda b, t: (jnp.where(t < nct, nb, b * mb + k), 0, col))

    mod_specs = [mod_spec(col, k) for k in range(mb) for col in (2, 3, 4)]
    consts = [skip.reshape(1, -1), wg, wb, wo, ln_g.reshape(1, -1), ln_b.reshape(1, -1), rwt, rbias.reshape(-1, 1)]
    gate_shape = jax.ShapeDtypeStruct((ROUTE_ROWS, nb // mb * ta), F32)
    gate_spec = pl.BlockSpec((ROUTE_ROWS, TT), lambda b, t: (0, b * nt + t))
    outs = pl.pallas_call(
        _merge_kernel,
        out_shape=[jax.ShapeDtypeStruct((nb, ta, d), F32), jax.ShapeDtypeStruct((nb, ta, d), BF16)] + [gate_shape] * mb,
        grid=(nb // mb, nt),
        in_specs=[nat] * 6 + [wide, wide] + mod_specs + [full(a) for a in consts],
        out_specs=[wide, wide] + [gate_spec] * mb,
        compiler_params=_cparams(("parallel", "parallel"), VMEM_LIMIT),
        name="merge",
    )(ya, x0, u, yconv, ycv, yd, h, x_all, *([mod3] * (3 * mb)), *consts)
    gates_t = jnp.stack([g.reshape(ROUTE_ROWS, nb // mb, ta) for g in outs[2:]], axis=2).reshape(ROUTE_ROWS, nb * ta)
    return outs[0], outs[1], gates_t


MOE_TILE = 1024
MOE_ALIGN = 2 * SUBLANES
MOE_CHUNK = 18 * MOE_ALIGN
MOE_SORTED = MOE_TILE + LANES
MOE_ROWS = MOE_SORTED + MOE_CHUNK
GID_ROW = N_EXPERTS
ROUTE_ROWS = 3 * SUBLANES
META_LANES = 2 * N_GROUPS
assert N_GROUPS * (MOE_ALIGN - 1) <= MOE_SORTED - MOE_TILE and MOE_ROWS % MOE_ALIGN == 0


def _moe_sort_kernel(gt_ref, g_ref, h_ref, up_ref, hs_o, gs_o, pt_o, meta_o):
    gid = gt_ref[GID_ROW:GID_ROW + 1, :]
    onehot = [jnp.where(gid == float(g), 1.0, 0.0) for g in range(N_GROUPS)]
    g4 = jnp.concatenate(onehot + [jnp.zeros((SUBLANES - N_GROUPS, MOE_TILE), F32)], axis=0)
    before = jnp.dot(g4.astype(BF16), up_ref[...], preferred_element_type=F32)
    lane = lax.broadcasted_iota(jnp.int32, (SUBLANES, LANES), 1)
    meta = jnp.zeros((SUBLANES, LANES), F32)
    off = jnp.zeros((1, 1), F32)
    pos = jnp.zeros((1, MOE_TILE), F32)
    for g in range(N_GROUPS):
        cnt = jnp.sum(onehot[g], axis=1, keepdims=True)
        pos = pos + onehot[g] * (before[g:g + 1] + off)
        meta = jnp.where(lane == g, off, meta)
        meta = jnp.where(lane == N_GROUPS + g, cnt, meta)
        off = off + jnp.ceil(cnt * (1.0 / MOE_ALIGN)) * MOE_ALIGN
    meta_o[0] = meta.astype(jnp.int32)
    row = lax.broadcasted_iota(jnp.int32, (MOE_SORTED, MOE_TILE), 0)
    place = jnp.where(row == pos.astype(jnp.int32), 1.0, 0.0)
    p16 = place.astype(BF16)
    hs_o[0, :MOE_SORTED, :] = jnp.dot(p16, h_ref[...], preferred_element_type=F32).astype(BF16)
    hs_o[0, MOE_SORTED:, :] = jnp.zeros((MOE_ROWS - MOE_SORTED, h_ref.shape[1]), BF16)
    gts = g_ref[...]
    hi = gts.astype(BF16)
    r1 = gts - hi.astype(F32)
    mid = r1.astype(BF16)
    low = (r1 - mid.astype(F32)).astype(BF16)
    gs_o[0, :MOE_SORTED, :] = (jnp.dot(p16, hi, preferred_element_type=F32)
                               + jnp.dot(p16, mid, preferred_element_type=F32)
                               + jnp.dot(p16, low, preferred_element_type=F32))
    gs_o[0, MOE_SORTED:, :] = jnp.zeros((MOE_ROWS - MOE_SORTED, LANES), F32)
    pt_o[...] = place.T.astype(BF16)


def moe_sort(hf, gates_t, gates):
    n, d = hf.shape
    ntile = n // MOE_TILE
    upper = jnp.asarray(np.triu(np.ones((MOE_TILE, MOE_TILE), np.float32), 1), BF16)
    return pl.pallas_call(
        _moe_sort_kernel,
        out_shape=[jax.ShapeDtypeStruct((ntile, MOE_ROWS, d), BF16),
                   jax.ShapeDtypeStruct((ntile, MOE_ROWS, LANES), F32),
                   jax.ShapeDtypeStruct((n, MOE_SORTED), BF16),
                   jax.ShapeDtypeStruct((ntile, SUBLANES, LANES), jnp.int32)],
        grid=(ntile,),
        in_specs=[pl.BlockSpec((ROUTE_ROWS, MOE_TILE), lambda i: (0, i)),
                  pl.BlockSpec((MOE_TILE, LANES), lambda i: (i, 0)),
                  pl.BlockSpec((MOE_TILE, d), lambda i: (i, 0)),
                  pl.BlockSpec((MOE_TILE, MOE_TILE), lambda i: (0, 0))],
        out_specs=[pl.BlockSpec((1, MOE_ROWS, d), lambda i: (i, 0, 0)),
                   pl.BlockSpec((1, MOE_ROWS, LANES), lambda i: (i, 0, 0)),
                   pl.BlockSpec((MOE_TILE, MOE_SORTED), lambda i: (i, 0)),
                   pl.BlockSpec((1, SUBLANES, LANES), lambda i: (i, 0, 0))],
        compiler_params=_cparams(("parallel",), VMEM_LIMIT),
        name="moe_sort",
    )(gates_t, gates, hf, upper)


def _moe_group_kernel(meta_ref, hs_ref, gs_ref, w1_ref, w3_ref, w2_ref, prev_ref, ys_o):
    g, i = pl.program_id(0), pl.program_id(1)

    @pl.when(g == 0)
    def _():
        ys_o[...] = jnp.zeros_like(ys_o)

    @pl.when(g > 0)
    def _():
        ys_o[...] = prev_ref[...]

    off = meta_ref[i * META_LANES + g]
    cnt = meta_ref[i * META_LANES + N_GROUPS + g]
    lane = lax.broadcasted_iota(jnp.int32, (MOE_CHUNK, LANES), 1)

    def chunk(j, carry):
        rows = pl.ds(pl.multiple_of(off + j * MOE_CHUNK, MOE_ALIGN), MOE_CHUNK)
        hs = hs_ref[0, rows, :]
        gs = gs_ref[0, rows, :]
        acc = ys_o[0, rows, :].astype(F32)
        for e in range(EXPERTS_PER_GROUP):
            a = jnp.dot(hs, w1_ref[e], preferred_element_type=F32)
            b = jnp.dot(hs, w3_ref[e], preferred_element_type=F32)
            act = (a * jax.nn.sigmoid(a)) * b
            gcol = jnp.sum(jnp.where(lane == g * EXPERTS_PER_GROUP + e, gs, 0.0), axis=1, keepdims=True)
            acc = acc + gcol * jnp.dot(act.astype(BF16), w2_ref[e], preferred_element_type=F32)
        ys_o[0, rows, :] = acc.astype(BF16)
        return carry

    lax.fori_loop(0, lax.div(cnt + (MOE_CHUNK - 1), MOE_CHUNK), chunk, 0)


def moe_group(hs, gs, meta, w1, w3, w2):
    ntile, _, d = hs.shape
    de = w1.shape[2]
    epg = EXPERTS_PER_GROUP
    tile_spec = pl.BlockSpec((1, MOE_ROWS, d), lambda g, i, m: (i, 0, 0))
    grid_spec = pltpu.PrefetchScalarGridSpec(
        num_scalar_prefetch=1,
        grid=(N_GROUPS, ntile),
        in_specs=[tile_spec,
                  pl.BlockSpec((1, MOE_ROWS, LANES), lambda g, i, m: (i, 0, 0)),
                  pl.BlockSpec((epg, d, de), lambda g, i, m: (g, 0, 0)),
                  pl.BlockSpec((epg, d, de), lambda g, i, m: (g, 0, 0)),
                  pl.BlockSpec((epg, de, d), lambda g, i, m: (g, 0, 0)),
                  tile_spec],
        out_specs=tile_spec,
    )
    carried = jnp.zeros((ntile, MOE_ROWS, d), BF16)
    return pl.pallas_call(
        _moe_group_kernel,
        out_shape=jax.ShapeDtypeStruct((ntile, MOE_ROWS, d), BF16),
        grid_spec=grid_spec,
        input_output_aliases={6: 0},
        compiler_params=_cparams(("arbitrary", "arbitrary"), VMEM_LIMIT),
        name="moe_group",
    )(meta, hs, gs, w1, w3, w2, carried)


def _ln2_kernel(emit_next, x_ref, pt_ref, ys_ref, gf_ref, g_ref, b_ref, *rest):
    alpha = (2 * 2) ** 0.25
    f = jnp.dot(pt_ref[...], ys_ref[0], preferred_element_type=F32)
    x2 = _ln(alpha * x_ref[0] + gf_ref[0] * f) * g_ref[...] + b_ref[...]
    if emit_next:
        sh_ref, sc_ref, o_ref, h_o = rest
        h_o[0] = (_ln(x2) * (1.0 + sc_ref[0]) + sh_ref[0]).astype(BF16)
    else:
        o_ref, = rest
    o_ref[0] = x2


def ln2(x1, pt, ysb, mod3, nct, ln_g, ln_b, mod3_next):
    nb, ta, d = x1.shape
    nt = ta // TT
    per = MOE_TILE // TT
    wide = pl.BlockSpec((1, TT, d), lambda b, t: (b, t, 0))
    row = pl.BlockSpec((1, d), lambda b, t: (0, 0))
    in_specs = [wide, pl.BlockSpec((TT, MOE_SORTED), lambda b, t: (b * nt + t, 0)),
                pl.BlockSpec((1, MOE_SORTED, d), lambda b, t: ((b * nt + t) // per, 0, 0)),
                _mod_spec(5, nct, nb), row, row]
    args = [x1, pt, ysb, mod3, ln_g.reshape(1, -1), ln_b.reshape(1, -1)]
    if mod3_next is None:
        out_shape = jax.ShapeDtypeStruct((nb, ta - nct * TT, d), F32)
        out_specs = pl.BlockSpec((1, TT, d), lambda b, t: (b, jnp.maximum(t - nct, 0), 0))
    else:
        in_specs += [_mod_spec(0, nct, nb), _mod_spec(1, nct, nb)]
        args += [mod3_next, mod3_next]
        out_shape = [jax.ShapeDtypeStruct((nb, ta, d), F32), jax.ShapeDtypeStruct((nb, ta, d), BF16)]
        out_specs = [wide, wide]
    return pl.pallas_call(
        functools.partial(_ln2_kernel, mod3_next is not None),
        out_shape=out_shape,
        grid=(nb, nt),
        in_specs=in_specs,
        out_specs=out_specs,
        compiler_params=_cparams(("parallel", "arbitrary"), VMEM_LIMIT),
        name="ln2",
    )(*args)


def _rope_tables(l, lc):
    half = HEAD_DIM // 2
    inv = ROPE_THETA ** (-np.arange(0, half, 2, dtype=np.float64) / half)
    t = np.arange(l)
    rows, cols = t // GRID_W, t % GRID_W
    ang = np.concatenate([rows[:, None] * inv, cols[:, None] * inv], -1)
    ang = np.concatenate([np.zeros((lc, half)), ang], 0)
    cos = np.repeat(np.cos(ang), 2, axis=1)
    sin = np.repeat(np.sin(ang), 2, axis=1)
    sin[:, 0::2] *= -1.0
    reps = Q_W // HEAD_DIM
    return (jnp.asarray(np.tile(cos, (1, reps)), F32), jnp.asarray(np.tile(sin, (1, reps)), F32))


def _block_diag_ones():
    i = np.arange(BRANCH_W) // HEAD_DIM
    return jnp.asarray((i[:, None] == i[None, :]).astype(np.float32))


def kernel(x, c, ctx, c_ctx, ada_w, ada_b, w_in, rwkv_mu, rwkv_w0, rwkv_w_up, rwkv_a0, rwkv_a_up, rwkv_g_up, rwkv_k_k, rwkv_k_a, rwkv_r_k, rwkv_lnx_g, rwkv_lnx_b, hyena_conv, hyena_w1, hyena_b1, hyena_freq1, hyena_w2, hyena_b2, hyena_freq2, hyena_w3, hyena_skip, sconv_w, attn_q_norm, attn_k_norm, w_branch, w_out, ln1_g, ln1_b, ln2_g, ln2_b, router_w, router_bias, exp_w1, exp_w3, exp_w2):
    nb, l, d = x.shape
    lc = ctx.shape[1]
    depth = ada_w.shape[0]
    assert d == D_MODEL and lc % TT == 0 and l % TT == 0 and l % GRID_W == 0 and (nb * (lc + l)) % MOE_TILE == 0
    ta = lc + l
    nct = lc // TT
    x_all = jnp.concatenate([ctx, x], axis=1)

    mod_rows = -(-(nb + 1) // SUBLANES) * SUBLANES
    cc = jnp.zeros((mod_rows, d), F32).at[:nb].set(c).at[nb].set(c_ctx)
    mod = ada_mod(cc, ada_w, ada_b)

    cos_t, sin_t = _rope_tables(l, lc)
    bd = _block_diag_ones()
    rwt = router_w.T

    for li in range(depth):
        mod3 = mod[li].reshape(mod_rows, 1, N_MOD * d)
        wl = w_in[li].astype(BF16)
        if li == 0:
            h3 = lnmod(x_all, mod3, nct, 0, 1)
        h = h3.reshape(nb * ta, d)
        p_rwkv = matmul(h, wl[:, :OFF_HYENA]).reshape(nb, ta, -1)
        p_hs = matmul(h, wl[:, OFF_HYENA:OFF_ATTN]).reshape(nb, ta, -1)
        p_attn = matmul(h, wl[:, OFF_ATTN:OFF_GATE]).reshape(nb, ta, -1)

        r, kk, w0, w1, k0, k1, b0, b1, v, g, bon = rwkv_prep(
            p_rwkv, nct, rwkv_mu[li], rwkv_w0[li], rwkv_w_up[li], rwkv_a0[li], rwkv_a_up[li], rwkv_g_up[li],
            rwkv_k_k[li], rwkv_k_a[li], rwkv_r_k[li], bd)
        yf, yb = rwkv_scan(r, kk, v, w0, w1, k0, k1, b0, b1, lc)
        ya = rwkv_out(yf, yb, g, bon, rwkv_lnx_g[li], rwkv_lnx_b[li], bd)

        q, kx, vx = attn_prep(p_attn, cos_t, sin_t, attn_q_norm[li], attn_k_norm[li], bd)
        yd = attention(q, kx, vx, lc)

        x0, u, ycv = hs_pre(p_hs, nct, hyena_conv[li], sconv_w[li])
        fargs = (hyena_w1[li], hyena_b1[li], hyena_freq1[li], hyena_w2[li], hyena_b2[li], hyena_freq2[li],
                 hyena_w3[li])
        yconv_ctx = hyena_conv_seg(u[:, :lc], fargs) if li < depth - 1 else jnp.zeros((nb, lc, BRANCH_W), F32)
        yconv = jnp.concatenate([yconv_ctx, hyena_conv_seg(u[:, lc:], fargs)], axis=1)

        x1, hf, gates_t = merge(ya, x0, u, yconv, ycv, yd, h3, x_all, mod3, nct, hyena_skip[li], wl[:, OFF_GATE:],
                                w_branch[li].astype(BF16), w_out[li].astype(BF16), ln1_g[li], ln1_b[li],
                                rwt, router_bias)
        gates = jnp.pad(gates_t.T, ((0, 0), (0, LANES - ROUTE_ROWS)))
        hs, gs, pt, meta = moe_sort(hf.reshape(nb * ta, d), gates_t, gates)
        ysb = moe_group(hs, gs, meta[:, 0, :META_LANES].reshape(-1), exp_w1[li].astype(BF16),
                        exp_w3[li].astype(BF16), exp_w2[li].astype(BF16))
        if li == depth - 1:
            return ln2(x1, pt, ysb, mod3, nct, ln2_g[li], ln2_b[li], None)
        x_all, h3 = ln2(x1, pt, ysb, mod3, nct, ln2_g[li], ln2_b[li], mod[li + 1].reshape(mod_rows, 1, N_MOD * d))


def hyena_conv_seg(u_seg, fargs):
    ktab = hyena_filter_table(u_seg.shape[1], *fargs)
    return hyena_conv(u_seg, ktab)
```

```python
import functools
import math

import numpy as np
import jax
import jax.numpy as jnp
from jax import lax
from jax.experimental import pallas as pl
from jax.experimental.pallas import tpu as pltpu

F32 = jnp.float32
BF16 = jnp.bfloat16
HI = lax.Precision.HIGHEST

D_MODEL = 1024
GRID_W = 64
BRANCH_W = 256
HEAD_DIM = 64
N_BRANCHES = 4
N_MOD = 6
RWKV_HEADS = 4
RWKV_COLS = 1024
RWKV_GN_EPS = 64e-5
HYENA_COLS = 768
HYENA_EMB = 33
HYENA_FAST_DECAY = 0.3
HYENA_SLOW_DECAY = 1.5
HYENA_TARGET = 1e-2
SCONV_COLS = 768
Q_W = 256
KV_W = 128
ATTN_COLS = 512
ROPE_THETA = 10000.0
RMS_EPS = 1e-6
OFF_HYENA = RWKV_COLS
OFF_SCONV = OFF_HYENA + HYENA_COLS
OFF_ATTN = OFF_SCONV + SCONV_COLS
OFF_GATE = OFF_ATTN + ATTN_COLS
N_EXPERTS = 16
N_GROUPS = 4
EXPERTS_PER_GROUP = 4
D_EXPERT = 512
LN_EPS = 1e-6

SUBLANES = 8
LANES = 128
TT = 256
SCAN_BLK = LANES
HY_BLK = 256
VMEM_LIMIT = 56 * 1024 * 1024


def _cparams(sem, vmem=None):
    return pltpu.CompilerParams(dimension_semantics=sem, vmem_limit_bytes=vmem)


def _ln(xf):
    mu = jnp.mean(xf, -1, keepdims=True)
    xc = xf - mu
    var = jnp.mean(xc * xc, -1, keepdims=True)
    return xc * lax.rsqrt(var + LN_EPS)


def _head_sums(x, ones_bd):
    ones16 = ones_bd.astype(BF16)
    hi = x.astype(BF16)
    lo = (x - hi.astype(F32)).astype(BF16)
    return jnp.dot(hi, ones16, preferred_element_type=F32) + jnp.dot(lo, ones16, preferred_element_type=F32)


def _pick_tile(n, cands):
    for c in cands:
        if n % c == 0:
            return c
    raise ValueError(f"no tile for {n}")


def _ada_kernel(c_ref, w_ref, b_ref, o_ref):
    c = c_ref[...]
    a = c * jax.nn.sigmoid(c)
    o_ref[0] = jnp.dot(a, w_ref[0], precision=HI, preferred_element_type=F32) + b_ref[0]


def ada_mod(cc, ada_w, ada_b):
    depth, d, n = ada_w.shape
    rows = cc.shape[0]
    return pl.pallas_call(
        _ada_kernel,
        out_shape=jax.ShapeDtypeStruct((depth, rows, n), F32),
        grid=(depth, n // d),
        in_specs=[pl.BlockSpec((rows, d), lambda l, j: (0, 0)),
                  pl.BlockSpec((1, d, d), lambda l, j: (l, 0, j)),
                  pl.BlockSpec((1, 1, d), lambda l, j: (l, 0, j))],
        out_specs=pl.BlockSpec((1, rows, d), lambda l, j: (l, 0, j)),
        compiler_params=_cparams(("parallel", "parallel"), VMEM_LIMIT),
        name="ada_mod",
    )(cc, ada_w, ada_b.reshape(depth, 1, n))


def _lnmod_kernel(nct, c_ref, x_ref, sh_ref, sc_ref, xa_o, h_o):
    t = pl.program_id(1)
    xin = jnp.where(t < nct, c_ref[0], x_ref[0])
    xa_o[0] = xin
    h_o[0] = (_ln(xin) * (1.0 + sc_ref[0]) + sh_ref[0]).astype(BF16)


def _mod_spec(col, nct, nb):
    return pl.BlockSpec((1, 1, D_MODEL), lambda b, t: (jnp.where(t < nct, nb, b), 0, col))


def lnmod(ctx, x, mod3, nct, col_shift, col_scale):
    nb, l, d = x.shape
    ta = ctx.shape[1] + l
    wide = pl.BlockSpec((1, TT, d), lambda b, t: (b, t, 0))
    return pl.pallas_call(
        functools.partial(_lnmod_kernel, nct),
        out_shape=[jax.ShapeDtypeStruct((nb, ta, d), F32), jax.ShapeDtypeStruct((nb, ta, d), BF16)],
        grid=(nb, ta // TT),
        in_specs=[pl.BlockSpec((1, TT, d), lambda b, t: (b, jnp.minimum(t, nct - 1), 0)),
                  pl.BlockSpec((1, TT, d), lambda b, t: (b, jnp.maximum(t - nct, 0), 0)),
                  _mod_spec(col_shift, nct, nb), _mod_spec(col_scale, nct, nb)],
        out_specs=[wide, wide],
        compiler_params=_cparams(("parallel", "arbitrary")),
        name="lnmod",
    )(ctx, x, mod3, mod3)


def _mm_kernel(a_ref, b_ref, o_ref):
    o_ref[...] = jnp.dot(a_ref[...], b_ref[...], preferred_element_type=F32).astype(o_ref.dtype)


def matmul(a, b, out_dtype=F32):
    m, k = a.shape
    _, n = b.shape
    tm = _pick_tile(m, (1024, 512, 256))
    tn = _pick_tile(n, (1024, 512, 256))
    return pl.pallas_call(
        _mm_kernel,
        out_shape=jax.ShapeDtypeStruct((m, n), out_dtype),
        grid=(m // tm, n // tn),
        in_specs=[pl.BlockSpec((tm, k), lambda i, j: (i, 0)),
                  pl.BlockSpec((k, tn), lambda i, j: (0, j))],
        out_specs=pl.BlockSpec((tm, tn), lambda i, j: (i, j)),
        compiler_params=_cparams(("parallel", "parallel"), VMEM_LIMIT),
        name="matmul",
    )(a, b)


def _halo_specs(width, ta, halo=SUBLANES):
    nblk = ta // halo
    per = TT // halo
    cur = pl.BlockSpec((1, TT, width), lambda b, t: (b, t, 0))
    prev = pl.BlockSpec((1, halo, width), lambda b, t: (b, jnp.maximum(t * per - 1, 0), 0))
    nxt = pl.BlockSpec((1, halo, width), lambda b, t: (b, jnp.minimum((t + 1) * per, nblk - 1), 0))
    return [cur, prev, nxt]


def _neighbours(cur, prev_ref, next_ref, nct, nt):
    t = pl.program_id(1)
    seg_start = jnp.logical_or(t == 0, t == nct)
    seg_end = jnp.logical_or(t == nct - 1, t == nt - 1)
    halo = prev_ref.shape[1]
    prev_row = prev_ref[0][halo - 1:halo, :].astype(F32) * jnp.where(seg_start, 0.0, 1.0)
    next_row = next_ref[0][0:1, :].astype(F32) * jnp.where(seg_end, 0.0, 1.0)
    row = lax.broadcasted_iota(jnp.int32, (TT, 1), 0)
    xm1 = jnp.where(row == 0, prev_row, pltpu.roll(cur, 1, axis=0))
    xp1 = jnp.where(row == TT - 1, next_row, pltpu.roll(cur, TT - 1, axis=0))
    return xm1, xp1


def _rwkv_prep_kernel(nct, nt, cur_ref, prev_ref, next_ref, mu_ref, w0_ref, wup_ref, a0_ref, aup_ref,
                      gup_ref, kk_ref, ka_ref, rk_ref, bd_ref,
                      r_o, kk_o, w0_o, w1_o, k0_o, k1_o, b0_o, b1_o, v_o, g_o, bon_o):
    cur = cur_ref[0]
    xm1, xp1 = _neighbours(cur, prev_ref, next_ref, nct, nt)
    p = cur + mu_ref[...] * (0.5 * (xm1 + xp1) - cur)
    c = BRANCH_W
    r, k, v = p[:, 0:c], p[:, c:2 * c], p[:, 2 * c:3 * c]
    wd = p[:, 3 * c:3 * c + 64]
    ad = p[:, 3 * c + 64:3 * c + 128]
    gd = p[:, 3 * c + 128:3 * c + 256]
    bd = bd_ref[...]
    kk = k * kk_ref[...]
    ss = _head_sums(kk * kk, bd)
    kkn = kk * lax.rsqrt(jnp.maximum(ss, 1e-24))
    twd = jnp.tanh(wd)
    ka = ka_ref[...]
    kdirs = []
    w_outs, k_outs, b_outs = (w0_o, w1_o), (k0_o, k1_o), (b0_o, b1_o)
    for d in range(2):
        wlog = w0_ref[d:d + 1, :] + jnp.dot(twd, wup_ref[d], precision=HI, preferred_element_type=F32)
        decay = -math.exp(-0.5) * jax.nn.sigmoid(wlog)
        a = jax.nn.sigmoid(a0_ref[d:d + 1, :] + jnp.dot(ad, aup_ref[d], precision=HI, preferred_element_type=F32))
        kdir = k * (1.0 + (a - 1.0) * ka)
        bdir = kkn * a
        kdirs.append(kdir)
        w_outs[d][0] = decay
        k_outs[d][0] = kdir
        b_outs[d][0] = bdir
    r_o[0] = r
    kk_o[0] = kkn
    v_o[0] = v
    g_o[0] = jnp.dot(jax.nn.sigmoid(gd), gup_ref[...], precision=HI, preferred_element_type=F32)
    rkk = r * rk_ref[...] * (kdirs[0] + kdirs[1])
    bon_o[0] = _head_sums(rkk, bd) * v


def rwkv_prep(p_rwkv, nct, mu, w0, w_up, a0, a_up, g_up, k_k, k_a, r_k, bd):
    nb, ta, _ = p_rwkv.shape
    nt = ta // TT
    c = BRANCH_W
    nat = jax.ShapeDtypeStruct((nb, ta, c), F32)
    nat_spec = pl.BlockSpec((1, TT, c), lambda b, t: (b, t, 0))

    def full(a):
        nd = a.ndim
        return pl.BlockSpec(a.shape, lambda b, t: (0,) * nd)

    consts = [mu.reshape(1, -1), w0, w_up, a0, a_up, g_up, k_k.reshape(1, -1), k_a.reshape(1, -1),
              r_k.reshape(1, -1), bd]
    return pl.pallas_call(
        functools.partial(_rwkv_prep_kernel, nct, nt),
        out_shape=[nat] * 11,
        grid=(nb, nt),
        in_specs=_halo_specs(RWKV_COLS, ta) + [full(a) for a in consts],
        out_specs=[nat_spec] * 11,
        compiler_params=_cparams(("parallel", "parallel"), VMEM_LIMIT),
        name="rwkv_prep",
    )(p_rwkv, p_rwkv, p_rwkv, *consts)


CHUNK = 32
PREP_NB = 4


def _chunk_scan_rows(x, reverse):
    pos = lax.broadcasted_iota(jnp.int32, x.shape, 0) % CHUNK
    step = 1
    while step < CHUNK:
        if reverse:
            x = x + jnp.where(pos < CHUNK - step, pltpu.roll(x, x.shape[0] - step, axis=0), 0.0)
        else:
            x = x + jnp.where(pos >= step, pltpu.roll(x, step, axis=0), 0.0)
        step *= 2
    return x


def _chunk_prep_kernel(r_ref, kk_ref, v_ref, lw0, k0, b0, lw1, k1, b1,
                       a0_o, bm0_o, rp0_o, y00_o, a1_o, bm1_o, rp1_o, y01_o):
    blk = SCAN_BLK
    npair = BRANCH_W // LANES
    nchunk = blk // CHUNK
    ti = lax.broadcasted_iota(jnp.int32, (blk, blk), 0)
    si = lax.broadcasted_iota(jnp.int32, (blk, blk), 1)
    same = (ti // CHUNK) == (si // CHUNK)
    eye = ti == si
    bd64 = (ti // HEAD_DIM) == (si // HEAD_DIM)
    head0 = si < HEAD_DIM
    lane_half = lax.broadcasted_iota(jnp.int32, (HEAD_DIM, LANES), 1)
    eyef = jnp.where(eye, 1.0, 0.0)
    dot = lambda x, y: jnp.dot(x, y, preferred_element_type=F32)
    lo = lambda x: x.astype(BF16)
    split = lambda x: jnp.concatenate([jnp.where(head0, x, 0.0), jnp.where(head0, 0.0, x)], axis=0)
    cat = lambda ms: lo(jnp.concatenate(ms, axis=1))

    probs = []
    for bi, d in [(bi, d) for bi in range(PREP_NB) for d in range(2)]:
        lw_ref, k_ref, b_ref = ((lw0, k0, b0), (lw1, k1, b1))[d]
        reverse = d == 1
        r, kk, v = r_ref[bi], kk_ref[bi], v_ref[bi]
        lw, k, b = lw_ref[bi], k_ref[bi], b_ref[bi]
        lg = _chunk_scan_rows(lw, reverse)
        lg_end = lg + _chunk_scan_rows(lw, not reverse) - lw
        g, gi, g_end = jnp.exp(lg), jnp.exp(-lg), jnp.exp(lg_end)
        to_end = jnp.exp(lg_end - lg)
        arrs = (kk * jnp.exp(lg - lw), b * gi, k * gi, r * g, v, k * to_end, b * to_end, g_end)
        incl = jnp.logical_and(same, si >= ti if reverse else si <= ti)
        strict = jnp.logical_and(same, si > ti if reverse else si < ti)
        for p in range(npair):
            lanes = slice(p * LANES, (p + 1) * LANES)
            probs.append(dict(bi=bi, d=d, p=p, incl=incl, strict=strict, arrs=tuple(a[:, lanes] for a in arrs)))

    for q in probs:
        pp_, q_, kt_, rt_ = q["arrs"][:4]
        rhs_g = lo(jnp.concatenate([q_, kt_], axis=0))
        lm, mm, n2, nn = [], [], [], []
        for hh in range(2):
            hm = head0 if hh == 0 else jnp.logical_not(head0)
            lhs_g = lo(jnp.concatenate([jnp.where(hm, pp_, 0.0), jnp.where(hm, rt_, 0.0)], axis=0))
            gm = lax.dot_general(lhs_g, rhs_g, (((1,), (1,)), ((), ())), preferred_element_type=F32)
            lm.append(jnp.where(q["strict"], gm[:blk, :blk], 0.0))
            mm.append(jnp.where(q["strict"], gm[:blk, blk:], 0.0))
            n2.append(jnp.where(q["incl"], gm[blk:, :blk], 0.0))
            nn.append(jnp.where(q["incl"], gm[blk:, blk:], 0.0))
        q["pw"], q["tm"] = lm, [eyef - lm[0], eyef - lm[1]]
        q["m_cat"], q["n2_cat"], q["nn_cat"] = cat(mm), cat(n2), cat(nn)
    for _ in range(CHUNK.bit_length() - 2):
        for q in probs:
            pwl = [lo(x) for x in q["pw"]]
            q["pw"] = [dot(x, x) for x in pwl]
        for q in probs:
            q["tm"] = [dot(lo(t), lo(eyef + x)) for t, x in zip(q["tm"], q["pw"])]
    for q in probs:
        q["t_cat"] = cat(q["tm"])
        q["v_st"] = lo(split(q["arrs"][4]))
        q["pp"] = dot(q["t_cat"], lo(split(q["arrs"][0])))
        q["mv"] = dot(q["m_cat"], q["v_st"])
    for q in probs:
        q["w2"] = dot(q["t_cat"], lo(split(q["mv"])))
        q["rp"] = q["arrs"][3] - dot(q["n2_cat"], lo(split(q["pp"])))
    for q in probs:
        q["y0"] = dot(q["nn_cat"], q["v_st"]) - dot(q["n2_cat"], lo(split(q["w2"])))
    in_chunk = [si // CHUNK == c for c in range(nchunk)]
    for q in probs:
        ppt, vt, w2t = q["pp"].T, q["arrs"][4].T, q["w2"].T
        kg_, qg_ = q["arrs"][5], q["arrs"][6]
        lhs_a = jnp.concatenate([jnp.where(cm, ppt, 0.0) for cm in in_chunk], axis=0)
        q["pq"] = dot(lo(lhs_a), lo(qg_))
        lhs_b = jnp.concatenate([jnp.concatenate([jnp.where(cm, vt, 0.0), jnp.where(cm, -w2t, 0.0)], axis=1)
                                 for cm in in_chunk], axis=0)
        q["bf"] = dot(lo(lhs_b), lo(jnp.concatenate([kg_, qg_], axis=0)))
    outs = ((a0_o, bm0_o, rp0_o, y00_o), (a1_o, bm1_o, rp1_o, y01_o))
    for q in probs:
        a_o, bm_o = outs[q["d"]][:2]
        g_end_p = q["arrs"][7]
        for c in range(nchunk):
            pq_c = q["pq"][c * blk:(c + 1) * blk]
            a_o[q["bi"], 0, c, q["p"]] = (jnp.where(eye, g_end_p[c * CHUNK:c * CHUNK + 1], 0.0)
                                          - jnp.where(bd64, pq_c, 0.0)).astype(BF16)
            bm_o[q["bi"], 0, c, q["p"]] = jnp.where(lane_half < HEAD_DIM, q["bf"][c * blk:c * blk + HEAD_DIM],
                                                    q["bf"][c * blk + HEAD_DIM:(c + 1) * blk])
    for bi in range(PREP_NB):
        for d in range(2):
            rp_o, y0_o = outs[d][2:]
            rp_o[bi] = jnp.concatenate([q["rp"] for q in probs if q["d"] == d and q["bi"] == bi], axis=1)
            y0_o[bi] = jnp.concatenate([q["y0"] for q in probs if q["d"] == d and q["bi"] == bi], axis=1)


def chunk_prep(r, kk, v, lw0, lw1, k0, k1, b0, b1):
    nb, ta, c = r.shape
    nblk = ta // SCAN_BLK
    npair = c // LANES
    nchunk = SCAN_BLK // CHUNK
    pb = PREP_NB
    assert nb % pb == 0
    nat = pl.BlockSpec((pb, SCAN_BLK, c), lambda b, s: (b, s, 0))
    a_shape = jax.ShapeDtypeStruct((nb, nblk, nchunk, npair, LANES, LANES), BF16)
    bm_shape = jax.ShapeDtypeStruct((nb, nblk, nchunk, npair, HEAD_DIM, LANES), F32)
    nat_shape = jax.ShapeDtypeStruct((nb, ta, c), F32)
    a_spec = pl.BlockSpec((pb, 1, nchunk, npair, LANES, LANES), lambda b, s: (b, s, 0, 0, 0, 0))
    bm_spec = pl.BlockSpec((pb, 1, nchunk, npair, HEAD_DIM, LANES), lambda b, s: (b, s, 0, 0, 0, 0))
    return pl.pallas_call(
        _chunk_prep_kernel,
        out_shape=[a_shape, bm_shape, nat_shape, nat_shape] * 2,
        grid=(nb // pb, nblk),
        in_specs=[nat] * 9,
        out_specs=[a_spec, bm_spec, nat, nat] * 2,
        compiler_params=_cparams(("parallel", "parallel"), VMEM_LIMIT),
        name="chunk_prep",
    )(r, kk, v, lw0, k0, b0, lw1, k1, b1)


def _chunk_scan_kernel(nb, a0, bm0, rp0, y00, a1, bm1, rp1, y01, yf_o, yb_o, s_scr):
    step = pl.program_id(0)
    npair = BRANCH_W // LANES
    nchunk = SCAN_BLK // CHUNK

    @pl.when(step == 0)
    def _():
        s_scr[...] = jnp.zeros_like(s_scr)

    lane = lax.broadcasted_iota(jnp.int32, (CHUNK, LANES), 1)
    refs = ((a0, bm0, rp0, y00, yf_o), (a1, bm1, rp1, y01, yb_o))
    for ci in range(nchunk):
        for d in range(2):
            a_ref, bm_ref, rp_ref, y0_ref, y_ref = refs[d]
            c = ci if d == 0 else nchunk - 1 - ci
            rows = slice(c * CHUNK, (c + 1) * CHUNK)
            for b in range(nb):
                for p in range(npair):
                    lanes = slice(p * LANES, (p + 1) * LANES)
                    s = s_scr[d, b, p]
                    rpc = rp_ref[b, rows, lanes]
                    lhs = jnp.concatenate([jnp.where(lane < HEAD_DIM, rpc, 0.0), jnp.where(lane >= HEAD_DIM, rpc, 0.0)],
                                          axis=0)
                    yh = lax.dot_general(lhs, s, (((1,), (1,)), ((), ())), preferred_element_type=F32)
                    y_ref[b, rows, lanes] = jnp.concatenate([yh[:CHUNK], yh[CHUNK:]], axis=1) + y0_ref[b, rows, lanes]
                    s_scr[d, b, p] = (jnp.dot(s.astype(BF16), a_ref[b, 0, c, p], preferred_element_type=F32)
                                      + bm_ref[b, 0, c, p])


def rwkv_scan(r, kk, v, lw0, lw1, k0, k1, b0, b1, lc):
    nb, ta, c = r.shape
    nblk = ta // SCAN_BLK
    nctb = lc // SCAN_BLK
    npair = c // LANES
    nchunk = SCAN_BLK // CHUNK
    a0, bm0, rp0, y00, a1, bm1, rp1, y01 = chunk_prep(r, kk, v, lw0, lw1, k0, k1, b0, b1)

    def fwd(s):
        return s

    def bwd(s):
        return jnp.where(s < nctb, nctb - 1 - s, nblk - 1 - (s - nctb))

    def specs(idx):
        return [pl.BlockSpec((nb, 1, nchunk, npair, LANES, LANES), lambda s: (0, idx(s), 0, 0, 0, 0)),
                pl.BlockSpec((nb, 1, nchunk, npair, HEAD_DIM, LANES), lambda s: (0, idx(s), 0, 0, 0, 0)),
                pl.BlockSpec((nb, SCAN_BLK, c), lambda s: (0, idx(s), 0)),
                pl.BlockSpec((nb, SCAN_BLK, c), lambda s: (0, idx(s), 0))]

    out = jax.ShapeDtypeStruct((nb, ta, c), F32)
    return pl.pallas_call(
        functools.partial(_chunk_scan_kernel, nb),
        out_shape=[out, out],
        grid=(nblk,),
        in_specs=specs(fwd) + specs(bwd),
        out_specs=[pl.BlockSpec((nb, SCAN_BLK, c), lambda s: (0, fwd(s), 0)),
                   pl.BlockSpec((nb, SCAN_BLK, c), lambda s: (0, bwd(s), 0))],
        scratch_shapes=[pltpu.VMEM((2, nb, npair, HEAD_DIM, LANES), F32)],
        compiler_params=_cparams(("arbitrary",), VMEM_LIMIT),
        name="chunk_scan",
    )(a0, bm0, rp0, y00, a1, bm1, rp1, y01)


def _rwkv_out_kernel(yf_ref, yb_ref, g_ref, bon_ref, lg_ref, lb_ref, bd_ref, o_ref):
    y = yf_ref[0] + yb_ref[0]
    bd = bd_ref[...]
    mu = _head_sums(y, bd) * (1.0 / HEAD_DIM)
    yc = y - mu
    var = _head_sums(yc * yc, bd) * (1.0 / HEAD_DIM)
    yn = yc * lax.rsqrt(var + RWKV_GN_EPS) * lg_ref[...] + lb_ref[...]
    o_ref[0] = ((yn + bon_ref[0]) * g_ref[0]).astype(BF16)


def rwkv_out(yf, yb, g, bon, lnx_g, lnx_b, bd):
    nb, ta, _ = yf.shape
    nat_spec = pl.BlockSpec((1, TT, BRANCH_W), lambda b, t: (b, t, 0))
    row = pl.BlockSpec((1, BRANCH_W), lambda b, t: (0, 0))
    return pl.pallas_call(
        _rwkv_out_kernel,
        out_shape=jax.ShapeDtypeStruct((nb, ta, BRANCH_W), BF16),
        grid=(nb, ta // TT),
        in_specs=[nat_spec] * 4 + [row, row, pl.BlockSpec(bd.shape, lambda b, t: (0, 0))],
        out_specs=nat_spec,
        compiler_params=_cparams(("parallel", "parallel")),
        name="rwkv_out",
    )(yf, yb, g, bon, lnx_g.reshape(1, -1), lnx_b.reshape(1, -1), bd)


def _pair_swap(x):
    lane = lax.broadcasted_iota(jnp.int32, x.shape, 1)
    n = x.shape[1]
    return jnp.where(lane % 2 == 0, pltpu.roll(x, n - 1, axis=1), pltpu.roll(x, 1, axis=1))


def _attn_prep_kernel(p_ref, cos_ref, sin_ref, qg_ref, kg_ref, bd_ref, q_o, k_o, v_o):
    p = p_ref[0].astype(F32)
    q, k, v = p[:, :Q_W], p[:, Q_W:Q_W + KV_W], p[:, Q_W + KV_W:]
    bd = bd_ref[...]
    cos, sin = cos_ref[...], sin_ref[...]
    qms = _head_sums(q * q, bd) * (1.0 / HEAD_DIM)
    qn = q * lax.rsqrt(qms + RMS_EPS) * qg_ref[...]
    qr = qn * cos + _pair_swap(qn) * sin
    q_o[0] = (qr * HEAD_DIM ** -0.5).astype(BF16)
    kms = _head_sums(k * k, bd[:KV_W, :KV_W]) * (1.0 / HEAD_DIM)
    kn = k * lax.rsqrt(kms + RMS_EPS) * kg_ref[...]
    kr = kn * cos[:, :KV_W] + _pair_swap(kn) * sin[:, :KV_W]
    for g in range(KV_W // HEAD_DIM):
        sl = slice(g * HEAD_DIM, (g + 1) * HEAD_DIM)
        k_o[0, g] = kr[:, sl].astype(BF16)
        v_o[0, g] = v[:, sl].astype(BF16)


def attn_prep(p_attn, cos_t, sin_t, q_norm, k_norm, bd):
    nb, ta, _ = p_attn.shape
    ng = KV_W // HEAD_DIM
    qg = jnp.tile(q_norm, Q_W // HEAD_DIM).reshape(1, -1)
    kg = jnp.tile(k_norm, ng).reshape(1, -1)
    kv_shape = jax.ShapeDtypeStruct((nb, ng, ta, HEAD_DIM), BF16)
    kv_spec = pl.BlockSpec((1, ng, TT, HEAD_DIM), lambda b, t: (b, 0, t, 0))
    return pl.pallas_call(
        _attn_prep_kernel,
        out_shape=[jax.ShapeDtypeStruct((nb, ta, Q_W), BF16), kv_shape, kv_shape],
        grid=(nb, ta // TT),
        in_specs=[pl.BlockSpec((1, TT, ATTN_COLS), lambda b, t: (b, t, 0)),
                  pl.BlockSpec((TT, Q_W), lambda b, t: (t, 0)),
                  pl.BlockSpec((TT, Q_W), lambda b, t: (t, 0)),
                  pl.BlockSpec((1, Q_W), lambda b, t: (0, 0)),
                  pl.BlockSpec((1, KV_W), lambda b, t: (0, 0)),
                  pl.BlockSpec(bd.shape, lambda b, t: (0, 0))],
        out_specs=[pl.BlockSpec((1, TT, Q_W), lambda b, t: (b, t, 0)), kv_spec, kv_spec],
        compiler_params=_cparams(("parallel", "parallel")),
        name="attn_prep",
    )(p_attn, cos_t, sin_t, qg, kg, bd)


def _attn_kernel(nct, lc, q_ref, k_ref, v_ref, o_ref):
    t = pl.program_id(1)
    ng = k_ref.shape[1]
    rep = Q_W // HEAD_DIM // ng

    def run(nk):
        outs = []
        for g in range(ng):
            kk, vv = k_ref[0, g, :nk, :], v_ref[0, g, :nk, :]
            for r in range(rep):
                h = g * rep + r
                q = q_ref[0][:, h * HEAD_DIM:(h + 1) * HEAD_DIM]
                s = lax.dot_general(q, kk, (((1,), (1,)), ((), ())), preferred_element_type=F32)
                p = jnp.exp(s - jnp.max(s, -1, keepdims=True))
                l = jnp.sum(p, -1, keepdims=True)
                o = jnp.dot(p.astype(BF16), vv, preferred_element_type=F32)
                outs.append(o / l)
        o_ref[0] = jnp.concatenate(outs, axis=1).astype(BF16)

    @pl.when(t < nct)
    def _():
        run(lc)

    @pl.when(t >= nct)
    def _():
        run(k_ref.shape[2])


def attention(q, k, v, lc):
    nb, ta, _ = q.shape
    ng = k.shape[1]
    nct = lc // TT
    qo_spec = pl.BlockSpec((1, TT, Q_W), lambda b, t: (b, t, 0))
    kv_spec = pl.BlockSpec((1, ng, ta, HEAD_DIM), lambda b, t: (b, 0, 0, 0))
    return pl.pallas_call(
        functools.partial(_attn_kernel, nct, lc),
        out_shape=jax.ShapeDtypeStruct((nb, ta, Q_W), BF16),
        grid=(nb, ta // TT),
        in_specs=[qo_spec, kv_spec, kv_spec],
        out_specs=qo_spec,
        compiler_params=_cparams(("parallel", "arbitrary"), VMEM_LIMIT),
        name="attention",
    )(q, k, v)


def _hs_pre_kernel(nct, nt, cur_ref, prev_ref, next_ref, hw_ref, sw_ref, x0_o, u_o, ycv_o):
    cur = cur_ref[0].astype(F32)
    xm1, xp1 = _neighbours(cur, prev_ref, next_ref, nct, nt)
    c = BRANCH_W
    hc = HYENA_COLS
    hw = hw_ref[...]
    ph = hw[0:1] * xm1[:, :hc] + hw[1:2] * cur[:, :hc] + hw[2:3] * xp1[:, :hc]
    x0_o[0] = ph[:, :c]
    u_o[0] = ph[:, c:2 * c] * ph[:, 2 * c:3 * c]
    sw = sw_ref[...]

    def cx(a):
        return a[:, hc + c:hc + 2 * c] * a[:, hc + 2 * c:hc + 3 * c]

    conv = sw[0:1] * cx(xm1) + sw[1:2] * cx(cur) + sw[2:3] * cx(xp1)
    ycv_o[0] = (cur[:, hc:hc + c] * conv).astype(BF16)


def hs_pre(p_hs, nct, hyena_conv, sconv_w):
    nb, ta, w = p_hs.shape
    nt = ta // TT
    nat = pl.BlockSpec((1, TT, BRANCH_W), lambda b, t: (b, t, 0))
    return pl.pallas_call(
        functools.partial(_hs_pre_kernel, nct, nt),
        out_shape=[jax.ShapeDtypeStruct((nb, ta, BRANCH_W), F32),
                   jax.ShapeDtypeStruct((nb, ta, BRANCH_W), F32),
                   jax.ShapeDtypeStruct((nb, ta, BRANCH_W), BF16)],
        grid=(nb, nt),
        in_specs=_halo_specs(w, ta, 2 * SUBLANES) + [pl.BlockSpec(hyena_conv.shape, lambda b, t: (0, 0)),
                                       pl.BlockSpec(sconv_w.shape, lambda b, t: (0, 0))],
        out_specs=[nat, nat, nat],
        compiler_params=_cparams(("parallel", "parallel"), VMEM_LIMIT),
        name="hs_pre",
    )(p_hs, p_hs, p_hs, hyena_conv, sconv_w)


EMB_PAD = 40


def _filter_tables(lh):
    n = np.arange(2 * lh)
    pos = np.abs(n - (lh - 1)).astype(np.float64)
    bands = (HYENA_EMB - 1) // 2
    t = np.minimum(pos, lh - 1) / (lh - 1)
    wpos = 2.0 * math.pi * pos / lh
    f = np.linspace(1e-4, bands - 1, bands)[:, None]
    z = np.zeros((EMB_PAD, 2 * lh), np.float32)
    z[0] = t
    z[1:1 + bands] = np.cos(f * wpos[None, :])
    z[1 + bands:1 + 2 * bands] = -np.sin(f * wpos[None, :])
    max_decay = math.log(HYENA_TARGET) / HYENA_FAST_DECAY
    min_decay = math.log(HYENA_TARGET) / HYENA_SLOW_DECAY
    deltas = np.abs(np.linspace(min_decay, max_decay, BRANCH_W)).astype(np.float32)
    return z, deltas.reshape(-1, 1)


def _filter_kernel(lh, tn, z_ref, w1_ref, b1_ref, f1_ref, w2_ref, b2_ref, f2_ref, w3_ref, dl_ref, o_ref):
    z = z_ref[...]
    h1 = jnp.sin(f1_ref[...] * (jnp.dot(w1_ref[...], z, precision=HI, preferred_element_type=F32) + b1_ref[...]))
    h2 = jnp.sin(f2_ref[...] * (jnp.dot(w2_ref[...], h1, precision=HI, preferred_element_type=F32) + b2_ref[...]))
    f = jnp.dot(w3_ref[...], h2, precision=HI, preferred_element_type=F32)
    n = pl.program_id(0) * tn + lax.broadcasted_iota(jnp.int32, (1, tn), 1)
    filt = jnp.where(n >= lh - 1, f[:BRANCH_W], f[BRANCH_W:])
    win = jnp.exp(-z[0:1, :] * dl_ref[...])
    o_ref[...] = jnp.where(n == 2 * lh - 1, 0.0, filt * win)


def hyena_filter_table(lh, w1, b1, f1, w2, b2, f2, w3):
    z_np, dl_np = _filter_tables(lh)
    n2 = 2 * lh
    tn = _pick_tile(n2, (1024, 512))
    hd = w2.shape[0]
    w1t = jnp.zeros((hd, EMB_PAD), F32).at[:, :HYENA_EMB].set(w1.T)
    args = [jnp.asarray(z_np), w1t, b1.reshape(-1, 1), f1.reshape(-1, 1), w2.T, b2.reshape(-1, 1),
            f2.reshape(-1, 1), w3.T, jnp.asarray(dl_np)]

    def full(a):
        return pl.BlockSpec(a.shape, lambda j: (0, 0))

    return pl.pallas_call(
        functools.partial(_filter_kernel, lh, tn),
        out_shape=jax.ShapeDtypeStruct((BRANCH_W, n2), F32),
        grid=(n2 // tn,),
        in_specs=[pl.BlockSpec((EMB_PAD, tn), lambda j: (0, j))] + [full(a) for a in args[1:]],
        out_specs=pl.BlockSpec((BRANCH_W, tn), lambda j: (0, j)),
        compiler_params=_cparams(("parallel",)),
        name="hyena_filter",
    )(*args)


def _hyena_conv_kernel(nblk, bp, nch, k_ref, u_ref, o_ref, t_scr):
    ntile = 4 * nblk - 1
    mc = 2 * nblk - 1
    width = (ntile + 1) * LANES
    for ch in range(nch):
        big = pltpu.roll(jnp.broadcast_to(k_ref[ch], (LANES, width)), width - (LANES - 1), 1, stride=1, stride_axis=0)
        for m in range(ntile):
            t_scr[ch, m] = big[:, m * LANES:(m + 1) * LANES].astype(BF16)

    for ch in range(nch):
        for d in [0] + [s * a for a in range(1, nblk) for s in (1, -1)]:
            m0 = 2 * d + mc
            w = jnp.concatenate([jnp.concatenate([t_scr[ch, m0], t_scr[ch, m0 + 1]], axis=1),
                                 jnp.concatenate([t_scr[ch, m0 - 1], t_scr[ch, m0]], axis=1)], axis=0)
            i0, i1 = max(0, d), min(nblk, nblk + d)
            lhs = u_ref[ch, (i0 - d) * bp:(i1 - d) * bp, :].astype(BF16)
            res = jnp.dot(lhs, w, preferred_element_type=F32)
            if d == 0:
                o_ref[ch] = res
            else:
                o_ref[ch, i0 * bp:i1 * bp, :] += res


def hyena_conv(u, ktab):
    nb, l, c = u.shape
    nblk = l // HY_BLK
    bp = -(-nb // SUBLANES) * SUBLANES
    nch = SUBLANES if nblk == 1 else 1
    ut = jnp.transpose(u.reshape(nb, nblk, HY_BLK, c), (3, 1, 0, 2))
    if bp != nb:
        ut = jnp.pad(ut, ((0, 0), (0, 0), (0, bp - nb), (0, 0)))
    ut = ut.reshape(c, nblk * bp, HY_BLK)
    k3 = ktab.reshape(c, 1, 4 * nblk * LANES)
    out = pl.pallas_call(
        functools.partial(_hyena_conv_kernel, nblk, bp, nch),
        out_shape=jax.ShapeDtypeStruct((c, nblk * bp, HY_BLK), F32),
        grid=(c // nch,),
        in_specs=[pl.BlockSpec((nch, 1, 4 * nblk * LANES), lambda ch: (ch, 0, 0)),
                  pl.BlockSpec((nch, nblk * bp, HY_BLK), lambda ch: (ch, 0, 0))],
        out_specs=pl.BlockSpec((nch, nblk * bp, HY_BLK), lambda ch: (ch, 0, 0)),
        scratch_shapes=[pltpu.VMEM((nch, 4 * nblk - 1, LANES, LANES), BF16)],
        compiler_params=_cparams(("parallel",)),
        name="hyena_conv",
    )(k3, ut)
    out = out.reshape(c, nblk, bp, HY_BLK)[:, :, :nb]
    return jnp.transpose(out, (2, 1, 3, 0)).reshape(nb, l, c)


def _route(logits, bias):
    s = jax.nn.sigmoid(logits)
    sel = s + bias
    srow = [s[e:e + 1] for e in range(N_EXPERTS)]
    row = [sel[e:e + 1] for e in range(N_EXPERTS)]
    best, gi = None, None
    for g in range(N_GROUPS):
        a, b, c, d = row[4 * g:4 * g + 4]
        hi1, lo1, hi2, lo2 = jnp.maximum(a, b), jnp.minimum(a, b), jnp.maximum(c, d), jnp.minimum(c, d)
        score = jnp.maximum(hi1, hi2) + jnp.maximum(jnp.minimum(hi1, hi2), jnp.maximum(lo1, lo2))
        if g == 0:
            best, gi = score, jnp.zeros(score.shape, jnp.int32)
        else:
            better = score > best
            gi = jnp.where(better, g, gi)
            best = jnp.where(better, score, best)
    neg = -jnp.inf
    msel = [jnp.where(gi == e // EXPERTS_PER_GROUP, row[e], neg) for e in range(N_EXPERTS)]

    def arg_first_max(vals):
        bv, bi = vals[0], jnp.zeros(vals[0].shape, jnp.int32)
        for e in range(1, N_EXPERTS):
            better = vals[e] > bv
            bi = jnp.where(better, e, bi)
            bv = jnp.where(better, vals[e], bv)
        return bi

    i1 = arg_first_max(msel)
    i2 = arg_first_max([jnp.where(i1 == e, neg, msel[e]) for e in range(N_EXPERTS)])
    w1 = sum(jnp.where(i1 == e, srow[e], 0.0) for e in range(N_EXPERTS))
    w2 = sum(jnp.where(i2 == e, srow[e], 0.0) for e in range(N_EXPERTS))
    den = w1 + w2
    g1, g2 = w1 / den, w2 / den
    rows = [jnp.where(i1 == e, g1, 0.0) + jnp.where(i2 == e, g2, 0.0) for e in range(N_EXPERTS)]
    rows.append(gi.astype(F32))
    rows.extend([jnp.zeros_like(g1)] * (ROUTE_ROWS - len(rows)))
    return jnp.concatenate(rows, axis=0)


MERGE_NB = 4


def _merge_kernel(ya_ref, x0_ref, u_ref, yc_ref, ycv_ref, yd_ref, h_ref, x_ref, *rest):
    mods, rest = rest[:3 * MERGE_NB], rest[3 * MERGE_NB:]
    skip_ref, wg_ref, wb_ref, wo_ref, g1_ref, b1_ref, rwt_ref, rb_ref = rest[:8]
    x1_o, hf_o = rest[8:10]
    gates_o = rest[10:]
    rows = MERGE_NB * TT
    flat = lambda ref: ref[...].reshape(rows, ref.shape[-1])
    yb = (flat(x0_ref) * (flat(yc_ref) + flat(u_ref) * skip_ref[...])).astype(BF16)
    ys = (flat(ya_ref), yb, flat(ycv_ref), flat(yd_ref))
    h = flat(h_ref)
    merged = None
    for n in range(N_BRANCHES):
        gate = jax.nn.sigmoid(jnp.dot(h, wg_ref[:, n * D_MODEL:(n + 1) * D_MODEL], preferred_element_type=F32))
        term = gate * jnp.dot(ys[n], wb_ref[n], preferred_element_type=F32)
        merged = term if merged is None else merged + term
    out = jnp.dot(merged.astype(BF16), wo_ref[...], preferred_element_type=F32)
    alpha = (2 * 2) ** 0.25
    for k in range(MERGE_NB):
        ga_ref, shf_ref, scf_ref = mods[3 * k:3 * k + 3]
        x1 = _ln(alpha * x_ref[k] + ga_ref[0] * out[k * TT:(k + 1) * TT]) * g1_ref[...] + b1_ref[...]
        hf = _ln(x1) * (1.0 + scf_ref[0]) + shf_ref[0]
        x1_o[k] = x1
        hf_o[k] = hf.astype(BF16)
        logits = lax.dot_general(rwt_ref[...], hf, (((1,), (1,)), ((), ())), precision=HI,
                                 preferred_element_type=F32)
        gates_o[k][...] = _route(logits, rb_ref[...])


def merge(ya, x0, u, yconv, ycv, yd, h, x_all, mod3, nct, skip, wg, wb, wo, ln_g, ln_b, rwt, rbias):
    nb, ta, d = x_all.shape
    nt = ta // TT
    mb = MERGE_NB
    assert nb % mb == 0
    nat = pl.BlockSpec((mb, TT, BRANCH_W), lambda b, t: (b, t, 0))
    wide = pl.BlockSpec((mb, TT, d), lambda b, t: (b, t, 0))

    def full(a):
        nd = a.ndim
        return pl.BlockSpec(a.shape, lambda b, t: (0,) * nd)

    def mod_spec(col, k):
        return pl.BlockSpec((1, 1, D_MODEL), lambda b, t: (jnp.where(t < nct, nb, b * mb + k), 0, col))

    mod_specs = [mod_spec(col, k) for k in range(mb) for col in (2, 3, 4)]
    consts = [skip.reshape(1, -1), wg, wb, wo, ln_g.reshape(1, -1), ln_b.reshape(1, -1), rwt, rbias.reshape(-1, 1)]
    gate_shape = jax.ShapeDtypeStruct((ROUTE_ROWS, nb // mb * ta), F32)
    gate_spec = pl.BlockSpec((ROUTE_ROWS, TT), lambda b, t: (0, b * nt + t))
    outs = pl.pallas_call(
        _merge_kernel,
        out_shape=[jax.ShapeDtypeStruct((nb, ta, d), F32), jax.ShapeDtypeStruct((nb, ta, d), BF16)] + [gate_shape] * mb,
        grid=(nb // mb, nt),
        in_specs=[nat] * 6 + [wide, wide] + mod_specs + [full(a) for a in consts],
        out_specs=[wide, wide] + [gate_spec] * mb,
        compiler_params=_cparams(("parallel", "parallel"), VMEM_LIMIT),
        name="merge",
    )(ya, x0, u, yconv, ycv, yd, h, x_all, *([mod3] * (3 * mb)), *consts)
    gates_t = jnp.stack([g.reshape(ROUTE_ROWS, nb // mb, ta) for g in outs[2:]], axis=2).reshape(ROUTE_ROWS, nb * ta)
    return outs[0], outs[1], gates_t


MOE_TILE = 1024
MOE_ALIGN = 2 * SUBLANES
MOE_CHUNK = 18 * MOE_ALIGN
MOE_SORTED = MOE_TILE + LANES
MOE_ROWS = MOE_SORTED + MOE_CHUNK
GID_ROW = N_EXPERTS
ROUTE_ROWS = 3 * SUBLANES
META_LANES = 2 * N_GROUPS
assert N_GROUPS * (MOE_ALIGN - 1) <= MOE_SORTED - MOE_TILE and MOE_ROWS % MOE_ALIGN == 0


def _moe_sort_kernel(gt_ref, g_ref, h_ref, up_ref, hs_o, gs_o, pt_o, meta_o):
    gid = gt_ref[GID_ROW:GID_ROW + 1, :]
    onehot = [jnp.where(gid == float(g), 1.0, 0.0) for g in range(N_GROUPS)]
    g4 = jnp.concatenate(onehot + [jnp.zeros((SUBLANES - N_GROUPS, MOE_TILE), F32)], axis=0)
    before = jnp.dot(g4.astype(BF16), up_ref[...], preferred_element_type=F32)
    lane = lax.broadcasted_iota(jnp.int32, (SUBLANES, LANES), 1)
    meta = jnp.zeros((SUBLANES, LANES), F32)
    off = jnp.zeros((1, 1), F32)
    pos = jnp.zeros((1, MOE_TILE), F32)
    for g in range(N_GROUPS):
        cnt = jnp.sum(onehot[g], axis=1, keepdims=True)
        pos = pos + onehot[g] * (before[g:g + 1] + off)
        meta = jnp.where(lane == g, off, meta)
        meta = jnp.where(lane == N_GROUPS + g, cnt, meta)
        off = off + jnp.ceil(cnt * (1.0 / MOE_ALIGN)) * MOE_ALIGN
    meta_o[0] = meta.astype(jnp.int32)
    row = lax.broadcasted_iota(jnp.int32, (MOE_SORTED, MOE_TILE), 0)
    place = jnp.where(row == pos.astype(jnp.int32), 1.0, 0.0)
    p16 = place.astype(BF16)
    hs_o[0, :MOE_SORTED, :] = jnp.dot(p16, h_ref[...], preferred_element_type=F32).astype(BF16)
    hs_o[0, MOE_SORTED:, :] = jnp.zeros((MOE_ROWS - MOE_SORTED, h_ref.shape[1]), BF16)
    gts = g_ref[...]
    hi = gts.astype(BF16)
    r1 = gts - hi.astype(F32)
    mid = r1.astype(BF16)
    low = (r1 - mid.astype(F32)).astype(BF16)
    gs_o[0, :MOE_SORTED, :] = (jnp.dot(p16, hi, preferred_element_type=F32)
                               + jnp.dot(p16, mid, preferred_element_type=F32)
                               + jnp.dot(p16, low, preferred_element_type=F32))
    gs_o[0, MOE_SORTED:, :] = jnp.zeros((MOE_ROWS - MOE_SORTED, LANES), F32)
    pt_o[...] = place.T.astype(BF16)


def moe_sort(hf, gates_t, gates):
    n, d = hf.shape
    ntile = n // MOE_TILE
    upper = jnp.asarray(np.triu(np.ones((MOE_TILE, MOE_TILE), np.float32), 1), BF16)
    return pl.pallas_call(
        _moe_sort_kernel,
        out_shape=[jax.ShapeDtypeStruct((ntile, MOE_ROWS, d), BF16),
                   jax.ShapeDtypeStruct((ntile, MOE_ROWS, LANES), F32),
                   jax.ShapeDtypeStruct((n, MOE_SORTED), BF16),
                   jax.ShapeDtypeStruct((ntile, SUBLANES, LANES), jnp.int32)],
        grid=(ntile,),
        in_specs=[pl.BlockSpec((ROUTE_ROWS, MOE_TILE), lambda i: (0, i)),
                  pl.BlockSpec((MOE_TILE, LANES), lambda i: (i, 0)),
                  pl.BlockSpec((MOE_TILE, d), lambda i: (i, 0)),
                  pl.BlockSpec((MOE_TILE, MOE_TILE), lambda i: (0, 0))],
        out_specs=[pl.BlockSpec((1, MOE_ROWS, d), lambda i: (i, 0, 0)),
                   pl.BlockSpec((1, MOE_ROWS, LANES), lambda i: (i, 0, 0)),
                   pl.BlockSpec((MOE_TILE, MOE_SORTED), lambda i: (i, 0)),
                   pl.BlockSpec((1, SUBLANES, LANES), lambda i: (i, 0, 0))],
        compiler_params=_cparams(("parallel",), VMEM_LIMIT),
        name="moe_sort",
    )(gates_t, gates, hf, upper)


def _moe_group_kernel(meta_ref, hs_ref, gs_ref, w1_ref, w3_ref, w2_ref, prev_ref, ys_o):
    g, i = pl.program_id(0), pl.program_id(1)

    @pl.when(g == 0)
    def _():
        ys_o[...] = jnp.zeros_like(ys_o)

    @pl.when(g > 0)
    def _():
        ys_o[...] = prev_ref[...]

    off = meta_ref[i * META_LANES + g]
    cnt = meta_ref[i * META_LANES + N_GROUPS + g]
    lane = lax.broadcasted_iota(jnp.int32, (MOE_CHUNK, LANES), 1)

    def chunk(j, carry):
        rows = pl.ds(pl.multiple_of(off + j * MOE_CHUNK, MOE_ALIGN), MOE_CHUNK)
        hs = hs_ref[0, rows, :]
        gs = gs_ref[0, rows, :]
        acc = ys_o[0, rows, :].astype(F32)
        for e in range(EXPERTS_PER_GROUP):
            a = jnp.dot(hs, w1_ref[e], preferred_element_type=F32)
            b = jnp.dot(hs, w3_ref[e], preferred_element_type=F32)
            act = (a * jax.nn.sigmoid(a)) * b
            gcol = jnp.sum(jnp.where(lane == g * EXPERTS_PER_GROUP + e, gs, 0.0), axis=1, keepdims=True)
            acc = acc + gcol * jnp.dot(act.astype(BF16), w2_ref[e], preferred_element_type=F32)
        ys_o[0, rows, :] = acc.astype(BF16)
        return carry

    lax.fori_loop(0, lax.div(cnt + (MOE_CHUNK - 1), MOE_CHUNK), chunk, 0)


def moe_group(hs, gs, meta, w1, w3, w2, carried=None):
    ntile, _, d = hs.shape
    de = w1.shape[2]
    epg = EXPERTS_PER_GROUP
    tile_spec = pl.BlockSpec((1, MOE_ROWS, d), lambda g, i, m: (i, 0, 0))
    grid_spec = pltpu.PrefetchScalarGridSpec(
        num_scalar_prefetch=1,
        grid=(N_GROUPS, ntile),
        in_specs=[tile_spec,
                  pl.BlockSpec((1, MOE_ROWS, LANES), lambda g, i, m: (i, 0, 0)),
                  pl.BlockSpec((epg, d, de), lambda g, i, m: (g, 0, 0)),
                  pl.BlockSpec((epg, d, de), lambda g, i, m: (g, 0, 0)),
                  pl.BlockSpec((epg, de, d), lambda g, i, m: (g, 0, 0)),
                  tile_spec],
        out_specs=tile_spec,
    )
    if carried is None:
        carried = jnp.zeros((ntile, MOE_ROWS, d), BF16)
    return pl.pallas_call(
        _moe_group_kernel,
        out_shape=jax.ShapeDtypeStruct((ntile, MOE_ROWS, d), BF16),
        grid_spec=grid_spec,
        input_output_aliases={6: 0},
        compiler_params=_cparams(("arbitrary", "arbitrary"), VMEM_LIMIT),
        name="moe_group",
    )(meta, hs, gs, w1, w3, w2, carried)


def _ln2_kernel(emit_next, x_ref, pt_ref, ys_ref, gf_ref, g_ref, b_ref, *rest):
    alpha = (2 * 2) ** 0.25
    f = jnp.dot(pt_ref[...], ys_ref[0], preferred_element_type=F32)
    x2 = _ln(alpha * x_ref[0] + gf_ref[0] * f) * g_ref[...] + b_ref[...]
    if emit_next:
        sh_ref, sc_ref, o_ref, h_o = rest
        h_o[0] = (_ln(x2) * (1.0 + sc_ref[0]) + sh_ref[0]).astype(BF16)
    else:
        o_ref, = rest
    o_ref[0] = x2


def ln2(x1, pt, ysb, mod3, nct, ln_g, ln_b, mod3_next):
    nb, ta, d = x1.shape
    nt = ta // TT
    per = MOE_TILE // TT
    wide = pl.BlockSpec((1, TT, d), lambda b, t: (b, t, 0))
    row = pl.BlockSpec((1, d), lambda b, t: (0, 0))
    in_specs = [wide, pl.BlockSpec((TT, MOE_SORTED), lambda b, t: (b * nt + t, 0)),
                pl.BlockSpec((1, MOE_SORTED, d), lambda b, t: ((b * nt + t) // per, 0, 0)),
                _mod_spec(5, nct, nb), row, row]
    args = [x1, pt, ysb, mod3, ln_g.reshape(1, -1), ln_b.reshape(1, -1)]
    if mod3_next is None:
        out_shape = jax.ShapeDtypeStruct((nb, ta - nct * TT, d), F32)
        out_specs = pl.BlockSpec((1, TT, d), lambda b, t: (b, jnp.maximum(t - nct, 0), 0))
    else:
        in_specs += [_mod_spec(0, nct, nb), _mod_spec(1, nct, nb)]
        args += [mod3_next, mod3_next]
        out_shape = [jax.ShapeDtypeStruct((nb, ta, d), F32), jax.ShapeDtypeStruct((nb, ta, d), BF16)]
        out_specs = [wide, wide]
    return pl.pallas_call(
        functools.partial(_ln2_kernel, mod3_next is not None),
        out_shape=out_shape,
        grid=(nb, nt),
        in_specs=in_specs,
        out_specs=out_specs,
        compiler_params=_cparams(("parallel", "arbitrary"), VMEM_LIMIT),
        name="ln2",
    )(*args)


def _rope_tables(l, lc):
    half = HEAD_DIM // 2
    inv = ROPE_THETA ** (-np.arange(0, half, 2, dtype=np.float64) / half)
    t = np.arange(l)
    rows, cols = t // GRID_W, t % GRID_W
    ang = np.concatenate([rows[:, None] * inv, cols[:, None] * inv], -1)
    ang = np.concatenate([np.zeros((lc, half)), ang], 0)
    cos = np.repeat(np.cos(ang), 2, axis=1)
    sin = np.repeat(np.sin(ang), 2, axis=1)
    sin[:, 0::2] *= -1.0
    reps = Q_W // HEAD_DIM
    return (jnp.asarray(np.tile(cos, (1, reps)), F32), jnp.asarray(np.tile(sin, (1, reps)), F32))


def _block_diag_ones():
    i = np.arange(BRANCH_W) // HEAD_DIM
    return jnp.asarray((i[:, None] == i[None, :]).astype(np.float32))


def kernel(x, c, ctx, c_ctx, ada_w, ada_b, w_in, rwkv_mu, rwkv_w0, rwkv_w_up, rwkv_a0, rwkv_a_up, rwkv_g_up, rwkv_k_k, rwkv_k_a, rwkv_r_k, rwkv_lnx_g, rwkv_lnx_b, hyena_conv, hyena_w1, hyena_b1, hyena_freq1, hyena_w2, hyena_b2, hyena_freq2, hyena_w3, hyena_skip, sconv_w, attn_q_norm, attn_k_norm, w_branch, w_out, ln1_g, ln1_b, ln2_g, ln2_b, router_w, router_bias, exp_w1, exp_w3, exp_w2):
    nb, l, d = x.shape
    lc = ctx.shape[1]
    depth = ada_w.shape[0]
    assert d == D_MODEL and lc % TT == 0 and l % TT == 0 and l % GRID_W == 0 and (nb * (lc + l)) % MOE_TILE == 0
    ta = lc + l
    nct = lc // TT

    mod_rows = -(-(nb + 1) // SUBLANES) * SUBLANES
    cc = jnp.zeros((mod_rows, d), F32).at[:nb].set(c).at[nb].set(c_ctx)
    mod = ada_mod(cc, ada_w, ada_b)

    cos_t, sin_t = _rope_tables(l, lc)
    bd = _block_diag_ones()
    rwt = router_w.T

    ysb = None
    for li in range(depth):
        mod3 = mod[li].reshape(mod_rows, 1, N_MOD * d)
        wl = w_in[li].astype(BF16)
        if li == 0:
            x_all, h3 = lnmod(ctx, x, mod3, nct, 0, 1)
        h = h3.reshape(nb * ta, d)
        p_rwkv = matmul(h, wl[:, :OFF_HYENA]).reshape(nb, ta, -1)
        p_hs = matmul(h, wl[:, OFF_HYENA:OFF_ATTN], BF16).reshape(nb, ta, -1)
        p_attn = matmul(h, wl[:, OFF_ATTN:OFF_GATE], BF16).reshape(nb, ta, -1)

        r, kk, w0, w1, k0, k1, b0, b1, v, g, bon = rwkv_prep(
            p_rwkv, nct, rwkv_mu[li], rwkv_w0[li], rwkv_w_up[li], rwkv_a0[li], rwkv_a_up[li], rwkv_g_up[li],
            rwkv_k_k[li], rwkv_k_a[li], rwkv_r_k[li], bd)
        yf, yb = rwkv_scan(r, kk, v, w0, w1, k0, k1, b0, b1, lc)
        ya = rwkv_out(yf, yb, g, bon, rwkv_lnx_g[li], rwkv_lnx_b[li], bd)

        q, kx, vx = attn_prep(p_attn, cos_t, sin_t, attn_q_norm[li], attn_k_norm[li], bd)
        yd = attention(q, kx, vx, lc)

        x0, u, ycv = hs_pre(p_hs, nct, hyena_conv[li], sconv_w[li])
        fargs = (hyena_w1[li], hyena_b1[li], hyena_freq1[li], hyena_w2[li], hyena_b2[li], hyena_freq2[li],
                 hyena_w3[li])
        yconv_ctx = hyena_conv_seg(u[:, :lc], fargs) if li < depth - 1 else jnp.zeros((nb, lc, BRANCH_W), F32)
        yconv = jnp.concatenate([yconv_ctx, hyena_conv_seg(u[:, lc:], fargs)], axis=1)

        x1, hf, gates_t = merge(ya, x0, u, yconv, ycv, yd, h3, x_all, mod3, nct, hyena_skip[li], wl[:, OFF_GATE:],
                                w_branch[li].astype(BF16), w_out[li].astype(BF16), ln1_g[li], ln1_b[li],
                                rwt, router_bias)
        gates = jnp.pad(gates_t.T, ((0, 0), (0, LANES - ROUTE_ROWS)))
        hs, gs, pt, meta = moe_sort(hf.reshape(nb * ta, d), gates_t, gates)
        ysb = moe_group(hs, gs, meta[:, 0, :META_LANES].reshape(-1), exp_w1[li].astype(BF16),
                        exp_w3[li].astype(BF16), exp_w2[li].astype(BF16), carried=ysb)
        if li == depth - 1:
            return ln2(x1, pt, ysb, mod3, nct, ln2_g[li], ln2_b[li], None)
        x_all, h3 = ln2(x1, pt, ysb, mod3, nct, ln2_g[li], ln2_b[li], mod[li + 1].reshape(mod_rows, 1, N_MOD * d))


def hyena_conv_seg(u_seg, fargs):
    ktab = hyena_filter_table(u_seg.shape[1], *fargs)
    return hyena_conv(u_seg, ktab)
```

```python
import functools
import math

import numpy as np
import jax
import jax.numpy as jnp
from jax import lax
from jax.experimental import pallas as pl
from jax.experimental.pallas import tpu as pltpu

F32 = jnp.float32
BF16 = jnp.bfloat16
HI = lax.Precision.HIGHEST

D_MODEL = 1024
GRID_W = 64
BRANCH_W = 256
HEAD_DIM = 64
N_BRANCHES = 4
N_MOD = 6
RWKV_HEADS = 4
RWKV_COLS = 1024
RWKV_GN_EPS = 64e-5
HYENA_COLS = 768
HYENA_EMB = 33
HYENA_FAST_DECAY = 0.3
HYENA_SLOW_DECAY = 1.5
HYENA_TARGET = 1e-2
SCONV_COLS = 768
Q_W = 256
KV_W = 128
ATTN_COLS = 512
ROPE_THETA = 10000.0
RMS_EPS = 1e-6
OFF_HYENA = RWKV_COLS
OFF_SCONV = OFF_HYENA + HYENA_COLS
OFF_ATTN = OFF_SCONV + SCONV_COLS
OFF_GATE = OFF_ATTN + ATTN_COLS
N_EXPERTS = 16
N_GROUPS = 4
EXPERTS_PER_GROUP = 4
D_EXPERT = 512
LN_EPS = 1e-6

SUBLANES = 8
LANES = 128
TT = 256
SCAN_BLK = LANES
HY_BLK = 256
VMEM_LIMIT = 56 * 1024 * 1024


def _cparams(sem, vmem=None):
    return pltpu.CompilerParams(dimension_semantics=sem, vmem_limit_bytes=vmem)


def _ln(xf):
    mu = jnp.mean(xf, -1, keepdims=True)
    xc = xf - mu
    var = jnp.mean(xc * xc, -1, keepdims=True)
    return xc * lax.rsqrt(var + LN_EPS)


def _head_sums(x, ones_bd):
    ones16 = ones_bd.astype(BF16)
    hi = x.astype(BF16)
    lo = (x - hi.astype(F32)).astype(BF16)
    return jnp.dot(hi, ones16, preferred_element_type=F32) + jnp.dot(lo, ones16, preferred_element_type=F32)


def _pick_tile(n, cands):
    for c in cands:
        if n % c == 0:
            return c
    raise ValueError(f"no tile for {n}")


def _ada_kernel(c_ref, w_ref, b_ref, o_ref):
    c = c_ref[...]
    a = c * jax.nn.sigmoid(c)
    o_ref[0] = jnp.dot(a, w_ref[0], precision=HI, preferred_element_type=F32) + b_ref[0]


def ada_mod(cc, ada_w, ada_b):
    depth, d, n = ada_w.shape
    rows = cc.shape[0]
    return pl.pallas_call(
        _ada_kernel,
        out_shape=jax.ShapeDtypeStruct((depth, rows, n), F32),
        grid=(depth, n // d),
        in_specs=[pl.BlockSpec((rows, d), lambda l, j: (0, 0)),
                  pl.BlockSpec((1, d, d), lambda l, j: (l, 0, j)),
                  pl.BlockSpec((1, 1, d), lambda l, j: (l, 0, j))],
        out_specs=pl.BlockSpec((1, rows, d), lambda l, j: (l, 0, j)),
        compiler_params=_cparams(("parallel", "parallel"), VMEM_LIMIT),
        name="ada_mod",
    )(cc, ada_w, ada_b.reshape(depth, 1, n))


def _lnmod_kernel(nct, c_ref, x_ref, sh_ref, sc_ref, xa_o, h_o):
    t = pl.program_id(1)
    xin = jnp.where(t < nct, c_ref[0], x_ref[0])
    xa_o[0] = xin
    h_o[0] = (_ln(xin) * (1.0 + sc_ref[0]) + sh_ref[0]).astype(BF16)


def _mod_spec(col, nct, nb):
    return pl.BlockSpec((1, 1, D_MODEL), lambda b, t: (jnp.where(t < nct, nb, b), 0, col))


def lnmod(ctx, x, mod3, nct, col_shift, col_scale):
    nb, l, d = x.shape
    ta = ctx.shape[1] + l
    wide = pl.BlockSpec((1, TT, d), lambda b, t: (b, t, 0))
    return pl.pallas_call(
        functools.partial(_lnmod_kernel, nct),
        out_shape=[jax.ShapeDtypeStruct((nb, ta, d), F32), jax.ShapeDtypeStruct((nb, ta, d), BF16)],
        grid=(nb, ta // TT),
        in_specs=[pl.BlockSpec((1, TT, d), lambda b, t: (b, jnp.minimum(t, nct - 1), 0)),
                  pl.BlockSpec((1, TT, d), lambda b, t: (b, jnp.maximum(t - nct, 0), 0)),
                  _mod_spec(col_shift, nct, nb), _mod_spec(col_scale, nct, nb)],
        out_specs=[wide, wide],
        compiler_params=_cparams(("parallel", "arbitrary")),
        name="lnmod",
    )(ctx, x, mod3, mod3)


def _mm_kernel(a_ref, b_ref, o_ref):
    o_ref[...] = jnp.dot(a_ref[...], b_ref[...], preferred_element_type=F32).astype(o_ref.dtype)


def matmul(a, b, out_dtype=F32):
    m, k = a.shape
    _, n = b.shape
    tm = _pick_tile(m, (1024, 512, 256))
    tn = _pick_tile(n, (1024, 512, 256))
    return pl.pallas_call(
        _mm_kernel,
        out_shape=jax.ShapeDtypeStruct((m, n), out_dtype),
        grid=(m // tm, n // tn),
        in_specs=[pl.BlockSpec((tm, k), lambda i, j: (i, 0)),
                  pl.BlockSpec((k, tn), lambda i, j: (0, j))],
        out_specs=pl.BlockSpec((tm, tn), lambda i, j: (i, j)),
        compiler_params=_cparams(("parallel", "parallel"), VMEM_LIMIT),
        name="matmul",
    )(a, b)


def _halo_specs(width, ta, halo=SUBLANES):
    nblk = ta // halo
    per = TT // halo
    cur = pl.BlockSpec((1, TT, width), lambda b, t: (b, t, 0))
    prev = pl.BlockSpec((1, halo, width), lambda b, t: (b, jnp.maximum(t * per - 1, 0), 0))
    nxt = pl.BlockSpec((1, halo, width), lambda b, t: (b, jnp.minimum((t + 1) * per, nblk - 1), 0))
    return [cur, prev, nxt]


def _neighbours(cur, prev_ref, next_ref, nct, nt):
    t = pl.program_id(1)
    seg_start = jnp.logical_or(t == 0, t == nct)
    seg_end = jnp.logical_or(t == nct - 1, t == nt - 1)
    halo = prev_ref.shape[1]
    prev_row = prev_ref[0][halo - 1:halo, :].astype(F32) * jnp.where(seg_start, 0.0, 1.0)
    next_row = next_ref[0][0:1, :].astype(F32) * jnp.where(seg_end, 0.0, 1.0)
    row = lax.broadcasted_iota(jnp.int32, (TT, 1), 0)
    xm1 = jnp.where(row == 0, prev_row, pltpu.roll(cur, 1, axis=0))
    xp1 = jnp.where(row == TT - 1, next_row, pltpu.roll(cur, TT - 1, axis=0))
    return xm1, xp1


def _rwkv_prep_kernel(nct, nt, cur_ref, prev_ref, next_ref, mu_ref, w0_ref, wup_ref, a0_ref, aup_ref,
                      gup_ref, kk_ref, ka_ref, rk_ref, bd_ref,
                      r_o, kk_o, w0_o, w1_o, k0_o, k1_o, b0_o, b1_o, v_o, g_o, bon_o):
    cur = cur_ref[0]
    xm1, xp1 = _neighbours(cur, prev_ref, next_ref, nct, nt)
    p = cur + mu_ref[...] * (0.5 * (xm1 + xp1) - cur)
    c = BRANCH_W
    r, k, v = p[:, 0:c], p[:, c:2 * c], p[:, 2 * c:3 * c]
    wd = p[:, 3 * c:3 * c + 64]
    ad = p[:, 3 * c + 64:3 * c + 128]
    gd = p[:, 3 * c + 128:3 * c + 256]
    bd = bd_ref[...]
    kk = k * kk_ref[...]
    ss = _head_sums(kk * kk, bd)
    kkn = kk * lax.rsqrt(jnp.maximum(ss, 1e-24))
    twd = jnp.tanh(wd)
    ka = ka_ref[...]
    kdirs = []
    w_outs, k_outs, b_outs = (w0_o, w1_o), (k0_o, k1_o), (b0_o, b1_o)
    for d in range(2):
        wlog = w0_ref[d:d + 1, :] + jnp.dot(twd, wup_ref[d], precision=HI, preferred_element_type=F32)
        decay = -math.exp(-0.5) * jax.nn.sigmoid(wlog)
        a = jax.nn.sigmoid(a0_ref[d:d + 1, :] + jnp.dot(ad, aup_ref[d], precision=HI, preferred_element_type=F32))
        kdir = k * (1.0 + (a - 1.0) * ka)
        bdir = kkn * a
        kdirs.append(kdir)
        w_outs[d][0] = decay
        k_outs[d][0] = kdir.astype(BF16)
        b_outs[d][0] = bdir.astype(BF16)
    r_o[0] = r.astype(BF16)
    kk_o[0] = kkn.astype(BF16)
    v_o[0] = v.astype(BF16)
    g_o[0] = jnp.dot(jax.nn.sigmoid(gd), gup_ref[...], precision=HI, preferred_element_type=F32)
    rkk = r * rk_ref[...] * (kdirs[0] + kdirs[1])
    bon_o[0] = _head_sums(rkk, bd) * v


def rwkv_prep(p_rwkv, nct, mu, w0, w_up, a0, a_up, g_up, k_k, k_a, r_k, bd):
    nb, ta, _ = p_rwkv.shape
    nt = ta // TT
    c = BRANCH_W
    nat = jax.ShapeDtypeStruct((nb, ta, c), F32)
    lo = jax.ShapeDtypeStruct((nb, ta, c), BF16)
    nat_spec = pl.BlockSpec((1, TT, c), lambda b, t: (b, t, 0))

    def full(a):
        nd = a.ndim
        return pl.BlockSpec(a.shape, lambda b, t: (0,) * nd)

    consts = [mu.reshape(1, -1), w0, w_up, a0, a_up, g_up, k_k.reshape(1, -1), k_a.reshape(1, -1),
              r_k.reshape(1, -1), bd]
    return pl.pallas_call(
        functools.partial(_rwkv_prep_kernel, nct, nt),
        out_shape=[lo, lo, nat, nat, lo, lo, lo, lo, lo, nat, nat],
        grid=(nb, nt),
        in_specs=_halo_specs(RWKV_COLS, ta) + [full(a) for a in consts],
        out_specs=[nat_spec] * 11,
        compiler_params=_cparams(("parallel", "parallel"), VMEM_LIMIT),
        name="rwkv_prep",
    )(p_rwkv, p_rwkv, p_rwkv, *consts)


CHUNK = 32
PREP_NB = 4


def _chunk_scan_rows(x, reverse):
    pos = lax.broadcasted_iota(jnp.int32, x.shape, 0) % CHUNK
    step = 1
    while step < CHUNK:
        if reverse:
            x = x + jnp.where(pos < CHUNK - step, pltpu.roll(x, x.shape[0] - step, axis=0), 0.0)
        else:
            x = x + jnp.where(pos >= step, pltpu.roll(x, step, axis=0), 0.0)
        step *= 2
    return x


def _chunk_prep_kernel(r_ref, kk_ref, v_ref, lw0, k0, b0, lw1, k1, b1,
                       a0_o, bm0_o, rp0_o, y00_o, a1_o, bm1_o, rp1_o, y01_o):
    blk = SCAN_BLK
    npair = BRANCH_W // LANES
    nchunk = blk // CHUNK
    ti = lax.broadcasted_iota(jnp.int32, (blk, blk), 0)
    si = lax.broadcasted_iota(jnp.int32, (blk, blk), 1)
    same = (ti // CHUNK) == (si // CHUNK)
    eye = ti == si
    bd64 = (ti // HEAD_DIM) == (si // HEAD_DIM)
    head0 = si < HEAD_DIM
    lane_half = lax.broadcasted_iota(jnp.int32, (HEAD_DIM, LANES), 1)
    eyef = jnp.where(eye, 1.0, 0.0)
    dot = lambda x, y: jnp.dot(x, y, preferred_element_type=F32)
    lo = lambda x: x.astype(BF16)
    split = lambda x: jnp.concatenate([jnp.where(head0, x, 0.0), jnp.where(head0, 0.0, x)], axis=0)
    cat = lambda ms: lo(jnp.concatenate(ms, axis=1))

    probs = []
    for bi, d in [(bi, d) for bi in range(PREP_NB) for d in range(2)]:
        lw_ref, k_ref, b_ref = ((lw0, k0, b0), (lw1, k1, b1))[d]
        reverse = d == 1
        r, kk, v = (a[bi].astype(F32) for a in (r_ref, kk_ref, v_ref))
        lw, k, b = lw_ref[bi], k_ref[bi].astype(F32), b_ref[bi].astype(F32)
        lg = _chunk_scan_rows(lw, reverse)
        lg_end = lg + _chunk_scan_rows(lw, not reverse) - lw
        g, gi, g_end = jnp.exp(lg), jnp.exp(-lg), jnp.exp(lg_end)
        to_end = jnp.exp(lg_end - lg)
        arrs = (kk * jnp.exp(lg - lw), b * gi, k * gi, r * g, v, k * to_end, b * to_end, g_end)
        incl = jnp.logical_and(same, si >= ti if reverse else si <= ti)
        strict = jnp.logical_and(same, si > ti if reverse else si < ti)
        for p in range(npair):
            lanes = slice(p * LANES, (p + 1) * LANES)
            probs.append(dict(bi=bi, d=d, p=p, incl=incl, strict=strict, arrs=tuple(a[:, lanes] for a in arrs)))

    for q in probs:
        pp_, q_, kt_, rt_ = q["arrs"][:4]
        rhs_g = lo(jnp.concatenate([q_, kt_], axis=0))
        lm, mm, n2, nn = [], [], [], []
        for hh in range(2):
            hm = head0 if hh == 0 else jnp.logical_not(head0)
            lhs_g = lo(jnp.concatenate([jnp.where(hm, pp_, 0.0), jnp.where(hm, rt_, 0.0)], axis=0))
            gm = lax.dot_general(lhs_g, rhs_g, (((1,), (1,)), ((), ())), preferred_element_type=F32)
            lm.append(jnp.where(q["strict"], gm[:blk, :blk], 0.0))
            mm.append(jnp.where(q["strict"], gm[:blk, blk:], 0.0))
            n2.append(jnp.where(q["incl"], gm[blk:, :blk], 0.0))
            nn.append(jnp.where(q["incl"], gm[blk:, blk:], 0.0))
        q["pw"], q["tm"] = lm, [eyef - lm[0], eyef - lm[1]]
        q["m_cat"], q["n2_cat"], q["nn_cat"] = cat(mm), cat(n2), cat(nn)
    for _ in range(CHUNK.bit_length() - 2):
        for q in probs:
            pwl = [lo(x) for x in q["pw"]]
            q["pw"] = [dot(x, x) for x in pwl]
        for q in probs:
            q["tm"] = [dot(lo(t), lo(eyef + x)) for t, x in zip(q["tm"], q["pw"])]
    for q in probs:
        q["t_cat"] = cat(q["tm"])
        q["v_st"] = lo(split(q["arrs"][4]))
        q["pp"] = dot(q["t_cat"], lo(split(q["arrs"][0])))
        q["mv"] = dot(q["m_cat"], q["v_st"])
    for q in probs:
        q["w2"] = dot(q["t_cat"], lo(split(q["mv"])))
        q["rp"] = q["arrs"][3] - dot(q["n2_cat"], lo(split(q["pp"])))
    for q in probs:
        q["y0"] = dot(q["nn_cat"], q["v_st"]) - dot(q["n2_cat"], lo(split(q["w2"])))
    in_chunk = [si // CHUNK == c for c in range(nchunk)]
    for q in probs:
        ppt, vt, w2t = q["pp"].T, q["arrs"][4].T, q["w2"].T
        kg_, qg_ = q["arrs"][5], q["arrs"][6]
        lhs_a = jnp.concatenate([jnp.where(cm, ppt, 0.0) for cm in in_chunk], axis=0)
        q["pq"] = dot(lo(lhs_a), lo(qg_))
        lhs_b = jnp.concatenate([jnp.concatenate([jnp.where(cm, vt, 0.0), jnp.where(cm, -w2t, 0.0)], axis=1)
                                 for cm in in_chunk], axis=0)
        q["bf"] = dot(lo(lhs_b), lo(jnp.concatenate([kg_, qg_], axis=0)))
    outs = ((a0_o, bm0_o, rp0_o, y00_o), (a1_o, bm1_o, rp1_o, y01_o))
    for q in probs:
        a_o, bm_o = outs[q["d"]][:2]
        g_end_p = q["arrs"][7]
        for c in range(nchunk):
            pq_c = q["pq"][c * blk:(c + 1) * blk]
            a_o[q["bi"], 0, c, q["p"]] = (jnp.where(eye, g_end_p[c * CHUNK:c * CHUNK + 1], 0.0)
                                          - jnp.where(bd64, pq_c, 0.0)).astype(BF16)
            bm_o[q["bi"], 0, c, q["p"]] = jnp.where(lane_half < HEAD_DIM, q["bf"][c * blk:c * blk + HEAD_DIM],
                                                    q["bf"][c * blk + HEAD_DIM:(c + 1) * blk])
    for bi in range(PREP_NB):
        for d in range(2):
            rp_o, y0_o = outs[d][2:]
            rp_o[bi] = jnp.concatenate([q["rp"] for q in probs if q["d"] == d and q["bi"] == bi], axis=1)
            y0_o[bi] = jnp.concatenate([q["y0"] for q in probs if q["d"] == d and q["bi"] == bi], axis=1)


def chunk_prep(r, kk, v, lw0, lw1, k0, k1, b0, b1):
    nb, ta, c = r.shape
    nblk = ta // SCAN_BLK
    npair = c // LANES
    nchunk = SCAN_BLK // CHUNK
    pb = PREP_NB
    assert nb % pb == 0
    nat = pl.BlockSpec((pb, SCAN_BLK, c), lambda b, s: (b, s, 0))
    a_shape = jax.ShapeDtypeStruct((nb, nblk, nchunk, npair, LANES, LANES), BF16)
    bm_shape = jax.ShapeDtypeStruct((nb, nblk, nchunk, npair, HEAD_DIM, LANES), F32)
    nat_shape = jax.ShapeDtypeStruct((nb, ta, c), F32)
    a_spec = pl.BlockSpec((pb, 1, nchunk, npair, LANES, LANES), lambda b, s: (b, s, 0, 0, 0, 0))
    bm_spec = pl.BlockSpec((pb, 1, nchunk, npair, HEAD_DIM, LANES), lambda b, s: (b, s, 0, 0, 0, 0))
    return pl.pallas_call(
        _chunk_prep_kernel,
        out_shape=[a_shape, bm_shape, nat_shape, nat_shape] * 2,
        grid=(nb // pb, nblk),
        in_specs=[nat] * 9,
        out_specs=[a_spec, bm_spec, nat, nat] * 2,
        compiler_params=_cparams(("parallel", "parallel"), VMEM_LIMIT),
        name="chunk_prep",
    )(r, kk, v, lw0, k0, b0, lw1, k1, b1)


def _chunk_scan_kernel(nb, a0, bm0, rp0, y00, a1, bm1, rp1, y01, yf_o, yb_o, s_scr):
    step = pl.program_id(0)
    npair = BRANCH_W // LANES
    nchunk = SCAN_BLK // CHUNK

    @pl.when(step == 0)
    def _():
        s_scr[...] = jnp.zeros_like(s_scr)

    lane = lax.broadcasted_iota(jnp.int32, (CHUNK, LANES), 1)
    refs = ((a0, bm0, rp0, y00, yf_o), (a1, bm1, rp1, y01, yb_o))
    for ci in range(nchunk):
        for d in range(2):
            a_ref, bm_ref, rp_ref, y0_ref, y_ref = refs[d]
            c = ci if d == 0 else nchunk - 1 - ci
            rows = slice(c * CHUNK, (c + 1) * CHUNK)
            for b in range(nb):
                for p in range(npair):
                    lanes = slice(p * LANES, (p + 1) * LANES)
                    s = s_scr[d, b, p]
                    rpc = rp_ref[b, rows, lanes]
                    lhs = jnp.concatenate([jnp.where(lane < HEAD_DIM, rpc, 0.0), jnp.where(lane >= HEAD_DIM, rpc, 0.0)],
                                          axis=0)
                    yh = lax.dot_general(lhs, s, (((1,), (1,)), ((), ())), preferred_element_type=F32)
                    y_ref[b, rows, lanes] = jnp.concatenate([yh[:CHUNK], yh[CHUNK:]], axis=1) + y0_ref[b, rows, lanes]
                    s_scr[d, b, p] = (jnp.dot(s.astype(BF16), a_ref[b, 0, c, p], preferred_element_type=F32)
                                      + bm_ref[b, 0, c, p])


def rwkv_scan(r, kk, v, lw0, lw1, k0, k1, b0, b1, lc):
    nb, ta, c = r.shape
    nblk = ta // SCAN_BLK
    nctb = lc // SCAN_BLK
    npair = c // LANES
    nchunk = SCAN_BLK // CHUNK
    a0, bm0, rp0, y00, a1, bm1, rp1, y01 = chunk_prep(r, kk, v, lw0, lw1, k0, k1, b0, b1)

    def fwd(s):
        return s

    def bwd(s):
        return jnp.where(s < nctb, nctb - 1 - s, nblk - 1 - (s - nctb))

    def specs(idx):
        return [pl.BlockSpec((nb, 1, nchunk, npair, LANES, LANES), lambda s: (0, idx(s), 0, 0, 0, 0)),
                pl.BlockSpec((nb, 1, nchunk, npair, HEAD_DIM, LANES), lambda s: (0, idx(s), 0, 0, 0, 0)),
                pl.BlockSpec((nb, SCAN_BLK, c), lambda s: (0, idx(s), 0)),
                pl.BlockSpec((nb, SCAN_BLK, c), lambda s: (0, idx(s), 0))]

    out = jax.ShapeDtypeStruct((nb, ta, c), F32)
    return pl.pallas_call(
        functools.partial(_chunk_scan_kernel, nb),
        out_shape=[out, out],
        grid=(nblk,),
        in_specs=specs(fwd) + specs(bwd),
        out_specs=[pl.BlockSpec((nb, SCAN_BLK, c), lambda s: (0, fwd(s), 0)),
                   pl.BlockSpec((nb, SCAN_BLK, c), lambda s: (0, bwd(s), 0))],
        scratch_shapes=[pltpu.VMEM((2, nb, npair, HEAD_DIM, LANES), F32)],
        compiler_params=_cparams(("arbitrary",), VMEM_LIMIT),
        name="chunk_scan",
    )(a0, bm0, rp0, y00, a1, bm1, rp1, y01)


def _rwkv_out_kernel(yf_ref, yb_ref, g_ref, bon_ref, lg_ref, lb_ref, bd_ref, o_ref):
    y = yf_ref[0] + yb_ref[0]
    bd = bd_ref[...]
    mu = _head_sums(y, bd) * (1.0 / HEAD_DIM)
    yc = y - mu
    var = _head_sums(yc * yc, bd) * (1.0 / HEAD_DIM)
    yn = yc * lax.rsqrt(var + RWKV_GN_EPS) * lg_ref[...] + lb_ref[...]
    o_ref[0] = ((yn + bon_ref[0]) * g_ref[0]).astype(BF16)


def rwkv_out(yf, yb, g, bon, lnx_g, lnx_b, bd):
    nb, ta, _ = yf.shape
    nat_spec = pl.BlockSpec((1, TT, BRANCH_W), lambda b, t: (b, t, 0))
    row = pl.BlockSpec((1, BRANCH_W), lambda b, t: (0, 0))
    return pl.pallas_call(
        _rwkv_out_kernel,
        out_shape=jax.ShapeDtypeStruct((nb, ta, BRANCH_W), BF16),
        grid=(nb, ta // TT),
        in_specs=[nat_spec] * 4 + [row, row, pl.BlockSpec(bd.shape, lambda b, t: (0, 0))],
        out_specs=nat_spec,
        compiler_params=_cparams(("parallel", "parallel")),
        name="rwkv_out",
    )(yf, yb, g, bon, lnx_g.reshape(1, -1), lnx_b.reshape(1, -1), bd)


def _pair_swap(x):
    lane = lax.broadcasted_iota(jnp.int32, x.shape, 1)
    n = x.shape[1]
    return jnp.where(lane % 2 == 0, pltpu.roll(x, n - 1, axis=1), pltpu.roll(x, 1, axis=1))


def _attn_prep_kernel(p_ref, cos_ref, sin_ref, qg_ref, kg_ref, bd_ref, q_o, k_o, v_o):
    p = p_ref[0].astype(F32)
    q, k, v = p[:, :Q_W], p[:, Q_W:Q_W + KV_W], p[:, Q_W + KV_W:]
    bd = bd_ref[...]
    cos, sin = cos_ref[...], sin_ref[...]
    qms = _head_sums(q * q, bd) * (1.0 / HEAD_DIM)
    qn = q * lax.rsqrt(qms + RMS_EPS) * qg_ref[...]
    qr = qn * cos + _pair_swap(qn) * sin
    q_o[0] = (qr * HEAD_DIM ** -0.5).astype(BF16)
    kms = _head_sums(k * k, bd[:KV_W, :KV_W]) * (1.0 / HEAD_DIM)
    kn = k * lax.rsqrt(kms + RMS_EPS) * kg_ref[...]
    kr = kn * cos[:, :KV_W] + _pair_swap(kn) * sin[:, :KV_W]
    for g in range(KV_W // HEAD_DIM):
        sl = slice(g * HEAD_DIM, (g + 1) * HEAD_DIM)
        k_o[0, g] = kr[:, sl].astype(BF16)
        v_o[0, g] = v[:, sl].astype(BF16)


def attn_prep(p_attn, cos_t, sin_t, q_norm, k_norm, bd):
    nb, ta, _ = p_attn.shape
    ng = KV_W // HEAD_DIM
    qg = jnp.tile(q_norm, Q_W // HEAD_DIM).reshape(1, -1)
    kg = jnp.tile(k_norm, ng).reshape(1, -1)
    kv_shape = jax.ShapeDtypeStruct((nb, ng, ta, HEAD_DIM), BF16)
    kv_spec = pl.BlockSpec((1, ng, TT, HEAD_DIM), lambda b, t: (b, 0, t, 0))
    return pl.pallas_call(
        _attn_prep_kernel,
        out_shape=[jax.ShapeDtypeStruct((nb, ta, Q_W), BF16), kv_shape, kv_shape],
        grid=(nb, ta // TT),
        in_specs=[pl.BlockSpec((1, TT, ATTN_COLS), lambda b, t: (b, t, 0)),
                  pl.BlockSpec((TT, Q_W), lambda b, t: (t, 0)),
                  pl.BlockSpec((TT, Q_W), lambda b, t: (t, 0)),
                  pl.BlockSpec((1, Q_W), lambda b, t: (0, 0)),
                  pl.BlockSpec((1, KV_W), lambda b, t: (0, 0)),
                  pl.BlockSpec(bd.shape, lambda b, t: (0, 0))],
        out_specs=[pl.BlockSpec((1, TT, Q_W), lambda b, t: (b, t, 0)), kv_spec, kv_spec],
        compiler_params=_cparams(("parallel", "parallel")),
        name="attn_prep",
    )(p_attn, cos_t, sin_t, qg, kg, bd)


def _attn_kernel(nct, lc, q_ref, k_ref, v_ref, o_ref):
    t = pl.program_id(1)
    ng = k_ref.shape[1]
    rep = Q_W // HEAD_DIM // ng

    def run(nk):
        outs = []
        for g in range(ng):
            kk, vv = k_ref[0, g, :nk, :], v_ref[0, g, :nk, :]
            for r in range(rep):
                h = g * rep + r
                q = q_ref[0][:, h * HEAD_DIM:(h + 1) * HEAD_DIM]
                s = lax.dot_general(q, kk, (((1,), (1,)), ((), ())), preferred_element_type=F32)
                p = jnp.exp(s - jnp.max(s, -1, keepdims=True))
                l = jnp.sum(p, -1, keepdims=True)
                o = jnp.dot(p.astype(BF16), vv, preferred_element_type=F32)
                outs.append(o / l)
        o_ref[0] = jnp.concatenate(outs, axis=1).astype(BF16)

    @pl.when(t < nct)
    def _():
        run(lc)

    @pl.when(t >= nct)
    def _():
        run(k_ref.shape[2])


def attention(q, k, v, lc):
    nb, ta, _ = q.shape
    ng = k.shape[1]
    nct = lc // TT
    qo_spec = pl.BlockSpec((1, TT, Q_W), lambda b, t: (b, t, 0))
    kv_spec = pl.BlockSpec((1, ng, ta, HEAD_DIM), lambda b, t: (b, 0, 0, 0))
    return pl.pallas_call(
        functools.partial(_attn_kernel, nct, lc),
        out_shape=jax.ShapeDtypeStruct((nb, ta, Q_W), BF16),
        grid=(nb, ta // TT),
        in_specs=[qo_spec, kv_spec, kv_spec],
        out_specs=qo_spec,
        compiler_params=_cparams(("parallel", "arbitrary"), VMEM_LIMIT),
        name="attention",
    )(q, k, v)


def _hs_pre_kernel(nct, nt, cur_ref, prev_ref, next_ref, hw_ref, sw_ref, x0_o, u_o, ycv_o):
    cur = cur_ref[0].astype(F32)
    xm1, xp1 = _neighbours(cur, prev_ref, next_ref, nct, nt)
    c = BRANCH_W
    hc = HYENA_COLS
    hw = hw_ref[...]
    ph = hw[0:1] * xm1[:, :hc] + hw[1:2] * cur[:, :hc] + hw[2:3] * xp1[:, :hc]
    x0_o[0] = ph[:, :c]
    u_o[0] = ph[:, c:2 * c] * ph[:, 2 * c:3 * c]
    sw = sw_ref[...]

    def cx(a):
        return a[:, hc + c:hc + 2 * c] * a[:, hc + 2 * c:hc + 3 * c]

    conv = sw[0:1] * cx(xm1) + sw[1:2] * cx(cur) + sw[2:3] * cx(xp1)
    ycv_o[0] = (cur[:, hc:hc + c] * conv).astype(BF16)


def hs_pre(p_hs, nct, hyena_conv, sconv_w):
    nb, ta, w = p_hs.shape
    nt = ta // TT
    nat = pl.BlockSpec((1, TT, BRANCH_W), lambda b, t: (b, t, 0))
    return pl.pallas_call(
        functools.partial(_hs_pre_kernel, nct, nt),
        out_shape=[jax.ShapeDtypeStruct((nb, ta, BRANCH_W), F32),
                   jax.ShapeDtypeStruct((nb, ta, BRANCH_W), F32),
                   jax.ShapeDtypeStruct((nb, ta, BRANCH_W), BF16)],
        grid=(nb, nt),
        in_specs=_halo_specs(w, ta, 2 * SUBLANES) + [pl.BlockSpec(hyena_conv.shape, lambda b, t: (0, 0)),
                                       pl.BlockSpec(sconv_w.shape, lambda b, t: (0, 0))],
        out_specs=[nat, nat, nat],
        compiler_params=_cparams(("parallel", "parallel"), VMEM_LIMIT),
        name="hs_pre",
    )(p_hs, p_hs, p_hs, hyena_conv, sconv_w)


EMB_PAD = 40


def _filter_tables(lh):
    n = np.arange(2 * lh)
    pos = np.abs(n - (lh - 1)).astype(np.float64)
    bands = (HYENA_EMB - 1) // 2
    t = np.minimum(pos, lh - 1) / (lh - 1)
    wpos = 2.0 * math.pi * pos / lh
    f = np.linspace(1e-4, bands - 1, bands)[:, None]
    z = np.zeros((EMB_PAD, 2 * lh), np.float32)
    z[0] = t
    z[1:1 + bands] = np.cos(f * wpos[None, :])
    z[1 + bands:1 + 2 * bands] = -np.sin(f * wpos[None, :])
    max_decay = math.log(HYENA_TARGET) / HYENA_FAST_DECAY
    min_decay = math.log(HYENA_TARGET) / HYENA_SLOW_DECAY
    deltas = np.abs(np.linspace(min_decay, max_decay, BRANCH_W)).astype(np.float32)
    return z, deltas.reshape(-1, 1)


def _filter_kernel(lh, tn, z_ref, w1_ref, b1_ref, f1_ref, w2_ref, b2_ref, f2_ref, w3_ref, dl_ref, o_ref):
    z = z_ref[...]
    h1 = jnp.sin(f1_ref[...] * (jnp.dot(w1_ref[...], z, precision=HI, preferred_element_type=F32) + b1_ref[...]))
    h2 = jnp.sin(f2_ref[...] * (jnp.dot(w2_ref[...], h1, precision=HI, preferred_element_type=F32) + b2_ref[...]))
    f = jnp.dot(w3_ref[...], h2, precision=HI, preferred_element_type=F32)
    n = pl.program_id(0) * tn + lax.broadcasted_iota(jnp.int32, (1, tn), 1)
    filt = jnp.where(n >= lh - 1, f[:BRANCH_W], f[BRANCH_W:])
    win = jnp.exp(-z[0:1, :] * dl_ref[...])
    o_ref[...] = jnp.where(n == 2 * lh - 1, 0.0, filt * win)


def hyena_filter_table(lh, w1, b1, f1, w2, b2, f2, w3):
    z_np, dl_np = _filter_tables(lh)
    n2 = 2 * lh
    tn = _pick_tile(n2, (1024, 512))
    hd = w2.shape[0]
    w1t = jnp.zeros((hd, EMB_PAD), F32).at[:, :HYENA_EMB].set(w1.T)
    args = [jnp.asarray(z_np), w1t, b1.reshape(-1, 1), f1.reshape(-1, 1), w2.T, b2.reshape(-1, 1),
            f2.reshape(-1, 1), w3.T, jnp.asarray(dl_np)]

    def full(a):
        return pl.BlockSpec(a.shape, lambda j: (0, 0))

    return pl.pallas_call(
        functools.partial(_filter_kernel, lh, tn),
        out_shape=jax.ShapeDtypeStruct((BRANCH_W, n2), F32),
        grid=(n2 // tn,),
        in_specs=[pl.BlockSpec((EMB_PAD, tn), lambda j: (0, j))] + [full(a) for a in args[1:]],
        out_specs=pl.BlockSpec((BRANCH_W, tn), lambda j: (0, j)),
        compiler_params=_cparams(("parallel",)),
        name="hyena_filter",
    )(*args)


def _hyena_conv_kernel(nblk, bp, nch, k_ref, u_ref, o_ref, t_scr):
    ntile = 4 * nblk - 1
    mc = 2 * nblk - 1
    width = (ntile + 1) * LANES
    for ch in range(nch):
        big = pltpu.roll(jnp.broadcast_to(k_ref[ch], (LANES, width)), width - (LANES - 1), 1, stride=1, stride_axis=0)
        for m in range(ntile):
            t_scr[ch, m] = big[:, m * LANES:(m + 1) * LANES].astype(BF16)

    for ch in range(nch):
        for d in [0] + [s * a for a in range(1, nblk) for s in (1, -1)]:
            m0 = 2 * d + mc
            w = jnp.concatenate([jnp.concatenate([t_scr[ch, m0], t_scr[ch, m0 + 1]], axis=1),
                                 jnp.concatenate([t_scr[ch, m0 - 1], t_scr[ch, m0]], axis=1)], axis=0)
            i0, i1 = max(0, d), min(nblk, nblk + d)
            lhs = u_ref[ch, (i0 - d) * bp:(i1 - d) * bp, :].astype(BF16)
            res = jnp.dot(lhs, w, preferred_element_type=F32)
            if d == 0:
                o_ref[ch] = res
            else:
                o_ref[ch, i0 * bp:i1 * bp, :] += res


def hyena_conv(u, ktab):
    nb, l, c = u.shape
    nblk = l // HY_BLK
    bp = -(-nb // SUBLANES) * SUBLANES
    nch = SUBLANES if nblk == 1 else 1
    ut = jnp.transpose(u.reshape(nb, nblk, HY_BLK, c), (3, 1, 0, 2))
    if bp != nb:
        ut = jnp.pad(ut, ((0, 0), (0, 0), (0, bp - nb), (0, 0)))
    ut = ut.reshape(c, nblk * bp, HY_BLK)
    k3 = ktab.reshape(c, 1, 4 * nblk * LANES)
    out = pl.pallas_call(
        functools.partial(_hyena_conv_kernel, nblk, bp, nch),
        out_shape=jax.ShapeDtypeStruct((c, nblk * bp, HY_BLK), F32),
        grid=(c // nch,),
        in_specs=[pl.BlockSpec((nch, 1, 4 * nblk * LANES), lambda ch: (ch, 0, 0)),
                  pl.BlockSpec((nch, nblk * bp, HY_BLK), lambda ch: (ch, 0, 0))],
        out_specs=pl.BlockSpec((nch, nblk * bp, HY_BLK), lambda ch: (ch, 0, 0)),
        scratch_shapes=[pltpu.VMEM((nch, 4 * nblk - 1, LANES, LANES), BF16)],
        compiler_params=_cparams(("parallel",)),
        name="hyena_conv",
    )(k3, ut)
    out = out.reshape(c, nblk, bp, HY_BLK)[:, :, :nb]
    return jnp.transpose(out, (2, 1, 3, 0)).reshape(nb, l, c)


def _route(logits, bias):
    s = jax.nn.sigmoid(logits)
    sel = s + bias
    srow = [s[e:e + 1] for e in range(N_EXPERTS)]
    row = [sel[e:e + 1] for e in range(N_EXPERTS)]
    best, gi = None, None
    for g in range(N_GROUPS):
        a, b, c, d = row[4 * g:4 * g + 4]
        hi1, lo1, hi2, lo2 = jnp.maximum(a, b), jnp.minimum(a, b), jnp.maximum(c, d), jnp.minimum(c, d)
        score = jnp.maximum(hi1, hi2) + jnp.maximum(jnp.minimum(hi1, hi2), jnp.maximum(lo1, lo2))
        if g == 0:
            best, gi = score, jnp.zeros(score.shape, jnp.int32)
        else:
            better = score > best
            gi = jnp.where(better, g, gi)
            best = jnp.where(better, score, best)
    neg = -jnp.inf
    msel = [jnp.where(gi == e // EXPERTS_PER_GROUP, row[e], neg) for e in range(N_EXPERTS)]

    def arg_first_max(vals):
        bv, bi = vals[0], jnp.zeros(vals[0].shape, jnp.int32)
        for e in range(1, N_EXPERTS):
            better = vals[e] > bv
            bi = jnp.where(better, e, bi)
            bv = jnp.where(better, vals[e], bv)
        return bi

    i1 = arg_first_max(msel)
    i2 = arg_first_max([jnp.where(i1 == e, neg, msel[e]) for e in range(N_EXPERTS)])
    w1 = sum(jnp.where(i1 == e, srow[e], 0.0) for e in range(N_EXPERTS))
    w2 = sum(jnp.where(i2 == e, srow[e], 0.0) for e in range(N_EXPERTS))
    den = w1 + w2
    g1, g2 = w1 / den, w2 / den
    rows = [jnp.where(i1 == e, g1, 0.0) + jnp.where(i2 == e, g2, 0.0) for e in range(N_EXPERTS)]
    rows.append(gi.astype(F32))
    rows.extend([jnp.zeros_like(g1)] * (ROUTE_ROWS - len(rows)))
    return jnp.concatenate(rows, axis=0)


MERGE_NB = 4


def _merge_kernel(ya_ref, x0_ref, u_ref, yc_ref, ycv_ref, yd_ref, h_ref, x_ref, *rest):
    mods, rest = rest[:3 * MERGE_NB], rest[3 * MERGE_NB:]
    skip_ref, wg_ref, wb_ref, wo_ref, g1_ref, b1_ref, rwt_ref, rb_ref = rest[:8]
    x1_o, hf_o = rest[8:10]
    gates_o = rest[10:]
    rows = MERGE_NB * TT
    flat = lambda ref: ref[...].reshape(rows, ref.shape[-1])
    yb = (flat(x0_ref) * (flat(yc_ref) + flat(u_ref) * skip_ref[...])).astype(BF16)
    ys = (flat(ya_ref), yb, flat(ycv_ref), flat(yd_ref))
    h = flat(h_ref)
    merged = None
    for n in range(N_BRANCHES):
        gate = jax.nn.sigmoid(jnp.dot(h, wg_ref[:, n * D_MODEL:(n + 1) * D_MODEL], preferred_element_type=F32))
        term = gate * jnp.dot(ys[n], wb_ref[n], preferred_element_type=F32)
        merged = term if merged is None else merged + term
    out = jnp.dot(merged.astype(BF16), wo_ref[...], preferred_element_type=F32)
    alpha = (2 * 2) ** 0.25
    for k in range(MERGE_NB):
        ga_ref, shf_ref, scf_ref = mods[3 * k:3 * k + 3]
        x1 = _ln(alpha * x_ref[k] + ga_ref[0] * out[k * TT:(k + 1) * TT]) * g1_ref[...] + b1_ref[...]
        hf = _ln(x1) * (1.0 + scf_ref[0]) + shf_ref[0]
        x1_o[k] = x1
        hf_o[k] = hf.astype(BF16)
        logits = lax.dot_general(rwt_ref[...], hf, (((1,), (1,)), ((), ())), precision=HI,
                                 preferred_element_type=F32)
        gates_o[k][...] = _route(logits, rb_ref[...])


def merge(ya, x0, u, yconv, ycv, yd, h, x_all, mod3, nct, skip, wg, wb, wo, ln_g, ln_b, rwt, rbias):
    nb, ta, d = x_all.shape
    nt = ta // TT
    mb = MERGE_NB
    assert nb % mb == 0
    nat = pl.BlockSpec((mb, TT, BRANCH_W), lambda b, t: (b, t, 0))
    wide = pl.BlockSpec((mb, TT, d), lambda b, t: (b, t, 0))

    def full(a):
        nd = a.ndim
        return pl.BlockSpec(a.shape, lambda b, t: (0,) * nd)

    def mod_spec(col, k):
        return pl.BlockSpec((1, 1, D_MODEL), lambda b, t: (jnp.where(t < nct, nb, b * mb + k), 0, col))

    mod_specs = [mod_spec(col, k) for k in range(mb) for col in (2, 3, 4)]
    consts = [skip.reshape(1, -1), wg, wb, wo, ln_g.reshape(1, -1), ln_b.reshape(1, -1), rwt, rbias.reshape(-1, 1)]
    gate_shape = jax.ShapeDtypeStruct((ROUTE_ROWS, nb // mb * ta), F32)
    gate_spec = pl.BlockSpec((ROUTE_ROWS, TT), lambda b, t: (0, b * nt + t))
    outs = pl.pallas_call(
        _merge_kernel,
        out_shape=[jax.ShapeDtypeStruct((nb, ta, d), F32), jax.ShapeDtypeStruct((nb, ta, d), BF16)] + [gate_shape] * mb,
        grid=(nb // mb, nt),
        in_specs=[nat] * 6 + [wide, wide] + mod_specs + [full(a) for a in consts],
        out_specs=[wide, wide] + [gate_spec] * mb,
        compiler_params=_cparams(("parallel", "parallel"), VMEM_LIMIT),
        name="merge",
    )(ya, x0, u, yconv, ycv, yd, h, x_all, *([mod3] * (3 * mb)), *consts)
    gates_t = jnp.stack([g.reshape(ROUTE_ROWS, nb // mb, ta) for g in outs[2:]], axis=2).reshape(ROUTE_ROWS, nb * ta)
    return outs[0], outs[1], gates_t


MOE_TILE = 1024
MOE_ALIGN = 2 * SUBLANES
MOE_CHUNK = 18 * MOE_ALIGN
MOE_SORTED = MOE_TILE + LANES
MOE_ROWS = MOE_SORTED + MOE_CHUNK
GID_ROW = N_EXPERTS
ROUTE_ROWS = 3 * SUBLANES
META_LANES = 2 * N_GROUPS
assert N_GROUPS * (MOE_ALIGN - 1) <= MOE_SORTED - MOE_TILE and MOE_ROWS % MOE_ALIGN == 0


def _moe_sort_kernel(gt_ref, g_ref, h_ref, up_ref, hs_o, gs_o, pt_o, meta_o):
    gid = gt_ref[GID_ROW:GID_ROW + 1, :]
    onehot = [jnp.where(gid == float(g), 1.0, 0.0) for g in range(N_GROUPS)]
    g4 = jnp.concatenate(onehot + [jnp.zeros((SUBLANES - N_GROUPS, MOE_TILE), F32)], axis=0)
    before = jnp.dot(g4.astype(BF16), up_ref[...], preferred_element_type=F32)
    lane = lax.broadcasted_iota(jnp.int32, (SUBLANES, LANES), 1)
    meta = jnp.zeros((SUBLANES, LANES), F32)
    off = jnp.zeros((1, 1), F32)
    pos = jnp.zeros((1, MOE_TILE), F32)
    for g in range(N_GROUPS):
        cnt = jnp.sum(onehot[g], axis=1, keepdims=True)
        pos = pos + onehot[g] * (before[g:g + 1] + off)
        meta = jnp.where(lane == g, off, meta)
        meta = jnp.where(lane == N_GROUPS + g, cnt, meta)
        off = off + jnp.ceil(cnt * (1.0 / MOE_ALIGN)) * MOE_ALIGN
    meta_o[0] = meta.astype(jnp.int32)
    row = lax.broadcasted_iota(jnp.int32, (MOE_SORTED, MOE_TILE), 0)
    place = jnp.where(row == pos.astype(jnp.int32), 1.0, 0.0)
    p16 = place.astype(BF16)
    hs_o[0, :MOE_SORTED, :] = jnp.dot(p16, h_ref[...], preferred_element_type=F32).astype(BF16)
    hs_o[0, MOE_SORTED:, :] = jnp.zeros((MOE_ROWS - MOE_SORTED, h_ref.shape[1]), BF16)
    gts = g_ref[...]
    hi = gts.astype(BF16)
    r1 = gts - hi.astype(F32)
    mid = r1.astype(BF16)
    low = (r1 - mid.astype(F32)).astype(BF16)
    gs_o[0, :MOE_SORTED, :] = (jnp.dot(p16, hi, preferred_element_type=F32)
                               + jnp.dot(p16, mid, preferred_element_type=F32)
                               + jnp.dot(p16, low, preferred_element_type=F32))
    gs_o[0, MOE_SORTED:, :] = jnp.zeros((MOE_ROWS - MOE_SORTED, LANES), F32)
    pt_o[...] = place.T.astype(BF16)


def moe_sort(hf, gates_t, gates):
    n, d = hf.shape
    ntile = n // MOE_TILE
    upper = jnp.asarray(np.triu(np.ones((MOE_TILE, MOE_TILE), np.float32), 1), BF16)
    return pl.pallas_call(
        _moe_sort_kernel,
        out_shape=[jax.ShapeDtypeStruct((ntile, MOE_ROWS, d), BF16),
                   jax.ShapeDtypeStruct((ntile, MOE_ROWS, LANES), F32),
                   jax.ShapeDtypeStruct((n, MOE_SORTED), BF16),
                   jax.ShapeDtypeStruct((ntile, SUBLANES, LANES), jnp.int32)],
        grid=(ntile,),
        in_specs=[pl.BlockSpec((ROUTE_ROWS, MOE_TILE), lambda i: (0, i)),
                  pl.BlockSpec((MOE_TILE, LANES), lambda i: (i, 0)),
                  pl.BlockSpec((MOE_TILE, d), lambda i: (i, 0)),
                  pl.BlockSpec((MOE_TILE, MOE_TILE), lambda i: (0, 0))],
        out_specs=[pl.BlockSpec((1, MOE_ROWS, d), lambda i: (i, 0, 0)),
                   pl.BlockSpec((1, MOE_ROWS, LANES), lambda i: (i, 0, 0)),
                   pl.BlockSpec((MOE_TILE, MOE_SORTED), lambda i: (i, 0)),
                   pl.BlockSpec((1, SUBLANES, LANES), lambda i: (i, 0, 0))],
        compiler_params=_cparams(("parallel",), VMEM_LIMIT),
        name="moe_sort",
    )(gates_t, gates, hf, upper)


def _moe_group_kernel(meta_ref, hs_ref, gs_ref, w1_ref, w3_ref, w2_ref, prev_ref, ys_o):
    g, i = pl.program_id(0), pl.program_id(1)

    @pl.when(g == 0)
    def _():
        ys_o[...] = jnp.zeros_like(ys_o)

    @pl.when(g > 0)
    def _():
        ys_o[...] = prev_ref[...]

    off = meta_ref[i * META_LANES + g]
    cnt = meta_ref[i * META_LANES + N_GROUPS + g]
    lane = lax.broadcasted_iota(jnp.int32, (MOE_CHUNK, LANES), 1)

    def chunk(j, carry):
        rows = pl.ds(pl.multiple_of(off + j * MOE_CHUNK, MOE_ALIGN), MOE_CHUNK)
        hs = hs_ref[0, rows, :]
        gs = gs_ref[0, rows, :]
        acc = ys_o[0, rows, :].astype(F32)
        for e in range(EXPERTS_PER_GROUP):
            a = jnp.dot(hs, w1_ref[e], preferred_element_type=F32)
            b = jnp.dot(hs, w3_ref[e], preferred_element_type=F32)
            act = (a * jax.nn.sigmoid(a)) * b
            gcol = jnp.sum(jnp.where(lane == g * EXPERTS_PER_GROUP + e, gs, 0.0), axis=1, keepdims=True)
            acc = acc + gcol * jnp.dot(act.astype(BF16), w2_ref[e], preferred_element_type=F32)
        ys_o[0, rows, :] = acc.astype(BF16)
        return carry

    lax.fori_loop(0, lax.div(cnt + (MOE_CHUNK - 1), MOE_CHUNK), chunk, 0)


def moe_group(hs, gs, meta, w1, w3, w2, carried=None):
    ntile, _, d = hs.shape
    de = w1.shape[2]
    epg = EXPERTS_PER_GROUP
    assert ntile > 1
    tile_spec = pl.BlockSpec((1, MOE_ROWS, d), lambda g, i, m: (i, 0, 0))
    grid_spec = pltpu.PrefetchScalarGridSpec(
        num_scalar_prefetch=1,
        grid=(N_GROUPS, ntile),
        in_specs=[tile_spec,
                  pl.BlockSpec((1, MOE_ROWS, LANES), lambda g, i, m: (i, 0, 0)),
                  pl.BlockSpec((epg, d, de), lambda g, i, m: (g, 0, 0)),
                  pl.BlockSpec((epg, d, de), lambda g, i, m: (g, 0, 0)),
                  pl.BlockSpec((epg, de, d), lambda g, i, m: (g, 0, 0)),
                  pl.BlockSpec((1, MOE_ROWS, d), lambda g, i, m: (jnp.where(g == 0, ntile - 1, i), 0, 0))],
        out_specs=tile_spec,
    )
    if carried is None:
        carried = jnp.zeros((ntile, MOE_ROWS, d), BF16)
    return pl.pallas_call(
        _moe_group_kernel,
        out_shape=jax.ShapeDtypeStruct((ntile, MOE_ROWS, d), BF16),
        grid_spec=grid_spec,
        input_output_aliases={6: 0},
        compiler_params=_cparams(("arbitrary", "arbitrary"), VMEM_LIMIT),
        name="moe_group",
    )(meta, hs, gs, w1, w3, w2, carried)


def _ln2_kernel(emit_next, x_ref, pt_ref, ys_ref, gf_ref, g_ref, b_ref, *rest):
    alpha = (2 * 2) ** 0.25
    f = jnp.dot(pt_ref[...], ys_ref[0], preferred_element_type=F32)
    x2 = _ln(alpha * x_ref[0] + gf_ref[0] * f) * g_ref[...] + b_ref[...]
    if emit_next:
        sh_ref, sc_ref, o_ref, h_o = rest
        h_o[0] = (_ln(x2) * (1.0 + sc_ref[0]) + sh_ref[0]).astype(BF16)
    else:
        o_ref, = rest
    o_ref[0] = x2


def ln2(x1, pt, ysb, mod3, nct, ln_g, ln_b, mod3_next):
    nb, ta, d = x1.shape
    nt = ta // TT
    per = MOE_TILE // TT
    wide = pl.BlockSpec((1, TT, d), lambda b, t: (b, t, 0))
    row = pl.BlockSpec((1, d), lambda b, t: (0, 0))
    in_specs = [wide, pl.BlockSpec((TT, MOE_SORTED), lambda b, t: (b * nt + t, 0)),
                pl.BlockSpec((1, MOE_SORTED, d), lambda b, t: ((b * nt + t) // per, 0, 0)),
                _mod_spec(5, nct, nb), row, row]
    args = [x1, pt, ysb, mod3, ln_g.reshape(1, -1), ln_b.reshape(1, -1)]
    if mod3_next is None:
        out_shape = jax.ShapeDtypeStruct((nb, ta - nct * TT, d), F32)
        out_specs = pl.BlockSpec((1, TT, d), lambda b, t: (b, jnp.maximum(t - nct, 0), 0))
    else:
        in_specs += [_mod_spec(0, nct, nb), _mod_spec(1, nct, nb)]
        args += [mod3_next, mod3_next]
        out_shape = [jax.ShapeDtypeStruct((nb, ta, d), F32), jax.ShapeDtypeStruct((nb, ta, d), BF16)]
        out_specs = [wide, wide]
    return pl.pallas_call(
        functools.partial(_ln2_kernel, mod3_next is not None),
        out_shape=out_shape,
        grid=(nb, nt),
        in_specs=in_specs,
        out_specs=out_specs,
        compiler_params=_cparams(("parallel", "arbitrary"), VMEM_LIMIT),
        name="ln2",
    )(*args)


def _rope_tables(l, lc):
    half = HEAD_DIM // 2
    inv = ROPE_THETA ** (-np.arange(0, half, 2, dtype=np.float64) / half)
    t = np.arange(l)
    rows, cols = t // GRID_W, t % GRID_W
    ang = np.concatenate([rows[:, None] * inv, cols[:, None] * inv], -1)
    ang = np.concatenate([np.zeros((lc, half)), ang], 0)
    cos = np.repeat(np.cos(ang), 2, axis=1)
    sin = np.repeat(np.sin(ang), 2, axis=1)
    sin[:, 0::2] *= -1.0
    reps = Q_W // HEAD_DIM
    return (jnp.asarray(np.tile(cos, (1, reps)), F32), jnp.asarray(np.tile(sin, (1, reps)), F32))


def _block_diag_ones():
    i = np.arange(BRANCH_W) // HEAD_DIM
    return jnp.asarray((i[:, None] == i[None, :]).astype(np.float32))


def kernel(x, c, ctx, c_ctx, ada_w, ada_b, w_in, rwkv_mu, rwkv_w0, rwkv_w_up, rwkv_a0, rwkv_a_up, rwkv_g_up, rwkv_k_k, rwkv_k_a, rwkv_r_k, rwkv_lnx_g, rwkv_lnx_b, hyena_conv, hyena_w1, hyena_b1, hyena_freq1, hyena_w2, hyena_b2, hyena_freq2, hyena_w3, hyena_skip, sconv_w, attn_q_norm, attn_k_norm, w_branch, w_out, ln1_g, ln1_b, ln2_g, ln2_b, router_w, router_bias, exp_w1, exp_w3, exp_w2):
    nb, l, d = x.shape
    lc = ctx.shape[1]
    depth = ada_w.shape[0]
    assert d == D_MODEL and lc % TT == 0 and l % TT == 0 and l % GRID_W == 0 and (nb * (lc + l)) % MOE_TILE == 0
    ta = lc + l
    nct = lc // TT

    mod_rows = -(-(nb + 1) // SUBLANES) * SUBLANES
    cc = jnp.zeros((mod_rows, d), F32).at[:nb].set(c).at[nb].set(c_ctx)
    mod = ada_mod(cc, ada_w, ada_b)

    cos_t, sin_t = _rope_tables(l, lc)
    bd = _block_diag_ones()
    rwt = router_w.T

    ysb = None
    for li in range(depth):
        mod3 = mod[li].reshape(mod_rows, 1, N_MOD * d)
        wl = w_in[li].astype(BF16)
        if li == 0:
            x_all, h3 = lnmod(ctx, x, mod3, nct, 0, 1)
        h = h3.reshape(nb * ta, d)
        p_rwkv = matmul(h, wl[:, :OFF_HYENA]).reshape(nb, ta, -1)
        p_hs = matmul(h, wl[:, OFF_HYENA:OFF_ATTN], BF16).reshape(nb, ta, -1)
        p_attn = matmul(h, wl[:, OFF_ATTN:OFF_GATE], BF16).reshape(nb, ta, -1)

        r, kk, w0, w1, k0, k1, b0, b1, v, g, bon = rwkv_prep(
            p_rwkv, nct, rwkv_mu[li], rwkv_w0[li], rwkv_w_up[li], rwkv_a0[li], rwkv_a_up[li], rwkv_g_up[li],
            rwkv_k_k[li], rwkv_k_a[li], rwkv_r_k[li], bd)
        yf, yb = rwkv_scan(r, kk, v, w0, w1, k0, k1, b0, b1, lc)
        ya = rwkv_out(yf, yb, g, bon, rwkv_lnx_g[li], rwkv_lnx_b[li], bd)

        q, kx, vx = attn_prep(p_attn, cos_t, sin_t, attn_q_norm[li], attn_k_norm[li], bd)
        yd = attention(q, kx, vx, lc)

        x0, u, ycv = hs_pre(p_hs, nct, hyena_conv[li], sconv_w[li])
        fargs = (hyena_w1[li], hyena_b1[li], hyena_freq1[li], hyena_w2[li], hyena_b2[li], hyena_freq2[li],
                 hyena_w3[li])
        yconv_ctx = hyena_conv_seg(u[:, :lc], fargs) if li < depth - 1 else jnp.zeros((nb, lc, BRANCH_W), F32)
        yconv = jnp.concatenate([yconv_ctx, hyena_conv_seg(u[:, lc:], fargs)], axis=1)

        x1, hf, gates_t = merge(ya, x0, u, yconv, ycv, yd, h3, x_all, mod3, nct, hyena_skip[li], wl[:, OFF_GATE:],
                                w_branch[li].astype(BF16), w_out[li].astype(BF16), ln1_g[li], ln1_b[li],
                                rwt, router_bias)
        gates = jnp.pad(gates_t.T, ((0, 0), (0, LANES - ROUTE_ROWS)))
        hs, gs, pt, meta = moe_sort(hf.reshape(nb * ta, d), gates_t, gates)
        ysb = moe_group(hs, gs, meta[:, 0, :META_LANES].reshape(-1), exp_w1[li].astype(BF16),
                        exp_w3[li].astype(BF16), exp_w2[li].astype(BF16), carried=ysb)
        if li == depth - 1:
            return ln2(x1, pt, ysb, mod3, nct, ln2_g[li], ln2_b[li], None)
        x_all, h3 = ln2(x1, pt, ysb, mod3, nct, ln2_g[li], ln2_b[li], mod[li + 1].reshape(mod_rows, 1, N_MOD * d))


def hyena_conv_seg(u_seg, fargs):
    ktab = hyena_filter_table(u_seg.shape[1], *fargs)
    return hyena_conv(u_seg, ktab)
```

```python
import functools
import math

import numpy as np
import jax
import jax.numpy as jnp
from jax import lax
from jax.experimental import pallas as pl
from jax.experimental.pallas import tpu as pltpu

F32 = jnp.float32
BF16 = jnp.bfloat16
HI = lax.Precision.HIGHEST

D_MODEL = 1024
GRID_W = 64
BRANCH_W = 256
HEAD_DIM = 64
N_BRANCHES = 4
N_MOD = 6
RWKV_COLS = 1024
RWKV_GN_EPS = 64e-5
HYENA_COLS = 768
HYENA_EMB = 33
HYENA_FAST_DECAY = 0.3
HYENA_SLOW_DECAY = 1.5
HYENA_TARGET = 1e-2
SCONV_COLS = 768
Q_W = 256
KV_W = 128
ATTN_COLS = 512
ROPE_THETA = 10000.0
RMS_EPS = 1e-6
OFF_HYENA = RWKV_COLS
OFF_SCONV = OFF_HYENA + HYENA_COLS
OFF_ATTN = OFF_SCONV + SCONV_COLS
OFF_GATE = OFF_ATTN + ATTN_COLS
N_EXPERTS = 16
N_GROUPS = 4
EXPERTS_PER_GROUP = 4
LN_EPS = 1e-6

SUBLANES = 8
LANES = 128
TT = 256
SCAN_BLK = LANES
HY_BLK = 256
VMEM_LIMIT = 56 * 1024 * 1024


def _cparams(sem, vmem=None):
    return pltpu.CompilerParams(dimension_semantics=sem, vmem_limit_bytes=vmem)


def _ln(xf):
    mu = jnp.mean(xf, -1, keepdims=True)
    xc = xf - mu
    var = jnp.mean(xc * xc, -1, keepdims=True)
    return xc * lax.rsqrt(var + LN_EPS)


def _head_sums(x, ones_bd):
    ones16 = ones_bd.astype(BF16)
    hi = x.astype(BF16)
    lo = (x - hi.astype(F32)).astype(BF16)
    return jnp.dot(hi, ones16, preferred_element_type=F32) + jnp.dot(lo, ones16, preferred_element_type=F32)


def _pick_tile(n, cands):
    for c in cands:
        if n % c == 0:
            return c
    raise ValueError(f"no tile for {n}")


def _ada_kernel(c_ref, w_ref, b_ref, o_ref):
    c = c_ref[...]
    a = c * jax.nn.sigmoid(c)
    o_ref[0] = jnp.dot(a, w_ref[0], precision=HI, preferred_element_type=F32) + b_ref[0]


def ada_mod(cc, ada_w, ada_b):
    depth, d, n = ada_w.shape
    rows = cc.shape[0]
    return pl.pallas_call(
        _ada_kernel,
        out_shape=jax.ShapeDtypeStruct((depth, rows, n), F32),
        grid=(depth, n // d),
        in_specs=[pl.BlockSpec((rows, d), lambda l, j: (0, 0)),
                  pl.BlockSpec((1, d, d), lambda l, j: (l, 0, j)),
                  pl.BlockSpec((1, 1, d), lambda l, j: (l, 0, j))],
        out_specs=pl.BlockSpec((1, rows, d), lambda l, j: (l, 0, j)),
        compiler_params=_cparams(("parallel", "parallel"), VMEM_LIMIT),
        name="ada_mod",
    )(cc, ada_w, ada_b.reshape(depth, 1, n))


def _lnmod_kernel(nct, c_ref, x_ref, sh_ref, sc_ref, xa_o, h_o):
    t = pl.program_id(1)
    xin = jnp.where(t < nct, c_ref[0], x_ref[0])
    xa_o[0] = xin
    h_o[0] = (_ln(xin) * (1.0 + sc_ref[0]) + sh_ref[0]).astype(BF16)


def _mod_spec(col, nct, nb):
    return pl.BlockSpec((1, 1, D_MODEL), lambda b, t: (jnp.where(t < nct, nb, b), 0, col))


def lnmod(ctx, x, mod3, nct, col_shift, col_scale):
    nb, l, d = x.shape
    ta = ctx.shape[1] + l
    wide = pl.BlockSpec((1, TT, d), lambda b, t: (b, t, 0))
    return pl.pallas_call(
        functools.partial(_lnmod_kernel, nct),
        out_shape=[jax.ShapeDtypeStruct((nb, ta, d), F32), jax.ShapeDtypeStruct((nb, ta, d), BF16)],
        grid=(nb, ta // TT),
        in_specs=[pl.BlockSpec((1, TT, d), lambda b, t: (b, jnp.minimum(t, nct - 1), 0)),
                  pl.BlockSpec((1, TT, d), lambda b, t: (b, jnp.maximum(t - nct, 0), 0)),
                  _mod_spec(col_shift, nct, nb), _mod_spec(col_scale, nct, nb)],
        out_specs=[wide, wide],
        compiler_params=_cparams(("parallel", "arbitrary")),
        name="lnmod",
    )(ctx, x, mod3, mod3)


def _mm_kernel(a_ref, b_ref, o_ref):
    o_ref[...] = jnp.dot(a_ref[...], b_ref[...], preferred_element_type=F32).astype(o_ref.dtype)


def matmul(a, b, out_dtype=F32):
    m, k = a.shape
    _, n = b.shape
    tm = _pick_tile(m, (1024, 512, 256))
    tn = _pick_tile(n, (1024, 512, 256))
    return pl.pallas_call(
        _mm_kernel,
        out_shape=jax.ShapeDtypeStruct((m, n), out_dtype),
        grid=(m // tm, n // tn),
        in_specs=[pl.BlockSpec((tm, k), lambda i, j: (i, 0)),
                  pl.BlockSpec((k, tn), lambda i, j: (0, j))],
        out_specs=pl.BlockSpec((tm, tn), lambda i, j: (i, j)),
        compiler_params=_cparams(("parallel", "parallel"), VMEM_LIMIT),
        name="matmul",
    )(a, b)


def _halo_specs(width, ta, halo=SUBLANES):
    nblk = ta // halo
    per = TT // halo
    cur = pl.BlockSpec((1, TT, width), lambda b, t: (b, t, 0))
    prev = pl.BlockSpec((1, halo, width), lambda b, t: (b, jnp.maximum(t * per - 1, 0), 0))
    nxt = pl.BlockSpec((1, halo, width), lambda b, t: (b, jnp.minimum((t + 1) * per, nblk - 1), 0))
    return [cur, prev, nxt]


def _neighbours(cur, prev_ref, next_ref, nct, nt):
    t = pl.program_id(1)
    seg_start = jnp.logical_or(t == 0, t == nct)
    seg_end = jnp.logical_or(t == nct - 1, t == nt - 1)
    halo = prev_ref.shape[1]
    prev_row = prev_ref[0][halo - 1:halo, :].astype(F32) * jnp.where(seg_start, 0.0, 1.0)
    next_row = next_ref[0][0:1, :].astype(F32) * jnp.where(seg_end, 0.0, 1.0)
    row = lax.broadcasted_iota(jnp.int32, (TT, 1), 0)
    xm1 = jnp.where(row == 0, prev_row, pltpu.roll(cur, 1, axis=0))
    xp1 = jnp.where(row == TT - 1, next_row, pltpu.roll(cur, TT - 1, axis=0))
    return xm1, xp1


def _rwkv_prep_kernel(nct, nt, cur_ref, prev_ref, next_ref, mu_ref, w0_ref, wup_ref, a0_ref, aup_ref,
                      gup_ref, kk_ref, ka_ref, rk_ref, bd_ref,
                      r_o, kk_o, w0_o, w1_o, k0_o, k1_o, b0_o, b1_o, v_o, g_o, bon_o):
    cur = cur_ref[0].astype(F32)
    xm1, xp1 = _neighbours(cur, prev_ref, next_ref, nct, nt)
    p = cur + mu_ref[...] * (0.5 * (xm1 + xp1) - cur)
    c = BRANCH_W
    r, k, v = p[:, 0:c], p[:, c:2 * c], p[:, 2 * c:3 * c]
    wd = p[:, 3 * c:3 * c + 64]
    ad = p[:, 3 * c + 64:3 * c + 128]
    gd = p[:, 3 * c + 128:3 * c + 256]
    bd = bd_ref[...]
    kk = k * kk_ref[...]
    ss = _head_sums(kk * kk, bd)
    kkn = kk * lax.rsqrt(jnp.maximum(ss, 1e-24))
    twd = jnp.tanh(wd)
    ka = ka_ref[...]
    kdirs = []
    w_outs, k_outs, b_outs = (w0_o, w1_o), (k0_o, k1_o), (b0_o, b1_o)
    for d in range(2):
        wlog = w0_ref[d:d + 1, :] + jnp.dot(twd, wup_ref[d], precision=HI, preferred_element_type=F32)
        decay = -math.exp(-0.5) * jax.nn.sigmoid(wlog)
        a = jax.nn.sigmoid(a0_ref[d:d + 1, :] + jnp.dot(ad, aup_ref[d], precision=HI, preferred_element_type=F32))
        kdir = k * (1.0 + (a - 1.0) * ka)
        bdir = kkn * a
        kdirs.append(kdir)
        w_outs[d][0] = decay
        k_outs[d][0] = kdir.astype(BF16)
        b_outs[d][0] = bdir.astype(BF16)
    r_o[0] = r.astype(BF16)
    kk_o[0] = kkn.astype(BF16)
    v_o[0] = v.astype(BF16)
    g_o[0] = jnp.dot(jax.nn.sigmoid(gd), gup_ref[...], precision=HI, preferred_element_type=F32)
    rkk = r * rk_ref[...] * (kdirs[0] + kdirs[1])
    bon_o[0] = _head_sums(rkk, bd) * v


def rwkv_prep(p_rwkv, nct, mu, w0, w_up, a0, a_up, g_up, k_k, k_a, r_k, bd):
    nb, ta, _ = p_rwkv.shape
    nt = ta // TT
    c = BRANCH_W
    nat = jax.ShapeDtypeStruct((nb, ta, c), F32)
    lo = jax.ShapeDtypeStruct((nb, ta, c), BF16)
    nat_spec = pl.BlockSpec((1, TT, c), lambda b, t: (b, t, 0))

    def full(a):
        nd = a.ndim
        return pl.BlockSpec(a.shape, lambda b, t: (0,) * nd)

    consts = [mu.reshape(1, -1), w0, w_up, a0, a_up, g_up, k_k.reshape(1, -1), k_a.reshape(1, -1),
              r_k.reshape(1, -1), bd]
    return pl.pallas_call(
        functools.partial(_rwkv_prep_kernel, nct, nt),
        out_shape=[lo, lo, nat, nat, lo, lo, lo, lo, lo, nat, nat],
        grid=(nb, nt),
        in_specs=_halo_specs(RWKV_COLS, ta, 2 * SUBLANES) + [full(a) for a in consts],
        out_specs=[nat_spec] * 11,
        compiler_params=_cparams(("parallel", "parallel"), VMEM_LIMIT),
        name="rwkv_prep",
    )(p_rwkv, p_rwkv, p_rwkv, *consts)


CHUNK = 32
PREP_NB = 4


def _chunk_scan_rows(x, reverse):
    pos = lax.broadcasted_iota(jnp.int32, x.shape, 0) % CHUNK
    step = 1
    while step < CHUNK:
        if reverse:
            x = x + jnp.where(pos < CHUNK - step, pltpu.roll(x, x.shape[0] - step, axis=0), 0.0)
        else:
            x = x + jnp.where(pos >= step, pltpu.roll(x, step, axis=0), 0.0)
        step *= 2
    return x


def _chunk_prep_kernel(r_ref, kk_ref, v_ref, lw0, k0, b0, lw1, k1, b1,
                       a0_o, bm0_o, rp0_o, y00_o, a1_o, bm1_o, rp1_o, y01_o):
    blk = SCAN_BLK
    npair = BRANCH_W // LANES
    nchunk = blk // CHUNK
    ti = lax.broadcasted_iota(jnp.int32, (blk, blk), 0)
    si = lax.broadcasted_iota(jnp.int32, (blk, blk), 1)
    same = (ti // CHUNK) == (si // CHUNK)
    eye = ti == si
    bd64 = (ti // HEAD_DIM) == (si // HEAD_DIM)
    head0 = si < HEAD_DIM
    lane_half = lax.broadcasted_iota(jnp.int32, (HEAD_DIM, LANES), 1)
    eyef = jnp.where(eye, 1.0, 0.0)
    dot = lambda x, y: jnp.dot(x, y, preferred_element_type=F32)
    lo = lambda x: x.astype(BF16)
    split = lambda x: jnp.concatenate([jnp.where(head0, x, 0.0), jnp.where(head0, 0.0, x)], axis=0)
    cat = lambda ms: lo(jnp.concatenate(ms, axis=1))

    probs = []
    for bi, d in [(bi, d) for bi in range(PREP_NB) for d in range(2)]:
        lw_ref, k_ref, b_ref = ((lw0, k0, b0), (lw1, k1, b1))[d]
        reverse = d == 1
        r, kk, v = (a[bi].astype(F32) for a in (r_ref, kk_ref, v_ref))
        lw, k, b = lw_ref[bi], k_ref[bi].astype(F32), b_ref[bi].astype(F32)
        lg = _chunk_scan_rows(lw, reverse)
        lg_end = lg + _chunk_scan_rows(lw, not reverse) - lw
        g, gi, g_end = jnp.exp(lg), jnp.exp(-lg), jnp.exp(lg_end)
        to_end = jnp.exp(lg_end - lg)
        arrs = (kk * jnp.exp(lg - lw), b * gi, k * gi, r * g, v, k * to_end, b * to_end, g_end)
        incl = jnp.logical_and(same, si >= ti if reverse else si <= ti)
        strict = jnp.logical_and(same, si > ti if reverse else si < ti)
        for p in range(npair):
            lanes = slice(p * LANES, (p + 1) * LANES)
            probs.append(dict(bi=bi, d=d, p=p, incl=incl, strict=strict, arrs=tuple(a[:, lanes] for a in arrs)))

    for q in probs:
        pp_, q_, kt_, rt_ = q["arrs"][:4]
        rhs_g = lo(jnp.concatenate([q_, kt_], axis=0))
        lm, mm, n2, nn = [], [], [], []
        for hh in range(2):
            hm = head0 if hh == 0 else jnp.logical_not(head0)
            lhs_g = lo(jnp.concatenate([jnp.where(hm, pp_, 0.0), jnp.where(hm, rt_, 0.0)], axis=0))
            gm = lax.dot_general(lhs_g, rhs_g, (((1,), (1,)), ((), ())), preferred_element_type=F32)
            lm.append(jnp.where(q["strict"], gm[:blk, :blk], 0.0))
            mm.append(jnp.where(q["strict"], gm[:blk, blk:], 0.0))
            n2.append(jnp.where(q["incl"], gm[blk:, :blk], 0.0))
            nn.append(jnp.where(q["incl"], gm[blk:, blk:], 0.0))
        q["pw"], q["tm"] = lm, [eyef - lm[0], eyef - lm[1]]
        q["m_cat"], q["n2_cat"], q["nn_cat"] = cat(mm), cat(n2), cat(nn)
    for _ in range(CHUNK.bit_length() - 2):
        for q in probs:
            pwl = [lo(x) for x in q["pw"]]
            q["pw"] = [dot(x, x) for x in pwl]
        for q in probs:
            q["tm"] = [dot(lo(t), lo(eyef + x)) for t, x in zip(q["tm"], q["pw"])]
    for q in probs:
        q["t_cat"] = cat(q["tm"])
        q["v_st"] = lo(split(q["arrs"][4]))
        q["pp"] = dot(q["t_cat"], lo(split(q["arrs"][0])))
        q["mv"] = dot(q["m_cat"], q["v_st"])
    for q in probs:
        q["w2"] = dot(q["t_cat"], lo(split(q["mv"])))
        q["rp"] = q["arrs"][3] - dot(q["n2_cat"], lo(split(q["pp"])))
    for q in probs:
        q["y0"] = dot(q["nn_cat"], q["v_st"]) - dot(q["n2_cat"], lo(split(q["w2"])))
    in_chunk = [si // CHUNK == c for c in range(nchunk)]
    for q in probs:
        ppt, vt, w2t = q["pp"].T, q["arrs"][4].T, q["w2"].T
        kg_, qg_ = q["arrs"][5], q["arrs"][6]
        lhs_a = jnp.concatenate([jnp.where(cm, ppt, 0.0) for cm in in_chunk], axis=0)
        q["pq"] = dot(lo(lhs_a), lo(qg_))
        lhs_b = jnp.concatenate([jnp.concatenate([jnp.where(cm, vt, 0.0), jnp.where(cm, -w2t, 0.0)], axis=1)
                                 for cm in in_chunk], axis=0)
        q["bf"] = dot(lo(lhs_b), lo(jnp.concatenate([kg_, qg_], axis=0)))
    outs = ((a0_o, bm0_o, rp0_o, y00_o), (a1_o, bm1_o, rp1_o, y01_o))
    for q in probs:
        a_o, bm_o = outs[q["d"]][:2]
        g_end_p = q["arrs"][7]
        for c in range(nchunk):
            pq_c = q["pq"][c * blk:(c + 1) * blk]
            a_o[q["bi"], 0, c, q["p"]] = (jnp.where(eye, g_end_p[c * CHUNK:c * CHUNK + 1], 0.0)
                                          - jnp.where(bd64, pq_c, 0.0)).astype(BF16)
            bm_o[q["bi"], 0, c, q["p"]] = jnp.where(lane_half < HEAD_DIM, q["bf"][c * blk:c * blk + HEAD_DIM],
                                                    q["bf"][c * blk + HEAD_DIM:(c + 1) * blk])
    for bi in range(PREP_NB):
        for d in range(2):
            rp_o, y0_o = outs[d][2:]
            rp_o[bi] = jnp.concatenate([q["rp"] for q in probs if q["d"] == d and q["bi"] == bi], axis=1)
            y0_o[bi] = jnp.concatenate([q["y0"] for q in probs if q["d"] == d and q["bi"] == bi], axis=1)


def chunk_prep(r, kk, v, lw0, lw1, k0, k1, b0, b1):
    nb, ta, c = r.shape
    nblk = ta // SCAN_BLK
    npair = c // LANES
    nchunk = SCAN_BLK // CHUNK
    pb = PREP_NB
    assert nb % pb == 0
    nat = pl.BlockSpec((pb, SCAN_BLK, c), lambda b, s: (b, s, 0))
    a_shape = jax.ShapeDtypeStruct((nb, nblk, nchunk, npair, LANES, LANES), BF16)
    bm_shape = jax.ShapeDtypeStruct((nb, nblk, nchunk, npair, HEAD_DIM, LANES), F32)
    nat_shape = jax.ShapeDtypeStruct((nb, ta, c), F32)
    a_spec = pl.BlockSpec((pb, 1, nchunk, npair, LANES, LANES), lambda b, s: (b, s, 0, 0, 0, 0))
    bm_spec = pl.BlockSpec((pb, 1, nchunk, npair, HEAD_DIM, LANES), lambda b, s: (b, s, 0, 0, 0, 0))
    return pl.pallas_call(
        _chunk_prep_kernel,
        out_shape=[a_shape, bm_shape, nat_shape, nat_shape] * 2,
        grid=(nb // pb, nblk),
        in_specs=[nat] * 9,
        out_specs=[a_spec, bm_spec, nat, nat] * 2,
        compiler_params=_cparams(("parallel", "parallel"), VMEM_LIMIT),
        name="chunk_prep",
    )(r, kk, v, lw0, k0, b0, lw1, k1, b1)


def _chunk_scan_kernel(nb, a0, bm0, rp0, y00, a1, bm1, rp1, y01, yf_o, yb_o, s_scr):
    step = pl.program_id(0)
    npair = BRANCH_W // LANES
    nchunk = SCAN_BLK // CHUNK

    @pl.when(step == 0)
    def _():
        s_scr[...] = jnp.zeros_like(s_scr)

    lane = lax.broadcasted_iota(jnp.int32, (CHUNK, LANES), 1)
    refs = ((a0, bm0, rp0, y00, yf_o), (a1, bm1, rp1, y01, yb_o))
    for ci in range(nchunk):
        for d in range(2):
            a_ref, bm_ref, rp_ref, y0_ref, y_ref = refs[d]
            c = ci if d == 0 else nchunk - 1 - ci
            rows = slice(c * CHUNK, (c + 1) * CHUNK)
            for b in range(nb):
                for p in range(npair):
                    lanes = slice(p * LANES, (p + 1) * LANES)
                    s = s_scr[d, b, p]
                    rpc = rp_ref[b, rows, lanes]
                    lhs = jnp.concatenate([jnp.where(lane < HEAD_DIM, rpc, 0.0), jnp.where(lane >= HEAD_DIM, rpc, 0.0)],
                                          axis=0)
                    yh = lax.dot_general(lhs, s, (((1,), (1,)), ((), ())), preferred_element_type=F32)
                    y_ref[b, rows, lanes] = jnp.concatenate([yh[:CHUNK], yh[CHUNK:]], axis=1) + y0_ref[b, rows, lanes]
                    s_scr[d, b, p] = (jnp.dot(s.astype(BF16), a_ref[b, 0, c, p], preferred_element_type=F32)
                                      + bm_ref[b, 0, c, p])


def rwkv_scan(r, kk, v, lw0, lw1, k0, k1, b0, b1, lc):
    nb, ta, c = r.shape
    nblk = ta // SCAN_BLK
    nctb = lc // SCAN_BLK
    npair = c // LANES
    nchunk = SCAN_BLK // CHUNK
    a0, bm0, rp0, y00, a1, bm1, rp1, y01 = chunk_prep(r, kk, v, lw0, lw1, k0, k1, b0, b1)

    def fwd(s):
        return s

    def bwd(s):
        return jnp.where(s < nctb, nctb - 1 - s, nblk - 1 - (s - nctb))

    def specs(idx):
        return [pl.BlockSpec((nb, 1, nchunk, npair, LANES, LANES), lambda s: (0, idx(s), 0, 0, 0, 0)),
                pl.BlockSpec((nb, 1, nchunk, npair, HEAD_DIM, LANES), lambda s: (0, idx(s), 0, 0, 0, 0)),
                pl.BlockSpec((nb, SCAN_BLK, c), lambda s: (0, idx(s), 0)),
                pl.BlockSpec((nb, SCAN_BLK, c), lambda s: (0, idx(s), 0))]

    out = jax.ShapeDtypeStruct((nb, ta, c), F32)
    return pl.pallas_call(
        functools.partial(_chunk_scan_kernel, nb),
        out_shape=[out, out],
        grid=(nblk,),
        in_specs=specs(fwd) + specs(bwd),
        out_specs=[pl.BlockSpec((nb, SCAN_BLK, c), lambda s: (0, fwd(s), 0)),
                   pl.BlockSpec((nb, SCAN_BLK, c), lambda s: (0, bwd(s), 0))],
        scratch_shapes=[pltpu.VMEM((2, nb, npair, HEAD_DIM, LANES), F32)],
        compiler_params=_cparams(("arbitrary",), VMEM_LIMIT),
        name="chunk_scan",
    )(a0, bm0, rp0, y00, a1, bm1, rp1, y01)


def _rwkv_out_kernel(yf_ref, yb_ref, g_ref, bon_ref, lg_ref, lb_ref, bd_ref, o_ref):
    y = yf_ref[0] + yb_ref[0]
    bd = bd_ref[...]
    mu = _head_sums(y, bd) * (1.0 / HEAD_DIM)
    yc = y - mu
    var = _head_sums(yc * yc, bd) * (1.0 / HEAD_DIM)
    yn = yc * lax.rsqrt(var + RWKV_GN_EPS) * lg_ref[...] + lb_ref[...]
    o_ref[0] = ((yn + bon_ref[0]) * g_ref[0]).astype(BF16)


def rwkv_out(yf, yb, g, bon, lnx_g, lnx_b, bd):
    nb, ta, _ = yf.shape
    nat_spec = pl.BlockSpec((1, TT, BRANCH_W), lambda b, t: (b, t, 0))
    row = pl.BlockSpec((1, BRANCH_W), lambda b, t: (0, 0))
    return pl.pallas_call(
        _rwkv_out_kernel,
        out_shape=jax.ShapeDtypeStruct((nb, ta, BRANCH_W), BF16),
        grid=(nb, ta // TT),
        in_specs=[nat_spec] * 4 + [row, row, pl.BlockSpec(bd.shape, lambda b, t: (0, 0))],
        out_specs=nat_spec,
        compiler_params=_cparams(("parallel", "parallel")),
        name="rwkv_out",
    )(yf, yb, g, bon, lnx_g.reshape(1, -1), lnx_b.reshape(1, -1), bd)


def _pair_swap(x):
    lane = lax.broadcasted_iota(jnp.int32, x.shape, 1)
    n = x.shape[1]
    return jnp.where(lane % 2 == 0, pltpu.roll(x, n - 1, axis=1), pltpu.roll(x, 1, axis=1))


def _attn_prep_kernel(p_ref, cos_ref, sin_ref, qg_ref, kg_ref, bd_ref, q_o, k_o, v_o):
    p = p_ref[0].astype(F32)
    q, k, v = p[:, :Q_W], p[:, Q_W:Q_W + KV_W], p[:, Q_W + KV_W:]
    bd = bd_ref[...]
    cos, sin = cos_ref[...], sin_ref[...]
    qms = _head_sums(q * q, bd) * (1.0 / HEAD_DIM)
    qn = q * lax.rsqrt(qms + RMS_EPS) * qg_ref[...]
    qr = qn * cos + _pair_swap(qn) * sin
    q_o[0] = (qr * HEAD_DIM ** -0.5).astype(BF16)
    kms = _head_sums(k * k, bd[:KV_W, :KV_W]) * (1.0 / HEAD_DIM)
    kn = k * lax.rsqrt(kms + RMS_EPS) * kg_ref[...]
    kr = kn * cos[:, :KV_W] + _pair_swap(kn) * sin[:, :KV_W]
    for g in range(KV_W // HEAD_DIM):
        sl = slice(g * HEAD_DIM, (g + 1) * HEAD_DIM)
        k_o[0, g] = kr[:, sl].astype(BF16)
        v_o[0, g] = v[:, sl].astype(BF16)


def attn_prep(p_attn, cos_t, sin_t, q_norm, k_norm, bd):
    nb, ta, _ = p_attn.shape
    ng = KV_W // HEAD_DIM
    qg = jnp.tile(q_norm, Q_W // HEAD_DIM).reshape(1, -1)
    kg = jnp.tile(k_norm, ng).reshape(1, -1)
    kv_shape = jax.ShapeDtypeStruct((nb, ng, ta, HEAD_DIM), BF16)
    kv_spec = pl.BlockSpec((1, ng, TT, HEAD_DIM), lambda b, t: (b, 0, t, 0))
    return pl.pallas_call(
        _attn_prep_kernel,
        out_shape=[jax.ShapeDtypeStruct((nb, ta, Q_W), BF16), kv_shape, kv_shape],
        grid=(nb, ta // TT),
        in_specs=[pl.BlockSpec((1, TT, ATTN_COLS), lambda b, t: (b, t, 0)),
                  pl.BlockSpec((TT, Q_W), lambda b, t: (t, 0)),
                  pl.BlockSpec((TT, Q_W), lambda b, t: (t, 0)),
                  pl.BlockSpec((1, Q_W), lambda b, t: (0, 0)),
                  pl.BlockSpec((1, KV_W), lambda b, t: (0, 0)),
                  pl.BlockSpec(bd.shape, lambda b, t: (0, 0))],
        out_specs=[pl.BlockSpec((1, TT, Q_W), lambda b, t: (b, t, 0)), kv_spec, kv_spec],
        compiler_params=_cparams(("parallel", "parallel")),
        name="attn_prep",
    )(p_attn, cos_t, sin_t, qg, kg, bd)


def _attn_kernel(nct, lc, q_ref, k_ref, v_ref, o_ref):
    t = pl.program_id(1)
    ng = k_ref.shape[1]
    rep = Q_W // HEAD_DIM // ng

    def run(nk):
        outs = []
        for g in range(ng):
            kk, vv = k_ref[0, g, :nk, :], v_ref[0, g, :nk, :]
            for r in range(rep):
                h = g * rep + r
                q = q_ref[0][:, h * HEAD_DIM:(h + 1) * HEAD_DIM]
                s = lax.dot_general(q, kk, (((1,), (1,)), ((), ())), preferred_element_type=F32)
                p = jnp.exp(s - jnp.max(s, -1, keepdims=True))
                l = jnp.sum(p, -1, keepdims=True)
                o = jnp.dot(p.astype(BF16), vv, preferred_element_type=F32)
                outs.append(o / l)
        o_ref[0] = jnp.concatenate(outs, axis=1).astype(BF16)

    @pl.when(t < nct)
    def _():
        run(lc)

    @pl.when(t >= nct)
    def _():
        run(k_ref.shape[2])


def attention(q, k, v, lc):
    nb, ta, _ = q.shape
    ng = k.shape[1]
    nct = lc // TT
    qo_spec = pl.BlockSpec((1, TT, Q_W), lambda b, t: (b, t, 0))
    kv_spec = pl.BlockSpec((1, ng, ta, HEAD_DIM), lambda b, t: (b, 0, 0, 0))
    return pl.pallas_call(
        functools.partial(_attn_kernel, nct, lc),
        out_shape=jax.ShapeDtypeStruct((nb, ta, Q_W), BF16),
        grid=(nb, ta // TT),
        in_specs=[qo_spec, kv_spec, kv_spec],
        out_specs=qo_spec,
        compiler_params=_cparams(("parallel", "arbitrary"), VMEM_LIMIT),
        name="attention",
    )(q, k, v)


def _hs_pre_kernel(nct, nt, cur_ref, prev_ref, next_ref, hw_ref, sw_ref, x0_o, u_o, ycv_o):
    cur = cur_ref[0].astype(F32)
    xm1, xp1 = _neighbours(cur, prev_ref, next_ref, nct, nt)
    c = BRANCH_W
    hc = HYENA_COLS
    hw = hw_ref[...]
    ph = hw[0:1] * xm1[:, :hc] + hw[1:2] * cur[:, :hc] + hw[2:3] * xp1[:, :hc]
    x0_o[0] = ph[:, :c].astype(BF16)
    u_o[0] = (ph[:, c:2 * c] * ph[:, 2 * c:3 * c]).astype(BF16)
    sw = sw_ref[...]

    def cx(a):
        return a[:, hc + c:hc + 2 * c] * a[:, hc + 2 * c:hc + 3 * c]

    conv = sw[0:1] * cx(xm1) + sw[1:2] * cx(cur) + sw[2:3] * cx(xp1)
    ycv_o[0] = (cur[:, hc:hc + c] * conv).astype(BF16)


def hs_pre(p_hs, nct, hyena_conv, sconv_w):
    nb, ta, w = p_hs.shape
    nt = ta // TT
    nat = pl.BlockSpec((1, TT, BRANCH_W), lambda b, t: (b, t, 0))
    return pl.pallas_call(
        functools.partial(_hs_pre_kernel, nct, nt),
        out_shape=[jax.ShapeDtypeStruct((nb, ta, BRANCH_W), BF16)] * 3,
        grid=(nb, nt),
        in_specs=_halo_specs(w, ta, 2 * SUBLANES) + [pl.BlockSpec(hyena_conv.shape, lambda b, t: (0, 0)),
                                       pl.BlockSpec(sconv_w.shape, lambda b, t: (0, 0))],
        out_specs=[nat, nat, nat],
        compiler_params=_cparams(("parallel", "parallel"), VMEM_LIMIT),
        name="hs_pre",
    )(p_hs, p_hs, p_hs, hyena_conv, sconv_w)


EMB_PAD = 40


def _filter_tables(lh):
    n = np.arange(2 * lh)
    pos = np.abs(n - (lh - 1)).astype(np.float64)
    bands = (HYENA_EMB - 1) // 2
    t = np.minimum(pos, lh - 1) / (lh - 1)
    wpos = 2.0 * math.pi * pos / lh
    f = np.linspace(1e-4, bands - 1, bands)[:, None]
    z = np.zeros((EMB_PAD, 2 * lh), np.float32)
    z[0] = t
    z[1:1 + bands] = np.cos(f * wpos[None, :])
    z[1 + bands:1 + 2 * bands] = -np.sin(f * wpos[None, :])
    max_decay = math.log(HYENA_TARGET) / HYENA_FAST_DECAY
    min_decay = math.log(HYENA_TARGET) / HYENA_SLOW_DECAY
    deltas = np.abs(np.linspace(min_decay, max_decay, BRANCH_W)).astype(np.float32)
    return z, deltas.reshape(-1, 1)


def _filter_kernel(lh, tn, z_ref, w1_ref, b1_ref, f1_ref, w2_ref, b2_ref, f2_ref, w3_ref, dl_ref, o_ref):
    z = z_ref[...]
    h1 = jnp.sin(f1_ref[...] * (jnp.dot(w1_ref[...], z, precision=HI, preferred_element_type=F32) + b1_ref[...]))
    h2 = jnp.sin(f2_ref[...] * (jnp.dot(w2_ref[...], h1, precision=HI, preferred_element_type=F32) + b2_ref[...]))
    f = jnp.dot(w3_ref[...], h2, precision=HI, preferred_element_type=F32)
    n = pl.program_id(0) * tn + lax.broadcasted_iota(jnp.int32, (1, tn), 1)
    filt = jnp.where(n >= lh - 1, f[:BRANCH_W], f[BRANCH_W:])
    win = jnp.exp(-z[0:1, :] * dl_ref[...])
    o_ref[...] = jnp.where(n == 2 * lh - 1, 0.0, filt * win)


def hyena_filter_table(lh, w1, b1, f1, w2, b2, f2, w3):
    z_np, dl_np = _filter_tables(lh)
    n2 = 2 * lh
    tn = _pick_tile(n2, (1024, 512))
    hd = w2.shape[0]
    w1t = jnp.zeros((hd, EMB_PAD), F32).at[:, :HYENA_EMB].set(w1.T)
    args = [jnp.asarray(z_np), w1t, b1.reshape(-1, 1), f1.reshape(-1, 1), w2.T, b2.reshape(-1, 1),
            f2.reshape(-1, 1), w3.T, jnp.asarray(dl_np)]

    def full(a):
        return pl.BlockSpec(a.shape, lambda j: (0, 0))

    return pl.pallas_call(
        functools.partial(_filter_kernel, lh, tn),
        out_shape=jax.ShapeDtypeStruct((BRANCH_W, n2), F32),
        grid=(n2 // tn,),
        in_specs=[pl.BlockSpec((EMB_PAD, tn), lambda j: (0, j))] + [full(a) for a in args[1:]],
        out_specs=pl.BlockSpec((BRANCH_W, tn), lambda j: (0, j)),
        compiler_params=_cparams(("parallel",)),
        name="hyena_filter",
    )(*args)


def _hyena_conv_kernel(nblk, bp, nch, k_ref, u_ref, o_ref, t_scr):
    ntile = 4 * nblk - 1
    mc = 2 * nblk - 1
    width = (ntile + 1) * LANES
    for ch in range(nch):
        big = pltpu.roll(jnp.broadcast_to(k_ref[ch], (LANES, width)), width - (LANES - 1), 1, stride=1, stride_axis=0)
        for m in range(ntile):
            t_scr[ch, m] = big[:, m * LANES:(m + 1) * LANES].astype(BF16)

    for ch in range(nch):
        for d in [0] + [s * a for a in range(1, nblk) for s in (1, -1)]:
            m0 = 2 * d + mc
            w = jnp.concatenate([jnp.concatenate([t_scr[ch, m0], t_scr[ch, m0 + 1]], axis=1),
                                 jnp.concatenate([t_scr[ch, m0 - 1], t_scr[ch, m0]], axis=1)], axis=0)
            i0, i1 = max(0, d), min(nblk, nblk + d)
            lhs = u_ref[ch, (i0 - d) * bp:(i1 - d) * bp, :].astype(BF16)
            res = jnp.dot(lhs, w, preferred_element_type=F32)
            if d == 0:
                o_ref[ch] = res
            else:
                o_ref[ch, i0 * bp:i1 * bp, :] += res


def hyena_conv(u, ktab):
    nb, l, c = u.shape
    nblk = l // HY_BLK
    bp = -(-nb // SUBLANES) * SUBLANES
    nch = SUBLANES if nblk == 1 else 1
    ut = jnp.transpose(u.astype(F32).reshape(nb, nblk, HY_BLK, c), (3, 1, 0, 2))
    if bp != nb:
        ut = jnp.pad(ut, ((0, 0), (0, 0), (0, bp - nb), (0, 0)))
    ut = ut.reshape(c, nblk * bp, HY_BLK)
    k3 = ktab.reshape(c, 1, 4 * nblk * LANES)
    out = pl.pallas_call(
        functools.partial(_hyena_conv_kernel, nblk, bp, nch),
        out_shape=jax.ShapeDtypeStruct((c, nblk * bp, HY_BLK), F32),
        grid=(c // nch,),
        in_specs=[pl.BlockSpec((nch, 1, 4 * nblk * LANES), lambda ch: (ch, 0, 0)),
                  pl.BlockSpec((nch, nblk * bp, HY_BLK), lambda ch: (ch, 0, 0))],
        out_specs=pl.BlockSpec((nch, nblk * bp, HY_BLK), lambda ch: (ch, 0, 0)),
        scratch_shapes=[pltpu.VMEM((nch, 4 * nblk - 1, LANES, LANES), BF16)],
        compiler_params=_cparams(("parallel",)),
        name="hyena_conv",
    )(k3, ut)
    out = out.reshape(c, nblk, bp, HY_BLK)[:, :, :nb]
    return jnp.transpose(out, (2, 1, 3, 0)).reshape(nb, l, c)


def _route(logits, bias):
    s = jax.nn.sigmoid(logits)
    sel = s + bias
    srow = [s[e:e + 1] for e in range(N_EXPERTS)]
    row = [sel[e:e + 1] for e in range(N_EXPERTS)]
    best, gi = None, None
    for g in range(N_GROUPS):
        a, b, c, d = row[4 * g:4 * g + 4]
        hi1, lo1, hi2, lo2 = jnp.maximum(a, b), jnp.minimum(a, b), jnp.maximum(c, d), jnp.minimum(c, d)
        score = jnp.maximum(hi1, hi2) + jnp.maximum(jnp.minimum(hi1, hi2), jnp.maximum(lo1, lo2))
        if g == 0:
            best, gi = score, jnp.zeros(score.shape, jnp.int32)
        else:
            better = score > best
            gi = jnp.where(better, g, gi)
            best = jnp.where(better, score, best)
    neg = -jnp.inf
    msel = [jnp.where(gi == e // EXPERTS_PER_GROUP, row[e], neg) for e in range(N_EXPERTS)]

    def arg_first_max(vals):
        bv, bi = vals[0], jnp.zeros(vals[0].shape, jnp.int32)
        for e in range(1, N_EXPERTS):
            better = vals[e] > bv
            bi = jnp.where(better, e, bi)
            bv = jnp.where(better, vals[e], bv)
        return bi

    i1 = arg_first_max(msel)
    i2 = arg_first_max([jnp.where(i1 == e, neg, msel[e]) for e in range(N_EXPERTS)])
    w1 = sum(jnp.where(i1 == e, srow[e], 0.0) for e in range(N_EXPERTS))
    w2 = sum(jnp.where(i2 == e, srow[e], 0.0) for e in range(N_EXPERTS))
    den = w1 + w2
    g1, g2 = w1 / den, w2 / den
    rows = [jnp.where(i1 == e, g1, 0.0) + jnp.where(i2 == e, g2, 0.0) for e in range(N_EXPERTS)]
    rows.append(gi.astype(F32))
    rows.extend([jnp.zeros_like(g1)] * (ROUTE_ROWS - len(rows)))
    return jnp.concatenate(rows, axis=0)


MERGE_NB = 4


def _merge_kernel(ya_ref, x0_ref, u_ref, yc_ref, ycv_ref, yd_ref, h_ref, x_ref, *rest):
    mods, rest = rest[:3 * MERGE_NB], rest[3 * MERGE_NB:]
    skip_ref, wg_ref, wb_ref, wo_ref, g1_ref, b1_ref, rwt_ref, rb_ref = rest[:8]
    x1_o, hf_o = rest[8:10]
    gates_o = rest[10:]
    rows = MERGE_NB * TT
    flat = lambda ref: ref[...].reshape(rows, ref.shape[-1])
    yb = (flat(x0_ref) * (flat(yc_ref) + flat(u_ref) * skip_ref[...])).astype(BF16)
    ys = (flat(ya_ref), yb, flat(ycv_ref), flat(yd_ref))
    h = flat(h_ref)
    merged = None
    for n in range(N_BRANCHES):
        gate = jax.nn.sigmoid(jnp.dot(h, wg_ref[:, n * D_MODEL:(n + 1) * D_MODEL], preferred_element_type=F32))
        term = gate * jnp.dot(ys[n], wb_ref[n], preferred_element_type=F32)
        merged = term if merged is None else merged + term
    out = jnp.dot(merged.astype(BF16), wo_ref[...], preferred_element_type=F32)
    alpha = (2 * 2) ** 0.25
    for k in range(MERGE_NB):
        ga_ref, shf_ref, scf_ref = mods[3 * k:3 * k + 3]
        x1 = _ln(alpha * x_ref[k] + ga_ref[0] * out[k * TT:(k + 1) * TT]) * g1_ref[...] + b1_ref[...]
        hf = _ln(x1) * (1.0 + scf_ref[0]) + shf_ref[0]
        x1_o[k] = x1
        hf_o[k] = hf.astype(BF16)
        logits = lax.dot_general(rwt_ref[...], hf, (((1,), (1,)), ((), ())), precision=HI,
                                 preferred_element_type=F32)
        gates_o[k][...] = _route(logits, rb_ref[...])


def merge(ya, x0, u, yconv, ycv, yd, h, x_all, mod3, nct, skip, wg, wb, wo, ln_g, ln_b, rwt, rbias):
    nb, ta, d = x_all.shape
    nt = ta // TT
    mb = MERGE_NB
    assert nb % mb == 0
    nat = pl.BlockSpec((mb, TT, BRANCH_W), lambda b, t: (b, t, 0))
    wide = pl.BlockSpec((mb, TT, d), lambda b, t: (b, t, 0))

    def full(a):
        nd = a.ndim
        return pl.BlockSpec(a.shape, lambda b, t: (0,) * nd)

    def mod_spec(col, k):
        return pl.BlockSpec((1, 1, D_MODEL), lambda b, t: (jnp.where(t < nct, nb, b * mb + k), 0, col))

    mod_specs = [mod_spec(col, k) for k in range(mb) for col in (2, 3, 4)]
    consts = [skip.reshape(1, -1), wg, wb, wo, ln_g.reshape(1, -1), ln_b.reshape(1, -1), rwt, rbias.reshape(-1, 1)]
    gate_shape = jax.ShapeDtypeStruct((ROUTE_ROWS, nb // mb * ta), F32)
    gate_spec = pl.BlockSpec((ROUTE_ROWS, TT), lambda b, t: (0, b * nt + t))
    outs = pl.pallas_call(
        _merge_kernel,
        out_shape=[jax.ShapeDtypeStruct((nb, ta, d), F32), jax.ShapeDtypeStruct((nb, ta, d), BF16)] + [gate_shape] * mb,
        grid=(nb // mb, nt),
        in_specs=[nat] * 6 + [wide, wide] + mod_specs + [full(a) for a in consts],
        out_specs=[wide, wide] + [gate_spec] * mb,
        compiler_params=_cparams(("parallel", "parallel"), VMEM_LIMIT),
        name="merge",
    )(ya, x0, u, yconv, ycv, yd, h, x_all, *([mod3] * (3 * mb)), *consts)
    gates_t = jnp.stack([g.reshape(ROUTE_ROWS, nb // mb, ta) for g in outs[2:]], axis=2).reshape(ROUTE_ROWS, nb * ta)
    return outs[0], outs[1], gates_t


MOE_TILE = 1024
MOE_ALIGN = 2 * SUBLANES
MOE_CHUNK = 18 * MOE_ALIGN
MOE_SORTED = MOE_TILE + LANES
MOE_ROWS = MOE_SORTED + MOE_CHUNK
GID_ROW = N_EXPERTS
ROUTE_ROWS = 3 * SUBLANES
META_LANES = 2 * N_GROUPS
assert N_GROUPS * (MOE_ALIGN - 1) <= MOE_SORTED - MOE_TILE and MOE_ROWS % MOE_ALIGN == 0


def _moe_sort_kernel(gt_ref, g_ref, h_ref, up_ref, hs_o, gs_o, pt_o, meta_o):
    gid = gt_ref[GID_ROW:GID_ROW + 1, :]
    onehot = [jnp.where(gid == float(g), 1.0, 0.0) for g in range(N_GROUPS)]
    g4 = jnp.concatenate(onehot + [jnp.zeros((SUBLANES - N_GROUPS, MOE_TILE), F32)], axis=0)
    before = jnp.dot(g4.astype(BF16), up_ref[...], preferred_element_type=F32)
    lane = lax.broadcasted_iota(jnp.int32, (SUBLANES, LANES), 1)
    meta = jnp.zeros((SUBLANES, LANES), F32)
    off = jnp.zeros((1, 1), F32)
    pos = jnp.zeros((1, MOE_TILE), F32)
    for g in range(N_GROUPS):
        cnt = jnp.sum(onehot[g], axis=1, keepdims=True)
        pos = pos + onehot[g] * (before[g:g + 1] + off)
        meta = jnp.where(lane == g, off, meta)
        meta = jnp.where(lane == N_GROUPS + g, cnt, meta)
        off = off + jnp.ceil(cnt * (1.0 / MOE_ALIGN)) * MOE_ALIGN
    meta_o[0] = meta.astype(jnp.int32)
    row = lax.broadcasted_iota(jnp.int32, (MOE_SORTED, MOE_TILE), 0)
    place = jnp.where(row == pos.astype(jnp.int32), 1.0, 0.0)
    p16 = place.astype(BF16)
    hs_o[0, :MOE_SORTED, :] = jnp.dot(p16, h_ref[...], preferred_element_type=F32).astype(BF16)
    hs_o[0, MOE_SORTED:, :] = jnp.zeros((MOE_ROWS - MOE_SORTED, h_ref.shape[1]), BF16)
    gts = g_ref[...]
    hi = gts.astype(BF16)
    r1 = gts - hi.astype(F32)
    mid = r1.astype(BF16)
    low = (r1 - mid.astype(F32)).astype(BF16)
    gs_o[0, :MOE_SORTED, :] = (jnp.dot(p16, hi, preferred_element_type=F32)
                               + jnp.dot(p16, mid, preferred_element_type=F32)
                               + jnp.dot(p16, low, preferred_element_type=F32))
    gs_o[0, MOE_SORTED:, :] = jnp.zeros((MOE_ROWS - MOE_SORTED, LANES), F32)
    pt_o[...] = place.T.astype(BF16)


def moe_sort(hf, gates_t, gates):
    n, d = hf.shape
    ntile = n // MOE_TILE
    upper = jnp.asarray(np.triu(np.ones((MOE_TILE, MOE_TILE), np.float32), 1), BF16)
    return pl.pallas_call(
        _moe_sort_kernel,
        out_shape=[jax.ShapeDtypeStruct((ntile, MOE_ROWS, d), BF16),
                   jax.ShapeDtypeStruct((ntile, MOE_ROWS, LANES), F32),
                   jax.ShapeDtypeStruct((n, MOE_SORTED), BF16),
                   jax.ShapeDtypeStruct((ntile, SUBLANES, LANES), jnp.int32)],
        grid=(ntile,),
        in_specs=[pl.BlockSpec((ROUTE_ROWS, MOE_TILE), lambda i: (0, i)),
                  pl.BlockSpec((MOE_TILE, LANES), lambda i: (i, 0)),
                  pl.BlockSpec((MOE_TILE, d), lambda i: (i, 0)),
                  pl.BlockSpec((MOE_TILE, MOE_TILE), lambda i: (0, 0))],
        out_specs=[pl.BlockSpec((1, MOE_ROWS, d), lambda i: (i, 0, 0)),
                   pl.BlockSpec((1, MOE_ROWS, LANES), lambda i: (i, 0, 0)),
                   pl.BlockSpec((MOE_TILE, MOE_SORTED), lambda i: (i, 0)),
                   pl.BlockSpec((1, SUBLANES, LANES), lambda i: (i, 0, 0))],
        compiler_params=_cparams(("parallel",), VMEM_LIMIT),
        name="moe_sort",
    )(gates_t, gates, hf, upper)


def _moe_group_kernel(meta_ref, hs_ref, gs_ref, w1_ref, w3_ref, w2_ref, prev_ref, ys_o):
    g, i = pl.program_id(0), pl.program_id(1)

    @pl.when(g == 0)
    def _():
        ys_o[...] = jnp.zeros_like(ys_o)

    @pl.when(g > 0)
    def _():
        ys_o[...] = prev_ref[...]

    off = meta_ref[i * META_LANES + g]
    cnt = meta_ref[i * META_LANES + N_GROUPS + g]
    lane = lax.broadcasted_iota(jnp.int32, (MOE_CHUNK, LANES), 1)

    def chunk(j, carry):
        rows = pl.ds(pl.multiple_of(off + j * MOE_CHUNK, MOE_ALIGN), MOE_CHUNK)
        hs = hs_ref[0, rows, :]
        gs = gs_ref[0, rows, :]
        acc = ys_o[0, rows, :].astype(F32)
        for e in range(EXPERTS_PER_GROUP):
            a = jnp.dot(hs, w1_ref[e], preferred_element_type=F32)
            b = jnp.dot(hs, w3_ref[e], preferred_element_type=F32)
            act = (a * jax.nn.sigmoid(a)) * b
            gcol = jnp.sum(jnp.where(lane == g * EXPERTS_PER_GROUP + e, gs, 0.0), axis=1, keepdims=True)
            acc = acc + gcol * jnp.dot(act.astype(BF16), w2_ref[e], preferred_element_type=F32)
        ys_o[0, rows, :] = acc.astype(BF16)
        return carry

    lax.fori_loop(0, lax.div(cnt + (MOE_CHUNK - 1), MOE_CHUNK), chunk, 0)


def moe_group(hs, gs, meta, w1, w3, w2, carried=None):
    ntile, _, d = hs.shape
    de = w1.shape[2]
    epg = EXPERTS_PER_GROUP
    assert ntile > 1
    tile_spec = pl.BlockSpec((1, MOE_ROWS, d), lambda g, i, m: (i, 0, 0))
    grid_spec = pltpu.PrefetchScalarGridSpec(
        num_scalar_prefetch=1,
        grid=(N_GROUPS, ntile),
        in_specs=[tile_spec,
                  pl.BlockSpec((1, MOE_ROWS, LANES), lambda g, i, m: (i, 0, 0)),
                  pl.BlockSpec((epg, d, de), lambda g, i, m: (g, 0, 0)),
                  pl.BlockSpec((epg, d, de), lambda g, i, m: (g, 0, 0)),
                  pl.BlockSpec((epg, de, d), lambda g, i, m: (g, 0, 0)),
                  pl.BlockSpec((1, MOE_ROWS, d), lambda g, i, m: (jnp.where(g == 0, ntile - 1, i), 0, 0))],
        out_specs=tile_spec,
    )
    if carried is None:
        carried = jnp.zeros((ntile, MOE_ROWS, d), BF16)
    return pl.pallas_call(
        _moe_group_kernel,
        out_shape=jax.ShapeDtypeStruct((ntile, MOE_ROWS, d), BF16),
        grid_spec=grid_spec,
        input_output_aliases={6: 0},
        compiler_params=_cparams(("arbitrary", "arbitrary"), VMEM_LIMIT),
        name="moe_group",
    )(meta, hs, gs, w1, w3, w2, carried)


def _ln2_kernel(emit_next, x_ref, pt_ref, ys_ref, gf_ref, g_ref, b_ref, *rest):
    alpha = (2 * 2) ** 0.25
    f = jnp.dot(pt_ref[...], ys_ref[0], preferred_element_type=F32)
    x2 = _ln(alpha * x_ref[0] + gf_ref[0] * f) * g_ref[...] + b_ref[...]
    if emit_next:
        sh_ref, sc_ref, o_ref, h_o = rest
        h_o[0] = (_ln(x2) * (1.0 + sc_ref[0]) + sh_ref[0]).astype(BF16)
    else:
        o_ref, = rest
    o_ref[0] = x2


def ln2(x1, pt, ysb, mod3, nct, ln_g, ln_b, mod3_next):
    nb, ta, d = x1.shape
    nt = ta // TT
    per = MOE_TILE // TT
    wide = pl.BlockSpec((1, TT, d), lambda b, t: (b, t, 0))
    row = pl.BlockSpec((1, d), lambda b, t: (0, 0))
    in_specs = [wide, pl.BlockSpec((TT, MOE_SORTED), lambda b, t: (b * nt + t, 0)),
                pl.BlockSpec((1, MOE_SORTED, d), lambda b, t: ((b * nt + t) // per, 0, 0)),
                _mod_spec(5, nct, nb), row, row]
    args = [x1, pt, ysb, mod3, ln_g.reshape(1, -1), ln_b.reshape(1, -1)]
    if mod3_next is None:
        out_shape = jax.ShapeDtypeStruct((nb, ta - nct * TT, d), F32)
        out_specs = pl.BlockSpec((1, TT, d), lambda b, t: (b, jnp.maximum(t - nct, 0), 0))
    else:
        in_specs += [_mod_spec(0, nct, nb), _mod_spec(1, nct, nb)]
        args += [mod3_next, mod3_next]
        out_shape = [jax.ShapeDtypeStruct((nb, ta, d), F32), jax.ShapeDtypeStruct((nb, ta, d), BF16)]
        out_specs = [wide, wide]
    return pl.pallas_call(
        functools.partial(_ln2_kernel, mod3_next is not None),
        out_shape=out_shape,
        grid=(nb, nt),
        in_specs=in_specs,
        out_specs=out_specs,
        compiler_params=_cparams(("parallel", "arbitrary"), VMEM_LIMIT),
        name="ln2",
    )(*args)


def _rope_tables(l, lc):
    half = HEAD_DIM // 2
    inv = ROPE_THETA ** (-np.arange(0, half, 2, dtype=np.float64) / half)
    t = np.arange(l)
    rows, cols = t // GRID_W, t % GRID_W
    ang = np.concatenate([rows[:, None] * inv, cols[:, None] * inv], -1)
    ang = np.concatenate([np.zeros((lc, half)), ang], 0)
    cos = np.repeat(np.cos(ang), 2, axis=1)
    sin = np.repeat(np.sin(ang), 2, axis=1)
    sin[:, 0::2] *= -1.0
    reps = Q_W // HEAD_DIM
    return (jnp.asarray(np.tile(cos, (1, reps)), F32), jnp.asarray(np.tile(sin, (1, reps)), F32))


def _block_diag_ones():
    i = np.arange(BRANCH_W) // HEAD_DIM
    return jnp.asarray((i[:, None] == i[None, :]).astype(np.float32))


def kernel(x, c, ctx, c_ctx, ada_w, ada_b, w_in, rwkv_mu, rwkv_w0, rwkv_w_up, rwkv_a0, rwkv_a_up, rwkv_g_up, rwkv_k_k, rwkv_k_a, rwkv_r_k, rwkv_lnx_g, rwkv_lnx_b, hyena_conv, hyena_w1, hyena_b1, hyena_freq1, hyena_w2, hyena_b2, hyena_freq2, hyena_w3, hyena_skip, sconv_w, attn_q_norm, attn_k_norm, w_branch, w_out, ln1_g, ln1_b, ln2_g, ln2_b, router_w, router_bias, exp_w1, exp_w3, exp_w2):
    nb, l, d = x.shape
    lc = ctx.shape[1]
    depth = ada_w.shape[0]
    assert d == D_MODEL and lc % TT == 0 and l % TT == 0 and l % GRID_W == 0 and (nb * (lc + l)) % MOE_TILE == 0
    ta = lc + l
    nct = lc // TT

    mod_rows = -(-(nb + 1) // SUBLANES) * SUBLANES
    cc = jnp.zeros((mod_rows, d), F32).at[:nb].set(c).at[nb].set(c_ctx)
    mod = ada_mod(cc, ada_w, ada_b)

    cos_t, sin_t = _rope_tables(l, lc)
    bd = _block_diag_ones()
    rwt = router_w.T

    ysb = None
    for li in range(depth):
        mod3 = mod[li].reshape(mod_rows, 1, N_MOD * d)
        wl = w_in[li].astype(BF16)
        if li == 0:
            x_all, h3 = lnmod(ctx, x, mod3, nct, 0, 1)
        h = h3.reshape(nb * ta, d)
        p_rwkv = matmul(h, wl[:, :OFF_HYENA], BF16).reshape(nb, ta, -1)
        p_hs = matmul(h, wl[:, OFF_HYENA:OFF_ATTN], BF16).reshape(nb, ta, -1)
        p_attn = matmul(h, wl[:, OFF_ATTN:OFF_GATE], BF16).reshape(nb, ta, -1)

        r, kk, w0, w1, k0, k1, b0, b1, v, g, bon = rwkv_prep(
            p_rwkv, nct, rwkv_mu[li], rwkv_w0[li], rwkv_w_up[li], rwkv_a0[li], rwkv_a_up[li], rwkv_g_up[li],
            rwkv_k_k[li], rwkv_k_a[li], rwkv_r_k[li], bd)
        yf, yb = rwkv_scan(r, kk, v, w0, w1, k0, k1, b0, b1, lc)
        ya = rwkv_out(yf, yb, g, bon, rwkv_lnx_g[li], rwkv_lnx_b[li], bd)

        q, kx, vx = attn_prep(p_attn, cos_t, sin_t, attn_q_norm[li], attn_k_norm[li], bd)
        yd = attention(q, kx, vx, lc)

        x0, u, ycv = hs_pre(p_hs, nct, hyena_conv[li], sconv_w[li])
        fargs = (hyena_w1[li], hyena_b1[li], hyena_freq1[li], hyena_w2[li], hyena_b2[li], hyena_freq2[li],
                 hyena_w3[li])
        yconv_ctx = hyena_conv_seg(u[:, :lc], fargs) if li < depth - 1 else jnp.zeros((nb, lc, BRANCH_W), F32)
        yconv = jnp.concatenate([yconv_ctx, hyena_conv_seg(u[:, lc:], fargs)], axis=1)

        x1, hf, gates_t = merge(ya, x0, u, yconv, ycv, yd, h3, x_all, mod3, nct, hyena_skip[li], wl[:, OFF_GATE:],
                                w_branch[li].astype(BF16), w_out[li].astype(BF16), ln1_g[li], ln1_b[li],
                                rwt, router_bias)
        gates = jnp.pad(gates_t.T, ((0, 0), (0, LANES - ROUTE_ROWS)))
        hs, gs, pt, meta = moe_sort(hf.reshape(nb * ta, d), gates_t, gates)
        ysb = moe_group(hs, gs, meta[:, 0, :META_LANES].reshape(-1), exp_w1[li].astype(BF16),
                        exp_w3[li].astype(BF16), exp_w2[li].astype(BF16), carried=ysb)
        if li == depth - 1:
            return ln2(x1, pt, ysb, mod3, nct, ln2_g[li], ln2_b[li], None)
        x_all, h3 = ln2(x1, pt, ysb, mod3, nct, ln2_g[li], ln2_b[li], mod[li + 1].reshape(mod_rows, 1, N_MOD * d))


def hyena_conv_seg(u_seg, fargs):
    ktab = hyena_filter_table(u_seg.shape[1], *fargs)
    return hyena_conv(u_seg, ktab)
```

```python
import functools
import math

import numpy as np
import jax
import jax.numpy as jnp
from jax import lax
from jax.experimental import pallas as pl
from jax.experimental.pallas import tpu as pltpu

F32 = jnp.float32
BF16 = jnp.bfloat16
HI = lax.Precision.HIGHEST

D_MODEL = 1024
GRID_W = 64
BRANCH_W = 256
HEAD_DIM = 64
N_BRANCHES = 4
N_MOD = 6
RWKV_COLS = 1024
RWKV_GN_EPS = 64e-5
HYENA_COLS = 768
HYENA_EMB = 33
HYENA_FAST_DECAY = 0.3
HYENA_SLOW_DECAY = 1.5
HYENA_TARGET = 1e-2
SCONV_COLS = 768
Q_W = 256
KV_W = 128
ATTN_COLS = 512
ROPE_THETA = 10000.0
RMS_EPS = 1e-6
OFF_HYENA = RWKV_COLS
OFF_SCONV = OFF_HYENA + HYENA_COLS
OFF_ATTN = OFF_SCONV + SCONV_COLS
OFF_GATE = OFF_ATTN + ATTN_COLS
N_EXPERTS = 16
N_GROUPS = 4
EXPERTS_PER_GROUP = 4
LN_EPS = 1e-6

SUBLANES = 8
LANES = 128
TT = 256
SCAN_BLK = LANES
HY_BLK = 256
VMEM_LIMIT = 56 * 1024 * 1024


def _cparams(sem, vmem=None):
    return pltpu.CompilerParams(dimension_semantics=sem, vmem_limit_bytes=vmem)


def _ln(xf):
    mu = jnp.mean(xf, -1, keepdims=True)
    xc = xf - mu
    var = jnp.mean(xc * xc, -1, keepdims=True)
    return xc * lax.rsqrt(var + LN_EPS)


def _head_sums(x, ones_bd):
    ones16 = ones_bd.astype(BF16)
    hi = x.astype(BF16)
    lo = (x - hi.astype(F32)).astype(BF16)
    return jnp.dot(hi, ones16, preferred_element_type=F32) + jnp.dot(lo, ones16, preferred_element_type=F32)


def _pick_tile(n, cands):
    for c in cands:
        if n % c == 0:
            return c
    raise ValueError(f"no tile for {n}")


def _ada_kernel(c_ref, w_ref, b_ref, o_ref):
    c = c_ref[...]
    a = c * jax.nn.sigmoid(c)
    o_ref[0] = jnp.dot(a, w_ref[0], precision=HI, preferred_element_type=F32) + b_ref[0]


def ada_mod(cc, ada_w, ada_b):
    depth, d, n = ada_w.shape
    rows = cc.shape[0]
    return pl.pallas_call(
        _ada_kernel,
        out_shape=jax.ShapeDtypeStruct((depth, rows, n), F32),
        grid=(depth, n // d),
        in_specs=[pl.BlockSpec((rows, d), lambda l, j: (0, 0)),
                  pl.BlockSpec((1, d, d), lambda l, j: (l, 0, j)),
                  pl.BlockSpec((1, 1, d), lambda l, j: (l, 0, j))],
        out_specs=pl.BlockSpec((1, rows, d), lambda l, j: (l, 0, j)),
        compiler_params=_cparams(("parallel", "parallel"), VMEM_LIMIT),
        name="ada_mod",
    )(cc, ada_w, ada_b.reshape(depth, 1, n))


def _lnmod_kernel(nct, c_ref, x_ref, sh_ref, sc_ref, xa_o, h_o):
    t = pl.program_id(1)
    xin = jnp.where(t < nct, c_ref[0], x_ref[0])
    xa_o[0] = xin
    h_o[0] = (_ln(xin) * (1.0 + sc_ref[0]) + sh_ref[0]).astype(BF16)


def _mod_spec(col, nct, nb):
    return pl.BlockSpec((1, 1, D_MODEL), lambda b, t: (jnp.where(t < nct, nb, b), 0, col))


def lnmod(ctx, x, mod3, nct, col_shift, col_scale):
    nb, l, d = x.shape
    ta = ctx.shape[1] + l
    wide = pl.BlockSpec((1, TT, d), lambda b, t: (b, t, 0))
    return pl.pallas_call(
        functools.partial(_lnmod_kernel, nct),
        out_shape=[jax.ShapeDtypeStruct((nb, ta, d), F32), jax.ShapeDtypeStruct((nb, ta, d), BF16)],
        grid=(nb, ta // TT),
        in_specs=[pl.BlockSpec((1, TT, d), lambda b, t: (b, jnp.minimum(t, nct - 1), 0)),
                  pl.BlockSpec((1, TT, d), lambda b, t: (b, jnp.maximum(t - nct, 0), 0)),
                  _mod_spec(col_shift, nct, nb), _mod_spec(col_scale, nct, nb)],
        out_specs=[wide, wide],
        compiler_params=_cparams(("parallel", "arbitrary")),
        name="lnmod",
    )(ctx, x, mod3, mod3)


def _mm_kernel(a_ref, b_ref, o_ref):
    o_ref[...] = jnp.dot(a_ref[...], b_ref[...], preferred_element_type=F32).astype(o_ref.dtype)


def matmul(a, b, out_dtype=F32):
    m, k = a.shape
    _, n = b.shape
    tm = _pick_tile(m, (1024, 512, 256))
    tn = _pick_tile(n, (1024, 512, 256))
    return pl.pallas_call(
        _mm_kernel,
        out_shape=jax.ShapeDtypeStruct((m, n), out_dtype),
        grid=(m // tm, n // tn),
        in_specs=[pl.BlockSpec((tm, k), lambda i, j: (i, 0)),
                  pl.BlockSpec((k, tn), lambda i, j: (0, j))],
        out_specs=pl.BlockSpec((tm, tn), lambda i, j: (i, j)),
        compiler_params=_cparams(("parallel", "parallel"), VMEM_LIMIT),
        name="matmul",
    )(a, b)


def _halo_specs(width, ta, halo=SUBLANES):
    nblk = ta // halo
    per = TT // halo
    cur = pl.BlockSpec((1, TT, width), lambda b, t: (b, t, 0))
    prev = pl.BlockSpec((1, halo, width), lambda b, t: (b, jnp.maximum(t * per - 1, 0), 0))
    nxt = pl.BlockSpec((1, halo, width), lambda b, t: (b, jnp.minimum((t + 1) * per, nblk - 1), 0))
    return [cur, prev, nxt]


def _neighbours(cur, prev_ref, next_ref, nct, nt):
    t = pl.program_id(1)
    seg_start = jnp.logical_or(t == 0, t == nct)
    seg_end = jnp.logical_or(t == nct - 1, t == nt - 1)
    halo = prev_ref.shape[1]
    prev_row = prev_ref[0][halo - 1:halo, :].astype(F32) * jnp.where(seg_start, 0.0, 1.0)
    next_row = next_ref[0][0:1, :].astype(F32) * jnp.where(seg_end, 0.0, 1.0)
    row = lax.broadcasted_iota(jnp.int32, (TT, 1), 0)
    xm1 = jnp.where(row == 0, prev_row, pltpu.roll(cur, 1, axis=0))
    xp1 = jnp.where(row == TT - 1, next_row, pltpu.roll(cur, TT - 1, axis=0))
    return xm1, xp1


def _rwkv_prep_kernel(nct, nt, cur_ref, prev_ref, next_ref, mu_ref, w0_ref, wup_ref, a0_ref, aup_ref,
                      gup_ref, kk_ref, ka_ref, rk_ref, bd_ref,
                      r_o, kk_o, w0_o, w1_o, k0_o, k1_o, b0_o, b1_o, v_o, g_o, bon_o):
    cur = cur_ref[0].astype(F32)
    xm1, xp1 = _neighbours(cur, prev_ref, next_ref, nct, nt)
    p = cur + mu_ref[...] * (0.5 * (xm1 + xp1) - cur)
    c = BRANCH_W
    r, k, v = p[:, 0:c], p[:, c:2 * c], p[:, 2 * c:3 * c]
    wd = p[:, 3 * c:3 * c + 64]
    ad = p[:, 3 * c + 64:3 * c + 128]
    gd = p[:, 3 * c + 128:3 * c + 256]
    bd = bd_ref[...]
    kk = k * kk_ref[...]
    ss = _head_sums(kk * kk, bd)
    kkn = kk * lax.rsqrt(jnp.maximum(ss, 1e-24))
    twd = jnp.tanh(wd)
    ka = ka_ref[...]
    kdirs = []
    w_outs, k_outs, b_outs = (w0_o, w1_o), (k0_o, k1_o), (b0_o, b1_o)
    for d in range(2):
        wlog = w0_ref[d:d + 1, :] + jnp.dot(twd, wup_ref[d], precision=HI, preferred_element_type=F32)
        decay = -math.exp(-0.5) * jax.nn.sigmoid(wlog)
        a = jax.nn.sigmoid(a0_ref[d:d + 1, :] + jnp.dot(ad, aup_ref[d], precision=HI, preferred_element_type=F32))
        kdir = k * (1.0 + (a - 1.0) * ka)
        bdir = kkn * a
        kdirs.append(kdir)
        w_outs[d][0] = decay
        k_outs[d][0] = kdir.astype(BF16)
        b_outs[d][0] = bdir.astype(BF16)
    r_o[0] = r.astype(BF16)
    kk_o[0] = kkn.astype(BF16)
    v_o[0] = v.astype(BF16)
    g_o[0] = jnp.dot(jax.nn.sigmoid(gd), gup_ref[...], precision=HI, preferred_element_type=F32)
    rkk = r * rk_ref[...] * (kdirs[0] + kdirs[1])
    bon_o[0] = _head_sums(rkk, bd) * v


def rwkv_prep(p_rwkv, nct, mu, w0, w_up, a0, a_up, g_up, k_k, k_a, r_k, bd):
    nb, ta, _ = p_rwkv.shape
    nt = ta // TT
    c = BRANCH_W
    nat = jax.ShapeDtypeStruct((nb, ta, c), F32)
    lo = jax.ShapeDtypeStruct((nb, ta, c), BF16)
    nat_spec = pl.BlockSpec((1, TT, c), lambda b, t: (b, t, 0))

    def full(a):
        nd = a.ndim
        return pl.BlockSpec(a.shape, lambda b, t: (0,) * nd)

    consts = [mu.reshape(1, -1), w0, w_up, a0, a_up, g_up, k_k.reshape(1, -1), k_a.reshape(1, -1),
              r_k.reshape(1, -1), bd]
    return pl.pallas_call(
        functools.partial(_rwkv_prep_kernel, nct, nt),
        out_shape=[lo, lo, nat, nat, lo, lo, lo, lo, lo, nat, nat],
        grid=(nb, nt),
        in_specs=_halo_specs(RWKV_COLS, ta, 2 * SUBLANES) + [full(a) for a in consts],
        out_specs=[nat_spec] * 11,
        compiler_params=_cparams(("parallel", "parallel"), VMEM_LIMIT),
        name="rwkv_prep",
    )(p_rwkv, p_rwkv, p_rwkv, *consts)


CHUNK = 32
PREP_NB = 4


def _chunk_scan_rows(x, reverse):
    pos = lax.broadcasted_iota(jnp.int32, x.shape, 0) % CHUNK
    step = 1
    while step < CHUNK:
        if reverse:
            x = x + jnp.where(pos < CHUNK - step, pltpu.roll(x, x.shape[0] - step, axis=0), 0.0)
        else:
            x = x + jnp.where(pos >= step, pltpu.roll(x, step, axis=0), 0.0)
        step *= 2
    return x


def _chunk_prep_kernel(r_ref, kk_ref, v_ref, lw0, k0, b0, lw1, k1, b1,
                       a0_o, bm0_o, rp0_o, y00_o, a1_o, bm1_o, rp1_o, y01_o):
    blk = SCAN_BLK
    npair = BRANCH_W // LANES
    nchunk = blk // CHUNK
    ti = lax.broadcasted_iota(jnp.int32, (blk, blk), 0)
    si = lax.broadcasted_iota(jnp.int32, (blk, blk), 1)
    same = (ti // CHUNK) == (si // CHUNK)
    eye = ti == si
    bd64 = (ti // HEAD_DIM) == (si // HEAD_DIM)
    head0 = si < HEAD_DIM
    lane_half = lax.broadcasted_iota(jnp.int32, (HEAD_DIM, LANES), 1)
    eyef = jnp.where(eye, 1.0, 0.0)
    dot = lambda x, y: jnp.dot(x, y, preferred_element_type=F32)
    lo = lambda x: x.astype(BF16)
    split = lambda x: jnp.concatenate([jnp.where(head0, x, 0.0), jnp.where(head0, 0.0, x)], axis=0)
    cat = lambda ms: lo(jnp.concatenate(ms, axis=1))

    probs = []
    for bi, d in [(bi, d) for bi in range(PREP_NB) for d in range(2)]:
        lw_ref, k_ref, b_ref = ((lw0, k0, b0), (lw1, k1, b1))[d]
        reverse = d == 1
        r, kk, v = (a[bi].astype(F32) for a in (r_ref, kk_ref, v_ref))
        lw, k, b = lw_ref[bi], k_ref[bi].astype(F32), b_ref[bi].astype(F32)
        lg = _chunk_scan_rows(lw, reverse)
        lg_end = lg + _chunk_scan_rows(lw, not reverse) - lw
        g, gi, g_end = jnp.exp(lg), jnp.exp(-lg), jnp.exp(lg_end)
        to_end = jnp.exp(lg_end - lg)
        arrs = (kk * jnp.exp(lg - lw), b * gi, k * gi, r * g, v, k * to_end, b * to_end, g_end)
        incl = jnp.logical_and(same, si >= ti if reverse else si <= ti)
        strict = jnp.logical_and(same, si > ti if reverse else si < ti)
        for p in range(npair):
            lanes = slice(p * LANES, (p + 1) * LANES)
            probs.append(dict(bi=bi, d=d, p=p, incl=incl, strict=strict, arrs=tuple(a[:, lanes] for a in arrs)))

    for q in probs:
        pp_, q_, kt_, rt_ = q["arrs"][:4]
        rhs_g = lo(jnp.concatenate([q_, kt_], axis=0))
        lm, mm, n2, nn = [], [], [], []
        for hh in range(2):
            hm = head0 if hh == 0 else jnp.logical_not(head0)
            lhs_g = lo(jnp.concatenate([jnp.where(hm, pp_, 0.0), jnp.where(hm, rt_, 0.0)], axis=0))
            gm = lax.dot_general(lhs_g, rhs_g, (((1,), (1,)), ((), ())), preferred_element_type=F32)
            lm.append(jnp.where(q["strict"], gm[:blk, :blk], 0.0))
            mm.append(jnp.where(q["strict"], gm[:blk, blk:], 0.0))
            n2.append(jnp.where(q["incl"], gm[blk:, :blk], 0.0))
            nn.append(jnp.where(q["incl"], gm[blk:, blk:], 0.0))
        q["pw"], q["tm"] = lm, [eyef - lm[0], eyef - lm[1]]
        q["m_cat"], q["n2_cat"], q["nn_cat"] = cat(mm), cat(n2), cat(nn)
    for _ in range(CHUNK.bit_length() - 2):
        for q in probs:
            pwl = [lo(x) for x in q["pw"]]
            q["pw"] = [dot(x, x) for x in pwl]
        for q in probs:
            q["tm"] = [dot(lo(t), lo(eyef + x)) for t, x in zip(q["tm"], q["pw"])]
    for q in probs:
        q["t_cat"] = cat(q["tm"])
        q["v_st"] = lo(split(q["arrs"][4]))
        q["pp"] = dot(q["t_cat"], lo(split(q["arrs"][0])))
        q["mv"] = dot(q["m_cat"], q["v_st"])
    for q in probs:
        q["w2"] = dot(q["t_cat"], lo(split(q["mv"])))
        q["rp"] = q["arrs"][3] - dot(q["n2_cat"], lo(split(q["pp"])))
    for q in probs:
        q["y0"] = dot(q["nn_cat"], q["v_st"]) - dot(q["n2_cat"], lo(split(q["w2"])))
    in_chunk = [si // CHUNK == c for c in range(nchunk)]
    for q in probs:
        ppt, vt, w2t = q["pp"].T, q["arrs"][4].T, q["w2"].T
        kg_, qg_ = q["arrs"][5], q["arrs"][6]
        lhs_a = jnp.concatenate([jnp.where(cm, ppt, 0.0) for cm in in_chunk], axis=0)
        q["pq"] = dot(lo(lhs_a), lo(qg_))
        lhs_b = jnp.concatenate([jnp.concatenate([jnp.where(cm, vt, 0.0), jnp.where(cm, -w2t, 0.0)], axis=1)
                                 for cm in in_chunk], axis=0)
        q["bf"] = dot(lo(lhs_b), lo(jnp.concatenate([kg_, qg_], axis=0)))
    outs = ((a0_o, bm0_o, rp0_o, y00_o), (a1_o, bm1_o, rp1_o, y01_o))
    for q in probs:
        a_o, bm_o = outs[q["d"]][:2]
        g_end_p = q["arrs"][7]
        for c in range(nchunk):
            pq_c = q["pq"][c * blk:(c + 1) * blk]
            a_o[q["bi"], 0, c, q["p"]] = (jnp.where(eye, g_end_p[c * CHUNK:c * CHUNK + 1], 0.0)
                                          - jnp.where(bd64, pq_c, 0.0)).astype(BF16)
            bm_o[q["bi"], 0, c, q["p"]] = jnp.where(lane_half < HEAD_DIM, q["bf"][c * blk:c * blk + HEAD_DIM],
                                                    q["bf"][c * blk + HEAD_DIM:(c + 1) * blk]).astype(BF16)
    for bi in range(PREP_NB):
        for d in range(2):
            rp_o, y0_o = outs[d][2:]
            rp_o[bi] = jnp.concatenate([q["rp"] for q in probs if q["d"] == d and q["bi"] == bi], axis=1)
            y0_o[bi] = jnp.concatenate([q["y0"] for q in probs if q["d"] == d and q["bi"] == bi], axis=1)


def chunk_prep(r, kk, v, lw0, lw1, k0, k1, b0, b1):
    nb, ta, c = r.shape
    nblk = ta // SCAN_BLK
    npair = c // LANES
    nchunk = SCAN_BLK // CHUNK
    pb = PREP_NB
    assert nb % pb == 0
    nat = pl.BlockSpec((pb, SCAN_BLK, c), lambda b, s: (b, s, 0))
    a_shape = jax.ShapeDtypeStruct((nb, nblk, nchunk, npair, LANES, LANES), BF16)
    bm_shape = jax.ShapeDtypeStruct((nb, nblk, nchunk, npair, HEAD_DIM, LANES), BF16)
    nat_shape = jax.ShapeDtypeStruct((nb, ta, c), F32)
    a_spec = pl.BlockSpec((pb, 1, nchunk, npair, LANES, LANES), lambda b, s: (b, s, 0, 0, 0, 0))
    bm_spec = pl.BlockSpec((pb, 1, nchunk, npair, HEAD_DIM, LANES), lambda b, s: (b, s, 0, 0, 0, 0))
    return pl.pallas_call(
        _chunk_prep_kernel,
        out_shape=[a_shape, bm_shape, nat_shape, nat_shape] * 2,
        grid=(nb // pb, nblk),
        in_specs=[nat] * 9,
        out_specs=[a_spec, bm_spec, nat, nat] * 2,
        compiler_params=_cparams(("parallel", "parallel"), VMEM_LIMIT),
        name="chunk_prep",
    )(r, kk, v, lw0, k0, b0, lw1, k1, b1)


def _chunk_scan_kernel(nb, a0, bm0, rp0, y00, a1, bm1, rp1, y01, yf_o, yb_o, s_scr):
    step = pl.program_id(0)
    npair = BRANCH_W // LANES
    nchunk = SCAN_BLK // CHUNK

    @pl.when(step == 0)
    def _():
        s_scr[...] = jnp.zeros_like(s_scr)

    lane = lax.broadcasted_iota(jnp.int32, (CHUNK, LANES), 1)
    refs = ((a0, bm0, rp0, y00, yf_o), (a1, bm1, rp1, y01, yb_o))
    for ci in range(nchunk):
        for d in range(2):
            a_ref, bm_ref, rp_ref, y0_ref, y_ref = refs[d]
            c = ci if d == 0 else nchunk - 1 - ci
            rows = slice(c * CHUNK, (c + 1) * CHUNK)
            for b in range(nb):
                for p in range(npair):
                    lanes = slice(p * LANES, (p + 1) * LANES)
                    s = s_scr[d, b, p]
                    rpc = rp_ref[b, rows, lanes]
                    lhs = jnp.concatenate([jnp.where(lane < HEAD_DIM, rpc, 0.0), jnp.where(lane >= HEAD_DIM, rpc, 0.0)],
                                          axis=0)
                    yh = lax.dot_general(lhs, s, (((1,), (1,)), ((), ())), preferred_element_type=F32)
                    y_ref[b, rows, lanes] = jnp.concatenate([yh[:CHUNK], yh[CHUNK:]], axis=1) + y0_ref[b, rows, lanes]
                    s_scr[d, b, p] = (jnp.dot(s.astype(BF16), a_ref[b, 0, c, p], preferred_element_type=F32)
                                      + bm_ref[b, 0, c, p])


def rwkv_scan(r, kk, v, lw0, lw1, k0, k1, b0, b1, lc):
    nb, ta, c = r.shape
    nblk = ta // SCAN_BLK
    nctb = lc // SCAN_BLK
    npair = c // LANES
    nchunk = SCAN_BLK // CHUNK
    a0, bm0, rp0, y00, a1, bm1, rp1, y01 = chunk_prep(r, kk, v, lw0, lw1, k0, k1, b0, b1)

    def fwd(s):
        return s

    def bwd(s):
        return jnp.where(s < nctb, nctb - 1 - s, nblk - 1 - (s - nctb))

    def specs(idx):
        return [pl.BlockSpec((nb, 1, nchunk, npair, LANES, LANES), lambda s: (0, idx(s), 0, 0, 0, 0)),
                pl.BlockSpec((nb, 1, nchunk, npair, HEAD_DIM, LANES), lambda s: (0, idx(s), 0, 0, 0, 0)),
                pl.BlockSpec((nb, SCAN_BLK, c), lambda s: (0, idx(s), 0)),
                pl.BlockSpec((nb, SCAN_BLK, c), lambda s: (0, idx(s), 0))]

    out = jax.ShapeDtypeStruct((nb, ta, c), F32)
    return pl.pallas_call(
        functools.partial(_chunk_scan_kernel, nb),
        out_shape=[out, out],
        grid=(nblk,),
        in_specs=specs(fwd) + specs(bwd),
        out_specs=[pl.BlockSpec((nb, SCAN_BLK, c), lambda s: (0, fwd(s), 0)),
                   pl.BlockSpec((nb, SCAN_BLK, c), lambda s: (0, bwd(s), 0))],
        scratch_shapes=[pltpu.VMEM((2, nb, npair, HEAD_DIM, LANES), F32)],
        compiler_params=_cparams(("arbitrary",), VMEM_LIMIT),
        name="chunk_scan",
    )(a0, bm0, rp0, y00, a1, bm1, rp1, y01)


def _rwkv_out_kernel(yf_ref, yb_ref, g_ref, bon_ref, lg_ref, lb_ref, bd_ref, o_ref):
    y = yf_ref[0] + yb_ref[0]
    bd = bd_ref[...]
    mu = _head_sums(y, bd) * (1.0 / HEAD_DIM)
    yc = y - mu
    var = _head_sums(yc * yc, bd) * (1.0 / HEAD_DIM)
    yn = yc * lax.rsqrt(var + RWKV_GN_EPS) * lg_ref[...] + lb_ref[...]
    o_ref[0] = ((yn + bon_ref[0]) * g_ref[0]).astype(BF16)


def rwkv_out(yf, yb, g, bon, lnx_g, lnx_b, bd):
    nb, ta, _ = yf.shape
    nat_spec = pl.BlockSpec((1, TT, BRANCH_W), lambda b, t: (b, t, 0))
    row = pl.BlockSpec((1, BRANCH_W), lambda b, t: (0, 0))
    return pl.pallas_call(
        _rwkv_out_kernel,
        out_shape=jax.ShapeDtypeStruct((nb, ta, BRANCH_W), BF16),
        grid=(nb, ta // TT),
        in_specs=[nat_spec] * 4 + [row, row, pl.BlockSpec(bd.shape, lambda b, t: (0, 0))],
        out_specs=nat_spec,
        compiler_params=_cparams(("parallel", "parallel")),
        name="rwkv_out",
    )(yf, yb, g, bon, lnx_g.reshape(1, -1), lnx_b.reshape(1, -1), bd)


def _pair_swap(x):
    lane = lax.broadcasted_iota(jnp.int32, x.shape, 1)
    n = x.shape[1]
    return jnp.where(lane % 2 == 0, pltpu.roll(x, n - 1, axis=1), pltpu.roll(x, 1, axis=1))


def _attn_prep_kernel(p_ref, cos_ref, sin_ref, qg_ref, kg_ref, bd_ref, q_o, k_o, v_o):
    p = p_ref[0].astype(F32)
    q, k, v = p[:, :Q_W], p[:, Q_W:Q_W + KV_W], p[:, Q_W + KV_W:]
    bd = bd_ref[...]
    cos, sin = cos_ref[...], sin_ref[...]
    qms = _head_sums(q * q, bd) * (1.0 / HEAD_DIM)
    qn = q * lax.rsqrt(qms + RMS_EPS) * qg_ref[...]
    qr = qn * cos + _pair_swap(qn) * sin
    q_o[0] = (qr * HEAD_DIM ** -0.5).astype(BF16)
    kms = _head_sums(k * k, bd[:KV_W, :KV_W]) * (1.0 / HEAD_DIM)
    kn = k * lax.rsqrt(kms + RMS_EPS) * kg_ref[...]
    kr = kn * cos[:, :KV_W] + _pair_swap(kn) * sin[:, :KV_W]
    for g in range(KV_W // HEAD_DIM):
        sl = slice(g * HEAD_DIM, (g + 1) * HEAD_DIM)
        k_o[0, g] = kr[:, sl].astype(BF16)
        v_o[0, g] = v[:, sl].astype(BF16)


def attn_prep(p_attn, cos_t, sin_t, q_norm, k_norm, bd):
    nb, ta, _ = p_attn.shape
    ng = KV_W // HEAD_DIM
    qg = jnp.tile(q_norm, Q_W // HEAD_DIM).reshape(1, -1)
    kg = jnp.tile(k_norm, ng).reshape(1, -1)
    kv_shape = jax.ShapeDtypeStruct((nb, ng, ta, HEAD_DIM), BF16)
    kv_spec = pl.BlockSpec((1, ng, TT, HEAD_DIM), lambda b, t: (b, 0, t, 0))
    return pl.pallas_call(
        _attn_prep_kernel,
        out_shape=[jax.ShapeDtypeStruct((nb, ta, Q_W), BF16), kv_shape, kv_shape],
        grid=(nb, ta // TT),
        in_specs=[pl.BlockSpec((1, TT, ATTN_COLS), lambda b, t: (b, t, 0)),
                  pl.BlockSpec((TT, Q_W), lambda b, t: (t, 0)),
                  pl.BlockSpec((TT, Q_W), lambda b, t: (t, 0)),
                  pl.BlockSpec((1, Q_W), lambda b, t: (0, 0)),
                  pl.BlockSpec((1, KV_W), lambda b, t: (0, 0)),
                  pl.BlockSpec(bd.shape, lambda b, t: (0, 0))],
        out_specs=[pl.BlockSpec((1, TT, Q_W), lambda b, t: (b, t, 0)), kv_spec, kv_spec],
        compiler_params=_cparams(("parallel", "parallel")),
        name="attn_prep",
    )(p_attn, cos_t, sin_t, qg, kg, bd)


def _attn_kernel(nct, lc, q_ref, k_ref, v_ref, o_ref):
    t = pl.program_id(1)
    ng = k_ref.shape[1]
    rep = Q_W // HEAD_DIM // ng

    def run(nk):
        outs = []
        for g in range(ng):
            kk, vv = k_ref[0, g, :nk, :], v_ref[0, g, :nk, :]
            for r in range(rep):
                h = g * rep + r
                q = q_ref[0][:, h * HEAD_DIM:(h + 1) * HEAD_DIM]
                s = lax.dot_general(q, kk, (((1,), (1,)), ((), ())), preferred_element_type=F32)
                p = jnp.exp(s - jnp.max(s, -1, keepdims=True))
                l = jnp.sum(p, -1, keepdims=True)
                o = jnp.dot(p.astype(BF16), vv, preferred_element_type=F32)
                outs.append(o / l)
        o_ref[0] = jnp.concatenate(outs, axis=1).astype(BF16)

    @pl.when(t < nct)
    def _():
        run(lc)

    @pl.when(t >= nct)
    def _():
        run(k_ref.shape[2])


def attention(q, k, v, lc):
    nb, ta, _ = q.shape
    ng = k.shape[1]
    nct = lc // TT
    qo_spec = pl.BlockSpec((1, TT, Q_W), lambda b, t: (b, t, 0))
    kv_spec = pl.BlockSpec((1, ng, ta, HEAD_DIM), lambda b, t: (b, 0, 0, 0))
    return pl.pallas_call(
        functools.partial(_attn_kernel, nct, lc),
        out_shape=jax.ShapeDtypeStruct((nb, ta, Q_W), BF16),
        grid=(nb, ta // TT),
        in_specs=[qo_spec, kv_spec, kv_spec],
        out_specs=qo_spec,
        compiler_params=_cparams(("parallel", "arbitrary"), VMEM_LIMIT),
        name="attention",
    )(q, k, v)


def _hs_pre_kernel(nct, nt, cur_ref, prev_ref, next_ref, hw_ref, sw_ref, x0_o, u_o, ycv_o):
    cur = cur_ref[0].astype(F32)
    xm1, xp1 = _neighbours(cur, prev_ref, next_ref, nct, nt)
    c = BRANCH_W
    hc = HYENA_COLS
    hw = hw_ref[...]
    ph = hw[0:1] * xm1[:, :hc] + hw[1:2] * cur[:, :hc] + hw[2:3] * xp1[:, :hc]
    x0_o[0] = ph[:, :c].astype(BF16)
    u_o[0] = (ph[:, c:2 * c] * ph[:, 2 * c:3 * c]).astype(BF16)
    sw = sw_ref[...]

    def cx(a):
        return a[:, hc + c:hc + 2 * c] * a[:, hc + 2 * c:hc + 3 * c]

    conv = sw[0:1] * cx(xm1) + sw[1:2] * cx(cur) + sw[2:3] * cx(xp1)
    ycv_o[0] = (cur[:, hc:hc + c] * conv).astype(BF16)


def hs_pre(p_hs, nct, hyena_conv, sconv_w):
    nb, ta, w = p_hs.shape
    nt = ta // TT
    nat = pl.BlockSpec((1, TT, BRANCH_W), lambda b, t: (b, t, 0))
    return pl.pallas_call(
        functools.partial(_hs_pre_kernel, nct, nt),
        out_shape=[jax.ShapeDtypeStruct((nb, ta, BRANCH_W), BF16)] * 3,
        grid=(nb, nt),
        in_specs=_halo_specs(w, ta, 2 * SUBLANES) + [pl.BlockSpec(hyena_conv.shape, lambda b, t: (0, 0)),
                                       pl.BlockSpec(sconv_w.shape, lambda b, t: (0, 0))],
        out_specs=[nat, nat, nat],
        compiler_params=_cparams(("parallel", "parallel"), VMEM_LIMIT),
        name="hs_pre",
    )(p_hs, p_hs, p_hs, hyena_conv, sconv_w)


EMB_PAD = 40


def _filter_tables(lh):
    n = np.arange(2 * lh)
    pos = np.abs(n - (lh - 1)).astype(np.float64)
    bands = (HYENA_EMB - 1) // 2
    t = np.minimum(pos, lh - 1) / (lh - 1)
    wpos = 2.0 * math.pi * pos / lh
    f = np.linspace(1e-4, bands - 1, bands)[:, None]
    z = np.zeros((EMB_PAD, 2 * lh), np.float32)
    z[0] = t
    z[1:1 + bands] = np.cos(f * wpos[None, :])
    z[1 + bands:1 + 2 * bands] = -np.sin(f * wpos[None, :])
    max_decay = math.log(HYENA_TARGET) / HYENA_FAST_DECAY
    min_decay = math.log(HYENA_TARGET) / HYENA_SLOW_DECAY
    deltas = np.abs(np.linspace(min_decay, max_decay, BRANCH_W)).astype(np.float32)
    return z, deltas.reshape(-1, 1)


def _filter_kernel(lh, tn, z_ref, w1_ref, b1_ref, f1_ref, w2_ref, b2_ref, f2_ref, w3_ref, dl_ref, o_ref):
    z = z_ref[...]
    h1 = jnp.sin(f1_ref[...] * (jnp.dot(w1_ref[...], z, precision=HI, preferred_element_type=F32) + b1_ref[...]))
    h2 = jnp.sin(f2_ref[...] * (jnp.dot(w2_ref[...], h1, precision=HI, preferred_element_type=F32) + b2_ref[...]))
    f = jnp.dot(w3_ref[...], h2, precision=HI, preferred_element_type=F32)
    n = pl.program_id(0) * tn + lax.broadcasted_iota(jnp.int32, (1, tn), 1)
    filt = jnp.where(n >= lh - 1, f[:BRANCH_W], f[BRANCH_W:])
    win = jnp.exp(-z[0:1, :] * dl_ref[...])
    o_ref[...] = jnp.where(n == 2 * lh - 1, 0.0, filt * win)


def hyena_filter_table(lh, w1, b1, f1, w2, b2, f2, w3):
    z_np, dl_np = _filter_tables(lh)
    n2 = 2 * lh
    tn = _pick_tile(n2, (1024, 512))
    hd = w2.shape[0]
    w1t = jnp.zeros((hd, EMB_PAD), F32).at[:, :HYENA_EMB].set(w1.T)
    args = [jnp.asarray(z_np), w1t, b1.reshape(-1, 1), f1.reshape(-1, 1), w2.T, b2.reshape(-1, 1),
            f2.reshape(-1, 1), w3.T, jnp.asarray(dl_np)]

    def full(a):
        return pl.BlockSpec(a.shape, lambda j: (0, 0))

    return pl.pallas_call(
        functools.partial(_filter_kernel, lh, tn),
        out_shape=jax.ShapeDtypeStruct((BRANCH_W, n2), F32),
        grid=(n2 // tn,),
        in_specs=[pl.BlockSpec((EMB_PAD, tn), lambda j: (0, j))] + [full(a) for a in args[1:]],
        out_specs=pl.BlockSpec((BRANCH_W, tn), lambda j: (0, j)),
        compiler_params=_cparams(("parallel",)),
        name="hyena_filter",
    )(*args)


def _hyena_conv_kernel(nblk, bp, nch, k_ref, u_ref, o_ref, t_scr):
    ntile = 4 * nblk - 1
    mc = 2 * nblk - 1
    width = (ntile + 1) * LANES
    for ch in range(nch):
        big = pltpu.roll(jnp.broadcast_to(k_ref[ch], (LANES, width)), width - (LANES - 1), 1, stride=1, stride_axis=0)
        for m in range(ntile):
            t_scr[ch, m] = big[:, m * LANES:(m + 1) * LANES].astype(BF16)

    for ch in range(nch):
        for d in [0] + [s * a for a in range(1, nblk) for s in (1, -1)]:
            m0 = 2 * d + mc
            w = jnp.concatenate([jnp.concatenate([t_scr[ch, m0], t_scr[ch, m0 + 1]], axis=1),
                                 jnp.concatenate([t_scr[ch, m0 - 1], t_scr[ch, m0]], axis=1)], axis=0)
            i0, i1 = max(0, d), min(nblk, nblk + d)
            lhs = u_ref[ch, (i0 - d) * bp:(i1 - d) * bp, :].astype(BF16)
            res = jnp.dot(lhs, w, preferred_element_type=F32)
            if d == 0:
                o_ref[ch] = res
            else:
                o_ref[ch, i0 * bp:i1 * bp, :] += res


def hyena_conv(u, ktab):
    nb, l, c = u.shape
    nblk = l // HY_BLK
    bp = -(-nb // SUBLANES) * SUBLANES
    nch = SUBLANES if nblk == 1 else 2
    ut = jnp.transpose(u.astype(F32).reshape(nb, nblk, HY_BLK, c), (3, 1, 0, 2))
    if bp != nb:
        ut = jnp.pad(ut, ((0, 0), (0, 0), (0, bp - nb), (0, 0)))
    ut = ut.reshape(c, nblk * bp, HY_BLK)
    k3 = ktab.reshape(c, 1, 4 * nblk * LANES)
    out = pl.pallas_call(
        functools.partial(_hyena_conv_kernel, nblk, bp, nch),
        out_shape=jax.ShapeDtypeStruct((c, nblk * bp, HY_BLK), F32),
        grid=(c // nch,),
        in_specs=[pl.BlockSpec((nch, 1, 4 * nblk * LANES), lambda ch: (ch, 0, 0)),
                  pl.BlockSpec((nch, nblk * bp, HY_BLK), lambda ch: (ch, 0, 0))],
        out_specs=pl.BlockSpec((nch, nblk * bp, HY_BLK), lambda ch: (ch, 0, 0)),
        scratch_shapes=[pltpu.VMEM((nch, 4 * nblk - 1, LANES, LANES), BF16)],
        compiler_params=_cparams(("parallel",)),
        name="hyena_conv",
    )(k3, ut)
    out = out.reshape(c, nblk, bp, HY_BLK)[:, :, :nb]
    return jnp.transpose(out, (2, 1, 3, 0)).reshape(nb, l, c)


def _route(logits, bias):
    s = jax.nn.sigmoid(logits)
    sel = s + bias
    srow = [s[e:e + 1] for e in range(N_EXPERTS)]
    row = [sel[e:e + 1] for e in range(N_EXPERTS)]
    best, gi = None, None
    for g in range(N_GROUPS):
        a, b, c, d = row[4 * g:4 * g + 4]
        hi1, lo1, hi2, lo2 = jnp.maximum(a, b), jnp.minimum(a, b), jnp.maximum(c, d), jnp.minimum(c, d)
        score = jnp.maximum(hi1, hi2) + jnp.maximum(jnp.minimum(hi1, hi2), jnp.maximum(lo1, lo2))
        if g == 0:
            best, gi = score, jnp.zeros(score.shape, jnp.int32)
        else:
            better = score > best
            gi = jnp.where(better, g, gi)
            best = jnp.where(better, score, best)
    neg = -jnp.inf
    msel = [jnp.where(gi == e // EXPERTS_PER_GROUP, row[e], neg) for e in range(N_EXPERTS)]

    def arg_first_max(vals):
        bv, bi = vals[0], jnp.zeros(vals[0].shape, jnp.int32)
        for e in range(1, N_EXPERTS):
            better = vals[e] > bv
            bi = jnp.where(better, e, bi)
            bv = jnp.where(better, vals[e], bv)
        return bi

    i1 = arg_first_max(msel)
    i2 = arg_first_max([jnp.where(i1 == e, neg, msel[e]) for e in range(N_EXPERTS)])
    w1 = sum(jnp.where(i1 == e, srow[e], 0.0) for e in range(N_EXPERTS))
    w2 = sum(jnp.where(i2 == e, srow[e], 0.0) for e in range(N_EXPERTS))
    den = w1 + w2
    g1, g2 = w1 / den, w2 / den
    rows = [jnp.where(i1 == e, g1, 0.0) + jnp.where(i2 == e, g2, 0.0) for e in range(N_EXPERTS)]
    rows.append(gi.astype(F32))
    rows.extend([jnp.zeros_like(g1)] * (ROUTE_ROWS - len(rows)))
    return jnp.concatenate(rows, axis=0)


MERGE_NB = 4


def _merge_kernel(ya_ref, x0_ref, u_ref, yc_ref, ycv_ref, yd_ref, h_ref, x_ref, *rest):
    mods, rest = rest[:3 * MERGE_NB], rest[3 * MERGE_NB:]
    skip_ref, wg_ref, wb_ref, wo_ref, g1_ref, b1_ref, rwt_ref, rb_ref = rest[:8]
    x1_o, hf_o = rest[8:10]
    gates_o = rest[10:]
    rows = MERGE_NB * TT
    flat = lambda ref: ref[...].reshape(rows, ref.shape[-1])
    yb = (flat(x0_ref) * (flat(yc_ref) + flat(u_ref) * skip_ref[...])).astype(BF16)
    ys = (flat(ya_ref), yb, flat(ycv_ref), flat(yd_ref))
    h = flat(h_ref)
    merged = None
    for n in range(N_BRANCHES):
        gate = jax.nn.sigmoid(jnp.dot(h, wg_ref[:, n * D_MODEL:(n + 1) * D_MODEL], preferred_element_type=F32))
        term = gate * jnp.dot(ys[n], wb_ref[n], preferred_element_type=F32)
        merged = term if merged is None else merged + term
    out = jnp.dot(merged.astype(BF16), wo_ref[...], preferred_element_type=F32)
    alpha = (2 * 2) ** 0.25
    for k in range(MERGE_NB):
        ga_ref, shf_ref, scf_ref = mods[3 * k:3 * k + 3]
        x1 = _ln(alpha * x_ref[k] + ga_ref[0] * out[k * TT:(k + 1) * TT]) * g1_ref[...] + b1_ref[...]
        hf = _ln(x1) * (1.0 + scf_ref[0]) + shf_ref[0]
        x1_o[k] = x1
        hf_o[k] = hf.astype(BF16)
        logits = lax.dot_general(rwt_ref[...], hf, (((1,), (1,)), ((), ())), precision=HI,
                                 preferred_element_type=F32)
        gates_o[k][...] = _route(logits, rb_ref[...])


def merge(ya, x0, u, yconv, ycv, yd, h, x_all, mod3, nct, skip, wg, wb, wo, ln_g, ln_b, rwt, rbias):
    nb, ta, d = x_all.shape
    nt = ta // TT
    mb = MERGE_NB
    assert nb % mb == 0
    nat = pl.BlockSpec((mb, TT, BRANCH_W), lambda b, t: (b, t, 0))
    wide = pl.BlockSpec((mb, TT, d), lambda b, t: (b, t, 0))

    def full(a):
        nd = a.ndim
        return pl.BlockSpec(a.shape, lambda b, t: (0,) * nd)

    def mod_spec(col, k):
        return pl.BlockSpec((1, 1, D_MODEL), lambda b, t: (jnp.where(t < nct, nb, b * mb + k), 0, col))

    mod_specs = [mod_spec(col, k) for k in range(mb) for col in (2, 3, 4)]
    consts = [skip.reshape(1, -1), wg, wb, wo, ln_g.reshape(1, -1), ln_b.reshape(1, -1), rwt, rbias.reshape(-1, 1)]
    gate_shape = jax.ShapeDtypeStruct((ROUTE_ROWS, nb // mb * ta), F32)
    gate_spec = pl.BlockSpec((ROUTE_ROWS, TT), lambda b, t: (0, b * nt + t))
    outs = pl.pallas_call(
        _merge_kernel,
        out_shape=[jax.ShapeDtypeStruct((nb, ta, d), F32), jax.ShapeDtypeStruct((nb, ta, d), BF16)] + [gate_shape] * mb,
        grid=(nb // mb, nt),
        in_specs=[nat] * 6 + [wide, wide] + mod_specs + [full(a) for a in consts],
        out_specs=[wide, wide] + [gate_spec] * mb,
        compiler_params=_cparams(("parallel", "parallel"), VMEM_LIMIT),
        name="merge",
    )(ya, x0, u, yconv, ycv, yd, h, x_all, *([mod3] * (3 * mb)), *consts)
    gates_t = jnp.stack([g.reshape(ROUTE_ROWS, nb // mb, ta) for g in outs[2:]], axis=2).reshape(ROUTE_ROWS, nb * ta)
    return outs[0], outs[1], gates_t


MOE_TILE = 1024
MOE_ALIGN = 2 * SUBLANES
MOE_CHUNK = 18 * MOE_ALIGN
MOE_SORTED = MOE_TILE + LANES
MOE_ROWS = MOE_SORTED + MOE_CHUNK
GID_ROW = N_EXPERTS
ROUTE_ROWS = 3 * SUBLANES
META_LANES = 2 * N_GROUPS
assert N_GROUPS * (MOE_ALIGN - 1) <= MOE_SORTED - MOE_TILE and MOE_ROWS % MOE_ALIGN == 0


def _moe_sort_kernel(gt_ref, g_ref, h_ref, up_ref, hs_o, gs_o, pt_o, meta_o):
    gid = gt_ref[GID_ROW:GID_ROW + 1, :]
    onehot = [jnp.where(gid == float(g), 1.0, 0.0) for g in range(N_GROUPS)]
    g4 = jnp.concatenate(onehot + [jnp.zeros((SUBLANES - N_GROUPS, MOE_TILE), F32)], axis=0)
    before = jnp.dot(g4.astype(BF16), up_ref[...], preferred_element_type=F32)
    lane = lax.broadcasted_iota(jnp.int32, (SUBLANES, LANES), 1)
    meta = jnp.zeros((SUBLANES, LANES), F32)
    off = jnp.zeros((1, 1), F32)
    pos = jnp.zeros((1, MOE_TILE), F32)
    for g in range(N_GROUPS):
        cnt = jnp.sum(onehot[g], axis=1, keepdims=True)
        pos = pos + onehot[g] * (before[g:g + 1] + off)
        meta = jnp.where(lane == g, off, meta)
        meta = jnp.where(lane == N_GROUPS + g, cnt, meta)
        off = off + jnp.ceil(cnt * (1.0 / MOE_ALIGN)) * MOE_ALIGN
    meta_o[0] = meta.astype(jnp.int32)
    row = lax.broadcasted_iota(jnp.int32, (MOE_SORTED, MOE_TILE), 0)
    place = jnp.where(row == pos.astype(jnp.int32), 1.0, 0.0)
    p16 = place.astype(BF16)
    hs_o[0, :MOE_SORTED, :] = jnp.dot(p16, h_ref[...], preferred_element_type=F32).astype(BF16)
    hs_o[0, MOE_SORTED:, :] = jnp.zeros((MOE_ROWS - MOE_SORTED, h_ref.shape[1]), BF16)
    gts = g_ref[...]
    hi = gts.astype(BF16)
    r1 = gts - hi.astype(F32)
    mid = r1.astype(BF16)
    low = (r1 - mid.astype(F32)).astype(BF16)
    gs_o[0, :MOE_SORTED, :] = (jnp.dot(p16, hi, preferred_element_type=F32)
                               + jnp.dot(p16, mid, preferred_element_type=F32)
                               + jnp.dot(p16, low, preferred_element_type=F32))
    gs_o[0, MOE_SORTED:, :] = jnp.zeros((MOE_ROWS - MOE_SORTED, LANES), F32)
    pt_o[...] = place.T.astype(BF16)


def moe_sort(hf, gates_t, gates):
    n, d = hf.shape
    ntile = n // MOE_TILE
    upper = jnp.asarray(np.triu(np.ones((MOE_TILE, MOE_TILE), np.float32), 1), BF16)
    return pl.pallas_call(
        _moe_sort_kernel,
        out_shape=[jax.ShapeDtypeStruct((ntile, MOE_ROWS, d), BF16),
                   jax.ShapeDtypeStruct((ntile, MOE_ROWS, LANES), F32),
                   jax.ShapeDtypeStruct((n, MOE_SORTED), BF16),
                   jax.ShapeDtypeStruct((ntile, SUBLANES, LANES), jnp.int32)],
        grid=(ntile,),
        in_specs=[pl.BlockSpec((ROUTE_ROWS, MOE_TILE), lambda i: (0, i)),
                  pl.BlockSpec((MOE_TILE, LANES), lambda i: (i, 0)),
                  pl.BlockSpec((MOE_TILE, d), lambda i: (i, 0)),
                  pl.BlockSpec((MOE_TILE, MOE_TILE), lambda i: (0, 0))],
        out_specs=[pl.BlockSpec((1, MOE_ROWS, d), lambda i: (i, 0, 0)),
                   pl.BlockSpec((1, MOE_ROWS, LANES), lambda i: (i, 0, 0)),
                   pl.BlockSpec((MOE_TILE, MOE_SORTED), lambda i: (i, 0)),
                   pl.BlockSpec((1, SUBLANES, LANES), lambda i: (i, 0, 0))],
        compiler_params=_cparams(("parallel",), VMEM_LIMIT),
        name="moe_sort",
    )(gates_t, gates, hf, upper)


def _moe_group_kernel(meta_ref, hs_ref, gs_ref, w1_ref, w3_ref, w2_ref, prev_ref, ys_o):
    g, i = pl.program_id(0), pl.program_id(1)

    @pl.when(g == 0)
    def _():
        ys_o[...] = jnp.zeros_like(ys_o)

    @pl.when(g > 0)
    def _():
        ys_o[...] = prev_ref[...]

    off = meta_ref[i * META_LANES + g]
    cnt = meta_ref[i * META_LANES + N_GROUPS + g]
    lane = lax.broadcasted_iota(jnp.int32, (MOE_CHUNK, LANES), 1)

    def chunk(j, carry):
        rows = pl.ds(pl.multiple_of(off + j * MOE_CHUNK, MOE_ALIGN), MOE_CHUNK)
        hs = hs_ref[0, rows, :]
        gs = gs_ref[0, rows, :]
        acc = ys_o[0, rows, :].astype(F32)
        for e in range(EXPERTS_PER_GROUP):
            a = jnp.dot(hs, w1_ref[e], preferred_element_type=F32)
            b = jnp.dot(hs, w3_ref[e], preferred_element_type=F32)
            act = (a * jax.nn.sigmoid(a)) * b
            gcol = jnp.sum(jnp.where(lane == g * EXPERTS_PER_GROUP + e, gs, 0.0), axis=1, keepdims=True)
            acc = acc + gcol * jnp.dot(act.astype(BF16), w2_ref[e], preferred_element_type=F32)
        ys_o[0, rows, :] = acc.astype(BF16)
        return carry

    lax.fori_loop(0, lax.div(cnt + (MOE_CHUNK - 1), MOE_CHUNK), chunk, 0)


def moe_group(hs, gs, meta, w1, w3, w2, carried=None):
    ntile, _, d = hs.shape
    de = w1.shape[2]
    epg = EXPERTS_PER_GROUP
    assert ntile > 1
    tile_spec = pl.BlockSpec((1, MOE_ROWS, d), lambda g, i, m: (i, 0, 0))
    grid_spec = pltpu.PrefetchScalarGridSpec(
        num_scalar_prefetch=1,
        grid=(N_GROUPS, ntile),
        in_specs=[tile_spec,
                  pl.BlockSpec((1, MOE_ROWS, LANES), lambda g, i, m: (i, 0, 0)),
                  pl.BlockSpec((epg, d, de), lambda g, i, m: (g, 0, 0)),
                  pl.BlockSpec((epg, d, de), lambda g, i, m: (g, 0, 0)),
                  pl.BlockSpec((epg, de, d), lambda g, i, m: (g, 0, 0)),
                  pl.BlockSpec((1, MOE_ROWS, d), lambda g, i, m: (jnp.where(g == 0, ntile - 1, i), 0, 0))],
        out_specs=tile_spec,
    )
    if carried is None:
        carried = jnp.zeros((ntile, MOE_ROWS, d), BF16)
    return pl.pallas_call(
        _moe_group_kernel,
        out_shape=jax.ShapeDtypeStruct((ntile, MOE_ROWS, d), BF16),
        grid_spec=grid_spec,
        input_output_aliases={6: 0},
        compiler_params=_cparams(("arbitrary", "arbitrary"), VMEM_LIMIT),
        name="moe_group",
    )(meta, hs, gs, w1, w3, w2, carried)


def _ln2_kernel(emit_next, x_ref, pt_ref, ys_ref, gf_ref, g_ref, b_ref, *rest):
    alpha = (2 * 2) ** 0.25
    f = jnp.dot(pt_ref[...], ys_ref[0], preferred_element_type=F32)
    x2 = _ln(alpha * x_ref[0] + gf_ref[0] * f) * g_ref[...] + b_ref[...]
    if emit_next:
        sh_ref, sc_ref, o_ref, h_o = rest
        h_o[0] = (_ln(x2) * (1.0 + sc_ref[0]) + sh_ref[0]).astype(BF16)
    else:
        o_ref, = rest
    o_ref[0] = x2


def ln2(x1, pt, ysb, mod3, nct, ln_g, ln_b, mod3_next):
    nb, ta, d = x1.shape
    nt = ta // TT
    per = MOE_TILE // TT
    wide = pl.BlockSpec((1, TT, d), lambda b, t: (b, t, 0))
    row = pl.BlockSpec((1, d), lambda b, t: (0, 0))
    in_specs = [wide, pl.BlockSpec((TT, MOE_SORTED), lambda b, t: (b * nt + t, 0)),
                pl.BlockSpec((1, MOE_SORTED, d), lambda b, t: ((b * nt + t) // per, 0, 0)),
                _mod_spec(5, nct, nb), row, row]
    args = [x1, pt, ysb, mod3, ln_g.reshape(1, -1), ln_b.reshape(1, -1)]
    if mod3_next is None:
        out_shape = jax.ShapeDtypeStruct((nb, ta - nct * TT, d), F32)
        out_specs = pl.BlockSpec((1, TT, d), lambda b, t: (b, jnp.maximum(t - nct, 0), 0))
    else:
        in_specs += [_mod_spec(0, nct, nb), _mod_spec(1, nct, nb)]
        args += [mod3_next, mod3_next]
        out_shape = [jax.ShapeDtypeStruct((nb, ta, d), F32), jax.ShapeDtypeStruct((nb, ta, d), BF16)]
        out_specs = [wide, wide]
    return pl.pallas_call(
        functools.partial(_ln2_kernel, mod3_next is not None),
        out_shape=out_shape,
        grid=(nb, nt),
        in_specs=in_specs,
        out_specs=out_specs,
        compiler_params=_cparams(("parallel", "arbitrary"), VMEM_LIMIT),
        name="ln2",
    )(*args)


def _rope_tables(l, lc):
    half = HEAD_DIM // 2
    inv = ROPE_THETA ** (-np.arange(0, half, 2, dtype=np.float64) / half)
    t = np.arange(l)
    rows, cols = t // GRID_W, t % GRID_W
    ang = np.concatenate([rows[:, None] * inv, cols[:, None] * inv], -1)
    ang = np.concatenate([np.zeros((lc, half)), ang], 0)
    cos = np.repeat(np.cos(ang), 2, axis=1)
    sin = np.repeat(np.sin(ang), 2, axis=1)
    sin[:, 0::2] *= -1.0
    reps = Q_W // HEAD_DIM
    return (jnp.asarray(np.tile(cos, (1, reps)), F32), jnp.asarray(np.tile(sin, (1, reps)), F32))


def _block_diag_ones():
    i = np.arange(BRANCH_W) // HEAD_DIM
    return jnp.asarray((i[:, None] == i[None, :]).astype(np.float32))


def kernel(x, c, ctx, c_ctx, ada_w, ada_b, w_in, rwkv_mu, rwkv_w0, rwkv_w_up, rwkv_a0, rwkv_a_up, rwkv_g_up, rwkv_k_k, rwkv_k_a, rwkv_r_k, rwkv_lnx_g, rwkv_lnx_b, hyena_conv, hyena_w1, hyena_b1, hyena_freq1, hyena_w2, hyena_b2, hyena_freq2, hyena_w3, hyena_skip, sconv_w, attn_q_norm, attn_k_norm, w_branch, w_out, ln1_g, ln1_b, ln2_g, ln2_b, router_w, router_bias, exp_w1, exp_w3, exp_w2):
    nb, l, d = x.shape
    lc = ctx.shape[1]
    depth = ada_w.shape[0]
    assert d == D_MODEL and lc % TT == 0 and l % TT == 0 and l % GRID_W == 0 and (nb * (lc + l)) % MOE_TILE == 0
    ta = lc + l
    nct = lc // TT

    mod_rows = -(-(nb + 1) // SUBLANES) * SUBLANES
    cc = jnp.zeros((mod_rows, d), F32).at[:nb].set(c).at[nb].set(c_ctx)
    mod = ada_mod(cc, ada_w, ada_b)

    cos_t, sin_t = _rope_tables(l, lc)
    bd = _block_diag_ones()
    rwt = router_w.T

    ysb = None
    for li in range(depth):
        mod3 = mod[li].reshape(mod_rows, 1, N_MOD * d)
        wl = w_in[li].astype(BF16)
        if li == 0:
            x_all, h3 = lnmod(ctx, x, mod3, nct, 0, 1)
        h = h3.reshape(nb * ta, d)
        p_rwkv = matmul(h, wl[:, :OFF_HYENA], BF16).reshape(nb, ta, -1)
        p_hs = matmul(h, wl[:, OFF_HYENA:OFF_ATTN], BF16).reshape(nb, ta, -1)
        p_attn = matmul(h, wl[:, OFF_ATTN:OFF_GATE], BF16).reshape(nb, ta, -1)

        r, kk, w0, w1, k0, k1, b0, b1, v, g, bon = rwkv_prep(
            p_rwkv, nct, rwkv_mu[li], rwkv_w0[li], rwkv_w_up[li], rwkv_a0[li], rwkv_a_up[li], rwkv_g_up[li],
            rwkv_k_k[li], rwkv_k_a[li], rwkv_r_k[li], bd)
        yf, yb = rwkv_scan(r, kk, v, w0, w1, k0, k1, b0, b1, lc)
        ya = rwkv_out(yf, yb, g, bon, rwkv_lnx_g[li], rwkv_lnx_b[li], bd)

        q, kx, vx = attn_prep(p_attn, cos_t, sin_t, attn_q_norm[li], attn_k_norm[li], bd)
        yd = attention(q, kx, vx, lc)

        x0, u, ycv = hs_pre(p_hs, nct, hyena_conv[li], sconv_w[li])
        fargs = (hyena_w1[li], hyena_b1[li], hyena_freq1[li], hyena_w2[li], hyena_b2[li], hyena_freq2[li],
                 hyena_w3[li])
        yconv_ctx = hyena_conv_seg(u[:, :lc], fargs) if li < depth - 1 else jnp.zeros((nb, lc, BRANCH_W), F32)
        yconv = jnp.concatenate([yconv_ctx, hyena_conv_seg(u[:, lc:], fargs)], axis=1)

        x1, hf, gates_t = merge(ya, x0, u, yconv, ycv, yd, h3, x_all, mod3, nct, hyena_skip[li], wl[:, OFF_GATE:],
                                w_branch[li].astype(BF16), w_out[li].astype(BF16), ln1_g[li], ln1_b[li],
                                rwt, router_bias)
        gates = jnp.pad(gates_t.T, ((0, 0), (0, LANES - ROUTE_ROWS)))
        hs, gs, pt, meta = moe_sort(hf.reshape(nb * ta, d), gates_t, gates)
        ysb = moe_group(hs, gs, meta[:, 0, :META_LANES].reshape(-1), exp_w1[li].astype(BF16),
                        exp_w3[li].astype(BF16), exp_w2[li].astype(BF16), carried=ysb)
        if li == depth - 1:
            return ln2(x1, pt, ysb, mod3, nct, ln2_g[li], ln2_b[li], None)
        x_all, h3 = ln2(x1, pt, ysb, mod3, nct, ln2_g[li], ln2_b[li], mod[li + 1].reshape(mod_rows, 1, N_MOD * d))


def hyena_conv_seg(u_seg, fargs):
    ktab = hyena_filter_table(u_seg.shape[1], *fargs)
    return hyena_conv(u_seg, ktab)
```

```python
import functools
import math

import numpy as np
import jax
import jax.numpy as jnp
from jax import lax
from jax.experimental import pallas as pl
from jax.experimental.pallas import tpu as pltpu

F32 = jnp.float32
BF16 = jnp.bfloat16
HI = lax.Precision.HIGHEST

D_MODEL = 1024
GRID_W = 64
BRANCH_W = 256
HEAD_DIM = 64
N_BRANCHES = 4
N_MOD = 6
RWKV_COLS = 1024
RWKV_GN_EPS = 64e-5
HYENA_COLS = 768
HYENA_EMB = 33
HYENA_FAST_DECAY = 0.3
HYENA_SLOW_DECAY = 1.5
HYENA_TARGET = 1e-2
SCONV_COLS = 768
Q_W = 256
KV_W = 128
ATTN_COLS = 512
ROPE_THETA = 10000.0
RMS_EPS = 1e-6
OFF_HYENA = RWKV_COLS
OFF_SCONV = OFF_HYENA + HYENA_COLS
OFF_ATTN = OFF_SCONV + SCONV_COLS
OFF_GATE = OFF_ATTN + ATTN_COLS
N_EXPERTS = 16
N_GROUPS = 4
EXPERTS_PER_GROUP = 4
LN_EPS = 1e-6

SUBLANES = 8
LANES = 128
TT = 256
SCAN_BLK = LANES
HY_BLK = 256
VMEM_LIMIT = 56 * 1024 * 1024


def _cparams(sem, vmem=None):
    return pltpu.CompilerParams(dimension_semantics=sem, vmem_limit_bytes=vmem)


def _ln(xf):
    mu = jnp.mean(xf, -1, keepdims=True)
    xc = xf - mu
    var = jnp.mean(xc * xc, -1, keepdims=True)
    return xc * lax.rsqrt(var + LN_EPS)


def _head_sums(x, ones_bd):
    ones16 = ones_bd.astype(BF16)
    hi = x.astype(BF16)
    lo = (x - hi.astype(F32)).astype(BF16)
    return jnp.dot(hi, ones16, preferred_element_type=F32) + jnp.dot(lo, ones16, preferred_element_type=F32)


def _pick_tile(n, cands):
    for c in cands:
        if n % c == 0:
            return c
    raise ValueError(f"no tile for {n}")


def _ada_kernel(c_ref, w_ref, b_ref, o_ref):
    c = c_ref[...]
    a = c * jax.nn.sigmoid(c)
    o_ref[0] = jnp.dot(a, w_ref[0], precision=HI, preferred_element_type=F32) + b_ref[0]


def ada_mod(cc, ada_w, ada_b):
    depth, d, n = ada_w.shape
    rows = cc.shape[0]
    return pl.pallas_call(
        _ada_kernel,
        out_shape=jax.ShapeDtypeStruct((depth, rows, n), F32),
        grid=(depth, n // d),
        in_specs=[pl.BlockSpec((rows, d), lambda l, j: (0, 0)),
                  pl.BlockSpec((1, d, d), lambda l, j: (l, 0, j)),
                  pl.BlockSpec((1, 1, d), lambda l, j: (l, 0, j))],
        out_specs=pl.BlockSpec((1, rows, d), lambda l, j: (l, 0, j)),
        compiler_params=_cparams(("parallel", "parallel"), VMEM_LIMIT),
        name="ada_mod",
    )(cc, ada_w, ada_b.reshape(depth, 1, n))


def _lnmod_kernel(nct, c_ref, x_ref, sh_ref, sc_ref, xa_o, h_o):
    t = pl.program_id(1)
    xin = jnp.where(t < nct, c_ref[0], x_ref[0])
    xa_o[0] = xin
    h_o[0] = (_ln(xin) * (1.0 + sc_ref[0]) + sh_ref[0]).astype(BF16)


def _mod_spec(col, nct, nb):
    return pl.BlockSpec((1, 1, D_MODEL), lambda b, t: (jnp.where(t < nct, nb, b), 0, col))


def lnmod(ctx, x, mod3, nct, col_shift, col_scale):
    nb, l, d = x.shape
    ta = ctx.shape[1] + l
    wide = pl.BlockSpec((1, TT, d), lambda b, t: (b, t, 0))
    return pl.pallas_call(
        functools.partial(_lnmod_kernel, nct),
        out_shape=[jax.ShapeDtypeStruct((nb, ta, d), F32), jax.ShapeDtypeStruct((nb, ta, d), BF16)],
        grid=(nb, ta // TT),
        in_specs=[pl.BlockSpec((1, TT, d), lambda b, t: (b, jnp.minimum(t, nct - 1), 0)),
                  pl.BlockSpec((1, TT, d), lambda b, t: (b, jnp.maximum(t - nct, 0), 0)),
                  _mod_spec(col_shift, nct, nb), _mod_spec(col_scale, nct, nb)],
        out_specs=[wide, wide],
        compiler_params=_cparams(("parallel", "arbitrary")),
        name="lnmod",
    )(ctx, x, mod3, mod3)


def _mm_kernel(a_ref, b_ref, o_ref):
    o_ref[...] = jnp.dot(a_ref[...], b_ref[...], preferred_element_type=F32).astype(o_ref.dtype)


def matmul(a, b, out_dtype=F32):
    m, k = a.shape
    _, n = b.shape
    tm = _pick_tile(m, (1024, 512, 256))
    tn = _pick_tile(n, (1024, 512, 256))
    return pl.pallas_call(
        _mm_kernel,
        out_shape=jax.ShapeDtypeStruct((m, n), out_dtype),
        grid=(m // tm, n // tn),
        in_specs=[pl.BlockSpec((tm, k), lambda i, j: (i, 0)),
                  pl.BlockSpec((k, tn), lambda i, j: (0, j))],
        out_specs=pl.BlockSpec((tm, tn), lambda i, j: (i, j)),
        compiler_params=_cparams(("parallel", "parallel"), VMEM_LIMIT),
        name="matmul",
    )(a, b)


def _halo_specs(width, ta, halo=SUBLANES):
    nblk = ta // halo
    per = TT // halo
    cur = pl.BlockSpec((1, TT, width), lambda b, t: (b, t, 0))
    prev = pl.BlockSpec((1, halo, width), lambda b, t: (b, jnp.maximum(t * per - 1, 0), 0))
    nxt = pl.BlockSpec((1, halo, width), lambda b, t: (b, jnp.minimum((t + 1) * per, nblk - 1), 0))
    return [cur, prev, nxt]


def _neighbours(cur, prev_ref, next_ref, nct, nt):
    t = pl.program_id(1)
    seg_start = jnp.logical_or(t == 0, t == nct)
    seg_end = jnp.logical_or(t == nct - 1, t == nt - 1)
    halo = prev_ref.shape[1]
    prev_row = prev_ref[0][halo - 1:halo, :].astype(F32) * jnp.where(seg_start, 0.0, 1.0)
    next_row = next_ref[0][0:1, :].astype(F32) * jnp.where(seg_end, 0.0, 1.0)
    row = lax.broadcasted_iota(jnp.int32, (TT, 1), 0)
    xm1 = jnp.where(row == 0, prev_row, pltpu.roll(cur, 1, axis=0))
    xp1 = jnp.where(row == TT - 1, next_row, pltpu.roll(cur, TT - 1, axis=0))
    return xm1, xp1


def _rwkv_prep_kernel(nct, nt, cur_ref, prev_ref, next_ref, mu_ref, w0_ref, wup_ref, a0_ref, aup_ref,
                      gup_ref, kk_ref, ka_ref, rk_ref, bd_ref,
                      r_o, kk_o, w0_o, w1_o, k0_o, k1_o, b0_o, b1_o, v_o, g_o, bon_o):
    cur = cur_ref[0].astype(F32)
    xm1, xp1 = _neighbours(cur, prev_ref, next_ref, nct, nt)
    p = cur + mu_ref[...] * (0.5 * (xm1 + xp1) - cur)
    c = BRANCH_W
    r, k, v = p[:, 0:c], p[:, c:2 * c], p[:, 2 * c:3 * c]
    wd = p[:, 3 * c:3 * c + 64]
    ad = p[:, 3 * c + 64:3 * c + 128]
    gd = p[:, 3 * c + 128:3 * c + 256]
    bd = bd_ref[...]
    kk = k * kk_ref[...]
    ss = _head_sums(kk * kk, bd)
    kkn = kk * lax.rsqrt(jnp.maximum(ss, 1e-24))
    twd = jnp.tanh(wd)
    ka = ka_ref[...]
    kdirs = []
    w_outs, k_outs, b_outs = (w0_o, w1_o), (k0_o, k1_o), (b0_o, b1_o)
    for d in range(2):
        wlog = w0_ref[d:d + 1, :] + jnp.dot(twd, wup_ref[d], precision=HI, preferred_element_type=F32)
        decay = -math.exp(-0.5) * jax.nn.sigmoid(wlog)
        a = jax.nn.sigmoid(a0_ref[d:d + 1, :] + jnp.dot(ad, aup_ref[d], precision=HI, preferred_element_type=F32))
        kdir = k * (1.0 + (a - 1.0) * ka)
        bdir = kkn * a
        kdirs.append(kdir)
        w_outs[d][0] = decay
        k_outs[d][0] = kdir.astype(BF16)
        b_outs[d][0] = bdir.astype(BF16)
    r_o[0] = r.astype(BF16)
    kk_o[0] = kkn.astype(BF16)
    v_o[0] = v.astype(BF16)
    g_o[0] = jnp.dot(jax.nn.sigmoid(gd), gup_ref[...], precision=HI, preferred_element_type=F32)
    rkk = r * rk_ref[...] * (kdirs[0] + kdirs[1])
    bon_o[0] = _head_sums(rkk, bd) * v


def rwkv_prep(p_rwkv, nct, mu, w0, w_up, a0, a_up, g_up, k_k, k_a, r_k, bd):
    nb, ta, _ = p_rwkv.shape
    nt = ta // TT
    c = BRANCH_W
    nat = jax.ShapeDtypeStruct((nb, ta, c), F32)
    lo = jax.ShapeDtypeStruct((nb, ta, c), BF16)
    nat_spec = pl.BlockSpec((1, TT, c), lambda b, t: (b, t, 0))

    def full(a):
        nd = a.ndim
        return pl.BlockSpec(a.shape, lambda b, t: (0,) * nd)

    consts = [mu.reshape(1, -1), w0, w_up, a0, a_up, g_up, k_k.reshape(1, -1), k_a.reshape(1, -1),
              r_k.reshape(1, -1), bd]
    return pl.pallas_call(
        functools.partial(_rwkv_prep_kernel, nct, nt),
        out_shape=[lo, lo, nat, nat, lo, lo, lo, lo, lo, nat, nat],
        grid=(nb, nt),
        in_specs=_halo_specs(RWKV_COLS, ta, 2 * SUBLANES) + [full(a) for a in consts],
        out_specs=[nat_spec] * 11,
        compiler_params=_cparams(("parallel", "parallel"), VMEM_LIMIT),
        name="rwkv_prep",
    )(p_rwkv, p_rwkv, p_rwkv, *consts)


CHUNK = 64
PREP_NB = 4


def _chunk_scan_rows(x, reverse):
    pos = lax.broadcasted_iota(jnp.int32, x.shape, 0) % CHUNK
    step = 1
    while step < CHUNK:
        if reverse:
            x = x + jnp.where(pos < CHUNK - step, pltpu.roll(x, x.shape[0] - step, axis=0), 0.0)
        else:
            x = x + jnp.where(pos >= step, pltpu.roll(x, step, axis=0), 0.0)
        step *= 2
    return x


def _chunk_prep_kernel(r_ref, kk_ref, v_ref, lw0, k0, b0, lw1, k1, b1,
                       a0_o, bm0_o, rp0_o, y00_o, a1_o, bm1_o, rp1_o, y01_o):
    blk = SCAN_BLK
    npair = BRANCH_W // LANES
    nchunk = blk // CHUNK
    ti = lax.broadcasted_iota(jnp.int32, (blk, blk), 0)
    si = lax.broadcasted_iota(jnp.int32, (blk, blk), 1)
    same = (ti // CHUNK) == (si // CHUNK)
    eye = ti == si
    bd64 = (ti // HEAD_DIM) == (si // HEAD_DIM)
    head0 = si < HEAD_DIM
    lane_half = lax.broadcasted_iota(jnp.int32, (HEAD_DIM, LANES), 1)
    eyef = jnp.where(eye, 1.0, 0.0)
    dot = lambda x, y: jnp.dot(x, y, preferred_element_type=F32)
    lo = lambda x: x.astype(BF16)
    split = lambda x: jnp.concatenate([jnp.where(head0, x, 0.0), jnp.where(head0, 0.0, x)], axis=0)
    cat = lambda ms: lo(jnp.concatenate(ms, axis=1))

    probs = []
    for bi, d in [(bi, d) for bi in range(PREP_NB) for d in range(2)]:
        lw_ref, k_ref, b_ref = ((lw0, k0, b0), (lw1, k1, b1))[d]
        reverse = d == 1
        r, kk, v = (a[bi].astype(F32) for a in (r_ref, kk_ref, v_ref))
        lw, k, b = lw_ref[bi], k_ref[bi].astype(F32), b_ref[bi].astype(F32)
        lg = _chunk_scan_rows(lw, reverse)
        lg_end = lg + _chunk_scan_rows(lw, not reverse) - lw
        g, gi, g_end = jnp.exp(lg), jnp.exp(-lg), jnp.exp(lg_end)
        to_end = jnp.exp(lg_end - lg)
        arrs = (kk * jnp.exp(lg - lw), b * gi, k * gi, r * g, v, k * to_end, b * to_end, g_end)
        incl = jnp.logical_and(same, si >= ti if reverse else si <= ti)
        strict = jnp.logical_and(same, si > ti if reverse else si < ti)
        for p in range(npair):
            lanes = slice(p * LANES, (p + 1) * LANES)
            probs.append(dict(bi=bi, d=d, p=p, incl=incl, strict=strict, arrs=tuple(a[:, lanes] for a in arrs)))

    for q in probs:
        pp_, q_, kt_, rt_ = q["arrs"][:4]
        rhs_g = lo(jnp.concatenate([q_, kt_], axis=0))
        lm, mm, n2, nn = [], [], [], []
        for hh in range(2):
            hm = head0 if hh == 0 else jnp.logical_not(head0)
            lhs_g = lo(jnp.concatenate([jnp.where(hm, pp_, 0.0), jnp.where(hm, rt_, 0.0)], axis=0))
            gm = lax.dot_general(lhs_g, rhs_g, (((1,), (1,)), ((), ())), preferred_element_type=F32)
            lm.append(jnp.where(q["strict"], gm[:blk, :blk], 0.0))
            mm.append(jnp.where(q["strict"], gm[:blk, blk:], 0.0))
            n2.append(jnp.where(q["incl"], gm[blk:, :blk], 0.0))
            nn.append(jnp.where(q["incl"], gm[blk:, blk:], 0.0))
        q["pw"], q["tm"] = lm, [eyef - lm[0], eyef - lm[1]]
        q["m_cat"], q["n2_cat"], q["nn_cat"] = cat(mm), cat(n2), cat(nn)
    for _ in range(CHUNK.bit_length() - 2):
        for q in probs:
            pwl = [lo(x) for x in q["pw"]]
            q["pw"] = [dot(x, x) for x in pwl]
        for q in probs:
            q["tm"] = [dot(lo(t), lo(eyef + x)) for t, x in zip(q["tm"], q["pw"])]
    for q in probs:
        q["t_cat"] = cat(q["tm"])
        q["v_st"] = lo(split(q["arrs"][4]))
        q["pp"] = dot(q["t_cat"], lo(split(q["arrs"][0])))
        q["mv"] = dot(q["m_cat"], q["v_st"])
    for q in probs:
        q["w2"] = dot(q["t_cat"], lo(split(q["mv"])))
        q["rp"] = q["arrs"][3] - dot(q["n2_cat"], lo(split(q["pp"])))
    for q in probs:
        q["y0"] = dot(q["nn_cat"], q["v_st"]) - dot(q["n2_cat"], lo(split(q["w2"])))
    in_chunk = [si // CHUNK == c for c in range(nchunk)]
    for q in probs:
        ppt, vt, w2t = q["pp"].T, q["arrs"][4].T, q["w2"].T
        kg_, qg_ = q["arrs"][5], q["arrs"][6]
        lhs_a = jnp.concatenate([jnp.where(cm, ppt, 0.0) for cm in in_chunk], axis=0)
        q["pq"] = dot(lo(lhs_a), lo(qg_))
        lhs_b = jnp.concatenate([jnp.concatenate([jnp.where(cm, vt, 0.0), jnp.where(cm, -w2t, 0.0)], axis=1)
                                 for cm in in_chunk], axis=0)
        q["bf"] = dot(lo(lhs_b), lo(jnp.concatenate([kg_, qg_], axis=0)))
    outs = ((a0_o, bm0_o, rp0_o, y00_o), (a1_o, bm1_o, rp1_o, y01_o))
    for q in probs:
        a_o, bm_o = outs[q["d"]][:2]
        g_end_p = q["arrs"][7]
        for c in range(nchunk):
            pq_c = q["pq"][c * blk:(c + 1) * blk]
            a_o[q["bi"], 0, c, q["p"]] = (jnp.where(eye, g_end_p[c * CHUNK:c * CHUNK + 1], 0.0)
                                          - jnp.where(bd64, pq_c, 0.0)).astype(BF16)
            bm_o[q["bi"], 0, c, q["p"]] = jnp.where(lane_half < HEAD_DIM, q["bf"][c * blk:c * blk + HEAD_DIM],
                                                    q["bf"][c * blk + HEAD_DIM:(c + 1) * blk]).astype(BF16)
    for bi in range(PREP_NB):
        for d in range(2):
            rp_o, y0_o = outs[d][2:]
            rp_o[bi] = jnp.concatenate([q["rp"] for q in probs if q["d"] == d and q["bi"] == bi], axis=1)
            y0_o[bi] = jnp.concatenate([q["y0"] for q in probs if q["d"] == d and q["bi"] == bi], axis=1)


def chunk_prep(r, kk, v, lw0, lw1, k0, k1, b0, b1):
    nb, ta, c = r.shape
    nblk = ta // SCAN_BLK
    npair = c // LANES
    nchunk = SCAN_BLK // CHUNK
    pb = PREP_NB
    assert nb % pb == 0
    nat = pl.BlockSpec((pb, SCAN_BLK, c), lambda b, s: (b, s, 0))
    a_shape = jax.ShapeDtypeStruct((nb, nblk, nchunk, npair, LANES, LANES), BF16)
    bm_shape = jax.ShapeDtypeStruct((nb, nblk, nchunk, npair, HEAD_DIM, LANES), BF16)
    nat_shape = jax.ShapeDtypeStruct((nb, ta, c), F32)
    a_spec = pl.BlockSpec((pb, 1, nchunk, npair, LANES, LANES), lambda b, s: (b, s, 0, 0, 0, 0))
    bm_spec = pl.BlockSpec((pb, 1, nchunk, npair, HEAD_DIM, LANES), lambda b, s: (b, s, 0, 0, 0, 0))
    return pl.pallas_call(
        _chunk_prep_kernel,
        out_shape=[a_shape, bm_shape, nat_shape, nat_shape] * 2,
        grid=(nb // pb, nblk),
        in_specs=[nat] * 9,
        out_specs=[a_spec, bm_spec, nat, nat] * 2,
        compiler_params=_cparams(("parallel", "parallel"), VMEM_LIMIT),
        name="chunk_prep",
    )(r, kk, v, lw0, k0, b0, lw1, k1, b1)


def _chunk_scan_kernel(nb, a0, bm0, rp0, y00, a1, bm1, rp1, y01, yf_o, yb_o, s_scr):
    step = pl.program_id(0)
    npair = BRANCH_W // LANES
    nchunk = SCAN_BLK // CHUNK

    @pl.when(step == 0)
    def _():
        s_scr[...] = jnp.zeros_like(s_scr)

    lane = lax.broadcasted_iota(jnp.int32, (CHUNK, LANES), 1)
    refs = ((a0, bm0, rp0, y00, yf_o), (a1, bm1, rp1, y01, yb_o))
    for ci in range(nchunk):
        for d in range(2):
            a_ref, bm_ref, rp_ref, y0_ref, y_ref = refs[d]
            c = ci if d == 0 else nchunk - 1 - ci
            rows = slice(c * CHUNK, (c + 1) * CHUNK)
            for b in range(nb):
                for p in range(npair):
                    lanes = slice(p * LANES, (p + 1) * LANES)
                    s = s_scr[d, b, p]
                    rpc = rp_ref[b, rows, lanes]
                    lhs = jnp.concatenate([jnp.where(lane < HEAD_DIM, rpc, 0.0), jnp.where(lane >= HEAD_DIM, rpc, 0.0)],
                                          axis=0)
                    yh = lax.dot_general(lhs, s, (((1,), (1,)), ((), ())), preferred_element_type=F32)
                    y_ref[b, rows, lanes] = jnp.concatenate([yh[:CHUNK], yh[CHUNK:]], axis=1) + y0_ref[b, rows, lanes]
                    s_scr[d, b, p] = (jnp.dot(s.astype(BF16), a_ref[b, 0, c, p], preferred_element_type=F32)
                                      + bm_ref[b, 0, c, p])


def rwkv_scan(r, kk, v, lw0, lw1, k0, k1, b0, b1, lc):
    nb, ta, c = r.shape
    nblk = ta // SCAN_BLK
    nctb = lc // SCAN_BLK
    npair = c // LANES
    nchunk = SCAN_BLK // CHUNK
    a0, bm0, rp0, y00, a1, bm1, rp1, y01 = chunk_prep(r, kk, v, lw0, lw1, k0, k1, b0, b1)

    def fwd(s):
        return s

    def bwd(s):
        return jnp.where(s < nctb, nctb - 1 - s, nblk - 1 - (s - nctb))

    def specs(idx):
        return [pl.BlockSpec((nb, 1, nchunk, npair, LANES, LANES), lambda s: (0, idx(s), 0, 0, 0, 0)),
                pl.BlockSpec((nb, 1, nchunk, npair, HEAD_DIM, LANES), lambda s: (0, idx(s), 0, 0, 0, 0)),
                pl.BlockSpec((nb, SCAN_BLK, c), lambda s: (0, idx(s), 0)),
                pl.BlockSpec((nb, SCAN_BLK, c), lambda s: (0, idx(s), 0))]

    out = jax.ShapeDtypeStruct((nb, ta, c), F32)
    return pl.pallas_call(
        functools.partial(_chunk_scan_kernel, nb),
        out_shape=[out, out],
        grid=(nblk,),
        in_specs=specs(fwd) + specs(bwd),
        out_specs=[pl.BlockSpec((nb, SCAN_BLK, c), lambda s: (0, fwd(s), 0)),
                   pl.BlockSpec((nb, SCAN_BLK, c), lambda s: (0, bwd(s), 0))],
        scratch_shapes=[pltpu.VMEM((2, nb, npair, HEAD_DIM, LANES), F32)],
        compiler_params=_cparams(("arbitrary",), VMEM_LIMIT),
        name="chunk_scan",
    )(a0, bm0, rp0, y00, a1, bm1, rp1, y01)


def _rwkv_out_kernel(yf_ref, yb_ref, g_ref, bon_ref, lg_ref, lb_ref, bd_ref, o_ref):
    y = yf_ref[0] + yb_ref[0]
    bd = bd_ref[...]
    mu = _head_sums(y, bd) * (1.0 / HEAD_DIM)
    yc = y - mu
    var = _head_sums(yc * yc, bd) * (1.0 / HEAD_DIM)
    yn = yc * lax.rsqrt(var + RWKV_GN_EPS) * lg_ref[...] + lb_ref[...]
    o_ref[0] = ((yn + bon_ref[0]) * g_ref[0]).astype(BF16)


def rwkv_out(yf, yb, g, bon, lnx_g, lnx_b, bd):
    nb, ta, _ = yf.shape
    nat_spec = pl.BlockSpec((1, TT, BRANCH_W), lambda b, t: (b, t, 0))
    row = pl.BlockSpec((1, BRANCH_W), lambda b, t: (0, 0))
    return pl.pallas_call(
        _rwkv_out_kernel,
        out_shape=jax.ShapeDtypeStruct((nb, ta, BRANCH_W), BF16),
        grid=(nb, ta // TT),
        in_specs=[nat_spec] * 4 + [row, row, pl.BlockSpec(bd.shape, lambda b, t: (0, 0))],
        out_specs=nat_spec,
        compiler_params=_cparams(("parallel", "parallel")),
        name="rwkv_out",
    )(yf, yb, g, bon, lnx_g.reshape(1, -1), lnx_b.reshape(1, -1), bd)


def _pair_swap(x):
    lane = lax.broadcasted_iota(jnp.int32, x.shape, 1)
    n = x.shape[1]
    return jnp.where(lane % 2 == 0, pltpu.roll(x, n - 1, axis=1), pltpu.roll(x, 1, axis=1))


def _attn_prep_kernel(p_ref, cos_ref, sin_ref, qg_ref, kg_ref, bd_ref, q_o, k_o, v_o):
    p = p_ref[0].astype(F32)
    q, k, v = p[:, :Q_W], p[:, Q_W:Q_W + KV_W], p[:, Q_W + KV_W:]
    bd = bd_ref[...]
    cos, sin = cos_ref[...], sin_ref[...]
    qms = _head_sums(q * q, bd) * (1.0 / HEAD_DIM)
    qn = q * lax.rsqrt(qms + RMS_EPS) * qg_ref[...]
    qr = qn * cos + _pair_swap(qn) * sin
    q_o[0] = (qr * HEAD_DIM ** -0.5).astype(BF16)
    kms = _head_sums(k * k, bd[:KV_W, :KV_W]) * (1.0 / HEAD_DIM)
    kn = k * lax.rsqrt(kms + RMS_EPS) * kg_ref[...]
    kr = kn * cos[:, :KV_W] + _pair_swap(kn) * sin[:, :KV_W]
    for g in range(KV_W // HEAD_DIM):
        sl = slice(g * HEAD_DIM, (g + 1) * HEAD_DIM)
        k_o[0, g] = kr[:, sl].astype(BF16)
        v_o[0, g] = v[:, sl].astype(BF16)


def attn_prep(p_attn, cos_t, sin_t, q_norm, k_norm, bd):
    nb, ta, _ = p_attn.shape
    ng = KV_W // HEAD_DIM
    qg = jnp.tile(q_norm, Q_W // HEAD_DIM).reshape(1, -1)
    kg = jnp.tile(k_norm, ng).reshape(1, -1)
    kv_shape = jax.ShapeDtypeStruct((nb, ng, ta, HEAD_DIM), BF16)
    kv_spec = pl.BlockSpec((1, ng, TT, HEAD_DIM), lambda b, t: (b, 0, t, 0))
    return pl.pallas_call(
        _attn_prep_kernel,
        out_shape=[jax.ShapeDtypeStruct((nb, ta, Q_W), BF16), kv_shape, kv_shape],
        grid=(nb, ta // TT),
        in_specs=[pl.BlockSpec((1, TT, ATTN_COLS), lambda b, t: (b, t, 0)),
                  pl.BlockSpec((TT, Q_W), lambda b, t: (t, 0)),
                  pl.BlockSpec((TT, Q_W), lambda b, t: (t, 0)),
                  pl.BlockSpec((1, Q_W), lambda b, t: (0, 0)),
                  pl.BlockSpec((1, KV_W), lambda b, t: (0, 0)),
                  pl.BlockSpec(bd.shape, lambda b, t: (0, 0))],
        out_specs=[pl.BlockSpec((1, TT, Q_W), lambda b, t: (b, t, 0)), kv_spec, kv_spec],
        compiler_params=_cparams(("parallel", "parallel")),
        name="attn_prep",
    )(p_attn, cos_t, sin_t, qg, kg, bd)


def _attn_kernel(nct, lc, q_ref, k_ref, v_ref, o_ref):
    t = pl.program_id(1)
    ng = k_ref.shape[1]
    rep = Q_W // HEAD_DIM // ng

    def run(nk):
        outs = []
        for g in range(ng):
            kk, vv = k_ref[0, g, :nk, :], v_ref[0, g, :nk, :]
            for r in range(rep):
                h = g * rep + r
                q = q_ref[0][:, h * HEAD_DIM:(h + 1) * HEAD_DIM]
                s = lax.dot_general(q, kk, (((1,), (1,)), ((), ())), preferred_element_type=F32)
                p = jnp.exp(s - jnp.max(s, -1, keepdims=True))
                l = jnp.sum(p, -1, keepdims=True)
                o = jnp.dot(p.astype(BF16), vv, preferred_element_type=F32)
                outs.append(o / l)
        o_ref[0] = jnp.concatenate(outs, axis=1).astype(BF16)

    @pl.when(t < nct)
    def _():
        run(lc)

    @pl.when(t >= nct)
    def _():
        run(k_ref.shape[2])


def attention(q, k, v, lc):
    nb, ta, _ = q.shape
    ng = k.shape[1]
    nct = lc // TT
    qo_spec = pl.BlockSpec((1, TT, Q_W), lambda b, t: (b, t, 0))
    kv_spec = pl.BlockSpec((1, ng, ta, HEAD_DIM), lambda b, t: (b, 0, 0, 0))
    return pl.pallas_call(
        functools.partial(_attn_kernel, nct, lc),
        out_shape=jax.ShapeDtypeStruct((nb, ta, Q_W), BF16),
        grid=(nb, ta // TT),
        in_specs=[qo_spec, kv_spec, kv_spec],
        out_specs=qo_spec,
        compiler_params=_cparams(("parallel", "arbitrary"), VMEM_LIMIT),
        name="attention",
    )(q, k, v)


def _hs_pre_kernel(nct, nt, cur_ref, prev_ref, next_ref, hw_ref, sw_ref, x0_o, u_o, ycv_o):
    cur = cur_ref[0].astype(F32)
    xm1, xp1 = _neighbours(cur, prev_ref, next_ref, nct, nt)
    c = BRANCH_W
    hc = HYENA_COLS
    hw = hw_ref[...]
    ph = hw[0:1] * xm1[:, :hc] + hw[1:2] * cur[:, :hc] + hw[2:3] * xp1[:, :hc]
    x0_o[0] = ph[:, :c].astype(BF16)
    u_o[0] = (ph[:, c:2 * c] * ph[:, 2 * c:3 * c]).astype(BF16)
    sw = sw_ref[...]

    def cx(a):
        return a[:, hc + c:hc + 2 * c] * a[:, hc + 2 * c:hc + 3 * c]

    conv = sw[0:1] * cx(xm1) + sw[1:2] * cx(cur) + sw[2:3] * cx(xp1)
    ycv_o[0] = (cur[:, hc:hc + c] * conv).astype(BF16)


def hs_pre(p_hs, nct, hyena_conv, sconv_w):
    nb, ta, w = p_hs.shape
    nt = ta // TT
    nat = pl.BlockSpec((1, TT, BRANCH_W), lambda b, t: (b, t, 0))
    return pl.pallas_call(
        functools.partial(_hs_pre_kernel, nct, nt),
        out_shape=[jax.ShapeDtypeStruct((nb, ta, BRANCH_W), BF16)] * 3,
        grid=(nb, nt),
        in_specs=_halo_specs(w, ta, 2 * SUBLANES) + [pl.BlockSpec(hyena_conv.shape, lambda b, t: (0, 0)),
                                       pl.BlockSpec(sconv_w.shape, lambda b, t: (0, 0))],
        out_specs=[nat, nat, nat],
        compiler_params=_cparams(("parallel", "parallel"), VMEM_LIMIT),
        name="hs_pre",
    )(p_hs, p_hs, p_hs, hyena_conv, sconv_w)


EMB_PAD = 40


def _filter_tables(lh):
    n = np.arange(2 * lh)
    pos = np.abs(n - (lh - 1)).astype(np.float64)
    bands = (HYENA_EMB - 1) // 2
    t = np.minimum(pos, lh - 1) / (lh - 1)
    wpos = 2.0 * math.pi * pos / lh
    f = np.linspace(1e-4, bands - 1, bands)[:, None]
    z = np.zeros((EMB_PAD, 2 * lh), np.float32)
    z[0] = t
    z[1:1 + bands] = np.cos(f * wpos[None, :])
    z[1 + bands:1 + 2 * bands] = -np.sin(f * wpos[None, :])
    max_decay = math.log(HYENA_TARGET) / HYENA_FAST_DECAY
    min_decay = math.log(HYENA_TARGET) / HYENA_SLOW_DECAY
    deltas = np.abs(np.linspace(min_decay, max_decay, BRANCH_W)).astype(np.float32)
    return z, deltas.reshape(-1, 1)


def _filter_kernel(lh, tn, z_ref, w1_ref, b1_ref, f1_ref, w2_ref, b2_ref, f2_ref, w3_ref, dl_ref, o_ref):
    z = z_ref[...]
    h1 = jnp.sin(f1_ref[...] * (jnp.dot(w1_ref[...], z, precision=HI, preferred_element_type=F32) + b1_ref[...]))
    h2 = jnp.sin(f2_ref[...] * (jnp.dot(w2_ref[...], h1, precision=HI, preferred_element_type=F32) + b2_ref[...]))
    f = jnp.dot(w3_ref[...], h2, precision=HI, preferred_element_type=F32)
    n = pl.program_id(0) * tn + lax.broadcasted_iota(jnp.int32, (1, tn), 1)
    filt = jnp.where(n >= lh - 1, f[:BRANCH_W], f[BRANCH_W:])
    win = jnp.exp(-z[0:1, :] * dl_ref[...])
    o_ref[...] = jnp.where(n == 2 * lh - 1, 0.0, filt * win)


def hyena_filter_table(lh, w1, b1, f1, w2, b2, f2, w3):
    z_np, dl_np = _filter_tables(lh)
    n2 = 2 * lh
    tn = _pick_tile(n2, (1024, 512))
    hd = w2.shape[0]
    w1t = jnp.zeros((hd, EMB_PAD), F32).at[:, :HYENA_EMB].set(w1.T)
    args = [jnp.asarray(z_np), w1t, b1.reshape(-1, 1), f1.reshape(-1, 1), w2.T, b2.reshape(-1, 1),
            f2.reshape(-1, 1), w3.T, jnp.asarray(dl_np)]

    def full(a):
        return pl.BlockSpec(a.shape, lambda j: (0, 0))

    return pl.pallas_call(
        functools.partial(_filter_kernel, lh, tn),
        out_shape=jax.ShapeDtypeStruct((BRANCH_W, n2), F32),
        grid=(n2 // tn,),
        in_specs=[pl.BlockSpec((EMB_PAD, tn), lambda j: (0, j))] + [full(a) for a in args[1:]],
        out_specs=pl.BlockSpec((BRANCH_W, tn), lambda j: (0, j)),
        compiler_params=_cparams(("parallel",)),
        name="hyena_filter",
    )(*args)


def _hyena_conv_kernel(nblk, bp, nch, k_ref, u_ref, o_ref, t_scr):
    ntile = 4 * nblk - 1
    mc = 2 * nblk - 1
    width = (ntile + 1) * LANES
    for ch in range(nch):
        big = pltpu.roll(jnp.broadcast_to(k_ref[ch], (LANES, width)), width - (LANES - 1), 1, stride=1, stride_axis=0)
        for m in range(ntile):
            t_scr[ch, m] = big[:, m * LANES:(m + 1) * LANES].astype(BF16)

    for ch in range(nch):
        for d in [0] + [s * a for a in range(1, nblk) for s in (1, -1)]:
            m0 = 2 * d + mc
            w = jnp.concatenate([jnp.concatenate([t_scr[ch, m0], t_scr[ch, m0 + 1]], axis=1),
                                 jnp.concatenate([t_scr[ch, m0 - 1], t_scr[ch, m0]], axis=1)], axis=0)
            i0, i1 = max(0, d), min(nblk, nblk + d)
            lhs = u_ref[ch, (i0 - d) * bp:(i1 - d) * bp, :].astype(BF16)
            res = jnp.dot(lhs, w, preferred_element_type=F32)
            if d == 0:
                o_ref[ch] = res
            else:
                o_ref[ch, i0 * bp:i1 * bp, :] += res


def hyena_conv(u, ktab):
    nb, l, c = u.shape
    nblk = l // HY_BLK
    bp = -(-nb // SUBLANES) * SUBLANES
    nch = SUBLANES if nblk == 1 else 2
    ut = jnp.transpose(u.astype(F32).reshape(nb, nblk, HY_BLK, c), (3, 1, 0, 2))
    if bp != nb:
        ut = jnp.pad(ut, ((0, 0), (0, 0), (0, bp - nb), (0, 0)))
    ut = ut.reshape(c, nblk * bp, HY_BLK)
    k3 = ktab.reshape(c, 1, 4 * nblk * LANES)
    out = pl.pallas_call(
        functools.partial(_hyena_conv_kernel, nblk, bp, nch),
        out_shape=jax.ShapeDtypeStruct((c, nblk * bp, HY_BLK), F32),
        grid=(c // nch,),
        in_specs=[pl.BlockSpec((nch, 1, 4 * nblk * LANES), lambda ch: (ch, 0, 0)),
                  pl.BlockSpec((nch, nblk * bp, HY_BLK), lambda ch: (ch, 0, 0))],
        out_specs=pl.BlockSpec((nch, nblk * bp, HY_BLK), lambda ch: (ch, 0, 0)),
        scratch_shapes=[pltpu.VMEM((nch, 4 * nblk - 1, LANES, LANES), BF16)],
        compiler_params=_cparams(("parallel",)),
        name="hyena_conv",
    )(k3, ut)
    out = out.reshape(c, nblk, bp, HY_BLK)[:, :, :nb]
    return jnp.transpose(out, (2, 1, 3, 0)).reshape(nb, l, c)


def _route(logits, bias):
    s = jax.nn.sigmoid(logits)
    sel = s + bias
    srow = [s[e:e + 1] for e in range(N_EXPERTS)]
    row = [sel[e:e + 1] for e in range(N_EXPERTS)]
    best, gi = None, None
    for g in range(N_GROUPS):
        a, b, c, d = row[4 * g:4 * g + 4]
        hi1, lo1, hi2, lo2 = jnp.maximum(a, b), jnp.minimum(a, b), jnp.maximum(c, d), jnp.minimum(c, d)
        score = jnp.maximum(hi1, hi2) + jnp.maximum(jnp.minimum(hi1, hi2), jnp.maximum(lo1, lo2))
        if g == 0:
            best, gi = score, jnp.zeros(score.shape, jnp.int32)
        else:
            better = score > best
            gi = jnp.where(better, g, gi)
            best = jnp.where(better, score, best)
    neg = -jnp.inf
    msel = [jnp.where(gi == e // EXPERTS_PER_GROUP, row[e], neg) for e in range(N_EXPERTS)]

    def arg_first_max(vals):
        bv, bi = vals[0], jnp.zeros(vals[0].shape, jnp.int32)
        for e in range(1, N_EXPERTS):
            better = vals[e] > bv
            bi = jnp.where(better, e, bi)
            bv = jnp.where(better, vals[e], bv)
        return bi

    i1 = arg_first_max(msel)
    i2 = arg_first_max([jnp.where(i1 == e, neg, msel[e]) for e in range(N_EXPERTS)])
    w1 = sum(jnp.where(i1 == e, srow[e], 0.0) for e in range(N_EXPERTS))
    w2 = sum(jnp.where(i2 == e, srow[e], 0.0) for e in range(N_EXPERTS))
    den = w1 + w2
    g1, g2 = w1 / den, w2 / den
    rows = [jnp.where(i1 == e, g1, 0.0) + jnp.where(i2 == e, g2, 0.0) for e in range(N_EXPERTS)]
    rows.append(gi.astype(F32))
    rows.extend([jnp.zeros_like(g1)] * (ROUTE_ROWS - len(rows)))
    return jnp.concatenate(rows, axis=0)


MERGE_NB = 4


def _merge_kernel(ya_ref, x0_ref, u_ref, yc_ref, ycv_ref, yd_ref, h_ref, x_ref, *rest):
    mods, rest = rest[:3 * MERGE_NB], rest[3 * MERGE_NB:]
    skip_ref, wg_ref, wb_ref, wo_ref, g1_ref, b1_ref, rwt_ref, rb_ref = rest[:8]
    x1_o, hf_o = rest[8:10]
    gates_o = rest[10:]
    rows = MERGE_NB * TT
    flat = lambda ref: ref[...].reshape(rows, ref.shape[-1])
    yb = (flat(x0_ref) * (flat(yc_ref) + flat(u_ref) * skip_ref[...])).astype(BF16)
    ys = (flat(ya_ref), yb, flat(ycv_ref), flat(yd_ref))
    h = flat(h_ref)
    merged = None
    for n in range(N_BRANCHES):
        gate = jax.nn.sigmoid(jnp.dot(h, wg_ref[:, n * D_MODEL:(n + 1) * D_MODEL], preferred_element_type=F32))
        term = gate * jnp.dot(ys[n], wb_ref[n], preferred_element_type=F32)
        merged = term if merged is None else merged + term
    out = jnp.dot(merged.astype(BF16), wo_ref[...], preferred_element_type=F32)
    alpha = (2 * 2) ** 0.25
    for k in range(MERGE_NB):
        ga_ref, shf_ref, scf_ref = mods[3 * k:3 * k + 3]
        x1 = _ln(alpha * x_ref[k] + ga_ref[0] * out[k * TT:(k + 1) * TT]) * g1_ref[...] + b1_ref[...]
        hf = _ln(x1) * (1.0 + scf_ref[0]) + shf_ref[0]
        x1_o[k] = x1
        hf_o[k] = hf.astype(BF16)
        logits = lax.dot_general(rwt_ref[...], hf, (((1,), (1,)), ((), ())), precision=HI,
                                 preferred_element_type=F32)
        gates_o[k][...] = _route(logits, rb_ref[...])


def merge(ya, x0, u, yconv, ycv, yd, h, x_all, mod3, nct, skip, wg, wb, wo, ln_g, ln_b, rwt, rbias):
    nb, ta, d = x_all.shape
    nt = ta // TT
    mb = MERGE_NB
    assert nb % mb == 0
    nat = pl.BlockSpec((mb, TT, BRANCH_W), lambda b, t: (b, t, 0))
    wide = pl.BlockSpec((mb, TT, d), lambda b, t: (b, t, 0))

    def full(a):
        nd = a.ndim
        return pl.BlockSpec(a.shape, lambda b, t: (0,) * nd)

    def mod_spec(col, k):
        return pl.BlockSpec((1, 1, D_MODEL), lambda b, t: (jnp.where(t < nct, nb, b * mb + k), 0, col))

    mod_specs = [mod_spec(col, k) for k in range(mb) for col in (2, 3, 4)]
    consts = [skip.reshape(1, -1), wg, wb, wo, ln_g.reshape(1, -1), ln_b.reshape(1, -1), rwt, rbias.reshape(-1, 1)]
    gate_shape = jax.ShapeDtypeStruct((ROUTE_ROWS, nb // mb * ta), F32)
    gate_spec = pl.BlockSpec((ROUTE_ROWS, TT), lambda b, t: (0, b * nt + t))
    outs = pl.pallas_call(
        _merge_kernel,
        out_shape=[jax.ShapeDtypeStruct((nb, ta, d), F32), jax.ShapeDtypeStruct((nb, ta, d), BF16)] + [gate_shape] * mb,
        grid=(nb // mb, nt),
        in_specs=[nat] * 6 + [wide, wide] + mod_specs + [full(a) for a in consts],
        out_specs=[wide, wide] + [gate_spec] * mb,
        compiler_params=_cparams(("parallel", "parallel"), VMEM_LIMIT),
        name="merge",
    )(ya, x0, u, yconv, ycv, yd, h, x_all, *([mod3] * (3 * mb)), *consts)
    gates_t = jnp.stack([g.reshape(ROUTE_ROWS, nb // mb, ta) for g in outs[2:]], axis=2).reshape(ROUTE_ROWS, nb * ta)
    return outs[0], outs[1], gates_t


MOE_TILE = 1024
MOE_ALIGN = 2 * SUBLANES
MOE_CHUNK = 18 * MOE_ALIGN
MOE_SORTED = MOE_TILE + LANES
MOE_ROWS = MOE_SORTED + MOE_CHUNK
GID_ROW = N_EXPERTS
ROUTE_ROWS = 3 * SUBLANES
META_LANES = 2 * N_GROUPS
assert N_GROUPS * (MOE_ALIGN - 1) <= MOE_SORTED - MOE_TILE and MOE_ROWS % MOE_ALIGN == 0


def _moe_sort_kernel(gt_ref, g_ref, h_ref, up_ref, hs_o, gs_o, pt_o, meta_o):
    gid = gt_ref[GID_ROW:GID_ROW + 1, :]
    onehot = [jnp.where(gid == float(g), 1.0, 0.0) for g in range(N_GROUPS)]
    g4 = jnp.concatenate(onehot + [jnp.zeros((SUBLANES - N_GROUPS, MOE_TILE), F32)], axis=0)
    before = jnp.dot(g4.astype(BF16), up_ref[...], preferred_element_type=F32)
    lane = lax.broadcasted_iota(jnp.int32, (SUBLANES, LANES), 1)
    meta = jnp.zeros((SUBLANES, LANES), F32)
    off = jnp.zeros((1, 1), F32)
    pos = jnp.zeros((1, MOE_TILE), F32)
    for g in range(N_GROUPS):
        cnt = jnp.sum(onehot[g], axis=1, keepdims=True)
        pos = pos + onehot[g] * (before[g:g + 1] + off)
        meta = jnp.where(lane == g, off, meta)
        meta = jnp.where(lane == N_GROUPS + g, cnt, meta)
        off = off + jnp.ceil(cnt * (1.0 / MOE_ALIGN)) * MOE_ALIGN
    meta_o[0] = meta.astype(jnp.int32)
    row = lax.broadcasted_iota(jnp.int32, (MOE_SORTED, MOE_TILE), 0)
    place = jnp.where(row == pos.astype(jnp.int32), 1.0, 0.0)
    p16 = place.astype(BF16)
    hs_o[0, :MOE_SORTED, :] = jnp.dot(p16, h_ref[...], preferred_element_type=F32).astype(BF16)
    hs_o[0, MOE_SORTED:, :] = jnp.zeros((MOE_ROWS - MOE_SORTED, h_ref.shape[1]), BF16)
    gts = g_ref[...]
    hi = gts.astype(BF16)
    r1 = gts - hi.astype(F32)
    mid = r1.astype(BF16)
    low = (r1 - mid.astype(F32)).astype(BF16)
    gs_o[0, :MOE_SORTED, :] = (jnp.dot(p16, hi, preferred_element_type=F32)
                               + jnp.dot(p16, mid, preferred_element_type=F32)
                               + jnp.dot(p16, low, preferred_element_type=F32))
    gs_o[0, MOE_SORTED:, :] = jnp.zeros((MOE_ROWS - MOE_SORTED, LANES), F32)
    pt_o[...] = place.T.astype(BF16)


def moe_sort(hf, gates_t, gates):
    n, d = hf.shape
    ntile = n // MOE_TILE
    upper = jnp.asarray(np.triu(np.ones((MOE_TILE, MOE_TILE), np.float32), 1), BF16)
    return pl.pallas_call(
        _moe_sort_kernel,
        out_shape=[jax.ShapeDtypeStruct((ntile, MOE_ROWS, d), BF16),
                   jax.ShapeDtypeStruct((ntile, MOE_ROWS, LANES), F32),
                   jax.ShapeDtypeStruct((n, MOE_SORTED), BF16),
                   jax.ShapeDtypeStruct((ntile, SUBLANES, LANES), jnp.int32)],
        grid=(ntile,),
        in_specs=[pl.BlockSpec((ROUTE_ROWS, MOE_TILE), lambda i: (0, i)),
                  pl.BlockSpec((MOE_TILE, LANES), lambda i: (i, 0)),
                  pl.BlockSpec((MOE_TILE, d), lambda i: (i, 0)),
                  pl.BlockSpec((MOE_TILE, MOE_TILE), lambda i: (0, 0))],
        out_specs=[pl.BlockSpec((1, MOE_ROWS, d), lambda i: (i, 0, 0)),
                   pl.BlockSpec((1, MOE_ROWS, LANES), lambda i: (i, 0, 0)),
                   pl.BlockSpec((MOE_TILE, MOE_SORTED), lambda i: (i, 0)),
                   pl.BlockSpec((1, SUBLANES, LANES), lambda i: (i, 0, 0))],
        compiler_params=_cparams(("parallel",), VMEM_LIMIT),
        name="moe_sort",
    )(gates_t, gates, hf, upper)


def _moe_group_kernel(meta_ref, hs_ref, gs_ref, w1_ref, w3_ref, w2_ref, prev_ref, ys_o):
    g, i = pl.program_id(0), pl.program_id(1)

    @pl.when(g == 0)
    def _():
        ys_o[...] = jnp.zeros_like(ys_o)

    @pl.when(g > 0)
    def _():
        ys_o[...] = prev_ref[...]

    off = meta_ref[i * META_LANES + g]
    cnt = meta_ref[i * META_LANES + N_GROUPS + g]
    lane = lax.broadcasted_iota(jnp.int32, (MOE_CHUNK, LANES), 1)

    def chunk(j, carry):
        rows = pl.ds(pl.multiple_of(off + j * MOE_CHUNK, MOE_ALIGN), MOE_CHUNK)
        hs = hs_ref[0, rows, :]
        gs = gs_ref[0, rows, :]
        acc = ys_o[0, rows, :].astype(F32)
        for e in range(EXPERTS_PER_GROUP):
            a = jnp.dot(hs, w1_ref[e], preferred_element_type=F32)
            b = jnp.dot(hs, w3_ref[e], preferred_element_type=F32)
            act = (a * jax.nn.sigmoid(a)) * b
            gcol = jnp.sum(jnp.where(lane == g * EXPERTS_PER_GROUP + e, gs, 0.0), axis=1, keepdims=True)
            acc = acc + gcol * jnp.dot(act.astype(BF16), w2_ref[e], preferred_element_type=F32)
        ys_o[0, rows, :] = acc.astype(BF16)
        return carry

    lax.fori_loop(0, lax.div(cnt + (MOE_CHUNK - 1), MOE_CHUNK), chunk, 0)


def moe_group(hs, gs, meta, w1, w3, w2, carried=None):
    ntile, _, d = hs.shape
    de = w1.shape[2]
    epg = EXPERTS_PER_GROUP
    assert ntile > 1
    tile_spec = pl.BlockSpec((1, MOE_ROWS, d), lambda g, i, m: (i, 0, 0))
    grid_spec = pltpu.PrefetchScalarGridSpec(
        num_scalar_prefetch=1,
        grid=(N_GROUPS, ntile),
        in_specs=[tile_spec,
                  pl.BlockSpec((1, MOE_ROWS, LANES), lambda g, i, m: (i, 0, 0)),
                  pl.BlockSpec((epg, d, de), lambda g, i, m: (g, 0, 0)),
                  pl.BlockSpec((epg, d, de), lambda g, i, m: (g, 0, 0)),
                  pl.BlockSpec((epg, de, d), lambda g, i, m: (g, 0, 0)),
                  pl.BlockSpec((1, MOE_ROWS, d), lambda g, i, m: (jnp.where(g == 0, ntile - 1, i), 0, 0))],
        out_specs=tile_spec,
    )
    if carried is None:
        carried = jnp.zeros((ntile, MOE_ROWS, d), BF16)
    return pl.pallas_call(
        _moe_group_kernel,
        out_shape=jax.ShapeDtypeStruct((ntile, MOE_ROWS, d), BF16),
        grid_spec=grid_spec,
        input_output_aliases={6: 0},
        compiler_params=_cparams(("arbitrary", "arbitrary"), VMEM_LIMIT),
        name="moe_group",
    )(meta, hs, gs, w1, w3, w2, carried)


def _ln2_kernel(emit_next, x_ref, pt_ref, ys_ref, gf_ref, g_ref, b_ref, *rest):
    alpha = (2 * 2) ** 0.25
    f = jnp.dot(pt_ref[...], ys_ref[0], preferred_element_type=F32)
    x2 = _ln(alpha * x_ref[0] + gf_ref[0] * f) * g_ref[...] + b_ref[...]
    if emit_next:
        sh_ref, sc_ref, o_ref, h_o = rest
        h_o[0] = (_ln(x2) * (1.0 + sc_ref[0]) + sh_ref[0]).astype(BF16)
    else:
        o_ref, = rest
    o_ref[0] = x2


def ln2(x1, pt, ysb, mod3, nct, ln_g, ln_b, mod3_next):
    nb, ta, d = x1.shape
    nt = ta // TT
    per = MOE_TILE // TT
    wide = pl.BlockSpec((1, TT, d), lambda b, t: (b, t, 0))
    row = pl.BlockSpec((1, d), lambda b, t: (0, 0))
    in_specs = [wide, pl.BlockSpec((TT, MOE_SORTED), lambda b, t: (b * nt + t, 0)),
                pl.BlockSpec((1, MOE_SORTED, d), lambda b, t: ((b * nt + t) // per, 0, 0)),
                _mod_spec(5, nct, nb), row, row]
    args = [x1, pt, ysb, mod3, ln_g.reshape(1, -1), ln_b.reshape(1, -1)]
    if mod3_next is None:
        out_shape = jax.ShapeDtypeStruct((nb, ta - nct * TT, d), F32)
        out_specs = pl.BlockSpec((1, TT, d), lambda b, t: (b, jnp.maximum(t - nct, 0), 0))
    else:
        in_specs += [_mod_spec(0, nct, nb), _mod_spec(1, nct, nb)]
        args += [mod3_next, mod3_next]
        out_shape = [jax.ShapeDtypeStruct((nb, ta, d), F32), jax.ShapeDtypeStruct((nb, ta, d), BF16)]
        out_specs = [wide, wide]
    return pl.pallas_call(
        functools.partial(_ln2_kernel, mod3_next is not None),
        out_shape=out_shape,
        grid=(nb, nt),
        in_specs=in_specs,
        out_specs=out_specs,
        compiler_params=_cparams(("parallel", "arbitrary"), VMEM_LIMIT),
        name="ln2",
    )(*args)


def _rope_tables(l, lc):
    half = HEAD_DIM // 2
    inv = ROPE_THETA ** (-np.arange(0, half, 2, dtype=np.float64) / half)
    t = np.arange(l)
    rows, cols = t // GRID_W, t % GRID_W
    ang = np.concatenate([rows[:, None] * inv, cols[:, None] * inv], -1)
    ang = np.concatenate([np.zeros((lc, half)), ang], 0)
    cos = np.repeat(np.cos(ang), 2, axis=1)
    sin = np.repeat(np.sin(ang), 2, axis=1)
    sin[:, 0::2] *= -1.0
    reps = Q_W // HEAD_DIM
    return (jnp.asarray(np.tile(cos, (1, reps)), F32), jnp.asarray(np.tile(sin, (1, reps)), F32))


def _block_diag_ones():
    i = np.arange(BRANCH_W) // HEAD_DIM
    return jnp.asarray((i[:, None] == i[None, :]).astype(np.float32))


def kernel(x, c, ctx, c_ctx, ada_w, ada_b, w_in, rwkv_mu, rwkv_w0, rwkv_w_up, rwkv_a0, rwkv_a_up, rwkv_g_up, rwkv_k_k, rwkv_k_a, rwkv_r_k, rwkv_lnx_g, rwkv_lnx_b, hyena_conv, hyena_w1, hyena_b1, hyena_freq1, hyena_w2, hyena_b2, hyena_freq2, hyena_w3, hyena_skip, sconv_w, attn_q_norm, attn_k_norm, w_branch, w_out, ln1_g, ln1_b, ln2_g, ln2_b, router_w, router_bias, exp_w1, exp_w3, exp_w2):
    nb, l, d = x.shape
    lc = ctx.shape[1]
    depth = ada_w.shape[0]
    assert d == D_MODEL and lc % TT == 0 and l % TT == 0 and l % GRID_W == 0 and (nb * (lc + l)) % MOE_TILE == 0
    ta = lc + l
    nct = lc // TT

    mod_rows = -(-(nb + 1) // SUBLANES) * SUBLANES
    cc = jnp.zeros((mod_rows, d), F32).at[:nb].set(c).at[nb].set(c_ctx)
    mod = ada_mod(cc, ada_w, ada_b)

    cos_t, sin_t = _rope_tables(l, lc)
    bd = _block_diag_ones()
    rwt = router_w.T

    ysb = None
    for li in range(depth):
        mod3 = mod[li].reshape(mod_rows, 1, N_MOD * d)
        wl = w_in[li].astype(BF16)
        if li == 0:
            x_all, h3 = lnmod(ctx, x, mod3, nct, 0, 1)
        h = h3.reshape(nb * ta, d)
        p_rwkv = matmul(h, wl[:, :OFF_HYENA], BF16).reshape(nb, ta, -1)
        p_hs = matmul(h, wl[:, OFF_HYENA:OFF_ATTN], BF16).reshape(nb, ta, -1)
        p_attn = matmul(h, wl[:, OFF_ATTN:OFF_GATE], BF16).reshape(nb, ta, -1)

        r, kk, w0, w1, k0, k1, b0, b1, v, g, bon = rwkv_prep(
            p_rwkv, nct, rwkv_mu[li], rwkv_w0[li], rwkv_w_up[li], rwkv_a0[li], rwkv_a_up[li], rwkv_g_up[li],
            rwkv_k_k[li], rwkv_k_a[li], rwkv_r_k[li], bd)
        yf, yb = rwkv_scan(r, kk, v, w0, w1, k0, k1, b0, b1, lc)
        ya = rwkv_out(yf, yb, g, bon, rwkv_lnx_g[li], rwkv_lnx_b[li], bd)

        q, kx, vx = attn_prep(p_attn, cos_t, sin_t, attn_q_norm[li], attn_k_norm[li], bd)
        yd = attention(q, kx, vx, lc)

        x0, u, ycv = hs_pre(p_hs, nct, hyena_conv[li], sconv_w[li])
        fargs = (hyena_w1[li], hyena_b1[li], hyena_freq1[li], hyena_w2[li], hyena_b2[li], hyena_freq2[li],
                 hyena_w3[li])
        yconv_ctx = hyena_conv_seg(u[:, :lc], fargs) if li < depth - 1 else jnp.zeros((nb, lc, BRANCH_W), F32)
        yconv = jnp.concatenate([yconv_ctx, hyena_conv_seg(u[:, lc:], fargs)], axis=1)

        x1, hf, gates_t = merge(ya, x0, u, yconv, ycv, yd, h3, x_all, mod3, nct, hyena_skip[li], wl[:, OFF_GATE:],
                                w_branch[li].astype(BF16), w_out[li].astype(BF16), ln1_g[li], ln1_b[li],
                                rwt, router_bias)
        gates = jnp.pad(gates_t.T, ((0, 0), (0, LANES - ROUTE_ROWS)))
        hs, gs, pt, meta = moe_sort(hf.reshape(nb * ta, d), gates_t, gates)
        ysb = moe_group(hs, gs, meta[:, 0, :META_LANES].reshape(-1), exp_w1[li].astype(BF16),
                        exp_w3[li].astype(BF16), exp_w2[li].astype(BF16), carried=ysb)
        if li == depth - 1:
            return ln2(x1, pt, ysb, mod3, nct, ln2_g[li], ln2_b[li], None)
        x_all, h3 = ln2(x1, pt, ysb, mod3, nct, ln2_g[li], ln2_b[li], mod[li + 1].reshape(mod_rows, 1, N_MOD * d))


def hyena_conv_seg(u_seg, fargs):
    ktab = hyena_filter_table(u_seg.shape[1], *fargs)
    return hyena_conv(u_seg, ktab)
```

```python
import functools
import math

import numpy as np
import jax
import jax.numpy as jnp
from jax import lax
from jax.experimental import pallas as pl
from jax.experimental.pallas import tpu as pltpu

F32 = jnp.float32
BF16 = jnp.bfloat16
HI = lax.Precision.HIGHEST

D_MODEL = 1024
GRID_W = 64
BRANCH_W = 256
HEAD_DIM = 64
N_BRANCHES = 4
N_MOD = 6
RWKV_COLS = 1024
RWKV_GN_EPS = 64e-5
HYENA_COLS = 768
HYENA_EMB = 33
HYENA_FAST_DECAY = 0.3
HYENA_SLOW_DECAY = 1.5
HYENA_TARGET = 1e-2
SCONV_COLS = 768
Q_W = 256
KV_W = 128
ATTN_COLS = 512
ROPE_THETA = 10000.0
RMS_EPS = 1e-6
OFF_HYENA = RWKV_COLS
OFF_SCONV = OFF_HYENA + HYENA_COLS
OFF_ATTN = OFF_SCONV + SCONV_COLS
OFF_GATE = OFF_ATTN + ATTN_COLS
N_EXPERTS = 16
N_GROUPS = 4
EXPERTS_PER_GROUP = 4
LN_EPS = 1e-6

SUBLANES = 8
LANES = 128
TT = 256
SCAN_BLK = LANES
HY_BLK = 256
VMEM_LIMIT = 56 * 1024 * 1024


def _cparams(sem, vmem=None):
    return pltpu.CompilerParams(dimension_semantics=sem, vmem_limit_bytes=vmem)


def _ln(xf):
    mu = jnp.mean(xf, -1, keepdims=True)
    xc = xf - mu
    var = jnp.mean(xc * xc, -1, keepdims=True)
    return xc * lax.rsqrt(var + LN_EPS)


def _head_sums(x, ones_bd):
    ones16 = ones_bd.astype(BF16)
    hi = x.astype(BF16)
    lo = (x - hi.astype(F32)).astype(BF16)
    return jnp.dot(hi, ones16, preferred_element_type=F32) + jnp.dot(lo, ones16, preferred_element_type=F32)


def _pick_tile(n, cands):
    for c in cands:
        if n % c == 0:
            return c
    raise ValueError(f"no tile for {n}")


def _ada_kernel(c_ref, w_ref, b_ref, o_ref):
    c = c_ref[...]
    a = c * jax.nn.sigmoid(c)
    o_ref[0] = jnp.dot(a, w_ref[0], precision=HI, preferred_element_type=F32) + b_ref[0]


def ada_mod(cc, ada_w, ada_b):
    depth, d, n = ada_w.shape
    rows = cc.shape[0]
    return pl.pallas_call(
        _ada_kernel,
        out_shape=jax.ShapeDtypeStruct((depth, rows, n), F32),
        grid=(depth, n // d),
        in_specs=[pl.BlockSpec((rows, d), lambda l, j: (0, 0)),
                  pl.BlockSpec((1, d, d), lambda l, j: (l, 0, j)),
                  pl.BlockSpec((1, 1, d), lambda l, j: (l, 0, j))],
        out_specs=pl.BlockSpec((1, rows, d), lambda l, j: (l, 0, j)),
        compiler_params=_cparams(("parallel", "parallel"), VMEM_LIMIT),
        name="ada_mod",
    )(cc, ada_w, ada_b.reshape(depth, 1, n))


def _lnmod_kernel(nct, c_ref, x_ref, sh_ref, sc_ref, xa_o, h_o):
    t = pl.program_id(1)
    xin = jnp.where(t < nct, c_ref[0], x_ref[0])
    xa_o[0] = xin
    h_o[0] = (_ln(xin) * (1.0 + sc_ref[0]) + sh_ref[0]).astype(BF16)


def _mod_spec(col, nct, nb):
    return pl.BlockSpec((1, 1, D_MODEL), lambda b, t: (jnp.where(t < nct, nb, b), 0, col))


def lnmod(ctx, x, mod3, nct, col_shift, col_scale):
    nb, l, d = x.shape
    ta = ctx.shape[1] + l
    wide = pl.BlockSpec((1, TT, d), lambda b, t: (b, t, 0))
    return pl.pallas_call(
        functools.partial(_lnmod_kernel, nct),
        out_shape=[jax.ShapeDtypeStruct((nb, ta, d), F32), jax.ShapeDtypeStruct((nb, ta, d), BF16)],
        grid=(nb, ta // TT),
        in_specs=[pl.BlockSpec((1, TT, d), lambda b, t: (b, jnp.minimum(t, nct - 1), 0)),
                  pl.BlockSpec((1, TT, d), lambda b, t: (b, jnp.maximum(t - nct, 0), 0)),
                  _mod_spec(col_shift, nct, nb), _mod_spec(col_scale, nct, nb)],
        out_specs=[wide, wide],
        compiler_params=_cparams(("parallel", "arbitrary")),
        name="lnmod",
    )(ctx, x, mod3, mod3)


def _mm_kernel(a_ref, b_ref, o_ref):
    o_ref[...] = jnp.dot(a_ref[...], b_ref[...], preferred_element_type=F32).astype(o_ref.dtype)


def matmul(a, b, out_dtype=F32):
    m, k = a.shape
    _, n = b.shape
    tm = _pick_tile(m, (1024, 512, 256))
    tn = _pick_tile(n, (1024, 512, 256))
    return pl.pallas_call(
        _mm_kernel,
        out_shape=jax.ShapeDtypeStruct((m, n), out_dtype),
        grid=(m // tm, n // tn),
        in_specs=[pl.BlockSpec((tm, k), lambda i, j: (i, 0)),
                  pl.BlockSpec((k, tn), lambda i, j: (0, j))],
        out_specs=pl.BlockSpec((tm, tn), lambda i, j: (i, j)),
        compiler_params=_cparams(("parallel", "parallel"), VMEM_LIMIT),
        name="matmul",
    )(a, b)


def _halo_specs(width, ta, halo=SUBLANES):
    nblk = ta // halo
    per = TT // halo
    cur = pl.BlockSpec((1, TT, width), lambda b, t: (b, t, 0))
    prev = pl.BlockSpec((1, halo, width), lambda b, t: (b, jnp.maximum(t * per - 1, 0), 0))
    nxt = pl.BlockSpec((1, halo, width), lambda b, t: (b, jnp.minimum((t + 1) * per, nblk - 1), 0))
    return [cur, prev, nxt]


def _neighbours(cur, prev_ref, next_ref, nct, nt):
    t = pl.program_id(1)
    seg_start = jnp.logical_or(t == 0, t == nct)
    seg_end = jnp.logical_or(t == nct - 1, t == nt - 1)
    halo = prev_ref.shape[1]
    prev_row = prev_ref[0][halo - 1:halo, :].astype(F32) * jnp.where(seg_start, 0.0, 1.0)
    next_row = next_ref[0][0:1, :].astype(F32) * jnp.where(seg_end, 0.0, 1.0)
    row = lax.broadcasted_iota(jnp.int32, (TT, 1), 0)
    xm1 = jnp.where(row == 0, prev_row, pltpu.roll(cur, 1, axis=0))
    xp1 = jnp.where(row == TT - 1, next_row, pltpu.roll(cur, TT - 1, axis=0))
    return xm1, xp1


def _rwkv_prep_kernel(nct, nt, cur_ref, prev_ref, next_ref, mu_ref, w0_ref, wup_ref, a0_ref, aup_ref,
                      gup_ref, kk_ref, ka_ref, rk_ref, bd_ref,
                      r_o, kk_o, w0_o, w1_o, k0_o, k1_o, b0_o, b1_o, v_o, g_o, bon_o):
    cur = cur_ref[0].astype(F32)
    xm1, xp1 = _neighbours(cur, prev_ref, next_ref, nct, nt)
    p = cur + mu_ref[...] * (0.5 * (xm1 + xp1) - cur)
    c = BRANCH_W
    r, k, v = p[:, 0:c], p[:, c:2 * c], p[:, 2 * c:3 * c]
    wd = p[:, 3 * c:3 * c + 64]
    ad = p[:, 3 * c + 64:3 * c + 128]
    gd = p[:, 3 * c + 128:3 * c + 256]
    bd = bd_ref[...]
    kk = k * kk_ref[...]
    ss = _head_sums(kk * kk, bd)
    kkn = kk * lax.rsqrt(jnp.maximum(ss, 1e-24))
    twd = jnp.tanh(wd)
    ka = ka_ref[...]
    kdirs = []
    w_outs, k_outs, b_outs = (w0_o, w1_o), (k0_o, k1_o), (b0_o, b1_o)
    for d in range(2):
        wlog = w0_ref[d:d + 1, :] + jnp.dot(twd, wup_ref[d], precision=HI, preferred_element_type=F32)
        decay = -math.exp(-0.5) * jax.nn.sigmoid(wlog)
        a = jax.nn.sigmoid(a0_ref[d:d + 1, :] + jnp.dot(ad, aup_ref[d], precision=HI, preferred_element_type=F32))
        kdir = k * (1.0 + (a - 1.0) * ka)
        bdir = kkn * a
        kdirs.append(kdir)
        w_outs[d][0] = decay
        k_outs[d][0] = kdir.astype(BF16)
        b_outs[d][0] = bdir.astype(BF16)
    r_o[0] = r.astype(BF16)
    kk_o[0] = kkn.astype(BF16)
    v_o[0] = v.astype(BF16)
    g_o[0] = jnp.dot(jax.nn.sigmoid(gd), gup_ref[...], precision=HI, preferred_element_type=F32)
    rkk = r * rk_ref[...] * (kdirs[0] + kdirs[1])
    bon_o[0] = _head_sums(rkk, bd) * v


def rwkv_prep(p_rwkv, nct, mu, w0, w_up, a0, a_up, g_up, k_k, k_a, r_k, bd):
    nb, ta, _ = p_rwkv.shape
    nt = ta // TT
    c = BRANCH_W
    nat = jax.ShapeDtypeStruct((nb, ta, c), F32)
    lo = jax.ShapeDtypeStruct((nb, ta, c), BF16)
    nat_spec = pl.BlockSpec((1, TT, c), lambda b, t: (b, t, 0))

    def full(a):
        nd = a.ndim
        return pl.BlockSpec(a.shape, lambda b, t: (0,) * nd)

    consts = [mu.reshape(1, -1), w0, w_up, a0, a_up, g_up, k_k.reshape(1, -1), k_a.reshape(1, -1),
              r_k.reshape(1, -1), bd]
    return pl.pallas_call(
        functools.partial(_rwkv_prep_kernel, nct, nt),
        out_shape=[lo, lo, nat, nat, lo, lo, lo, lo, lo, nat, nat],
        grid=(nb, nt),
        in_specs=_halo_specs(RWKV_COLS, ta, 2 * SUBLANES) + [full(a) for a in consts],
        out_specs=[nat_spec] * 11,
        compiler_params=_cparams(("parallel", "parallel"), VMEM_LIMIT),
        name="rwkv_prep",
    )(p_rwkv, p_rwkv, p_rwkv, *consts)


CHUNK = 32
PREP_NB = 4


def _chunk_scan_rows(x, reverse):
    pos = lax.broadcasted_iota(jnp.int32, x.shape, 0) % CHUNK
    step = 1
    while step < CHUNK:
        if reverse:
            x = x + jnp.where(pos < CHUNK - step, pltpu.roll(x, x.shape[0] - step, axis=0), 0.0)
        else:
            x = x + jnp.where(pos >= step, pltpu.roll(x, step, axis=0), 0.0)
        step *= 2
    return x


def _chunk_prep_kernel(r_ref, kk_ref, v_ref, lw0, k0, b0, lw1, k1, b1,
                       a0_o, bm0_o, rp0_o, y00_o, a1_o, bm1_o, rp1_o, y01_o):
    blk = SCAN_BLK
    npair = BRANCH_W // LANES
    nchunk = blk // CHUNK
    ti = lax.broadcasted_iota(jnp.int32, (blk, blk), 0)
    si = lax.broadcasted_iota(jnp.int32, (blk, blk), 1)
    same = (ti // CHUNK) == (si // CHUNK)
    eye = ti == si
    bd64 = (ti // HEAD_DIM) == (si // HEAD_DIM)
    head0 = si < HEAD_DIM
    lane_half = lax.broadcasted_iota(jnp.int32, (HEAD_DIM, LANES), 1)
    eyef = jnp.where(eye, 1.0, 0.0)
    dot = lambda x, y: jnp.dot(x, y, preferred_element_type=F32)
    lo = lambda x: x.astype(BF16)
    split = lambda x: jnp.concatenate([jnp.where(head0, x, 0.0), jnp.where(head0, 0.0, x)], axis=0)
    cat = lambda ms: lo(jnp.concatenate(ms, axis=1))

    probs = []
    for bi, d in [(bi, d) for bi in range(PREP_NB) for d in range(2)]:
        lw_ref, k_ref, b_ref = ((lw0, k0, b0), (lw1, k1, b1))[d]
        reverse = d == 1
        r, kk, v = (a[bi].astype(F32) for a in (r_ref, kk_ref, v_ref))
        lw, k, b = lw_ref[bi], k_ref[bi].astype(F32), b_ref[bi].astype(F32)
        lg = _chunk_scan_rows(lw, reverse)
        lg_end = lg + _chunk_scan_rows(lw, not reverse) - lw
        g, gi, g_end = jnp.exp(lg), jnp.exp(-lg), jnp.exp(lg_end)
        to_end = jnp.exp(lg_end - lg)
        arrs = (kk * jnp.exp(lg - lw), b * gi, k * gi, r * g, v, k * to_end, b * to_end, g_end)
        incl = jnp.logical_and(same, si >= ti if reverse else si <= ti)
        strict = jnp.logical_and(same, si > ti if reverse else si < ti)
        for p in range(npair):
            lanes = slice(p * LANES, (p + 1) * LANES)
            probs.append(dict(bi=bi, d=d, p=p, incl=incl, strict=strict, arrs=tuple(a[:, lanes] for a in arrs)))

    for q in probs:
        pp_, q_, kt_, rt_ = q["arrs"][:4]
        rhs_g = lo(jnp.concatenate([q_, kt_], axis=0))
        lm, mm, n2, nn = [], [], [], []
        for hh in range(2):
            hm = head0 if hh == 0 else jnp.logical_not(head0)
            lhs_g = lo(jnp.concatenate([jnp.where(hm, pp_, 0.0), jnp.where(hm, rt_, 0.0)], axis=0))
            gm = lax.dot_general(lhs_g, rhs_g, (((1,), (1,)), ((), ())), preferred_element_type=F32)
            lm.append(jnp.where(q["strict"], gm[:blk, :blk], 0.0))
            mm.append(jnp.where(q["strict"], gm[:blk, blk:], 0.0))
            n2.append(jnp.where(q["incl"], gm[blk:, :blk], 0.0))
            nn.append(jnp.where(q["incl"], gm[blk:, blk:], 0.0))
        q["pw"], q["tm"] = lm, [eyef - lm[0], eyef - lm[1]]
        q["m_cat"], q["n2_cat"], q["nn_cat"] = cat(mm), cat(n2), cat(nn)
    for _ in range(CHUNK.bit_length() - 2):
        for q in probs:
            pwl = [lo(x) for x in q["pw"]]
            q["pw"] = [dot(x, x) for x in pwl]
        for q in probs:
            q["tm"] = [dot(lo(t), lo(eyef + x)) for t, x in zip(q["tm"], q["pw"])]
    for q in probs:
        q["t_cat"] = cat(q["tm"])
        q["v_st"] = lo(split(q["arrs"][4]))
        q["pp"] = dot(q["t_cat"], lo(split(q["arrs"][0])))
        q["mv"] = dot(q["m_cat"], q["v_st"])
    for q in probs:
        q["w2"] = dot(q["t_cat"], lo(split(q["mv"])))
        q["rp"] = q["arrs"][3] - dot(q["n2_cat"], lo(split(q["pp"])))
    for q in probs:
        q["y0"] = dot(q["nn_cat"], q["v_st"]) - dot(q["n2_cat"], lo(split(q["w2"])))
    in_chunk = [si // CHUNK == c for c in range(nchunk)]
    for q in probs:
        ppt, vt, w2t = q["pp"].T, q["arrs"][4].T, q["w2"].T
        kg_, qg_ = q["arrs"][5], q["arrs"][6]
        lhs_a = jnp.concatenate([jnp.where(cm, ppt, 0.0) for cm in in_chunk], axis=0)
        q["pq"] = dot(lo(lhs_a), lo(qg_))
        lhs_b = jnp.concatenate([jnp.concatenate([jnp.where(cm, vt, 0.0), jnp.where(cm, -w2t, 0.0)], axis=1)
                                 for cm in in_chunk], axis=0)
        q["bf"] = dot(lo(lhs_b), lo(jnp.concatenate([kg_, qg_], axis=0)))
    outs = ((a0_o, bm0_o, rp0_o, y00_o), (a1_o, bm1_o, rp1_o, y01_o))
    for q in probs:
        a_o, bm_o = outs[q["d"]][:2]
        g_end_p = q["arrs"][7]
        for c in range(nchunk):
            pq_c = q["pq"][c * blk:(c + 1) * blk]
            a_o[q["bi"], 0, c, q["p"]] = (jnp.where(eye, g_end_p[c * CHUNK:c * CHUNK + 1], 0.0)
                                          - jnp.where(bd64, pq_c, 0.0)).astype(BF16)
            bm_o[q["bi"], 0, c, q["p"]] = jnp.where(lane_half < HEAD_DIM, q["bf"][c * blk:c * blk + HEAD_DIM],
                                                    q["bf"][c * blk + HEAD_DIM:(c + 1) * blk]).astype(BF16)
    for bi in range(PREP_NB):
        for d in range(2):
            rp_o, y0_o = outs[d][2:]
            rp_o[bi] = jnp.concatenate([q["rp"] for q in probs if q["d"] == d and q["bi"] == bi], axis=1)
            y0_o[bi] = jnp.concatenate([q["y0"] for q in probs if q["d"] == d and q["bi"] == bi], axis=1)


def chunk_prep(r, kk, v, lw0, lw1, k0, k1, b0, b1):
    nb, ta, c = r.shape
    nblk = ta // SCAN_BLK
    npair = c // LANES
    nchunk = SCAN_BLK // CHUNK
    pb = PREP_NB
    assert nb % pb == 0
    nat = pl.BlockSpec((pb, SCAN_BLK, c), lambda b, s: (b, s, 0))
    a_shape = jax.ShapeDtypeStruct((nb, nblk, nchunk, npair, LANES, LANES), BF16)
    bm_shape = jax.ShapeDtypeStruct((nb, nblk, nchunk, npair, HEAD_DIM, LANES), BF16)
    nat_shape = jax.ShapeDtypeStruct((nb, ta, c), F32)
    a_spec = pl.BlockSpec((pb, 1, nchunk, npair, LANES, LANES), lambda b, s: (b, s, 0, 0, 0, 0))
    bm_spec = pl.BlockSpec((pb, 1, nchunk, npair, HEAD_DIM, LANES), lambda b, s: (b, s, 0, 0, 0, 0))
    return pl.pallas_call(
        _chunk_prep_kernel,
        out_shape=[a_shape, bm_shape, nat_shape, nat_shape] * 2,
        grid=(nb // pb, nblk),
        in_specs=[nat] * 9,
        out_specs=[a_spec, bm_spec, nat, nat] * 2,
        compiler_params=_cparams(("parallel", "parallel"), VMEM_LIMIT),
        name="chunk_prep",
    )(r, kk, v, lw0, k0, b0, lw1, k1, b1)


def _chunk_scan_kernel(nb, a0, bm0, rp0, y00, a1, bm1, rp1, y01, yf_o, yb_o, s_scr):
    step = pl.program_id(0)
    npair = BRANCH_W // LANES
    nchunk = SCAN_BLK // CHUNK

    @pl.when(step == 0)
    def _():
        s_scr[...] = jnp.zeros_like(s_scr)

    lane = lax.broadcasted_iota(jnp.int32, (CHUNK, LANES), 1)
    refs = ((a0, bm0, rp0, y00, yf_o), (a1, bm1, rp1, y01, yb_o))
    for ci in range(nchunk):
        for d in range(2):
            a_ref, bm_ref, rp_ref, y0_ref, y_ref = refs[d]
            c = ci if d == 0 else nchunk - 1 - ci
            rows = slice(c * CHUNK, (c + 1) * CHUNK)
            for b in range(nb):
                for p in range(npair):
                    lanes = slice(p * LANES, (p + 1) * LANES)
                    s = s_scr[d, b, p]
                    rpc = rp_ref[b, rows, lanes]
                    lhs = jnp.concatenate([jnp.where(lane < HEAD_DIM, rpc, 0.0), jnp.where(lane >= HEAD_DIM, rpc, 0.0)],
                                          axis=0)
                    yh = lax.dot_general(lhs, s, (((1,), (1,)), ((), ())), preferred_element_type=F32)
                    y_ref[b, rows, lanes] = jnp.concatenate([yh[:CHUNK], yh[CHUNK:]], axis=1) + y0_ref[b, rows, lanes]
                    s_scr[d, b, p] = (jnp.dot(s.astype(BF16), a_ref[b, 0, c, p], preferred_element_type=F32)
                                      + bm_ref[b, 0, c, p])


def rwkv_scan(r, kk, v, lw0, lw1, k0, k1, b0, b1, lc):
    nb, ta, c = r.shape
    nblk = ta // SCAN_BLK
    nctb = lc // SCAN_BLK
    npair = c // LANES
    nchunk = SCAN_BLK // CHUNK
    a0, bm0, rp0, y00, a1, bm1, rp1, y01 = chunk_prep(r, kk, v, lw0, lw1, k0, k1, b0, b1)

    def fwd(s):
        return s

    def bwd(s):
        return jnp.where(s < nctb, nctb - 1 - s, nblk - 1 - (s - nctb))

    def specs(idx):
        return [pl.BlockSpec((nb, 1, nchunk, npair, LANES, LANES), lambda s: (0, idx(s), 0, 0, 0, 0)),
                pl.BlockSpec((nb, 1, nchunk, npair, HEAD_DIM, LANES), lambda s: (0, idx(s), 0, 0, 0, 0)),
                pl.BlockSpec((nb, SCAN_BLK, c), lambda s: (0, idx(s), 0)),
                pl.BlockSpec((nb, SCAN_BLK, c), lambda s: (0, idx(s), 0))]

    out = jax.ShapeDtypeStruct((nb, ta, c), F32)
    return pl.pallas_call(
        functools.partial(_chunk_scan_kernel, nb),
        out_shape=[out, out],
        grid=(nblk,),
        in_specs=specs(fwd) + specs(bwd),
        out_specs=[pl.BlockSpec((nb, SCAN_BLK, c), lambda s: (0, fwd(s), 0)),
                   pl.BlockSpec((nb, SCAN_BLK, c), lambda s: (0, bwd(s), 0))],
        scratch_shapes=[pltpu.VMEM((2, nb, npair, HEAD_DIM, LANES), F32)],
        compiler_params=_cparams(("arbitrary",), VMEM_LIMIT),
        name="chunk_scan",
    )(a0, bm0, rp0, y00, a1, bm1, rp1, y01)


def _pair_swap(x):
    lane = lax.broadcasted_iota(jnp.int32, x.shape, 1)
    n = x.shape[1]
    return jnp.where(lane % 2 == 0, pltpu.roll(x, n - 1, axis=1), pltpu.roll(x, 1, axis=1))


def _attn_prep_kernel(p_ref, cos_ref, sin_ref, qg_ref, kg_ref, bd_ref, q_o, k_o, v_o):
    p = p_ref[0].astype(F32)
    q, k, v = p[:, :Q_W], p[:, Q_W:Q_W + KV_W], p[:, Q_W + KV_W:]
    bd = bd_ref[...]
    cos, sin = cos_ref[...], sin_ref[...]
    qms = _head_sums(q * q, bd) * (1.0 / HEAD_DIM)
    qn = q * lax.rsqrt(qms + RMS_EPS) * qg_ref[...]
    qr = qn * cos + _pair_swap(qn) * sin
    q_o[0] = (qr * HEAD_DIM ** -0.5).astype(BF16)
    kms = _head_sums(k * k, bd[:KV_W, :KV_W]) * (1.0 / HEAD_DIM)
    kn = k * lax.rsqrt(kms + RMS_EPS) * kg_ref[...]
    kr = kn * cos[:, :KV_W] + _pair_swap(kn) * sin[:, :KV_W]
    for g in range(KV_W // HEAD_DIM):
        sl = slice(g * HEAD_DIM, (g + 1) * HEAD_DIM)
        k_o[0, g] = kr[:, sl].astype(BF16)
        v_o[0, g] = v[:, sl].astype(BF16)


def attn_prep(p_attn, cos_t, sin_t, q_norm, k_norm, bd):
    nb, ta, _ = p_attn.shape
    ng = KV_W // HEAD_DIM
    qg = jnp.tile(q_norm, Q_W // HEAD_DIM).reshape(1, -1)
    kg = jnp.tile(k_norm, ng).reshape(1, -1)
    kv_shape = jax.ShapeDtypeStruct((nb, ng, ta, HEAD_DIM), BF16)
    kv_spec = pl.BlockSpec((1, ng, TT, HEAD_DIM), lambda b, t: (b, 0, t, 0))
    return pl.pallas_call(
        _attn_prep_kernel,
        out_shape=[jax.ShapeDtypeStruct((nb, ta, Q_W), BF16), kv_shape, kv_shape],
        grid=(nb, ta // TT),
        in_specs=[pl.BlockSpec((1, TT, ATTN_COLS), lambda b, t: (b, t, 0)),
                  pl.BlockSpec((TT, Q_W), lambda b, t: (t, 0)),
                  pl.BlockSpec((TT, Q_W), lambda b, t: (t, 0)),
                  pl.BlockSpec((1, Q_W), lambda b, t: (0, 0)),
                  pl.BlockSpec((1, KV_W), lambda b, t: (0, 0)),
                  pl.BlockSpec(bd.shape, lambda b, t: (0, 0))],
        out_specs=[pl.BlockSpec((1, TT, Q_W), lambda b, t: (b, t, 0)), kv_spec, kv_spec],
        compiler_params=_cparams(("parallel", "parallel")),
        name="attn_prep",
    )(p_attn, cos_t, sin_t, qg, kg, bd)


def _attn_kernel(nct, lc, q_ref, k_ref, v_ref, o_ref):
    t = pl.program_id(1)
    ng = k_ref.shape[1]
    rep = Q_W // HEAD_DIM // ng

    def run(nk):
        outs = []
        for g in range(ng):
            kk, vv = k_ref[0, g, :nk, :], v_ref[0, g, :nk, :]
            for r in range(rep):
                h = g * rep + r
                q = q_ref[0][:, h * HEAD_DIM:(h + 1) * HEAD_DIM]
                s = lax.dot_general(q, kk, (((1,), (1,)), ((), ())), preferred_element_type=F32)
                p = jnp.exp(s - jnp.max(s, -1, keepdims=True))
                l = jnp.sum(p, -1, keepdims=True)
                o = jnp.dot(p.astype(BF16), vv, preferred_element_type=F32)
                outs.append(o / l)
        o_ref[0] = jnp.concatenate(outs, axis=1).astype(BF16)

    @pl.when(t < nct)
    def _():
        run(lc)

    @pl.when(t >= nct)
    def _():
        run(k_ref.shape[2])


def attention(q, k, v, lc):
    nb, ta, _ = q.shape
    ng = k.shape[1]
    nct = lc // TT
    qo_spec = pl.BlockSpec((1, TT, Q_W), lambda b, t: (b, t, 0))
    kv_spec = pl.BlockSpec((1, ng, ta, HEAD_DIM), lambda b, t: (b, 0, 0, 0))
    return pl.pallas_call(
        functools.partial(_attn_kernel, nct, lc),
        out_shape=jax.ShapeDtypeStruct((nb, ta, Q_W), BF16),
        grid=(nb, ta // TT),
        in_specs=[qo_spec, kv_spec, kv_spec],
        out_specs=qo_spec,
        compiler_params=_cparams(("parallel", "arbitrary"), VMEM_LIMIT),
        name="attention",
    )(q, k, v)


def _hs_pre_kernel(nct, nt, cur_ref, prev_ref, next_ref, hw_ref, sw_ref, x0_o, u_o, ycv_o):
    cur = cur_ref[0].astype(F32)
    xm1, xp1 = _neighbours(cur, prev_ref, next_ref, nct, nt)
    c = BRANCH_W
    hc = HYENA_COLS
    hw = hw_ref[...]
    ph = hw[0:1] * xm1[:, :hc] + hw[1:2] * cur[:, :hc] + hw[2:3] * xp1[:, :hc]
    x0_o[0] = ph[:, :c].astype(BF16)
    u_o[0] = (ph[:, c:2 * c] * ph[:, 2 * c:3 * c]).astype(BF16)
    sw = sw_ref[...]

    def cx(a):
        return a[:, hc + c:hc + 2 * c] * a[:, hc + 2 * c:hc + 3 * c]

    conv = sw[0:1] * cx(xm1) + sw[1:2] * cx(cur) + sw[2:3] * cx(xp1)
    ycv_o[0] = (cur[:, hc:hc + c] * conv).astype(BF16)


def hs_pre(p_hs, nct, hyena_conv, sconv_w):
    nb, ta, w = p_hs.shape
    nt = ta // TT
    nat = pl.BlockSpec((1, TT, BRANCH_W), lambda b, t: (b, t, 0))
    return pl.pallas_call(
        functools.partial(_hs_pre_kernel, nct, nt),
        out_shape=[jax.ShapeDtypeStruct((nb, ta, BRANCH_W), BF16)] * 3,
        grid=(nb, nt),
        in_specs=_halo_specs(w, ta, 2 * SUBLANES) + [pl.BlockSpec(hyena_conv.shape, lambda b, t: (0, 0)),
                                       pl.BlockSpec(sconv_w.shape, lambda b, t: (0, 0))],
        out_specs=[nat, nat, nat],
        compiler_params=_cparams(("parallel", "parallel"), VMEM_LIMIT),
        name="hs_pre",
    )(p_hs, p_hs, p_hs, hyena_conv, sconv_w)


EMB_PAD = 40


def _filter_tables(lh):
    n = np.arange(2 * lh)
    pos = np.abs(n - (lh - 1)).astype(np.float64)
    bands = (HYENA_EMB - 1) // 2
    t = np.minimum(pos, lh - 1) / (lh - 1)
    wpos = 2.0 * math.pi * pos / lh
    f = np.linspace(1e-4, bands - 1, bands)[:, None]
    z = np.zeros((EMB_PAD, 2 * lh), np.float32)
    z[0] = t
    z[1:1 + bands] = np.cos(f * wpos[None, :])
    z[1 + bands:1 + 2 * bands] = -np.sin(f * wpos[None, :])
    max_decay = math.log(HYENA_TARGET) / HYENA_FAST_DECAY
    min_decay = math.log(HYENA_TARGET) / HYENA_SLOW_DECAY
    deltas = np.abs(np.linspace(min_decay, max_decay, BRANCH_W)).astype(np.float32)
    return z, deltas.reshape(-1, 1)


def _filter_kernel(lh, tn, z_ref, w1_ref, b1_ref, f1_ref, w2_ref, b2_ref, f2_ref, w3_ref, dl_ref, o_ref):
    z = z_ref[...]
    h1 = jnp.sin(f1_ref[...] * (jnp.dot(w1_ref[...], z, precision=HI, preferred_element_type=F32) + b1_ref[...]))
    h2 = jnp.sin(f2_ref[...] * (jnp.dot(w2_ref[...], h1, precision=HI, preferred_element_type=F32) + b2_ref[...]))
    f = jnp.dot(w3_ref[...], h2, precision=HI, preferred_element_type=F32)
    n = pl.program_id(0) * tn + lax.broadcasted_iota(jnp.int32, (1, tn), 1)
    filt = jnp.where(n >= lh - 1, f[:BRANCH_W], f[BRANCH_W:])
    win = jnp.exp(-z[0:1, :] * dl_ref[...])
    o_ref[...] = jnp.where(n == 2 * lh - 1, 0.0, filt * win)


def hyena_filter_table(lh, w1, b1, f1, w2, b2, f2, w3):
    z_np, dl_np = _filter_tables(lh)
    n2 = 2 * lh
    tn = _pick_tile(n2, (1024, 512))
    hd = w2.shape[0]
    w1t = jnp.zeros((hd, EMB_PAD), F32).at[:, :HYENA_EMB].set(w1.T)
    args = [jnp.asarray(z_np), w1t, b1.reshape(-1, 1), f1.reshape(-1, 1), w2.T, b2.reshape(-1, 1),
            f2.reshape(-1, 1), w3.T, jnp.asarray(dl_np)]

    def full(a):
        return pl.BlockSpec(a.shape, lambda j: (0, 0))

    return pl.pallas_call(
        functools.partial(_filter_kernel, lh, tn),
        out_shape=jax.ShapeDtypeStruct((BRANCH_W, n2), F32),
        grid=(n2 // tn,),
        in_specs=[pl.BlockSpec((EMB_PAD, tn), lambda j: (0, j))] + [full(a) for a in args[1:]],
        out_specs=pl.BlockSpec((BRANCH_W, tn), lambda j: (0, j)),
        compiler_params=_cparams(("parallel",)),
        name="hyena_filter",
    )(*args)


def _hyena_conv_kernel(nblk, bp, nch, k_ref, u_ref, o_ref, t_scr):
    ntile = 4 * nblk - 1
    mc = 2 * nblk - 1
    width = (ntile + 1) * LANES
    for ch in range(nch):
        big = pltpu.roll(jnp.broadcast_to(k_ref[ch], (LANES, width)), width - (LANES - 1), 1, stride=1, stride_axis=0)
        for m in range(ntile):
            t_scr[ch, m] = big[:, m * LANES:(m + 1) * LANES].astype(BF16)

    for ch in range(nch):
        for d in [0] + [s * a for a in range(1, nblk) for s in (1, -1)]:
            m0 = 2 * d + mc
            w = jnp.concatenate([jnp.concatenate([t_scr[ch, m0], t_scr[ch, m0 + 1]], axis=1),
                                 jnp.concatenate([t_scr[ch, m0 - 1], t_scr[ch, m0]], axis=1)], axis=0)
            i0, i1 = max(0, d), min(nblk, nblk + d)
            lhs = u_ref[ch, (i0 - d) * bp:(i1 - d) * bp, :].astype(BF16)
            res = jnp.dot(lhs, w, preferred_element_type=F32)
            if d == 0:
                o_ref[ch] = res
            else:
                o_ref[ch, i0 * bp:i1 * bp, :] += res


def hyena_conv(u, ktab):
    nb, l, c = u.shape
    nblk = l // HY_BLK
    bp = -(-nb // SUBLANES) * SUBLANES
    nch = SUBLANES if nblk == 1 else 2
    ut = jnp.transpose(u.astype(F32).reshape(nb, nblk, HY_BLK, c), (3, 1, 0, 2))
    if bp != nb:
        ut = jnp.pad(ut, ((0, 0), (0, 0), (0, bp - nb), (0, 0)))
    ut = ut.reshape(c, nblk * bp, HY_BLK)
    k3 = ktab.reshape(c, 1, 4 * nblk * LANES)
    out = pl.pallas_call(
        functools.partial(_hyena_conv_kernel, nblk, bp, nch),
        out_shape=jax.ShapeDtypeStruct((c, nblk * bp, HY_BLK), F32),
        grid=(c // nch,),
        in_specs=[pl.BlockSpec((nch, 1, 4 * nblk * LANES), lambda ch: (ch, 0, 0)),
                  pl.BlockSpec((nch, nblk * bp, HY_BLK), lambda ch: (ch, 0, 0))],
        out_specs=pl.BlockSpec((nch, nblk * bp, HY_BLK), lambda ch: (ch, 0, 0)),
        scratch_shapes=[pltpu.VMEM((nch, 4 * nblk - 1, LANES, LANES), BF16)],
        compiler_params=_cparams(("parallel",)),
        name="hyena_conv",
    )(k3, ut)
    out = out.reshape(c, nblk, bp, HY_BLK)[:, :, :nb]
    return jnp.transpose(out, (2, 1, 3, 0)).reshape(nb, l, c)


def _route(logits, bias):
    s = jax.nn.sigmoid(logits)
    sel = s + bias
    srow = [s[e:e + 1] for e in range(N_EXPERTS)]
    row = [sel[e:e + 1] for e in range(N_EXPERTS)]
    best, gi = None, None
    for g in range(N_GROUPS):
        a, b, c, d = row[4 * g:4 * g + 4]
        hi1, lo1, hi2, lo2 = jnp.maximum(a, b), jnp.minimum(a, b), jnp.maximum(c, d), jnp.minimum(c, d)
        score = jnp.maximum(hi1, hi2) + jnp.maximum(jnp.minimum(hi1, hi2), jnp.maximum(lo1, lo2))
        if g == 0:
            best, gi = score, jnp.zeros(score.shape, jnp.int32)
        else:
            better = score > best
            gi = jnp.where(better, g, gi)
            best = jnp.where(better, score, best)
    neg = -jnp.inf
    msel = [jnp.where(gi == e // EXPERTS_PER_GROUP, row[e], neg) for e in range(N_EXPERTS)]

    def arg_first_max(vals):
        bv, bi = vals[0], jnp.zeros(vals[0].shape, jnp.int32)
        for e in range(1, N_EXPERTS):
            better = vals[e] > bv
            bi = jnp.where(better, e, bi)
            bv = jnp.where(better, vals[e], bv)
        return bi

    i1 = arg_first_max(msel)
    i2 = arg_first_max([jnp.where(i1 == e, neg, msel[e]) for e in range(N_EXPERTS)])
    w1 = sum(jnp.where(i1 == e, srow[e], 0.0) for e in range(N_EXPERTS))
    w2 = sum(jnp.where(i2 == e, srow[e], 0.0) for e in range(N_EXPERTS))
    den = w1 + w2
    g1, g2 = w1 / den, w2 / den
    rows = [jnp.where(i1 == e, g1, 0.0) + jnp.where(i2 == e, g2, 0.0) for e in range(N_EXPERTS)]
    rows.append(gi.astype(F32))
    rows.extend([jnp.zeros_like(g1)] * (ROUTE_ROWS - len(rows)))
    return jnp.concatenate(rows, axis=0)


MERGE_NB = 2


def _merge_kernel(yf_ref, yr_ref, rg_ref, bon_ref, x0_ref, u_ref, yc_ref, ycv_ref, yd_ref, h_ref, x_ref, *rest):
    mods, rest = rest[:3 * MERGE_NB], rest[3 * MERGE_NB:]
    skip_ref, wg_ref, wb_ref, wo_ref, g1_ref, b1_ref, rwt_ref, rb_ref, lxg_ref, lxb_ref, bd_ref = rest[:11]
    x1_o, hf_o = rest[11:13]
    gates_o = rest[13:]
    rows = MERGE_NB * TT
    flat = lambda ref: ref[...].reshape(rows, ref.shape[-1])
    y = flat(yf_ref) + flat(yr_ref)
    bd = bd_ref[...]
    yc = y - _head_sums(y, bd) * (1.0 / HEAD_DIM)
    var = _head_sums(yc * yc, bd) * (1.0 / HEAD_DIM)
    ya = ((yc * lax.rsqrt(var + RWKV_GN_EPS) * lxg_ref[...] + lxb_ref[...] + flat(bon_ref)) * flat(rg_ref)).astype(BF16)
    yb = (flat(x0_ref) * (flat(yc_ref) + flat(u_ref) * skip_ref[...])).astype(BF16)
    ys = (ya, yb, flat(ycv_ref), flat(yd_ref))
    h = flat(h_ref)
    merged = None
    for n in range(N_BRANCHES):
        gate = jax.nn.sigmoid(jnp.dot(h, wg_ref[:, n * D_MODEL:(n + 1) * D_MODEL], preferred_element_type=F32))
        term = gate * jnp.dot(ys[n], wb_ref[n], preferred_element_type=F32)
        merged = term if merged is None else merged + term
    out = jnp.dot(merged.astype(BF16), wo_ref[...], preferred_element_type=F32)
    alpha = (2 * 2) ** 0.25
    for k in range(MERGE_NB):
        ga_ref, shf_ref, scf_ref = mods[3 * k:3 * k + 3]
        x1 = _ln(alpha * x_ref[k] + ga_ref[0] * out[k * TT:(k + 1) * TT]) * g1_ref[...] + b1_ref[...]
        hf = _ln(x1) * (1.0 + scf_ref[0]) + shf_ref[0]
        x1_o[k] = x1
        hf_o[k] = hf.astype(BF16)
        logits = lax.dot_general(rwt_ref[...], hf, (((1,), (1,)), ((), ())), precision=HI,
                                 preferred_element_type=F32)
        gates_o[k][...] = _route(logits, rb_ref[...])


def merge(yf, yr, rg, bon, x0, u, yconv, ycv, yd, h, x_all, mod3, nct, skip, wg, wb, wo, ln_g, ln_b, rwt, rbias,
          lnx_g, lnx_b, bd):
    nb, ta, d = x_all.shape
    nt = ta // TT
    mb = MERGE_NB
    assert nb % mb == 0
    nat = pl.BlockSpec((mb, TT, BRANCH_W), lambda b, t: (b, t, 0))
    wide = pl.BlockSpec((mb, TT, d), lambda b, t: (b, t, 0))

    def full(a):
        nd = a.ndim
        return pl.BlockSpec(a.shape, lambda b, t: (0,) * nd)

    def mod_spec(col, k):
        return pl.BlockSpec((1, 1, D_MODEL), lambda b, t: (jnp.where(t < nct, nb, b * mb + k), 0, col))

    mod_specs = [mod_spec(col, k) for k in range(mb) for col in (2, 3, 4)]
    consts = [skip.reshape(1, -1), wg, wb, wo, ln_g.reshape(1, -1), ln_b.reshape(1, -1), rwt, rbias.reshape(-1, 1),
              lnx_g.reshape(1, -1), lnx_b.reshape(1, -1), bd]
    gate_shape = jax.ShapeDtypeStruct((ROUTE_ROWS, nb // mb * ta), F32)
    gate_spec = pl.BlockSpec((ROUTE_ROWS, TT), lambda b, t: (0, b * nt + t))
    outs = pl.pallas_call(
        _merge_kernel,
        out_shape=[jax.ShapeDtypeStruct((nb, ta, d), F32), jax.ShapeDtypeStruct((nb, ta, d), BF16)] + [gate_shape] * mb,
        grid=(nb // mb, nt),
        in_specs=[nat] * 9 + [wide, wide] + mod_specs + [full(a) for a in consts],
        out_specs=[wide, wide] + [gate_spec] * mb,
        compiler_params=_cparams(("parallel", "parallel"), VMEM_LIMIT),
        name="merge",
    )(yf, yr, rg, bon, x0, u, yconv, ycv, yd, h, x_all, *([mod3] * (3 * mb)), *consts)
    gates_t = jnp.stack([g.reshape(ROUTE_ROWS, nb // mb, ta) for g in outs[2:]], axis=2).reshape(ROUTE_ROWS, nb * ta)
    return outs[0], outs[1], gates_t


MOE_TILE = 1024
MOE_ALIGN = 2 * SUBLANES
MOE_CHUNK = 18 * MOE_ALIGN
MOE_SORTED = MOE_TILE + LANES
MOE_ROWS = MOE_SORTED + MOE_CHUNK
GID_ROW = N_EXPERTS
ROUTE_ROWS = 3 * SUBLANES
META_LANES = 2 * N_GROUPS
assert N_GROUPS * (MOE_ALIGN - 1) <= MOE_SORTED - MOE_TILE and MOE_ROWS % MOE_ALIGN == 0


def _moe_sort_kernel(gt_ref, g_ref, h_ref, up_ref, hs_o, gs_o, pt_o, meta_o):
    gid = gt_ref[GID_ROW:GID_ROW + 1, :]
    onehot = [jnp.where(gid == float(g), 1.0, 0.0) for g in range(N_GROUPS)]
    g4 = jnp.concatenate(onehot + [jnp.zeros((SUBLANES - N_GROUPS, MOE_TILE), F32)], axis=0)
    before = jnp.dot(g4.astype(BF16), up_ref[...], preferred_element_type=F32)
    lane = lax.broadcasted_iota(jnp.int32, (SUBLANES, LANES), 1)
    meta = jnp.zeros((SUBLANES, LANES), F32)
    off = jnp.zeros((1, 1), F32)
    pos = jnp.zeros((1, MOE_TILE), F32)
    for g in range(N_GROUPS):
        cnt = jnp.sum(onehot[g], axis=1, keepdims=True)
        pos = pos + onehot[g] * (before[g:g + 1] + off)
        meta = jnp.where(lane == g, off, meta)
        meta = jnp.where(lane == N_GROUPS + g, cnt, meta)
        off = off + jnp.ceil(cnt * (1.0 / MOE_ALIGN)) * MOE_ALIGN
    meta_o[0] = meta.astype(jnp.int32)
    row = lax.broadcasted_iota(jnp.int32, (MOE_SORTED, MOE_TILE), 0)
    place = jnp.where(row == pos.astype(jnp.int32), 1.0, 0.0)
    p16 = place.astype(BF16)
    hs_o[0, :MOE_SORTED, :] = jnp.dot(p16, h_ref[...], preferred_element_type=F32).astype(BF16)
    hs_o[0, MOE_SORTED:, :] = jnp.zeros((MOE_ROWS - MOE_SORTED, h_ref.shape[1]), BF16)
    gts = g_ref[...]
    hi = gts.astype(BF16)
    r1 = gts - hi.astype(F32)
    mid = r1.astype(BF16)
    low = (r1 - mid.astype(F32)).astype(BF16)
    gs_o[0, :MOE_SORTED, :] = (jnp.dot(p16, hi, preferred_element_type=F32)
                               + jnp.dot(p16, mid, preferred_element_type=F32)
                               + jnp.dot(p16, low, preferred_element_type=F32))
    gs_o[0, MOE_SORTED:, :] = jnp.zeros((MOE_ROWS - MOE_SORTED, LANES), F32)
    pt_o[...] = place.T.astype(BF16)


def moe_sort(hf, gates_t, gates):
    n, d = hf.shape
    ntile = n // MOE_TILE
    upper = jnp.asarray(np.triu(np.ones((MOE_TILE, MOE_TILE), np.float32), 1), BF16)
    return pl.pallas_call(
        _moe_sort_kernel,
        out_shape=[jax.ShapeDtypeStruct((ntile, MOE_ROWS, d), BF16),
                   jax.ShapeDtypeStruct((ntile, MOE_ROWS, LANES), F32),
                   jax.ShapeDtypeStruct((n, MOE_SORTED), BF16),
                   jax.ShapeDtypeStruct((ntile, SUBLANES, LANES), jnp.int32)],
        grid=(ntile,),
        in_specs=[pl.BlockSpec((ROUTE_ROWS, MOE_TILE), lambda i: (0, i)),
                  pl.BlockSpec((MOE_TILE, LANES), lambda i: (i, 0)),
                  pl.BlockSpec((MOE_TILE, d), lambda i: (i, 0)),
                  pl.BlockSpec((MOE_TILE, MOE_TILE), lambda i: (0, 0))],
        out_specs=[pl.BlockSpec((1, MOE_ROWS, d), lambda i: (i, 0, 0)),
                   pl.BlockSpec((1, MOE_ROWS, LANES), lambda i: (i, 0, 0)),
                   pl.BlockSpec((MOE_TILE, MOE_SORTED), lambda i: (i, 0)),
                   pl.BlockSpec((1, SUBLANES, LANES), lambda i: (i, 0, 0))],
        compiler_params=_cparams(("parallel",), VMEM_LIMIT),
        name="moe_sort",
    )(gates_t, gates, hf, upper)


def _moe_group_kernel(meta_ref, hs_ref, gs_ref, w1_ref, w3_ref, w2_ref, prev_ref, ys_o):
    g, i = pl.program_id(0), pl.program_id(1)

    @pl.when(g == 0)
    def _():
        ys_o[...] = jnp.zeros_like(ys_o)

    @pl.when(g > 0)
    def _():
        ys_o[...] = prev_ref[...]

    off = meta_ref[i * META_LANES + g]
    cnt = meta_ref[i * META_LANES + N_GROUPS + g]
    lane = lax.broadcasted_iota(jnp.int32, (MOE_CHUNK, LANES), 1)

    def chunk(j, carry):
        rows = pl.ds(pl.multiple_of(off + j * MOE_CHUNK, MOE_ALIGN), MOE_CHUNK)
        hs = hs_ref[0, rows, :]
        gs = gs_ref[0, rows, :]
        acc = ys_o[0, rows, :].astype(F32)
        for e in range(EXPERTS_PER_GROUP):
            a = jnp.dot(hs, w1_ref[e], preferred_element_type=F32)
            b = jnp.dot(hs, w3_ref[e], preferred_element_type=F32)
            act = (a * jax.nn.sigmoid(a)) * b
            gcol = jnp.sum(jnp.where(lane == g * EXPERTS_PER_GROUP + e, gs, 0.0), axis=1, keepdims=True)
            acc = acc + gcol * jnp.dot(act.astype(BF16), w2_ref[e], preferred_element_type=F32)
        ys_o[0, rows, :] = acc.astype(BF16)
        return carry

    lax.fori_loop(0, lax.div(cnt + (MOE_CHUNK - 1), MOE_CHUNK), chunk, 0)


def moe_group(hs, gs, meta, w1, w3, w2, carried=None):
    ntile, _, d = hs.shape
    de = w1.shape[2]
    epg = EXPERTS_PER_GROUP
    assert ntile > 1
    tile_spec = pl.BlockSpec((1, MOE_ROWS, d), lambda g, i, m: (i, 0, 0))
    grid_spec = pltpu.PrefetchScalarGridSpec(
        num_scalar_prefetch=1,
        grid=(N_GROUPS, ntile),
        in_specs=[tile_spec,
                  pl.BlockSpec((1, MOE_ROWS, LANES), lambda g, i, m: (i, 0, 0)),
                  pl.BlockSpec((epg, d, de), lambda g, i, m: (g, 0, 0)),
                  pl.BlockSpec((epg, d, de), lambda g, i, m: (g, 0, 0)),
                  pl.BlockSpec((epg, de, d), lambda g, i, m: (g, 0, 0)),
                  pl.BlockSpec((1, MOE_ROWS, d), lambda g, i, m: (jnp.where(g == 0, ntile - 1, i), 0, 0))],
        out_specs=tile_spec,
    )
    if carried is None:
        carried = jnp.zeros((ntile, MOE_ROWS, d), BF16)
    return pl.pallas_call(
        _moe_group_kernel,
        out_shape=jax.ShapeDtypeStruct((ntile, MOE_ROWS, d), BF16),
        grid_spec=grid_spec,
        input_output_aliases={6: 0},
        compiler_params=_cparams(("arbitrary", "arbitrary"), VMEM_LIMIT),
        name="moe_group",
    )(meta, hs, gs, w1, w3, w2, carried)


def _ln2_kernel(emit_next, x_ref, pt_ref, ys_ref, gf_ref, g_ref, b_ref, *rest):
    alpha = (2 * 2) ** 0.25
    f = jnp.dot(pt_ref[...], ys_ref[0], preferred_element_type=F32)
    x2 = _ln(alpha * x_ref[0] + gf_ref[0] * f) * g_ref[...] + b_ref[...]
    if emit_next:
        sh_ref, sc_ref, o_ref, h_o = rest
        h_o[0] = (_ln(x2) * (1.0 + sc_ref[0]) + sh_ref[0]).astype(BF16)
    else:
        o_ref, = rest
    o_ref[0] = x2


def ln2(x1, pt, ysb, mod3, nct, ln_g, ln_b, mod3_next):
    nb, ta, d = x1.shape
    nt = ta // TT
    per = MOE_TILE // TT
    wide = pl.BlockSpec((1, TT, d), lambda b, t: (b, t, 0))
    row = pl.BlockSpec((1, d), lambda b, t: (0, 0))
    in_specs = [wide, pl.BlockSpec((TT, MOE_SORTED), lambda b, t: (b * nt + t, 0)),
                pl.BlockSpec((1, MOE_SORTED, d), lambda b, t: ((b * nt + t) // per, 0, 0)),
                _mod_spec(5, nct, nb), row, row]
    args = [x1, pt, ysb, mod3, ln_g.reshape(1, -1), ln_b.reshape(1, -1)]
    if mod3_next is None:
        out_shape = jax.ShapeDtypeStruct((nb, ta - nct * TT, d), F32)
        out_specs = pl.BlockSpec((1, TT, d), lambda b, t: (b, jnp.maximum(t - nct, 0), 0))
    else:
        in_specs += [_mod_spec(0, nct, nb), _mod_spec(1, nct, nb)]
        args += [mod3_next, mod3_next]
        out_shape = [jax.ShapeDtypeStruct((nb, ta, d), F32), jax.ShapeDtypeStruct((nb, ta, d), BF16)]
        out_specs = [wide, wide]
    return pl.pallas_call(
        functools.partial(_ln2_kernel, mod3_next is not None),
        out_shape=out_shape,
        grid=(nb, nt),
        in_specs=in_specs,
        out_specs=out_specs,
        compiler_params=_cparams(("parallel", "arbitrary"), VMEM_LIMIT),
        name="ln2",
    )(*args)


def _rope_tables(l, lc):
    half = HEAD_DIM // 2
    inv = ROPE_THETA ** (-np.arange(0, half, 2, dtype=np.float64) / half)
    t = np.arange(l)
    rows, cols = t // GRID_W, t % GRID_W
    ang = np.concatenate([rows[:, None] * inv, cols[:, None] * inv], -1)
    ang = np.concatenate([np.zeros((lc, half)), ang], 0)
    cos = np.repeat(np.cos(ang), 2, axis=1)
    sin = np.repeat(np.sin(ang), 2, axis=1)
    sin[:, 0::2] *= -1.0
    reps = Q_W // HEAD_DIM
    return (jnp.asarray(np.tile(cos, (1, reps)), F32), jnp.asarray(np.tile(sin, (1, reps)), F32))


def _block_diag_ones():
    i = np.arange(BRANCH_W) // HEAD_DIM
    return jnp.asarray((i[:, None] == i[None, :]).astype(np.float32))


def kernel(x, c, ctx, c_ctx, ada_w, ada_b, w_in, rwkv_mu, rwkv_w0, rwkv_w_up, rwkv_a0, rwkv_a_up, rwkv_g_up, rwkv_k_k, rwkv_k_a, rwkv_r_k, rwkv_lnx_g, rwkv_lnx_b, hyena_conv, hyena_w1, hyena_b1, hyena_freq1, hyena_w2, hyena_b2, hyena_freq2, hyena_w3, hyena_skip, sconv_w, attn_q_norm, attn_k_norm, w_branch, w_out, ln1_g, ln1_b, ln2_g, ln2_b, router_w, router_bias, exp_w1, exp_w3, exp_w2):
    nb, l, d = x.shape
    lc = ctx.shape[1]
    depth = ada_w.shape[0]
    assert d == D_MODEL and lc % TT == 0 and l % TT == 0 and l % GRID_W == 0 and (nb * (lc + l)) % MOE_TILE == 0
    ta = lc + l
    nct = lc // TT

    mod_rows = -(-(nb + 1) // SUBLANES) * SUBLANES
    cc = jnp.zeros((mod_rows, d), F32).at[:nb].set(c).at[nb].set(c_ctx)
    mod = ada_mod(cc, ada_w, ada_b)

    cos_t, sin_t = _rope_tables(l, lc)
    bd = _block_diag_ones()
    rwt = router_w.T

    ysb = None
    for li in range(depth):
        mod3 = mod[li].reshape(mod_rows, 1, N_MOD * d)
        wl = w_in[li].astype(BF16)
        if li == 0:
            x_all, h3 = lnmod(ctx, x, mod3, nct, 0, 1)
        h = h3.reshape(nb * ta, d)
        p_rwkv = matmul(h, wl[:, :OFF_HYENA], BF16).reshape(nb, ta, -1)
        p_hs = matmul(h, wl[:, OFF_HYENA:OFF_ATTN], BF16).reshape(nb, ta, -1)
        p_attn = matmul(h, wl[:, OFF_ATTN:OFF_GATE], BF16).reshape(nb, ta, -1)

        r, kk, w0, w1, k0, k1, b0, b1, v, g, bon = rwkv_prep(
            p_rwkv, nct, rwkv_mu[li], rwkv_w0[li], rwkv_w_up[li], rwkv_a0[li], rwkv_a_up[li], rwkv_g_up[li],
            rwkv_k_k[li], rwkv_k_a[li], rwkv_r_k[li], bd)
        yf, yb = rwkv_scan(r, kk, v, w0, w1, k0, k1, b0, b1, lc)

        q, kx, vx = attn_prep(p_attn, cos_t, sin_t, attn_q_norm[li], attn_k_norm[li], bd)
        yd = attention(q, kx, vx, lc)

        x0, u, ycv = hs_pre(p_hs, nct, hyena_conv[li], sconv_w[li])
        fargs = (hyena_w1[li], hyena_b1[li], hyena_freq1[li], hyena_w2[li], hyena_b2[li], hyena_freq2[li],
                 hyena_w3[li])
        yconv_ctx = hyena_conv_seg(u[:, :lc], fargs) if li < depth - 1 else jnp.zeros((nb, lc, BRANCH_W), F32)
        yconv = jnp.concatenate([yconv_ctx, hyena_conv_seg(u[:, lc:], fargs)], axis=1)

        x1, hf, gates_t = merge(yf, yb, g, bon, x0, u, yconv, ycv, yd, h3, x_all, mod3, nct, hyena_skip[li],
                                wl[:, OFF_GATE:], w_branch[li].astype(BF16), w_out[li].astype(BF16), ln1_g[li],
                                ln1_b[li], rwt, router_bias, rwkv_lnx_g[li], rwkv_lnx_b[li], bd)
        gates = jnp.pad(gates_t.T, ((0, 0), (0, LANES - ROUTE_ROWS)))
        hs, gs, pt, meta = moe_sort(hf.reshape(nb * ta, d), gates_t, gates)
        ysb = moe_group(hs, gs, meta[:, 0, :META_LANES].reshape(-1), exp_w1[li].astype(BF16),
                        exp_w3[li].astype(BF16), exp_w2[li].astype(BF16), carried=ysb)
        if li == depth - 1:
            return ln2(x1, pt, ysb, mod3, nct, ln2_g[li], ln2_b[li], None)
        x_all, h3 = ln2(x1, pt, ysb, mod3, nct, ln2_g[li], ln2_b[li], mod[li + 1].reshape(mod_rows, 1, N_MOD * d))


def hyena_conv_seg(u_seg, fargs):
    ktab = hyena_filter_table(u_seg.shape[1], *fargs)
    return hyena_conv(u_seg, ktab)
```

```python
import functools
import math

import numpy as np
import jax
import jax.numpy as jnp
from jax import lax
from jax.experimental import pallas as pl
from jax.experimental.pallas import tpu as pltpu

F32 = jnp.float32
BF16 = jnp.bfloat16
HI = lax.Precision.HIGHEST

D_MODEL = 1024
GRID_W = 64
BRANCH_W = 256
HEAD_DIM = 64
N_BRANCHES = 4
N_MOD = 6
RWKV_COLS = 1024
RWKV_GN_EPS = 64e-5
HYENA_COLS = 768
HYENA_EMB = 33
HYENA_FAST_DECAY = 0.3
HYENA_SLOW_DECAY = 1.5
HYENA_TARGET = 1e-2
SCONV_COLS = 768
Q_W = 256
KV_W = 128
ATTN_COLS = 512
ROPE_THETA = 10000.0
RMS_EPS = 1e-6
OFF_HYENA = RWKV_COLS
OFF_SCONV = OFF_HYENA + HYENA_COLS
OFF_ATTN = OFF_SCONV + SCONV_COLS
OFF_GATE = OFF_ATTN + ATTN_COLS
N_EXPERTS = 16
N_GROUPS = 4
EXPERTS_PER_GROUP = 4
LN_EPS = 1e-6

SUBLANES = 8
LANES = 128
TT = 256
SCAN_BLK = LANES
HY_BLK = 256
VMEM_LIMIT = 56 * 1024 * 1024


def _cparams(sem, vmem=None):
    return pltpu.CompilerParams(dimension_semantics=sem, vmem_limit_bytes=vmem)


def _ln(xf):
    mu = jnp.mean(xf, -1, keepdims=True)
    xc = xf - mu
    var = jnp.mean(xc * xc, -1, keepdims=True)
    return xc * lax.rsqrt(var + LN_EPS)


def _head_sums(x, ones_bd):
    ones16 = ones_bd.astype(BF16)
    hi = x.astype(BF16)
    lo = (x - hi.astype(F32)).astype(BF16)
    return jnp.dot(hi, ones16, preferred_element_type=F32) + jnp.dot(lo, ones16, preferred_element_type=F32)


def _pick_tile(n, cands):
    for c in cands:
        if n % c == 0:
            return c
    raise ValueError(f"no tile for {n}")


def _ada_kernel(c_ref, w_ref, b_ref, o_ref):
    c = c_ref[...]
    a = c * jax.nn.sigmoid(c)
    o_ref[0] = jnp.dot(a, w_ref[0], precision=HI, preferred_element_type=F32) + b_ref[0]


def ada_mod(cc, ada_w, ada_b):
    depth, d, n = ada_w.shape
    rows = cc.shape[0]
    return pl.pallas_call(
        _ada_kernel,
        out_shape=jax.ShapeDtypeStruct((depth, rows, n), F32),
        grid=(depth, n // d),
        in_specs=[pl.BlockSpec((rows, d), lambda l, j: (0, 0)),
                  pl.BlockSpec((1, d, d), lambda l, j: (l, 0, j)),
                  pl.BlockSpec((1, 1, d), lambda l, j: (l, 0, j))],
        out_specs=pl.BlockSpec((1, rows, d), lambda l, j: (l, 0, j)),
        compiler_params=_cparams(("parallel", "parallel"), VMEM_LIMIT),
        name="ada_mod",
    )(cc, ada_w, ada_b.reshape(depth, 1, n))


def _lnmod_kernel(nct, c_ref, x_ref, sh_ref, sc_ref, xa_o, h_o):
    t = pl.program_id(1)
    xin = jnp.where(t < nct, c_ref[0], x_ref[0])
    xa_o[0] = xin
    h_o[0] = (_ln(xin) * (1.0 + sc_ref[0]) + sh_ref[0]).astype(BF16)


def _mod_spec(col, nct, nb):
    return pl.BlockSpec((1, 1, D_MODEL), lambda b, t: (jnp.where(t < nct, nb, b), 0, col))


def lnmod(ctx, x, mod3, nct, col_shift, col_scale):
    nb, l, d = x.shape
    ta = ctx.shape[1] + l
    wide = pl.BlockSpec((1, TT, d), lambda b, t: (b, t, 0))
    return pl.pallas_call(
        functools.partial(_lnmod_kernel, nct),
        out_shape=[jax.ShapeDtypeStruct((nb, ta, d), F32), jax.ShapeDtypeStruct((nb, ta, d), BF16)],
        grid=(nb, ta // TT),
        in_specs=[pl.BlockSpec((1, TT, d), lambda b, t: (b, jnp.minimum(t, nct - 1), 0)),
                  pl.BlockSpec((1, TT, d), lambda b, t: (b, jnp.maximum(t - nct, 0), 0)),
                  _mod_spec(col_shift, nct, nb), _mod_spec(col_scale, nct, nb)],
        out_specs=[wide, wide],
        compiler_params=_cparams(("parallel", "arbitrary")),
        name="lnmod",
    )(ctx, x, mod3, mod3)


def _mm_kernel(a_ref, b_ref, o_ref):
    o_ref[...] = jnp.dot(a_ref[...], b_ref[...], preferred_element_type=F32).astype(o_ref.dtype)


def matmul(a, b, out_dtype=F32):
    m, k = a.shape
    _, n = b.shape
    tm = _pick_tile(m, (1024, 512, 256))
    tn = _pick_tile(n, (1024, 512, 256))
    return pl.pallas_call(
        _mm_kernel,
        out_shape=jax.ShapeDtypeStruct((m, n), out_dtype),
        grid=(m // tm, n // tn),
        in_specs=[pl.BlockSpec((tm, k), lambda i, j: (i, 0)),
                  pl.BlockSpec((k, tn), lambda i, j: (0, j))],
        out_specs=pl.BlockSpec((tm, tn), lambda i, j: (i, j)),
        compiler_params=_cparams(("parallel", "parallel"), VMEM_LIMIT),
        name="matmul",
    )(a, b)


def _halo_specs(width, ta, halo=SUBLANES):
    nblk = ta // halo
    per = TT // halo
    cur = pl.BlockSpec((1, TT, width), lambda b, t: (b, t, 0))
    prev = pl.BlockSpec((1, halo, width), lambda b, t: (b, jnp.maximum(t * per - 1, 0), 0))
    nxt = pl.BlockSpec((1, halo, width), lambda b, t: (b, jnp.minimum((t + 1) * per, nblk - 1), 0))
    return [cur, prev, nxt]


def _neighbours(cur, prev_ref, next_ref, nct, nt, t=None, b=0):
    t = pl.program_id(1) if t is None else t
    seg_start = jnp.logical_or(t == 0, t == nct)
    seg_end = jnp.logical_or(t == nct - 1, t == nt - 1)
    halo = prev_ref.shape[1]
    prev_row = prev_ref[b][halo - 1:halo, :].astype(F32) * jnp.where(seg_start, 0.0, 1.0)
    next_row = next_ref[b][0:1, :].astype(F32) * jnp.where(seg_end, 0.0, 1.0)
    row = lax.broadcasted_iota(jnp.int32, (TT, 1), 0)
    xm1 = jnp.where(row == 0, prev_row, pltpu.roll(cur, 1, axis=0))
    xp1 = jnp.where(row == TT - 1, next_row, pltpu.roll(cur, TT - 1, axis=0))
    return xm1, xp1


def _rwkv_prep_kernel(nct, nt, cur_ref, prev_ref, next_ref, mu_ref, w0_ref, wup_ref, a0_ref, aup_ref,
                      gup_ref, kk_ref, ka_ref, rk_ref, bd_ref,
                      r_o, kk_o, w0_o, w1_o, k0_o, k1_o, b0_o, b1_o, v_o, g_o, bon_o):
    cur = cur_ref[0].astype(F32)
    xm1, xp1 = _neighbours(cur, prev_ref, next_ref, nct, nt)
    p = cur + mu_ref[...] * (0.5 * (xm1 + xp1) - cur)
    c = BRANCH_W
    r, k, v = p[:, 0:c], p[:, c:2 * c], p[:, 2 * c:3 * c]
    wd = p[:, 3 * c:3 * c + 64]
    ad = p[:, 3 * c + 64:3 * c + 128]
    gd = p[:, 3 * c + 128:3 * c + 256]
    bd = bd_ref[...]
    kk = k * kk_ref[...]
    ss = _head_sums(kk * kk, bd)
    kkn = kk * lax.rsqrt(jnp.maximum(ss, 1e-24))
    twd = jnp.tanh(wd)
    ka = ka_ref[...]
    kdirs = []
    w_outs, k_outs, b_outs = (w0_o, w1_o), (k0_o, k1_o), (b0_o, b1_o)
    for d in range(2):
        wlog = w0_ref[d:d + 1, :] + jnp.dot(twd, wup_ref[d], precision=HI, preferred_element_type=F32)
        decay = -math.exp(-0.5) * jax.nn.sigmoid(wlog)
        a = jax.nn.sigmoid(a0_ref[d:d + 1, :] + jnp.dot(ad, aup_ref[d], precision=HI, preferred_element_type=F32))
        kdir = k * (1.0 + (a - 1.0) * ka)
        bdir = kkn * a
        kdirs.append(kdir)
        w_outs[d][0] = decay
        k_outs[d][0] = kdir.astype(BF16)
        b_outs[d][0] = bdir.astype(BF16)
    r_o[0] = r.astype(BF16)
    kk_o[0] = kkn.astype(BF16)
    v_o[0] = v.astype(BF16)
    g_o[0] = jnp.dot(jax.nn.sigmoid(gd), gup_ref[...], precision=HI, preferred_element_type=F32)
    rkk = r * rk_ref[...] * (kdirs[0] + kdirs[1])
    bon_o[0] = _head_sums(rkk, bd) * v


def rwkv_prep(p_rwkv, nct, mu, w0, w_up, a0, a_up, g_up, k_k, k_a, r_k, bd):
    nb, ta, _ = p_rwkv.shape
    nt = ta // TT
    c = BRANCH_W
    nat = jax.ShapeDtypeStruct((nb, ta, c), F32)
    lo = jax.ShapeDtypeStruct((nb, ta, c), BF16)
    nat_spec = pl.BlockSpec((1, TT, c), lambda b, t: (b, t, 0))

    def full(a):
        nd = a.ndim
        return pl.BlockSpec(a.shape, lambda b, t: (0,) * nd)

    consts = [mu.reshape(1, -1), w0, w_up, a0, a_up, g_up, k_k.reshape(1, -1), k_a.reshape(1, -1),
              r_k.reshape(1, -1), bd]
    return pl.pallas_call(
        functools.partial(_rwkv_prep_kernel, nct, nt),
        out_shape=[lo, lo, nat, nat, lo, lo, lo, lo, lo, nat, nat],
        grid=(nb, nt),
        in_specs=_halo_specs(RWKV_COLS, ta, 2 * SUBLANES) + [full(a) for a in consts],
        out_specs=[nat_spec] * 11,
        compiler_params=_cparams(("parallel", "parallel"), VMEM_LIMIT),
        name="rwkv_prep",
    )(p_rwkv, p_rwkv, p_rwkv, *consts)


CHUNK = 32
PREP_NB = 4


def _chunk_scan_rows(x, reverse):
    pos = lax.broadcasted_iota(jnp.int32, x.shape, 0) % CHUNK
    step = 1
    while step < CHUNK:
        if reverse:
            x = x + jnp.where(pos < CHUNK - step, pltpu.roll(x, x.shape[0] - step, axis=0), 0.0)
        else:
            x = x + jnp.where(pos >= step, pltpu.roll(x, step, axis=0), 0.0)
        step *= 2
    return x


def _chunk_prep_kernel(r_ref, kk_ref, v_ref, lw0, k0, b0, lw1, k1, b1,
                       a0_o, bm0_o, rp0_o, y00_o, a1_o, bm1_o, rp1_o, y01_o):
    blk = SCAN_BLK
    npair = BRANCH_W // LANES
    nchunk = blk // CHUNK
    ti = lax.broadcasted_iota(jnp.int32, (blk, blk), 0)
    si = lax.broadcasted_iota(jnp.int32, (blk, blk), 1)
    same = (ti // CHUNK) == (si // CHUNK)
    eye = ti == si
    bd64 = (ti // HEAD_DIM) == (si // HEAD_DIM)
    head0 = si < HEAD_DIM
    lane_half = lax.broadcasted_iota(jnp.int32, (HEAD_DIM, LANES), 1)
    eyef = jnp.where(eye, 1.0, 0.0)
    dot = lambda x, y: jnp.dot(x, y, preferred_element_type=F32)
    lo = lambda x: x.astype(BF16)
    split = lambda x: jnp.concatenate([jnp.where(head0, x, 0.0), jnp.where(head0, 0.0, x)], axis=0)
    cat = lambda ms: lo(jnp.concatenate(ms, axis=1))

    probs = []
    for bi, d in [(bi, d) for bi in range(PREP_NB) for d in range(2)]:
        lw_ref, k_ref, b_ref = ((lw0, k0, b0), (lw1, k1, b1))[d]
        reverse = d == 1
        r, kk, v = (a[bi].astype(F32) for a in (r_ref, kk_ref, v_ref))
        lw, k, b = lw_ref[bi], k_ref[bi].astype(F32), b_ref[bi].astype(F32)
        lg = _chunk_scan_rows(lw, reverse)
        lg_end = lg + _chunk_scan_rows(lw, not reverse) - lw
        g, gi, g_end = jnp.exp(lg), jnp.exp(-lg), jnp.exp(lg_end)
        to_end = jnp.exp(lg_end - lg)
        arrs = (kk * jnp.exp(lg - lw), b * gi, k * gi, r * g, v, k * to_end, b * to_end, g_end)
        incl = jnp.logical_and(same, si >= ti if reverse else si <= ti)
        strict = jnp.logical_and(same, si > ti if reverse else si < ti)
        for p in range(npair):
            lanes = slice(p * LANES, (p + 1) * LANES)
            probs.append(dict(bi=bi, d=d, p=p, incl=incl, strict=strict, arrs=tuple(a[:, lanes] for a in arrs)))

    for q in probs:
        pp_, q_, kt_, rt_ = q["arrs"][:4]
        rhs_g = lo(jnp.concatenate([q_, kt_], axis=0))
        lm, mm, n2, nn = [], [], [], []
        for hh in range(2):
            hm = head0 if hh == 0 else jnp.logical_not(head0)
            lhs_g = lo(jnp.concatenate([jnp.where(hm, pp_, 0.0), jnp.where(hm, rt_, 0.0)], axis=0))
            gm = lax.dot_general(lhs_g, rhs_g, (((1,), (1,)), ((), ())), preferred_element_type=F32)
            lm.append(jnp.where(q["strict"], gm[:blk, :blk], 0.0))
            mm.append(jnp.where(q["strict"], gm[:blk, blk:], 0.0))
            n2.append(jnp.where(q["incl"], gm[blk:, :blk], 0.0))
            nn.append(jnp.where(q["incl"], gm[blk:, blk:], 0.0))
        q["pw"], q["tm"] = lm, [eyef - lm[0], eyef - lm[1]]
        q["m_cat"], q["n2_cat"], q["nn_cat"] = cat(mm), cat(n2), cat(nn)
    for _ in range(CHUNK.bit_length() - 2):
        for q in probs:
            pwl = [lo(x) for x in q["pw"]]
            q["pw"] = [dot(x, x) for x in pwl]
        for q in probs:
            q["tm"] = [dot(lo(t), lo(eyef + x)) for t, x in zip(q["tm"], q["pw"])]
    for q in probs:
        q["t_cat"] = cat(q["tm"])
        q["v_st"] = lo(split(q["arrs"][4]))
        q["pp"] = dot(q["t_cat"], lo(split(q["arrs"][0])))
        q["mv"] = dot(q["m_cat"], q["v_st"])
    for q in probs:
        q["w2"] = dot(q["t_cat"], lo(split(q["mv"])))
        q["rp"] = q["arrs"][3] - dot(q["n2_cat"], lo(split(q["pp"])))
    for q in probs:
        q["y0"] = dot(q["nn_cat"], q["v_st"]) - dot(q["n2_cat"], lo(split(q["w2"])))
    in_chunk = [si // CHUNK == c for c in range(nchunk)]
    for q in probs:
        ppt, vt, w2t = q["pp"].T, q["arrs"][4].T, q["w2"].T
        kg_, qg_ = q["arrs"][5], q["arrs"][6]
        lhs_a = jnp.concatenate([jnp.where(cm, ppt, 0.0) for cm in in_chunk], axis=0)
        q["pq"] = dot(lo(lhs_a), lo(qg_))
        lhs_b = jnp.concatenate([jnp.concatenate([jnp.where(cm, vt, 0.0), jnp.where(cm, -w2t, 0.0)], axis=1)
                                 for cm in in_chunk], axis=0)
        q["bf"] = dot(lo(lhs_b), lo(jnp.concatenate([kg_, qg_], axis=0)))
    outs = ((a0_o, bm0_o, rp0_o, y00_o), (a1_o, bm1_o, rp1_o, y01_o))
    for q in probs:
        a_o, bm_o = outs[q["d"]][:2]
        g_end_p = q["arrs"][7]
        for c in range(nchunk):
            pq_c = q["pq"][c * blk:(c + 1) * blk]
            a_o[q["bi"], 0, c, q["p"]] = (jnp.where(eye, g_end_p[c * CHUNK:c * CHUNK + 1], 0.0)
                                          - jnp.where(bd64, pq_c, 0.0)).astype(BF16)
            bm_o[q["bi"], 0, c, q["p"]] = jnp.where(lane_half < HEAD_DIM, q["bf"][c * blk:c * blk + HEAD_DIM],
                                                    q["bf"][c * blk + HEAD_DIM:(c + 1) * blk]).astype(BF16)
    for bi in range(PREP_NB):
        for d in range(2):
            rp_o, y0_o = outs[d][2:]
            rp_o[bi] = jnp.concatenate([q["rp"] for q in probs if q["d"] == d and q["bi"] == bi], axis=1)
            y0_o[bi] = jnp.concatenate([q["y0"] for q in probs if q["d"] == d and q["bi"] == bi], axis=1)


def chunk_prep(r, kk, v, lw0, lw1, k0, k1, b0, b1):
    nb, ta, c = r.shape
    nblk = ta // SCAN_BLK
    npair = c // LANES
    nchunk = SCAN_BLK // CHUNK
    pb = PREP_NB
    assert nb % pb == 0
    nat = pl.BlockSpec((pb, SCAN_BLK, c), lambda b, s: (b, s, 0))
    a_shape = jax.ShapeDtypeStruct((nb, nblk, nchunk, npair, LANES, LANES), BF16)
    bm_shape = jax.ShapeDtypeStruct((nb, nblk, nchunk, npair, HEAD_DIM, LANES), BF16)
    nat_shape = jax.ShapeDtypeStruct((nb, ta, c), F32)
    a_spec = pl.BlockSpec((pb, 1, nchunk, npair, LANES, LANES), lambda b, s: (b, s, 0, 0, 0, 0))
    bm_spec = pl.BlockSpec((pb, 1, nchunk, npair, HEAD_DIM, LANES), lambda b, s: (b, s, 0, 0, 0, 0))
    return pl.pallas_call(
        _chunk_prep_kernel,
        out_shape=[a_shape, bm_shape, nat_shape, nat_shape] * 2,
        grid=(nb // pb, nblk),
        in_specs=[nat] * 9,
        out_specs=[a_spec, bm_spec, nat, nat] * 2,
        compiler_params=_cparams(("parallel", "parallel"), VMEM_LIMIT),
        name="chunk_prep",
    )(r, kk, v, lw0, k0, b0, lw1, k1, b1)


def _chunk_scan_kernel(nb, a0, bm0, rp0, y00, a1, bm1, rp1, y01, yf_o, yb_o, s_scr):
    step = pl.program_id(0)
    npair = BRANCH_W // LANES
    nchunk = SCAN_BLK // CHUNK

    @pl.when(step == 0)
    def _():
        s_scr[...] = jnp.zeros_like(s_scr)

    lane = lax.broadcasted_iota(jnp.int32, (CHUNK, LANES), 1)
    refs = ((a0, bm0, rp0, y00, yf_o), (a1, bm1, rp1, y01, yb_o))
    for ci in range(nchunk):
        for d in range(2):
            a_ref, bm_ref, rp_ref, y0_ref, y_ref = refs[d]
            c = ci if d == 0 else nchunk - 1 - ci
            rows = slice(c * CHUNK, (c + 1) * CHUNK)
            for b in range(nb):
                for p in range(npair):
                    lanes = slice(p * LANES, (p + 1) * LANES)
                    s = s_scr[d, b, p]
                    rpc = rp_ref[b, rows, lanes]
                    lhs = jnp.concatenate([jnp.where(lane < HEAD_DIM, rpc, 0.0), jnp.where(lane >= HEAD_DIM, rpc, 0.0)],
                                          axis=0)
                    yh = lax.dot_general(lhs, s, (((1,), (1,)), ((), ())), preferred_element_type=F32)
                    y_ref[b, rows, lanes] = jnp.concatenate([yh[:CHUNK], yh[CHUNK:]], axis=1) + y0_ref[b, rows, lanes]
                    s_scr[d, b, p] = (jnp.dot(s.astype(BF16), a_ref[b, 0, c, p], preferred_element_type=F32)
                                      + bm_ref[b, 0, c, p])


def rwkv_scan(r, kk, v, lw0, lw1, k0, k1, b0, b1, lc):
    nb, ta, c = r.shape
    nblk = ta // SCAN_BLK
    nctb = lc // SCAN_BLK
    npair = c // LANES
    nchunk = SCAN_BLK // CHUNK
    a0, bm0, rp0, y00, a1, bm1, rp1, y01 = chunk_prep(r, kk, v, lw0, lw1, k0, k1, b0, b1)

    def fwd(s):
        return s

    def bwd(s):
        return jnp.where(s < nctb, nctb - 1 - s, nblk - 1 - (s - nctb))

    def specs(idx):
        return [pl.BlockSpec((nb, 1, nchunk, npair, LANES, LANES), lambda s: (0, idx(s), 0, 0, 0, 0)),
                pl.BlockSpec((nb, 1, nchunk, npair, HEAD_DIM, LANES), lambda s: (0, idx(s), 0, 0, 0, 0)),
                pl.BlockSpec((nb, SCAN_BLK, c), lambda s: (0, idx(s), 0)),
                pl.BlockSpec((nb, SCAN_BLK, c), lambda s: (0, idx(s), 0))]

    out = jax.ShapeDtypeStruct((nb, ta, c), F32)
    return pl.pallas_call(
        functools.partial(_chunk_scan_kernel, nb),
        out_shape=[out, out],
        grid=(nblk,),
        in_specs=specs(fwd) + specs(bwd),
        out_specs=[pl.BlockSpec((nb, SCAN_BLK, c), lambda s: (0, fwd(s), 0)),
                   pl.BlockSpec((nb, SCAN_BLK, c), lambda s: (0, bwd(s), 0))],
        scratch_shapes=[pltpu.VMEM((2, nb, npair, HEAD_DIM, LANES), F32)],
        compiler_params=_cparams(("arbitrary",), VMEM_LIMIT),
        name="chunk_scan",
    )(a0, bm0, rp0, y00, a1, bm1, rp1, y01)


def _pair_swap(x):
    lane = lax.broadcasted_iota(jnp.int32, x.shape, 1)
    n = x.shape[1]
    return jnp.where(lane % 2 == 0, pltpu.roll(x, n - 1, axis=1), pltpu.roll(x, 1, axis=1))


def _attn_prep_kernel(p_ref, cos_ref, sin_ref, qg_ref, kg_ref, bd_ref, q_o, k_o, v_o):
    p = p_ref[0].astype(F32)
    q, k, v = p[:, :Q_W], p[:, Q_W:Q_W + KV_W], p[:, Q_W + KV_W:]
    bd = bd_ref[...]
    cos, sin = cos_ref[...], sin_ref[...]
    qms = _head_sums(q * q, bd) * (1.0 / HEAD_DIM)
    qn = q * lax.rsqrt(qms + RMS_EPS) * qg_ref[...]
    qr = qn * cos + _pair_swap(qn) * sin
    q_o[0] = (qr * HEAD_DIM ** -0.5).astype(BF16)
    kms = _head_sums(k * k, bd[:KV_W, :KV_W]) * (1.0 / HEAD_DIM)
    kn = k * lax.rsqrt(kms + RMS_EPS) * kg_ref[...]
    kr = kn * cos[:, :KV_W] + _pair_swap(kn) * sin[:, :KV_W]
    for g in range(KV_W // HEAD_DIM):
        sl = slice(g * HEAD_DIM, (g + 1) * HEAD_DIM)
        k_o[0, g] = kr[:, sl].astype(BF16)
        v_o[0, g] = v[:, sl].astype(BF16)


def attn_prep(p_attn, cos_t, sin_t, q_norm, k_norm, bd):
    nb, ta, _ = p_attn.shape
    ng = KV_W // HEAD_DIM
    qg = jnp.tile(q_norm, Q_W // HEAD_DIM).reshape(1, -1)
    kg = jnp.tile(k_norm, ng).reshape(1, -1)
    kv_shape = jax.ShapeDtypeStruct((nb, ng, ta, HEAD_DIM), BF16)
    kv_spec = pl.BlockSpec((1, ng, TT, HEAD_DIM), lambda b, t: (b, 0, t, 0))
    return pl.pallas_call(
        _attn_prep_kernel,
        out_shape=[jax.ShapeDtypeStruct((nb, ta, Q_W), BF16), kv_shape, kv_shape],
        grid=(nb, ta // TT),
        in_specs=[pl.BlockSpec((1, TT, ATTN_COLS), lambda b, t: (b, t, 0)),
                  pl.BlockSpec((TT, Q_W), lambda b, t: (t, 0)),
                  pl.BlockSpec((TT, Q_W), lambda b, t: (t, 0)),
                  pl.BlockSpec((1, Q_W), lambda b, t: (0, 0)),
                  pl.BlockSpec((1, KV_W), lambda b, t: (0, 0)),
                  pl.BlockSpec(bd.shape, lambda b, t: (0, 0))],
        out_specs=[pl.BlockSpec((1, TT, Q_W), lambda b, t: (b, t, 0)), kv_spec, kv_spec],
        compiler_params=_cparams(("parallel", "parallel")),
        name="attn_prep",
    )(p_attn, cos_t, sin_t, qg, kg, bd)


def _attn_kernel(nct, lc, q_ref, k_ref, v_ref, o_ref):
    t = pl.program_id(1)
    ng = k_ref.shape[1]
    rep = Q_W // HEAD_DIM // ng

    def run(nk):
        outs = []
        for g in range(ng):
            kk, vv = k_ref[0, g, :nk, :], v_ref[0, g, :nk, :]
            for r in range(rep):
                h = g * rep + r
                q = q_ref[0][:, h * HEAD_DIM:(h + 1) * HEAD_DIM]
                s = lax.dot_general(q, kk, (((1,), (1,)), ((), ())), preferred_element_type=F32)
                p = jnp.exp(s - jnp.max(s, -1, keepdims=True))
                l = jnp.sum(p, -1, keepdims=True)
                o = jnp.dot(p.astype(BF16), vv, preferred_element_type=F32)
                outs.append(o / l)
        o_ref[0] = jnp.concatenate(outs, axis=1).astype(BF16)

    @pl.when(t < nct)
    def _():
        run(lc)

    @pl.when(t >= nct)
    def _():
        run(k_ref.shape[2])


def attention(q, k, v, lc):
    nb, ta, _ = q.shape
    ng = k.shape[1]
    nct = lc // TT
    qo_spec = pl.BlockSpec((1, TT, Q_W), lambda b, t: (b, t, 0))
    kv_spec = pl.BlockSpec((1, ng, ta, HEAD_DIM), lambda b, t: (b, 0, 0, 0))
    return pl.pallas_call(
        functools.partial(_attn_kernel, nct, lc),
        out_shape=jax.ShapeDtypeStruct((nb, ta, Q_W), BF16),
        grid=(nb, ta // TT),
        in_specs=[qo_spec, kv_spec, kv_spec],
        out_specs=qo_spec,
        compiler_params=_cparams(("parallel", "arbitrary"), VMEM_LIMIT),
        name="attention",
    )(q, k, v)


def _hs_pre_kernel(nct, nt, cur_ref, prev_ref, next_ref, hw_ref, sw_ref, x0_o, u_o, ycv_o, ut_o):
    t = pl.program_id(0)
    c = BRANCH_W
    hc = HYENA_COLS
    hw = hw_ref[...]
    sw = sw_ref[...]

    def cx(a):
        return a[:, hc + c:hc + 2 * c] * a[:, hc + 2 * c:hc + 3 * c]

    for b in range(cur_ref.shape[0]):
        cur = cur_ref[b].astype(F32)
        xm1, xp1 = _neighbours(cur, prev_ref, next_ref, nct, nt, t, b)
        ph = hw[0:1] * xm1[:, :hc] + hw[1:2] * cur[:, :hc] + hw[2:3] * xp1[:, :hc]
        x0_o[b] = ph[:, :c].astype(BF16)
        u = ph[:, c:2 * c] * ph[:, 2 * c:3 * c]
        u_o[b] = u.astype(BF16)
        ut_o[:, 0, b, :] = u.T
        conv = sw[0:1] * cx(xm1) + sw[1:2] * cx(cur) + sw[2:3] * cx(xp1)
        ycv_o[b] = (cur[:, hc:hc + c] * conv).astype(BF16)


def hs_pre(p_hs, nct, hyena_conv, sconv_w):
    nb, ta, w = p_hs.shape
    assert nb % SUBLANES == 0
    nt = ta // TT
    halo = 2 * SUBLANES
    per = TT // halo
    nat = pl.BlockSpec((nb, TT, BRANCH_W), lambda t: (0, t, 0))
    ut_spec = pl.BlockSpec((BRANCH_W, 1, nb, TT), lambda t: (0, jnp.where(t < nct, nt - nct + t, t - nct), 0, 0))
    return pl.pallas_call(
        functools.partial(_hs_pre_kernel, nct, nt),
        out_shape=[jax.ShapeDtypeStruct((nb, ta, BRANCH_W), BF16)] * 3
                  + [jax.ShapeDtypeStruct((BRANCH_W, nt, nb, TT), F32)],
        grid=(nt,),
        in_specs=[pl.BlockSpec((nb, TT, w), lambda t: (0, t, 0)),
                  pl.BlockSpec((nb, halo, w), lambda t: (0, jnp.maximum(t * per - 1, 0), 0)),
                  pl.BlockSpec((nb, halo, w), lambda t: (0, jnp.minimum((t + 1) * per, ta // halo - 1), 0)),
                  pl.BlockSpec(hyena_conv.shape, lambda t: (0, 0)),
                  pl.BlockSpec(sconv_w.shape, lambda t: (0, 0))],
        out_specs=[nat, nat, nat, ut_spec],
        compiler_params=_cparams(("parallel",), VMEM_LIMIT),
        name="hs_pre",
    )(p_hs, p_hs, p_hs, hyena_conv, sconv_w)


EMB_PAD = 40


def _filter_tables(lh):
    n = np.arange(2 * lh)
    pos = np.abs(n - (lh - 1)).astype(np.float64)
    bands = (HYENA_EMB - 1) // 2
    t = np.minimum(pos, lh - 1) / (lh - 1)
    wpos = 2.0 * math.pi * pos / lh
    f = np.linspace(1e-4, bands - 1, bands)[:, None]
    z = np.zeros((EMB_PAD, 2 * lh), np.float32)
    z[0] = t
    z[1:1 + bands] = np.cos(f * wpos[None, :])
    z[1 + bands:1 + 2 * bands] = -np.sin(f * wpos[None, :])
    max_decay = math.log(HYENA_TARGET) / HYENA_FAST_DECAY
    min_decay = math.log(HYENA_TARGET) / HYENA_SLOW_DECAY
    deltas = np.abs(np.linspace(min_decay, max_decay, BRANCH_W)).astype(np.float32)
    return z, deltas.reshape(-1, 1)


def _filter_kernel(lh, tn, z_ref, w1_ref, b1_ref, f1_ref, w2_ref, b2_ref, f2_ref, w3_ref, dl_ref, o_ref):
    z = z_ref[...]
    h1 = jnp.sin(f1_ref[...] * (jnp.dot(w1_ref[...], z, precision=HI, preferred_element_type=F32) + b1_ref[...]))
    h2 = jnp.sin(f2_ref[...] * (jnp.dot(w2_ref[...], h1, precision=HI, preferred_element_type=F32) + b2_ref[...]))
    f = jnp.dot(w3_ref[...], h2, precision=HI, preferred_element_type=F32)
    n = pl.program_id(0) * tn + lax.broadcasted_iota(jnp.int32, (1, tn), 1)
    filt = jnp.where(n >= lh - 1, f[:BRANCH_W], f[BRANCH_W:])
    win = jnp.exp(-z[0:1, :] * dl_ref[...])
    o_ref[...] = jnp.where(n == 2 * lh - 1, 0.0, filt * win)


def hyena_filter_table(lh, w1, b1, f1, w2, b2, f2, w3):
    z_np, dl_np = _filter_tables(lh)
    n2 = 2 * lh
    tn = _pick_tile(n2, (1024, 512))
    hd = w2.shape[0]
    w1t = jnp.zeros((hd, EMB_PAD), F32).at[:, :HYENA_EMB].set(w1.T)
    args = [jnp.asarray(z_np), w1t, b1.reshape(-1, 1), f1.reshape(-1, 1), w2.T, b2.reshape(-1, 1),
            f2.reshape(-1, 1), w3.T, jnp.asarray(dl_np)]

    def full(a):
        return pl.BlockSpec(a.shape, lambda j: (0, 0))

    return pl.pallas_call(
        functools.partial(_filter_kernel, lh, tn),
        out_shape=jax.ShapeDtypeStruct((BRANCH_W, n2), F32),
        grid=(n2 // tn,),
        in_specs=[pl.BlockSpec((EMB_PAD, tn), lambda j: (0, j))] + [full(a) for a in args[1:]],
        out_specs=pl.BlockSpec((BRANCH_W, tn), lambda j: (0, j)),
        compiler_params=_cparams(("parallel",)),
        name="hyena_filter",
    )(*args)


def _hyena_conv_kernel(nblk, bp, nch, k_ref, u_ref, o_ref, t_scr):
    ntile = 4 * nblk - 1
    mc = 2 * nblk - 1
    width = (ntile + 1) * LANES
    for ch in range(nch):
        big = pltpu.roll(jnp.broadcast_to(k_ref[ch], (LANES, width)), width - (LANES - 1), 1, stride=1, stride_axis=0)
        for m in range(ntile):
            t_scr[ch, m] = big[:, m * LANES:(m + 1) * LANES].astype(BF16)

    for ch in range(nch):
        for d in [0] + [s * a for a in range(1, nblk) for s in (1, -1)]:
            m0 = 2 * d + mc
            w = jnp.concatenate([jnp.concatenate([t_scr[ch, m0], t_scr[ch, m0 + 1]], axis=1),
                                 jnp.concatenate([t_scr[ch, m0 - 1], t_scr[ch, m0]], axis=1)], axis=0)
            i0, i1 = max(0, d), min(nblk, nblk + d)
            lhs = u_ref[ch, (i0 - d) * bp:(i1 - d) * bp, :].astype(BF16)
            res = jnp.dot(lhs, w, preferred_element_type=F32)
            if d == 0:
                o_ref[ch] = res
            else:
                o_ref[ch, i0 * bp:i1 * bp, :] += res


def hyena_conv(ut, ktab, nb, first_blk, nblk):
    c, _, bp, _ = ut.shape
    assert first_blk % nblk == 0
    nch = SUBLANES if nblk == 1 else 2
    ut = ut.reshape(c, -1, HY_BLK)
    k3 = ktab.reshape(c, 1, 4 * nblk * LANES)
    out = pl.pallas_call(
        functools.partial(_hyena_conv_kernel, nblk, bp, nch),
        out_shape=jax.ShapeDtypeStruct((c, nblk * bp, HY_BLK), F32),
        grid=(c // nch,),
        in_specs=[pl.BlockSpec((nch, 1, 4 * nblk * LANES), lambda ch: (ch, 0, 0)),
                  pl.BlockSpec((nch, nblk * bp, HY_BLK), lambda ch: (ch, first_blk // nblk, 0))],
        out_specs=pl.BlockSpec((nch, nblk * bp, HY_BLK), lambda ch: (ch, 0, 0)),
        scratch_shapes=[pltpu.VMEM((nch, 4 * nblk - 1, LANES, LANES), BF16)],
        compiler_params=_cparams(("parallel",)),
        name="hyena_conv",
    )(k3, ut)
    out = out.reshape(c, nblk, bp, HY_BLK)[:, :, :nb]
    return jnp.transpose(out, (2, 1, 3, 0)).reshape(nb, nblk * HY_BLK, c)


def _route(logits, bias):
    s = jax.nn.sigmoid(logits)
    sel = s + bias
    srow = [s[e:e + 1] for e in range(N_EXPERTS)]
    row = [sel[e:e + 1] for e in range(N_EXPERTS)]
    best, gi = None, None
    for g in range(N_GROUPS):
        a, b, c, d = row[4 * g:4 * g + 4]
        hi1, lo1, hi2, lo2 = jnp.maximum(a, b), jnp.minimum(a, b), jnp.maximum(c, d), jnp.minimum(c, d)
        score = jnp.maximum(hi1, hi2) + jnp.maximum(jnp.minimum(hi1, hi2), jnp.maximum(lo1, lo2))
        if g == 0:
            best, gi = score, jnp.zeros(score.shape, jnp.int32)
        else:
            better = score > best
            gi = jnp.where(better, g, gi)
            best = jnp.where(better, score, best)
    neg = -jnp.inf
    msel = [jnp.where(gi == e // EXPERTS_PER_GROUP, row[e], neg) for e in range(N_EXPERTS)]

    def arg_first_max(vals):
        bv, bi = vals[0], jnp.zeros(vals[0].shape, jnp.int32)
        for e in range(1, N_EXPERTS):
            better = vals[e] > bv
            bi = jnp.where(better, e, bi)
            bv = jnp.where(better, vals[e], bv)
        return bi

    i1 = arg_first_max(msel)
    i2 = arg_first_max([jnp.where(i1 == e, neg, msel[e]) for e in range(N_EXPERTS)])
    w1 = sum(jnp.where(i1 == e, srow[e], 0.0) for e in range(N_EXPERTS))
    w2 = sum(jnp.where(i2 == e, srow[e], 0.0) for e in range(N_EXPERTS))
    den = w1 + w2
    g1, g2 = w1 / den, w2 / den
    rows = [jnp.where(i1 == e, g1, 0.0) + jnp.where(i2 == e, g2, 0.0) for e in range(N_EXPERTS)]
    rows.append(gi.astype(F32))
    rows.extend([jnp.zeros_like(g1)] * (ROUTE_ROWS - len(rows)))
    return jnp.concatenate(rows, axis=0)


MERGE_NB = 2


def _merge_kernel(yf_ref, yr_ref, rg_ref, bon_ref, x0_ref, u_ref, yc_ref, ycv_ref, yd_ref, h_ref, x_ref, *rest):
    mods, rest = rest[:3 * MERGE_NB], rest[3 * MERGE_NB:]
    skip_ref, wg_ref, wb_ref, wo_ref, g1_ref, b1_ref, rwt_ref, rb_ref, lxg_ref, lxb_ref, bd_ref = rest[:11]
    x1_o, hf_o = rest[11:13]
    gates_o = rest[13:]
    rows = MERGE_NB * TT
    flat = lambda ref: ref[...].reshape(rows, ref.shape[-1])
    y = flat(yf_ref) + flat(yr_ref)
    bd = bd_ref[...]
    yc = y - _head_sums(y, bd) * (1.0 / HEAD_DIM)
    var = _head_sums(yc * yc, bd) * (1.0 / HEAD_DIM)
    ya = ((yc * lax.rsqrt(var + RWKV_GN_EPS) * lxg_ref[...] + lxb_ref[...] + flat(bon_ref)) * flat(rg_ref)).astype(BF16)
    yb = (flat(x0_ref) * (flat(yc_ref) + flat(u_ref) * skip_ref[...])).astype(BF16)
    ys = (ya, yb, flat(ycv_ref), flat(yd_ref))
    h = flat(h_ref)
    merged = None
    for n in range(N_BRANCHES):
        gate = jax.nn.sigmoid(jnp.dot(h, wg_ref[:, n * D_MODEL:(n + 1) * D_MODEL], preferred_element_type=F32))
        term = gate * jnp.dot(ys[n], wb_ref[n], preferred_element_type=F32)
        merged = term if merged is None else merged + term
    out = jnp.dot(merged.astype(BF16), wo_ref[...], preferred_element_type=F32)
    alpha = (2 * 2) ** 0.25
    for k in range(MERGE_NB):
        ga_ref, shf_ref, scf_ref = mods[3 * k:3 * k + 3]
        x1 = _ln(alpha * x_ref[k] + ga_ref[0] * out[k * TT:(k + 1) * TT]) * g1_ref[...] + b1_ref[...]
        hf = _ln(x1) * (1.0 + scf_ref[0]) + shf_ref[0]
        x1_o[k] = x1
        hf_o[k] = hf.astype(BF16)
        logits = lax.dot_general(rwt_ref[...], hf, (((1,), (1,)), ((), ())), precision=HI,
                                 preferred_element_type=F32)
        gates_o[k][...] = _route(logits, rb_ref[...])


def merge(yf, yr, rg, bon, x0, u, yconv, ycv, yd, h, x_all, mod3, nct, skip, wg, wb, wo, ln_g, ln_b, rwt, rbias,
          lnx_g, lnx_b, bd):
    nb, ta, d = x_all.shape
    nt = ta // TT
    mb = MERGE_NB
    assert nb % mb == 0
    nat = pl.BlockSpec((mb, TT, BRANCH_W), lambda b, t: (b, t, 0))
    wide = pl.BlockSpec((mb, TT, d), lambda b, t: (b, t, 0))

    def full(a):
        nd = a.ndim
        return pl.BlockSpec(a.shape, lambda b, t: (0,) * nd)

    def mod_spec(col, k):
        return pl.BlockSpec((1, 1, D_MODEL), lambda b, t: (jnp.where(t < nct, nb, b * mb + k), 0, col))

    mod_specs = [mod_spec(col, k) for k in range(mb) for col in (2, 3, 4)]
    consts = [skip.reshape(1, -1), wg, wb, wo, ln_g.reshape(1, -1), ln_b.reshape(1, -1), rwt, rbias.reshape(-1, 1),
              lnx_g.reshape(1, -1), lnx_b.reshape(1, -1), bd]
    gate_shape = jax.ShapeDtypeStruct((ROUTE_ROWS, nb // mb * ta), F32)
    gate_spec = pl.BlockSpec((ROUTE_ROWS, TT), lambda b, t: (0, b * nt + t))
    outs = pl.pallas_call(
        _merge_kernel,
        out_shape=[jax.ShapeDtypeStruct((nb, ta, d), F32), jax.ShapeDtypeStruct((nb, ta, d), BF16)] + [gate_shape] * mb,
        grid=(nb // mb, nt),
        in_specs=[nat] * 9 + [wide, wide] + mod_specs + [full(a) for a in consts],
        out_specs=[wide, wide] + [gate_spec] * mb,
        compiler_params=_cparams(("parallel", "parallel"), VMEM_LIMIT),
        name="merge",
    )(yf, yr, rg, bon, x0, u, yconv, ycv, yd, h, x_all, *([mod3] * (3 * mb)), *consts)
    gates_t = jnp.stack([g.reshape(ROUTE_ROWS, nb // mb, ta) for g in outs[2:]], axis=2).reshape(ROUTE_ROWS, nb * ta)
    return outs[0], outs[1], gates_t


MOE_TILE = 1024
MOE_ALIGN = 2 * SUBLANES
MOE_CHUNK = 18 * MOE_ALIGN
MOE_SORTED = MOE_TILE + LANES
MOE_ROWS = MOE_SORTED + MOE_CHUNK
GID_ROW = N_EXPERTS
ROUTE_ROWS = 3 * SUBLANES
META_LANES = 2 * N_GROUPS
assert N_GROUPS * (MOE_ALIGN - 1) <= MOE_SORTED - MOE_TILE and MOE_ROWS % MOE_ALIGN == 0


def _moe_sort_kernel(gt_ref, g_ref, h_ref, up_ref, hs_o, gs_o, pt_o, meta_o):
    gid = gt_ref[GID_ROW:GID_ROW + 1, :]
    onehot = [jnp.where(gid == float(g), 1.0, 0.0) for g in range(N_GROUPS)]
    g4 = jnp.concatenate(onehot + [jnp.zeros((SUBLANES - N_GROUPS, MOE_TILE), F32)], axis=0)
    before = jnp.dot(g4.astype(BF16), up_ref[...], preferred_element_type=F32)
    lane = lax.broadcasted_iota(jnp.int32, (SUBLANES, LANES), 1)
    meta = jnp.zeros((SUBLANES, LANES), F32)
    off = jnp.zeros((1, 1), F32)
    pos = jnp.zeros((1, MOE_TILE), F32)
    for g in range(N_GROUPS):
        cnt = jnp.sum(onehot[g], axis=1, keepdims=True)
        pos = pos + onehot[g] * (before[g:g + 1] + off)
        meta = jnp.where(lane == g, off, meta)
        meta = jnp.where(lane == N_GROUPS + g, cnt, meta)
        off = off + jnp.ceil(cnt * (1.0 / MOE_ALIGN)) * MOE_ALIGN
    meta_o[0] = meta.astype(jnp.int32)
    row = lax.broadcasted_iota(jnp.int32, (MOE_SORTED, MOE_TILE), 0)
    place = jnp.where(row == pos.astype(jnp.int32), 1.0, 0.0)
    p16 = place.astype(BF16)
    hs_o[0, :MOE_SORTED, :] = jnp.dot(p16, h_ref[...], preferred_element_type=F32).astype(BF16)
    hs_o[0, MOE_SORTED:, :] = jnp.zeros((MOE_ROWS - MOE_SORTED, h_ref.shape[1]), BF16)
    gts = g_ref[...]
    hi = gts.astype(BF16)
    r1 = gts - hi.astype(F32)
    mid = r1.astype(BF16)
    low = (r1 - mid.astype(F32)).astype(BF16)
    gs_o[0, :MOE_SORTED, :] = (jnp.dot(p16, hi, preferred_element_type=F32)
                               + jnp.dot(p16, mid, preferred_element_type=F32)
                               + jnp.dot(p16, low, preferred_element_type=F32))
    gs_o[0, MOE_SORTED:, :] = jnp.zeros((MOE_ROWS - MOE_SORTED, LANES), F32)
    pt_o[...] = place.T.astype(BF16)


def moe_sort(hf, gates_t, gates):
    n, d = hf.shape
    ntile = n // MOE_TILE
    upper = jnp.asarray(np.triu(np.ones((MOE_TILE, MOE_TILE), np.float32), 1), BF16)
    return pl.pallas_call(
        _moe_sort_kernel,
        out_shape=[jax.ShapeDtypeStruct((ntile, MOE_ROWS, d), BF16),
                   jax.ShapeDtypeStruct((ntile, MOE_ROWS, LANES), F32),
                   jax.ShapeDtypeStruct((n, MOE_SORTED), BF16),
                   jax.ShapeDtypeStruct((ntile, SUBLANES, LANES), jnp.int32)],
        grid=(ntile,),
        in_specs=[pl.BlockSpec((ROUTE_ROWS, MOE_TILE), lambda i: (0, i)),
                  pl.BlockSpec((MOE_TILE, LANES), lambda i: (i, 0)),
                  pl.BlockSpec((MOE_TILE, d), lambda i: (i, 0)),
                  pl.BlockSpec((MOE_TILE, MOE_TILE), lambda i: (0, 0))],
        out_specs=[pl.BlockSpec((1, MOE_ROWS, d), lambda i: (i, 0, 0)),
                   pl.BlockSpec((1, MOE_ROWS, LANES), lambda i: (i, 0, 0)),
                   pl.BlockSpec((MOE_TILE, MOE_SORTED), lambda i: (i, 0)),
                   pl.BlockSpec((1, SUBLANES, LANES), lambda i: (i, 0, 0))],
        compiler_params=_cparams(("parallel",), VMEM_LIMIT),
        name="moe_sort",
    )(gates_t, gates, hf, upper)


def _moe_group_kernel(meta_ref, hs_ref, gs_ref, w1_ref, w3_ref, w2_ref, prev_ref, ys_o):
    g, i = pl.program_id(0), pl.program_id(1)

    @pl.when(g == 0)
    def _():
        ys_o[...] = jnp.zeros_like(ys_o)

    @pl.when(g > 0)
    def _():
        ys_o[...] = prev_ref[...]

    off = meta_ref[i * META_LANES + g]
    cnt = meta_ref[i * META_LANES + N_GROUPS + g]
    lane = lax.broadcasted_iota(jnp.int32, (MOE_CHUNK, LANES), 1)

    def chunk(j, carry):
        rows = pl.ds(pl.multiple_of(off + j * MOE_CHUNK, MOE_ALIGN), MOE_CHUNK)
        hs = hs_ref[0, rows, :]
        gs = gs_ref[0, rows, :]
        acc = ys_o[0, rows, :].astype(F32)
        for e in range(EXPERTS_PER_GROUP):
            a = jnp.dot(hs, w1_ref[e], preferred_element_type=F32)
            b = jnp.dot(hs, w3_ref[e], preferred_element_type=F32)
            act = (a * jax.nn.sigmoid(a)) * b
            gcol = jnp.sum(jnp.where(lane == g * EXPERTS_PER_GROUP + e, gs, 0.0), axis=1, keepdims=True)
            acc = acc + gcol * jnp.dot(act.astype(BF16), w2_ref[e], preferred_element_type=F32)
        ys_o[0, rows, :] = acc.astype(BF16)
        return carry

    lax.fori_loop(0, lax.div(cnt + (MOE_CHUNK - 1), MOE_CHUNK), chunk, 0)


def moe_group(hs, gs, meta, w1, w3, w2, carried=None):
    ntile, _, d = hs.shape
    de = w1.shape[2]
    epg = EXPERTS_PER_GROUP
    assert ntile > 1
    tile_spec = pl.BlockSpec((1, MOE_ROWS, d), lambda g, i, m: (i, 0, 0))
    grid_spec = pltpu.PrefetchScalarGridSpec(
        num_scalar_prefetch=1,
        grid=(N_GROUPS, ntile),
        in_specs=[tile_spec,
                  pl.BlockSpec((1, MOE_ROWS, LANES), lambda g, i, m: (i, 0, 0)),
                  pl.BlockSpec((epg, d, de), lambda g, i, m: (g, 0, 0)),
                  pl.BlockSpec((epg, d, de), lambda g, i, m: (g, 0, 0)),
                  pl.BlockSpec((epg, de, d), lambda g, i, m: (g, 0, 0)),
                  pl.BlockSpec((1, MOE_ROWS, d), lambda g, i, m: (jnp.where(g == 0, ntile - 1, i), 0, 0))],
        out_specs=tile_spec,
    )
    if carried is None:
        carried = jnp.zeros((ntile, MOE_ROWS, d), BF16)
    return pl.pallas_call(
        _moe_group_kernel,
        out_shape=jax.ShapeDtypeStruct((ntile, MOE_ROWS, d), BF16),
        grid_spec=grid_spec,
        input_output_aliases={6: 0},
        compiler_params=_cparams(("arbitrary", "arbitrary"), VMEM_LIMIT),
        name="moe_group",
    )(meta, hs, gs, w1, w3, w2, carried)


def _ln2_kernel(emit_next, x_ref, pt_ref, ys_ref, gf_ref, g_ref, b_ref, *rest):
    alpha = (2 * 2) ** 0.25
    f = jnp.dot(pt_ref[...], ys_ref[0], preferred_element_type=F32)
    x2 = _ln(alpha * x_ref[0] + gf_ref[0] * f) * g_ref[...] + b_ref[...]
    if emit_next:
        sh_ref, sc_ref, o_ref, h_o = rest
        h_o[0] = (_ln(x2) * (1.0 + sc_ref[0]) + sh_ref[0]).astype(BF16)
    else:
        o_ref, = rest
    o_ref[0] = x2


def ln2(x1, pt, ysb, mod3, nct, ln_g, ln_b, mod3_next):
    nb, ta, d = x1.shape
    nt = ta // TT
    per = MOE_TILE // TT
    wide = pl.BlockSpec((1, TT, d), lambda b, t: (b, t, 0))
    row = pl.BlockSpec((1, d), lambda b, t: (0, 0))
    in_specs = [wide, pl.BlockSpec((TT, MOE_SORTED), lambda b, t: (b * nt + t, 0)),
                pl.BlockSpec((1, MOE_SORTED, d), lambda b, t: ((b * nt + t) // per, 0, 0)),
                _mod_spec(5, nct, nb), row, row]
    args = [x1, pt, ysb, mod3, ln_g.reshape(1, -1), ln_b.reshape(1, -1)]
    if mod3_next is None:
        out_shape = jax.ShapeDtypeStruct((nb, ta - nct * TT, d), F32)
        out_specs = pl.BlockSpec((1, TT, d), lambda b, t: (b, jnp.maximum(t - nct, 0), 0))
    else:
        in_specs += [_mod_spec(0, nct, nb), _mod_spec(1, nct, nb)]
        args += [mod3_next, mod3_next]
        out_shape = [jax.ShapeDtypeStruct((nb, ta, d), F32), jax.ShapeDtypeStruct((nb, ta, d), BF16)]
        out_specs = [wide, wide]
    return pl.pallas_call(
        functools.partial(_ln2_kernel, mod3_next is not None),
        out_shape=out_shape,
        grid=(nb, nt),
        in_specs=in_specs,
        out_specs=out_specs,
        compiler_params=_cparams(("parallel", "arbitrary"), VMEM_LIMIT),
        name="ln2",
    )(*args)


def _rope_tables(l, lc):
    half = HEAD_DIM // 2
    inv = ROPE_THETA ** (-np.arange(0, half, 2, dtype=np.float64) / half)
    t = np.arange(l)
    rows, cols = t // GRID_W, t % GRID_W
    ang = np.concatenate([rows[:, None] * inv, cols[:, None] * inv], -1)
    ang = np.concatenate([np.zeros((lc, half)), ang], 0)
    cos = np.repeat(np.cos(ang), 2, axis=1)
    sin = np.repeat(np.sin(ang), 2, axis=1)
    sin[:, 0::2] *= -1.0
    reps = Q_W // HEAD_DIM
    return (jnp.asarray(np.tile(cos, (1, reps)), F32), jnp.asarray(np.tile(sin, (1, reps)), F32))


def _block_diag_ones():
    i = np.arange(BRANCH_W) // HEAD_DIM
    return jnp.asarray((i[:, None] == i[None, :]).astype(np.float32))


def kernel(x, c, ctx, c_ctx, ada_w, ada_b, w_in, rwkv_mu, rwkv_w0, rwkv_w_up, rwkv_a0, rwkv_a_up, rwkv_g_up, rwkv_k_k, rwkv_k_a, rwkv_r_k, rwkv_lnx_g, rwkv_lnx_b, hyena_conv, hyena_w1, hyena_b1, hyena_freq1, hyena_w2, hyena_b2, hyena_freq2, hyena_w3, hyena_skip, sconv_w, attn_q_norm, attn_k_norm, w_branch, w_out, ln1_g, ln1_b, ln2_g, ln2_b, router_w, router_bias, exp_w1, exp_w3, exp_w2):
    nb, l, d = x.shape
    lc = ctx.shape[1]
    depth = ada_w.shape[0]
    assert TT == HY_BLK and (l // HY_BLK) % (lc // HY_BLK) == 0
    assert d == D_MODEL and lc % TT == 0 and l % TT == 0 and l % GRID_W == 0 and (nb * (lc + l)) % MOE_TILE == 0
    ta = lc + l
    nct = lc // TT

    mod_rows = -(-(nb + 1) // SUBLANES) * SUBLANES
    cc = jnp.zeros((mod_rows, d), F32).at[:nb].set(c).at[nb].set(c_ctx)
    mod = ada_mod(cc, ada_w, ada_b)

    cos_t, sin_t = _rope_tables(l, lc)
    bd = _block_diag_ones()
    rwt = router_w.T

    ysb = None
    for li in range(depth):
        mod3 = mod[li].reshape(mod_rows, 1, N_MOD * d)
        wl = w_in[li].astype(BF16)
        if li == 0:
            x_all, h3 = lnmod(ctx, x, mod3, nct, 0, 1)
        h = h3.reshape(nb * ta, d)
        p_rwkv = matmul(h, wl[:, :OFF_HYENA], BF16).reshape(nb, ta, -1)
        p_hs = matmul(h, wl[:, OFF_HYENA:OFF_ATTN], BF16).reshape(nb, ta, -1)
        p_attn = matmul(h, wl[:, OFF_ATTN:OFF_GATE], BF16).reshape(nb, ta, -1)

        r, kk, w0, w1, k0, k1, b0, b1, v, g, bon = rwkv_prep(
            p_rwkv, nct, rwkv_mu[li], rwkv_w0[li], rwkv_w_up[li], rwkv_a0[li], rwkv_a_up[li], rwkv_g_up[li],
            rwkv_k_k[li], rwkv_k_a[li], rwkv_r_k[li], bd)
        yf, yb = rwkv_scan(r, kk, v, w0, w1, k0, k1, b0, b1, lc)

        q, kx, vx = attn_prep(p_attn, cos_t, sin_t, attn_q_norm[li], attn_k_norm[li], bd)
        yd = attention(q, kx, vx, lc)

        x0, u, ycv, ut = hs_pre(p_hs, nct, hyena_conv[li], sconv_w[li])
        fargs = (hyena_w1[li], hyena_b1[li], hyena_freq1[li], hyena_w2[li], hyena_b2[li], hyena_freq2[li],
                 hyena_w3[li])
        nlat = l // HY_BLK
        yconv_ctx = (hyena_conv_seg(ut, nb, nlat, lc // HY_BLK, fargs) if li < depth - 1
                     else jnp.zeros((nb, lc, BRANCH_W), F32))
        yconv = jnp.concatenate([yconv_ctx, hyena_conv_seg(ut, nb, 0, nlat, fargs)], axis=1)

        x1, hf, gates_t = merge(yf, yb, g, bon, x0, u, yconv, ycv, yd, h3, x_all, mod3, nct, hyena_skip[li],
                                wl[:, OFF_GATE:], w_branch[li].astype(BF16), w_out[li].astype(BF16), ln1_g[li],
                                ln1_b[li], rwt, router_bias, rwkv_lnx_g[li], rwkv_lnx_b[li], bd)
        gates = jnp.pad(gates_t.T, ((0, 0), (0, LANES - ROUTE_ROWS)))
        hs, gs, pt, meta = moe_sort(hf.reshape(nb * ta, d), gates_t, gates)
        ysb = moe_group(hs, gs, meta[:, 0, :META_LANES].reshape(-1), exp_w1[li].astype(BF16),
                        exp_w3[li].astype(BF16), exp_w2[li].astype(BF16), carried=ysb)
        if li == depth - 1:
            return ln2(x1, pt, ysb, mod3, nct, ln2_g[li], ln2_b[li], None)
        x_all, h3 = ln2(x1, pt, ysb, mod3, nct, ln2_g[li], ln2_b[li], mod[li + 1].reshape(mod_rows, 1, N_MOD * d))


def hyena_conv_seg(ut, nb, first_blk, nblk, fargs):
    ktab = hyena_filter_table(nblk * HY_BLK, *fargs)
    return hyena_conv(ut, ktab, nb, first_blk, nblk)
```
